```python
import math
import jax, jax.numpy as jnp
from jax import lax
import numpy as np

D_MODEL = 1024
BATCH = 8
SEQ = 4096
DEPTH = 4

CHUNK = 64
Q_BLOCK = 128
D_MIX = D_MODEL
HEAD_DIM = 64
D_FOX = D_MIX // 2
FOX_HEADS = D_FOX // HEAD_DIM
D_S5 = D_MIX // 4
S5_GROUP_CH = 16
S5_GROUPS = D_S5 // S5_GROUP_CH
S5_STATE = 64
D_RET = D_MIX // 4
RET_HEADS = D_RET // HEAD_DIM
ROPE_BASE = 10000.0
EPS = 1e-6
SPLIT_SIZES = (D_FOX, D_FOX, D_FOX, FOX_HEADS, D_S5, D_RET, D_RET, D_RET, D_MIX)
D_IN_PROJ = 3 * D_FOX + FOX_HEADS + D_S5 + 3 * D_RET + D_MIX

kernel_name = 'hybrid_fox_s5_retention_block'


def rms_norm(x, g):
    xf = x.astype(jnp.float32)
    y = xf * lax.rsqrt(jnp.mean(xf * xf, axis=-1, keepdims=True) + EPS)
    return (y * g.astype(jnp.float32)).astype(x.dtype)


def split_cols(proj):
    idx = [int(i) for i in np.cumsum(SPLIT_SIZES)[:-1]]
    return jnp.split(proj, idx, axis=-1)


def rotary(x):
    L, d = x.shape[1], x.shape[3]
    half = d // 2
    freqs = ROPE_BASE ** (-jnp.arange(half, dtype=jnp.float32) / half)
    ang = jnp.arange(L, dtype=jnp.float32)[:, None] * freqs[None, :]
    cos = jnp.cos(ang)[None, :, None, :]
    sin = jnp.sin(ang)[None, :, None, :]
    xf = x.astype(jnp.float32)
    x1, x2 = xf[..., :half], xf[..., half:]
    return jnp.concatenate([x1 * cos - x2 * sin, x1 * sin + x2 * cos], axis=-1)


def forgetting_attention(q, k, v, logf):
    B, L, H, dh = q.shape
    nb = L // Q_BLOCK
    scale = 1.0 / math.sqrt(dh)
    qh = q.transpose(0, 2, 1, 3)
    kh = k.transpose(0, 2, 1, 3)
    vh = v.transpose(0, 2, 1, 3)
    c = jnp.cumsum(logf, axis=1).transpose(0, 2, 1)
    qb = qh.reshape(B, H, nb, Q_BLOCK, dh).transpose(2, 0, 1, 3, 4)
    cb = c.reshape(B, H, nb, Q_BLOCK).transpose(2, 0, 1, 3)
    key_pos = jnp.arange(L)

    def block(args):
        qi, ci, bi = args
        s = jnp.einsum('bhqd,bhkd->bhqk', qi, kh).astype(jnp.float32) * scale
        s = s + ci[..., None] - c[:, :, None, :]
        q_pos = bi * Q_BLOCK + jnp.arange(Q_BLOCK)
        s = jnp.where(key_pos[None, :] <= q_pos[:, None], s, -jnp.inf)
        p = jax.nn.softmax(s, axis=-1)
        return jnp.einsum('bhqk,bhkd->bhqd', p.astype(vh.dtype), vh)

    o = lax.map(block, (qb, cb, jnp.arange(nb)))
    return o.transpose(1, 0, 3, 2, 4).reshape(B, L, H * dh)


def complex_affine_combine(e1, e2):
    a1r, a1i, b1r, b1i = e1
    a2r, a2i, b2r, b2i = e2
    ar = a2r * a1r - a2i * a1i
    ai = a2r * a1i + a2i * a1r
    br = a2r * b1r - a2i * b1i + b2r
    bi = a2r * b1i + a2i * b1r + b2i
    return (ar, ai, br, bi)


def s5_layer(u, a_re, a_im, b_re, b_im, c_re, c_im, d, log_dt, w_glu):
    B, L, _ = u.shape
    uf = u.astype(jnp.float32)
    ug = uf.reshape(B, L, S5_GROUPS, S5_GROUP_CH)
    dt = jnp.exp(log_dt.astype(jnp.float32))[:, None]
    ar = a_re.astype(jnp.float32)
    ai = a_im.astype(jnp.float32)
    mag = jnp.exp(ar * dt)
    lr = mag * jnp.cos(ai * dt)
    li = mag * jnp.sin(ai * dt)
    den = ar * ar + ai * ai
    fr = ((lr - 1.0) * ar + li * ai) / den
    fi = (li * ar - (lr - 1.0) * ai) / den
    br = b_re.astype(jnp.float32)
    bi = b_im.astype(jnp.float32)
    bbr = fr[..., None] * br - fi[..., None] * bi
    bbi = fr[..., None] * bi + fi[..., None] * br
    bu_r = jnp.einsum('gph,blgh->blgp', bbr, ug)
    bu_i = jnp.einsum('gph,blgh->blgp', bbi, ug)
    lr_t = jnp.broadcast_to(lr, bu_r.shape)
    li_t = jnp.broadcast_to(li, bu_r.shape)
    _, _, xr, xi = lax.associative_scan(complex_affine_combine, (lr_t, li_t, bu_r, bu_i), axis=1)
    y = (jnp.einsum('ghp,blgp->blgh', c_re.astype(jnp.float32), xr)
         - jnp.einsum('ghp,blgp->blgh', c_im.astype(jnp.float32), xi))
    y = y.reshape(B, L, D_S5) + d.astype(jnp.float32) * uf
    y = jax.nn.gelu(y)
    y = y * jax.nn.sigmoid(y @ w_glu.astype(jnp.float32))
    return y.astype(u.dtype)


def retention(q, k, v, gn_w):
    B, L, H, dk = q.shape
    dv = v.shape[-1]
    nc = L // CHUNK
    scale = 1.0 / math.sqrt(dk)
    log_gamma = jnp.log1p(-(2.0 ** (-5.0 - jnp.arange(H, dtype=jnp.float32))))

    def chunks(t):
        return t.astype(jnp.float32).reshape(B, nc, CHUNK, H, t.shape[-1]).transpose(1, 0, 3, 2, 4)

    qc, kc, vc = chunks(q), chunks(k), chunks(v)
    pos = jnp.arange(CHUNK, dtype=jnp.float32)
    dmat = jnp.exp(log_gamma[:, None, None] * jnp.abs(pos[:, None] - pos[None, :]))
    scores = jnp.einsum('cbhnd,cbhmd->cbhnm', qc, kc) * scale * dmat
    inner = jnp.einsum('cbhnm,cbhme->cbhne', scores, vc)
    wk = jnp.exp(log_gamma[:, None] * (CHUNK - 1.0 - pos)[None, :])
    wq = jnp.exp(log_gamma[:, None] * (pos + 1.0)[None, :])
    upd = jnp.einsum('cbhmd,cbhme->cbhde', kc * wk[None, None, :, :, None], vc)
    g_chunk = jnp.exp(log_gamma * CHUNK)[:, None, None]

    def step(state, u_i):
        return g_chunk * state + u_i, state

    _, prev_states = lax.scan(step, jnp.zeros((B, H, dk, dv), jnp.float32), upd)
    cross = jnp.einsum('cbhnd,cbhde->cbhne', qc * wq[None, None, :, :, None], prev_states) * scale
    o = (inner + cross).transpose(1, 0, 3, 2, 4).reshape(B, L, H, dv)
    mu = jnp.mean(o, axis=-1, keepdims=True)
    var = jnp.mean(jnp.square(o - mu), axis=-1, keepdims=True)
    o = (o - mu) * lax.rsqrt(var + EPS)
    return (o.reshape(B, L, H * dv) * gn_w.astype(jnp.float32)).astype(v.dtype)


def _fwd_setup_inputs(seed: int = 0) -> dict:
    key = jax.random.key(seed)
    ks = jax.random.split(key, 17)
    f32 = jnp.float32
    x = jax.random.normal(ks[0], (BATCH, SEQ, D_MODEL), f32)
    norm_w = 1.0 + 0.02 * jax.random.normal(ks[1], (DEPTH, D_MODEL), f32)
    w_in = jax.random.normal(ks[2], (DEPTH, D_MODEL, D_IN_PROJ), f32) * D_MODEL ** -0.5
    fox_b_f = 3.0 + 0.5 * jax.random.normal(ks[3], (DEPTH, FOX_HEADS), f32)
    s5_a_re = -0.5 + 0.01 * jax.random.normal(ks[4], (DEPTH, S5_GROUPS, S5_STATE), f32)
    s5_a_im = (math.pi * jnp.arange(S5_STATE, dtype=f32)[None, None, :]
               + 0.01 * jax.random.normal(ks[5], (DEPTH, S5_GROUPS, S5_STATE), f32))
    s5_b_re = jax.random.normal(ks[6], (DEPTH, S5_GROUPS, S5_STATE, S5_GROUP_CH), f32) * (2 * S5_GROUP_CH) ** -0.5
    s5_b_im = jax.random.normal(ks[7], (DEPTH, S5_GROUPS, S5_STATE, S5_GROUP_CH), f32) * (2 * S5_GROUP_CH) ** -0.5
    s5_c_re = jax.random.normal(ks[8], (DEPTH, S5_GROUPS, S5_GROUP_CH, S5_STATE), f32) * S5_STATE ** -0.5
    s5_c_im = jax.random.normal(ks[9], (DEPTH, S5_GROUPS, S5_GROUP_CH, S5_STATE), f32) * S5_STATE ** -0.5
    s5_d = jax.random.normal(ks[10], (DEPTH, D_S5), f32)
    s5_log_dt = jax.random.uniform(ks[11], (DEPTH, S5_GROUPS), f32, math.log(1e-3), math.log(1e-1))
    s5_w_glu = jax.random.normal(ks[12], (DEPTH, D_S5, D_S5), f32) * D_S5 ** -0.5
    ret_gn_w = 1.0 + 0.02 * jax.random.normal(ks[13], (DEPTH, D_RET), f32)
    w_out = jax.random.normal(ks[14], (DEPTH, D_MIX, D_MODEL), f32) * (0.5 * D_MIX ** -0.5)
    final_norm_w = 1.0 + 0.02 * jax.random.normal(ks[15], (D_MODEL,), f32)
    return {'x': x, 'norm_w': norm_w, 'w_in': w_in, 'fox_b_f': fox_b_f,
            's5_a_re': s5_a_re, 's5_a_im': s5_a_im, 's5_b_re': s5_b_re, 's5_b_im': s5_b_im,
            's5_c_re': s5_c_re, 's5_c_im': s5_c_im, 's5_d': s5_d, 's5_log_dt': s5_log_dt,
            's5_w_glu': s5_w_glu, 'ret_gn_w': ret_gn_w, 'w_out': w_out, 'final_norm_w': final_norm_w}


def _fwd_reference(x, norm_w, w_in, fox_b_f, s5_a_re, s5_a_im, s5_b_re, s5_b_im,
              s5_c_re, s5_c_im, s5_d, s5_log_dt, s5_w_glu, ret_gn_w, w_out, final_norm_w):
    B, L, _ = x.shape
    for l in range(DEPTH):
        h = rms_norm(x, norm_w[l])
        proj = h @ w_in[l]
        fq, fk, fv, flog, su, rq, rk, rv, gate = split_cols(proj)
        logf = jax.nn.log_sigmoid(flog.astype(jnp.float32) + fox_b_f[l].astype(jnp.float32))
        y_fox = forgetting_attention(fq.reshape(B, L, FOX_HEADS, HEAD_DIM),
                                     fk.reshape(B, L, FOX_HEADS, HEAD_DIM),
                                     fv.reshape(B, L, FOX_HEADS, HEAD_DIM), logf)
        y_s5 = s5_layer(su, s5_a_re[l], s5_a_im[l], s5_b_re[l], s5_b_im[l],
                        s5_c_re[l], s5_c_im[l], s5_d[l], s5_log_dt[l], s5_w_glu[l])
        y_ret = retention(rotary(rq.reshape(B, L, RET_HEADS, HEAD_DIM)),
                          rotary(rk.reshape(B, L, RET_HEADS, HEAD_DIM)),
                          rv.reshape(B, L, RET_HEADS, HEAD_DIM), ret_gn_w[l])
        y = jnp.concatenate([y_fox.astype(x.dtype), y_s5.astype(x.dtype), y_ret.astype(x.dtype)], axis=-1)
        y = y * jax.nn.silu(gate)
        x = x + y @ w_out[l]
    return rms_norm(x, final_norm_w)


import jax as _jax
import jax.numpy as _jnp

TWIN_FORMAT = 'train_step'
FWD_PARAMS = ['x', 'norm_w', 'w_in', 'fox_b_f', 's5_a_re', 's5_a_im', 's5_b_re', 's5_b_im', 's5_c_re', 's5_c_im', 's5_d', 's5_log_dt', 's5_w_glu', 'ret_gn_w', 'w_out', 'final_norm_w']
TWIN_WEIGHTS = ['norm_w', 'w_in', 'fox_b_f', 's5_a_re', 's5_a_im', 's5_b_re', 's5_b_im', 's5_c_re', 's5_c_im', 's5_d', 's5_log_dt', 's5_w_glu', 'ret_gn_w', 'w_out', 'final_norm_w']
TWIN_DIFF_INPUT = 'x'
TWIN_INPUTS = ['x', 'norm_w', 'w_in', 'fox_b_f', 's5_a_re', 's5_a_im', 's5_b_re', 's5_b_im', 's5_c_re', 's5_c_im', 's5_d', 's5_log_dt', 's5_w_glu', 'ret_gn_w', 'w_out', 'final_norm_w', 'loss_target', 'm_norm_w', 'm_w_in', 'm_fox_b_f', 'm_s5_a_re', 'm_s5_a_im', 'm_s5_b_re', 'm_s5_b_im', 'm_s5_c_re', 'm_s5_c_im', 'm_s5_d', 'm_s5_log_dt', 'm_s5_w_glu', 'm_ret_gn_w', 'm_w_out', 'm_final_norm_w', 'v_norm_w', 'v_w_in', 'v_fox_b_f', 'v_s5_a_re', 'v_s5_a_im', 'v_s5_b_re', 'v_s5_b_im', 'v_s5_c_re', 'v_s5_c_im', 'v_s5_d', 'v_s5_log_dt', 'v_s5_w_glu', 'v_ret_gn_w', 'v_w_out', 'v_final_norm_w']
TWIN_OUTPUTS = ['loss', 'grad_x', 'grad_norm_w', 'grad_w_in', 'grad_fox_b_f', 'grad_s5_a_re', 'grad_s5_a_im', 'grad_s5_b_re', 'grad_s5_b_im', 'grad_s5_c_re', 'grad_s5_c_im', 'grad_s5_d', 'grad_s5_log_dt', 'grad_s5_w_glu', 'grad_ret_gn_w', 'grad_w_out', 'grad_final_norm_w', 'delta_norm_w', 'delta_w_in', 'delta_fox_b_f', 'delta_s5_a_re', 'delta_s5_a_im', 'delta_s5_b_re', 'delta_s5_b_im', 'delta_s5_c_re', 'delta_s5_c_im', 'delta_s5_d', 'delta_s5_log_dt', 'delta_s5_w_glu', 'delta_ret_gn_w', 'delta_w_out', 'delta_final_norm_w', 'new_m_norm_w', 'new_m_w_in', 'new_m_fox_b_f', 'new_m_s5_a_re', 'new_m_s5_a_im', 'new_m_s5_b_re', 'new_m_s5_b_im', 'new_m_s5_c_re', 'new_m_s5_c_im', 'new_m_s5_d', 'new_m_s5_log_dt', 'new_m_s5_w_glu', 'new_m_ret_gn_w', 'new_m_w_out', 'new_m_final_norm_w', 'new_v_norm_w', 'new_v_w_in', 'new_v_fox_b_f', 'new_v_s5_a_re', 'new_v_s5_a_im', 'new_v_s5_b_re', 'new_v_s5_b_im', 'new_v_s5_c_re', 'new_v_s5_c_im', 'new_v_s5_d', 'new_v_s5_log_dt', 'new_v_s5_w_glu', 'new_v_ret_gn_w', 'new_v_w_out', 'new_v_final_norm_w']
TWIN_LEAF_KINDS = {'loss': 'loss', 'grad_x': 'grad_x', 'grad_norm_w': 'grad_w', 'grad_w_in': 'grad_w', 'grad_fox_b_f': 'grad_w', 'grad_s5_a_re': 'grad_w', 'grad_s5_a_im': 'grad_w', 'grad_s5_b_re': 'grad_w', 'grad_s5_b_im': 'grad_w', 'grad_s5_c_re': 'grad_w', 'grad_s5_c_im': 'grad_w', 'grad_s5_d': 'grad_w', 'grad_s5_log_dt': 'grad_w', 'grad_s5_w_glu': 'grad_w', 'grad_ret_gn_w': 'grad_w', 'grad_w_out': 'grad_w', 'grad_final_norm_w': 'grad_w', 'delta_norm_w': 'delta_w', 'delta_w_in': 'delta_w', 'delta_fox_b_f': 'delta_w', 'delta_s5_a_re': 'delta_w', 'delta_s5_a_im': 'delta_w', 'delta_s5_b_re': 'delta_w', 'delta_s5_b_im': 'delta_w', 'delta_s5_c_re': 'delta_w', 'delta_s5_c_im': 'delta_w', 'delta_s5_d': 'delta_w', 'delta_s5_log_dt': 'delta_w', 'delta_s5_w_glu': 'delta_w', 'delta_ret_gn_w': 'delta_w', 'delta_w_out': 'delta_w', 'delta_final_norm_w': 'delta_w', 'new_m_norm_w': 'new_m', 'new_m_w_in': 'new_m', 'new_m_fox_b_f': 'new_m', 'new_m_s5_a_re': 'new_m', 'new_m_s5_a_im': 'new_m', 'new_m_s5_b_re': 'new_m', 'new_m_s5_b_im': 'new_m', 'new_m_s5_c_re': 'new_m', 'new_m_s5_c_im': 'new_m', 'new_m_s5_d': 'new_m', 'new_m_s5_log_dt': 'new_m', 'new_m_s5_w_glu': 'new_m', 'new_m_ret_gn_w': 'new_m', 'new_m_w_out': 'new_m', 'new_m_final_norm_w': 'new_m', 'new_v_norm_w': 'new_v', 'new_v_w_in': 'new_v', 'new_v_fox_b_f': 'new_v', 'new_v_s5_a_re': 'new_v', 'new_v_s5_a_im': 'new_v', 'new_v_s5_b_re': 'new_v', 'new_v_s5_b_im': 'new_v', 'new_v_s5_c_re': 'new_v', 'new_v_s5_c_im': 'new_v', 'new_v_s5_d': 'new_v', 'new_v_s5_log_dt': 'new_v', 'new_v_s5_w_glu': 'new_v', 'new_v_ret_gn_w': 'new_v', 'new_v_w_out': 'new_v', 'new_v_final_norm_w': 'new_v'}


def _forward(args):
    return _fwd_reference(*[args[k] for k in FWD_PARAMS])


def _output_shape():
    def fwd():
        inp = _fwd_setup_inputs(0)
        return _fwd_reference(*[inp[k] for k in FWD_PARAMS])
    out = _jax.eval_shape(fwd)
    return out.shape, out.dtype

N_MICROBATCH = 1
ADAM_LR = 0.001
ADAM_B1 = 0.9
ADAM_B2 = 0.999
ADAM_EPS = 1e-08
ADAM_WD = 0.01
ADAM_STEP = 10
PER_EXAMPLE_BATCH_AXIS = {'x': 0, 'loss_target': 0}
SHARED_INPUTS = []
_WEIGHT_DTYPES = {'norm_w': _jnp.float32, 'w_in': _jnp.float32, 'fox_b_f': _jnp.float32, 's5_a_re': _jnp.float32, 's5_a_im': _jnp.float32, 's5_b_re': _jnp.float32, 's5_b_im': _jnp.float32, 's5_c_re': _jnp.float32, 's5_c_im': _jnp.float32, 's5_d': _jnp.float32, 's5_log_dt': _jnp.float32, 's5_w_glu': _jnp.float32, 'ret_gn_w': _jnp.float32, 'w_out': _jnp.float32, 'final_norm_w': _jnp.float32}
MOMENT_SCALE = {'norm_w': 6.381345e-02, 'w_in': 3.271118e-02, 'fox_b_f': 7.811550e-02, 's5_a_re': 1.355179e-03, 's5_a_im': 1.414649e-03, 's5_b_re': 9.142834e-04, 's5_b_im': 8.842769e-04, 's5_c_re': 1.277668e-03, 's5_c_im': 1.254171e-03, 's5_d': 2.124712e-02, 's5_log_dt': 1.115438e+00, 's5_w_glu': 5.613135e-03, 'ret_gn_w': 5.605896e-02, 'w_out': 6.237377e-02, 'final_norm_w': 3.199199e+01}


def _to_microbatches(a, axis):
    t = _jnp.moveaxis(a, axis, 0)
    t = t.reshape((N_MICROBATCH, t.shape[0] // N_MICROBATCH) + t.shape[1:])
    return _jnp.moveaxis(t, 1, axis + 1)


def setup_inputs(seed: int = 0) -> dict:
    inp = _fwd_setup_inputs(seed)
    key = _jax.random.fold_in(_jax.random.key(seed), 7919)
    shape, _ = _output_shape()
    out = dict(inp)
    out["loss_target"] = _jax.random.normal(_jax.random.fold_in(key, 0), shape, _jnp.float32)
    for i, name in enumerate(TWIN_WEIGHTS):
        w = inp[name].astype(_jnp.float32)
        if MOMENT_SCALE is None:
            s = _jnp.sqrt(_jnp.mean(_jnp.square(w)) + 1e-30)
        else:
            s = MOMENT_SCALE[name]
        km, kv = _jax.random.split(_jax.random.fold_in(key, i + 1))
        out[name] = w
        out["m_" + name] = s * _jax.random.normal(km, w.shape, _jnp.float32)
        out["v_" + name] = (s * s) * _jax.random.uniform(kv, w.shape, _jnp.float32, 0.5, 1.5)
    if N_MICROBATCH > 1:
        for name, axis in PER_EXAMPLE_BATCH_AXIS.items():
            out[name] = _to_microbatches(out[name], axis)
    return {'x': out['x'], 'norm_w': out['norm_w'], 'w_in': out['w_in'], 'fox_b_f': out['fox_b_f'], 's5_a_re': out['s5_a_re'], 's5_a_im': out['s5_a_im'], 's5_b_re': out['s5_b_re'], 's5_b_im': out['s5_b_im'], 's5_c_re': out['s5_c_re'], 's5_c_im': out['s5_c_im'], 's5_d': out['s5_d'], 's5_log_dt': out['s5_log_dt'], 's5_w_glu': out['s5_w_glu'], 'ret_gn_w': out['ret_gn_w'], 'w_out': out['w_out'], 'final_norm_w': out['final_norm_w'], 'loss_target': out['loss_target'], 'm_norm_w': out['m_norm_w'], 'm_w_in': out['m_w_in'], 'm_fox_b_f': out['m_fox_b_f'], 'm_s5_a_re': out['m_s5_a_re'], 'm_s5_a_im': out['m_s5_a_im'], 'm_s5_b_re': out['m_s5_b_re'], 'm_s5_b_im': out['m_s5_b_im'], 'm_s5_c_re': out['m_s5_c_re'], 'm_s5_c_im': out['m_s5_c_im'], 'm_s5_d': out['m_s5_d'], 'm_s5_log_dt': out['m_s5_log_dt'], 'm_s5_w_glu': out['m_s5_w_glu'], 'm_ret_gn_w': out['m_ret_gn_w'], 'm_w_out': out['m_w_out'], 'm_final_norm_w': out['m_final_norm_w'], 'v_norm_w': out['v_norm_w'], 'v_w_in': out['v_w_in'], 'v_fox_b_f': out['v_fox_b_f'], 'v_s5_a_re': out['v_s5_a_re'], 'v_s5_a_im': out['v_s5_a_im'], 'v_s5_b_re': out['v_s5_b_re'], 'v_s5_b_im': out['v_s5_b_im'], 'v_s5_c_re': out['v_s5_c_re'], 'v_s5_c_im': out['v_s5_c_im'], 'v_s5_d': out['v_s5_d'], 'v_s5_log_dt': out['v_s5_log_dt'], 'v_s5_w_glu': out['v_s5_w_glu'], 'v_ret_gn_w': out['v_ret_gn_w'], 'v_w_out': out['v_w_out'], 'v_final_norm_w': out['v_final_norm_w']}


def _loss(weights, diff, rest, loss_target):
    with _jax.named_scope("forward"):
        args = {**rest, TWIN_DIFF_INPUT: diff, **{k: w.astype(_WEIGHT_DTYPES[k]) for k, w in weights.items()}}
        y = _forward(args)
    with _jax.named_scope("loss_head"):
        err = _jnp.square(y.astype(_jnp.float32) - loss_target)
        return 0.5 * _jnp.sum(_jnp.mean(err, axis=-1)) if err.ndim else 0.5 * err


def _adamw(w, g, m, v):
    m = ADAM_B1 * m + (1.0 - ADAM_B1) * g
    v = ADAM_B2 * v + (1.0 - ADAM_B2) * _jnp.square(g)
    m_hat = m / (1.0 - ADAM_B1 ** ADAM_STEP)
    v_hat = v / (1.0 - ADAM_B2 ** ADAM_STEP)
    delta = -ADAM_LR * (m_hat / (_jnp.sqrt(v_hat) + ADAM_EPS) + ADAM_WD * w)
    return delta, m, v


def reference(x, norm_w, w_in, fox_b_f, s5_a_re, s5_a_im, s5_b_re, s5_b_im, s5_c_re, s5_c_im, s5_d, s5_log_dt, s5_w_glu, ret_gn_w, w_out, final_norm_w, loss_target, m_norm_w, m_w_in, m_fox_b_f, m_s5_a_re, m_s5_a_im, m_s5_b_re, m_s5_b_im, m_s5_c_re, m_s5_c_im, m_s5_d, m_s5_log_dt, m_s5_w_glu, m_ret_gn_w, m_w_out, m_final_norm_w, v_norm_w, v_w_in, v_fox_b_f, v_s5_a_re, v_s5_a_im, v_s5_b_re, v_s5_b_im, v_s5_c_re, v_s5_c_im, v_s5_d, v_s5_log_dt, v_s5_w_glu, v_ret_gn_w, v_w_out, v_final_norm_w):
    given = dict(x=x, norm_w=norm_w, w_in=w_in, fox_b_f=fox_b_f, s5_a_re=s5_a_re, s5_a_im=s5_a_im, s5_b_re=s5_b_re, s5_b_im=s5_b_im, s5_c_re=s5_c_re, s5_c_im=s5_c_im, s5_d=s5_d, s5_log_dt=s5_log_dt, s5_w_glu=s5_w_glu, ret_gn_w=ret_gn_w, w_out=w_out, final_norm_w=final_norm_w, loss_target=loss_target, m_norm_w=m_norm_w, m_w_in=m_w_in, m_fox_b_f=m_fox_b_f, m_s5_a_re=m_s5_a_re, m_s5_a_im=m_s5_a_im, m_s5_b_re=m_s5_b_re, m_s5_b_im=m_s5_b_im, m_s5_c_re=m_s5_c_re, m_s5_c_im=m_s5_c_im, m_s5_d=m_s5_d, m_s5_log_dt=m_s5_log_dt, m_s5_w_glu=m_s5_w_glu, m_ret_gn_w=m_ret_gn_w, m_w_out=m_w_out, m_final_norm_w=m_final_norm_w, v_norm_w=v_norm_w, v_w_in=v_w_in, v_fox_b_f=v_fox_b_f, v_s5_a_re=v_s5_a_re, v_s5_a_im=v_s5_a_im, v_s5_b_re=v_s5_b_re, v_s5_b_im=v_s5_b_im, v_s5_c_re=v_s5_c_re, v_s5_c_im=v_s5_c_im, v_s5_d=v_s5_d, v_s5_log_dt=v_s5_log_dt, v_s5_w_glu=v_s5_w_glu, v_ret_gn_w=v_ret_gn_w, v_w_out=v_w_out, v_final_norm_w=v_final_norm_w)
    weights = {n: given[n] for n in TWIN_WEIGHTS}
    shared = {n: given[n] for n in SHARED_INPUTS}
    per_example = {n: given[n] for n in ['x']}
    grad_fn = _jax.value_and_grad(_loss, argnums=(0, 1))

    def one_microbatch(ex, loss_target):
        ex = dict(ex)
        diff = ex.pop(TWIN_DIFF_INPUT)
        return grad_fn(weights, diff, {**shared, **ex}, loss_target)

    if N_MICROBATCH == 1:
        loss, (grad_w, grad_x) = one_microbatch(per_example, given["loss_target"])
    else:
        def body(carry, xs):
            loss_sum, grad_sum = carry
            l_k, (gw_k, gx_k) = one_microbatch(xs[0], xs[1])
            with _jax.named_scope("update"):
                return (loss_sum + l_k, _jax.tree.map(_jnp.add, grad_sum, gw_k)), gx_k

        init = (_jnp.zeros((), _jnp.float32), _jax.tree.map(_jnp.zeros_like, weights))
        (loss, grad_w), grad_x = _jax.lax.scan(body, init, (per_example, given["loss_target"]))
    with _jax.named_scope("update"):
        delta_w, new_m, new_v = {}, {}, {}
        for n in TWIN_WEIGHTS:
            delta_w[n], new_m[n], new_v[n] = _adamw(weights[n], grad_w[n], given["m_" + n], given["v_" + n])
    return (loss, grad_x, *[grad_w[n] for n in TWIN_WEIGHTS], *[delta_w[n] for n in TWIN_WEIGHTS],
            *[new_m[n] for n in TWIN_WEIGHTS], *[new_v[n] for n in TWIN_WEIGHTS])
```

```python
import math

import jax
import jax.numpy as jnp
from jax import lax
from jax.experimental import pallas as pl
from jax.experimental.pallas import tpu as pltpu

F32 = jnp.float32
_MXU = jnp.bfloat16
_HI = lax.Precision.HIGHEST

N_DEV = 8
DEPTH = 4
D_MODEL = 1024
HEAD_DIM = 64
D_FOX = 512
FOX_HEADS = 8
D_S5 = 256
S5_GROUPS = 16
S5_GROUP_CH = 16
S5_STATE = 64
S5_CH = S5_GROUPS * S5_STATE
D_RET = 256
RET_HEADS = 4
CHUNK = 64
ROPE_BASE = 10000.0
EPS = 1e-6
D_IN = 3592
D_INP = 3712
W_SHARD = D_IN // N_DEV
O_GATE, O_FQ, O_FK, O_FV, O_SU, O_RQ, O_RK, O_RV, O_FL = 0, 1024, 1536, 2048, 2560, 2816, 3072, 3328, 3584

ADAM_LR, ADAM_B1, ADAM_B2, ADAM_EPS, ADAM_WD, ADAM_STEP = 0.001, 0.9, 0.999, 1e-08, 0.01, 10

TM = 256
TQ = 512
TS = 256
NEG = -1e30
VMEM_BIG = 56 * 1024 * 1024


def _pallas(body, **kw):
    return pl.pallas_call(body, **kw)


def _whole(shape):
    n = len(shape)
    return pl.BlockSpec(shape, lambda *_: (0,) * n)


def _rows(tm, width, col=0):
    return pl.BlockSpec((tm, width), lambda i: (i, col))


def _dot(a, b, dims=(((1,), (0,)), ((), ()))):
    return lax.dot_general(a.astype(_MXU), b.astype(_MXU), dims, preferred_element_type=F32)


_NT = (((1,), (1,)), ((), ()))
_TN = (((0,), (0,)), ((), ()))


def _mm(a, b, *, ta=False, tb=False, add=None, tm=512, tn=512, tk=512, name):
    m, k = (a.shape[1], a.shape[0]) if ta else a.shape
    n = b.shape[0] if tb else b.shape[1]
    tm, tn, tk = min(tm, m), min(tn, n), min(tk, k)
    assert m % tm == 0 and n % tn == 0 and k % tk == 0, (name, m, n, k)
    dims = (((0 if ta else 1,), (1 if tb else 0,)), ((), ()))

    def body(*refs):
        a_ref, b_ref = refs[0], refs[1]
        o_ref = refs[-1]
        p = _dot(a_ref[...], b_ref[...], dims)
        kk = pl.program_id(2)

        @pl.when(kk == 0)
        def _():
            o_ref[...] = p if add is None else p + refs[2][...]

        @pl.when(kk != 0)
        def _():
            o_ref[...] += p

    a_spec = pl.BlockSpec((tk, tm), lambda i, j, kk: (kk, i)) if ta else pl.BlockSpec((tm, tk), lambda i, j, kk: (i, kk))
    b_spec = pl.BlockSpec((tn, tk), lambda i, j, kk: (j, kk)) if tb else pl.BlockSpec((tk, tn), lambda i, j, kk: (kk, j))
    o_spec = pl.BlockSpec((tm, tn), lambda i, j, kk: (i, j))
    ins, specs = [a, b], [a_spec, b_spec]
    if add is not None:
        ins.append(add)
        specs.append(o_spec)
    return _pallas(body, out_shape=jax.ShapeDtypeStruct((m, n), F32), grid=(m // tm, n // tn, k // tk),
                   in_specs=specs, out_specs=o_spec, name=name,
                   compiler_params=pltpu.CompilerParams(vmem_limit_bytes=VMEM_BIG))(*ins)


def _norm_inproj(x, g, w):
    L = x.shape[0]

    def body(x_ref, g_ref, w_ref, p_ref, h_ref):
        xv = x_ref[...]
        r = lax.rsqrt(jnp.mean(xv * xv, axis=-1, keepdims=True) + EPS)
        h = (xv * r * g_ref[...]).astype(_MXU)
        h_ref[...] = h
        p_ref[...] = _dot(h, w_ref[...])

    return _pallas(body, out_shape=(jax.ShapeDtypeStruct((L, D_INP), F32), jax.ShapeDtypeStruct((L, D_MODEL), _MXU)),
                   grid=(L // TM,), in_specs=[_rows(TM, D_MODEL), _whole((1, D_MODEL)), _whole((D_MODEL, D_INP))],
                   out_specs=(_rows(TM, D_INP), _rows(TM, D_MODEL)), name="norm_inproj",
                   compiler_params=pltpu.CompilerParams(vmem_limit_bytes=VMEM_BIG))(x, g, w)


def _rms_bwd(xv, g, dh):
    r = lax.rsqrt(jnp.mean(xv * xv, axis=-1, keepdims=True) + EPS)
    xh = xv * r
    dg = jnp.sum(dh * xh, axis=0, keepdims=True)
    dxh = dh * g
    dx = r * (dxh - xh * jnp.mean(dxh * xh, axis=-1, keepdims=True))
    return dx, dg


def _inproj_bwd_dx(dproj, w, x, g, dres):
    L = x.shape[0]

    def body(dp_ref, w_ref, x_ref, g_ref, dr_ref, dx_ref, dg_ref):
        dh = _dot(dp_ref[...], w_ref[...], _NT)
        dx, dg = _rms_bwd(x_ref[...], g_ref[...], dh)
        dx_ref[...] = dx + dr_ref[...]

        @pl.when(pl.program_id(0) == 0)
        def _():
            dg_ref[...] = dg

        @pl.when(pl.program_id(0) != 0)
        def _():
            dg_ref[...] += dg

    return _pallas(body, out_shape=(jax.ShapeDtypeStruct((L, D_MODEL), F32), jax.ShapeDtypeStruct((1, D_MODEL), F32)),
                   grid=(L // TM,),
                   in_specs=[_rows(TM, D_INP), _whole((D_MODEL, D_INP)), _rows(TM, D_MODEL), _whole((1, D_MODEL)),
                             _rows(TM, D_MODEL)],
                   out_specs=(_rows(TM, D_MODEL), _whole((1, D_MODEL))), name="inproj_bwd_dx",
                   compiler_params=pltpu.CompilerParams(vmem_limit_bytes=VMEM_BIG))(dproj, w, x, g, dres)


def _tri(upper):
    r = lax.broadcasted_iota(jnp.int32, (128, 128), 0)
    c = lax.broadcasted_iota(jnp.int32, (128, 128), 1)
    return jnp.where(r <= c if upper else r >= c, 1.0, 0.0).astype(F32)


def _fox_cumsum(fl_t, b):
    H, L = fl_t.shape

    def body(fl_ref, b_ref, c_ref):
        u = _tri(True)

        def blk(n, carry):
            off = pl.multiple_of(n * 128, 128)
            lf = jax.nn.log_sigmoid(fl_ref[:, pl.ds(off, 128)] + b_ref[...])
            cs = jnp.dot(lf, u, precision=_HI, preferred_element_type=F32) + carry
            c_ref[:, pl.ds(off, 128)] = cs
            return cs[:, 127:128]

        lax.fori_loop(0, L // 128, blk, jnp.zeros((H, 1), F32))

    return _pallas(body, out_shape=jax.ShapeDtypeStruct((H, L), F32), name="fox_cumsum")(fl_t, b)


def _fox_cumsum_bwd(fl_t, b, dc, dc2):
    H, L = fl_t.shape
    nb = L // 128

    def body(fl_ref, b_ref, dc_ref, dc2_ref, dfl_ref, db_ref):
        lo = _tri(False)

        def blk(n, carry):
            tail, dbs = carry
            off = pl.multiple_of((nb - 1 - n) * 128, 128)
            dcv = dc_ref[:, pl.ds(off, 128)] + dc2_ref[:, pl.ds(off, 128)]
            rs = jnp.dot(dcv, lo, precision=_HI, preferred_element_type=F32) + tail
            z = fl_ref[:, pl.ds(off, 128)] + b_ref[...]
            dfl = rs * jax.nn.sigmoid(-z)
            dfl_ref[:, pl.ds(off, 128)] = dfl
            return rs[:, 0:1], dbs + jnp.sum(dfl, axis=1, keepdims=True)

        _, dbs = lax.fori_loop(0, nb, blk, (jnp.zeros((H, 1), F32), jnp.zeros((H, 1), F32)))
        db_ref[...] = dbs

    return _pallas(body, out_shape=(jax.ShapeDtypeStruct((H, L), F32), jax.ShapeDtypeStruct((H, 1), F32)),
                   name="fox_cumsum_bwd")(fl_t, b, dc, dc2)


def _causal(i, j, tq):
    row = i * tq + lax.broadcasted_iota(jnp.int32, (tq, tq), 0)
    col = j * tq + lax.broadcasted_iota(jnp.int32, (tq, tq), 1)
    return col <= row


def _fox_fwd(q, k, v, c_col, c_row):
    H, L, dh = q.shape
    tq = min(TQ, L)
    nq = L // tq
    scale = 1.0 / math.sqrt(dh)

    def body(q_ref, k_ref, v_ref, cc_ref, cr_ref, o_ref, lse_ref, m_sc, l_sc, acc_sc):
        i, j = pl.program_id(1), pl.program_id(2)

        @pl.when(j == 0)
        def _():
            m_sc[...] = jnp.full((tq, 1), NEG, F32)
            l_sc[...] = jnp.zeros((tq, 1), F32)
            acc_sc[...] = jnp.zeros((tq, dh), F32)

        @pl.when(j <= i)
        def _():
            s = _dot(q_ref[...], k_ref[...], _NT) * scale + cc_ref[...] - cr_ref[...]
            s = jnp.where(_causal(i, j, tq), s, NEG)
            m_prev = m_sc[...]
            m_new = jnp.maximum(m_prev, jnp.max(s, axis=-1, keepdims=True))
            alpha = jnp.exp(m_prev - m_new)
            p = jnp.exp(s - m_new)
            l_sc[...] = alpha * l_sc[...] + jnp.sum(p, axis=-1, keepdims=True)
            acc_sc[...] = alpha * acc_sc[...] + _dot(p, v_ref[...])
            m_sc[...] = m_new

        @pl.when(j == i)
        def _():
            o_ref[...] = acc_sc[...] / l_sc[...]
            lse_ref[...] = m_sc[...] + jnp.log(l_sc[...])

    qs = pl.BlockSpec((None, tq, dh), lambda h, i, j: (h, i, 0))
    ks = pl.BlockSpec((None, tq, dh), lambda h, i, j: (h, jnp.minimum(i, j), 0))
    return _pallas(
        body, out_shape=(jax.ShapeDtypeStruct((H, L, dh), F32), jax.ShapeDtypeStruct((H, L, 1), F32)),
        grid=(H, nq, nq),
        in_specs=[qs, ks, ks, pl.BlockSpec((None, tq, 1), lambda h, i, j: (h, i, 0)),
                  pl.BlockSpec((None, 1, tq), lambda h, i, j: (h, 0, jnp.minimum(i, j)))],
        out_specs=(qs, pl.BlockSpec((None, tq, 1), lambda h, i, j: (h, i, 0))),
        scratch_shapes=[pltpu.VMEM((tq, 1), F32), pltpu.VMEM((tq, 1), F32), pltpu.VMEM((tq, dh), F32)],
        name="fox_fwd")(q, k, v, c_col, c_row)


def _fox_bwd(q, k, v, c_col, c_row, o, lse, do):
    H, L, dh = q.shape
    tq = min(TQ, L)
    nq = L // tq
    scale = 1.0 / math.sqrt(dh)

    def body(q_ref, k_ref, v_ref, cc_ref, cr_ref, o_ref, lse_ref, do_ref, dq_ref, dk_ref, dv_ref, dc_ref, dcq_ref,
             dk_sc, dv_sc, dc_sc):
        j, i = pl.program_id(1), pl.program_id(2)

        @pl.when(i == 0)
        def _():
            dk_sc[...] = jnp.zeros((tq, dh), F32)
            dv_sc[...] = jnp.zeros((tq, dh), F32)
            dc_sc[...] = jnp.zeros((1, tq), F32)

        @pl.when(i >= j)
        def _():
            qv, kv, dov = q_ref[...], k_ref[...], do_ref[...]
            s = _dot(qv, kv, _NT) * scale + cc_ref[...] - cr_ref[...]
            p = jnp.where(_causal(i, j, tq), jnp.exp(s - lse_ref[...]), 0.0)
            delta = jnp.sum(dov * o_ref[...], axis=-1, keepdims=True)
            dv_sc[...] += _dot(p, dov, _TN)
            dp = _dot(dov, v_ref[...], _NT)
            ds = p * (dp - delta)
            dc_sc[...] -= jnp.sum(ds, axis=0, keepdims=True)
            dk_sc[...] += _dot(ds, qv, _TN) * scale
            dqb = _dot(ds, kv) * scale
            dcq = jnp.sum(ds, axis=1, keepdims=True)
            rows = pl.ds(pl.multiple_of(i * tq, tq), tq)

            @pl.when(j == 0)
            def _():
                dq_ref[rows, :] = dqb
                dcq_ref[rows, :] = dcq

            @pl.when(j != 0)
            def _():
                dq_ref[rows, :] += dqb
                dcq_ref[rows, :] += dcq

        @pl.when(i == nq - 1)
        def _():
            dk_ref[...] = dk_sc[...]
            dv_ref[...] = dv_sc[...]
            dc_ref[...] = dc_sc[...]

    qs = pl.BlockSpec((None, tq, dh), lambda h, j, i: (h, jnp.maximum(i, j), 0))
    q1 = pl.BlockSpec((None, tq, 1), lambda h, j, i: (h, jnp.maximum(i, j), 0))
    ks = pl.BlockSpec((None, tq, dh), lambda h, j, i: (h, j, 0))
    cr = pl.BlockSpec((None, 1, tq), lambda h, j, i: (h, 0, j))
    return _pallas(
        body, out_shape=(jax.ShapeDtypeStruct((H, L, dh), F32),) * 3
        + (jax.ShapeDtypeStruct((H, 1, L), F32), jax.ShapeDtypeStruct((H, L, 1), F32)),
        grid=(H, nq, nq), in_specs=[qs, ks, ks, q1, cr, qs, q1, qs],
        out_specs=(pl.BlockSpec((None, L, dh), lambda h, j, i: (h, 0, 0)), ks, ks, cr,
                   pl.BlockSpec((None, L, 1), lambda h, j, i: (h, 0, 0))),
        scratch_shapes=[pltpu.VMEM((tq, dh), F32), pltpu.VMEM((tq, dh), F32), pltpu.VMEM((1, tq), F32)],
        name="fox_bwd")(q, k, v, c_col, c_row, o, lse, do)


def _s5_expand():
    r = lax.broadcasted_iota(jnp.int32, (S5_STATE, S5_STATE * S5_GROUP_CH), 0)
    c = lax.broadcasted_iota(jnp.int32, (S5_STATE, S5_STATE * S5_GROUP_CH), 1)
    return jnp.where(c // S5_GROUP_CH == r, 1.0, 0.0).astype(F32)


def _s5_disc_math(ar, ai, ldt, br, bi):
    dt = jnp.exp(ldt)
    mag = jnp.exp(ar * dt)
    lr = mag * jnp.cos(ai * dt)
    li = mag * jnp.sin(ai * dt)
    den = ar * ar + ai * ai
    fr = ((lr - 1.0) * ar + li * ai) / den
    fi = (li * ar - (lr - 1.0) * ai) / den
    e = _s5_expand()
    fre = jnp.dot(fr, e, precision=_HI, preferred_element_type=F32)
    fie = jnp.dot(fi, e, precision=_HI, preferred_element_type=F32)
    return lr, li, fre * br - fie * bi, fre * bi + fie * br


def _s5_disc(ar, ai, ldt, br, bi):
    def body(ar_ref, ai_ref, ldt_ref, br_ref, bi_ref, lr_ref, li_ref, bbr_ref, bbi_ref):
        lr, li, bbr, bbi = _s5_disc_math(ar_ref[...], ai_ref[...], ldt_ref[...], br_ref[...], bi_ref[...])
        lr_ref[...] = lr
        li_ref[...] = li
        bbr_ref[...] = bbr
        bbi_ref[...] = bbi

    gp = jax.ShapeDtypeStruct(ar.shape, F32)
    gb = jax.ShapeDtypeStruct(br.shape, F32)
    return _pallas(body, out_shape=(gp, gp, gb, gb), name="s5_disc")(ar, ai, ldt, br, bi)


def _s5_disc_bwd(ar, ai, ldt, br, bi, dlr, dli, dbbr, dbbi):
    def body(ar_ref, ai_ref, ldt_ref, br_ref, bi_ref, dlr_ref, dli_ref, dbbr_ref, dbbi_ref,
             dar_ref, dai_ref, dldt_ref, dbr_ref, dbi_ref):
        _, vjp = jax.vjp(_s5_disc_math, ar_ref[...], ai_ref[...], ldt_ref[...], br_ref[...], bi_ref[...])
        dar, dai, dldt, dbr, dbi = vjp((dlr_ref[...], dli_ref[...], dbbr_ref[...], dbbi_ref[...]))
        dar_ref[...] = dar
        dai_ref[...] = dai
        dldt_ref[...] = dldt
        dbr_ref[...] = dbr
        dbi_ref[...] = dbi

    gp = jax.ShapeDtypeStruct(ar.shape, F32)
    gb = jax.ShapeDtypeStruct(br.shape, F32)
    return _pallas(body, out_shape=(gp, gp, jax.ShapeDtypeStruct(ldt.shape, F32), gb, gb),
                   name="s5_disc_bwd")(ar, ai, ldt, br, bi, dlr, dli, dbbr, dbbi)


def _s5_scan(bu, lam):
    L = bu.shape[0]
    ts = min(TS, L)

    def body(b_ref, lam_ref, x_ref, c_sc):
        @pl.when(pl.program_id(0) == 0)
        def _():
            c_sc[...] = jnp.zeros((16, 128), F32)

        lr, li = lam_ref[0:8, :], lam_ref[8:16, :]

        def step(t, carry):
            xr, xi = carry
            nr = lr * xr - li * xi + b_ref[t, 0:8, :]
            ni = lr * xi + li * xr + b_ref[t, 8:16, :]
            x_ref[t, 0:8, :] = nr
            x_ref[t, 8:16, :] = ni
            return nr, ni

        xr, xi = lax.fori_loop(0, ts, step, (c_sc[0:8, :], c_sc[8:16, :]), unroll=8)
        c_sc[0:8, :] = xr
        c_sc[8:16, :] = xi

    blk = pl.BlockSpec((ts, 16, 128), lambda n: (n, 0, 0))
    return _pallas(body, out_shape=jax.ShapeDtypeStruct((L, 16, 128), F32), grid=(L // ts,),
                   in_specs=[blk, _whole((16, 128))], out_specs=blk,
                   scratch_shapes=[pltpu.VMEM((16, 128), F32)], name="s5_scan")(bu, lam)


def _s5_scan_bwd(dx, xprev, lam):
    L = dx.shape[0]
    ts = min(TS, L)
    nb = L // ts

    def body(dx_ref, xp_ref, lam_ref, g_ref, dlam_ref, c_sc):
        @pl.when(pl.program_id(0) == 0)
        def _():
            c_sc[...] = jnp.zeros((16, 128), F32)
            dlam_ref[...] = jnp.zeros((16, 128), F32)

        lr, li = lam_ref[0:8, :], lam_ref[8:16, :]

        def step(n, carry):
            gr, gi, ar, ai = carry
            t = ts - 1 - n
            nr = dx_ref[t, 0:8, :] + lr * gr + li * gi
            ni = dx_ref[t, 8:16, :] - li * gr + lr * gi
            g_ref[t, 0:8, :] = nr
            g_ref[t, 8:16, :] = ni
            pr, pi = xp_ref[t, 0:8, :], xp_ref[t, 8:16, :]
            return nr, ni, ar + nr * pr + ni * pi, ai - nr * pi + ni * pr

        z = jnp.zeros((8, 128), F32)
        gr, gi, ar, ai = lax.fori_loop(0, ts, step, (c_sc[0:8, :], c_sc[8:16, :], z, z), unroll=8)
        c_sc[0:8, :] = gr
        c_sc[8:16, :] = gi
        dlam_ref[0:8, :] += ar
        dlam_ref[8:16, :] += ai

    blk = pl.BlockSpec((ts, 16, 128), lambda n: (nb - 1 - n, 0, 0))
    return _pallas(body, out_shape=(jax.ShapeDtypeStruct((L, 16, 128), F32), jax.ShapeDtypeStruct((16, 128), F32)),
                   grid=(nb,), in_specs=[blk, blk, _whole((16, 128))], out_specs=(blk, _whole((16, 128))),
                   scratch_shapes=[pltpu.VMEM((16, 128), F32)], name="s5_scan_bwd")(dx, xprev, lam)


def _s5_post(ypre, proj, d, w):
    L = ypre.shape[0]

    def body(y_ref, u_ref, d_ref, w_ref, o_ref):
        y1 = jax.nn.gelu(y_ref[...] + d_ref[...] * u_ref[...])
        o_ref[...] = y1 * jax.nn.sigmoid(_dot(y1, w_ref[...]))

    return _pallas(body, out_shape=jax.ShapeDtypeStruct((L, D_S5), F32), grid=(L // TM,),
                   in_specs=[_rows(TM, D_S5), _rows(TM, D_S5, O_SU // D_S5), _whole((1, D_S5)), _whole((D_S5, D_S5))],
                   out_specs=_rows(TM, D_S5), name="s5_post")(ypre, proj, d, w)


def _s5_post_bwd(ypre, proj, d, w, dout):
    L = ypre.shape[0]

    def body(y_ref, u_ref, d_ref, w_ref, do_ref, dy_ref, du_ref, dd_ref, dw_ref):
        u, dv, dout_v = u_ref[...], d_ref[...], do_ref[...]
        y1, gelu_vjp = jax.vjp(jax.nn.gelu, y_ref[...] + dv * u)
        sg = jax.nn.sigmoid(_dot(y1, w_ref[...]))
        dz = dout_v * y1 * sg * (1.0 - sg)
        dy1 = dout_v * sg + _dot(dz, w_ref[...], _NT)
        dy, = gelu_vjp(dy1)
        dy_ref[...] = dy
        du_ref[...] = dy * dv
        dd = jnp.sum(dy * u, axis=0, keepdims=True)
        dw = _dot(y1, dz, _TN)

        @pl.when(pl.program_id(0) == 0)
        def _():
            dd_ref[...] = dd
            dw_ref[...] = dw

        @pl.when(pl.program_id(0) != 0)
        def _():
            dd_ref[...] += dd
            dw_ref[...] += dw

    row = _rows(TM, D_S5)
    return _pallas(body, out_shape=(jax.ShapeDtypeStruct((L, D_S5), F32),) * 2
                   + (jax.ShapeDtypeStruct((1, D_S5), F32), jax.ShapeDtypeStruct((D_S5, D_S5), F32)),
                   grid=(L // TM,),
                   in_specs=[row, _rows(TM, D_S5, O_SU // D_S5), _whole((1, D_S5)), _whole((D_S5, D_S5)), row],
                   out_specs=(row, row, _whole((1, D_S5)), _whole((D_S5, D_S5))), name="s5_post_bwd")(ypre, proj, d, w, dout)


def _swap_halves(z):
    lane = lax.broadcasted_iota(jnp.int32, z.shape, 1)
    return jnp.where(lane % HEAD_DIM < HEAD_DIM // 2, pltpu.roll(z, D_RET - HEAD_DIM // 2, 1), pltpu.roll(z, HEAD_DIM // 2, 1))


def _rotary(x, col, cos_t, sin_t):
    L = x.shape[0]

    def body(x_ref, c_ref, s_ref, o_ref):
        xv = x_ref[...]
        o_ref[...] = xv * c_ref[...] + _swap_halves(xv * s_ref[...])

    row = _rows(TM, D_RET)
    return _pallas(body, out_shape=jax.ShapeDtypeStruct((L, D_RET), F32), grid=(L // TM,),
                   in_specs=[_rows(TM, D_RET, col), row, row], out_specs=row, name="rotary")(x, cos_t, sin_t)


def _ret_weights(i, j, tq, lg, scale):
    row = i * tq + lax.broadcasted_iota(jnp.int32, (tq, tq), 0)
    col = j * tq + lax.broadcasted_iota(jnp.int32, (tq, tq), 1)
    dist = jnp.abs(row - col).astype(F32)
    return jnp.where(col // CHUNK <= row // CHUNK, scale * jnp.exp(lg * dist), 0.0)


def _ret_fwd(q, k, v, gn, lg):
    H, L, dh = q.shape
    tq = min(TQ, L)
    nq = L // tq
    scale = 1.0 / math.sqrt(dh)

    def body(q_ref, k_ref, v_ref, gn_ref, lg_ref, o_ref, y_ref, acc_sc):
        i, j = pl.program_id(1), pl.program_id(2)

        @pl.when(j == 0)
        def _():
            acc_sc[...] = jnp.zeros((tq, dh), F32)

        @pl.when(j <= i)
        def _():
            a = _dot(q_ref[...], k_ref[...], _NT) * _ret_weights(i, j, tq, lg_ref[...], scale)
            acc_sc[...] += _dot(a, v_ref[...])

        @pl.when(j == i)
        def _():
            o = acc_sc[...]
            o_ref[...] = o
            mu = jnp.mean(o, axis=-1, keepdims=True)
            var = jnp.mean(jnp.square(o - mu), axis=-1, keepdims=True)
            y_ref[...] = (o - mu) * lax.rsqrt(var + EPS) * gn_ref[...]

    qs = pl.BlockSpec((None, tq, dh), lambda h, i, j: (h, i, 0))
    ks = pl.BlockSpec((None, tq, dh), lambda h, i, j: (h, jnp.minimum(i, j), 0))
    return _pallas(
        body, out_shape=(jax.ShapeDtypeStruct((H, L, dh), F32),) * 2, grid=(H, nq, nq),
        in_specs=[qs, ks, ks, pl.BlockSpec((None, 1, dh), lambda h, i, j: (h, 0, 0)),
                  pl.BlockSpec((None, 1, 1), lambda h, i, j: (h, 0, 0))],
        out_specs=(qs, qs), scratch_shapes=[pltpu.VMEM((tq, dh), F32)], name="ret_fwd")(q, k, v, gn, lg)


def _ret_norm_bwd(o, dy, gn):
    H, L, dh = o.shape
    tm = min(TQ, L)

    def body(o_ref, dy_ref, gn_ref, do_ref, dgn_ref):
        ov, dyv = o_ref[...], dy_ref[...]
        mu = jnp.mean(ov, axis=-1, keepdims=True)
        oc = ov - mu
        r = lax.rsqrt(jnp.mean(oc * oc, axis=-1, keepdims=True) + EPS)
        oh = oc * r
        dgn = jnp.sum(dyv * oh, axis=0, keepdims=True)
        doh = dyv * gn_ref[...]
        do_ref[...] = r * (doh - jnp.mean(doh, axis=-1, keepdims=True) - oh * jnp.mean(doh * oh, axis=-1, keepdims=True))

        @pl.when(pl.program_id(1) == 0)
        def _():
            dgn_ref[...] = dgn

        @pl.when(pl.program_id(1) != 0)
        def _():
            dgn_ref[...] += dgn

    blk = pl.BlockSpec((None, tm, dh), lambda h, i: (h, i, 0))
    one = pl.BlockSpec((None, 1, dh), lambda h, i: (h, 0, 0))
    return _pallas(body, out_shape=(jax.ShapeDtypeStruct((H, L, dh), F32), jax.ShapeDtypeStruct((H, 1, dh), F32)),
                   grid=(H, L // tm), in_specs=[blk, blk, one], out_specs=(blk, one), name="ret_norm_bwd")(o, dy, gn)


def _ret_bwd(q, k, v, do, lg):
    H, L, dh = q.shape
    tq = min(TQ, L)
    nq = L // tq
    scale = 1.0 / math.sqrt(dh)

    def body(q_ref, k_ref, v_ref, do_ref, lg_ref, dq_ref, dk_ref, dv_ref, dk_sc, dv_sc):
        j, i = pl.program_id(1), pl.program_id(2)

        @pl.when(i == 0)
        def _():
            dk_sc[...] = jnp.zeros((tq, dh), F32)
            dv_sc[...] = jnp.zeros((tq, dh), F32)

        @pl.when(i >= j)
        def _():
            qv, kv, dov = q_ref[...], k_ref[...], do_ref[...]
            w = _ret_weights(i, j, tq, lg_ref[...], scale)
            a = _dot(qv, kv, _NT) * w
            dv_sc[...] += _dot(a, dov, _TN)
            dqk = _dot(dov, v_ref[...], _NT) * w
            dk_sc[...] += _dot(dqk, qv, _TN)
            dqb = _dot(dqk, kv)
            rows = pl.ds(pl.multiple_of(i * tq, tq), tq)

            @pl.when(j == 0)
            def _():
                dq_ref[rows, :] = dqb

            @pl.when(j != 0)
            def _():
                dq_ref[rows, :] += dqb

        @pl.when(i == nq - 1)
        def _():
            dk_ref[...] = dk_sc[...]
            dv_ref[...] = dv_sc[...]

    qs = pl.BlockSpec((None, tq, dh), lambda h, j, i: (h, jnp.maximum(i, j), 0))
    ks = pl.BlockSpec((None, tq, dh), lambda h, j, i: (h, j, 0))
    return _pallas(
        body, out_shape=(jax.ShapeDtypeStruct((H, L, dh), F32),) * 3, grid=(H, nq, nq),
        in_specs=[qs, ks, ks, qs, pl.BlockSpec((None, 1, 1), lambda h, j, i: (h, 0, 0))],
        out_specs=(pl.BlockSpec((None, L, dh), lambda h, j, i: (h, 0, 0)), ks, ks),
        scratch_shapes=[pltpu.VMEM((tq, dh), F32), pltpu.VMEM((tq, dh), F32)], name="ret_bwd")(q, k, v, do, lg)


def _gate_out(yf, ys, yr, proj, x, w):
    L = x.shape[0]

    def body(yf_ref, ys_ref, yr_ref, g_ref, x_ref, w_ref, y_ref, xn_ref):
        cat = jnp.concatenate([yf_ref[...], ys_ref[...], yr_ref[...]], axis=-1)
        y = cat * jax.nn.silu(g_ref[...])
        y_ref[...] = y
        xn_ref[...] = x_ref[...] + _dot(y, w_ref[...])

    full = _rows(TM, D_MODEL)
    return _pallas(body, out_shape=(jax.ShapeDtypeStruct((L, D_MODEL), F32),) * 2, grid=(L // TM,),
                   in_specs=[_rows(TM, D_FOX), _rows(TM, D_S5), _rows(TM, D_RET), _rows(TM, D_MODEL, O_GATE // D_MODEL),
                             full, _whole((D_MODEL, D_MODEL))],
                   out_specs=(full, full), name="gate_out")(yf, ys, yr, proj, x, w)


def _gate_out_bwd(dxn, w, yf, ys, yr, proj):
    L = dxn.shape[0]

    def body(dx_ref, w_ref, yf_ref, ys_ref, yr_ref, g_ref, dyf_ref, dys_ref, dyr_ref, dg_ref):
        dy = _dot(dx_ref[...], w_ref[...], _NT)
        g = g_ref[...]
        sg = jax.nn.sigmoid(g)
        dcat = dy * (g * sg)
        dyf_ref[...] = dcat[:, :D_FOX]
        dys_ref[...] = dcat[:, D_FOX:D_FOX + D_S5]
        dyr_ref[...] = dcat[:, D_FOX + D_S5:]
        cat = jnp.concatenate([yf_ref[...], ys_ref[...], yr_ref[...]], axis=-1)
        dg_ref[...] = dy * cat * (sg * (1.0 + g * (1.0 - sg)))

    full = _rows(TM, D_MODEL)
    f, s, r = _rows(TM, D_FOX), _rows(TM, D_S5), _rows(TM, D_RET)
    return _pallas(body, out_shape=(jax.ShapeDtypeStruct((L, D_FOX), F32), jax.ShapeDtypeStruct((L, D_S5), F32),
                                    jax.ShapeDtypeStruct((L, D_RET), F32), jax.ShapeDtypeStruct((L, D_MODEL), F32)),
                   grid=(L // TM,),
                   in_specs=[full, _whole((D_MODEL, D_MODEL)), f, s, r, _rows(TM, D_MODEL, O_GATE // D_MODEL)],
                   out_specs=(f, s, r, full), name="gate_out_bwd")(dxn, w, yf, ys, yr, proj)


def _final_loss(x, g, tgt):
    L = x.shape[0]

    def body(x_ref, g_ref, t_ref, loss_ref, dx_ref, dg_ref):
        xv, gv = x_ref[...], g_ref[...]
        r = lax.rsqrt(jnp.mean(xv * xv, axis=-1, keepdims=True) + EPS)
        err = xv * r * gv - t_ref[...]
        part = 0.5 * jnp.sum(jnp.mean(err * err, axis=-1, keepdims=True), axis=0, keepdims=True)
        dx, dg = _rms_bwd(xv, gv, err * (1.0 / D_MODEL))
        dx_ref[...] = dx

        @pl.when(pl.program_id(0) == 0)
        def _():
            loss_ref[...] = part
            dg_ref[...] = dg

        @pl.when(pl.program_id(0) != 0)
        def _():
            loss_ref[...] += part
            dg_ref[...] += dg

    full = _rows(TM, D_MODEL)
    return _pallas(body, out_shape=(jax.ShapeDtypeStruct((1, 1), F32), jax.ShapeDtypeStruct((L, D_MODEL), F32),
                                    jax.ShapeDtypeStruct((1, D_MODEL), F32)),
                   grid=(L // TM,), in_specs=[full, _whole((1, D_MODEL)), full],
                   out_specs=(_whole((1, 1)), full, _whole((1, D_MODEL))), name="final_loss")(x, g, tgt)


def _heads(a, h):
    return a.reshape(a.shape[0], h, HEAD_DIM).transpose(1, 0, 2)


def _unheads(a):
    return a.transpose(1, 0, 2).reshape(a.shape[1], a.shape[0] * a.shape[2])


def _block_diag(blocks):
    g, r, c = blocks.shape
    eye = jnp.eye(g, dtype=blocks.dtype)
    return (blocks[:, :, None, :] * eye[:, None, :, None]).reshape(g * r, g * c)


def _diag_blocks(m, g):
    r, c = m.shape[0] // g, m.shape[1] // g
    eye = jnp.eye(g, dtype=m.dtype)
    return jnp.sum(m.reshape(g, r, g, c) * eye[:, None, :, None], axis=2)


def _rope_tables(L):
    half = HEAD_DIM // 2
    freqs = ROPE_BASE ** (-jnp.arange(half, dtype=F32) / half)
    ang = jnp.arange(L, dtype=F32)[:, None] * freqs[None, :]
    cos, sin = jnp.cos(ang), jnp.sin(ang)
    cos_t = jnp.tile(jnp.concatenate([cos, cos], axis=-1), (1, RET_HEADS))
    sin_t = jnp.tile(jnp.concatenate([sin, -sin], axis=-1), (1, RET_HEADS))
    return cos_t, sin_t


def _s5_mats(p):
    lr, li, bbr, bbi = _s5_disc(p["a_re"], p["a_im"], p["ldt"], p["b_re"], p["b_im"])
    lam = jnp.concatenate([lr.reshape(8, 128), li.reshape(8, 128)], axis=0)
    g, s, ch = S5_GROUPS, S5_STATE, S5_GROUP_CH
    wb = jnp.concatenate([_block_diag(b.reshape(g, s, ch).transpose(0, 2, 1)) for b in (bbr, bbi)], axis=1)
    wc = jnp.concatenate([_block_diag(c.transpose(0, 2, 1)) for c in (p["c_re"], -p["c_im"])], axis=0)
    return lam, wb, wc


def _layer_fwd(x, p, rope):
    L = x.shape[0]
    cos_t, sin_t = rope
    s = {"x": x}
    proj, h = _norm_inproj(x, p["norm_w"], p["w_in"])
    s["proj"], s["h"] = proj, h
    fl_t = proj[:, O_FL:O_FL + FOX_HEADS].T
    c_t = _fox_cumsum(fl_t, p["b_f"])
    q, k, v = (_heads(proj[:, o:o + D_FOX], FOX_HEADS) for o in (O_FQ, O_FK, O_FV))
    o_f, lse = _fox_fwd(q, k, v, c_t[:, :, None], c_t[:, None, :])
    yf = _unheads(o_f)
    s.update(fl_t=fl_t, c_t=c_t, fq=q, fk=k, fv=v, o_f=o_f, lse=lse, yf=yf)
    lam, wb, wc = _s5_mats(p)
    u = proj[:, O_SU:O_SU + D_S5]
    bu = _mm(u, wb, name="s5_bu")
    xs = _s5_scan(bu.reshape(L, 16, 128), lam)
    ypre = _mm(xs.reshape(L, 2 * S5_CH), wc, tk=1024, name="s5_cx")
    ys = _s5_post(ypre, proj, p["d"], p["w_glu"])
    s.update(lam=lam, wb=wb, wc=wc, u=u, xs=xs, ypre=ypre, ys=ys)
    rq = _heads(_rotary(proj, O_RQ // D_RET, cos_t, sin_t), RET_HEADS)
    rk = _heads(_rotary(proj, O_RK // D_RET, cos_t, sin_t), RET_HEADS)
    rv = _heads(proj[:, O_RV:O_RV + D_RET], RET_HEADS)
    o_pre, y_r = _ret_fwd(rq, rk, rv, p["gn_w"], p["lg"])
    yr = _unheads(y_r)
    s.update(rq=rq, rk=rk, rv=rv, o_pre=o_pre, yr=yr)
    y, xn = _gate_out(yf, ys, yr, proj, x, p["w_out"])
    s["y"] = y
    return xn, s


def _layer_bwd(dxn, s, p, rope):
    L = dxn.shape[0]
    cos_t, sin_t = rope
    g = {}
    proj = s["proj"]
    dyf, dys, dyr, dgate = _gate_out_bwd(dxn, p["w_out"], s["yf"], s["ys"], s["yr"], proj)
    g["w_out"] = _mm(s["y"], dxn, ta=True, name="dw_out")
    do_pre, dgn = _ret_norm_bwd(s["o_pre"], _heads(dyr, RET_HEADS), p["gn_w"])
    g["ret_gn_w"] = dgn.reshape(D_RET)
    dqh, dkh, dvh = _ret_bwd(s["rq"], s["rk"], s["rv"], do_pre, p["lg"])
    drq = _rotary(_unheads(dqh), 0, cos_t, -sin_t)
    drk = _rotary(_unheads(dkh), 0, cos_t, -sin_t)
    drv = _unheads(dvh)
    dypre, du1, dd, dwglu = _s5_post_bwd(s["ypre"], proj, p["d"], p["w_glu"], dys)
    g["s5_d"], g["s5_w_glu"] = dd.reshape(D_S5), dwglu
    xs2 = s["xs"].reshape(L, 2 * S5_CH)
    dxs = _mm(dypre, s["wc"], tb=True, name="s5_dx")
    dwc = _mm(xs2, dypre, ta=True, tm=1024, name="s5_dwc")
    dc = [_diag_blocks(m, S5_GROUPS).transpose(0, 2, 1) for m in (dwc[:S5_CH], dwc[S5_CH:])]
    g["s5_c_re"], g["s5_c_im"] = dc[0], -dc[1]
    xprev = jnp.concatenate([jnp.zeros((1, 16, 128), F32), s["xs"][:-1]], axis=0)
    gs, dlam = _s5_scan_bwd(dxs.reshape(L, 16, 128), xprev, s["lam"])
    gs2 = gs.reshape(L, 2 * S5_CH)
    dsu = _mm(gs2, s["wb"], tb=True, add=du1, tk=1024, name="s5_du")
    dwb = _mm(s["u"], gs2, ta=True, name="s5_dwb")
    dbb = [_diag_blocks(m, S5_GROUPS).transpose(0, 2, 1).reshape(S5_GROUPS, S5_STATE * S5_GROUP_CH)
           for m in (dwb[:, :S5_CH], dwb[:, S5_CH:])]
    dar, dai, dldt, dbr, dbi = _s5_disc_bwd(p["a_re"], p["a_im"], p["ldt"], p["b_re"], p["b_im"],
                                            dlam[:8].reshape(S5_GROUPS, S5_STATE), dlam[8:].reshape(S5_GROUPS, S5_STATE),
                                            dbb[0], dbb[1])
    shp = (S5_GROUPS, S5_STATE, S5_GROUP_CH)
    g.update(s5_a_re=dar, s5_a_im=dai, s5_log_dt=dldt.reshape(S5_GROUPS), s5_b_re=dbr.reshape(shp), s5_b_im=dbi.reshape(shp))
    c_t = s["c_t"]
    dq, dk, dv, dcs, dcq = _fox_bwd(s["fq"], s["fk"], s["fv"], c_t[:, :, None], c_t[:, None, :], s["o_f"], s["lse"],
                               _heads(dyf, FOX_HEADS))
    dfl_t, dbf = _fox_cumsum_bwd(s["fl_t"], p["b_f"], dcs.reshape(FOX_HEADS, L), dcq.reshape(FOX_HEADS, L))
    g["fox_b_f"] = dbf.reshape(FOX_HEADS)
    dfl = jnp.pad(dfl_t.T, ((0, 0), (0, D_INP - O_FL - FOX_HEADS)))
    dproj = jnp.concatenate([dgate, _unheads(dq), _unheads(dk), _unheads(dv), dsu, drq, drk, drv, dfl], axis=-1)
    dx, dnw = _inproj_bwd_dx(dproj, p["w_in"], s["x"], p["norm_w"], dxn)
    g["norm_w"] = dnw.reshape(D_MODEL)
    g["w_in"] = _mm(s["h"], dproj, ta=True, tm=512, tn=D_INP, tk=256, name="dw_in")
    return dx, g


def _pad_w_in(w):
    z = jnp.zeros(w.shape[:-1] + (D_INP - O_FL - FOX_HEADS,), w.dtype)
    return jnp.concatenate([w[..., 2568:3592], w[..., 0:1536], w[..., 1544:2568], w[..., 1536:1544], z], axis=-1)


def _unpad_w_in(w):
    return jnp.concatenate([w[..., O_FQ:O_SU], w[..., O_FL:O_FL + FOX_HEADS], w[..., O_SU:O_FL], w[..., O_GATE:O_FQ]], axis=-1)


def _layer_params(l, w_in_p, w_out, w_glu, small):
    g, s, ch = S5_GROUPS, S5_STATE, S5_GROUP_CH
    return dict(
        norm_w=small["norm_w"][l][None], w_in=w_in_p[l], b_f=small["fox_b_f"][l][:, None],
        a_re=small["s5_a_re"][l], a_im=small["s5_a_im"][l], ldt=small["s5_log_dt"][l][:, None],
        b_re=small["s5_b_re"][l].reshape(g, s * ch), b_im=small["s5_b_im"][l].reshape(g, s * ch),
        c_re=small["s5_c_re"][l], c_im=small["s5_c_im"][l], d=small["s5_d"][l][None], w_glu=w_glu[l],
        gn_w=small["ret_gn_w"][l].reshape(RET_HEADS, 1, HEAD_DIM), w_out=w_out[l],
        lg=jnp.log1p(-(2.0 ** (-5.0 - jnp.arange(RET_HEADS, dtype=F32)))).reshape(RET_HEADS, 1, 1))


def _step_grads(x, tgt, w_in_p, w_out, w_glu, small):
    L = x.shape[0]
    rope = _rope_tables(L)
    params = [_layer_params(l, w_in_p, w_out, w_glu, small) for l in range(DEPTH)]
    saved = []
    for l in range(DEPTH):
        x, s = _layer_fwd(x, params[l], rope)
        saved.append(s)
    loss, dx, dfw = _final_loss(x, small["final_norm_w"][None], tgt)
    grads = [None] * DEPTH
    for l in reversed(range(DEPTH)):
        dx, grads[l] = _layer_bwd(dx, saved[l], params[l], rope)
    return loss, dx, grads, dfw.reshape(D_MODEL)


_MESH = pl.DeviceIdType.MESH
_ANY = pl.BlockSpec(memory_space=pl.ANY)


def _me_and_peers():
    x, y, c = lax.axis_index("x"), lax.axis_index("y"), lax.axis_index("c")
    flip = lambda a, bit: (1 - a) if bit else a
    peers = []
    for r in range(1, N_DEV):
        px, py, pc = flip(x, (r >> 2) & 1), flip(y, (r >> 1) & 1), flip(c, r & 1)
        peers.append(((px, py, pc), 4 * px + 2 * py + pc))
    return 4 * x + 2 * y + c, peers


def _all_gather(shards):
    n = len(shards)

    def body(*refs):
        srcs, dsts = refs[:n], refs[n:2 * n]
        send_sems, recv_sems, local_sems = refs[2 * n:]
        me, peers = _me_and_peers()
        own = [pltpu.make_async_copy(srcs[t], dsts[t].at[me], local_sems.at[t]) for t in range(n)]
        for cp in own:
            cp.start()
        for r, (dev, _) in enumerate(peers):
            for t in range(n):
                pltpu.make_async_remote_copy(srcs[t], dsts[t].at[me], send_sems.at[t, r], recv_sems.at[t, r],
                                             device_id=dev, device_id_type=_MESH).start()
        for r, (dev, idx) in enumerate(peers):
            for t in range(n):
                pltpu.make_async_remote_copy(srcs[t], dsts[t].at[idx], send_sems.at[t, r], recv_sems.at[t, r],
                                             device_id=dev, device_id_type=_MESH).wait()
        for cp in own:
            cp.wait()

    return _pallas(body, out_shape=tuple(jax.ShapeDtypeStruct((N_DEV,) + s.shape, s.dtype) for s in shards),
                   in_specs=[_ANY] * n, out_specs=tuple([_ANY] * n),
                   scratch_shapes=[pltpu.SemaphoreType.DMA((n, N_DEV - 1)), pltpu.SemaphoreType.DMA((n, N_DEV - 1)),
                                   pltpu.SemaphoreType.DMA((n,))],
                   name="all_gather_weights")(*shards)


def _exchange_grads(parts, small):
    n = len(parts)

    def body(*refs):
        srcs, dsts = refs[:n + 1], refs[n + 1:2 * n + 2]
        send_sems, recv_sems, local_sems = refs[2 * n + 2:]
        me, peers = _me_and_peers()

        def src(t, to):
            return srcs[t].at[to] if t < n else srcs[t]

        own = [pltpu.make_async_copy(src(t, me), dsts[t].at[me], local_sems.at[t]) for t in range(n + 1)]
        for cp in own:
            cp.start()
        for r, (dev, idx) in enumerate(peers):
            for t in range(n + 1):
                pltpu.make_async_remote_copy(src(t, idx), dsts[t].at[me], send_sems.at[t, r], recv_sems.at[t, r],
                                             device_id=dev, device_id_type=_MESH).start()
        for r, (dev, idx) in enumerate(peers):
            for t in range(n + 1):
                pltpu.make_async_remote_copy(src(t, idx), dsts[t].at[idx], send_sems.at[t, r], recv_sems.at[t, r],
                                             device_id=dev, device_id_type=_MESH).wait()
        for cp in own:
            cp.wait()

    outs = tuple(jax.ShapeDtypeStruct(p.shape, p.dtype) for p in parts) + (jax.ShapeDtypeStruct((N_DEV,) + small.shape, small.dtype),)
    return _pallas(body, out_shape=outs, in_specs=[_ANY] * (n + 1), out_specs=tuple([_ANY] * (n + 1)),
                   scratch_shapes=[pltpu.SemaphoreType.DMA((n + 1, N_DEV - 1)), pltpu.SemaphoreType.DMA((n + 1, N_DEV - 1)),
                                   pltpu.SemaphoreType.DMA((n + 1,))],
                   name="exchange_grads")(*parts, small)


def _adamw(parts, w, m, v, name):
    n, rows, cols = parts.shape
    tm = next(t for t in (256, 128, 64, 32, 16, 8) if rows % t == 0)

    def body(p_ref, w_ref, m_ref, v_ref, g_ref, d_ref, nm_ref, nv_ref):
        g = p_ref[0]
        for i in range(1, n):
            g = g + p_ref[i]
        nm = ADAM_B1 * m_ref[...] + (1.0 - ADAM_B1) * g
        nv = ADAM_B2 * v_ref[...] + (1.0 - ADAM_B2) * jnp.square(g)
        m_hat = nm / (1.0 - ADAM_B1 ** ADAM_STEP)
        v_hat = nv / (1.0 - ADAM_B2 ** ADAM_STEP)
        g_ref[...] = g
        d_ref[...] = -ADAM_LR * (m_hat / (jnp.sqrt(v_hat) + ADAM_EPS) + ADAM_WD * w_ref[...])
        nm_ref[...] = nm
        nv_ref[...] = nv

    row = pl.BlockSpec((tm, cols), lambda i: (i, 0))
    return _pallas(body, out_shape=(jax.ShapeDtypeStruct((rows, cols), F32),) * 4, grid=(rows // tm,),
                   in_specs=[pl.BlockSpec((n, tm, cols), lambda i: (0, i, 0)), row, row, row], out_specs=(row,) * 4,
                   name=name)(parts, w, m, v)


_SHARDED = ("w_in", "s5_w_glu", "w_out")
_WEIGHTS = ("norm_w", "w_in", "fox_b_f", "s5_a_re", "s5_a_im", "s5_b_re", "s5_b_im", "s5_c_re", "s5_c_im", "s5_d",
            "s5_log_dt", "s5_w_glu", "ret_gn_w", "w_out", "final_norm_w")
_SMALL = tuple(n for n in _WEIGHTS if n not in _SHARDED)
_LANES = 128


def _pack(arrs):
    flat = jnp.concatenate([a.reshape(-1) for a in arrs])
    rows = -(-flat.shape[0] // (_LANES * _LANES)) * _LANES
    return jnp.pad(flat, (0, rows * _LANES - flat.shape[0])).reshape(rows, _LANES)


def _unpack(packed, like):
    flat, out, off = packed.reshape(-1), [], 0
    for a in like:
        out.append(flat[off:off + a.size].reshape(a.shape))
        off += a.size
    return out


def _to_slots(g, axis):
    shp = g.shape[:axis] + (N_DEV, g.shape[axis] // N_DEV) + g.shape[axis + 1:]
    return jnp.moveaxis(g.reshape(shp), axis, 0)


def kernel(x, norm_w, w_in, fox_b_f, s5_a_re, s5_a_im, s5_b_re, s5_b_im, s5_c_re, s5_c_im, s5_d, s5_log_dt, s5_w_glu, ret_gn_w, w_out, final_norm_w, loss_target, m_norm_w, m_w_in, m_fox_b_f, m_s5_a_re, m_s5_a_im, m_s5_b_re, m_s5_b_im, m_s5_c_re, m_s5_c_im, m_s5_d, m_s5_log_dt, m_s5_w_glu, m_ret_gn_w, m_w_out, m_final_norm_w, v_norm_w, v_w_in, v_fox_b_f, v_s5_a_re, v_s5_a_im, v_s5_b_re, v_s5_b_im, v_s5_c_re, v_s5_c_im, v_s5_d, v_s5_log_dt, v_s5_w_glu, v_ret_gn_w, v_w_out, v_final_norm_w):
    w = dict(norm_w=norm_w, w_in=w_in, fox_b_f=fox_b_f, s5_a_re=s5_a_re, s5_a_im=s5_a_im, s5_b_re=s5_b_re, s5_b_im=s5_b_im,
             s5_c_re=s5_c_re, s5_c_im=s5_c_im, s5_d=s5_d, s5_log_dt=s5_log_dt, s5_w_glu=s5_w_glu, ret_gn_w=ret_gn_w,
             w_out=w_out, final_norm_w=final_norm_w)
    m = dict(norm_w=m_norm_w, w_in=m_w_in, fox_b_f=m_fox_b_f, s5_a_re=m_s5_a_re, s5_a_im=m_s5_a_im, s5_b_re=m_s5_b_re,
             s5_b_im=m_s5_b_im, s5_c_re=m_s5_c_re, s5_c_im=m_s5_c_im, s5_d=m_s5_d, s5_log_dt=m_s5_log_dt,
             s5_w_glu=m_s5_w_glu, ret_gn_w=m_ret_gn_w, w_out=m_w_out, final_norm_w=m_final_norm_w)
    v = dict(norm_w=v_norm_w, w_in=v_w_in, fox_b_f=v_fox_b_f, s5_a_re=v_s5_a_re, s5_a_im=v_s5_a_im, s5_b_re=v_s5_b_re,
             s5_b_im=v_s5_b_im, s5_c_re=v_s5_c_re, s5_c_im=v_s5_c_im, s5_d=v_s5_d, s5_log_dt=v_s5_log_dt,
             s5_w_glu=v_s5_w_glu, ret_gn_w=v_ret_gn_w, w_out=v_w_out, final_norm_w=v_final_norm_w)

    g_in, g_glu, g_out = _all_gather([w[n].astype(_MXU) for n in _SHARDED])
    w_in_p = _pad_w_in(jnp.moveaxis(g_in, 0, 2).reshape(DEPTH, D_MODEL, D_IN))
    w_glu_f = jnp.moveaxis(g_glu, 0, 1).reshape(DEPTH, D_S5, D_S5)
    w_out_f = jnp.moveaxis(g_out, 0, 1).reshape(DEPTH, D_MODEL, D_MODEL)

    small = {n: w[n] for n in _SMALL}
    loss, dx, grads, dfw = _step_grads(x[0], loss_target[0], w_in_p, w_out_f, w_glu_f, small)

    full = {n: jnp.stack([g[n] for g in grads]) for n in _WEIGHTS if n != "final_norm_w"}
    full["final_norm_w"] = dfw
    parts = [_to_slots(_unpad_w_in(full["w_in"]), 2), _to_slots(full["s5_w_glu"], 1), _to_slots(full["w_out"], 1)]
    r_in, r_glu, r_out, r_small = _exchange_grads(parts, _pack([full[n] for n in _SMALL]))

    res = {}
    for n, r in zip(_SHARDED, (r_in, r_glu, r_out)):
        cols = w[n].shape[-1]
        outs = _adamw(r.reshape(N_DEV, -1, cols), w[n].reshape(-1, cols), m[n].reshape(-1, cols), v[n].reshape(-1, cols),
                      "adamw_" + n)
        res[n] = [o.reshape(w[n].shape) for o in outs]
    small_w = [w[n] for n in _SMALL]
    outs = _adamw(r_small, _pack(small_w), _pack([m[n] for n in _SMALL]), _pack([v[n] for n in _SMALL]), "adamw_small")
    for k, o in enumerate(outs):
        for n, a in zip(_SMALL, _unpack(o, small_w)):
            res.setdefault(n, [None] * 4)[k] = a

    loss = lax.psum(loss[0, 0], ("x", "y", "c"))
    return (loss, dx[None], *[res[n][0] for n in _WEIGHTS], *[res[n][1] for n in _WEIGHTS],
            *[res[n][2] for n in _WEIGHTS], *[res[n][3] for n in _WEIGHTS])
```

```python
import math

import jax
import jax.numpy as jnp
from jax import lax
from jax.experimental import pallas as pl
from jax.experimental.pallas import tpu as pltpu

F32 = jnp.float32
_MXU = jnp.bfloat16
_HI = lax.Precision.HIGHEST

N_DEV = 8
DEPTH = 4
D_MODEL = 1024
HEAD_DIM = 64
D_FOX = 512
FOX_HEADS = 8
D_S5 = 256
S5_GROUPS = 16
S5_GROUP_CH = 16
S5_STATE = 64
S5_CH = S5_GROUPS * S5_STATE
D_RET = 256
RET_HEADS = 4
CHUNK = 64
ROPE_BASE = 10000.0
EPS = 1e-6
D_IN = 3592
D_INP = 3712
W_SHARD = D_IN // N_DEV
O_GATE, O_FQ, O_FK, O_FV, O_SU, O_RQ, O_RK, O_RV, O_FL = 0, 1024, 1536, 2048, 2560, 2816, 3072, 3328, 3584

ADAM_LR, ADAM_B1, ADAM_B2, ADAM_EPS, ADAM_WD, ADAM_STEP = 0.001, 0.9, 0.999, 1e-08, 0.01, 10

TM = 256
TQ = 512
TS = 256
NEG = -1e30
VMEM_BIG = 56 * 1024 * 1024


def _pallas(body, **kw):
    return pl.pallas_call(body, **kw)


def _whole(shape):
    n = len(shape)
    return pl.BlockSpec(shape, lambda *_: (0,) * n)


def _rows(tm, width, col=0):
    return pl.BlockSpec((tm, width), lambda i: (i, col))


def _dot(a, b, dims=(((1,), (0,)), ((), ()))):
    return lax.dot_general(a.astype(_MXU), b.astype(_MXU), dims, preferred_element_type=F32)


_NT = (((1,), (1,)), ((), ()))
_TN = (((0,), (0,)), ((), ()))


def _mm(a, b, *, ta=False, tb=False, add=None, tm=512, tn=512, tk=512, name):
    m, k = (a.shape[1], a.shape[0]) if ta else a.shape
    n = b.shape[0] if tb else b.shape[1]
    tm, tn, tk = min(tm, m), min(tn, n), min(tk, k)
    assert m % tm == 0 and n % tn == 0 and k % tk == 0, (name, m, n, k)
    dims = (((0 if ta else 1,), (1 if tb else 0,)), ((), ()))

    def body(*refs):
        a_ref, b_ref = refs[0], refs[1]
        o_ref = refs[-1]
        p = _dot(a_ref[...], b_ref[...], dims)
        kk = pl.program_id(2)

        @pl.when(kk == 0)
        def _():
            o_ref[...] = p if add is None else p + refs[2][...]

        @pl.when(kk != 0)
        def _():
            o_ref[...] += p

    a_spec = pl.BlockSpec((tk, tm), lambda i, j, kk: (kk, i)) if ta else pl.BlockSpec((tm, tk), lambda i, j, kk: (i, kk))
    b_spec = pl.BlockSpec((tn, tk), lambda i, j, kk: (j, kk)) if tb else pl.BlockSpec((tk, tn), lambda i, j, kk: (kk, j))
    o_spec = pl.BlockSpec((tm, tn), lambda i, j, kk: (i, j))
    ins, specs = [a, b], [a_spec, b_spec]
    if add is not None:
        ins.append(add)
        specs.append(o_spec)
    return _pallas(body, out_shape=jax.ShapeDtypeStruct((m, n), F32), grid=(m // tm, n // tn, k // tk),
                   in_specs=specs, out_specs=o_spec, name=name,
                   compiler_params=pltpu.CompilerParams(vmem_limit_bytes=VMEM_BIG))(*ins)


def _norm_inproj(x, g, w):
    L = x.shape[0]

    def body(x_ref, g_ref, w_ref, p_ref, h_ref):
        xv = x_ref[...]
        r = lax.rsqrt(jnp.mean(xv * xv, axis=-1, keepdims=True) + EPS)
        h = (xv * r * g_ref[...]).astype(_MXU)
        h_ref[...] = h
        p_ref[...] = _dot(h, w_ref[...])

    return _pallas(body, out_shape=(jax.ShapeDtypeStruct((L, D_INP), F32), jax.ShapeDtypeStruct((L, D_MODEL), _MXU)),
                   grid=(L // TM,), in_specs=[_rows(TM, D_MODEL), _whole((1, D_MODEL)), _whole((D_MODEL, D_INP))],
                   out_specs=(_rows(TM, D_INP), _rows(TM, D_MODEL)), name="norm_inproj",
                   compiler_params=pltpu.CompilerParams(vmem_limit_bytes=VMEM_BIG))(x, g, w)


def _rms_bwd(xv, g, dh):
    r = lax.rsqrt(jnp.mean(xv * xv, axis=-1, keepdims=True) + EPS)
    xh = xv * r
    dg = jnp.sum(dh * xh, axis=0, keepdims=True)
    dxh = dh * g
    dx = r * (dxh - xh * jnp.mean(dxh * xh, axis=-1, keepdims=True))
    return dx, dg


def _inproj_bwd_dx(dproj, w, x, g, dres):
    L = x.shape[0]

    def body(dp_ref, w_ref, x_ref, g_ref, dr_ref, dx_ref, dg_ref):
        dh = _dot(dp_ref[...], w_ref[...], _NT)
        dx, dg = _rms_bwd(x_ref[...], g_ref[...], dh)
        dx_ref[...] = dx + dr_ref[...]

        @pl.when(pl.program_id(0) == 0)
        def _():
            dg_ref[...] = dg

        @pl.when(pl.program_id(0) != 0)
        def _():
            dg_ref[...] += dg

    return _pallas(body, out_shape=(jax.ShapeDtypeStruct((L, D_MODEL), F32), jax.ShapeDtypeStruct((1, D_MODEL), F32)),
                   grid=(L // TM,),
                   in_specs=[_rows(TM, D_INP), _whole((D_MODEL, D_INP)), _rows(TM, D_MODEL), _whole((1, D_MODEL)),
                             _rows(TM, D_MODEL)],
                   out_specs=(_rows(TM, D_MODEL), _whole((1, D_MODEL))), name="inproj_bwd_dx",
                   compiler_params=pltpu.CompilerParams(vmem_limit_bytes=VMEM_BIG))(dproj, w, x, g, dres)


PAIR = 2 * HEAD_DIM
N_AUX = 3


def _own(shape, h):
    return lax.broadcasted_iota(jnp.int32, shape, len(shape) - 1) // HEAD_DIM == h


def _hi_dot(a, b):
    return jnp.dot(a, b, precision=_HI, preferred_element_type=F32)


def _tri(n, lower):
    r = lax.broadcasted_iota(jnp.int32, (n, n), 0)
    c = lax.broadcasted_iota(jnp.int32, (n, n), 1)
    return jnp.where(r >= c if lower else r <= c, 1.0, 0.0).astype(F32)


def _fox_cumsum(proj, b):
    L = proj.shape[0]

    def body(fl_ref, b_ref, c_ref, carry_sc):
        @pl.when(pl.program_id(0) == 0)
        def _():
            carry_sc[...] = jnp.zeros((1, PAIR), F32)

        lane = lax.broadcasted_iota(jnp.int32, (TM, PAIR), 1)
        lf = jnp.where(lane < FOX_HEADS, jax.nn.log_sigmoid(fl_ref[...] + b_ref[...]), 0.0)
        cs = _hi_dot(_tri(TM, True), lf) + carry_sc[...]
        c_ref[...] = cs
        carry_sc[...] = cs[TM - 1:TM, :]

    return _pallas(body, out_shape=jax.ShapeDtypeStruct((L, PAIR), F32), grid=(L // TM,),
                   in_specs=[_rows(TM, PAIR, O_FL // PAIR), _whole((1, PAIR))], out_specs=_rows(TM, PAIR),
                   scratch_shapes=[pltpu.VMEM((1, PAIR), F32)], name="fox_cumsum")(proj, b)


def _fox_prep(proj, c):
    L = proj.shape[0]

    def body(q_ref, k_ref, v_ref, c_ref, qa_ref, ka_ref, kat_ref, vt_ref):
        lane = lax.broadcasted_iota(jnp.int32, (TM, PAIR), 1)
        cv = c_ref[...]
        for p in range(FOX_HEADS // 2):
            cols = slice(PAIR * p, PAIR * (p + 1))
            q2, k2 = q_ref[:, cols], k_ref[:, cols]
            vt_ref[p] = v_ref[:, cols].T.astype(_MXU)
            for e in range(2):
                h = 2 * p + e
                own = lane // HEAD_DIM == e
                a = lane - (HEAD_DIM if e == 0 else 0)
                pick = (lax.broadcasted_iota(jnp.int32, (PAIR, PAIR), 0) == h).astype(F32)
                rest = _hi_dot(cv, pick)
                aux_q = jnp.where((a >= N_AUX) & (a < 2 * N_AUX), 1.0, 0.0)
                aux_k = jnp.where((a >= 0) & (a < N_AUX), 1.0, 0.0)
                for n in range(N_AUX):
                    part = rest.astype(_MXU).astype(F32)
                    rest = rest - part
                    aux_q = jnp.where(a == n, part, aux_q)
                    aux_k = jnp.where(a == N_AUX + n, -part, aux_k)
                ka = jnp.where(own, k2, aux_k)
                qa_ref[h] = jnp.where(own, q2 * (1.0 / math.sqrt(HEAD_DIM)), aux_q).astype(_MXU)
                ka_ref[h] = ka.astype(_MXU)
                kat_ref[h] = ka.T.astype(_MXU)

    hl = jax.ShapeDtypeStruct((FOX_HEADS, L, PAIR), _MXU)
    nat = lambda o: _rows(TM, D_FOX, o // D_FOX)
    return _pallas(
        body, out_shape=(hl, hl, jax.ShapeDtypeStruct((FOX_HEADS, PAIR, L), _MXU),
                         jax.ShapeDtypeStruct((FOX_HEADS // 2, PAIR, L), _MXU)),
        grid=(L // TM,), in_specs=[nat(O_FQ), nat(O_FK), nat(O_FV), _rows(TM, PAIR)],
        out_specs=(pl.BlockSpec((FOX_HEADS, TM, PAIR), lambda i: (0, i, 0)), pl.BlockSpec((FOX_HEADS, TM, PAIR), lambda i: (0, i, 0)),
                   pl.BlockSpec((FOX_HEADS, PAIR, TM), lambda i: (0, 0, i)), pl.BlockSpec((FOX_HEADS // 2, PAIR, TM), lambda i: (0, 0, i))),
        name="fox_prep")(proj, proj, proj, c)


def _key_le_query(tq):
    return lax.broadcasted_iota(jnp.int32, (tq, tq), 0) <= lax.broadcasted_iota(jnp.int32, (tq, tq), 1)


def _fox_fwd(qa, ka, vt):
    H, L, _ = qa.shape
    tq = min(TQ, L)
    nq = L // tq

    def body(qa_ref, ka_ref, vt_ref, o_ref, lse_ref, m_sc, l_sc, acc_sc):
        i = pl.program_id(1)
        m_sc[...] = jnp.full((2, 1, tq), NEG, F32)
        l_sc[...] = jnp.zeros((2, 1, tq), F32)
        acc_sc[...] = jnp.zeros((2, PAIR, tq), F32)

        def block(j, masked):
            keys = pl.ds(pl.multiple_of(j * tq, tq), tq)
            vt_blk = vt_ref[:, keys]
            for e in range(2):
                st = _dot(ka_ref[e, keys, :], qa_ref[e], _NT)
                if masked:
                    st = jnp.where(_key_le_query(tq), st, NEG)
                m_prev = m_sc[e]
                m_new = jnp.maximum(m_prev, jnp.max(st, axis=0, keepdims=True))
                alpha = jnp.exp(m_prev - m_new)
                pt = jnp.exp(st - m_new)
                l_sc[e] = alpha * l_sc[e] + jnp.sum(pt, axis=0, keepdims=True)
                acc_sc[e] = alpha * acc_sc[e] + _dot(vt_blk, pt)
                m_sc[e] = m_new

        def off_diagonal(j, carry):
            block(j, False)
            return carry

        lax.fori_loop(0, i, off_diagonal, 0)
        block(i, True)
        row = lax.broadcasted_iota(jnp.int32, (PAIR, tq), 0)
        ot = jnp.where(row < HEAD_DIM, acc_sc[0] / l_sc[0], acc_sc[1] / l_sc[1])
        o_ref[...] = ot.T
        for e in range(2):
            lse_ref[e] = m_sc[e] + jnp.log(l_sc[e])

    return _pallas(
        body, out_shape=(jax.ShapeDtypeStruct((L, D_FOX), F32), jax.ShapeDtypeStruct((H, 1, L), F32)),
        grid=(H // 2, nq),
        in_specs=[pl.BlockSpec((2, tq, PAIR), lambda p, i: (p, i, 0)), pl.BlockSpec((2, L, PAIR), lambda p, i: (p, 0, 0)),
                  pl.BlockSpec((None, PAIR, L), lambda p, i: (p, 0, 0))],
        out_specs=(pl.BlockSpec((tq, PAIR), lambda p, i: (i, p)), pl.BlockSpec((2, 1, tq), lambda p, i: (p, 0, i))),
        scratch_shapes=[pltpu.VMEM((2, 1, tq), F32), pltpu.VMEM((2, 1, tq), F32), pltpu.VMEM((2, PAIR, tq), F32)],
        name="fox_fwd")(qa, ka, vt)


def _fox_bwd(qa, ka, kat, proj, do, o, lse):
    H, L, _ = qa.shape
    tq = min(TQ, L)
    nq = L // tq

    def body(qa_ref, ka_ref, kat_ref, v_ref, do_ref, o_ref, lse_ref, dqt_ref, dk_ref, dv_ref, delta_sc, dk_sc, dv_sc):
        j = pl.program_id(1)

        @pl.when(j == 0)
        def _():
            head_rows = (lax.broadcasted_iota(jnp.int32, (8, PAIR), 1) // HEAD_DIM
                         == lax.broadcasted_iota(jnp.int32, (8, PAIR), 0)).astype(F32)
            delta_sc[...] = lax.dot_general(head_rows, do_ref[...] * o_ref[...], _NT, precision=_HI,
                                            preferred_element_type=F32)
            dqt_ref[...] = jnp.zeros((2, PAIR, L), F32)

        dk_sc[...] = jnp.zeros((2, tq, PAIR), F32)
        dv_sc[...] = jnp.zeros((tq, PAIR), F32)
        vb = v_ref[...]

        def block(i, masked):
            qs = pl.ds(pl.multiple_of(i * tq, tq), tq)
            dob = do_ref[qs, :]
            for e in range(2):
                own = _own((tq, PAIR), e)
                qh = qa_ref[e, qs, :]
                pt = jnp.exp(_dot(ka_ref[e], qh, _NT) - lse_ref[e, :, qs])
                if masked:
                    pt = jnp.where(_key_le_query(tq), pt, 0.0)
                dv_sc[...] += _dot(pt, jnp.where(own, dob, 0.0))
                dpt = _dot(jnp.where(own, vb, 0.0), dob, _NT)
                ds = (pt * (dpt - delta_sc[e:e + 1, qs])).astype(_MXU)
                dk_sc[e] += _dot(ds, qh)
                dqt_ref[e, :, qs] += _dot(kat_ref[e], ds)

        def off_diagonal(i, carry):
            block(i, False)
            return carry

        block(j, True)
        lax.fori_loop(j + 1, nq, off_diagonal, 0)
        dk_ref[...] = dk_sc[...]
        dv_ref[...] = dv_sc[...]

    nat = pl.BlockSpec((L, PAIR), lambda p, j: (0, p))
    return _pallas(
        body, out_shape=(jax.ShapeDtypeStruct((H, PAIR, L), F32), jax.ShapeDtypeStruct((H, L, PAIR), F32),
                         jax.ShapeDtypeStruct((L, D_FOX), F32)),
        grid=(H // 2, nq),
        in_specs=[pl.BlockSpec((2, L, PAIR), lambda p, j: (p, 0, 0)), pl.BlockSpec((2, tq, PAIR), lambda p, j: (p, j, 0)),
                  pl.BlockSpec((2, PAIR, tq), lambda p, j: (p, 0, j)),
                  pl.BlockSpec((tq, PAIR), lambda p, j: (j, O_FV // PAIR + p)), nat, nat,
                  pl.BlockSpec((2, 1, L), lambda p, j: (p, 0, 0))],
        out_specs=(pl.BlockSpec((2, PAIR, L), lambda p, j: (p, 0, 0)), pl.BlockSpec((2, tq, PAIR), lambda p, j: (p, j, 0)),
                   pl.BlockSpec((tq, PAIR), lambda p, j: (j, p))),
        scratch_shapes=[pltpu.VMEM((8, L), F32), pltpu.VMEM((2, tq, PAIR), F32), pltpu.VMEM((tq, PAIR), F32)],
        name="fox_bwd", compiler_params=pltpu.CompilerParams(vmem_limit_bytes=VMEM_BIG))(qa, ka, kat, proj, do, o, lse)


def _fox_post_bwd(dqt, dkraw, proj, b):
    L = proj.shape[0]
    nb = L // TM

    def body(dqt_ref, dkr_ref, fl_ref, b_ref, dq_ref, dk_ref, dfl_ref, db_ref, carry_sc):
        first = pl.program_id(0) == 0

        @pl.when(first)
        def _():
            carry_sc[...] = jnp.zeros((1, PAIR), F32)

        lane = lax.broadcasted_iota(jnp.int32, (TM, PAIR), 1)
        rr = lax.broadcasted_iota(jnp.int32, (PAIR, PAIR), 0)
        cc = lax.broadcasted_iota(jnp.int32, (PAIR, PAIR), 1)
        dc = jnp.zeros((TM, PAIR), F32)
        for p in range(FOX_HEADS // 2):
            cols = slice(PAIR * p, PAIR * (p + 1))
            dqs = [dqt_ref[2 * p + e].T for e in range(2)]
            dks = [dkr_ref[2 * p + e] for e in range(2)]
            dq_ref[:, cols] = jnp.where(lane < HEAD_DIM, dqs[0], dqs[1]) * (1.0 / math.sqrt(HEAD_DIM))
            dk_ref[:, cols] = jnp.where(lane < HEAD_DIM, dks[0], dks[1])
            for e in range(2):
                base = HEAD_DIM if e == 0 else 0
                to_head = cc == 2 * p + e
                dc = dc + _hi_dot(dqs[e], jnp.where((rr == base) & to_head, 1.0, 0.0))
                dc = dc + _hi_dot(dks[e], jnp.where((rr == base + N_AUX) & to_head, -1.0, 0.0))
        rs = _hi_dot(_tri(TM, False), dc) + carry_sc[...]
        carry_sc[...] = rs[0:1, :]
        dfl = jnp.where(lane < FOX_HEADS, rs * jax.nn.sigmoid(-(fl_ref[...] + b_ref[...])), 0.0)
        dfl_ref[...] = dfl
        db = jnp.sum(dfl, axis=0, keepdims=True)

        @pl.when(first)
        def _():
            db_ref[...] = db

        @pl.when(jnp.logical_not(first))
        def _():
            db_ref[...] += db

    rev = lambda i: nb - 1 - i
    nat = pl.BlockSpec((TM, D_FOX), lambda i: (rev(i), 0))
    return _pallas(
        body, out_shape=(jax.ShapeDtypeStruct((L, D_FOX), F32),) * 2
        + (jax.ShapeDtypeStruct((L, PAIR), F32), jax.ShapeDtypeStruct((1, PAIR), F32)),
        grid=(nb,),
        in_specs=[pl.BlockSpec((FOX_HEADS, PAIR, TM), lambda i: (0, 0, rev(i))),
                  pl.BlockSpec((FOX_HEADS, TM, PAIR), lambda i: (0, rev(i), 0)),
                  pl.BlockSpec((TM, PAIR), lambda i: (rev(i), O_FL // PAIR)), _whole((1, PAIR))],
        out_specs=(nat, nat, pl.BlockSpec((TM, PAIR), lambda i: (rev(i), 0)), _whole((1, PAIR))),
        scratch_shapes=[pltpu.VMEM((1, PAIR), F32)], name="fox_post_bwd")(dqt, dkraw, proj, b)


def _s5_expand():
    r = lax.broadcasted_iota(jnp.int32, (S5_STATE, S5_STATE * S5_GROUP_CH), 0)
    c = lax.broadcasted_iota(jnp.int32, (S5_STATE, S5_STATE * S5_GROUP_CH), 1)
    return jnp.where(c // S5_GROUP_CH == r, 1.0, 0.0).astype(F32)


def _s5_disc_math(ar, ai, ldt, br, bi):
    dt = jnp.exp(ldt)
    mag = jnp.exp(ar * dt)
    lr = mag * jnp.cos(ai * dt)
    li = mag * jnp.sin(ai * dt)
    den = ar * ar + ai * ai
    fr = ((lr - 1.0) * ar + li * ai) / den
    fi = (li * ar - (lr - 1.0) * ai) / den
    e = _s5_expand()
    fre = jnp.dot(fr, e, precision=_HI, preferred_element_type=F32)
    fie = jnp.dot(fi, e, precision=_HI, preferred_element_type=F32)
    return lr, li, fre * br - fie * bi, fre * bi + fie * br


def _s5_disc(ar, ai, ldt, br, bi):
    def body(ar_ref, ai_ref, ldt_ref, br_ref, bi_ref, lr_ref, li_ref, bbr_ref, bbi_ref):
        lr, li, bbr, bbi = _s5_disc_math(ar_ref[...], ai_ref[...], ldt_ref[...], br_ref[...], bi_ref[...])
        lr_ref[...] = lr
        li_ref[...] = li
        bbr_ref[...] = bbr
        bbi_ref[...] = bbi

    gp = jax.ShapeDtypeStruct(ar.shape, F32)
    gb = jax.ShapeDtypeStruct(br.shape, F32)
    return _pallas(body, out_shape=(gp, gp, gb, gb), name="s5_disc")(ar, ai, ldt, br, bi)


def _s5_disc_bwd(ar, ai, ldt, br, bi, dlr, dli, dbbr, dbbi):
    def body(ar_ref, ai_ref, ldt_ref, br_ref, bi_ref, dlr_ref, dli_ref, dbbr_ref, dbbi_ref,
             dar_ref, dai_ref, dldt_ref, dbr_ref, dbi_ref):
        _, vjp = jax.vjp(_s5_disc_math, ar_ref[...], ai_ref[...], ldt_ref[...], br_ref[...], bi_ref[...])
        dar, dai, dldt, dbr, dbi = vjp((dlr_ref[...], dli_ref[...], dbbr_ref[...], dbbi_ref[...]))
        dar_ref[...] = dar
        dai_ref[...] = dai
        dldt_ref[...] = dldt
        dbr_ref[...] = dbr
        dbi_ref[...] = dbi

    gp = jax.ShapeDtypeStruct(ar.shape, F32)
    gb = jax.ShapeDtypeStruct(br.shape, F32)
    return _pallas(body, out_shape=(gp, gp, jax.ShapeDtypeStruct(ldt.shape, F32), gb, gb),
                   name="s5_disc_bwd")(ar, ai, ldt, br, bi, dlr, dli, dbbr, dbbi)


def _s5_scan(bu, lam):
    L = bu.shape[0]
    ts = min(TS, L)

    def body(b_ref, lam_ref, x_ref, c_sc):
        @pl.when(pl.program_id(0) == 0)
        def _():
            c_sc[...] = jnp.zeros((16, 128), F32)

        lr, li = lam_ref[0:8, :], lam_ref[8:16, :]

        def step(t, carry):
            xr, xi = carry
            nr = lr * xr - li * xi + b_ref[t, 0:8, :]
            ni = lr * xi + li * xr + b_ref[t, 8:16, :]
            x_ref[t, 0:8, :] = nr
            x_ref[t, 8:16, :] = ni
            return nr, ni

        xr, xi = lax.fori_loop(0, ts, step, (c_sc[0:8, :], c_sc[8:16, :]), unroll=8)
        c_sc[0:8, :] = xr
        c_sc[8:16, :] = xi

    blk = pl.BlockSpec((ts, 16, 128), lambda n: (n, 0, 0))
    return _pallas(body, out_shape=jax.ShapeDtypeStruct((L, 16, 128), F32), grid=(L // ts,),
                   in_specs=[blk, _whole((16, 128))], out_specs=blk,
                   scratch_shapes=[pltpu.VMEM((16, 128), F32)], name="s5_scan")(bu, lam)


def _s5_scan_bwd(dx, xprev, lam):
    L = dx.shape[0]
    ts = min(TS, L)
    nb = L // ts

    def body(dx_ref, xp_ref, lam_ref, g_ref, dlam_ref, c_sc):
        @pl.when(pl.program_id(0) == 0)
        def _():
            c_sc[...] = jnp.zeros((16, 128), F32)
            dlam_ref[...] = jnp.zeros((16, 128), F32)

        lr, li = lam_ref[0:8, :], lam_ref[8:16, :]

        def step(n, carry):
            gr, gi, ar, ai = carry
            t = ts - 1 - n
            nr = dx_ref[t, 0:8, :] + lr * gr + li * gi
            ni = dx_ref[t, 8:16, :] - li * gr + lr * gi
            g_ref[t, 0:8, :] = nr
            g_ref[t, 8:16, :] = ni
            pr, pi = xp_ref[t, 0:8, :], xp_ref[t, 8:16, :]
            return nr, ni, ar + nr * pr + ni * pi, ai - nr * pi + ni * pr

        z = jnp.zeros((8, 128), F32)
        gr, gi, ar, ai = lax.fori_loop(0, ts, step, (c_sc[0:8, :], c_sc[8:16, :], z, z), unroll=8)
        c_sc[0:8, :] = gr
        c_sc[8:16, :] = gi
        dlam_ref[0:8, :] += ar
        dlam_ref[8:16, :] += ai

    blk = pl.BlockSpec((ts, 16, 128), lambda n: (nb - 1 - n, 0, 0))
    return _pallas(body, out_shape=(jax.ShapeDtypeStruct((L, 16, 128), F32), jax.ShapeDtypeStruct((16, 128), F32)),
                   grid=(nb,), in_specs=[blk, blk, _whole((16, 128))], out_specs=(blk, _whole((16, 128))),
                   scratch_shapes=[pltpu.VMEM((16, 128), F32)], name="s5_scan_bwd")(dx, xprev, lam)


def _s5_post(ypre, proj, d, w):
    L = ypre.shape[0]

    def body(y_ref, u_ref, d_ref, w_ref, o_ref):
        y1 = jax.nn.gelu(y_ref[...] + d_ref[...] * u_ref[...])
        o_ref[...] = y1 * jax.nn.sigmoid(_dot(y1, w_ref[...]))

    return _pallas(body, out_shape=jax.ShapeDtypeStruct((L, D_S5), F32), grid=(L // TM,),
                   in_specs=[_rows(TM, D_S5), _rows(TM, D_S5, O_SU // D_S5), _whole((1, D_S5)), _whole((D_S5, D_S5))],
                   out_specs=_rows(TM, D_S5), name="s5_post")(ypre, proj, d, w)


def _s5_post_bwd(ypre, proj, d, w, dout):
    L = ypre.shape[0]

    def body(y_ref, u_ref, d_ref, w_ref, do_ref, dy_ref, du_ref, dd_ref, dw_ref):
        u, dv, dout_v = u_ref[...], d_ref[...], do_ref[...]
        y1, gelu_vjp = jax.vjp(jax.nn.gelu, y_ref[...] + dv * u)
        sg = jax.nn.sigmoid(_dot(y1, w_ref[...]))
        dz = dout_v * y1 * sg * (1.0 - sg)
        dy1 = dout_v * sg + _dot(dz, w_ref[...], _NT)
        dy, = gelu_vjp(dy1)
        dy_ref[...] = dy
        du_ref[...] = dy * dv
        dd = jnp.sum(dy * u, axis=0, keepdims=True)
        dw = _dot(y1, dz, _TN)

        @pl.when(pl.program_id(0) == 0)
        def _():
            dd_ref[...] = dd
            dw_ref[...] = dw

        @pl.when(pl.program_id(0) != 0)
        def _():
            dd_ref[...] += dd
            dw_ref[...] += dw

    row = _rows(TM, D_S5)
    return _pallas(body, out_shape=(jax.ShapeDtypeStruct((L, D_S5), F32),) * 2
                   + (jax.ShapeDtypeStruct((1, D_S5), F32), jax.ShapeDtypeStruct((D_S5, D_S5), F32)),
                   grid=(L // TM,),
                   in_specs=[row, _rows(TM, D_S5, O_SU // D_S5), _whole((1, D_S5)), _whole((D_S5, D_S5)), row],
                   out_specs=(row, row, _whole((1, D_S5)), _whole((D_S5, D_S5))), name="s5_post_bwd")(ypre, proj, d, w, dout)


def _rot(z, cos, sin):
    lane = lax.broadcasted_iota(jnp.int32, z.shape, 1)
    zs = z * sin
    half = HEAD_DIM // 2
    return z * cos + jnp.where(lane % HEAD_DIM < half, pltpu.roll(zs, PAIR - half, 1), pltpu.roll(zs, half, 1))


def _head_avg():
    r = lax.broadcasted_iota(jnp.int32, (PAIR, PAIR), 0) // HEAD_DIM
    c = lax.broadcasted_iota(jnp.int32, (PAIR, PAIR), 1) // HEAD_DIM
    return jnp.where(r == c, 1.0 / HEAD_DIM, 0.0).astype(F32)


def _ret_tables(tq):
    lg = jnp.log1p(-(2.0 ** (-5.0 - jnp.arange(RET_HEADS, dtype=F32))))
    scale = 1.0 / math.sqrt(HEAD_DIM)
    pos = jnp.arange(tq)
    n = pos.astype(F32)
    dist = jnp.abs(n[:, None] - n[None, :])
    ok = (pos[None, :] // CHUNK) <= (pos[:, None] // CHUNK)
    w = jnp.where(ok[None], scale * jnp.exp(lg[:, None, None] * dist[None]), 0.0)
    lgl = jnp.repeat(lg, HEAD_DIM)
    dq_tab = scale * jnp.exp(lgl[None, :] * (n[:, None] + 1.0))
    dk_tab = jnp.exp(lgl[None, :] * (tq - 1.0 - n[:, None]))
    blk = jnp.arange(PAIR) // HEAD_DIM
    bd = (blk[:, None] == blk[None, :]).astype(F32)
    gbd = bd[None] * jnp.exp(lgl.reshape(RET_HEADS // 2, PAIR)[:, :, None] * tq)
    return dict(w=w, wt=w.transpose(0, 2, 1), dq=dq_tab, dk=dk_tab, gbd=gbd, bd=bd)


def _ret_specs(tq, nq, rev):
    blk = (lambda i: nq - 1 - i) if rev else (lambda i: i)
    col = lambda o: pl.BlockSpec((tq, PAIR), lambda p, i: (blk(i), o // PAIR + p))
    return dict(
        rq=col(O_RQ), rk=col(O_RK), rv=col(O_RV), nat=col(0),
        w=pl.BlockSpec((2, tq, tq), lambda p, i: (p, 0, 0)), tab=pl.BlockSpec((tq, PAIR), lambda p, i: (0, p)),
        gbd=pl.BlockSpec((None, PAIR, PAIR), lambda p, i: (p, 0, 0)), bd=pl.BlockSpec((PAIR, PAIR), lambda p, i: (0, 0)),
        gn=pl.BlockSpec((1, PAIR), lambda p, i: (0, p)),
        st=pl.BlockSpec((None, None, PAIR, PAIR), lambda p, i: (p, blk(i), 0, 0)))


def _ret_fwd(proj, cos_t, sin_t, tabs, gn):
    L = proj.shape[0]
    tq = tabs["w"].shape[1]
    nq = L // tq

    def body(rq_ref, rk_ref, rv_ref, cos_ref, sin_ref, w_ref, dqt_ref, dkt_ref, gbd_ref, bd_ref, gn_ref,
             o_ref, y_ref, st_ref, s_sc):
        @pl.when(pl.program_id(1) == 0)
        def _():
            s_sc[...] = jnp.zeros((PAIR, PAIR), F32)

        state = s_sc[...]
        st_ref[...] = state
        cos, sin = cos_ref[...], sin_ref[...]
        q2, k2, v2 = _rot(rq_ref[...], cos, sin), _rot(rk_ref[...], cos, sin), rv_ref[...]
        o = _dot(q2 * dqt_ref[...], state)
        for h in range(2):
            own = _own((tq, PAIR), h)
            a = _dot(jnp.where(own, q2, 0.0), k2, _NT) * w_ref[h]
            o = o + _dot(a, jnp.where(own, v2, 0.0))
        s_sc[...] = gbd_ref[...] * state + bd_ref[...] * _dot(k2 * dkt_ref[...], v2, _TN)
        o_ref[...] = o
        avg = _head_avg()
        oc = o - _hi_dot(o, avg)
        y_ref[...] = oc * lax.rsqrt(_hi_dot(oc * oc, avg) + EPS) * gn_ref[...]

    sp = _ret_specs(tq, nq, False)
    nat = jax.ShapeDtypeStruct((L, D_RET), F32)
    return _pallas(
        body, out_shape=(nat, nat, jax.ShapeDtypeStruct((RET_HEADS // 2, nq, PAIR, PAIR), F32)), grid=(RET_HEADS // 2, nq),
        in_specs=[sp["rq"], sp["rk"], sp["rv"], sp["nat"], sp["nat"], sp["w"], sp["tab"], sp["tab"], sp["gbd"], sp["bd"],
                  sp["gn"]],
        out_specs=(sp["nat"], sp["nat"], sp["st"]), scratch_shapes=[pltpu.VMEM((PAIR, PAIR), F32)],
        name="ret_fwd")(proj, proj, proj, cos_t, sin_t, tabs["w"], tabs["dq"], tabs["dk"], tabs["gbd"], tabs["bd"], gn)


def _ret_bwd(proj, cos_t, sin_t, tabs, gn, o_pre, dy, states):
    L = proj.shape[0]
    tq = tabs["w"].shape[1]
    nq = L // tq

    def body(rq_ref, rk_ref, rv_ref, cos_ref, sin_ref, w_ref, wt_ref, dqt_ref, dkt_ref, gbd_ref, bd_ref, gn_ref,
             o_ref, dy_ref, st_ref, drq_ref, drk_ref, drv_ref, dgn_ref, g_sc):
        first = pl.program_id(1) == 0

        @pl.when(first)
        def _():
            g_sc[...] = jnp.zeros((PAIR, PAIR), F32)

        cos, sin = cos_ref[...], sin_ref[...]
        q2, k2, v2 = _rot(rq_ref[...], cos, sin), _rot(rk_ref[...], cos, sin), rv_ref[...]
        avg = _head_avg()
        ov, dyv = o_ref[...], dy_ref[...]
        oc = ov - _hi_dot(ov, avg)
        r = lax.rsqrt(_hi_dot(oc * oc, avg) + EPS)
        oh = oc * r
        dgn = jnp.sum(dyv * oh, axis=0, keepdims=True)
        doh = dyv * gn_ref[...]
        do = r * (doh - _hi_dot(doh, avg) - oh * _hi_dot(doh * oh, avg))
        state, g = st_ref[...], g_sc[...]
        dqt, dkt = dqt_ref[...], dkt_ref[...]
        dq = _dot(do, state, _NT) * dqt
        dk = _dot(v2, g, _NT) * dkt
        dv = _dot(k2 * dkt, g)
        g_sc[...] = gbd_ref[...] * g + bd_ref[...] * _dot(q2 * dqt, do, _TN)
        for h in range(2):
            own = _own((tq, PAIR), h)
            qm, dom = jnp.where(own, q2, 0.0), jnp.where(own, do, 0.0)
            dv = dv + _dot(_dot(k2, qm, _NT) * wt_ref[h], dom)
            dq = dq + _dot(_dot(dom, v2, _NT) * w_ref[h], jnp.where(own, k2, 0.0))
            dk = dk + _dot(_dot(v2, dom, _NT) * wt_ref[h], qm)
        drq_ref[...] = _rot(dq, cos, -sin)
        drk_ref[...] = _rot(dk, cos, -sin)
        drv_ref[...] = dv

        @pl.when(first)
        def _():
            dgn_ref[...] = dgn

        @pl.when(jnp.logical_not(first))
        def _():
            dgn_ref[...] += dgn

    sp = _ret_specs(tq, nq, True)
    nat = jax.ShapeDtypeStruct((L, D_RET), F32)
    return _pallas(
        body, out_shape=(nat, nat, nat, jax.ShapeDtypeStruct((1, D_RET), F32)), grid=(RET_HEADS // 2, nq),
        in_specs=[sp["rq"], sp["rk"], sp["rv"], sp["nat"], sp["nat"], sp["w"], sp["w"], sp["tab"], sp["tab"], sp["gbd"],
                  sp["bd"], sp["gn"], sp["nat"], sp["nat"], sp["st"]],
        out_specs=(sp["nat"], sp["nat"], sp["nat"], sp["gn"]), scratch_shapes=[pltpu.VMEM((PAIR, PAIR), F32)],
        name="ret_bwd")(proj, proj, proj, cos_t, sin_t, tabs["w"], tabs["wt"], tabs["dq"], tabs["dk"], tabs["gbd"],
                        tabs["bd"], gn, o_pre, dy, states)


def _gate_out(yf, ys, yr, proj, x, w):
    L = x.shape[0]

    def body(yf_ref, ys_ref, yr_ref, g_ref, x_ref, w_ref, y_ref, xn_ref):
        cat = jnp.concatenate([yf_ref[...], ys_ref[...], yr_ref[...]], axis=-1)
        y = cat * jax.nn.silu(g_ref[...])
        y_ref[...] = y
        xn_ref[...] = x_ref[...] + _dot(y, w_ref[...])

    full = _rows(TM, D_MODEL)
    return _pallas(body, out_shape=(jax.ShapeDtypeStruct((L, D_MODEL), F32),) * 2, grid=(L // TM,),
                   in_specs=[_rows(TM, D_FOX), _rows(TM, D_S5), _rows(TM, D_RET), _rows(TM, D_MODEL, O_GATE // D_MODEL),
                             full, _whole((D_MODEL, D_MODEL))],
                   out_specs=(full, full), name="gate_out")(yf, ys, yr, proj, x, w)


def _gate_out_bwd(dxn, w, yf, ys, yr, proj):
    L = dxn.shape[0]

    def body(dx_ref, w_ref, yf_ref, ys_ref, yr_ref, g_ref, dyf_ref, dys_ref, dyr_ref, dg_ref):
        dy = _dot(dx_ref[...], w_ref[...], _NT)
        g = g_ref[...]
        sg = jax.nn.sigmoid(g)
        dcat = dy * (g * sg)
        dyf_ref[...] = dcat[:, :D_FOX]
        dys_ref[...] = dcat[:, D_FOX:D_FOX + D_S5]
        dyr_ref[...] = dcat[:, D_FOX + D_S5:]
        cat = jnp.concatenate([yf_ref[...], ys_ref[...], yr_ref[...]], axis=-1)
        dg_ref[...] = dy * cat * (sg * (1.0 + g * (1.0 - sg)))

    full = _rows(TM, D_MODEL)
    f, s, r = _rows(TM, D_FOX), _rows(TM, D_S5), _rows(TM, D_RET)
    return _pallas(body, out_shape=(jax.ShapeDtypeStruct((L, D_FOX), F32), jax.ShapeDtypeStruct((L, D_S5), F32),
                                    jax.ShapeDtypeStruct((L, D_RET), F32), jax.ShapeDtypeStruct((L, D_MODEL), F32)),
                   grid=(L // TM,),
                   in_specs=[full, _whole((D_MODEL, D_MODEL)), f, s, r, _rows(TM, D_MODEL, O_GATE // D_MODEL)],
                   out_specs=(f, s, r, full), name="gate_out_bwd")(dxn, w, yf, ys, yr, proj)


def _final_loss(x, g, tgt):
    L = x.shape[0]

    def body(x_ref, g_ref, t_ref, loss_ref, dx_ref, dg_ref):
        xv, gv = x_ref[...], g_ref[...]
        r = lax.rsqrt(jnp.mean(xv * xv, axis=-1, keepdims=True) + EPS)
        err = xv * r * gv - t_ref[...]
        part = 0.5 * jnp.sum(jnp.mean(err * err, axis=-1, keepdims=True), axis=0, keepdims=True)
        dx, dg = _rms_bwd(xv, gv, err * (1.0 / D_MODEL))
        dx_ref[...] = dx

        @pl.when(pl.program_id(0) == 0)
        def _():
            loss_ref[...] = part
            dg_ref[...] = dg

        @pl.when(pl.program_id(0) != 0)
        def _():
            loss_ref[...] += part
            dg_ref[...] += dg

    full = _rows(TM, D_MODEL)
    return _pallas(body, out_shape=(jax.ShapeDtypeStruct((1, 1), F32), jax.ShapeDtypeStruct((L, D_MODEL), F32),
                                    jax.ShapeDtypeStruct((1, D_MODEL), F32)),
                   grid=(L // TM,), in_specs=[full, _whole((1, D_MODEL)), full],
                   out_specs=(_whole((1, 1)), full, _whole((1, D_MODEL))), name="final_loss")(x, g, tgt)


def _heads(a, h):
    return a.reshape(a.shape[0], h, HEAD_DIM).transpose(1, 0, 2)


def _unheads(a):
    return a.transpose(1, 0, 2).reshape(a.shape[1], a.shape[0] * a.shape[2])


def _block_diag(blocks):
    g, r, c = blocks.shape
    eye = jnp.eye(g, dtype=blocks.dtype)
    return (blocks[:, :, None, :] * eye[:, None, :, None]).reshape(g * r, g * c)


def _diag_blocks(m, g):
    r, c = m.shape[0] // g, m.shape[1] // g
    eye = jnp.eye(g, dtype=m.dtype)
    return jnp.sum(m.reshape(g, r, g, c) * eye[:, None, :, None], axis=2)


def _rope_tables(L):
    half = HEAD_DIM // 2
    freqs = ROPE_BASE ** (-jnp.arange(half, dtype=F32) / half)
    ang = jnp.arange(L, dtype=F32)[:, None] * freqs[None, :]
    cos, sin = jnp.cos(ang), jnp.sin(ang)
    cos_t = jnp.tile(jnp.concatenate([cos, cos], axis=-1), (1, RET_HEADS))
    sin_t = jnp.tile(jnp.concatenate([sin, -sin], axis=-1), (1, RET_HEADS))
    return cos_t, sin_t


def _s5_mats(p):
    lr, li, bbr, bbi = _s5_disc(p["a_re"], p["a_im"], p["ldt"], p["b_re"], p["b_im"])
    lam = jnp.concatenate([lr.reshape(8, 128), li.reshape(8, 128)], axis=0)
    g, s, ch = S5_GROUPS, S5_STATE, S5_GROUP_CH
    wb = jnp.concatenate([_block_diag(b.reshape(g, s, ch).transpose(0, 2, 1)) for b in (bbr, bbi)], axis=1)
    wc = jnp.concatenate([_block_diag(c.transpose(0, 2, 1)) for c in (p["c_re"], -p["c_im"])], axis=0)
    return lam, wb, wc


def _layer_fwd(x, p, rope):
    L = x.shape[0]
    cos_t, sin_t, ret_tabs = rope
    s = {"x": x}
    proj, h = _norm_inproj(x, p["norm_w"], p["w_in"])
    s["proj"], s["h"] = proj, h
    qa, ka, kat, vt = _fox_prep(proj, _fox_cumsum(proj, p["b_f"]))
    yf, lse = _fox_fwd(qa, ka, vt)
    s.update(qa=qa, ka=ka, kat=kat, lse=lse, yf=yf)
    lam, wb, wc = _s5_mats(p)
    u = proj[:, O_SU:O_SU + D_S5]
    bu = _mm(u, wb, name="s5_bu")
    xs = _s5_scan(bu.reshape(L, 16, 128), lam)
    ypre = _mm(xs.reshape(L, 2 * S5_CH), wc, tk=1024, name="s5_cx")
    ys = _s5_post(ypre, proj, p["d"], p["w_glu"])
    s.update(lam=lam, wb=wb, wc=wc, u=u, xs=xs, ypre=ypre, ys=ys)
    o_pre, yr, states = _ret_fwd(proj, cos_t, sin_t, ret_tabs, p["gn_w"])
    s.update(o_pre=o_pre, yr=yr, states=states)
    y, xn = _gate_out(yf, ys, yr, proj, x, p["w_out"])
    s["y"] = y
    return xn, s


def _layer_bwd(dxn, s, p, rope):
    L = dxn.shape[0]
    cos_t, sin_t, ret_tabs = rope
    g = {}
    proj = s["proj"]
    dyf, dys, dyr, dgate = _gate_out_bwd(dxn, p["w_out"], s["yf"], s["ys"], s["yr"], proj)
    g["w_out"] = _mm(s["y"], dxn, ta=True, name="dw_out")
    drq, drk, drv, dgn = _ret_bwd(proj, cos_t, sin_t, ret_tabs, p["gn_w"], s["o_pre"], dyr, s["states"])
    g["ret_gn_w"] = dgn.reshape(D_RET)
    dypre, du1, dd, dwglu = _s5_post_bwd(s["ypre"], proj, p["d"], p["w_glu"], dys)
    g["s5_d"], g["s5_w_glu"] = dd.reshape(D_S5), dwglu
    xs2 = s["xs"].reshape(L, 2 * S5_CH)
    dxs = _mm(dypre, s["wc"], tb=True, name="s5_dx")
    dwc = _mm(xs2, dypre, ta=True, tm=1024, name="s5_dwc")
    dc = [_diag_blocks(m, S5_GROUPS).transpose(0, 2, 1) for m in (dwc[:S5_CH], dwc[S5_CH:])]
    g["s5_c_re"], g["s5_c_im"] = dc[0], -dc[1]
    xprev = jnp.concatenate([jnp.zeros((1, 16, 128), F32), s["xs"][:-1]], axis=0)
    gs, dlam = _s5_scan_bwd(dxs.reshape(L, 16, 128), xprev, s["lam"])
    gs2 = gs.reshape(L, 2 * S5_CH)
    dsu = _mm(gs2, s["wb"], tb=True, add=du1, tk=1024, name="s5_du")
    dwb = _mm(s["u"], gs2, ta=True, name="s5_dwb")
    dbb = [_diag_blocks(m, S5_GROUPS).transpose(0, 2, 1).reshape(S5_GROUPS, S5_STATE * S5_GROUP_CH)
           for m in (dwb[:, :S5_CH], dwb[:, S5_CH:])]
    dar, dai, dldt, dbr, dbi = _s5_disc_bwd(p["a_re"], p["a_im"], p["ldt"], p["b_re"], p["b_im"],
                                            dlam[:8].reshape(S5_GROUPS, S5_STATE), dlam[8:].reshape(S5_GROUPS, S5_STATE),
                                            dbb[0], dbb[1])
    shp = (S5_GROUPS, S5_STATE, S5_GROUP_CH)
    g.update(s5_a_re=dar, s5_a_im=dai, s5_log_dt=dldt.reshape(S5_GROUPS), s5_b_re=dbr.reshape(shp), s5_b_im=dbi.reshape(shp))
    dqt, dkraw, dv = _fox_bwd(s["qa"], s["ka"], s["kat"], proj, dyf, s["yf"], s["lse"])
    dq, dk, dfl, dbf = _fox_post_bwd(dqt, dkraw, proj, p["b_f"])
    g["fox_b_f"] = dbf[0, :FOX_HEADS]
    dproj = jnp.concatenate([dgate, dq, dk, dv, dsu, drq, drk, drv, dfl], axis=-1)
    dx, dnw = _inproj_bwd_dx(dproj, p["w_in"], s["x"], p["norm_w"], dxn)
    g["norm_w"] = dnw.reshape(D_MODEL)
    g["w_in"] = _mm(s["h"], dproj, ta=True, tm=512, tn=D_INP, tk=256, name="dw_in")
    return dx, g


def _pad_w_in(w):
    z = jnp.zeros(w.shape[:-1] + (D_INP - O_FL - FOX_HEADS,), w.dtype)
    return jnp.concatenate([w[..., 2568:3592], w[..., 0:1536], w[..., 1544:2568], w[..., 1536:1544], z], axis=-1)


def _unpad_w_in(w):
    return jnp.concatenate([w[..., O_FQ:O_SU], w[..., O_FL:O_FL + FOX_HEADS], w[..., O_SU:O_FL], w[..., O_GATE:O_FQ]], axis=-1)


def _layer_params(l, w_in_p, w_out, w_glu, small):
    g, s, ch = S5_GROUPS, S5_STATE, S5_GROUP_CH
    return dict(
        norm_w=small["norm_w"][l][None], w_in=w_in_p[l], b_f=jnp.pad(small["fox_b_f"][l], (0, PAIR - FOX_HEADS))[None],
        a_re=small["s5_a_re"][l], a_im=small["s5_a_im"][l], ldt=small["s5_log_dt"][l][:, None],
        b_re=small["s5_b_re"][l].reshape(g, s * ch), b_im=small["s5_b_im"][l].reshape(g, s * ch),
        c_re=small["s5_c_re"][l], c_im=small["s5_c_im"][l], d=small["s5_d"][l][None], w_glu=w_glu[l],
        gn_w=small["ret_gn_w"][l][None], w_out=w_out[l])


def _step_grads(x, tgt, w_in_p, w_out, w_glu, small):
    L = x.shape[0]
    rope = _rope_tables(L) + (_ret_tables(min(TQ, L)),)
    params = [_layer_params(l, w_in_p, w_out, w_glu, small) for l in range(DEPTH)]
    saved = []
    for l in range(DEPTH):
        x, s = _layer_fwd(x, params[l], rope)
        saved.append(s)
    loss, dx, dfw = _final_loss(x, small["final_norm_w"][None], tgt)
    grads = [None] * DEPTH
    for l in reversed(range(DEPTH)):
        dx, grads[l] = _layer_bwd(dx, saved[l], params[l], rope)
    return loss, dx, grads, dfw.reshape(D_MODEL)


_MESH = pl.DeviceIdType.MESH
_ANY = pl.BlockSpec(memory_space=pl.ANY)


def _me_and_peers():
    x, y, c = lax.axis_index("x"), lax.axis_index("y"), lax.axis_index("c")
    flip = lambda a, bit: (1 - a) if bit else a
    peers = []
    for r in range(1, N_DEV):
        px, py, pc = flip(x, (r >> 2) & 1), flip(y, (r >> 1) & 1), flip(c, r & 1)
        peers.append(((px, py, pc), 4 * px + 2 * py + pc))
    return 4 * x + 2 * y + c, peers


def _all_gather(shards):
    n = len(shards)

    def body(*refs):
        srcs, dsts = refs[:n], refs[n:2 * n]
        send_sems, recv_sems, local_sems = refs[2 * n:]
        me, peers = _me_and_peers()
        own = [pltpu.make_async_copy(srcs[t], dsts[t].at[me], local_sems.at[t]) for t in range(n)]
        for cp in own:
            cp.start()
        for r, (dev, _) in enumerate(peers):
            for t in range(n):
                pltpu.make_async_remote_copy(srcs[t], dsts[t].at[me], send_sems.at[t, r], recv_sems.at[t, r],
                                             device_id=dev, device_id_type=_MESH).start()
        for r, (dev, idx) in enumerate(peers):
            for t in range(n):
                pltpu.make_async_remote_copy(srcs[t], dsts[t].at[idx], send_sems.at[t, r], recv_sems.at[t, r],
                                             device_id=dev, device_id_type=_MESH).wait()
        for cp in own:
            cp.wait()

    return _pallas(body, out_shape=tuple(jax.ShapeDtypeStruct((N_DEV,) + s.shape, s.dtype) for s in shards),
                   in_specs=[_ANY] * n, out_specs=tuple([_ANY] * n),
                   scratch_shapes=[pltpu.SemaphoreType.DMA((n, N_DEV - 1)), pltpu.SemaphoreType.DMA((n, N_DEV - 1)),
                                   pltpu.SemaphoreType.DMA((n,))],
                   name="all_gather_weights")(*shards)


def _exchange_grads(parts, small):
    n = len(parts)

    def body(*refs):
        srcs, dsts = refs[:n + 1], refs[n + 1:2 * n + 2]
        send_sems, recv_sems, local_sems = refs[2 * n + 2:]
        me, peers = _me_and_peers()

        def src(t, to):
            return srcs[t].at[to] if t < n else srcs[t]

        own = [pltpu.make_async_copy(src(t, me), dsts[t].at[me], local_sems.at[t]) for t in range(n + 1)]
        for cp in own:
            cp.start()
        for r, (dev, idx) in enumerate(peers):
            for t in range(n + 1):
                pltpu.make_async_remote_copy(src(t, idx), dsts[t].at[me], send_sems.at[t, r], recv_sems.at[t, r],
                                             device_id=dev, device_id_type=_MESH).start()
        for r, (dev, idx) in enumerate(peers):
            for t in range(n + 1):
                pltpu.make_async_remote_copy(src(t, idx), dsts[t].at[idx], send_sems.at[t, r], recv_sems.at[t, r],
                                             device_id=dev, device_id_type=_MESH).wait()
        for cp in own:
            cp.wait()

    outs = tuple(jax.ShapeDtypeStruct(p.shape, p.dtype) for p in parts) + (jax.ShapeDtypeStruct((N_DEV,) + small.shape, small.dtype),)
    return _pallas(body, out_shape=outs, in_specs=[_ANY] * (n + 1), out_specs=tuple([_ANY] * (n + 1)),
                   scratch_shapes=[pltpu.SemaphoreType.DMA((n + 1, N_DEV - 1)), pltpu.SemaphoreType.DMA((n + 1, N_DEV - 1)),
                                   pltpu.SemaphoreType.DMA((n + 1,))],
                   name="exchange_grads")(*parts, small)


def _adamw(parts, w, m, v, name):
    n, rows, cols = parts.shape
    tm = next(t for t in (256, 128, 64, 32, 16, 8) if rows % t == 0)

    def body(p_ref, w_ref, m_ref, v_ref, g_ref, d_ref, nm_ref, nv_ref):
        g = p_ref[0]
        for i in range(1, n):
            g = g + p_ref[i]
        nm = ADAM_B1 * m_ref[...] + (1.0 - ADAM_B1) * g
        nv = ADAM_B2 * v_ref[...] + (1.0 - ADAM_B2) * jnp.square(g)
        m_hat = nm / (1.0 - ADAM_B1 ** ADAM_STEP)
        v_hat = nv / (1.0 - ADAM_B2 ** ADAM_STEP)
        g_ref[...] = g
        d_ref[...] = -ADAM_LR * (m_hat / (jnp.sqrt(v_hat) + ADAM_EPS) + ADAM_WD * w_ref[...])
        nm_ref[...] = nm
        nv_ref[...] = nv

    row = pl.BlockSpec((tm, cols), lambda i: (i, 0))
    return _pallas(body, out_shape=(jax.ShapeDtypeStruct((rows, cols), F32),) * 4, grid=(rows // tm,),
                   in_specs=[pl.BlockSpec((n, tm, cols), lambda i: (0, i, 0)), row, row, row], out_specs=(row,) * 4,
                   name=name)(parts, w, m, v)


_SHARDED = ("w_in", "s5_w_glu", "w_out")
_WEIGHTS = ("norm_w", "w_in", "fox_b_f", "s5_a_re", "s5_a_im", "s5_b_re", "s5_b_im", "s5_c_re", "s5_c_im", "s5_d",
            "s5_log_dt", "s5_w_glu", "ret_gn_w", "w_out", "final_norm_w")
_SMALL = tuple(n for n in _WEIGHTS if n not in _SHARDED)
_LANES = 128


def _pack(arrs):
    flat = jnp.concatenate([a.reshape(-1) for a in arrs])
    rows = -(-flat.shape[0] // (_LANES * _LANES)) * _LANES
    return jnp.pad(flat, (0, rows * _LANES - flat.shape[0])).reshape(rows, _LANES)


def _unpack(packed, like):
    flat, out, off = packed.reshape(-1), [], 0
    for a in like:
        out.append(flat[off:off + a.size].reshape(a.shape))
        off += a.size
    return out


def _to_slots(g, axis):
    shp = g.shape[:axis] + (N_DEV, g.shape[axis] // N_DEV) + g.shape[axis + 1:]
    return jnp.moveaxis(g.reshape(shp), axis, 0)


def kernel(x, norm_w, w_in, fox_b_f, s5_a_re, s5_a_im, s5_b_re, s5_b_im, s5_c_re, s5_c_im, s5_d, s5_log_dt, s5_w_glu, ret_gn_w, w_out, final_norm_w, loss_target, m_norm_w, m_w_in, m_fox_b_f, m_s5_a_re, m_s5_a_im, m_s5_b_re, m_s5_b_im, m_s5_c_re, m_s5_c_im, m_s5_d, m_s5_log_dt, m_s5_w_glu, m_ret_gn_w, m_w_out, m_final_norm_w, v_norm_w, v_w_in, v_fox_b_f, v_s5_a_re, v_s5_a_im, v_s5_b_re, v_s5_b_im, v_s5_c_re, v_s5_c_im, v_s5_d, v_s5_log_dt, v_s5_w_glu, v_ret_gn_w, v_w_out, v_final_norm_w):
    w = dict(norm_w=norm_w, w_in=w_in, fox_b_f=fox_b_f, s5_a_re=s5_a_re, s5_a_im=s5_a_im, s5_b_re=s5_b_re, s5_b_im=s5_b_im,
             s5_c_re=s5_c_re, s5_c_im=s5_c_im, s5_d=s5_d, s5_log_dt=s5_log_dt, s5_w_glu=s5_w_glu, ret_gn_w=ret_gn_w,
             w_out=w_out, final_norm_w=final_norm_w)
    m = dict(norm_w=m_norm_w, w_in=m_w_in, fox_b_f=m_fox_b_f, s5_a_re=m_s5_a_re, s5_a_im=m_s5_a_im, s5_b_re=m_s5_b_re,
             s5_b_im=m_s5_b_im, s5_c_re=m_s5_c_re, s5_c_im=m_s5_c_im, s5_d=m_s5_d, s5_log_dt=m_s5_log_dt,
             s5_w_glu=m_s5_w_glu, ret_gn_w=m_ret_gn_w, w_out=m_w_out, final_norm_w=m_final_norm_w)
    v = dict(norm_w=v_norm_w, w_in=v_w_in, fox_b_f=v_fox_b_f, s5_a_re=v_s5_a_re, s5_a_im=v_s5_a_im, s5_b_re=v_s5_b_re,
             s5_b_im=v_s5_b_im, s5_c_re=v_s5_c_re, s5_c_im=v_s5_c_im, s5_d=v_s5_d, s5_log_dt=v_s5_log_dt,
             s5_w_glu=v_s5_w_glu, ret_gn_w=v_ret_gn_w, w_out=v_w_out, final_norm_w=v_final_norm_w)

    g_in, g_glu, g_out = _all_gather([w[n].astype(_MXU) for n in _SHARDED])
    w_in_p = _pad_w_in(jnp.moveaxis(g_in, 0, 2).reshape(DEPTH, D_MODEL, D_IN))
    w_glu_f = jnp.moveaxis(g_glu, 0, 1).reshape(DEPTH, D_S5, D_S5)
    w_out_f = jnp.moveaxis(g_out, 0, 1).reshape(DEPTH, D_MODEL, D_MODEL)

    small = {n: w[n] for n in _SMALL}
    loss, dx, grads, dfw = _step_grads(x[0], loss_target[0], w_in_p, w_out_f, w_glu_f, small)

    full = {n: jnp.stack([g[n] for g in grads]) for n in _WEIGHTS if n != "final_norm_w"}
    full["final_norm_w"] = dfw
    parts = [_to_slots(_unpad_w_in(full["w_in"]), 2), _to_slots(full["s5_w_glu"], 1), _to_slots(full["w_out"], 1)]
    r_in, r_glu, r_out, r_small = _exchange_grads(parts, _pack([full[n] for n in _SMALL]))

    res = {}
    for n, r in zip(_SHARDED, (r_in, r_glu, r_out)):
        cols = w[n].shape[-1]
        outs = _adamw(r.reshape(N_DEV, -1, cols), w[n].reshape(-1, cols), m[n].reshape(-1, cols), v[n].reshape(-1, cols),
                      "adamw_" + n)
        res[n] = [o.reshape(w[n].shape) for o in outs]
    small_w = [w[n] for n in _SMALL]
    outs = _adamw(r_small, _pack(small_w), _pack([m[n] for n in _SMALL]), _pack([v[n] for n in _SMALL]), "adamw_small")
    for k, o in enumerate(outs):
        for n, a in zip(_SMALL, _unpack(o, small_w)):
            res.setdefault(n, [None] * 4)[k] = a

    loss = lax.psum(loss[0, 0], ("x", "y", "c"))
    return (loss, dx[None], *[res[n][0] for n in _WEIGHTS], *[res[n][1] for n in _WEIGHTS],
            *[res[n][2] for n in _WEIGHTS], *[res[n][3] for n in _WEIGHTS])
```

```python
import math

import jax
import jax.numpy as jnp
from jax import lax
from jax.experimental import pallas as pl
from jax.experimental.pallas import tpu as pltpu

F32 = jnp.float32
_MXU = jnp.bfloat16
_HI = lax.Precision.HIGHEST

N_DEV = 8
DEPTH = 4
D_MODEL = 1024
HEAD_DIM = 64
D_FOX = 512
FOX_HEADS = 8
D_S5 = 256
S5_GROUPS = 16
S5_GROUP_CH = 16
S5_STATE = 64
S5_CH = S5_GROUPS * S5_STATE
D_RET = 256
RET_HEADS = 4
CHUNK = 64
ROPE_BASE = 10000.0
EPS = 1e-6
D_IN = 3592
D_INP = 3712
W_SHARD = D_IN // N_DEV
O_GATE, O_FQ, O_FK, O_FV, O_SU, O_RQ, O_RK, O_RV, O_FL = 0, 1024, 1536, 2048, 2560, 2816, 3072, 3328, 3584

ADAM_LR, ADAM_B1, ADAM_B2, ADAM_EPS, ADAM_WD, ADAM_STEP = 0.001, 0.9, 0.999, 1e-08, 0.01, 10

TM = 256
TQ = 512
TS = 256
NEG = -1e30
VMEM_BIG = 56 * 1024 * 1024


def _pallas(body, **kw):
    return pl.pallas_call(body, **kw)


def _whole(shape):
    n = len(shape)
    return pl.BlockSpec(shape, lambda *_: (0,) * n)


def _rows(tm, width, col=0):
    return pl.BlockSpec((tm, width), lambda i: (i, col))


def _dot(a, b, dims=(((1,), (0,)), ((), ()))):
    return lax.dot_general(a.astype(_MXU), b.astype(_MXU), dims, preferred_element_type=F32)


_NT = (((1,), (1,)), ((), ()))
_TN = (((0,), (0,)), ((), ()))


def _mm(a, b, *, ta=False, tb=False, add=None, tm=512, tn=512, tk=512, name):
    m, k = (a.shape[1], a.shape[0]) if ta else a.shape
    n = b.shape[0] if tb else b.shape[1]
    tm, tn, tk = min(tm, m), min(tn, n), min(tk, k)
    assert m % tm == 0 and n % tn == 0 and k % tk == 0, (name, m, n, k)
    dims = (((0 if ta else 1,), (1 if tb else 0,)), ((), ()))

    def body(*refs):
        a_ref, b_ref = refs[0], refs[1]
        o_ref = refs[-1]
        p = _dot(a_ref[...], b_ref[...], dims)
        kk = pl.program_id(2)

        @pl.when(kk == 0)
        def _():
            o_ref[...] = p if add is None else p + refs[2][...]

        @pl.when(kk != 0)
        def _():
            o_ref[...] += p

    a_spec = pl.BlockSpec((tk, tm), lambda i, j, kk: (kk, i)) if ta else pl.BlockSpec((tm, tk), lambda i, j, kk: (i, kk))
    b_spec = pl.BlockSpec((tn, tk), lambda i, j, kk: (j, kk)) if tb else pl.BlockSpec((tk, tn), lambda i, j, kk: (kk, j))
    o_spec = pl.BlockSpec((tm, tn), lambda i, j, kk: (i, j))
    ins, specs = [a, b], [a_spec, b_spec]
    if add is not None:
        ins.append(add)
        specs.append(o_spec)
    return _pallas(body, out_shape=jax.ShapeDtypeStruct((m, n), F32), grid=(m // tm, n // tn, k // tk),
                   in_specs=specs, out_specs=o_spec, name=name,
                   compiler_params=pltpu.CompilerParams(vmem_limit_bytes=VMEM_BIG))(*ins)


def _norm_inproj(x, g, w):
    L = x.shape[0]

    def body(x_ref, g_ref, w_ref, p_ref, h_ref):
        xv = x_ref[...]
        r = lax.rsqrt(jnp.mean(xv * xv, axis=-1, keepdims=True) + EPS)
        h = (xv * r * g_ref[...]).astype(_MXU)
        h_ref[...] = h
        p_ref[...] = _dot(h, w_ref[...])

    return _pallas(body, out_shape=(jax.ShapeDtypeStruct((L, D_INP), F32), jax.ShapeDtypeStruct((L, D_MODEL), _MXU)),
                   grid=(L // TM,), in_specs=[_rows(TM, D_MODEL), _whole((1, D_MODEL)), _whole((D_MODEL, D_INP))],
                   out_specs=(_rows(TM, D_INP), _rows(TM, D_MODEL)), name="norm_inproj",
                   compiler_params=pltpu.CompilerParams(vmem_limit_bytes=VMEM_BIG))(x, g, w)


def _rms_bwd(xv, g, dh):
    r = lax.rsqrt(jnp.mean(xv * xv, axis=-1, keepdims=True) + EPS)
    xh = xv * r
    dg = jnp.sum(dh * xh, axis=0, keepdims=True)
    dxh = dh * g
    dx = r * (dxh - xh * jnp.mean(dxh * xh, axis=-1, keepdims=True))
    return dx, dg


def _inproj_bwd_dx(dproj, w, x, g, dres):
    L = x.shape[0]

    def body(dp_ref, w_ref, x_ref, g_ref, dr_ref, dx_ref, dg_ref):
        dh = _dot(dp_ref[...], w_ref[...], _NT)
        dx, dg = _rms_bwd(x_ref[...], g_ref[...], dh)
        dx_ref[...] = dx + dr_ref[...]

        @pl.when(pl.program_id(0) == 0)
        def _():
            dg_ref[...] = dg

        @pl.when(pl.program_id(0) != 0)
        def _():
            dg_ref[...] += dg

    return _pallas(body, out_shape=(jax.ShapeDtypeStruct((L, D_MODEL), F32), jax.ShapeDtypeStruct((1, D_MODEL), F32)),
                   grid=(L // TM,),
                   in_specs=[_rows(TM, D_INP), _whole((D_MODEL, D_INP)), _rows(TM, D_MODEL), _whole((1, D_MODEL)),
                             _rows(TM, D_MODEL)],
                   out_specs=(_rows(TM, D_MODEL), _whole((1, D_MODEL))), name="inproj_bwd_dx",
                   compiler_params=pltpu.CompilerParams(vmem_limit_bytes=VMEM_BIG))(dproj, w, x, g, dres)


PAIR = 2 * HEAD_DIM
N_AUX = 3


def _own(shape, h):
    return lax.broadcasted_iota(jnp.int32, shape, len(shape) - 1) // HEAD_DIM == h


def _hi_dot(a, b):
    return jnp.dot(a, b, precision=_HI, preferred_element_type=F32)


def _tri(n, lower):
    r = lax.broadcasted_iota(jnp.int32, (n, n), 0)
    c = lax.broadcasted_iota(jnp.int32, (n, n), 1)
    return jnp.where(r >= c if lower else r <= c, 1.0, 0.0).astype(F32)


def _fox_cumsum(proj, b):
    L = proj.shape[0]

    def body(fl_ref, b_ref, c_ref, carry_sc):
        @pl.when(pl.program_id(0) == 0)
        def _():
            carry_sc[...] = jnp.zeros((1, PAIR), F32)

        lane = lax.broadcasted_iota(jnp.int32, (TM, PAIR), 1)
        lf = jnp.where(lane < FOX_HEADS, jax.nn.log_sigmoid(fl_ref[...] + b_ref[...]), 0.0)
        cs = _hi_dot(_tri(TM, True), lf) + carry_sc[...]
        c_ref[...] = cs
        carry_sc[...] = cs[TM - 1:TM, :]

    return _pallas(body, out_shape=jax.ShapeDtypeStruct((L, PAIR), F32), grid=(L // TM,),
                   in_specs=[_rows(TM, PAIR, O_FL // PAIR), _whole((1, PAIR))], out_specs=_rows(TM, PAIR),
                   scratch_shapes=[pltpu.VMEM((1, PAIR), F32)], name="fox_cumsum")(proj, b)


def _fox_prep(proj, c):
    L = proj.shape[0]

    def body(q_ref, k_ref, v_ref, c_ref, qa_ref, ka_ref, kat_ref, vt_ref):
        lane = lax.broadcasted_iota(jnp.int32, (TM, PAIR), 1)
        cv = c_ref[...]
        for p in range(FOX_HEADS // 2):
            cols = slice(PAIR * p, PAIR * (p + 1))
            q2, k2 = q_ref[:, cols], k_ref[:, cols]
            vt_ref[p] = v_ref[:, cols].T.astype(_MXU)
            for e in range(2):
                h = 2 * p + e
                own = lane // HEAD_DIM == e
                a = lane - (HEAD_DIM if e == 0 else 0)
                pick = (lax.broadcasted_iota(jnp.int32, (PAIR, PAIR), 0) == h).astype(F32)
                rest = _hi_dot(cv, pick)
                aux_q = jnp.where((a >= N_AUX) & (a < 2 * N_AUX), 1.0, 0.0)
                aux_k = jnp.where((a >= 0) & (a < N_AUX), 1.0, 0.0)
                for n in range(N_AUX):
                    part = rest.astype(_MXU).astype(F32)
                    rest = rest - part
                    aux_q = jnp.where(a == n, part, aux_q)
                    aux_k = jnp.where(a == N_AUX + n, -part, aux_k)
                ka = jnp.where(own, k2, aux_k)
                qa_ref[h] = jnp.where(own, q2 * (1.0 / math.sqrt(HEAD_DIM)), aux_q).astype(_MXU)
                ka_ref[h] = ka.astype(_MXU)
                kat_ref[h] = ka.T.astype(_MXU)

    hl = jax.ShapeDtypeStruct((FOX_HEADS, L, PAIR), _MXU)
    nat = lambda o: _rows(TM, D_FOX, o // D_FOX)
    return _pallas(
        body, out_shape=(hl, hl, jax.ShapeDtypeStruct((FOX_HEADS, PAIR, L), _MXU),
                         jax.ShapeDtypeStruct((FOX_HEADS // 2, PAIR, L), _MXU)),
        grid=(L // TM,), in_specs=[nat(O_FQ), nat(O_FK), nat(O_FV), _rows(TM, PAIR)],
        out_specs=(pl.BlockSpec((FOX_HEADS, TM, PAIR), lambda i: (0, i, 0)), pl.BlockSpec((FOX_HEADS, TM, PAIR), lambda i: (0, i, 0)),
                   pl.BlockSpec((FOX_HEADS, PAIR, TM), lambda i: (0, 0, i)), pl.BlockSpec((FOX_HEADS // 2, PAIR, TM), lambda i: (0, 0, i))),
        name="fox_prep")(proj, proj, proj, c)


def _key_le_query(tq):
    return lax.broadcasted_iota(jnp.int32, (tq, tq), 0) <= lax.broadcasted_iota(jnp.int32, (tq, tq), 1)


def _grid_ends(n0, n1):
    first = lambda: (pl.program_id(0) == 0) & (pl.program_id(1) == 0)
    last = lambda: (pl.program_id(0) == n0 - 1) & (pl.program_id(1) == n1 - 1)
    return first, last


def _fox_fwd(qa, ka, vt, ride=None):
    H, L, _ = qa.shape
    tq = min(TQ, L)
    nq = L // tq

    def body(qa_ref, ka_ref, vt_ref, o_ref, lse_ref, m_sc, l_sc, acc_sc):
        i = pl.program_id(1)
        m_sc[...] = jnp.full((2, 1, tq), NEG, F32)
        l_sc[...] = jnp.zeros((2, 1, tq), F32)
        acc_sc[...] = jnp.zeros((2, PAIR, tq), F32)

        def block(j, masked):
            keys = pl.ds(pl.multiple_of(j * tq, tq), tq)
            vt_blk = vt_ref[:, keys]
            for e in range(2):
                st = _dot(ka_ref[e, keys, :], qa_ref[e], _NT)
                if masked:
                    st = jnp.where(_key_le_query(tq), st, NEG)
                m_prev = m_sc[e]
                m_new = jnp.maximum(m_prev, jnp.max(st, axis=0, keepdims=True))
                alpha = jnp.exp(m_prev - m_new)
                pt = jnp.exp(st - m_new)
                l_sc[e] = alpha * l_sc[e] + jnp.sum(pt, axis=0, keepdims=True)
                acc_sc[e] = alpha * acc_sc[e] + _dot(vt_blk, pt)
                m_sc[e] = m_new

        def off_diagonal(j, carry):
            block(j, False)
            return carry

        lax.fori_loop(0, i, off_diagonal, 0)
        block(i, True)
        row = lax.broadcasted_iota(jnp.int32, (PAIR, tq), 0)
        ot = jnp.where(row < HEAD_DIM, acc_sc[0] / l_sc[0], acc_sc[1] / l_sc[1])
        o_ref[...] = ot.T
        for e in range(2):
            lse_ref[e] = m_sc[e] + jnp.log(l_sc[e])

    body, ex_in, ex_specs, ex_out, ex_sems = _riding(body, 3, 2, ride, *_grid_ends(H // 2, nq))
    res = _pallas(
        body, out_shape=(jax.ShapeDtypeStruct((L, D_FOX), F32), jax.ShapeDtypeStruct((H, 1, L), F32), *ex_out),
        grid=(H // 2, nq),
        in_specs=[pl.BlockSpec((2, tq, PAIR), lambda p, i: (p, i, 0)), pl.BlockSpec((2, L, PAIR), lambda p, i: (p, 0, 0)),
                  pl.BlockSpec((None, PAIR, L), lambda p, i: (p, 0, 0)), *ex_specs],
        out_specs=(pl.BlockSpec((tq, PAIR), lambda p, i: (i, p)), pl.BlockSpec((2, 1, tq), lambda p, i: (p, 0, i)),
                   *ex_specs),
        scratch_shapes=[pltpu.VMEM((2, 1, tq), F32), pltpu.VMEM((2, 1, tq), F32), pltpu.VMEM((2, PAIR, tq), F32), *ex_sems],
        name="fox_fwd" if ride is None else "fox_fwd_gather")(qa, ka, vt, *ex_in)
    return res[0], res[1], list(res[2:])


def _fox_bwd(qa, ka, kat, proj, do, o, lse, ride=None):
    H, L, _ = qa.shape
    tq = min(TQ, L)
    nq = L // tq

    def body(qa_ref, ka_ref, kat_ref, v_ref, do_ref, o_ref, lse_ref, dqt_ref, dk_ref, dv_ref, delta_sc, dk_sc, dv_sc):
        j = pl.program_id(1)

        @pl.when(j == 0)
        def _():
            head_rows = (lax.broadcasted_iota(jnp.int32, (8, PAIR), 1) // HEAD_DIM
                         == lax.broadcasted_iota(jnp.int32, (8, PAIR), 0)).astype(F32)
            delta_sc[...] = lax.dot_general(head_rows, do_ref[...] * o_ref[...], _NT, precision=_HI,
                                            preferred_element_type=F32)
            dqt_ref[...] = jnp.zeros((2, PAIR, L), F32)

        dk_sc[...] = jnp.zeros((2, tq, PAIR), F32)
        dv_sc[...] = jnp.zeros((tq, PAIR), F32)
        vb = v_ref[...]

        def block(i, masked):
            qs = pl.ds(pl.multiple_of(i * tq, tq), tq)
            dob = do_ref[qs, :]
            for e in range(2):
                own = _own((tq, PAIR), e)
                qh = qa_ref[e, qs, :]
                pt = jnp.exp(_dot(ka_ref[e], qh, _NT) - lse_ref[e, :, qs])
                if masked:
                    pt = jnp.where(_key_le_query(tq), pt, 0.0)
                dv_sc[...] += _dot(pt, jnp.where(own, dob, 0.0))
                dpt = _dot(jnp.where(own, vb, 0.0), dob, _NT)
                ds = (pt * (dpt - delta_sc[e:e + 1, qs])).astype(_MXU)
                dk_sc[e] += _dot(ds, qh)
                dqt_ref[e, :, qs] += _dot(kat_ref[e], ds)

        def off_diagonal(i, carry):
            block(i, False)
            return carry

        block(j, True)
        lax.fori_loop(j + 1, nq, off_diagonal, 0)
        dk_ref[...] = dk_sc[...]
        dv_ref[...] = dv_sc[...]

    nat = pl.BlockSpec((L, PAIR), lambda p, j: (0, p))
    body, ex_in, ex_specs, ex_out, ex_sems = _riding(body, 7, 3, ride, *_grid_ends(H // 2, nq))
    res = _pallas(
        body, out_shape=(jax.ShapeDtypeStruct((H, PAIR, L), F32), jax.ShapeDtypeStruct((H, L, PAIR), F32),
                         jax.ShapeDtypeStruct((L, D_FOX), F32), *ex_out),
        grid=(H // 2, nq),
        in_specs=[pl.BlockSpec((2, L, PAIR), lambda p, j: (p, 0, 0)), pl.BlockSpec((2, tq, PAIR), lambda p, j: (p, j, 0)),
                  pl.BlockSpec((2, PAIR, tq), lambda p, j: (p, 0, j)),
                  pl.BlockSpec((tq, PAIR), lambda p, j: (j, O_FV // PAIR + p)), nat, nat,
                  pl.BlockSpec((2, 1, L), lambda p, j: (p, 0, 0)), *ex_specs],
        out_specs=(pl.BlockSpec((2, PAIR, L), lambda p, j: (p, 0, 0)), pl.BlockSpec((2, tq, PAIR), lambda p, j: (p, j, 0)),
                   pl.BlockSpec((tq, PAIR), lambda p, j: (j, p)), *ex_specs),
        scratch_shapes=[pltpu.VMEM((8, L), F32), pltpu.VMEM((2, tq, PAIR), F32), pltpu.VMEM((tq, PAIR), F32), *ex_sems],
        name="fox_bwd" if ride is None else "fox_bwd_exchange",
        compiler_params=pltpu.CompilerParams(vmem_limit_bytes=VMEM_BIG))(qa, ka, kat, proj, do, o, lse, *ex_in)
    return res[0], res[1], res[2], list(res[3:])


def _fox_post_bwd(dqt, dkraw, proj, b):
    L = proj.shape[0]
    nb = L // TM

    def body(dqt_ref, dkr_ref, fl_ref, b_ref, dq_ref, dk_ref, dfl_ref, db_ref, carry_sc):
        first = pl.program_id(0) == 0

        @pl.when(first)
        def _():
            carry_sc[...] = jnp.zeros((1, PAIR), F32)

        lane = lax.broadcasted_iota(jnp.int32, (TM, PAIR), 1)
        rr = lax.broadcasted_iota(jnp.int32, (PAIR, PAIR), 0)
        cc = lax.broadcasted_iota(jnp.int32, (PAIR, PAIR), 1)
        dc = jnp.zeros((TM, PAIR), F32)
        for p in range(FOX_HEADS // 2):
            cols = slice(PAIR * p, PAIR * (p + 1))
            dqs = [dqt_ref[2 * p + e].T for e in range(2)]
            dks = [dkr_ref[2 * p + e] for e in range(2)]
            dq_ref[:, cols] = jnp.where(lane < HEAD_DIM, dqs[0], dqs[1]) * (1.0 / math.sqrt(HEAD_DIM))
            dk_ref[:, cols] = jnp.where(lane < HEAD_DIM, dks[0], dks[1])
            for e in range(2):
                base = HEAD_DIM if e == 0 else 0
                to_head = cc == 2 * p + e
                dc = dc + _hi_dot(dqs[e], jnp.where((rr == base) & to_head, 1.0, 0.0))
                dc = dc + _hi_dot(dks[e], jnp.where((rr == base + N_AUX) & to_head, -1.0, 0.0))
        rs = _hi_dot(_tri(TM, False), dc) + carry_sc[...]
        carry_sc[...] = rs[0:1, :]
        dfl = jnp.where(lane < FOX_HEADS, rs * jax.nn.sigmoid(-(fl_ref[...] + b_ref[...])), 0.0)
        dfl_ref[...] = dfl
        db = jnp.sum(dfl, axis=0, keepdims=True)

        @pl.when(first)
        def _():
            db_ref[...] = db

        @pl.when(jnp.logical_not(first))
        def _():
            db_ref[...] += db

    rev = lambda i: nb - 1 - i
    nat = pl.BlockSpec((TM, D_FOX), lambda i: (rev(i), 0))
    return _pallas(
        body, out_shape=(jax.ShapeDtypeStruct((L, D_FOX), F32),) * 2
        + (jax.ShapeDtypeStruct((L, PAIR), F32), jax.ShapeDtypeStruct((1, PAIR), F32)),
        grid=(nb,),
        in_specs=[pl.BlockSpec((FOX_HEADS, PAIR, TM), lambda i: (0, 0, rev(i))),
                  pl.BlockSpec((FOX_HEADS, TM, PAIR), lambda i: (0, rev(i), 0)),
                  pl.BlockSpec((TM, PAIR), lambda i: (rev(i), O_FL // PAIR)), _whole((1, PAIR))],
        out_specs=(nat, nat, pl.BlockSpec((TM, PAIR), lambda i: (rev(i), 0)), _whole((1, PAIR))),
        scratch_shapes=[pltpu.VMEM((1, PAIR), F32)], name="fox_post_bwd")(dqt, dkraw, proj, b)


def _s5_expand():
    r = lax.broadcasted_iota(jnp.int32, (S5_STATE, S5_STATE * S5_GROUP_CH), 0)
    c = lax.broadcasted_iota(jnp.int32, (S5_STATE, S5_STATE * S5_GROUP_CH), 1)
    return jnp.where(c // S5_GROUP_CH == r, 1.0, 0.0).astype(F32)


def _s5_disc_math(ar, ai, ldt, br, bi):
    dt = jnp.exp(ldt)
    mag = jnp.exp(ar * dt)
    lr = mag * jnp.cos(ai * dt)
    li = mag * jnp.sin(ai * dt)
    den = ar * ar + ai * ai
    fr = ((lr - 1.0) * ar + li * ai) / den
    fi = (li * ar - (lr - 1.0) * ai) / den
    e = _s5_expand()
    fre = jnp.dot(fr, e, precision=_HI, preferred_element_type=F32)
    fie = jnp.dot(fi, e, precision=_HI, preferred_element_type=F32)
    return lr, li, fre * br - fie * bi, fre * bi + fie * br


def _s5_disc(ar, ai, ldt, br, bi):
    def body(ar_ref, ai_ref, ldt_ref, br_ref, bi_ref, lr_ref, li_ref, bbr_ref, bbi_ref):
        lr, li, bbr, bbi = _s5_disc_math(ar_ref[...], ai_ref[...], ldt_ref[...], br_ref[...], bi_ref[...])
        lr_ref[...] = lr
        li_ref[...] = li
        bbr_ref[...] = bbr
        bbi_ref[...] = bbi

    gp = jax.ShapeDtypeStruct(ar.shape, F32)
    gb = jax.ShapeDtypeStruct(br.shape, F32)
    return _pallas(body, out_shape=(gp, gp, gb, gb), name="s5_disc")(ar, ai, ldt, br, bi)


def _s5_disc_bwd(ar, ai, ldt, br, bi, dlr, dli, dbbr, dbbi):
    def body(ar_ref, ai_ref, ldt_ref, br_ref, bi_ref, dlr_ref, dli_ref, dbbr_ref, dbbi_ref,
             dar_ref, dai_ref, dldt_ref, dbr_ref, dbi_ref):
        _, vjp = jax.vjp(_s5_disc_math, ar_ref[...], ai_ref[...], ldt_ref[...], br_ref[...], bi_ref[...])
        dar, dai, dldt, dbr, dbi = vjp((dlr_ref[...], dli_ref[...], dbbr_ref[...], dbbi_ref[...]))
        dar_ref[...] = dar
        dai_ref[...] = dai
        dldt_ref[...] = dldt
        dbr_ref[...] = dbr
        dbi_ref[...] = dbi

    gp = jax.ShapeDtypeStruct(ar.shape, F32)
    gb = jax.ShapeDtypeStruct(br.shape, F32)
    return _pallas(body, out_shape=(gp, gp, jax.ShapeDtypeStruct(ldt.shape, F32), gb, gb),
                   name="s5_disc_bwd")(ar, ai, ldt, br, bi, dlr, dli, dbbr, dbbi)


def _s5_scan(bu, lam):
    L = bu.shape[0]
    ts = min(TS, L)

    def body(b_ref, lam_ref, x_ref, c_sc):
        @pl.when(pl.program_id(0) == 0)
        def _():
            c_sc[...] = jnp.zeros((16, 128), F32)

        lr, li = lam_ref[0:8, :], lam_ref[8:16, :]

        def step(t, carry):
            xr, xi = carry
            nr = lr * xr - li * xi + b_ref[t, 0:8, :]
            ni = lr * xi + li * xr + b_ref[t, 8:16, :]
            x_ref[t, 0:8, :] = nr
            x_ref[t, 8:16, :] = ni
            return nr, ni

        xr, xi = lax.fori_loop(0, ts, step, (c_sc[0:8, :], c_sc[8:16, :]), unroll=8)
        c_sc[0:8, :] = xr
        c_sc[8:16, :] = xi

    blk = pl.BlockSpec((ts, 16, 128), lambda n: (n, 0, 0))
    return _pallas(body, out_shape=jax.ShapeDtypeStruct((L, 16, 128), F32), grid=(L // ts,),
                   in_specs=[blk, _whole((16, 128))], out_specs=blk,
                   scratch_shapes=[pltpu.VMEM((16, 128), F32)], name="s5_scan")(bu, lam)


def _s5_scan_bwd(dx, xprev, lam):
    L = dx.shape[0]
    ts = min(TS, L)
    nb = L // ts

    def body(dx_ref, xp_ref, lam_ref, g_ref, dlam_ref, c_sc):
        @pl.when(pl.program_id(0) == 0)
        def _():
            c_sc[...] = jnp.zeros((16, 128), F32)
            dlam_ref[...] = jnp.zeros((16, 128), F32)

        lr, li = lam_ref[0:8, :], lam_ref[8:16, :]

        def step(n, carry):
            gr, gi, ar, ai = carry
            t = ts - 1 - n
            nr = dx_ref[t, 0:8, :] + lr * gr + li * gi
            ni = dx_ref[t, 8:16, :] - li * gr + lr * gi
            g_ref[t, 0:8, :] = nr
            g_ref[t, 8:16, :] = ni
            pr, pi = xp_ref[t, 0:8, :], xp_ref[t, 8:16, :]
            return nr, ni, ar + nr * pr + ni * pi, ai - nr * pi + ni * pr

        z = jnp.zeros((8, 128), F32)
        gr, gi, ar, ai = lax.fori_loop(0, ts, step, (c_sc[0:8, :], c_sc[8:16, :], z, z), unroll=8)
        c_sc[0:8, :] = gr
        c_sc[8:16, :] = gi
        dlam_ref[0:8, :] += ar
        dlam_ref[8:16, :] += ai

    blk = pl.BlockSpec((ts, 16, 128), lambda n: (nb - 1 - n, 0, 0))
    return _pallas(body, out_shape=(jax.ShapeDtypeStruct((L, 16, 128), F32), jax.ShapeDtypeStruct((16, 128), F32)),
                   grid=(nb,), in_specs=[blk, blk, _whole((16, 128))], out_specs=(blk, _whole((16, 128))),
                   scratch_shapes=[pltpu.VMEM((16, 128), F32)], name="s5_scan_bwd")(dx, xprev, lam)


def _s5_post(ypre, proj, d, w):
    L = ypre.shape[0]

    def body(y_ref, u_ref, d_ref, w_ref, o_ref):
        y1 = jax.nn.gelu(y_ref[...] + d_ref[...] * u_ref[...])
        o_ref[...] = y1 * jax.nn.sigmoid(_dot(y1, w_ref[...]))

    return _pallas(body, out_shape=jax.ShapeDtypeStruct((L, D_S5), F32), grid=(L // TM,),
                   in_specs=[_rows(TM, D_S5), _rows(TM, D_S5, O_SU // D_S5), _whole((1, D_S5)), _whole((D_S5, D_S5))],
                   out_specs=_rows(TM, D_S5), name="s5_post")(ypre, proj, d, w)


def _s5_post_bwd(ypre, proj, d, w, dout):
    L = ypre.shape[0]

    def body(y_ref, u_ref, d_ref, w_ref, do_ref, dy_ref, du_ref, dd_ref, dw_ref):
        u, dv, dout_v = u_ref[...], d_ref[...], do_ref[...]
        y1, gelu_vjp = jax.vjp(jax.nn.gelu, y_ref[...] + dv * u)
        sg = jax.nn.sigmoid(_dot(y1, w_ref[...]))
        dz = dout_v * y1 * sg * (1.0 - sg)
        dy1 = dout_v * sg + _dot(dz, w_ref[...], _NT)
        dy, = gelu_vjp(dy1)
        dy_ref[...] = dy
        du_ref[...] = dy * dv
        dd = jnp.sum(dy * u, axis=0, keepdims=True)
        dw = _dot(y1, dz, _TN)

        @pl.when(pl.program_id(0) == 0)
        def _():
            dd_ref[...] = dd
            dw_ref[...] = dw

        @pl.when(pl.program_id(0) != 0)
        def _():
            dd_ref[...] += dd
            dw_ref[...] += dw

    row = _rows(TM, D_S5)
    return _pallas(body, out_shape=(jax.ShapeDtypeStruct((L, D_S5), F32),) * 2
                   + (jax.ShapeDtypeStruct((1, D_S5), F32), jax.ShapeDtypeStruct((D_S5, D_S5), F32)),
                   grid=(L // TM,),
                   in_specs=[row, _rows(TM, D_S5, O_SU // D_S5), _whole((1, D_S5)), _whole((D_S5, D_S5)), row],
                   out_specs=(row, row, _whole((1, D_S5)), _whole((D_S5, D_S5))), name="s5_post_bwd")(ypre, proj, d, w, dout)


def _rot(z, cos, sin):
    lane = lax.broadcasted_iota(jnp.int32, z.shape, 1)
    zs = z * sin
    half = HEAD_DIM // 2
    return z * cos + jnp.where(lane % HEAD_DIM < half, pltpu.roll(zs, PAIR - half, 1), pltpu.roll(zs, half, 1))


def _head_avg():
    r = lax.broadcasted_iota(jnp.int32, (PAIR, PAIR), 0) // HEAD_DIM
    c = lax.broadcasted_iota(jnp.int32, (PAIR, PAIR), 1) // HEAD_DIM
    return jnp.where(r == c, 1.0 / HEAD_DIM, 0.0).astype(F32)


def _ret_tables(tq):
    lg = jnp.log1p(-(2.0 ** (-5.0 - jnp.arange(RET_HEADS, dtype=F32))))
    scale = 1.0 / math.sqrt(HEAD_DIM)
    pos = jnp.arange(tq)
    n = pos.astype(F32)
    dist = jnp.abs(n[:, None] - n[None, :])
    ok = (pos[None, :] // CHUNK) <= (pos[:, None] // CHUNK)
    w = jnp.where(ok[None], scale * jnp.exp(lg[:, None, None] * dist[None]), 0.0)
    lgl = jnp.repeat(lg, HEAD_DIM)
    dq_tab = scale * jnp.exp(lgl[None, :] * (n[:, None] + 1.0))
    dk_tab = jnp.exp(lgl[None, :] * (tq - 1.0 - n[:, None]))
    blk = jnp.arange(PAIR) // HEAD_DIM
    bd = (blk[:, None] == blk[None, :]).astype(F32)
    gbd = bd[None] * jnp.exp(lgl.reshape(RET_HEADS // 2, PAIR)[:, :, None] * tq)
    return dict(w=w, wt=w.transpose(0, 2, 1), dq=dq_tab, dk=dk_tab, gbd=gbd, bd=bd)


def _ret_specs(tq, nq, rev):
    blk = (lambda i: nq - 1 - i) if rev else (lambda i: i)
    col = lambda o: pl.BlockSpec((tq, PAIR), lambda p, i: (blk(i), o // PAIR + p))
    return dict(
        rq=col(O_RQ), rk=col(O_RK), rv=col(O_RV), nat=col(0),
        w=pl.BlockSpec((2, tq, tq), lambda p, i: (p, 0, 0)), tab=pl.BlockSpec((tq, PAIR), lambda p, i: (0, p)),
        gbd=pl.BlockSpec((None, PAIR, PAIR), lambda p, i: (p, 0, 0)), bd=pl.BlockSpec((PAIR, PAIR), lambda p, i: (0, 0)),
        gn=pl.BlockSpec((1, PAIR), lambda p, i: (0, p)),
        st=pl.BlockSpec((None, None, PAIR, PAIR), lambda p, i: (p, blk(i), 0, 0)))


def _ret_fwd(proj, cos_t, sin_t, tabs, gn):
    L = proj.shape[0]
    tq = tabs["w"].shape[1]
    nq = L // tq

    def body(rq_ref, rk_ref, rv_ref, cos_ref, sin_ref, w_ref, dqt_ref, dkt_ref, gbd_ref, bd_ref, gn_ref,
             o_ref, y_ref, st_ref, s_sc):
        @pl.when(pl.program_id(1) == 0)
        def _():
            s_sc[...] = jnp.zeros((PAIR, PAIR), F32)

        state = s_sc[...]
        st_ref[...] = state
        cos, sin = cos_ref[...], sin_ref[...]
        q2, k2, v2 = _rot(rq_ref[...], cos, sin), _rot(rk_ref[...], cos, sin), rv_ref[...]
        o = _dot(q2 * dqt_ref[...], state)
        for h in range(2):
            own = _own((tq, PAIR), h)
            a = _dot(jnp.where(own, q2, 0.0), k2, _NT) * w_ref[h]
            o = o + _dot(a, jnp.where(own, v2, 0.0))
        s_sc[...] = gbd_ref[...] * state + bd_ref[...] * _dot(k2 * dkt_ref[...], v2, _TN)
        o_ref[...] = o
        avg = _head_avg()
        oc = o - _hi_dot(o, avg)
        y_ref[...] = oc * lax.rsqrt(_hi_dot(oc * oc, avg) + EPS) * gn_ref[...]

    sp = _ret_specs(tq, nq, False)
    nat = jax.ShapeDtypeStruct((L, D_RET), F32)
    return _pallas(
        body, out_shape=(nat, nat, jax.ShapeDtypeStruct((RET_HEADS // 2, nq, PAIR, PAIR), F32)), grid=(RET_HEADS // 2, nq),
        in_specs=[sp["rq"], sp["rk"], sp["rv"], sp["nat"], sp["nat"], sp["w"], sp["tab"], sp["tab"], sp["gbd"], sp["bd"],
                  sp["gn"]],
        out_specs=(sp["nat"], sp["nat"], sp["st"]), scratch_shapes=[pltpu.VMEM((PAIR, PAIR), F32)],
        name="ret_fwd")(proj, proj, proj, cos_t, sin_t, tabs["w"], tabs["dq"], tabs["dk"], tabs["gbd"], tabs["bd"], gn)


def _ret_bwd(proj, cos_t, sin_t, tabs, gn, o_pre, dy, states):
    L = proj.shape[0]
    tq = tabs["w"].shape[1]
    nq = L // tq

    def body(rq_ref, rk_ref, rv_ref, cos_ref, sin_ref, w_ref, wt_ref, dqt_ref, dkt_ref, gbd_ref, bd_ref, gn_ref,
             o_ref, dy_ref, st_ref, drq_ref, drk_ref, drv_ref, dgn_ref, g_sc):
        first = pl.program_id(1) == 0

        @pl.when(first)
        def _():
            g_sc[...] = jnp.zeros((PAIR, PAIR), F32)

        cos, sin = cos_ref[...], sin_ref[...]
        q2, k2, v2 = _rot(rq_ref[...], cos, sin), _rot(rk_ref[...], cos, sin), rv_ref[...]
        avg = _head_avg()
        ov, dyv = o_ref[...], dy_ref[...]
        oc = ov - _hi_dot(ov, avg)
        r = lax.rsqrt(_hi_dot(oc * oc, avg) + EPS)
        oh = oc * r
        dgn = jnp.sum(dyv * oh, axis=0, keepdims=True)
        doh = dyv * gn_ref[...]
        do = r * (doh - _hi_dot(doh, avg) - oh * _hi_dot(doh * oh, avg))
        state, g = st_ref[...], g_sc[...]
        dqt, dkt = dqt_ref[...], dkt_ref[...]
        dq = _dot(do, state, _NT) * dqt
        dk = _dot(v2, g, _NT) * dkt
        dv = _dot(k2 * dkt, g)
        g_sc[...] = gbd_ref[...] * g + bd_ref[...] * _dot(q2 * dqt, do, _TN)
        for h in range(2):
            own = _own((tq, PAIR), h)
            qm, dom = jnp.where(own, q2, 0.0), jnp.where(own, do, 0.0)
            dv = dv + _dot(_dot(k2, qm, _NT) * wt_ref[h], dom)
            dq = dq + _dot(_dot(dom, v2, _NT) * w_ref[h], jnp.where(own, k2, 0.0))
            dk = dk + _dot(_dot(v2, dom, _NT) * wt_ref[h], qm)
        drq_ref[...] = _rot(dq, cos, -sin)
        drk_ref[...] = _rot(dk, cos, -sin)
        drv_ref[...] = dv

        @pl.when(first)
        def _():
            dgn_ref[...] = dgn

        @pl.when(jnp.logical_not(first))
        def _():
            dgn_ref[...] += dgn

    sp = _ret_specs(tq, nq, True)
    nat = jax.ShapeDtypeStruct((L, D_RET), F32)
    return _pallas(
        body, out_shape=(nat, nat, nat, jax.ShapeDtypeStruct((1, D_RET), F32)), grid=(RET_HEADS // 2, nq),
        in_specs=[sp["rq"], sp["rk"], sp["rv"], sp["nat"], sp["nat"], sp["w"], sp["w"], sp["tab"], sp["tab"], sp["gbd"],
                  sp["bd"], sp["gn"], sp["nat"], sp["nat"], sp["st"]],
        out_specs=(sp["nat"], sp["nat"], sp["nat"], sp["gn"]), scratch_shapes=[pltpu.VMEM((PAIR, PAIR), F32)],
        name="ret_bwd")(proj, proj, proj, cos_t, sin_t, tabs["w"], tabs["wt"], tabs["dq"], tabs["dk"], tabs["gbd"],
                        tabs["bd"], gn, o_pre, dy, states)


def _gate_out(yf, ys, yr, proj, x, w):
    L = x.shape[0]

    def body(yf_ref, ys_ref, yr_ref, g_ref, x_ref, w_ref, y_ref, xn_ref):
        cat = jnp.concatenate([yf_ref[...], ys_ref[...], yr_ref[...]], axis=-1)
        y = cat * jax.nn.silu(g_ref[...])
        y_ref[...] = y
        xn_ref[...] = x_ref[...] + _dot(y, w_ref[...])

    full = _rows(TM, D_MODEL)
    return _pallas(body, out_shape=(jax.ShapeDtypeStruct((L, D_MODEL), F32),) * 2, grid=(L // TM,),
                   in_specs=[_rows(TM, D_FOX), _rows(TM, D_S5), _rows(TM, D_RET), _rows(TM, D_MODEL, O_GATE // D_MODEL),
                             full, _whole((D_MODEL, D_MODEL))],
                   out_specs=(full, full), name="gate_out")(yf, ys, yr, proj, x, w)


def _gate_out_bwd(dxn, w, yf, ys, yr, proj):
    L = dxn.shape[0]

    def body(dx_ref, w_ref, yf_ref, ys_ref, yr_ref, g_ref, dyf_ref, dys_ref, dyr_ref, dg_ref):
        dy = _dot(dx_ref[...], w_ref[...], _NT)
        g = g_ref[...]
        sg = jax.nn.sigmoid(g)
        dcat = dy * (g * sg)
        dyf_ref[...] = dcat[:, :D_FOX]
        dys_ref[...] = dcat[:, D_FOX:D_FOX + D_S5]
        dyr_ref[...] = dcat[:, D_FOX + D_S5:]
        cat = jnp.concatenate([yf_ref[...], ys_ref[...], yr_ref[...]], axis=-1)
        dg_ref[...] = dy * cat * (sg * (1.0 + g * (1.0 - sg)))

    full = _rows(TM, D_MODEL)
    f, s, r = _rows(TM, D_FOX), _rows(TM, D_S5), _rows(TM, D_RET)
    return _pallas(body, out_shape=(jax.ShapeDtypeStruct((L, D_FOX), F32), jax.ShapeDtypeStruct((L, D_S5), F32),
                                    jax.ShapeDtypeStruct((L, D_RET), F32), jax.ShapeDtypeStruct((L, D_MODEL), F32)),
                   grid=(L // TM,),
                   in_specs=[full, _whole((D_MODEL, D_MODEL)), f, s, r, _rows(TM, D_MODEL, O_GATE // D_MODEL)],
                   out_specs=(f, s, r, full), name="gate_out_bwd")(dxn, w, yf, ys, yr, proj)


def _final_loss(x, g, tgt):
    L = x.shape[0]

    def body(x_ref, g_ref, t_ref, loss_ref, dx_ref, dg_ref):
        xv, gv = x_ref[...], g_ref[...]
        r = lax.rsqrt(jnp.mean(xv * xv, axis=-1, keepdims=True) + EPS)
        err = xv * r * gv - t_ref[...]
        part = 0.5 * jnp.sum(jnp.mean(err * err, axis=-1, keepdims=True), axis=0, keepdims=True)
        dx, dg = _rms_bwd(xv, gv, err * (1.0 / D_MODEL))
        dx_ref[...] = dx

        @pl.when(pl.program_id(0) == 0)
        def _():
            loss_ref[...] = part
            dg_ref[...] = dg

        @pl.when(pl.program_id(0) != 0)
        def _():
            loss_ref[...] += part
            dg_ref[...] += dg

    full = _rows(TM, D_MODEL)
    return _pallas(body, out_shape=(jax.ShapeDtypeStruct((1, 1), F32), jax.ShapeDtypeStruct((L, D_MODEL), F32),
                                    jax.ShapeDtypeStruct((1, D_MODEL), F32)),
                   grid=(L // TM,), in_specs=[full, _whole((1, D_MODEL)), full],
                   out_specs=(_whole((1, 1)), full, _whole((1, D_MODEL))), name="final_loss")(x, g, tgt)


def _heads(a, h):
    return a.reshape(a.shape[0], h, HEAD_DIM).transpose(1, 0, 2)


def _unheads(a):
    return a.transpose(1, 0, 2).reshape(a.shape[1], a.shape[0] * a.shape[2])


def _block_diag(blocks):
    g, r, c = blocks.shape
    eye = jnp.eye(g, dtype=blocks.dtype)
    return (blocks[:, :, None, :] * eye[:, None, :, None]).reshape(g * r, g * c)


def _diag_blocks(m, g):
    r, c = m.shape[0] // g, m.shape[1] // g
    eye = jnp.eye(g, dtype=m.dtype)
    return jnp.sum(m.reshape(g, r, g, c) * eye[:, None, :, None], axis=2)


def _rope_tables(L):
    half = HEAD_DIM // 2
    freqs = ROPE_BASE ** (-jnp.arange(half, dtype=F32) / half)
    ang = jnp.arange(L, dtype=F32)[:, None] * freqs[None, :]
    cos, sin = jnp.cos(ang), jnp.sin(ang)
    cos_t = jnp.tile(jnp.concatenate([cos, cos], axis=-1), (1, RET_HEADS))
    sin_t = jnp.tile(jnp.concatenate([sin, -sin], axis=-1), (1, RET_HEADS))
    return cos_t, sin_t


def _s5_mats(p):
    lr, li, bbr, bbi = _s5_disc(p["a_re"], p["a_im"], p["ldt"], p["b_re"], p["b_im"])
    lam = jnp.concatenate([lr.reshape(8, 128), li.reshape(8, 128)], axis=0)
    g, s, ch = S5_GROUPS, S5_STATE, S5_GROUP_CH
    wb = jnp.concatenate([_block_diag(b.reshape(g, s, ch).transpose(0, 2, 1)) for b in (bbr, bbi)], axis=1)
    wc = jnp.concatenate([_block_diag(c.transpose(0, 2, 1)) for c in (p["c_re"], -p["c_im"])], axis=0)
    return lam, wb, wc


def _layer_fwd(x, p, rope, ride=None):
    L = x.shape[0]
    cos_t, sin_t, ret_tabs = rope
    s = {"x": x}
    proj, h = _norm_inproj(x, p["norm_w"], p["w_in"])
    s["proj"], s["h"] = proj, h
    qa, ka, kat, vt = _fox_prep(proj, _fox_cumsum(proj, p["b_f"]))
    yf, lse, landed = _fox_fwd(qa, ka, vt, ride)
    s.update(qa=qa, ka=ka, kat=kat, lse=lse, yf=yf)
    lam, wb, wc = _s5_mats(p)
    u = proj[:, O_SU:O_SU + D_S5]
    bu = _mm(u, wb, name="s5_bu")
    xs = _s5_scan(bu.reshape(L, 16, 128), lam)
    ypre = _mm(xs.reshape(L, 2 * S5_CH), wc, tk=1024, name="s5_cx")
    ys = _s5_post(ypre, proj, p["d"], p["w_glu"])
    s.update(lam=lam, wb=wb, wc=wc, u=u, xs=xs, ypre=ypre, ys=ys)
    o_pre, yr, states = _ret_fwd(proj, cos_t, sin_t, ret_tabs, p["gn_w"])
    s.update(o_pre=o_pre, yr=yr, states=states)
    y, xn = _gate_out(yf, ys, yr, proj, x, p["w_out"])
    s["y"] = y
    return xn, s, landed


def _layer_bwd(dxn, s, p, rope, ride=None):
    L = dxn.shape[0]
    cos_t, sin_t, ret_tabs = rope
    g = {}
    proj = s["proj"]
    dyf, dys, dyr, dgate = _gate_out_bwd(dxn, p["w_out"], s["yf"], s["ys"], s["yr"], proj)
    g["w_out"] = _mm(s["y"], dxn, ta=True, name="dw_out")
    drq, drk, drv, dgn = _ret_bwd(proj, cos_t, sin_t, ret_tabs, p["gn_w"], s["o_pre"], dyr, s["states"])
    g["ret_gn_w"] = dgn.reshape(D_RET)
    dypre, du1, dd, dwglu = _s5_post_bwd(s["ypre"], proj, p["d"], p["w_glu"], dys)
    g["s5_d"], g["s5_w_glu"] = dd.reshape(D_S5), dwglu
    xs2 = s["xs"].reshape(L, 2 * S5_CH)
    dxs = _mm(dypre, s["wc"], tb=True, name="s5_dx")
    dwc = _mm(xs2, dypre, ta=True, tm=1024, name="s5_dwc")
    dc = [_diag_blocks(m, S5_GROUPS).transpose(0, 2, 1) for m in (dwc[:S5_CH], dwc[S5_CH:])]
    g["s5_c_re"], g["s5_c_im"] = dc[0], -dc[1]
    xprev = jnp.concatenate([jnp.zeros((1, 16, 128), F32), s["xs"][:-1]], axis=0)
    gs, dlam = _s5_scan_bwd(dxs.reshape(L, 16, 128), xprev, s["lam"])
    gs2 = gs.reshape(L, 2 * S5_CH)
    dsu = _mm(gs2, s["wb"], tb=True, add=du1, tk=1024, name="s5_du")
    dwb = _mm(s["u"], gs2, ta=True, name="s5_dwb")
    dbb = [_diag_blocks(m, S5_GROUPS).transpose(0, 2, 1).reshape(S5_GROUPS, S5_STATE * S5_GROUP_CH)
           for m in (dwb[:, :S5_CH], dwb[:, S5_CH:])]
    dar, dai, dldt, dbr, dbi = _s5_disc_bwd(p["a_re"], p["a_im"], p["ldt"], p["b_re"], p["b_im"],
                                            dlam[:8].reshape(S5_GROUPS, S5_STATE), dlam[8:].reshape(S5_GROUPS, S5_STATE),
                                            dbb[0], dbb[1])
    shp = (S5_GROUPS, S5_STATE, S5_GROUP_CH)
    g.update(s5_a_re=dar, s5_a_im=dai, s5_log_dt=dldt.reshape(S5_GROUPS), s5_b_re=dbr.reshape(shp), s5_b_im=dbi.reshape(shp))
    dqt, dkraw, dv, landed = _fox_bwd(s["qa"], s["ka"], s["kat"], proj, dyf, s["yf"], s["lse"], ride)
    dq, dk, dfl, dbf = _fox_post_bwd(dqt, dkraw, proj, p["b_f"])
    g["fox_b_f"] = dbf[0, :FOX_HEADS]
    dproj = jnp.concatenate([dgate, dq, dk, dv, dsu, drq, drk, drv, dfl], axis=-1)
    dx, dnw = _inproj_bwd_dx(dproj, p["w_in"], s["x"], p["norm_w"], dxn)
    g["norm_w"] = dnw.reshape(D_MODEL)
    g["w_in"] = _mm(s["h"], dproj, ta=True, tm=512, tn=D_INP, tk=256, name="dw_in")
    return dx, g, landed


def _pad_w_in(w):
    z = jnp.zeros(w.shape[:-1] + (D_INP - O_FL - FOX_HEADS,), w.dtype)
    return jnp.concatenate([w[..., 2568:3592], w[..., 0:1536], w[..., 1544:2568], w[..., 1536:1544], z], axis=-1)


def _unpad_w_in(w):
    return jnp.concatenate([w[..., O_FQ:O_SU], w[..., O_FL:O_FL + FOX_HEADS], w[..., O_SU:O_FL], w[..., O_GATE:O_FQ]], axis=-1)


def _layer_params(l, w_in_p, w_glu, w_out, small):
    g, s, ch = S5_GROUPS, S5_STATE, S5_GROUP_CH
    return dict(
        norm_w=small["norm_w"][l][None], w_in=w_in_p, b_f=jnp.pad(small["fox_b_f"][l], (0, PAIR - FOX_HEADS))[None],
        a_re=small["s5_a_re"][l], a_im=small["s5_a_im"][l], ldt=small["s5_log_dt"][l][:, None],
        b_re=small["s5_b_re"][l].reshape(g, s * ch), b_im=small["s5_b_im"][l].reshape(g, s * ch),
        c_re=small["s5_c_re"][l], c_im=small["s5_c_im"][l], d=small["s5_d"][l][None], w_glu=w_glu,
        gn_w=small["ret_gn_w"][l][None], w_out=w_out)


_SHARDED = ("w_in", "s5_w_glu", "w_out")
_WIRE = jnp.bfloat16


def _gathered_weights(g_in, g_glu, g_out):
    return (_pad_w_in(jnp.moveaxis(g_in, 0, 1).reshape(D_MODEL, D_IN)), g_glu.reshape(D_S5, D_S5),
            g_out.reshape(D_MODEL, D_MODEL))


def _grad_slots(g):
    w_in = _unpad_w_in(g["w_in"]).reshape(D_MODEL, N_DEV, W_SHARD)
    return [jnp.moveaxis(w_in, 1, 0).astype(_WIRE), g["s5_w_glu"].reshape(N_DEV, D_S5 // N_DEV, D_S5).astype(_WIRE),
            g["w_out"].reshape(N_DEV, D_MODEL // N_DEV, D_MODEL).astype(_WIRE)]


def _step_grads(x, tgt, small, full=None, shards=None):
    L = x.shape[0]
    rope = _rope_tables(L) + (_ret_tables(min(TQ, L)),)
    gather = [False] * len(_SHARDED)
    if shards is not None:
        nxt = _gathered_weights(*_exchange([s[0] for s in shards], gather, "gather_layer0"))
    saved, params = [], []
    for l in range(DEPTH):
        weights = nxt if shards is not None else tuple(f[l] for f in full)
        ride = ([s[l + 1] for s in shards], gather) if shards is not None and l + 1 < DEPTH else None
        params.append(_layer_params(l, *weights, small))
        x, s, landed = _layer_fwd(x, params[l], rope, ride)
        if ride is not None:
            nxt = _gathered_weights(*landed)
        saved.append(s)
    loss, dx, dfw = _final_loss(x, small["final_norm_w"][None], tgt)
    grads, partials, waiting = [None] * DEPTH, [None] * DEPTH, None
    scatter = [True] * len(_SHARDED)
    for l in reversed(range(DEPTH)):
        dx, grads[l], landed = _layer_bwd(dx, saved[l], params[l], rope, (waiting, scatter) if waiting is not None else None)
        if waiting is not None:
            partials[l + 1] = landed
        if shards is not None:
            waiting = _grad_slots(grads[l])
    dfw = dfw.reshape(D_MODEL)
    if shards is None:
        return loss, dx, grads, dfw
    small_g = _pack([dfw if n == "final_norm_w" else jnp.stack([g[n] for g in grads]) for n in _SMALL]).astype(_WIRE)
    landed = _exchange(waiting + [small_g], scatter + [False], "exchange_layer0")
    partials[0] = landed[:-1]
    return loss, dx, grads, dfw, partials, landed[-1]


_MESH = pl.DeviceIdType.MESH
_ANY = pl.BlockSpec(memory_space=pl.ANY)


def _me_and_peers():
    x, y, c = lax.axis_index("x"), lax.axis_index("y"), lax.axis_index("c")
    flip = lambda a, bit: (1 - a) if bit else a
    peers = []
    for r in range(1, N_DEV):
        px, py, pc = flip(x, (r >> 2) & 1), flip(y, (r >> 1) & 1), flip(c, r & 1)
        peers.append(((px, py, pc), 4 * px + 2 * py + pc))
    return 4 * x + 2 * y + c, peers


def _exchange_copies(srcs, dsts, sems, scatter):
    send_sems, recv_sems, local_sems = sems
    me, peers = _me_and_peers()
    pick = lambda t, to: srcs[t].at[to] if scatter[t] else srcs[t]
    own = [pltpu.make_async_copy(pick(t, me), dsts[t].at[me], local_sems.at[t]) for t in range(len(srcs))]
    sends, waits = [], []
    for r, (dev, idx) in enumerate(peers):
        for t in range(len(srcs)):
            for land, out in ((me, sends), (idx, waits)):
                out.append(pltpu.make_async_remote_copy(pick(t, idx), dsts[t].at[land], send_sems.at[t, r], recv_sems.at[t, r],
                                                        device_id=dev, device_id_type=_MESH))
    return own, sends, waits


def _exchange_start(srcs, dsts, sems, scatter):
    own, sends, _ = _exchange_copies(srcs, dsts, sems, scatter)
    for cp in own + sends:
        cp.start()


def _exchange_wait(srcs, dsts, sems, scatter):
    own, _, waits = _exchange_copies(srcs, dsts, sems, scatter)
    for cp in waits + own:
        cp.wait()


def _exchange_shapes(arrs, scatter):
    outs = [jax.ShapeDtypeStruct(a.shape if sc else (N_DEV,) + a.shape, a.dtype) for a, sc in zip(arrs, scatter)]
    n = len(arrs)
    sems = [pltpu.SemaphoreType.DMA((n, N_DEV - 1)), pltpu.SemaphoreType.DMA((n, N_DEV - 1)), pltpu.SemaphoreType.DMA((n,))]
    return outs, sems


def _exchange(arrs, scatter, name):
    n = len(arrs)

    def body(*refs):
        _exchange_start(refs[:n], refs[n:2 * n], refs[2 * n:], scatter)
        _exchange_wait(refs[:n], refs[n:2 * n], refs[2 * n:], scatter)

    outs, sems = _exchange_shapes(arrs, scatter)
    return _pallas(body, out_shape=tuple(outs), in_specs=[_ANY] * n, out_specs=tuple([_ANY] * n), scratch_shapes=sems,
                   name=name)(*arrs)


def _riding(body, n_in, n_out, ride, is_first, is_last):
    if ride is None:
        return body, [], [], [], []
    arrs, scatter = ride
    n = len(arrs)
    outs, sems = _exchange_shapes(arrs, scatter)

    def wrapped(*refs):
        ins, srcs = refs[:n_in], refs[n_in:n_in + n]
        own_outs, dsts = refs[n_in + n:n_in + n + n_out], refs[n_in + n + n_out:n_in + 2 * n + n_out]
        scratch, ex_sems = refs[n_in + 2 * n + n_out:-3], refs[-3:]

        @pl.when(is_first())
        def _():
            _exchange_start(srcs, dsts, ex_sems, scatter)

        body(*ins, *own_outs, *scratch)

        @pl.when(is_last())
        def _():
            _exchange_wait(srcs, dsts, ex_sems, scatter)

    return wrapped, list(arrs), [_ANY] * n, outs, sems


def _adamw(parts, w, m, v, name):
    n, rows, cols = parts.shape
    tm = next(t for t in (256, 128, 64, 32, 16) if rows % t == 0)

    def body(p_ref, w_ref, m_ref, v_ref, g_ref, d_ref, nm_ref, nv_ref):
        g = p_ref[0].astype(F32)
        for i in range(1, n):
            g = g + p_ref[i].astype(F32)
        nm = ADAM_B1 * m_ref[...] + (1.0 - ADAM_B1) * g
        nv = ADAM_B2 * v_ref[...] + (1.0 - ADAM_B2) * jnp.square(g)
        m_hat = nm / (1.0 - ADAM_B1 ** ADAM_STEP)
        v_hat = nv / (1.0 - ADAM_B2 ** ADAM_STEP)
        g_ref[...] = g
        d_ref[...] = -ADAM_LR * (m_hat / (jnp.sqrt(v_hat) + ADAM_EPS) + ADAM_WD * w_ref[...])
        nm_ref[...] = nm
        nv_ref[...] = nv

    row = pl.BlockSpec((tm, cols), lambda i: (i, 0))
    return _pallas(body, out_shape=(jax.ShapeDtypeStruct((rows, cols), F32),) * 4, grid=(rows // tm,),
                   in_specs=[pl.BlockSpec((n, tm, cols), lambda i: (0, i, 0)), row, row, row], out_specs=(row,) * 4,
                   name=name)(parts, w, m, v)


_WEIGHTS = ("norm_w", "w_in", "fox_b_f", "s5_a_re", "s5_a_im", "s5_b_re", "s5_b_im", "s5_c_re", "s5_c_im", "s5_d",
            "s5_log_dt", "s5_w_glu", "ret_gn_w", "w_out", "final_norm_w")
_SMALL = tuple(n for n in _WEIGHTS if n not in _SHARDED)
_LANES = 128


def _pack(arrs):
    flat = jnp.concatenate([a.reshape(-1) for a in arrs])
    rows = -(-flat.shape[0] // (_LANES * _LANES)) * _LANES
    return jnp.pad(flat, (0, rows * _LANES - flat.shape[0])).reshape(rows, _LANES)


def _unpack(packed, like):
    flat, out, off = packed.reshape(-1), [], 0
    for a in like:
        out.append(flat[off:off + a.size].reshape(a.shape))
        off += a.size
    return out


def kernel(x, norm_w, w_in, fox_b_f, s5_a_re, s5_a_im, s5_b_re, s5_b_im, s5_c_re, s5_c_im, s5_d, s5_log_dt, s5_w_glu, ret_gn_w, w_out, final_norm_w, loss_target, m_norm_w, m_w_in, m_fox_b_f, m_s5_a_re, m_s5_a_im, m_s5_b_re, m_s5_b_im, m_s5_c_re, m_s5_c_im, m_s5_d, m_s5_log_dt, m_s5_w_glu, m_ret_gn_w, m_w_out, m_final_norm_w, v_norm_w, v_w_in, v_fox_b_f, v_s5_a_re, v_s5_a_im, v_s5_b_re, v_s5_b_im, v_s5_c_re, v_s5_c_im, v_s5_d, v_s5_log_dt, v_s5_w_glu, v_ret_gn_w, v_w_out, v_final_norm_w):
    w = dict(norm_w=norm_w, w_in=w_in, fox_b_f=fox_b_f, s5_a_re=s5_a_re, s5_a_im=s5_a_im, s5_b_re=s5_b_re, s5_b_im=s5_b_im,
             s5_c_re=s5_c_re, s5_c_im=s5_c_im, s5_d=s5_d, s5_log_dt=s5_log_dt, s5_w_glu=s5_w_glu, ret_gn_w=ret_gn_w,
             w_out=w_out, final_norm_w=final_norm_w)
    m = dict(norm_w=m_norm_w, w_in=m_w_in, fox_b_f=m_fox_b_f, s5_a_re=m_s5_a_re, s5_a_im=m_s5_a_im, s5_b_re=m_s5_b_re,
             s5_b_im=m_s5_b_im, s5_c_re=m_s5_c_re, s5_c_im=m_s5_c_im, s5_d=m_s5_d, s5_log_dt=m_s5_log_dt,
             s5_w_glu=m_s5_w_glu, ret_gn_w=m_ret_gn_w, w_out=m_w_out, final_norm_w=m_final_norm_w)
    v = dict(norm_w=v_norm_w, w_in=v_w_in, fox_b_f=v_fox_b_f, s5_a_re=v_s5_a_re, s5_a_im=v_s5_a_im, s5_b_re=v_s5_b_re,
             s5_b_im=v_s5_b_im, s5_c_re=v_s5_c_re, s5_c_im=v_s5_c_im, s5_d=v_s5_d, s5_log_dt=v_s5_log_dt,
             s5_w_glu=v_s5_w_glu, ret_gn_w=v_ret_gn_w, w_out=v_w_out, final_norm_w=v_final_norm_w)

    small = {n: w[n] for n in _SMALL}
    loss, dx, _, _, partials, r_small = _step_grads(x[0], loss_target[0], small,
                                                    shards=[w[n].astype(_MXU) for n in _SHARDED])

    res = {}
    for t, n in enumerate(_SHARDED):
        cols = w[n].shape[-1]
        r = jnp.stack([partials[l][t] for l in range(DEPTH)], axis=1)
        outs = _adamw(r.reshape(N_DEV, -1, cols), w[n].reshape(-1, cols), m[n].reshape(-1, cols), v[n].reshape(-1, cols),
                      "adamw_" + n)
        res[n] = [o.reshape(w[n].shape) for o in outs]
    small_w = [w[n] for n in _SMALL]
    outs = _adamw(r_small, _pack(small_w), _pack([m[n] for n in _SMALL]), _pack([v[n] for n in _SMALL]), "adamw_small")
    for k, o in enumerate(outs):
        for n, a in zip(_SMALL, _unpack(o, small_w)):
            res.setdefault(n, [None] * 4)[k] = a

    loss = lax.psum(loss[0, 0], ("x", "y", "c"))
    return (loss, dx[None], *[res[n][0] for n in _WEIGHTS], *[res[n][1] for n in _WEIGHTS],
            *[res[n][2] for n in _WEIGHTS], *[res[n][3] for n in _WEIGHTS])
```

```python
import math

import jax
import jax.numpy as jnp
from jax import lax
from jax.experimental import pallas as pl
from jax.experimental.pallas import tpu as pltpu

F32 = jnp.float32
_MXU = jnp.bfloat16
_HI = lax.Precision.HIGHEST

N_DEV = 8
DEPTH = 4
D_MODEL = 1024
HEAD_DIM = 64
D_FOX = 512
FOX_HEADS = 8
D_S5 = 256
S5_GROUPS = 16
S5_GROUP_CH = 16
S5_STATE = 64
S5_CH = S5_GROUPS * S5_STATE
D_RET = 256
RET_HEADS = 4
CHUNK = 64
ROPE_BASE = 10000.0
EPS = 1e-6
D_IN = 3592
D_INP = 3712
W_SHARD = D_IN // N_DEV
O_GATE, O_FQ, O_FK, O_FV, O_SU, O_RQ, O_RK, O_RV, O_FL = 0, 1024, 1536, 2048, 2560, 2816, 3072, 3328, 3584

ADAM_LR, ADAM_B1, ADAM_B2, ADAM_EPS, ADAM_WD, ADAM_STEP = 0.001, 0.9, 0.999, 1e-08, 0.01, 10

TM = 256
TQ = 512
TS = 256
NEG = -1e30
VMEM_BIG = 56 * 1024 * 1024


def _pallas(body, **kw):
    return pl.pallas_call(body, **kw)


def _whole(shape):
    n = len(shape)
    return pl.BlockSpec(shape, lambda *_: (0,) * n)


def _rows(tm, width, col=0):
    return pl.BlockSpec((tm, width), lambda i: (i, col))


def _dot(a, b, dims=(((1,), (0,)), ((), ()))):
    return lax.dot_general(a.astype(_MXU), b.astype(_MXU), dims, preferred_element_type=F32)


_NT = (((1,), (1,)), ((), ()))
_TN = (((0,), (0,)), ((), ()))


def _mm(a, b, *, ta=False, tb=False, add=None, tm=512, tn=512, tk=512, name):
    m, k = (a.shape[1], a.shape[0]) if ta else a.shape
    n = b.shape[0] if tb else b.shape[1]
    tm, tn, tk = min(tm, m), min(tn, n), min(tk, k)
    assert m % tm == 0 and n % tn == 0 and k % tk == 0, (name, m, n, k)
    dims = (((0 if ta else 1,), (1 if tb else 0,)), ((), ()))

    def body(*refs):
        a_ref, b_ref = refs[0], refs[1]
        o_ref = refs[-1]
        p = _dot(a_ref[...], b_ref[...], dims)
        kk = pl.program_id(2)

        @pl.when(kk == 0)
        def _():
            o_ref[...] = p if add is None else p + refs[2][...]

        @pl.when(kk != 0)
        def _():
            o_ref[...] += p

    a_spec = pl.BlockSpec((tk, tm), lambda i, j, kk: (kk, i)) if ta else pl.BlockSpec((tm, tk), lambda i, j, kk: (i, kk))
    b_spec = pl.BlockSpec((tn, tk), lambda i, j, kk: (j, kk)) if tb else pl.BlockSpec((tk, tn), lambda i, j, kk: (kk, j))
    o_spec = pl.BlockSpec((tm, tn), lambda i, j, kk: (i, j))
    ins, specs = [a, b], [a_spec, b_spec]
    if add is not None:
        ins.append(add)
        specs.append(o_spec)
    return _pallas(body, out_shape=jax.ShapeDtypeStruct((m, n), F32), grid=(m // tm, n // tn, k // tk),
                   in_specs=specs, out_specs=o_spec, name=name,
                   compiler_params=pltpu.CompilerParams(vmem_limit_bytes=VMEM_BIG))(*ins)


def _norm_inproj(x, g, w):
    L = x.shape[0]

    def body(x_ref, g_ref, w_ref, p_ref, h_ref):
        xv = x_ref[...]
        r = lax.rsqrt(jnp.mean(xv * xv, axis=-1, keepdims=True) + EPS)
        h = (xv * r * g_ref[...]).astype(_MXU)
        h_ref[...] = h
        p_ref[...] = _dot(h, w_ref[...])

    return _pallas(body, out_shape=(jax.ShapeDtypeStruct((L, D_INP), F32), jax.ShapeDtypeStruct((L, D_MODEL), _MXU)),
                   grid=(L // TM,), in_specs=[_rows(TM, D_MODEL), _whole((1, D_MODEL)), _whole((D_MODEL, D_INP))],
                   out_specs=(_rows(TM, D_INP), _rows(TM, D_MODEL)), name="norm_inproj",
                   compiler_params=pltpu.CompilerParams(vmem_limit_bytes=VMEM_BIG))(x, g, w)


def _rms_bwd(xv, g, dh):
    r = lax.rsqrt(jnp.mean(xv * xv, axis=-1, keepdims=True) + EPS)
    xh = xv * r
    dg = jnp.sum(dh * xh, axis=0, keepdims=True)
    dxh = dh * g
    dx = r * (dxh - xh * jnp.mean(dxh * xh, axis=-1, keepdims=True))
    return dx, dg


def _inproj_bwd_dx(pieces, w, x, g, dres):
    L = x.shape[0]
    n = len(pieces)

    def body(*refs):
        w_ref, x_ref, g_ref, dr_ref, dx_ref, dg_ref = refs[n:]
        dproj = jnp.concatenate([r[...].astype(_MXU) for r in refs[:n]], axis=-1)
        dh = _dot(dproj, w_ref[...], _NT)
        dx, dg = _rms_bwd(x_ref[...], g_ref[...], dh)
        dx_ref[...] = dx + dr_ref[...]

        @pl.when(pl.program_id(0) == 0)
        def _():
            dg_ref[...] = dg

        @pl.when(pl.program_id(0) != 0)
        def _():
            dg_ref[...] += dg

    return _pallas(body, out_shape=(jax.ShapeDtypeStruct((L, D_MODEL), F32), jax.ShapeDtypeStruct((1, D_MODEL), F32)),
                   grid=(L // TM,),
                   in_specs=[_rows(TM, p.shape[1]) for p in pieces]
                   + [_whole((D_MODEL, D_INP)), _rows(TM, D_MODEL), _whole((1, D_MODEL)), _rows(TM, D_MODEL)],
                   out_specs=(_rows(TM, D_MODEL), _whole((1, D_MODEL))), name="inproj_bwd_dx",
                   compiler_params=pltpu.CompilerParams(vmem_limit_bytes=VMEM_BIG))(*pieces, w, x, g, dres)


def _dw_in(h, pieces):
    L = h.shape[0]
    n = len(pieces)

    def body(*refs):
        h_ref, o_ref = refs[n], refs[n + 1]
        dproj = jnp.concatenate([r[...].astype(_MXU) for r in refs[:n]], axis=-1)
        part = _dot(h_ref[...], dproj, _TN)

        @pl.when(pl.program_id(0) == 0)
        def _():
            o_ref[...] = part

        @pl.when(pl.program_id(0) != 0)
        def _():
            o_ref[...] += part

    return _pallas(body, out_shape=jax.ShapeDtypeStruct((D_MODEL, D_INP), F32), grid=(L // TM,),
                   in_specs=[_rows(TM, p.shape[1]) for p in pieces] + [_rows(TM, D_MODEL)],
                   out_specs=_whole((D_MODEL, D_INP)), name="dw_in",
                   compiler_params=pltpu.CompilerParams(vmem_limit_bytes=VMEM_BIG))(*pieces, h)


PAIR = 2 * HEAD_DIM
N_AUX = 3


def _own(shape, h):
    return lax.broadcasted_iota(jnp.int32, shape, len(shape) - 1) // HEAD_DIM == h


def _hi_dot(a, b):
    return jnp.dot(a, b, precision=_HI, preferred_element_type=F32)


def _tri(n, lower):
    r = lax.broadcasted_iota(jnp.int32, (n, n), 0)
    c = lax.broadcasted_iota(jnp.int32, (n, n), 1)
    return jnp.where(r >= c if lower else r <= c, 1.0, 0.0).astype(F32)


def _fox_cumsum(proj, b):
    L = proj.shape[0]

    def body(fl_ref, b_ref, c_ref, carry_sc):
        @pl.when(pl.program_id(0) == 0)
        def _():
            carry_sc[...] = jnp.zeros((1, PAIR), F32)

        lane = lax.broadcasted_iota(jnp.int32, (TM, PAIR), 1)
        lf = jnp.where(lane < FOX_HEADS, jax.nn.log_sigmoid(fl_ref[...] + b_ref[...]), 0.0)
        cs = _hi_dot(_tri(TM, True), lf) + carry_sc[...]
        c_ref[...] = cs
        carry_sc[...] = cs[TM - 1:TM, :]

    return _pallas(body, out_shape=jax.ShapeDtypeStruct((L, PAIR), F32), grid=(L // TM,),
                   in_specs=[_rows(TM, PAIR, O_FL // PAIR), _whole((1, PAIR))], out_specs=_rows(TM, PAIR),
                   scratch_shapes=[pltpu.VMEM((1, PAIR), F32)], name="fox_cumsum")(proj, b)


def _fox_prep(proj, c):
    L = proj.shape[0]

    def body(q_ref, k_ref, v_ref, c_ref, qa_ref, ka_ref, kat_ref, vt_ref):
        lane = lax.broadcasted_iota(jnp.int32, (TM, PAIR), 1)
        cv = c_ref[...]
        for p in range(FOX_HEADS // 2):
            cols = slice(PAIR * p, PAIR * (p + 1))
            q2, k2 = q_ref[:, cols], k_ref[:, cols]
            vt_ref[p] = v_ref[:, cols].T.astype(_MXU)
            for e in range(2):
                h = 2 * p + e
                own = lane // HEAD_DIM == e
                a = lane - (HEAD_DIM if e == 0 else 0)
                pick = (lax.broadcasted_iota(jnp.int32, (PAIR, PAIR), 0) == h).astype(F32)
                rest = _hi_dot(cv, pick)
                aux_q = jnp.where((a >= N_AUX) & (a < 2 * N_AUX), 1.0, 0.0)
                aux_k = jnp.where((a >= 0) & (a < N_AUX), 1.0, 0.0)
                for n in range(N_AUX):
                    part = rest.astype(_MXU).astype(F32)
                    rest = rest - part
                    aux_q = jnp.where(a == n, part, aux_q)
                    aux_k = jnp.where(a == N_AUX + n, -part, aux_k)
                ka = jnp.where(own, k2, aux_k)
                qa_ref[h] = jnp.where(own, q2 * (1.0 / math.sqrt(HEAD_DIM)), aux_q).astype(_MXU)
                ka_ref[h] = ka.astype(_MXU)
                kat_ref[h] = ka.T.astype(_MXU)

    hl = jax.ShapeDtypeStruct((FOX_HEADS, L, PAIR), _MXU)
    nat = lambda o: _rows(TM, D_FOX, o // D_FOX)
    return _pallas(
        body, out_shape=(hl, hl, jax.ShapeDtypeStruct((FOX_HEADS, PAIR, L), _MXU),
                         jax.ShapeDtypeStruct((FOX_HEADS // 2, PAIR, L), _MXU)),
        grid=(L // TM,), in_specs=[nat(O_FQ), nat(O_FK), nat(O_FV), _rows(TM, PAIR)],
        out_specs=(pl.BlockSpec((FOX_HEADS, TM, PAIR), lambda i: (0, i, 0)), pl.BlockSpec((FOX_HEADS, TM, PAIR), lambda i: (0, i, 0)),
                   pl.BlockSpec((FOX_HEADS, PAIR, TM), lambda i: (0, 0, i)), pl.BlockSpec((FOX_HEADS // 2, PAIR, TM), lambda i: (0, 0, i))),
        name="fox_prep")(proj, proj, proj, c)


def _key_le_query(tq):
    return lax.broadcasted_iota(jnp.int32, (tq, tq), 0) <= lax.broadcasted_iota(jnp.int32, (tq, tq), 1)


def _grid_ends(n0, n1):
    first = lambda: (pl.program_id(0) == 0) & (pl.program_id(1) == 0)
    last = lambda: (pl.program_id(0) == n0 - 1) & (pl.program_id(1) == n1 - 1)
    return first, last


def _fox_fwd(qa, ka, vt, ride=None):
    H, L, _ = qa.shape
    tq = min(TQ, L)
    nq = L // tq

    def body(qa_ref, ka_ref, vt_ref, o_ref, lse_ref, m_sc, l_sc, acc_sc):
        i = pl.program_id(1)
        m_sc[...] = jnp.full((2, 1, tq), NEG, F32)
        l_sc[...] = jnp.zeros((2, 1, tq), F32)
        acc_sc[...] = jnp.zeros((2, PAIR, tq), F32)

        def block(j, masked):
            keys = pl.ds(pl.multiple_of(j * tq, tq), tq)
            vt_blk = vt_ref[:, keys]
            for e in range(2):
                st = _dot(ka_ref[e, keys, :], qa_ref[e], _NT)
                if masked:
                    st = jnp.where(_key_le_query(tq), st, NEG)
                m_prev = m_sc[e]
                m_new = jnp.maximum(m_prev, jnp.max(st, axis=0, keepdims=True))
                alpha = jnp.exp(m_prev - m_new)
                pt = jnp.exp(st - m_new)
                l_sc[e] = alpha * l_sc[e] + jnp.sum(pt, axis=0, keepdims=True)
                acc_sc[e] = alpha * acc_sc[e] + _dot(vt_blk, pt)
                m_sc[e] = m_new

        def off_diagonal(j, carry):
            block(j, False)
            return carry

        lax.fori_loop(0, i, off_diagonal, 0)
        block(i, True)
        row = lax.broadcasted_iota(jnp.int32, (PAIR, tq), 0)
        ot = jnp.where(row < HEAD_DIM, acc_sc[0] / l_sc[0], acc_sc[1] / l_sc[1])
        o_ref[...] = ot.T
        for e in range(2):
            lse_ref[e] = m_sc[e] + jnp.log(l_sc[e])

    body, ex_in, ex_specs, ex_out, ex_sems = _riding(body, 3, 2, ride, *_grid_ends(H // 2, nq))
    res = _pallas(
        body, out_shape=(jax.ShapeDtypeStruct((L, D_FOX), F32), jax.ShapeDtypeStruct((H, 1, L), F32), *ex_out),
        grid=(H // 2, nq),
        in_specs=[pl.BlockSpec((2, tq, PAIR), lambda p, i: (p, i, 0)), pl.BlockSpec((2, L, PAIR), lambda p, i: (p, 0, 0)),
                  pl.BlockSpec((None, PAIR, L), lambda p, i: (p, 0, 0)), *ex_specs],
        out_specs=(pl.BlockSpec((tq, PAIR), lambda p, i: (i, p)), pl.BlockSpec((2, 1, tq), lambda p, i: (p, 0, i)),
                   *ex_specs),
        scratch_shapes=[pltpu.VMEM((2, 1, tq), F32), pltpu.VMEM((2, 1, tq), F32), pltpu.VMEM((2, PAIR, tq), F32), *ex_sems],
        name="fox_fwd" if ride is None else "fox_fwd_gather")(qa, ka, vt, *ex_in)
    return res[0], res[1], list(res[2:])


def _fox_bwd(qa, ka, kat, proj, do, o, lse, ride=None):
    H, L, _ = qa.shape
    tq = min(TQ, L)
    nq = L // tq

    def body(qa_ref, ka_ref, kat_ref, v_ref, do_ref, o_ref, lse_ref, dqt_ref, dk_ref, dv_ref, delta_sc, dk_sc, dv_sc):
        j = pl.program_id(1)

        @pl.when(j == 0)
        def _():
            head_rows = (lax.broadcasted_iota(jnp.int32, (8, PAIR), 1) // HEAD_DIM
                         == lax.broadcasted_iota(jnp.int32, (8, PAIR), 0)).astype(F32)
            delta_sc[...] = lax.dot_general(head_rows, do_ref[...] * o_ref[...], _NT, precision=_HI,
                                            preferred_element_type=F32)
            dqt_ref[...] = jnp.zeros((2, PAIR, L), F32)

        dk_sc[...] = jnp.zeros((2, tq, PAIR), F32)
        dv_sc[...] = jnp.zeros((tq, PAIR), F32)
        vb = v_ref[...]

        def block(i, masked):
            qs = pl.ds(pl.multiple_of(i * tq, tq), tq)
            dob = do_ref[qs, :]
            for e in range(2):
                own = _own((tq, PAIR), e)
                qh = qa_ref[e, qs, :]
                pt = jnp.exp(_dot(ka_ref[e], qh, _NT) - lse_ref[e, :, qs])
                if masked:
                    pt = jnp.where(_key_le_query(tq), pt, 0.0)
                dv_sc[...] += _dot(pt, jnp.where(own, dob, 0.0))
                dpt = _dot(jnp.where(own, vb, 0.0), dob, _NT)
                ds = (pt * (dpt - delta_sc[e:e + 1, qs])).astype(_MXU)
                dk_sc[e] += _dot(ds, qh)
                dqt_ref[e, :, qs] += _dot(kat_ref[e], ds)

        def off_diagonal(i, carry):
            block(i, False)
            return carry

        block(j, True)
        lax.fori_loop(j + 1, nq, off_diagonal, 0)
        dk_ref[...] = dk_sc[...]
        dv_ref[...] = dv_sc[...]

    nat = pl.BlockSpec((L, PAIR), lambda p, j: (0, p))
    body, ex_in, ex_specs, ex_out, ex_sems = _riding(body, 7, 3, ride, *_grid_ends(H // 2, nq))
    res = _pallas(
        body, out_shape=(jax.ShapeDtypeStruct((H, PAIR, L), F32), jax.ShapeDtypeStruct((H, L, PAIR), F32),
                         jax.ShapeDtypeStruct((L, D_FOX), F32), *ex_out),
        grid=(H // 2, nq),
        in_specs=[pl.BlockSpec((2, L, PAIR), lambda p, j: (p, 0, 0)), pl.BlockSpec((2, tq, PAIR), lambda p, j: (p, j, 0)),
                  pl.BlockSpec((2, PAIR, tq), lambda p, j: (p, 0, j)),
                  pl.BlockSpec((tq, PAIR), lambda p, j: (j, O_FV // PAIR + p)), nat, nat,
                  pl.BlockSpec((2, 1, L), lambda p, j: (p, 0, 0)), *ex_specs],
        out_specs=(pl.BlockSpec((2, PAIR, L), lambda p, j: (p, 0, 0)), pl.BlockSpec((2, tq, PAIR), lambda p, j: (p, j, 0)),
                   pl.BlockSpec((tq, PAIR), lambda p, j: (j, p)), *ex_specs),
        scratch_shapes=[pltpu.VMEM((8, L), F32), pltpu.VMEM((2, tq, PAIR), F32), pltpu.VMEM((tq, PAIR), F32), *ex_sems],
        name="fox_bwd" if ride is None else "fox_bwd_exchange",
        compiler_params=pltpu.CompilerParams(vmem_limit_bytes=VMEM_BIG))(qa, ka, kat, proj, do, o, lse, *ex_in)
    return res[0], res[1], res[2], list(res[3:])


def _fox_post_bwd(dqt, dkraw, proj, b):
    L = proj.shape[0]
    nb = L // TM

    def body(dqt_ref, dkr_ref, fl_ref, b_ref, dq_ref, dk_ref, dfl_ref, db_ref, carry_sc):
        first = pl.program_id(0) == 0

        @pl.when(first)
        def _():
            carry_sc[...] = jnp.zeros((1, PAIR), F32)

        lane = lax.broadcasted_iota(jnp.int32, (TM, PAIR), 1)
        rr = lax.broadcasted_iota(jnp.int32, (PAIR, PAIR), 0)
        cc = lax.broadcasted_iota(jnp.int32, (PAIR, PAIR), 1)
        dc = jnp.zeros((TM, PAIR), F32)
        for p in range(FOX_HEADS // 2):
            cols = slice(PAIR * p, PAIR * (p + 1))
            dqs = [dqt_ref[2 * p + e].T for e in range(2)]
            dks = [dkr_ref[2 * p + e] for e in range(2)]
            dq_ref[:, cols] = jnp.where(lane < HEAD_DIM, dqs[0], dqs[1]) * (1.0 / math.sqrt(HEAD_DIM))
            dk_ref[:, cols] = jnp.where(lane < HEAD_DIM, dks[0], dks[1])
            for e in range(2):
                base = HEAD_DIM if e == 0 else 0
                to_head = cc == 2 * p + e
                dc = dc + _hi_dot(dqs[e], jnp.where((rr == base) & to_head, 1.0, 0.0))
                dc = dc + _hi_dot(dks[e], jnp.where((rr == base + N_AUX) & to_head, -1.0, 0.0))
        rs = _hi_dot(_tri(TM, False), dc) + carry_sc[...]
        carry_sc[...] = rs[0:1, :]
        dfl = jnp.where(lane < FOX_HEADS, rs * jax.nn.sigmoid(-(fl_ref[...] + b_ref[...])), 0.0)
        dfl_ref[...] = dfl
        db = jnp.sum(dfl, axis=0, keepdims=True)

        @pl.when(first)
        def _():
            db_ref[...] = db

        @pl.when(jnp.logical_not(first))
        def _():
            db_ref[...] += db

    rev = lambda i: nb - 1 - i
    nat = pl.BlockSpec((TM, D_FOX), lambda i: (rev(i), 0))
    return _pallas(
        body, out_shape=(jax.ShapeDtypeStruct((L, D_FOX), F32),) * 2
        + (jax.ShapeDtypeStruct((L, PAIR), F32), jax.ShapeDtypeStruct((1, PAIR), F32)),
        grid=(nb,),
        in_specs=[pl.BlockSpec((FOX_HEADS, PAIR, TM), lambda i: (0, 0, rev(i))),
                  pl.BlockSpec((FOX_HEADS, TM, PAIR), lambda i: (0, rev(i), 0)),
                  pl.BlockSpec((TM, PAIR), lambda i: (rev(i), O_FL // PAIR)), _whole((1, PAIR))],
        out_specs=(nat, nat, pl.BlockSpec((TM, PAIR), lambda i: (rev(i), 0)), _whole((1, PAIR))),
        scratch_shapes=[pltpu.VMEM((1, PAIR), F32)], name="fox_post_bwd")(dqt, dkraw, proj, b)


def _s5_expand():
    r = lax.broadcasted_iota(jnp.int32, (S5_STATE, S5_STATE * S5_GROUP_CH), 0)
    c = lax.broadcasted_iota(jnp.int32, (S5_STATE, S5_STATE * S5_GROUP_CH), 1)
    return jnp.where(c // S5_GROUP_CH == r, 1.0, 0.0).astype(F32)


def _s5_disc_math(ar, ai, ldt, br, bi):
    dt = jnp.exp(ldt)
    mag = jnp.exp(ar * dt)
    lr = mag * jnp.cos(ai * dt)
    li = mag * jnp.sin(ai * dt)
    den = ar * ar + ai * ai
    fr = ((lr - 1.0) * ar + li * ai) / den
    fi = (li * ar - (lr - 1.0) * ai) / den
    e = _s5_expand()
    fre = jnp.dot(fr, e, precision=_HI, preferred_element_type=F32)
    fie = jnp.dot(fi, e, precision=_HI, preferred_element_type=F32)
    return lr, li, fre * br - fie * bi, fre * bi + fie * br


def _layer_blocks(arrs):
    return [pl.BlockSpec((None,) + a.shape[1:], lambda l: (l, 0, 0)) for a in arrs]


def _s5_disc(ar, ai, ldt, br, bi):
    def body(ar_ref, ai_ref, ldt_ref, br_ref, bi_ref, lr_ref, li_ref, bbr_ref, bbi_ref):
        lr, li, bbr, bbi = _s5_disc_math(ar_ref[...], ai_ref[...], ldt_ref[...], br_ref[...], bi_ref[...])
        lr_ref[...] = lr
        li_ref[...] = li
        bbr_ref[...] = bbr
        bbi_ref[...] = bbi

    ins = (ar, ai, ldt, br, bi)
    outs = (ar, ai, br, bi)
    return _pallas(body, out_shape=tuple(jax.ShapeDtypeStruct(a.shape, F32) for a in outs), grid=(DEPTH,),
                   in_specs=_layer_blocks(ins), out_specs=tuple(_layer_blocks(outs)), name="s5_disc")(*ins)


def _s5_disc_bwd(ar, ai, ldt, br, bi, dlr, dli, dbbr, dbbi):
    def body(ar_ref, ai_ref, ldt_ref, br_ref, bi_ref, dlr_ref, dli_ref, dbbr_ref, dbbi_ref,
             dar_ref, dai_ref, dldt_ref, dbr_ref, dbi_ref):
        _, vjp = jax.vjp(_s5_disc_math, ar_ref[...], ai_ref[...], ldt_ref[...], br_ref[...], bi_ref[...])
        dar, dai, dldt, dbr, dbi = vjp((dlr_ref[...], dli_ref[...], dbbr_ref[...], dbbi_ref[...]))
        dar_ref[...] = dar
        dai_ref[...] = dai
        dldt_ref[...] = dldt
        dbr_ref[...] = dbr
        dbi_ref[...] = dbi

    ins = (ar, ai, ldt, br, bi, dlr, dli, dbbr, dbbi)
    outs = (ar, ai, ldt, br, bi)
    return _pallas(body, out_shape=tuple(jax.ShapeDtypeStruct(a.shape, F32) for a in outs), grid=(DEPTH,),
                   in_specs=_layer_blocks(ins), out_specs=tuple(_layer_blocks(outs)), name="s5_disc_bwd")(*ins)


SLAB = 2 * S5_CH // 128


def _slab_rows(s, ts):
    return pl.ds(s, ts, stride=SLAB)


def _s5_fwd(proj, wb, wc, lam, d, w_glu):
    L = proj.shape[0]
    ts = min(TS, L)

    def body(u_ref, wb_ref, wc_ref, lam_ref, d_ref, wg_ref, xs_ref, ypre_ref, ys_ref, b_sc, c_sc):
        @pl.when(pl.program_id(0) == 0)
        def _():
            c_sc[...] = jnp.zeros((SLAB, 128), F32)

        u = u_ref[...]
        ub = u.astype(_MXU)
        for s in range(SLAB):
            b_sc[_slab_rows(s, ts), :] = _dot(ub, wb_ref[:, 128 * s:128 * (s + 1)])
        lr, li = lam_ref[0:8, :], lam_ref[8:16, :]

        def step(t, carry):
            xr, xi = carry
            row = pl.multiple_of(t * SLAB, SLAB)
            nr = lr * xr - li * xi + b_sc[pl.ds(row, 8), :]
            ni = lr * xi + li * xr + b_sc[pl.ds(row + 8, 8), :]
            xs_ref[pl.ds(row, 8), :] = nr
            xs_ref[pl.ds(row + 8, 8), :] = ni
            return nr, ni

        xr, xi = lax.fori_loop(0, ts, step, (c_sc[0:8, :], c_sc[8:16, :]), unroll=8)
        c_sc[0:8, :] = xr
        c_sc[8:16, :] = xi
        y = jnp.zeros((ts, D_S5), F32)
        for s in range(SLAB):
            y = y + _dot(xs_ref[_slab_rows(s, ts), :], wc_ref[128 * s:128 * (s + 1), :])
        ypre_ref[...] = y
        y1 = jax.nn.gelu(y + d_ref[...] * u)
        ys_ref[...] = y1 * jax.nn.sigmoid(_dot(y1, wg_ref[...]))

    row = _rows(ts, D_S5)
    slabs = pl.BlockSpec((ts * SLAB, 128), lambda n: (n, 0))
    return _pallas(
        body, out_shape=(jax.ShapeDtypeStruct((L * SLAB, 128), F32), jax.ShapeDtypeStruct((L, D_S5), F32),
                         jax.ShapeDtypeStruct((L, D_S5), F32)),
        grid=(L // ts,),
        in_specs=[_rows(ts, D_S5, O_SU // D_S5), _whole((D_S5, 2 * S5_CH)), _whole((2 * S5_CH, D_S5)), _whole((SLAB, 128)),
                  _whole((1, D_S5)), _whole((D_S5, D_S5))],
        out_specs=(slabs, row, row),
        scratch_shapes=[pltpu.VMEM((ts * SLAB, 128), F32), pltpu.VMEM((SLAB, 128), F32)], name="s5_fwd")(
            proj, wb, wc, lam, d, w_glu)


def _s5_bwd(proj, ypre, dys, xs, wb, wc, lam, d, w_glu):
    L = proj.shape[0]
    ts = min(TS, L)
    nb = L // ts

    def body(u_ref, y_ref, dys_ref, xs_ref, xp_ref, wb_ref, wc_ref, lam_ref, d_ref, wg_ref,
             du_ref, dwb_ref, dwc_ref, dlam_ref, dd_ref, dwg_ref, dx_sc, g_sc, c_sc):
        n = pl.program_id(0)

        @pl.when(n == 0)
        def _():
            c_sc[...] = jnp.zeros((SLAB, 128), F32)
            dlam_ref[...] = jnp.zeros((SLAB, 128), F32)
            dwb_ref[...] = jnp.zeros((D_S5, 2 * S5_CH), F32)
            dwc_ref[...] = jnp.zeros((2 * S5_CH, D_S5), F32)
            dd_ref[...] = jnp.zeros((1, D_S5), F32)
            dwg_ref[...] = jnp.zeros((D_S5, D_S5), F32)

        u, dv, dout = u_ref[...], d_ref[...], dys_ref[...]
        y1, gelu_vjp = jax.vjp(jax.nn.gelu, y_ref[...] + dv * u)
        sg = jax.nn.sigmoid(_dot(y1, wg_ref[...]))
        dz = dout * y1 * sg * (1.0 - sg)
        dy, = gelu_vjp(dout * sg + _dot(dz, wg_ref[...], _NT))
        dd_ref[...] += jnp.sum(dy * u, axis=0, keepdims=True)
        dwg_ref[...] += _dot(y1, dz, _TN)
        dyb = dy.astype(_MXU)
        for s in range(SLAB):
            cols = slice(128 * s, 128 * (s + 1))
            dx_sc[_slab_rows(s, ts), :] = _dot(dyb, wc_ref[cols, :], _NT)
            dwc_ref[cols, :] += _dot(xs_ref[_slab_rows(s, ts), :], dyb, _TN)
        lr, li = lam_ref[0:8, :], lam_ref[8:16, :]

        def adjoint(row, pr, pi, carry):
            gr, gi, ar, ai = carry
            nr = dx_sc[pl.ds(row, 8), :] + lr * gr + li * gi
            ni = dx_sc[pl.ds(row + 8, 8), :] - li * gr + lr * gi
            g_sc[pl.ds(row, 8), :] = nr
            g_sc[pl.ds(row + 8, 8), :] = ni
            return nr, ni, ar + nr * pr + ni * pi, ai - nr * pi + ni * pr

        def step(k, carry):
            row = pl.multiple_of((ts - 1 - k) * SLAB, SLAB)
            prev = pl.multiple_of((ts - 2 - k) * SLAB, SLAB)
            return adjoint(row, xs_ref[pl.ds(prev, 8), :], xs_ref[pl.ds(prev + 8, 8), :], carry)

        z = jnp.zeros((8, 128), F32)
        carry = lax.fori_loop(0, ts - 1, step, (c_sc[0:8, :], c_sc[8:16, :], z, z), unroll=8)
        has_prev = jnp.where(n == nb - 1, 0.0, 1.0)
        gr, gi, ar, ai = adjoint(0, xp_ref[0:8, :] * has_prev, xp_ref[8:16, :] * has_prev, carry)
        c_sc[0:8, :] = gr
        c_sc[8:16, :] = gi
        dlam_ref[0:8, :] += ar
        dlam_ref[8:16, :] += ai
        ub = u.astype(_MXU)
        du = dy * dv
        for s in range(SLAB):
            cols = slice(128 * s, 128 * (s + 1))
            gs = g_sc[_slab_rows(s, ts), :]
            du = du + _dot(gs, wb_ref[:, cols], _NT)
            dwb_ref[:, cols] += _dot(ub, gs, _TN)
        du_ref[...] = du

    blk = lambda n: nb - 1 - n
    row = pl.BlockSpec((ts, D_S5), lambda n: (blk(n), 0))
    return _pallas(
        body, out_shape=(jax.ShapeDtypeStruct((L, D_S5), F32), jax.ShapeDtypeStruct((D_S5, 2 * S5_CH), F32),
                         jax.ShapeDtypeStruct((2 * S5_CH, D_S5), F32), jax.ShapeDtypeStruct((SLAB, 128), F32),
                         jax.ShapeDtypeStruct((1, D_S5), F32), jax.ShapeDtypeStruct((D_S5, D_S5), F32)),
        grid=(nb,),
        in_specs=[pl.BlockSpec((ts, D_S5), lambda n: (blk(n), O_SU // D_S5)), row, row,
                  pl.BlockSpec((ts * SLAB, 128), lambda n: (blk(n), 0)),
                  pl.BlockSpec((SLAB, 128), lambda n: (jnp.maximum(blk(n) * ts - 1, 0), 0)),
                  _whole((D_S5, 2 * S5_CH)), _whole((2 * S5_CH, D_S5)), _whole((SLAB, 128)), _whole((1, D_S5)),
                  _whole((D_S5, D_S5))],
        out_specs=(row, _whole((D_S5, 2 * S5_CH)), _whole((2 * S5_CH, D_S5)), _whole((SLAB, 128)), _whole((1, D_S5)),
                   _whole((D_S5, D_S5))),
        scratch_shapes=[pltpu.VMEM((ts * SLAB, 128), F32), pltpu.VMEM((ts * SLAB, 128), F32), pltpu.VMEM((SLAB, 128), F32)],
        name="s5_bwd")(proj, ypre, dys, xs, xs, wb, wc, lam, d, w_glu)


def _rot(z, cos, sin):
    lane = lax.broadcasted_iota(jnp.int32, z.shape, 1)
    zs = z * sin
    half = HEAD_DIM // 2
    return z * cos + jnp.where(lane % HEAD_DIM < half, pltpu.roll(zs, PAIR - half, 1), pltpu.roll(zs, half, 1))


def _head_avg():
    r = lax.broadcasted_iota(jnp.int32, (PAIR, PAIR), 0) // HEAD_DIM
    c = lax.broadcasted_iota(jnp.int32, (PAIR, PAIR), 1) // HEAD_DIM
    return jnp.where(r == c, 1.0 / HEAD_DIM, 0.0).astype(F32)


def _ret_tables(tq):
    lg = jnp.log1p(-(2.0 ** (-5.0 - jnp.arange(RET_HEADS, dtype=F32))))
    scale = 1.0 / math.sqrt(HEAD_DIM)
    pos = jnp.arange(tq)
    n = pos.astype(F32)
    dist = jnp.abs(n[:, None] - n[None, :])
    ok = (pos[None, :] // CHUNK) <= (pos[:, None] // CHUNK)
    w = jnp.where(ok[None], scale * jnp.exp(lg[:, None, None] * dist[None]), 0.0)
    lgl = jnp.repeat(lg, HEAD_DIM)
    dq_tab = scale * jnp.exp(lgl[None, :] * (n[:, None] + 1.0))
    dk_tab = jnp.exp(lgl[None, :] * (tq - 1.0 - n[:, None]))
    blk = jnp.arange(PAIR) // HEAD_DIM
    bd = (blk[:, None] == blk[None, :]).astype(F32)
    gbd = bd[None] * jnp.exp(lgl.reshape(RET_HEADS // 2, PAIR)[:, :, None] * tq)
    return dict(w=w, wt=w.transpose(0, 2, 1), dq=dq_tab, dk=dk_tab, gbd=gbd, bd=bd)


def _ret_specs(tq, nq, rev):
    blk = (lambda i: nq - 1 - i) if rev else (lambda i: i)
    col = lambda o: pl.BlockSpec((tq, PAIR), lambda p, i: (blk(i), o // PAIR + p))
    return dict(
        rq=col(O_RQ), rk=col(O_RK), rv=col(O_RV), nat=col(0),
        w=pl.BlockSpec((2, tq, tq), lambda p, i: (p, 0, 0)), tab=pl.BlockSpec((tq, PAIR), lambda p, i: (0, p)),
        gbd=pl.BlockSpec((None, PAIR, PAIR), lambda p, i: (p, 0, 0)), bd=pl.BlockSpec((PAIR, PAIR), lambda p, i: (0, 0)),
        gn=pl.BlockSpec((1, PAIR), lambda p, i: (0, p)),
        st=pl.BlockSpec((None, None, PAIR, PAIR), lambda p, i: (p, blk(i), 0, 0)))


def _ret_fwd(proj, cos_t, sin_t, tabs, gn):
    L = proj.shape[0]
    tq = tabs["w"].shape[1]
    nq = L // tq

    def body(rq_ref, rk_ref, rv_ref, cos_ref, sin_ref, w_ref, dqt_ref, dkt_ref, gbd_ref, bd_ref, gn_ref,
             o_ref, y_ref, st_ref, s_sc):
        @pl.when(pl.program_id(1) == 0)
        def _():
            s_sc[...] = jnp.zeros((PAIR, PAIR), F32)

        state = s_sc[...]
        st_ref[...] = state
        cos, sin = cos_ref[...], sin_ref[...]
        q2, k2, v2 = _rot(rq_ref[...], cos, sin), _rot(rk_ref[...], cos, sin), rv_ref[...]
        o = _dot(q2 * dqt_ref[...], state)
        for h in range(2):
            own = _own((tq, PAIR), h)
            a = _dot(jnp.where(own, q2, 0.0), k2, _NT) * w_ref[h]
            o = o + _dot(a, jnp.where(own, v2, 0.0))
        s_sc[...] = gbd_ref[...] * state + bd_ref[...] * _dot(k2 * dkt_ref[...], v2, _TN)
        o_ref[...] = o
        avg = _head_avg()
        oc = o - _hi_dot(o, avg)
        y_ref[...] = oc * lax.rsqrt(_hi_dot(oc * oc, avg) + EPS) * gn_ref[...]

    sp = _ret_specs(tq, nq, False)
    nat = jax.ShapeDtypeStruct((L, D_RET), F32)
    return _pallas(
        body, out_shape=(nat, nat, jax.ShapeDtypeStruct((RET_HEADS // 2, nq, PAIR, PAIR), F32)), grid=(RET_HEADS // 2, nq),
        in_specs=[sp["rq"], sp["rk"], sp["rv"], sp["nat"], sp["nat"], sp["w"], sp["tab"], sp["tab"], sp["gbd"], sp["bd"],
                  sp["gn"]],
        out_specs=(sp["nat"], sp["nat"], sp["st"]), scratch_shapes=[pltpu.VMEM((PAIR, PAIR), F32)],
        name="ret_fwd")(proj, proj, proj, cos_t, sin_t, tabs["w"], tabs["dq"], tabs["dk"], tabs["gbd"], tabs["bd"], gn)


def _ret_bwd(proj, cos_t, sin_t, tabs, gn, o_pre, dy, states):
    L = proj.shape[0]
    tq = tabs["w"].shape[1]
    nq = L // tq

    def body(rq_ref, rk_ref, rv_ref, cos_ref, sin_ref, w_ref, wt_ref, dqt_ref, dkt_ref, gbd_ref, bd_ref, gn_ref,
             o_ref, dy_ref, st_ref, drq_ref, drk_ref, drv_ref, dgn_ref, g_sc):
        first = pl.program_id(1) == 0

        @pl.when(first)
        def _():
            g_sc[...] = jnp.zeros((PAIR, PAIR), F32)

        cos, sin = cos_ref[...], sin_ref[...]
        q2, k2, v2 = _rot(rq_ref[...], cos, sin), _rot(rk_ref[...], cos, sin), rv_ref[...]
        avg = _head_avg()
        ov, dyv = o_ref[...], dy_ref[...]
        oc = ov - _hi_dot(ov, avg)
        r = lax.rsqrt(_hi_dot(oc * oc, avg) + EPS)
        oh = oc * r
        dgn = jnp.sum(dyv * oh, axis=0, keepdims=True)
        doh = dyv * gn_ref[...]
        do = r * (doh - _hi_dot(doh, avg) - oh * _hi_dot(doh * oh, avg))
        state, g = st_ref[...], g_sc[...]
        dqt, dkt = dqt_ref[...], dkt_ref[...]
        dq = _dot(do, state, _NT) * dqt
        dk = _dot(v2, g, _NT) * dkt
        dv = _dot(k2 * dkt, g)
        g_sc[...] = gbd_ref[...] * g + bd_ref[...] * _dot(q2 * dqt, do, _TN)
        for h in range(2):
            own = _own((tq, PAIR), h)
            qm, dom = jnp.where(own, q2, 0.0), jnp.where(own, do, 0.0)
            dv = dv + _dot(_dot(k2, qm, _NT) * wt_ref[h], dom)
            dq = dq + _dot(_dot(dom, v2, _NT) * w_ref[h], jnp.where(own, k2, 0.0))
            dk = dk + _dot(_dot(v2, dom, _NT) * wt_ref[h], qm)
        drq_ref[...] = _rot(dq, cos, -sin)
        drk_ref[...] = _rot(dk, cos, -sin)
        drv_ref[...] = dv

        @pl.when(first)
        def _():
            dgn_ref[...] = dgn

        @pl.when(jnp.logical_not(first))
        def _():
            dgn_ref[...] += dgn

    sp = _ret_specs(tq, nq, True)
    nat = jax.ShapeDtypeStruct((L, D_RET), F32)
    return _pallas(
        body, out_shape=(nat, nat, nat, jax.ShapeDtypeStruct((1, D_RET), F32)), grid=(RET_HEADS // 2, nq),
        in_specs=[sp["rq"], sp["rk"], sp["rv"], sp["nat"], sp["nat"], sp["w"], sp["w"], sp["tab"], sp["tab"], sp["gbd"],
                  sp["bd"], sp["gn"], sp["nat"], sp["nat"], sp["st"]],
        out_specs=(sp["nat"], sp["nat"], sp["nat"], sp["gn"]), scratch_shapes=[pltpu.VMEM((PAIR, PAIR), F32)],
        name="ret_bwd")(proj, proj, proj, cos_t, sin_t, tabs["w"], tabs["wt"], tabs["dq"], tabs["dk"], tabs["gbd"],
                        tabs["bd"], gn, o_pre, dy, states)


def _gate_out(yf, ys, yr, proj, x, w):
    L = x.shape[0]

    def body(yf_ref, ys_ref, yr_ref, g_ref, x_ref, w_ref, y_ref, xn_ref):
        cat = jnp.concatenate([yf_ref[...], ys_ref[...], yr_ref[...]], axis=-1)
        y = cat * jax.nn.silu(g_ref[...])
        y_ref[...] = y
        xn_ref[...] = x_ref[...] + _dot(y, w_ref[...])

    full = _rows(TM, D_MODEL)
    return _pallas(body, out_shape=(jax.ShapeDtypeStruct((L, D_MODEL), F32),) * 2, grid=(L // TM,),
                   in_specs=[_rows(TM, D_FOX), _rows(TM, D_S5), _rows(TM, D_RET), _rows(TM, D_MODEL, O_GATE // D_MODEL),
                             full, _whole((D_MODEL, D_MODEL))],
                   out_specs=(full, full), name="gate_out")(yf, ys, yr, proj, x, w)


def _gate_out_bwd(dxn, w, yf, ys, yr, proj):
    L = dxn.shape[0]

    def body(dx_ref, w_ref, yf_ref, ys_ref, yr_ref, g_ref, dyf_ref, dys_ref, dyr_ref, dg_ref):
        dy = _dot(dx_ref[...], w_ref[...], _NT)
        g = g_ref[...]
        sg = jax.nn.sigmoid(g)
        dcat = dy * (g * sg)
        dyf_ref[...] = dcat[:, :D_FOX]
        dys_ref[...] = dcat[:, D_FOX:D_FOX + D_S5]
        dyr_ref[...] = dcat[:, D_FOX + D_S5:]
        cat = jnp.concatenate([yf_ref[...], ys_ref[...], yr_ref[...]], axis=-1)
        dg_ref[...] = dy * cat * (sg * (1.0 + g * (1.0 - sg)))

    full = _rows(TM, D_MODEL)
    f, s, r = _rows(TM, D_FOX), _rows(TM, D_S5), _rows(TM, D_RET)
    return _pallas(body, out_shape=(jax.ShapeDtypeStruct((L, D_FOX), F32), jax.ShapeDtypeStruct((L, D_S5), F32),
                                    jax.ShapeDtypeStruct((L, D_RET), F32), jax.ShapeDtypeStruct((L, D_MODEL), F32)),
                   grid=(L // TM,),
                   in_specs=[full, _whole((D_MODEL, D_MODEL)), f, s, r, _rows(TM, D_MODEL, O_GATE // D_MODEL)],
                   out_specs=(f, s, r, full), name="gate_out_bwd")(dxn, w, yf, ys, yr, proj)


def _final_loss(x, g, tgt):
    L = x.shape[0]

    def body(x_ref, g_ref, t_ref, loss_ref, dx_ref, dg_ref):
        xv, gv = x_ref[...], g_ref[...]
        r = lax.rsqrt(jnp.mean(xv * xv, axis=-1, keepdims=True) + EPS)
        err = xv * r * gv - t_ref[...]
        part = 0.5 * jnp.sum(jnp.mean(err * err, axis=-1, keepdims=True), axis=0, keepdims=True)
        dx, dg = _rms_bwd(xv, gv, err * (1.0 / D_MODEL))
        dx_ref[...] = dx

        @pl.when(pl.program_id(0) == 0)
        def _():
            loss_ref[...] = part
            dg_ref[...] = dg

        @pl.when(pl.program_id(0) != 0)
        def _():
            loss_ref[...] += part
            dg_ref[...] += dg

    full = _rows(TM, D_MODEL)
    return _pallas(body, out_shape=(jax.ShapeDtypeStruct((1, 1), F32), jax.ShapeDtypeStruct((L, D_MODEL), F32),
                                    jax.ShapeDtypeStruct((1, D_MODEL), F32)),
                   grid=(L // TM,), in_specs=[full, _whole((1, D_MODEL)), full],
                   out_specs=(_whole((1, 1)), full, _whole((1, D_MODEL))), name="final_loss")(x, g, tgt)


def _block_diag(blocks):
    n, g, r, c = blocks.shape
    eye = jnp.eye(g, dtype=blocks.dtype)
    return (blocks[:, :, :, None, :] * eye[None, :, None, :, None]).reshape(n, g * r, g * c)


def _diag_blocks(m, g):
    n, r, c = m.shape[0], m.shape[1] // g, m.shape[2] // g
    eye = jnp.eye(g, dtype=m.dtype)
    return jnp.sum(m.reshape(n, g, r, g, c) * eye[None, :, None, :, None], axis=3)


def _rope_tables(L):
    half = HEAD_DIM // 2
    freqs = ROPE_BASE ** (-jnp.arange(half, dtype=F32) / half)
    ang = jnp.arange(L, dtype=F32)[:, None] * freqs[None, :]
    cos, sin = jnp.cos(ang), jnp.sin(ang)
    cos_t = jnp.tile(jnp.concatenate([cos, cos], axis=-1), (1, RET_HEADS))
    sin_t = jnp.tile(jnp.concatenate([sin, -sin], axis=-1), (1, RET_HEADS))
    return cos_t, sin_t


def _s5_disc_args(small):
    g, s, ch = S5_GROUPS, S5_STATE, S5_GROUP_CH
    return (small["s5_a_re"], small["s5_a_im"], small["s5_log_dt"][:, :, None],
            small["s5_b_re"].reshape(DEPTH, g, s * ch), small["s5_b_im"].reshape(DEPTH, g, s * ch))


def _s5_mats(small):
    g, s, ch = S5_GROUPS, S5_STATE, S5_GROUP_CH
    lr, li, bbr, bbi = _s5_disc(*_s5_disc_args(small))
    lam = jnp.concatenate([lr.reshape(DEPTH, 8, 128), li.reshape(DEPTH, 8, 128)], axis=1)
    wb = jnp.concatenate([_block_diag(b.reshape(DEPTH, g, s, ch).transpose(0, 1, 3, 2)) for b in (bbr, bbi)], axis=2)
    wc = jnp.concatenate([_block_diag(c.transpose(0, 1, 3, 2)) for c in (small["s5_c_re"], -small["s5_c_im"])], axis=1)
    return lam, wb.astype(_MXU), wc.astype(_MXU)


def _s5_param_grads(small, dwb, dwc, dlam):
    g, s, ch = S5_GROUPS, S5_STATE, S5_GROUP_CH
    dc = [_diag_blocks(m, g).transpose(0, 1, 3, 2) for m in (dwc[:, :S5_CH], dwc[:, S5_CH:])]
    dbb = [_diag_blocks(m, g).transpose(0, 1, 3, 2).reshape(DEPTH, g, s * ch) for m in (dwb[:, :, :S5_CH], dwb[:, :, S5_CH:])]
    dar, dai, dldt, dbr, dbi = _s5_disc_bwd(*_s5_disc_args(small), dlam[:, :8].reshape(DEPTH, g, s),
                                            dlam[:, 8:].reshape(DEPTH, g, s), dbb[0], dbb[1])
    shp = (DEPTH, g, s, ch)
    return dict(s5_a_re=dar, s5_a_im=dai, s5_log_dt=dldt.reshape(DEPTH, g), s5_b_re=dbr.reshape(shp),
                s5_b_im=dbi.reshape(shp), s5_c_re=dc[0], s5_c_im=-dc[1])


def _layer_fwd(x, p, rope, ride=None):
    L = x.shape[0]
    cos_t, sin_t, ret_tabs = rope
    s = {"x": x}
    proj, h = _norm_inproj(x, p["norm_w"], p["w_in"])
    s["proj"], s["h"] = proj, h
    qa, ka, kat, vt = _fox_prep(proj, _fox_cumsum(proj, p["b_f"]))
    yf, lse, landed = _fox_fwd(qa, ka, vt, ride)
    s.update(qa=qa, ka=ka, kat=kat, lse=lse, yf=yf)
    xs, ypre, ys = _s5_fwd(proj, p["wb"], p["wc"], p["lam"], p["d"], p["w_glu"])
    s.update(xs=xs, ypre=ypre, ys=ys)
    o_pre, yr, states = _ret_fwd(proj, cos_t, sin_t, ret_tabs, p["gn_w"])
    s.update(o_pre=o_pre, yr=yr, states=states)
    y, xn = _gate_out(yf, ys, yr, proj, x, p["w_out"])
    s["y"] = y
    return xn, s, landed


def _layer_bwd(dxn, s, p, rope, ride=None):
    L = dxn.shape[0]
    cos_t, sin_t, ret_tabs = rope
    g = {}
    proj = s["proj"]
    dyf, dys, dyr, dgate = _gate_out_bwd(dxn, p["w_out"], s["yf"], s["ys"], s["yr"], proj)
    g["w_out"] = _mm(s["y"], dxn, ta=True, name="dw_out")
    drq, drk, drv, dgn = _ret_bwd(proj, cos_t, sin_t, ret_tabs, p["gn_w"], s["o_pre"], dyr, s["states"])
    g["ret_gn_w"] = dgn.reshape(D_RET)
    dsu, g["wb"], g["wc"], g["lam"], dd, g["s5_w_glu"] = _s5_bwd(proj, s["ypre"], dys, s["xs"], p["wb"], p["wc"], p["lam"],
                                                                 p["d"], p["w_glu"])
    g["s5_d"] = dd.reshape(D_S5)
    dqt, dkraw, dv, landed = _fox_bwd(s["qa"], s["ka"], s["kat"], proj, dyf, s["yf"], s["lse"], ride)
    dq, dk, dfl, dbf = _fox_post_bwd(dqt, dkraw, proj, p["b_f"])
    g["fox_b_f"] = dbf[0, :FOX_HEADS]
    pieces = [dgate, dq, dk, dv, dsu, drq, drk, drv, dfl]
    dx, dnw = _inproj_bwd_dx(pieces, p["w_in"], s["x"], p["norm_w"], dxn)
    g["norm_w"] = dnw.reshape(D_MODEL)
    g["w_in"] = _dw_in(s["h"], pieces)
    return dx, g, landed


def _pad_w_in(w):
    z = jnp.zeros(w.shape[:-1] + (D_INP - O_FL - FOX_HEADS,), w.dtype)
    return jnp.concatenate([w[..., 2568:3592], w[..., 0:1536], w[..., 1544:2568], w[..., 1536:1544], z], axis=-1)


def _unpad_w_in(w):
    return jnp.concatenate([w[..., O_FQ:O_SU], w[..., O_FL:O_FL + FOX_HEADS], w[..., O_SU:O_FL], w[..., O_GATE:O_FQ]], axis=-1)


def _layer_params(l, w_in_p, w_glu, w_out, small, s5_mats):
    lam, wb, wc = s5_mats
    return dict(
        norm_w=small["norm_w"][l][None], w_in=w_in_p, b_f=jnp.pad(small["fox_b_f"][l], (0, PAIR - FOX_HEADS))[None],
        lam=lam[l], wb=wb[l], wc=wc[l], d=small["s5_d"][l][None], w_glu=w_glu, gn_w=small["ret_gn_w"][l][None], w_out=w_out)


_SHARDED = ("w_in", "s5_w_glu", "w_out")
_WIRE = jnp.bfloat16


def _gathered_weights(g_in, g_glu, g_out):
    return (_pad_w_in(jnp.moveaxis(g_in, 0, 1).reshape(D_MODEL, D_IN)), g_glu.reshape(D_S5, D_S5),
            g_out.reshape(D_MODEL, D_MODEL))


def _grad_slots(g):
    w_in = _unpad_w_in(g["w_in"]).reshape(D_MODEL, N_DEV, W_SHARD)
    return [jnp.moveaxis(w_in, 1, 0).astype(_WIRE), g["s5_w_glu"].reshape(N_DEV, D_S5 // N_DEV, D_S5).astype(_WIRE),
            g["w_out"].reshape(N_DEV, D_MODEL // N_DEV, D_MODEL).astype(_WIRE)]


def _step_grads(x, tgt, small, full=None, shards=None):
    L = x.shape[0]
    rope = _rope_tables(L) + (_ret_tables(min(TQ, L)),)
    s5_mats = _s5_mats(small)
    gather = [False] * len(_SHARDED)
    if shards is not None:
        nxt = _gathered_weights(*_exchange([s[0] for s in shards], gather, "gather_layer0"))
    saved, params = [], []
    for l in range(DEPTH):
        weights = nxt if shards is not None else tuple(f[l] for f in full)
        ride = ([s[l + 1] for s in shards], gather) if shards is not None and l + 1 < DEPTH else None
        params.append(_layer_params(l, *weights, small, s5_mats))
        x, s, landed = _layer_fwd(x, params[l], rope, ride)
        if ride is not None:
            nxt = _gathered_weights(*landed)
        saved.append(s)
    loss, dx, dfw = _final_loss(x, small["final_norm_w"][None], tgt)
    grads, partials, waiting = [None] * DEPTH, [None] * DEPTH, None
    scatter = [True] * len(_SHARDED)
    for l in reversed(range(DEPTH)):
        dx, grads[l], landed = _layer_bwd(dx, saved[l], params[l], rope, (waiting, scatter) if waiting is not None else None)
        if waiting is not None:
            partials[l + 1] = landed
        if shards is not None:
            waiting = _grad_slots(grads[l])
    stack = lambda n: jnp.stack([g[n] for g in grads])
    small_g = {n: stack(n) for n in ("norm_w", "fox_b_f", "s5_d", "ret_gn_w")}
    small_g.update(_s5_param_grads(small, stack("wb"), stack("wc"), stack("lam")), final_norm_w=dfw.reshape(D_MODEL))
    if shards is None:
        return loss, dx, grads, small_g
    packed = _pack([small_g[n] for n in _SMALL]).astype(_WIRE)
    landed = _exchange(waiting + [packed], scatter + [False], "exchange_layer0")
    partials[0] = landed[:-1]
    return loss, dx, grads, small_g, partials, landed[-1]


_MESH = pl.DeviceIdType.MESH
_ANY = pl.BlockSpec(memory_space=pl.ANY)


def _me_and_peers():
    x, y, c = lax.axis_index("x"), lax.axis_index("y"), lax.axis_index("c")
    flip = lambda a, bit: (1 - a) if bit else a
    peers = []
    for r in range(1, N_DEV):
        px, py, pc = flip(x, (r >> 2) & 1), flip(y, (r >> 1) & 1), flip(c, r & 1)
        peers.append(((px, py, pc), 4 * px + 2 * py + pc))
    return 4 * x + 2 * y + c, peers


def _exchange_copies(srcs, dsts, sems, scatter):
    send_sems, recv_sems, local_sems = sems
    me, peers = _me_and_peers()
    pick = lambda t, to: srcs[t].at[to] if scatter[t] else srcs[t]
    own = [pltpu.make_async_copy(pick(t, me), dsts[t].at[me], local_sems.at[t]) for t in range(len(srcs))]
    sends, waits = [], []
    for r, (dev, idx) in enumerate(peers):
        for t in range(len(srcs)):
            for land, out in ((me, sends), (idx, waits)):
                out.append(pltpu.make_async_remote_copy(pick(t, idx), dsts[t].at[land], send_sems.at[t, r], recv_sems.at[t, r],
                                                        device_id=dev, device_id_type=_MESH))
    return own, sends, waits


def _exchange_start(srcs, dsts, sems, scatter):
    own, sends, _ = _exchange_copies(srcs, dsts, sems, scatter)
    for cp in own + sends:
        cp.start()


def _exchange_wait(srcs, dsts, sems, scatter):
    own, _, waits = _exchange_copies(srcs, dsts, sems, scatter)
    for cp in waits + own:
        cp.wait()


def _exchange_shapes(arrs, scatter):
    outs = [jax.ShapeDtypeStruct(a.shape if sc else (N_DEV,) + a.shape, a.dtype) for a, sc in zip(arrs, scatter)]
    n = len(arrs)
    sems = [pltpu.SemaphoreType.DMA((n, N_DEV - 1)), pltpu.SemaphoreType.DMA((n, N_DEV - 1)), pltpu.SemaphoreType.DMA((n,))]
    return outs, sems


def _exchange(arrs, scatter, name):
    n = len(arrs)

    def body(*refs):
        _exchange_start(refs[:n], refs[n:2 * n], refs[2 * n:], scatter)
        _exchange_wait(refs[:n], refs[n:2 * n], refs[2 * n:], scatter)

    outs, sems = _exchange_shapes(arrs, scatter)
    return _pallas(body, out_shape=tuple(outs), in_specs=[_ANY] * n, out_specs=tuple([_ANY] * n), scratch_shapes=sems,
                   name=name)(*arrs)


def _riding(body, n_in, n_out, ride, is_first, is_last):
    if ride is None:
        return body, [], [], [], []
    arrs, scatter = ride
    n = len(arrs)
    outs, sems = _exchange_shapes(arrs, scatter)

    def wrapped(*refs):
        ins, srcs = refs[:n_in], refs[n_in:n_in + n]
        own_outs, dsts = refs[n_in + n:n_in + n + n_out], refs[n_in + n + n_out:n_in + 2 * n + n_out]
        scratch, ex_sems = refs[n_in + 2 * n + n_out:-3], refs[-3:]

        @pl.when(is_first())
        def _():
            _exchange_start(srcs, dsts, ex_sems, scatter)

        body(*ins, *own_outs, *scratch)

        @pl.when(is_last())
        def _():
            _exchange_wait(srcs, dsts, ex_sems, scatter)

    return wrapped, list(arrs), [_ANY] * n, outs, sems


def _adamw(parts, w, m, v, name):
    n, rows, cols = parts.shape
    tm = next(t for t in (256, 128, 64, 32, 16) if rows % t == 0)

    def body(p_ref, w_ref, m_ref, v_ref, g_ref, d_ref, nm_ref, nv_ref):
        g = p_ref[0].astype(F32)
        for i in range(1, n):
            g = g + p_ref[i].astype(F32)
        nm = ADAM_B1 * m_ref[...] + (1.0 - ADAM_B1) * g
        nv = ADAM_B2 * v_ref[...] + (1.0 - ADAM_B2) * jnp.square(g)
        m_hat = nm / (1.0 - ADAM_B1 ** ADAM_STEP)
        v_hat = nv / (1.0 - ADAM_B2 ** ADAM_STEP)
        g_ref[...] = g
        d_ref[...] = -ADAM_LR * (m_hat / (jnp.sqrt(v_hat) + ADAM_EPS) + ADAM_WD * w_ref[...])
        nm_ref[...] = nm
        nv_ref[...] = nv

    row = pl.BlockSpec((tm, cols), lambda i: (i, 0))
    return _pallas(body, out_shape=(jax.ShapeDtypeStruct((rows, cols), F32),) * 4, grid=(rows // tm,),
                   in_specs=[pl.BlockSpec((n, tm, cols), lambda i: (0, i, 0)), row, row, row], out_specs=(row,) * 4,
                   name=name)(parts, w, m, v)


_WEIGHTS = ("norm_w", "w_in", "fox_b_f", "s5_a_re", "s5_a_im", "s5_b_re", "s5_b_im", "s5_c_re", "s5_c_im", "s5_d",
            "s5_log_dt", "s5_w_glu", "ret_gn_w", "w_out", "final_norm_w")
_SMALL = tuple(n for n in _WEIGHTS if n not in _SHARDED)
_LANES = 128


def _pack(arrs):
    flat = jnp.concatenate([a.reshape(-1) for a in arrs])
    rows = -(-flat.shape[0] // (_LANES * _LANES)) * _LANES
    return jnp.pad(flat, (0, rows * _LANES - flat.shape[0])).reshape(rows, _LANES)


def _unpack(packed, like):
    flat, out, off = packed.reshape(-1), [], 0
    for a in like:
        out.append(flat[off:off + a.size].reshape(a.shape))
        off += a.size
    return out


def kernel(x, norm_w, w_in, fox_b_f, s5_a_re, s5_a_im, s5_b_re, s5_b_im, s5_c_re, s5_c_im, s5_d, s5_log_dt, s5_w_glu, ret_gn_w, w_out, final_norm_w, loss_target, m_norm_w, m_w_in, m_fox_b_f, m_s5_a_re, m_s5_a_im, m_s5_b_re, m_s5_b_im, m_s5_c_re, m_s5_c_im, m_s5_d, m_s5_log_dt, m_s5_w_glu, m_ret_gn_w, m_w_out, m_final_norm_w, v_norm_w, v_w_in, v_fox_b_f, v_s5_a_re, v_s5_a_im, v_s5_b_re, v_s5_b_im, v_s5_c_re, v_s5_c_im, v_s5_d, v_s5_log_dt, v_s5_w_glu, v_ret_gn_w, v_w_out, v_final_norm_w):
    w = dict(norm_w=norm_w, w_in=w_in, fox_b_f=fox_b_f, s5_a_re=s5_a_re, s5_a_im=s5_a_im, s5_b_re=s5_b_re, s5_b_im=s5_b_im,
             s5_c_re=s5_c_re, s5_c_im=s5_c_im, s5_d=s5_d, s5_log_dt=s5_log_dt, s5_w_glu=s5_w_glu, ret_gn_w=ret_gn_w,
             w_out=w_out, final_norm_w=final_norm_w)
    m = dict(norm_w=m_norm_w, w_in=m_w_in, fox_b_f=m_fox_b_f, s5_a_re=m_s5_a_re, s5_a_im=m_s5_a_im, s5_b_re=m_s5_b_re,
             s5_b_im=m_s5_b_im, s5_c_re=m_s5_c_re, s5_c_im=m_s5_c_im, s5_d=m_s5_d, s5_log_dt=m_s5_log_dt,
             s5_w_glu=m_s5_w_glu, ret_gn_w=m_ret_gn_w, w_out=m_w_out, final_norm_w=m_final_norm_w)
    v = dict(norm_w=v_norm_w, w_in=v_w_in, fox_b_f=v_fox_b_f, s5_a_re=v_s5_a_re, s5_a_im=v_s5_a_im, s5_b_re=v_s5_b_re,
             s5_b_im=v_s5_b_im, s5_c_re=v_s5_c_re, s5_c_im=v_s5_c_im, s5_d=v_s5_d, s5_log_dt=v_s5_log_dt,
             s5_w_glu=v_s5_w_glu, ret_gn_w=v_ret_gn_w, w_out=v_w_out, final_norm_w=v_final_norm_w)

    small = {n: w[n] for n in _SMALL}
    loss, dx, _, _, partials, r_small = _step_grads(x[0], loss_target[0], small, shards=[w[n].astype(_MXU) for n in _SHARDED])

    res = {}
    for t, n in enumerate(_SHARDED):
        cols = w[n].shape[-1]
        r = jnp.stack([partials[l][t] for l in range(DEPTH)], axis=1)
        outs = _adamw(r.reshape(N_DEV, -1, cols), w[n].reshape(-1, cols), m[n].reshape(-1, cols), v[n].reshape(-1, cols),
                      "adamw_" + n)
        res[n] = [o.reshape(w[n].shape) for o in outs]
    small_w = [w[n] for n in _SMALL]
    outs = _adamw(r_small, _pack(small_w), _pack([m[n] for n in _SMALL]), _pack([v[n] for n in _SMALL]), "adamw_small")
    for k, o in enumerate(outs):
        for n, a in zip(_SMALL, _unpack(o, small_w)):
            res.setdefault(n, [None] * 4)[k] = a

    loss = lax.psum(loss[0, 0], ("x", "y", "c"))
    return (loss, dx[None], *[res[n][0] for n in _WEIGHTS], *[res[n][1] for n in _WEIGHTS],
            *[res[n][2] for n in _WEIGHTS], *[res[n][3] for n in _WEIGHTS])
```

```python
import math

import jax
import jax.numpy as jnp
from jax import lax
from jax.experimental import pallas as pl
from jax.experimental.pallas import tpu as pltpu

F32 = jnp.float32
_MXU = jnp.bfloat16
_HI = lax.Precision.HIGHEST

N_DEV = 8
DEPTH = 4
D_MODEL = 1024
HEAD_DIM = 64
D_FOX = 512
FOX_HEADS = 8
D_S5 = 256
S5_GROUPS = 16
S5_GROUP_CH = 16
S5_STATE = 64
S5_CH = S5_GROUPS * S5_STATE
D_RET = 256
RET_HEADS = 4
CHUNK = 64
ROPE_BASE = 10000.0
EPS = 1e-6
D_IN = 3592
D_INP = 3712
W_SHARD = D_IN // N_DEV
O_GATE, O_FQ, O_FK, O_FV, O_SU, O_RQ, O_RK, O_RV, O_FL = 0, 1024, 1536, 2048, 2560, 2816, 3072, 3328, 3584

ADAM_LR, ADAM_B1, ADAM_B2, ADAM_EPS, ADAM_WD, ADAM_STEP = 0.001, 0.9, 0.999, 1e-08, 0.01, 10

TM = 256
TQ = 512
TS = 256
NEG = -1e30
VMEM_BIG = 56 * 1024 * 1024


def _pallas(body, **kw):
    return pl.pallas_call(body, **kw)


def _whole(shape):
    n = len(shape)
    return pl.BlockSpec(shape, lambda *_: (0,) * n)


def _rows(tm, width, col=0):
    return pl.BlockSpec((tm, width), lambda i: (i, col))


def _dot(a, b, dims=(((1,), (0,)), ((), ()))):
    return lax.dot_general(a.astype(_MXU), b.astype(_MXU), dims, preferred_element_type=F32)


_NT = (((1,), (1,)), ((), ()))
_TN = (((0,), (0,)), ((), ()))


def _mm(a, b, *, ta=False, tb=False, add=None, tm=512, tn=512, tk=512, name):
    m, k = (a.shape[1], a.shape[0]) if ta else a.shape
    n = b.shape[0] if tb else b.shape[1]
    tm, tn, tk = min(tm, m), min(tn, n), min(tk, k)
    assert m % tm == 0 and n % tn == 0 and k % tk == 0, (name, m, n, k)
    dims = (((0 if ta else 1,), (1 if tb else 0,)), ((), ()))

    def body(*refs):
        a_ref, b_ref = refs[0], refs[1]
        o_ref = refs[-1]
        p = _dot(a_ref[...], b_ref[...], dims)
        kk = pl.program_id(2)

        @pl.when(kk == 0)
        def _():
            o_ref[...] = p if add is None else p + refs[2][...]

        @pl.when(kk != 0)
        def _():
            o_ref[...] += p

    a_spec = pl.BlockSpec((tk, tm), lambda i, j, kk: (kk, i)) if ta else pl.BlockSpec((tm, tk), lambda i, j, kk: (i, kk))
    b_spec = pl.BlockSpec((tn, tk), lambda i, j, kk: (j, kk)) if tb else pl.BlockSpec((tk, tn), lambda i, j, kk: (kk, j))
    o_spec = pl.BlockSpec((tm, tn), lambda i, j, kk: (i, j))
    ins, specs = [a, b], [a_spec, b_spec]
    if add is not None:
        ins.append(add)
        specs.append(o_spec)
    return _pallas(body, out_shape=jax.ShapeDtypeStruct((m, n), F32), grid=(m // tm, n // tn, k // tk),
                   in_specs=specs, out_specs=o_spec, name=name,
                   compiler_params=pltpu.CompilerParams(vmem_limit_bytes=VMEM_BIG))(*ins)


def _norm_inproj(x, g, w):
    L = x.shape[0]

    def body(x_ref, g_ref, w_ref, p_ref, h_ref):
        xv = x_ref[...]
        r = lax.rsqrt(jnp.mean(xv * xv, axis=-1, keepdims=True) + EPS)
        h = (xv * r * g_ref[...]).astype(_MXU)
        h_ref[...] = h
        p_ref[...] = _dot(h, w_ref[...])

    return _pallas(body, out_shape=(jax.ShapeDtypeStruct((L, D_INP), F32), jax.ShapeDtypeStruct((L, D_MODEL), _MXU)),
                   grid=(L // TM,), in_specs=[_rows(TM, D_MODEL), _whole((1, D_MODEL)), _whole((D_MODEL, D_INP))],
                   out_specs=(_rows(TM, D_INP), _rows(TM, D_MODEL)), name="norm_inproj",
                   compiler_params=pltpu.CompilerParams(vmem_limit_bytes=VMEM_BIG))(x, g, w)


def _rms_bwd(xv, g, dh):
    r = lax.rsqrt(jnp.mean(xv * xv, axis=-1, keepdims=True) + EPS)
    xh = xv * r
    dg = jnp.sum(dh * xh, axis=0, keepdims=True)
    dxh = dh * g
    dx = r * (dxh - xh * jnp.mean(dxh * xh, axis=-1, keepdims=True))
    return dx, dg


def _inproj_bwd_dx(pieces, w, x, g, dres):
    L = x.shape[0]
    n = len(pieces)

    def body(*refs):
        w_ref, x_ref, g_ref, dr_ref, dx_ref, dg_ref = refs[n:]
        dproj = jnp.concatenate([r[...].astype(_MXU) for r in refs[:n]], axis=-1)
        dh = _dot(dproj, w_ref[...], _NT)
        dx, dg = _rms_bwd(x_ref[...], g_ref[...], dh)
        dx_ref[...] = dx + dr_ref[...]

        @pl.when(pl.program_id(0) == 0)
        def _():
            dg_ref[...] = dg

        @pl.when(pl.program_id(0) != 0)
        def _():
            dg_ref[...] += dg

    return _pallas(body, out_shape=(jax.ShapeDtypeStruct((L, D_MODEL), F32), jax.ShapeDtypeStruct((1, D_MODEL), F32)),
                   grid=(L // TM,),
                   in_specs=[_rows(TM, p.shape[1]) for p in pieces]
                   + [_whole((D_MODEL, D_INP)), _rows(TM, D_MODEL), _whole((1, D_MODEL)), _rows(TM, D_MODEL)],
                   out_specs=(_rows(TM, D_MODEL), _whole((1, D_MODEL))), name="inproj_bwd_dx",
                   compiler_params=pltpu.CompilerParams(vmem_limit_bytes=VMEM_BIG))(*pieces, w, x, g, dres)


def _dw_in(h, pieces):
    L = h.shape[0]
    n = len(pieces)

    def body(*refs):
        h_ref, o_ref = refs[n], refs[n + 1]
        dproj = jnp.concatenate([r[...].astype(_MXU) for r in refs[:n]], axis=-1)
        part = _dot(h_ref[...], dproj, _TN)

        @pl.when(pl.program_id(0) == 0)
        def _():
            o_ref[...] = part

        @pl.when(pl.program_id(0) != 0)
        def _():
            o_ref[...] += part

    return _pallas(body, out_shape=jax.ShapeDtypeStruct((D_MODEL, D_INP), F32), grid=(L // TM,),
                   in_specs=[_rows(TM, p.shape[1]) for p in pieces] + [_rows(TM, D_MODEL)],
                   out_specs=_whole((D_MODEL, D_INP)), name="dw_in",
                   compiler_params=pltpu.CompilerParams(vmem_limit_bytes=VMEM_BIG))(*pieces, h)


PAIR = 2 * HEAD_DIM
N_AUX = 3


def _own(shape, h):
    return lax.broadcasted_iota(jnp.int32, shape, len(shape) - 1) // HEAD_DIM == h


def _hi_dot(a, b):
    return jnp.dot(a, b, precision=_HI, preferred_element_type=F32)


def _tri(n, lower):
    r = lax.broadcasted_iota(jnp.int32, (n, n), 0)
    c = lax.broadcasted_iota(jnp.int32, (n, n), 1)
    return jnp.where(r >= c if lower else r <= c, 1.0, 0.0).astype(F32)


def _fox_cumsum(proj, b):
    L = proj.shape[0]

    def body(fl_ref, b_ref, c_ref, carry_sc):
        @pl.when(pl.program_id(0) == 0)
        def _():
            carry_sc[...] = jnp.zeros((1, PAIR), F32)

        lane = lax.broadcasted_iota(jnp.int32, (TM, PAIR), 1)
        lf = jnp.where(lane < FOX_HEADS, jax.nn.log_sigmoid(fl_ref[...] + b_ref[...]), 0.0)
        cs = _hi_dot(_tri(TM, True), lf) + carry_sc[...]
        c_ref[...] = cs
        carry_sc[...] = cs[TM - 1:TM, :]

    return _pallas(body, out_shape=jax.ShapeDtypeStruct((L, PAIR), F32), grid=(L // TM,),
                   in_specs=[_rows(TM, PAIR, O_FL // PAIR), _whole((1, PAIR))], out_specs=_rows(TM, PAIR),
                   scratch_shapes=[pltpu.VMEM((1, PAIR), F32)], name="fox_cumsum")(proj, b)


def _fox_prep(proj, c):
    L = proj.shape[0]

    def body(q_ref, k_ref, v_ref, c_ref, qa_ref, ka_ref, qat_ref, kat_ref, vt_ref):
        lane = lax.broadcasted_iota(jnp.int32, (TM, PAIR), 1)
        cv = c_ref[...]
        for p in range(FOX_HEADS // 2):
            cols = slice(PAIR * p, PAIR * (p + 1))
            q2, k2 = q_ref[:, cols], k_ref[:, cols]
            vt_ref[p] = v_ref[:, cols].T.astype(_MXU)
            for e in range(2):
                h = 2 * p + e
                own = lane // HEAD_DIM == e
                a = lane - (HEAD_DIM if e == 0 else 0)
                pick = (lax.broadcasted_iota(jnp.int32, (PAIR, PAIR), 0) == h).astype(F32)
                rest = _hi_dot(cv, pick)
                aux_q = jnp.where((a >= N_AUX) & (a < 2 * N_AUX), 1.0, 0.0)
                aux_k = jnp.where((a >= 0) & (a < N_AUX), 1.0, 0.0)
                for n in range(N_AUX):
                    part = rest.astype(_MXU).astype(F32)
                    rest = rest - part
                    aux_q = jnp.where(a == n, part, aux_q)
                    aux_k = jnp.where(a == N_AUX + n, -part, aux_k)
                ka = jnp.where(own, k2, aux_k)
                qa = jnp.where(own, q2 * (1.0 / math.sqrt(HEAD_DIM)), aux_q)
                qa_ref[h] = qa.astype(_MXU)
                ka_ref[h] = ka.astype(_MXU)
                qat_ref[h] = qa.T.astype(_MXU)
                kat_ref[h] = ka.T.astype(_MXU)

    hl = jax.ShapeDtypeStruct((FOX_HEADS, L, PAIR), _MXU)
    hlt = jax.ShapeDtypeStruct((FOX_HEADS, PAIR, L), _MXU)
    nat = lambda o: _rows(TM, D_FOX, o // D_FOX)
    rows = pl.BlockSpec((FOX_HEADS, TM, PAIR), lambda i: (0, i, 0))
    cols = pl.BlockSpec((FOX_HEADS, PAIR, TM), lambda i: (0, 0, i))
    return _pallas(
        body, out_shape=(hl, hl, hlt, hlt, jax.ShapeDtypeStruct((FOX_HEADS // 2, PAIR, L), _MXU)),
        grid=(L // TM,), in_specs=[nat(O_FQ), nat(O_FK), nat(O_FV), _rows(TM, PAIR)],
        out_specs=(rows, rows, cols, cols, pl.BlockSpec((FOX_HEADS // 2, PAIR, TM), lambda i: (0, 0, i))),
        name="fox_prep")(proj, proj, proj, c)


def _key_le_query(tq):
    return lax.broadcasted_iota(jnp.int32, (tq, tq), 0) <= lax.broadcasted_iota(jnp.int32, (tq, tq), 1)


def _grid_ends(n0, n1):
    first = lambda: (pl.program_id(0) == 0) & (pl.program_id(1) == 0)
    last = lambda: (pl.program_id(0) == n0 - 1) & (pl.program_id(1) == n1 - 1)
    return first, last


def _fox_fwd(qa, ka, vt, ride=None):
    H, L, _ = qa.shape
    tq = min(TQ, L)
    nq = L // tq

    def body(qa_ref, ka_ref, vt_ref, o_ref, lse_ref, m_sc, l_sc, acc_sc):
        i = pl.program_id(1)
        m_sc[...] = jnp.full((2, 1, tq), NEG, F32)
        l_sc[...] = jnp.zeros((2, 1, tq), F32)
        acc_sc[...] = jnp.zeros((2, PAIR, tq), F32)

        def block(j, nk, masked):
            keys = pl.ds(pl.multiple_of(j * tq, tq), nk * tq)
            vt_blk = vt_ref[:, keys]
            sts = [_dot(ka_ref[e, keys, :], qa_ref[e], _NT) for e in range(2)]
            pts, alphas = [], []
            for e in range(2):
                st = jnp.where(_key_le_query(tq), sts[e], NEG) if masked else sts[e]
                m_prev = m_sc[e]
                m_new = jnp.maximum(m_prev, jnp.max(st, axis=0, keepdims=True))
                alphas.append(jnp.exp(m_prev - m_new))
                pt = jnp.exp(st - m_new)
                l_sc[e] = alphas[e] * l_sc[e] + jnp.sum(pt, axis=0, keepdims=True)
                m_sc[e] = m_new
                pts.append(pt.astype(_MXU))
            for e in range(2):
                acc_sc[e] = alphas[e] * acc_sc[e] + _dot(vt_blk, pts[e])

        def two_blocks(jj, carry):
            block(2 * jj, 2, False)
            return carry

        lax.fori_loop(0, i // 2, two_blocks, 0)

        @pl.when(i % 2 == 1)
        def _():
            block(i - 1, 1, False)

        block(i, 1, True)
        row = lax.broadcasted_iota(jnp.int32, (PAIR, tq), 0)
        ot = jnp.where(row < HEAD_DIM, acc_sc[0] / l_sc[0], acc_sc[1] / l_sc[1])
        o_ref[...] = ot.T
        for e in range(2):
            lse_ref[e] = m_sc[e] + jnp.log(l_sc[e])

    body, ex_in, ex_specs, ex_out, ex_sems = _riding(body, 3, 2, ride, *_grid_ends(H // 2, nq))
    res = _pallas(
        body, out_shape=(jax.ShapeDtypeStruct((L, D_FOX), F32), jax.ShapeDtypeStruct((H, 1, L), F32), *ex_out),
        grid=(H // 2, nq),
        in_specs=[pl.BlockSpec((2, tq, PAIR), lambda p, i: (p, i, 0)), pl.BlockSpec((2, L, PAIR), lambda p, i: (p, 0, 0)),
                  pl.BlockSpec((None, PAIR, L), lambda p, i: (p, 0, 0)), *ex_specs],
        out_specs=(pl.BlockSpec((tq, PAIR), lambda p, i: (i, p)), pl.BlockSpec((2, 1, tq), lambda p, i: (p, 0, i)),
                   *ex_specs),
        scratch_shapes=[pltpu.VMEM((2, 1, tq), F32), pltpu.VMEM((2, 1, tq), F32), pltpu.VMEM((2, PAIR, tq), F32), *ex_sems],
        name="fox_fwd" if ride is None else "fox_fwd_gather")(qa, ka, vt, *ex_in)
    return res[0], res[1], list(res[2:])


def _fox_bwd(qa, qat, ka, kat, proj, do, o, lse, ride=None):
    H, L, _ = qa.shape
    tq = min(TQ, L)
    nq = L // tq

    def body(qa_ref, qat_ref, ka_ref, kat_ref, v_ref, do_ref, o_ref, lse_ref, dqt_ref, dkt_ref, dvt_ref,
             delta_sc, dot_sc, dk_sc, dv_sc):
        j = pl.program_id(1)

        @pl.when(j == 0)
        def _():
            head_rows = (lax.broadcasted_iota(jnp.int32, (8, PAIR), 1) // HEAD_DIM
                         == lax.broadcasted_iota(jnp.int32, (8, PAIR), 0)).astype(F32)
            dov = do_ref[...]
            delta_sc[...] = lax.dot_general(head_rows, dov * o_ref[...], _NT, precision=_HI, preferred_element_type=F32)
            dot_sc[...] = dov.T.astype(_MXU)
            dqt_ref[...] = jnp.zeros((2, PAIR, L), F32)

        dk_sc[...] = jnp.zeros((2, PAIR, tq), F32)
        dv_sc[...] = jnp.zeros((PAIR, tq), F32)
        vb = v_ref[...]
        head_of_row = lax.broadcasted_iota(jnp.int32, (PAIR, tq), 0) // HEAD_DIM

        def block(i, masked):
            qs = pl.ds(pl.multiple_of(i * tq, tq), tq)
            dob = do_ref[qs, :].astype(_MXU)
            dot_blk = dot_sc[:, qs]
            sts = [_dot(ka_ref[e], qa_ref[e, qs, :], _NT) for e in range(2)]
            dpts = [_dot(jnp.where(_own((tq, PAIR), e), vb, 0.0), dob, _NT) for e in range(2)]
            for e in range(2):
                pt = jnp.exp(sts[e] - lse_ref[e, :, qs])
                if masked:
                    pt = jnp.where(_key_le_query(tq), pt, 0.0)
                ds = (pt * (dpts[e] - delta_sc[e:e + 1, qs])).astype(_MXU)
                dv_sc[...] += _dot(jnp.where(head_of_row == e, dot_blk, 0.0), pt, _NT)
                dk_sc[e] += _dot(qat_ref[e, :, qs], ds, _NT)
                dqt_ref[e, :, qs] += _dot(kat_ref[e], ds)

        def off_diagonal(i, carry):
            block(i, False)
            return carry

        block(j, True)
        lax.fori_loop(j + 1, nq, off_diagonal, 0)
        dkt_ref[...] = dk_sc[...]
        dvt_ref[...] = dv_sc[...]

    nat = pl.BlockSpec((L, PAIR), lambda p, j: (0, p))
    whole_t = pl.BlockSpec((2, PAIR, L), lambda p, j: (p, 0, 0))
    block_t = pl.BlockSpec((2, PAIR, tq), lambda p, j: (p, 0, j))
    body, ex_in, ex_specs, ex_out, ex_sems = _riding(body, 8, 3, ride, *_grid_ends(H // 2, nq))
    res = _pallas(
        body, out_shape=(jax.ShapeDtypeStruct((H, PAIR, L), F32), jax.ShapeDtypeStruct((H, PAIR, L), F32),
                         jax.ShapeDtypeStruct((H // 2, PAIR, L), F32), *ex_out),
        grid=(H // 2, nq),
        in_specs=[pl.BlockSpec((2, L, PAIR), lambda p, j: (p, 0, 0)), whole_t, pl.BlockSpec((2, tq, PAIR), lambda p, j: (p, j, 0)),
                  block_t, pl.BlockSpec((tq, PAIR), lambda p, j: (j, O_FV // PAIR + p)), nat, nat,
                  pl.BlockSpec((2, 1, L), lambda p, j: (p, 0, 0)), *ex_specs],
        out_specs=(whole_t, block_t, pl.BlockSpec((None, PAIR, tq), lambda p, j: (p, 0, j)), *ex_specs),
        scratch_shapes=[pltpu.VMEM((8, L), F32), pltpu.VMEM((PAIR, L), _MXU), pltpu.VMEM((2, PAIR, tq), F32),
                        pltpu.VMEM((PAIR, tq), F32), *ex_sems],
        name="fox_bwd" if ride is None else "fox_bwd_exchange",
        compiler_params=pltpu.CompilerParams(vmem_limit_bytes=VMEM_BIG))(qa, qat, ka, kat, proj, do, o, lse, *ex_in)
    return res[0], res[1], res[2], list(res[3:])


def _fox_post_bwd(dqt, dkt, dvt, proj, b):
    L = proj.shape[0]
    nb = L // TM

    def body(dqt_ref, dkr_ref, dvt_ref, fl_ref, b_ref, dq_ref, dk_ref, dv_ref, dfl_ref, db_ref, carry_sc):
        first = pl.program_id(0) == 0

        @pl.when(first)
        def _():
            carry_sc[...] = jnp.zeros((1, PAIR), F32)

        lane = lax.broadcasted_iota(jnp.int32, (TM, PAIR), 1)
        rr = lax.broadcasted_iota(jnp.int32, (PAIR, PAIR), 0)
        cc = lax.broadcasted_iota(jnp.int32, (PAIR, PAIR), 1)
        dc = jnp.zeros((TM, PAIR), F32)
        for p in range(FOX_HEADS // 2):
            cols = slice(PAIR * p, PAIR * (p + 1))
            dqs = [dqt_ref[2 * p + e].T for e in range(2)]
            dks = [dkr_ref[2 * p + e].T for e in range(2)]
            dq_ref[:, cols] = jnp.where(lane < HEAD_DIM, dqs[0], dqs[1]) * (1.0 / math.sqrt(HEAD_DIM))
            dk_ref[:, cols] = jnp.where(lane < HEAD_DIM, dks[0], dks[1])
            dv_ref[:, cols] = dvt_ref[p].T
            for e in range(2):
                base = HEAD_DIM if e == 0 else 0
                to_head = cc == 2 * p + e
                dc = dc + _hi_dot(dqs[e], jnp.where((rr == base) & to_head, 1.0, 0.0))
                dc = dc + _hi_dot(dks[e], jnp.where((rr == base + N_AUX) & to_head, -1.0, 0.0))
        rs = _hi_dot(_tri(TM, False), dc) + carry_sc[...]
        carry_sc[...] = rs[0:1, :]
        dfl = jnp.where(lane < FOX_HEADS, rs * jax.nn.sigmoid(-(fl_ref[...] + b_ref[...])), 0.0)
        dfl_ref[...] = dfl
        db = jnp.sum(dfl, axis=0, keepdims=True)

        @pl.when(first)
        def _():
            db_ref[...] = db

        @pl.when(jnp.logical_not(first))
        def _():
            db_ref[...] += db

    rev = lambda i: nb - 1 - i
    nat = pl.BlockSpec((TM, D_FOX), lambda i: (rev(i), 0))
    heads_t = pl.BlockSpec((FOX_HEADS, PAIR, TM), lambda i: (0, 0, rev(i)))
    return _pallas(
        body, out_shape=(jax.ShapeDtypeStruct((L, D_FOX), F32),) * 3
        + (jax.ShapeDtypeStruct((L, PAIR), F32), jax.ShapeDtypeStruct((1, PAIR), F32)),
        grid=(nb,),
        in_specs=[heads_t, heads_t, pl.BlockSpec((FOX_HEADS // 2, PAIR, TM), lambda i: (0, 0, rev(i))),
                  pl.BlockSpec((TM, PAIR), lambda i: (rev(i), O_FL // PAIR)), _whole((1, PAIR))],
        out_specs=(nat, nat, nat, pl.BlockSpec((TM, PAIR), lambda i: (rev(i), 0)), _whole((1, PAIR))),
        scratch_shapes=[pltpu.VMEM((1, PAIR), F32)], name="fox_post_bwd")(dqt, dkt, dvt, proj, b)


def _s5_expand():
    r = lax.broadcasted_iota(jnp.int32, (S5_STATE, S5_STATE * S5_GROUP_CH), 0)
    c = lax.broadcasted_iota(jnp.int32, (S5_STATE, S5_STATE * S5_GROUP_CH), 1)
    return jnp.where(c // S5_GROUP_CH == r, 1.0, 0.0).astype(F32)


def _s5_disc_math(ar, ai, ldt, br, bi):
    dt = jnp.exp(ldt)
    mag = jnp.exp(ar * dt)
    lr = mag * jnp.cos(ai * dt)
    li = mag * jnp.sin(ai * dt)
    den = ar * ar + ai * ai
    fr = ((lr - 1.0) * ar + li * ai) / den
    fi = (li * ar - (lr - 1.0) * ai) / den
    e = _s5_expand()
    fre = jnp.dot(fr, e, precision=_HI, preferred_element_type=F32)
    fie = jnp.dot(fi, e, precision=_HI, preferred_element_type=F32)
    return lr, li, fre * br - fie * bi, fre * bi + fie * br


def _layer_blocks(arrs):
    return [pl.BlockSpec((None,) + a.shape[1:], lambda l: (l, 0, 0)) for a in arrs]


def _s5_disc(ar, ai, ldt, br, bi):
    def body(ar_ref, ai_ref, ldt_ref, br_ref, bi_ref, lr_ref, li_ref, bbr_ref, bbi_ref):
        lr, li, bbr, bbi = _s5_disc_math(ar_ref[...], ai_ref[...], ldt_ref[...], br_ref[...], bi_ref[...])
        lr_ref[...] = lr
        li_ref[...] = li
        bbr_ref[...] = bbr
        bbi_ref[...] = bbi

    ins = (ar, ai, ldt, br, bi)
    outs = (ar, ai, br, bi)
    return _pallas(body, out_shape=tuple(jax.ShapeDtypeStruct(a.shape, F32) for a in outs), grid=(DEPTH,),
                   in_specs=_layer_blocks(ins), out_specs=tuple(_layer_blocks(outs)), name="s5_disc")(*ins)


def _s5_disc_bwd(ar, ai, ldt, br, bi, dlr, dli, dbbr, dbbi):
    def body(ar_ref, ai_ref, ldt_ref, br_ref, bi_ref, dlr_ref, dli_ref, dbbr_ref, dbbi_ref,
             dar_ref, dai_ref, dldt_ref, dbr_ref, dbi_ref):
        _, vjp = jax.vjp(_s5_disc_math, ar_ref[...], ai_ref[...], ldt_ref[...], br_ref[...], bi_ref[...])
        dar, dai, dldt, dbr, dbi = vjp((dlr_ref[...], dli_ref[...], dbbr_ref[...], dbbi_ref[...]))
        dar_ref[...] = dar
        dai_ref[...] = dai
        dldt_ref[...] = dldt
        dbr_ref[...] = dbr
        dbi_ref[...] = dbi

    ins = (ar, ai, ldt, br, bi, dlr, dli, dbbr, dbbi)
    outs = (ar, ai, ldt, br, bi)
    return _pallas(body, out_shape=tuple(jax.ShapeDtypeStruct(a.shape, F32) for a in outs), grid=(DEPTH,),
                   in_specs=_layer_blocks(ins), out_specs=tuple(_layer_blocks(outs)), name="s5_disc_bwd")(*ins)


SLAB = 2 * S5_CH // 128


def _slab_rows(s, ts):
    return pl.ds(s, ts, stride=SLAB)


def _slab_pair(ref, s, ts):
    return jnp.concatenate([ref[_slab_rows(s, ts), :].astype(_MXU), ref[_slab_rows(s + 1, ts), :].astype(_MXU)], axis=-1)


def _s5_fwd(proj, wb, wc, lam, d, w_glu):
    L = proj.shape[0]
    ts = min(TS, L)

    def body(u_ref, wb_ref, wc_ref, lam_ref, d_ref, wg_ref, xs_ref, ypre_ref, ys_ref, b_sc, c_sc):
        @pl.when(pl.program_id(0) == 0)
        def _():
            c_sc[...] = jnp.zeros((SLAB, 128), F32)

        u = u_ref[...]
        ub = u.astype(_MXU)
        for s in range(0, SLAB, 2):
            b2 = _dot(ub, wb_ref[:, 128 * s:128 * (s + 2)])
            b_sc[_slab_rows(s, ts), :] = b2[:, :128]
            b_sc[_slab_rows(s + 1, ts), :] = b2[:, 128:]
        lr, li = lam_ref[0:8, :], lam_ref[8:16, :]

        def step(t, carry):
            xr, xi = carry
            row = pl.multiple_of(t * SLAB, SLAB)
            nr = lr * xr - li * xi + b_sc[pl.ds(row, 8), :]
            ni = lr * xi + li * xr + b_sc[pl.ds(row + 8, 8), :]
            xs_ref[pl.ds(row, 8), :] = nr
            xs_ref[pl.ds(row + 8, 8), :] = ni
            return nr, ni

        xr, xi = lax.fori_loop(0, ts, step, (c_sc[0:8, :], c_sc[8:16, :]), unroll=8)
        c_sc[0:8, :] = xr
        c_sc[8:16, :] = xi
        y = jnp.zeros((ts, D_S5), F32)
        for s in range(0, SLAB, 2):
            y = y + _dot(_slab_pair(xs_ref, s, ts), wc_ref[128 * s:128 * (s + 2), :])
        ypre_ref[...] = y
        y1 = jax.nn.gelu(y + d_ref[...] * u)
        ys_ref[...] = y1 * jax.nn.sigmoid(_dot(y1, wg_ref[...]))

    row = _rows(ts, D_S5)
    slabs = pl.BlockSpec((ts * SLAB, 128), lambda n: (n, 0))
    return _pallas(
        body, out_shape=(jax.ShapeDtypeStruct((L * SLAB, 128), F32), jax.ShapeDtypeStruct((L, D_S5), F32),
                         jax.ShapeDtypeStruct((L, D_S5), F32)),
        grid=(L // ts,),
        in_specs=[_rows(ts, D_S5, O_SU // D_S5), _whole((D_S5, 2 * S5_CH)), _whole((2 * S5_CH, D_S5)), _whole((SLAB, 128)),
                  _whole((1, D_S5)), _whole((D_S5, D_S5))],
        out_specs=(slabs, row, row),
        scratch_shapes=[pltpu.VMEM((ts * SLAB, 128), F32), pltpu.VMEM((SLAB, 128), F32)], name="s5_fwd")(
            proj, wb, wc, lam, d, w_glu)


def _s5_bwd(proj, ypre, dys, xs, wb, wc, lam, d, w_glu):
    L = proj.shape[0]
    ts = min(TS, L)
    nb = L // ts

    def body(u_ref, y_ref, dys_ref, xs_ref, xp_ref, wb_ref, wc_ref, lam_ref, d_ref, wg_ref,
             du_ref, dwb_ref, dwc_ref, dlam_ref, dd_ref, dwg_ref, dx_sc, g_sc, c_sc):
        n = pl.program_id(0)

        @pl.when(n == 0)
        def _():
            c_sc[...] = jnp.zeros((SLAB, 128), F32)
            dlam_ref[...] = jnp.zeros((SLAB, 128), F32)
            dwb_ref[...] = jnp.zeros((D_S5, 2 * S5_CH), F32)
            dwc_ref[...] = jnp.zeros((2 * S5_CH, D_S5), F32)
            dd_ref[...] = jnp.zeros((1, D_S5), F32)
            dwg_ref[...] = jnp.zeros((D_S5, D_S5), F32)

        u, dv, dout = u_ref[...], d_ref[...], dys_ref[...]
        y1, gelu_vjp = jax.vjp(jax.nn.gelu, y_ref[...] + dv * u)
        sg = jax.nn.sigmoid(_dot(y1, wg_ref[...]))
        dz = dout * y1 * sg * (1.0 - sg)
        dy, = gelu_vjp(dout * sg + _dot(dz, wg_ref[...], _NT))
        dd_ref[...] += jnp.sum(dy * u, axis=0, keepdims=True)
        dwg_ref[...] += _dot(y1, dz, _TN)
        dyb = dy.astype(_MXU)
        for s in range(0, SLAB, 2):
            cols = slice(128 * s, 128 * (s + 2))
            dx2 = _dot(dyb, wc_ref[cols, :], _NT)
            dx_sc[_slab_rows(s, ts), :] = dx2[:, :128]
            dx_sc[_slab_rows(s + 1, ts), :] = dx2[:, 128:]
            dwc_ref[cols, :] += _dot(_slab_pair(xs_ref, s, ts), dyb, _TN)
        lr, li = lam_ref[0:8, :], lam_ref[8:16, :]

        def adjoint(row, pr, pi, carry):
            gr, gi, ar, ai = carry
            nr = dx_sc[pl.ds(row, 8), :] + lr * gr + li * gi
            ni = dx_sc[pl.ds(row + 8, 8), :] - li * gr + lr * gi
            g_sc[pl.ds(row, 8), :] = nr
            g_sc[pl.ds(row + 8, 8), :] = ni
            return nr, ni, ar + nr * pr + ni * pi, ai - nr * pi + ni * pr

        def step(k, carry):
            row = pl.multiple_of((ts - 1 - k) * SLAB, SLAB)
            prev = pl.multiple_of((ts - 2 - k) * SLAB, SLAB)
            return adjoint(row, xs_ref[pl.ds(prev, 8), :], xs_ref[pl.ds(prev + 8, 8), :], carry)

        z = jnp.zeros((8, 128), F32)
        carry = lax.fori_loop(0, ts - 1, step, (c_sc[0:8, :], c_sc[8:16, :], z, z), unroll=8)
        has_prev = jnp.where(n == nb - 1, 0.0, 1.0)
        gr, gi, ar, ai = adjoint(0, xp_ref[0:8, :] * has_prev, xp_ref[8:16, :] * has_prev, carry)
        c_sc[0:8, :] = gr
        c_sc[8:16, :] = gi
        dlam_ref[0:8, :] += ar
        dlam_ref[8:16, :] += ai
        ub = u.astype(_MXU)
        du = dy * dv
        for s in range(0, SLAB, 2):
            cols = slice(128 * s, 128 * (s + 2))
            gs = _slab_pair(g_sc, s, ts)
            du = du + _dot(gs, wb_ref[:, cols], _NT)
            dwb_ref[:, cols] += _dot(ub, gs, _TN)
        du_ref[...] = du

    blk = lambda n: nb - 1 - n
    row = pl.BlockSpec((ts, D_S5), lambda n: (blk(n), 0))
    return _pallas(
        body, out_shape=(jax.ShapeDtypeStruct((L, D_S5), F32), jax.ShapeDtypeStruct((D_S5, 2 * S5_CH), F32),
                         jax.ShapeDtypeStruct((2 * S5_CH, D_S5), F32), jax.ShapeDtypeStruct((SLAB, 128), F32),
                         jax.ShapeDtypeStruct((1, D_S5), F32), jax.ShapeDtypeStruct((D_S5, D_S5), F32)),
        grid=(nb,),
        in_specs=[pl.BlockSpec((ts, D_S5), lambda n: (blk(n), O_SU // D_S5)), row, row,
                  pl.BlockSpec((ts * SLAB, 128), lambda n: (blk(n), 0)),
                  pl.BlockSpec((SLAB, 128), lambda n: (jnp.maximum(blk(n) * ts - 1, 0), 0)),
                  _whole((D_S5, 2 * S5_CH)), _whole((2 * S5_CH, D_S5)), _whole((SLAB, 128)), _whole((1, D_S5)),
                  _whole((D_S5, D_S5))],
        out_specs=(row, _whole((D_S5, 2 * S5_CH)), _whole((2 * S5_CH, D_S5)), _whole((SLAB, 128)), _whole((1, D_S5)),
                   _whole((D_S5, D_S5))),
        scratch_shapes=[pltpu.VMEM((ts * SLAB, 128), F32), pltpu.VMEM((ts * SLAB, 128), F32), pltpu.VMEM((SLAB, 128), F32)],
        name="s5_bwd")(proj, ypre, dys, xs, xs, wb, wc, lam, d, w_glu)


def _rot(z, cos, sin):
    lane = lax.broadcasted_iota(jnp.int32, z.shape, 1)
    zs = z * sin
    half = HEAD_DIM // 2
    return z * cos + jnp.where(lane % HEAD_DIM < half, pltpu.roll(zs, PAIR - half, 1), pltpu.roll(zs, half, 1))


def _head_avg():
    r = lax.broadcasted_iota(jnp.int32, (PAIR, PAIR), 0) // HEAD_DIM
    c = lax.broadcasted_iota(jnp.int32, (PAIR, PAIR), 1) // HEAD_DIM
    return jnp.where(r == c, 1.0 / HEAD_DIM, 0.0).astype(F32)


def _ret_tables(tq):
    lg = jnp.log1p(-(2.0 ** (-5.0 - jnp.arange(RET_HEADS, dtype=F32))))
    scale = 1.0 / math.sqrt(HEAD_DIM)
    pos = jnp.arange(tq)
    n = pos.astype(F32)
    dist = jnp.abs(n[:, None] - n[None, :])
    ok = (pos[None, :] // CHUNK) <= (pos[:, None] // CHUNK)
    w = jnp.where(ok[None], scale * jnp.exp(lg[:, None, None] * dist[None]), 0.0)
    lgl = jnp.repeat(lg, HEAD_DIM)
    dq_tab = scale * jnp.exp(lgl[None, :] * (n[:, None] + 1.0))
    dk_tab = jnp.exp(lgl[None, :] * (tq - 1.0 - n[:, None]))
    blk = jnp.arange(PAIR) // HEAD_DIM
    bd = (blk[:, None] == blk[None, :]).astype(F32)
    gbd = bd[None] * jnp.exp(lgl.reshape(RET_HEADS // 2, PAIR)[:, :, None] * tq)
    return dict(w=w, wt=w.transpose(0, 2, 1), dq=dq_tab, dk=dk_tab, gbd=gbd, bd=bd)


def _ret_specs(tq, nq, rev):
    blk = (lambda i: nq - 1 - i) if rev else (lambda i: i)
    col = lambda o: pl.BlockSpec((tq, PAIR), lambda p, i: (blk(i), o // PAIR + p))
    return dict(
        rq=col(O_RQ), rk=col(O_RK), rv=col(O_RV), nat=col(0),
        w=pl.BlockSpec((2, tq, tq), lambda p, i: (p, 0, 0)), tab=pl.BlockSpec((tq, PAIR), lambda p, i: (0, p)),
        gbd=pl.BlockSpec((None, PAIR, PAIR), lambda p, i: (p, 0, 0)), bd=pl.BlockSpec((PAIR, PAIR), lambda p, i: (0, 0)),
        gn=pl.BlockSpec((1, PAIR), lambda p, i: (0, p)),
        st=pl.BlockSpec((None, None, PAIR, PAIR), lambda p, i: (p, blk(i), 0, 0)))


def _ret_fwd(proj, cos_t, sin_t, tabs, gn):
    L = proj.shape[0]
    tq = tabs["w"].shape[1]
    nq = L // tq

    def body(rq_ref, rk_ref, rv_ref, cos_ref, sin_ref, w_ref, dqt_ref, dkt_ref, gbd_ref, bd_ref, gn_ref,
             o_ref, y_ref, st_ref, s_sc):
        @pl.when(pl.program_id(1) == 0)
        def _():
            s_sc[...] = jnp.zeros((PAIR, PAIR), F32)

        state = s_sc[...]
        st_ref[...] = state
        cos, sin = cos_ref[...], sin_ref[...]
        q2, k2, v2 = _rot(rq_ref[...], cos, sin), _rot(rk_ref[...], cos, sin), rv_ref[...]
        o = _dot(q2 * dqt_ref[...], state)
        for h in range(2):
            own = _own((tq, PAIR), h)
            a = _dot(jnp.where(own, q2, 0.0), k2, _NT) * w_ref[h]
            o = o + _dot(a, jnp.where(own, v2, 0.0))
        s_sc[...] = gbd_ref[...] * state + bd_ref[...] * _dot(k2 * dkt_ref[...], v2, _TN)
        o_ref[...] = o
        avg = _head_avg()
        oc = o - _hi_dot(o, avg)
        y_ref[...] = oc * lax.rsqrt(_hi_dot(oc * oc, avg) + EPS) * gn_ref[...]

    sp = _ret_specs(tq, nq, False)
    nat = jax.ShapeDtypeStruct((L, D_RET), F32)
    return _pallas(
        body, out_shape=(nat, nat, jax.ShapeDtypeStruct((RET_HEADS // 2, nq, PAIR, PAIR), F32)), grid=(RET_HEADS // 2, nq),
        in_specs=[sp["rq"], sp["rk"], sp["rv"], sp["nat"], sp["nat"], sp["w"], sp["tab"], sp["tab"], sp["gbd"], sp["bd"],
                  sp["gn"]],
        out_specs=(sp["nat"], sp["nat"], sp["st"]), scratch_shapes=[pltpu.VMEM((PAIR, PAIR), F32)],
        name="ret_fwd")(proj, proj, proj, cos_t, sin_t, tabs["w"], tabs["dq"], tabs["dk"], tabs["gbd"], tabs["bd"], gn)


def _ret_bwd(proj, cos_t, sin_t, tabs, gn, o_pre, dy, states):
    L = proj.shape[0]
    tq = tabs["w"].shape[1]
    nq = L // tq

    def body(rq_ref, rk_ref, rv_ref, cos_ref, sin_ref, w_ref, wt_ref, dqt_ref, dkt_ref, gbd_ref, bd_ref, gn_ref,
             o_ref, dy_ref, st_ref, drq_ref, drk_ref, drv_ref, dgn_ref, g_sc):
        first = pl.program_id(1) == 0

        @pl.when(first)
        def _():
            g_sc[...] = jnp.zeros((PAIR, PAIR), F32)

        cos, sin = cos_ref[...], sin_ref[...]
        q2, k2, v2 = _rot(rq_ref[...], cos, sin), _rot(rk_ref[...], cos, sin), rv_ref[...]
        avg = _head_avg()
        ov, dyv = o_ref[...], dy_ref[...]
        oc = ov - _hi_dot(ov, avg)
        r = lax.rsqrt(_hi_dot(oc * oc, avg) + EPS)
        oh = oc * r
        dgn = jnp.sum(dyv * oh, axis=0, keepdims=True)
        doh = dyv * gn_ref[...]
        do = r * (doh - _hi_dot(doh, avg) - oh * _hi_dot(doh * oh, avg))
        state, g = st_ref[...], g_sc[...]
        dqt, dkt = dqt_ref[...], dkt_ref[...]
        dq = _dot(do, state, _NT) * dqt
        dk = _dot(v2, g, _NT) * dkt
        dv = _dot(k2 * dkt, g)
        g_sc[...] = gbd_ref[...] * g + bd_ref[...] * _dot(q2 * dqt, do, _TN)
        for h in range(2):
            own = _own((tq, PAIR), h)
            qm, dom = jnp.where(own, q2, 0.0), jnp.where(own, do, 0.0)
            dv = dv + _dot(_dot(k2, qm, _NT) * wt_ref[h], dom)
            dq = dq + _dot(_dot(dom, v2, _NT) * w_ref[h], jnp.where(own, k2, 0.0))
            dk = dk + _dot(_dot(v2, dom, _NT) * wt_ref[h], qm)
        drq_ref[...] = _rot(dq, cos, -sin)
        drk_ref[...] = _rot(dk, cos, -sin)
        drv_ref[...] = dv

        @pl.when(first)
        def _():
            dgn_ref[...] = dgn

        @pl.when(jnp.logical_not(first))
        def _():
            dgn_ref[...] += dgn

    sp = _ret_specs(tq, nq, True)
    nat = jax.ShapeDtypeStruct((L, D_RET), F32)
    return _pallas(
        body, out_shape=(nat, nat, nat, jax.ShapeDtypeStruct((1, D_RET), F32)), grid=(RET_HEADS // 2, nq),
        in_specs=[sp["rq"], sp["rk"], sp["rv"], sp["nat"], sp["nat"], sp["w"], sp["w"], sp["tab"], sp["tab"], sp["gbd"],
                  sp["bd"], sp["gn"], sp["nat"], sp["nat"], sp["st"]],
        out_specs=(sp["nat"], sp["nat"], sp["nat"], sp["gn"]), scratch_shapes=[pltpu.VMEM((PAIR, PAIR), F32)],
        name="ret_bwd")(proj, proj, proj, cos_t, sin_t, tabs["w"], tabs["wt"], tabs["dq"], tabs["dk"], tabs["gbd"],
                        tabs["bd"], gn, o_pre, dy, states)


def _gate_out(yf, ys, yr, proj, x, w):
    L = x.shape[0]

    def body(yf_ref, ys_ref, yr_ref, g_ref, x_ref, w_ref, y_ref, xn_ref):
        cat = jnp.concatenate([yf_ref[...], ys_ref[...], yr_ref[...]], axis=-1)
        y = cat * jax.nn.silu(g_ref[...])
        y_ref[...] = y
        xn_ref[...] = x_ref[...] + _dot(y, w_ref[...])

    full = _rows(TM, D_MODEL)
    return _pallas(body, out_shape=(jax.ShapeDtypeStruct((L, D_MODEL), F32),) * 2, grid=(L // TM,),
                   in_specs=[_rows(TM, D_FOX), _rows(TM, D_S5), _rows(TM, D_RET), _rows(TM, D_MODEL, O_GATE // D_MODEL),
                             full, _whole((D_MODEL, D_MODEL))],
                   out_specs=(full, full), name="gate_out")(yf, ys, yr, proj, x, w)


def _gate_out_bwd(dxn, w, yf, ys, yr, proj):
    L = dxn.shape[0]

    def body(dx_ref, w_ref, yf_ref, ys_ref, yr_ref, g_ref, dyf_ref, dys_ref, dyr_ref, dg_ref):
        dy = _dot(dx_ref[...], w_ref[...], _NT)
        g = g_ref[...]
        sg = jax.nn.sigmoid(g)
        dcat = dy * (g * sg)
        dyf_ref[...] = dcat[:, :D_FOX]
        dys_ref[...] = dcat[:, D_FOX:D_FOX + D_S5]
        dyr_ref[...] = dcat[:, D_FOX + D_S5:]
        cat = jnp.concatenate([yf_ref[...], ys_ref[...], yr_ref[...]], axis=-1)
        dg_ref[...] = dy * cat * (sg * (1.0 + g * (1.0 - sg)))

    full = _rows(TM, D_MODEL)
    f, s, r = _rows(TM, D_FOX), _rows(TM, D_S5), _rows(TM, D_RET)
    return _pallas(body, out_shape=(jax.ShapeDtypeStruct((L, D_FOX), F32), jax.ShapeDtypeStruct((L, D_S5), F32),
                                    jax.ShapeDtypeStruct((L, D_RET), F32), jax.ShapeDtypeStruct((L, D_MODEL), F32)),
                   grid=(L // TM,),
                   in_specs=[full, _whole((D_MODEL, D_MODEL)), f, s, r, _rows(TM, D_MODEL, O_GATE // D_MODEL)],
                   out_specs=(f, s, r, full), name="gate_out_bwd")(dxn, w, yf, ys, yr, proj)


def _final_loss(x, g, tgt):
    L = x.shape[0]

    def body(x_ref, g_ref, t_ref, loss_ref, dx_ref, dg_ref):
        xv, gv = x_ref[...], g_ref[...]
        r = lax.rsqrt(jnp.mean(xv * xv, axis=-1, keepdims=True) + EPS)
        err = xv * r * gv - t_ref[...]
        part = 0.5 * jnp.sum(jnp.mean(err * err, axis=-1, keepdims=True), axis=0, keepdims=True)
        dx, dg = _rms_bwd(xv, gv, err * (1.0 / D_MODEL))
        dx_ref[...] = dx

        @pl.when(pl.program_id(0) == 0)
        def _():
            loss_ref[...] = part
            dg_ref[...] = dg

        @pl.when(pl.program_id(0) != 0)
        def _():
            loss_ref[...] += part
            dg_ref[...] += dg

    full = _rows(TM, D_MODEL)
    return _pallas(body, out_shape=(jax.ShapeDtypeStruct((1, 1), F32), jax.ShapeDtypeStruct((L, D_MODEL), F32),
                                    jax.ShapeDtypeStruct((1, D_MODEL), F32)),
                   grid=(L // TM,), in_specs=[full, _whole((1, D_MODEL)), full],
                   out_specs=(_whole((1, 1)), full, _whole((1, D_MODEL))), name="final_loss")(x, g, tgt)


def _block_diag(blocks):
    n, g, r, c = blocks.shape
    eye = jnp.eye(g, dtype=blocks.dtype)
    return (blocks[:, :, :, None, :] * eye[None, :, None, :, None]).reshape(n, g * r, g * c)


def _diag_blocks(m, g):
    n, r, c = m.shape[0], m.shape[1] // g, m.shape[2] // g
    eye = jnp.eye(g, dtype=m.dtype)
    return jnp.sum(m.reshape(n, g, r, g, c) * eye[None, :, None, :, None], axis=3)


def _rope_tables(L):
    half = HEAD_DIM // 2
    freqs = ROPE_BASE ** (-jnp.arange(half, dtype=F32) / half)
    ang = jnp.arange(L, dtype=F32)[:, None] * freqs[None, :]
    cos, sin = jnp.cos(ang), jnp.sin(ang)
    cos_t = jnp.tile(jnp.concatenate([cos, cos], axis=-1), (1, RET_HEADS))
    sin_t = jnp.tile(jnp.concatenate([sin, -sin], axis=-1), (1, RET_HEADS))
    return cos_t, sin_t


def _s5_disc_args(small):
    g, s, ch = S5_GROUPS, S5_STATE, S5_GROUP_CH
    return (small["s5_a_re"], small["s5_a_im"], small["s5_log_dt"][:, :, None],
            small["s5_b_re"].reshape(DEPTH, g, s * ch), small["s5_b_im"].reshape(DEPTH, g, s * ch))


def _s5_mats(small):
    g, s, ch = S5_GROUPS, S5_STATE, S5_GROUP_CH
    lr, li, bbr, bbi = _s5_disc(*_s5_disc_args(small))
    lam = jnp.concatenate([lr.reshape(DEPTH, 8, 128), li.reshape(DEPTH, 8, 128)], axis=1)
    wb = jnp.concatenate([_block_diag(b.reshape(DEPTH, g, s, ch).transpose(0, 1, 3, 2)) for b in (bbr, bbi)], axis=2)
    wc = jnp.concatenate([_block_diag(c.transpose(0, 1, 3, 2)) for c in (small["s5_c_re"], -small["s5_c_im"])], axis=1)
    return lam, wb.astype(_MXU), wc.astype(_MXU)


def _s5_param_grads(small, dwb, dwc, dlam):
    g, s, ch = S5_GROUPS, S5_STATE, S5_GROUP_CH
    dc = [_diag_blocks(m, g).transpose(0, 1, 3, 2) for m in (dwc[:, :S5_CH], dwc[:, S5_CH:])]
    dbb = [_diag_blocks(m, g).transpose(0, 1, 3, 2).reshape(DEPTH, g, s * ch) for m in (dwb[:, :, :S5_CH], dwb[:, :, S5_CH:])]
    dar, dai, dldt, dbr, dbi = _s5_disc_bwd(*_s5_disc_args(small), dlam[:, :8].reshape(DEPTH, g, s),
                                            dlam[:, 8:].reshape(DEPTH, g, s), dbb[0], dbb[1])
    shp = (DEPTH, g, s, ch)
    return dict(s5_a_re=dar, s5_a_im=dai, s5_log_dt=dldt.reshape(DEPTH, g), s5_b_re=dbr.reshape(shp),
                s5_b_im=dbi.reshape(shp), s5_c_re=dc[0], s5_c_im=-dc[1])


def _layer_fwd(x, p, rope, ride=None):
    L = x.shape[0]
    cos_t, sin_t, ret_tabs = rope
    s = {"x": x}
    proj, h = _norm_inproj(x, p["norm_w"], p["w_in"])
    s["proj"], s["h"] = proj, h
    qa, ka, qat, kat, vt = _fox_prep(proj, _fox_cumsum(proj, p["b_f"]))
    yf, lse, landed = _fox_fwd(qa, ka, vt, ride)
    s.update(qa=qa, ka=ka, qat=qat, kat=kat, lse=lse, yf=yf)
    xs, ypre, ys = _s5_fwd(proj, p["wb"], p["wc"], p["lam"], p["d"], p["w_glu"])
    s.update(xs=xs, ypre=ypre, ys=ys)
    o_pre, yr, states = _ret_fwd(proj, cos_t, sin_t, ret_tabs, p["gn_w"])
    s.update(o_pre=o_pre, yr=yr, states=states)
    y, xn = _gate_out(yf, ys, yr, proj, x, p["w_out"])
    s["y"] = y
    return xn, s, landed


def _layer_bwd(dxn, s, p, rope, ride=None):
    L = dxn.shape[0]
    cos_t, sin_t, ret_tabs = rope
    g = {}
    proj = s["proj"]
    dyf, dys, dyr, dgate = _gate_out_bwd(dxn, p["w_out"], s["yf"], s["ys"], s["yr"], proj)
    g["w_out"] = _mm(s["y"], dxn, ta=True, name="dw_out")
    drq, drk, drv, dgn = _ret_bwd(proj, cos_t, sin_t, ret_tabs, p["gn_w"], s["o_pre"], dyr, s["states"])
    g["ret_gn_w"] = dgn.reshape(D_RET)
    dsu, g["wb"], g["wc"], g["lam"], dd, g["s5_w_glu"] = _s5_bwd(proj, s["ypre"], dys, s["xs"], p["wb"], p["wc"], p["lam"],
                                                                 p["d"], p["w_glu"])
    g["s5_d"] = dd.reshape(D_S5)
    dqt, dkt, dvt, landed = _fox_bwd(s["qa"], s["qat"], s["ka"], s["kat"], proj, dyf, s["yf"], s["lse"], ride)
    dq, dk, dv, dfl, dbf = _fox_post_bwd(dqt, dkt, dvt, proj, p["b_f"])
    g["fox_b_f"] = dbf[0, :FOX_HEADS]
    pieces = [dgate, dq, dk, dv, dsu, drq, drk, drv, dfl]
    dx, dnw = _inproj_bwd_dx(pieces, p["w_in"], s["x"], p["norm_w"], dxn)
    g["norm_w"] = dnw.reshape(D_MODEL)
    g["w_in"] = _dw_in(s["h"], pieces)
    return dx, g, landed


def _pad_w_in(w):
    z = jnp.zeros(w.shape[:-1] + (D_INP - O_FL - FOX_HEADS,), w.dtype)
    return jnp.concatenate([w[..., 2568:3592], w[..., 0:1536], w[..., 1544:2568], w[..., 1536:1544], z], axis=-1)


def _unpad_w_in(w):
    return jnp.concatenate([w[..., O_FQ:O_SU], w[..., O_FL:O_FL + FOX_HEADS], w[..., O_SU:O_FL], w[..., O_GATE:O_FQ]], axis=-1)


def _layer_params(l, w_in_p, w_glu, w_out, small, s5_mats):
    lam, wb, wc = s5_mats
    return dict(
        norm_w=small["norm_w"][l][None], w_in=w_in_p, b_f=jnp.pad(small["fox_b_f"][l], (0, PAIR - FOX_HEADS))[None],
        lam=lam[l], wb=wb[l], wc=wc[l], d=small["s5_d"][l][None], w_glu=w_glu, gn_w=small["ret_gn_w"][l][None], w_out=w_out)


_SHARDED = ("w_in", "s5_w_glu", "w_out")
_WIRE = jnp.bfloat16


def _gathered_weights(g_in, g_glu, g_out):
    return (_pad_w_in(jnp.moveaxis(g_in, 0, 1).reshape(D_MODEL, D_IN)), g_glu.reshape(D_S5, D_S5),
            g_out.reshape(D_MODEL, D_MODEL))


def _grad_slots(g):
    w_in = _unpad_w_in(g["w_in"]).reshape(D_MODEL, N_DEV, W_SHARD)
    return [jnp.moveaxis(w_in, 1, 0).astype(_WIRE), g["s5_w_glu"].reshape(N_DEV, D_S5 // N_DEV, D_S5).astype(_WIRE),
            g["w_out"].reshape(N_DEV, D_MODEL // N_DEV, D_MODEL).astype(_WIRE)]


def _step_grads(x, tgt, small, full=None, shards=None):
    L = x.shape[0]
    rope = _rope_tables(L) + (_ret_tables(min(TQ, L)),)
    s5_mats = _s5_mats(small)
    gather = [False] * len(_SHARDED)
    if shards is not None:
        nxt = _gathered_weights(*_exchange([s[0] for s in shards], gather, "gather_layer0"))
    saved, params = [], []
    for l in range(DEPTH):
        weights = nxt if shards is not None else tuple(f[l] for f in full)
        ride = ([s[l + 1] for s in shards], gather) if shards is not None and l + 1 < DEPTH else None
        params.append(_layer_params(l, *weights, small, s5_mats))
        x, s, landed = _layer_fwd(x, params[l], rope, ride)
        if ride is not None:
            nxt = _gathered_weights(*landed)
        saved.append(s)
    loss, dx, dfw = _final_loss(x, small["final_norm_w"][None], tgt)
    grads, partials, waiting = [None] * DEPTH, [None] * DEPTH, None
    scatter = [True] * len(_SHARDED)
    for l in reversed(range(DEPTH)):
        dx, grads[l], landed = _layer_bwd(dx, saved[l], params[l], rope, (waiting, scatter) if waiting is not None else None)
        if waiting is not None:
            partials[l + 1] = landed
        if shards is not None:
            waiting = _grad_slots(grads[l])
    stack = lambda n: jnp.stack([g[n] for g in grads])
    small_g = {n: stack(n) for n in ("norm_w", "fox_b_f", "s5_d", "ret_gn_w")}
    small_g.update(_s5_param_grads(small, stack("wb"), stack("wc"), stack("lam")), final_norm_w=dfw.reshape(D_MODEL))
    if shards is None:
        return loss, dx, grads, small_g
    packed = _pack([small_g[n] for n in _SMALL]).astype(_WIRE)
    landed = _exchange(waiting + [packed], scatter + [False], "exchange_layer0")
    partials[0] = landed[:-1]
    return loss, dx, grads, small_g, partials, landed[-1]


_MESH = pl.DeviceIdType.MESH
_ANY = pl.BlockSpec(memory_space=pl.ANY)


def _me_and_peers():
    x, y, c = lax.axis_index("x"), lax.axis_index("y"), lax.axis_index("c")
    flip = lambda a, bit: (1 - a) if bit else a
    peers = []
    for r in range(1, N_DEV):
        px, py, pc = flip(x, (r >> 2) & 1), flip(y, (r >> 1) & 1), flip(c, r & 1)
        peers.append(((px, py, pc), 4 * px + 2 * py + pc))
    return 4 * x + 2 * y + c, peers


def _exchange_copies(srcs, dsts, sems, scatter):
    send_sems, recv_sems, local_sems = sems
    me, peers = _me_and_peers()
    pick = lambda t, to: srcs[t].at[to] if scatter[t] else srcs[t]
    own = [pltpu.make_async_copy(pick(t, me), dsts[t].at[me], local_sems.at[t]) for t in range(len(srcs))]
    sends, waits = [], []
    for r, (dev, idx) in enumerate(peers):
        for t in range(len(srcs)):
            for land, out in ((me, sends), (idx, waits)):
                out.append(pltpu.make_async_remote_copy(pick(t, idx), dsts[t].at[land], send_sems.at[t, r], recv_sems.at[t, r],
                                                        device_id=dev, device_id_type=_MESH))
    return own, sends, waits


def _exchange_start(srcs, dsts, sems, scatter):
    own, sends, _ = _exchange_copies(srcs, dsts, sems, scatter)
    for cp in own + sends:
        cp.start()


def _exchange_wait(srcs, dsts, sems, scatter):
    own, _, waits = _exchange_copies(srcs, dsts, sems, scatter)
    for cp in waits + own:
        cp.wait()


def _exchange_shapes(arrs, scatter):
    outs = [jax.ShapeDtypeStruct(a.shape if sc else (N_DEV,) + a.shape, a.dtype) for a, sc in zip(arrs, scatter)]
    n = len(arrs)
    sems = [pltpu.SemaphoreType.DMA((n, N_DEV - 1)), pltpu.SemaphoreType.DMA((n, N_DEV - 1)), pltpu.SemaphoreType.DMA((n,))]
    return outs, sems


def _exchange(arrs, scatter, name):
    n = len(arrs)

    def body(*refs):
        _exchange_start(refs[:n], refs[n:2 * n], refs[2 * n:], scatter)
        _exchange_wait(refs[:n], refs[n:2 * n], refs[2 * n:], scatter)

    outs, sems = _exchange_shapes(arrs, scatter)
    return _pallas(body, out_shape=tuple(outs), in_specs=[_ANY] * n, out_specs=tuple([_ANY] * n), scratch_shapes=sems,
                   name=name)(*arrs)


def _riding(body, n_in, n_out, ride, is_first, is_last):
    if ride is None:
        return body, [], [], [], []
    arrs, scatter = ride
    n = len(arrs)
    outs, sems = _exchange_shapes(arrs, scatter)

    def wrapped(*refs):
        ins, srcs = refs[:n_in], refs[n_in:n_in + n]
        own_outs, dsts = refs[n_in + n:n_in + n + n_out], refs[n_in + n + n_out:n_in + 2 * n + n_out]
        scratch, ex_sems = refs[n_in + 2 * n + n_out:-3], refs[-3:]

        @pl.when(is_first())
        def _():
            _exchange_start(srcs, dsts, ex_sems, scatter)

        body(*ins, *own_outs, *scratch)

        @pl.when(is_last())
        def _():
            _exchange_wait(srcs, dsts, ex_sems, scatter)

    return wrapped, list(arrs), [_ANY] * n, outs, sems


def _adamw(parts, w, m, v, name):
    n, rows, cols = parts.shape
    tm = next(t for t in (256, 128, 64, 32, 16) if rows % t == 0)

    def body(p_ref, w_ref, m_ref, v_ref, g_ref, d_ref, nm_ref, nv_ref):
        g = p_ref[0].astype(F32)
        for i in range(1, n):
            g = g + p_ref[i].astype(F32)
        nm = ADAM_B1 * m_ref[...] + (1.0 - ADAM_B1) * g
        nv = ADAM_B2 * v_ref[...] + (1.0 - ADAM_B2) * jnp.square(g)
        m_hat = nm / (1.0 - ADAM_B1 ** ADAM_STEP)
        v_hat = nv / (1.0 - ADAM_B2 ** ADAM_STEP)
        g_ref[...] = g
        d_ref[...] = -ADAM_LR * (m_hat / (jnp.sqrt(v_hat) + ADAM_EPS) + ADAM_WD * w_ref[...])
        nm_ref[...] = nm
        nv_ref[...] = nv

    row = pl.BlockSpec((tm, cols), lambda i: (i, 0))
    return _pallas(body, out_shape=(jax.ShapeDtypeStruct((rows, cols), F32),) * 4, grid=(rows // tm,),
                   in_specs=[pl.BlockSpec((n, tm, cols), lambda i: (0, i, 0)), row, row, row], out_specs=(row,) * 4,
                   name=name)(parts, w, m, v)


_WEIGHTS = ("norm_w", "w_in", "fox_b_f", "s5_a_re", "s5_a_im", "s5_b_re", "s5_b_im", "s5_c_re", "s5_c_im", "s5_d",
            "s5_log_dt", "s5_w_glu", "ret_gn_w", "w_out", "final_norm_w")
_SMALL = tuple(n for n in _WEIGHTS if n not in _SHARDED)
_LANES = 128


def _pack(arrs):
    flat = jnp.concatenate([a.reshape(-1) for a in arrs])
    rows = -(-flat.shape[0] // (_LANES * _LANES)) * _LANES
    return jnp.pad(flat, (0, rows * _LANES - flat.shape[0])).reshape(rows, _LANES)


def _unpack(packed, like):
    flat, out, off = packed.reshape(-1), [], 0
    for a in like:
        out.append(flat[off:off + a.size].reshape(a.shape))
        off += a.size
    return out


def kernel(x, norm_w, w_in, fox_b_f, s5_a_re, s5_a_im, s5_b_re, s5_b_im, s5_c_re, s5_c_im, s5_d, s5_log_dt, s5_w_glu, ret_gn_w, w_out, final_norm_w, loss_target, m_norm_w, m_w_in, m_fox_b_f, m_s5_a_re, m_s5_a_im, m_s5_b_re, m_s5_b_im, m_s5_c_re, m_s5_c_im, m_s5_d, m_s5_log_dt, m_s5_w_glu, m_ret_gn_w, m_w_out, m_final_norm_w, v_norm_w, v_w_in, v_fox_b_f, v_s5_a_re, v_s5_a_im, v_s5_b_re, v_s5_b_im, v_s5_c_re, v_s5_c_im, v_s5_d, v_s5_log_dt, v_s5_w_glu, v_ret_gn_w, v_w_out, v_final_norm_w):
    w = dict(norm_w=norm_w, w_in=w_in, fox_b_f=fox_b_f, s5_a_re=s5_a_re, s5_a_im=s5_a_im, s5_b_re=s5_b_re, s5_b_im=s5_b_im,
             s5_c_re=s5_c_re, s5_c_im=s5_c_im, s5_d=s5_d, s5_log_dt=s5_log_dt, s5_w_glu=s5_w_glu, ret_gn_w=ret_gn_w,
             w_out=w_out, final_norm_w=final_norm_w)
    m = dict(norm_w=m_norm_w, w_in=m_w_in, fox_b_f=m_fox_b_f, s5_a_re=m_s5_a_re, s5_a_im=m_s5_a_im, s5_b_re=m_s5_b_re,
             s5_b_im=m_s5_b_im, s5_c_re=m_s5_c_re, s5_c_im=m_s5_c_im, s5_d=m_s5_d, s5_log_dt=m_s5_log_dt,
             s5_w_glu=m_s5_w_glu, ret_gn_w=m_ret_gn_w, w_out=m_w_out, final_norm_w=m_final_norm_w)
    v = dict(norm_w=v_norm_w, w_in=v_w_in, fox_b_f=v_fox_b_f, s5_a_re=v_s5_a_re, s5_a_im=v_s5_a_im, s5_b_re=v_s5_b_re,
             s5_b_im=v_s5_b_im, s5_c_re=v_s5_c_re, s5_c_im=v_s5_c_im, s5_d=v_s5_d, s5_log_dt=v_s5_log_dt,
             s5_w_glu=v_s5_w_glu, ret_gn_w=v_ret_gn_w, w_out=v_w_out, final_norm_w=v_final_norm_w)

    small = {n: w[n] for n in _SMALL}
    loss, dx, _, _, partials, r_small = _step_grads(x[0], loss_target[0], small, shards=[w[n].astype(_MXU) for n in _SHARDED])

    res = {}
    for t, n in enumerate(_SHARDED):
        cols = w[n].shape[-1]
        r = jnp.stack([partials[l][t] for l in range(DEPTH)], axis=1)
        outs = _adamw(r.reshape(N_DEV, -1, cols), w[n].reshape(-1, cols), m[n].reshape(-1, cols), v[n].reshape(-1, cols),
                      "adamw_" + n)
        res[n] = [o.reshape(w[n].shape) for o in outs]
    small_w = [w[n] for n in _SMALL]
    outs = _adamw(r_small, _pack(small_w), _pack([m[n] for n in _SMALL]), _pack([v[n] for n in _SMALL]), "adamw_small")
    for k, o in enumerate(outs):
        for n, a in zip(_SMALL, _unpack(o, small_w)):
            res.setdefault(n, [None] * 4)[k] = a

    loss = lax.psum(loss[0, 0], ("x", "y", "c"))
    return (loss, dx[None], *[res[n][0] for n in _WEIGHTS], *[res[n][1] for n in _WEIGHTS],
            *[res[n][2] for n in _WEIGHTS], *[res[n][3] for n in _WEIGHTS])
```

```python
import math

import jax
import jax.numpy as jnp
from jax import lax
from jax.experimental import pallas as pl
from jax.experimental.pallas import tpu as pltpu

F32 = jnp.float32
_MXU = jnp.bfloat16
_HI = lax.Precision.HIGHEST

N_DEV = 8
DEPTH = 4
D_MODEL = 1024
HEAD_DIM = 64
D_FOX = 512
FOX_HEADS = 8
D_S5 = 256
S5_GROUPS = 16
S5_GROUP_CH = 16
S5_STATE = 64
S5_CH = S5_GROUPS * S5_STATE
D_RET = 256
RET_HEADS = 4
CHUNK = 64
ROPE_BASE = 10000.0
EPS = 1e-6
D_IN = 3592
D_INP = 3712
W_SHARD = D_IN // N_DEV
O_GATE, O_FQ, O_FK, O_FV, O_SU, O_RQ, O_RK, O_RV, O_FL = 0, 1024, 1536, 2048, 2560, 2816, 3072, 3328, 3584

ADAM_LR, ADAM_B1, ADAM_B2, ADAM_EPS, ADAM_WD, ADAM_STEP = 0.001, 0.9, 0.999, 1e-08, 0.01, 10

TM = 256
TQ = 512
TS = 256
NEG = -1e30
VMEM_BIG = 56 * 1024 * 1024


def _pallas(body, **kw):
    return pl.pallas_call(body, **kw)


def _whole(shape):
    n = len(shape)
    return pl.BlockSpec(shape, lambda *_: (0,) * n)


def _rows(tm, width, col=0):
    return pl.BlockSpec((tm, width), lambda i: (i, col))


def _dot(a, b, dims=(((1,), (0,)), ((), ()))):
    return lax.dot_general(a.astype(_MXU), b.astype(_MXU), dims, preferred_element_type=F32)


_NT = (((1,), (1,)), ((), ()))
_TN = (((0,), (0,)), ((), ()))


def _mm(a, b, *, ta=False, tb=False, add=None, tm=512, tn=512, tk=512, name):
    m, k = (a.shape[1], a.shape[0]) if ta else a.shape
    n = b.shape[0] if tb else b.shape[1]
    tm, tn, tk = min(tm, m), min(tn, n), min(tk, k)
    assert m % tm == 0 and n % tn == 0 and k % tk == 0, (name, m, n, k)
    dims = (((0 if ta else 1,), (1 if tb else 0,)), ((), ()))

    def body(*refs):
        a_ref, b_ref = refs[0], refs[1]
        o_ref = refs[-1]
        p = _dot(a_ref[...], b_ref[...], dims)
        kk = pl.program_id(2)

        @pl.when(kk == 0)
        def _():
            o_ref[...] = p if add is None else p + refs[2][...]

        @pl.when(kk != 0)
        def _():
            o_ref[...] += p

    a_spec = pl.BlockSpec((tk, tm), lambda i, j, kk: (kk, i)) if ta else pl.BlockSpec((tm, tk), lambda i, j, kk: (i, kk))
    b_spec = pl.BlockSpec((tn, tk), lambda i, j, kk: (j, kk)) if tb else pl.BlockSpec((tk, tn), lambda i, j, kk: (kk, j))
    o_spec = pl.BlockSpec((tm, tn), lambda i, j, kk: (i, j))
    ins, specs = [a, b], [a_spec, b_spec]
    if add is not None:
        ins.append(add)
        specs.append(o_spec)
    return _pallas(body, out_shape=jax.ShapeDtypeStruct((m, n), F32), grid=(m // tm, n // tn, k // tk),
                   in_specs=specs, out_specs=o_spec, name=name,
                   compiler_params=pltpu.CompilerParams(vmem_limit_bytes=VMEM_BIG))(*ins)


def _norm_inproj(x, g, w):
    L = x.shape[0]

    def body(x_ref, g_ref, w_ref, p_ref, h_ref):
        xv = x_ref[...]
        r = lax.rsqrt(jnp.mean(xv * xv, axis=-1, keepdims=True) + EPS)
        h = (xv * r * g_ref[...]).astype(_MXU)
        h_ref[...] = h
        p_ref[...] = _dot(h, w_ref[...])

    return _pallas(body, out_shape=(jax.ShapeDtypeStruct((L, D_INP), F32), jax.ShapeDtypeStruct((L, D_MODEL), _MXU)),
                   grid=(L // TM,), in_specs=[_rows(TM, D_MODEL), _whole((1, D_MODEL)), _whole((D_MODEL, D_INP))],
                   out_specs=(_rows(TM, D_INP), _rows(TM, D_MODEL)), name="norm_inproj",
                   compiler_params=pltpu.CompilerParams(vmem_limit_bytes=VMEM_BIG))(x, g, w)


def _rms_bwd(xv, g, dh):
    r = lax.rsqrt(jnp.mean(xv * xv, axis=-1, keepdims=True) + EPS)
    xh = xv * r
    dg = jnp.sum(dh * xh, axis=0, keepdims=True)
    dxh = dh * g
    dx = r * (dxh - xh * jnp.mean(dxh * xh, axis=-1, keepdims=True))
    return dx, dg


def _inproj_bwd_dx(pieces, w, x, g, dres):
    L = x.shape[0]
    n = len(pieces)

    def body(*refs):
        w_ref, x_ref, g_ref, dr_ref, dx_ref, dg_ref = refs[n:]
        dproj = jnp.concatenate([r[...].astype(_MXU) for r in refs[:n]], axis=-1)
        dh = _dot(dproj, w_ref[...], _NT)
        dx, dg = _rms_bwd(x_ref[...], g_ref[...], dh)
        dx_ref[...] = dx + dr_ref[...]

        @pl.when(pl.program_id(0) == 0)
        def _():
            dg_ref[...] = dg

        @pl.when(pl.program_id(0) != 0)
        def _():
            dg_ref[...] += dg

    return _pallas(body, out_shape=(jax.ShapeDtypeStruct((L, D_MODEL), F32), jax.ShapeDtypeStruct((1, D_MODEL), F32)),
                   grid=(L // TM,),
                   in_specs=[_rows(TM, p.shape[1]) for p in pieces]
                   + [_whole((D_MODEL, D_INP)), _rows(TM, D_MODEL), _whole((1, D_MODEL)), _rows(TM, D_MODEL)],
                   out_specs=(_rows(TM, D_MODEL), _whole((1, D_MODEL))), name="inproj_bwd_dx",
                   compiler_params=pltpu.CompilerParams(vmem_limit_bytes=VMEM_BIG))(*pieces, w, x, g, dres)


def _dw_in(h, pieces):
    L = h.shape[0]
    n = len(pieces)

    def body(*refs):
        h_ref, o_ref = refs[n], refs[n + 1]
        dproj = jnp.concatenate([r[...].astype(_MXU) for r in refs[:n]], axis=-1)
        part = _dot(h_ref[...], dproj, _TN)

        @pl.when(pl.program_id(0) == 0)
        def _():
            o_ref[...] = part

        @pl.when(pl.program_id(0) != 0)
        def _():
            o_ref[...] += part

    return _pallas(body, out_shape=jax.ShapeDtypeStruct((D_MODEL, D_INP), F32), grid=(L // TM,),
                   in_specs=[_rows(TM, p.shape[1]) for p in pieces] + [_rows(TM, D_MODEL)],
                   out_specs=_whole((D_MODEL, D_INP)), name="dw_in",
                   compiler_params=pltpu.CompilerParams(vmem_limit_bytes=VMEM_BIG))(*pieces, h)


PAIR = 2 * HEAD_DIM
N_AUX = 3


def _own(shape, h):
    return lax.broadcasted_iota(jnp.int32, shape, len(shape) - 1) // HEAD_DIM == h


def _hi_dot(a, b):
    return jnp.dot(a, b, precision=_HI, preferred_element_type=F32)


def _tri(n, lower):
    r = lax.broadcasted_iota(jnp.int32, (n, n), 0)
    c = lax.broadcasted_iota(jnp.int32, (n, n), 1)
    return jnp.where(r >= c if lower else r <= c, 1.0, 0.0).astype(F32)


def _fox_cumsum(proj, b):
    L = proj.shape[0]

    def body(fl_ref, b_ref, c_ref, carry_sc):
        @pl.when(pl.program_id(0) == 0)
        def _():
            carry_sc[...] = jnp.zeros((1, PAIR), F32)

        lane = lax.broadcasted_iota(jnp.int32, (TM, PAIR), 1)
        lf = jnp.where(lane < FOX_HEADS, jax.nn.log_sigmoid(fl_ref[...] + b_ref[...]), 0.0)
        cs = _hi_dot(_tri(TM, True), lf) + carry_sc[...]
        c_ref[...] = cs
        carry_sc[...] = cs[TM - 1:TM, :]

    return _pallas(body, out_shape=jax.ShapeDtypeStruct((L, PAIR), F32), grid=(L // TM,),
                   in_specs=[_rows(TM, PAIR, O_FL // PAIR), _whole((1, PAIR))], out_specs=_rows(TM, PAIR),
                   scratch_shapes=[pltpu.VMEM((1, PAIR), F32)], name="fox_cumsum")(proj, b)


def _fox_prep(proj, c):
    L = proj.shape[0]

    def body(q_ref, k_ref, v_ref, c_ref, qa_ref, ka_ref, kat_ref, vt_ref):
        lane = lax.broadcasted_iota(jnp.int32, (TM, PAIR), 1)
        cv = c_ref[...]
        for p in range(FOX_HEADS // 2):
            cols = slice(PAIR * p, PAIR * (p + 1))
            q2, k2 = q_ref[:, cols], k_ref[:, cols]
            vt_ref[p] = v_ref[:, cols].T.astype(_MXU)
            for e in range(2):
                h = 2 * p + e
                own = lane // HEAD_DIM == e
                a = lane - (HEAD_DIM if e == 0 else 0)
                pick = (lax.broadcasted_iota(jnp.int32, (PAIR, PAIR), 0) == h).astype(F32)
                rest = _hi_dot(cv, pick)
                aux_q = jnp.where((a >= N_AUX) & (a < 2 * N_AUX), 1.0, 0.0)
                aux_k = jnp.where((a >= 0) & (a < N_AUX), 1.0, 0.0)
                for n in range(N_AUX):
                    part = rest.astype(_MXU).astype(F32)
                    rest = rest - part
                    aux_q = jnp.where(a == n, part, aux_q)
                    aux_k = jnp.where(a == N_AUX + n, -part, aux_k)
                ka = jnp.where(own, k2, aux_k)
                qa_ref[h] = jnp.where(own, q2 * (1.0 / math.sqrt(HEAD_DIM)), aux_q).astype(_MXU)
                ka_ref[h] = ka.astype(_MXU)
                kat_ref[h] = ka.T.astype(_MXU)

    hl = jax.ShapeDtypeStruct((FOX_HEADS, L, PAIR), _MXU)
    nat = lambda o: _rows(TM, D_FOX, o // D_FOX)
    rows = pl.BlockSpec((FOX_HEADS, TM, PAIR), lambda i: (0, i, 0))
    return _pallas(
        body, out_shape=(hl, hl, jax.ShapeDtypeStruct((FOX_HEADS, PAIR, L), _MXU),
                         jax.ShapeDtypeStruct((FOX_HEADS // 2, PAIR, L), _MXU)),
        grid=(L // TM,), in_specs=[nat(O_FQ), nat(O_FK), nat(O_FV), _rows(TM, PAIR)],
        out_specs=(rows, rows, pl.BlockSpec((FOX_HEADS, PAIR, TM), lambda i: (0, 0, i)),
                   pl.BlockSpec((FOX_HEADS // 2, PAIR, TM), lambda i: (0, 0, i))),
        name="fox_prep")(proj, proj, proj, c)


def _key_le_query(tq):
    return lax.broadcasted_iota(jnp.int32, (tq, tq), 0) <= lax.broadcasted_iota(jnp.int32, (tq, tq), 1)


def _grid_ends(n0, n1):
    first = lambda: (pl.program_id(0) == 0) & (pl.program_id(1) == 0)
    last = lambda: (pl.program_id(0) == n0 - 1) & (pl.program_id(1) == n1 - 1)
    return first, last


def _fox_fwd(qa, ka, vt, ride=None):
    H, L, _ = qa.shape
    tq = min(TQ, L)
    nq = L // tq

    def body(qa_ref, ka_ref, vt_ref, o_ref, lse_ref, m_sc, l_sc, acc_sc):
        i = pl.program_id(1)
        m_sc[...] = jnp.full((2, 1, tq), NEG, F32)
        l_sc[...] = jnp.zeros((2, 1, tq), F32)
        acc_sc[...] = jnp.zeros((2, PAIR, tq), F32)

        def block(j, nk, masked):
            keys = pl.ds(pl.multiple_of(j * tq, tq), nk * tq)
            vt_blk = vt_ref[:, keys]
            sts = [_dot(ka_ref[e, keys, :], qa_ref[e], _NT) for e in range(2)]
            pts, alphas = [], []
            for e in range(2):
                st = jnp.where(_key_le_query(tq), sts[e], NEG) if masked else sts[e]
                m_prev = m_sc[e]
                m_new = jnp.maximum(m_prev, jnp.max(st, axis=0, keepdims=True))
                alphas.append(jnp.exp(m_prev - m_new))
                pt = jnp.exp(st - m_new)
                l_sc[e] = alphas[e] * l_sc[e] + jnp.sum(pt, axis=0, keepdims=True)
                m_sc[e] = m_new
                pts.append(pt.astype(_MXU))
            for e in range(2):
                acc_sc[e] = alphas[e] * acc_sc[e] + _dot(vt_blk, pts[e])

        def two_blocks(jj, carry):
            block(2 * jj, 2, False)
            return carry

        lax.fori_loop(0, i // 2, two_blocks, 0)

        @pl.when(i % 2 == 1)
        def _():
            block(i - 1, 1, False)

        block(i, 1, True)
        row = lax.broadcasted_iota(jnp.int32, (PAIR, tq), 0)
        ot = jnp.where(row < HEAD_DIM, acc_sc[0] / l_sc[0], acc_sc[1] / l_sc[1])
        o_ref[...] = ot.T
        for e in range(2):
            lse_ref[e] = m_sc[e] + jnp.log(l_sc[e])

    body, ex_in, ex_specs, ex_out, ex_sems = _riding(body, 3, 2, ride, *_grid_ends(H // 2, nq))
    res = _pallas(
        body, out_shape=(jax.ShapeDtypeStruct((L, D_FOX), F32), jax.ShapeDtypeStruct((H, 1, L), F32), *ex_out),
        grid=(H // 2, nq),
        in_specs=[pl.BlockSpec((2, tq, PAIR), lambda p, i: (p, i, 0)), pl.BlockSpec((2, L, PAIR), lambda p, i: (p, 0, 0)),
                  pl.BlockSpec((None, PAIR, L), lambda p, i: (p, 0, 0)), *ex_specs],
        out_specs=(pl.BlockSpec((tq, PAIR), lambda p, i: (i, p)), pl.BlockSpec((2, 1, tq), lambda p, i: (p, 0, i)),
                   *ex_specs),
        scratch_shapes=[pltpu.VMEM((2, 1, tq), F32), pltpu.VMEM((2, 1, tq), F32), pltpu.VMEM((2, PAIR, tq), F32), *ex_sems],
        name="fox_fwd" if ride is None else "fox_fwd_gather")(qa, ka, vt, *ex_in)
    return res[0], res[1], list(res[2:])


def _fox_bwd(qa, ka, kat, proj, do, o, lse, ride=None):
    H, L, _ = qa.shape
    tq = min(TQ, L)
    nq = L // tq

    def body(qa_ref, ka_ref, kat_ref, v_ref, do_ref, o_ref, lse_ref, dqt_ref, dk_ref, dv_ref, delta_sc, dk_sc, dv_sc):
        j = pl.program_id(1)

        @pl.when(j == 0)
        def _():
            head_rows = (lax.broadcasted_iota(jnp.int32, (8, PAIR), 1) // HEAD_DIM
                         == lax.broadcasted_iota(jnp.int32, (8, PAIR), 0)).astype(F32)
            delta_sc[...] = lax.dot_general(head_rows, do_ref[...] * o_ref[...], _NT, precision=_HI,
                                            preferred_element_type=F32)
            dqt_ref[...] = jnp.zeros((2, PAIR, L), F32)

        dk_sc[...] = jnp.zeros((2, tq, PAIR), F32)
        dv_sc[...] = jnp.zeros((tq, PAIR), F32)
        vb = v_ref[...]

        def block(i, masked):
            qs = pl.ds(pl.multiple_of(i * tq, tq), tq)
            dob = do_ref[qs, :]
            for e in range(2):
                own = _own((tq, PAIR), e)
                qh = qa_ref[e, qs, :]
                pt = jnp.exp(_dot(ka_ref[e], qh, _NT) - lse_ref[e, :, qs])
                if masked:
                    pt = jnp.where(_key_le_query(tq), pt, 0.0)
                dv_sc[...] += _dot(pt, jnp.where(own, dob, 0.0))
                dpt = _dot(jnp.where(own, vb, 0.0), dob, _NT)
                ds = (pt * (dpt - delta_sc[e:e + 1, qs])).astype(_MXU)
                dk_sc[e] += _dot(ds, qh)
                dqt_ref[e, :, qs] += _dot(kat_ref[e], ds)

        def off_diagonal(i, carry):
            block(i, False)
            return carry

        block(j, True)
        lax.fori_loop(j + 1, nq, off_diagonal, 0)
        dk_ref[...] = dk_sc[...]
        dv_ref[...] = dv_sc[...]

    nat = pl.BlockSpec((L, PAIR), lambda p, j: (0, p))
    body, ex_in, ex_specs, ex_out, ex_sems = _riding(body, 7, 3, ride, *_grid_ends(H // 2, nq))
    res = _pallas(
        body, out_shape=(jax.ShapeDtypeStruct((H, PAIR, L), F32), jax.ShapeDtypeStruct((H, L, PAIR), F32),
                         jax.ShapeDtypeStruct((L, D_FOX), F32), *ex_out),
        grid=(H // 2, nq),
        in_specs=[pl.BlockSpec((2, L, PAIR), lambda p, j: (p, 0, 0)), pl.BlockSpec((2, tq, PAIR), lambda p, j: (p, j, 0)),
                  pl.BlockSpec((2, PAIR, tq), lambda p, j: (p, 0, j)),
                  pl.BlockSpec((tq, PAIR), lambda p, j: (j, O_FV // PAIR + p)), nat, nat,
                  pl.BlockSpec((2, 1, L), lambda p, j: (p, 0, 0)), *ex_specs],
        out_specs=(pl.BlockSpec((2, PAIR, L), lambda p, j: (p, 0, 0)), pl.BlockSpec((2, tq, PAIR), lambda p, j: (p, j, 0)),
                   pl.BlockSpec((tq, PAIR), lambda p, j: (j, p)), *ex_specs),
        scratch_shapes=[pltpu.VMEM((8, L), F32), pltpu.VMEM((2, tq, PAIR), F32), pltpu.VMEM((tq, PAIR), F32), *ex_sems],
        name="fox_bwd" if ride is None else "fox_bwd_exchange",
        compiler_params=pltpu.CompilerParams(vmem_limit_bytes=VMEM_BIG))(qa, ka, kat, proj, do, o, lse, *ex_in)
    return res[0], res[1], res[2], list(res[3:])


def _fox_post_bwd(dqt, dkraw, proj, b):
    L = proj.shape[0]
    nb = L // TM

    def body(dqt_ref, dkr_ref, fl_ref, b_ref, dq_ref, dk_ref, dfl_ref, db_ref, carry_sc):
        first = pl.program_id(0) == 0

        @pl.when(first)
        def _():
            carry_sc[...] = jnp.zeros((1, PAIR), F32)

        lane = lax.broadcasted_iota(jnp.int32, (TM, PAIR), 1)
        rr = lax.broadcasted_iota(jnp.int32, (PAIR, PAIR), 0)
        cc = lax.broadcasted_iota(jnp.int32, (PAIR, PAIR), 1)
        dc = jnp.zeros((TM, PAIR), F32)
        for p in range(FOX_HEADS // 2):
            cols = slice(PAIR * p, PAIR * (p + 1))
            dqs = [dqt_ref[2 * p + e].T for e in range(2)]
            dks = [dkr_ref[2 * p + e] for e in range(2)]
            dq_ref[:, cols] = jnp.where(lane < HEAD_DIM, dqs[0], dqs[1]) * (1.0 / math.sqrt(HEAD_DIM))
            dk_ref[:, cols] = jnp.where(lane < HEAD_DIM, dks[0], dks[1])
            for e in range(2):
                base = HEAD_DIM if e == 0 else 0
                to_head = cc == 2 * p + e
                dc = dc + _hi_dot(dqs[e], jnp.where((rr == base) & to_head, 1.0, 0.0))
                dc = dc + _hi_dot(dks[e], jnp.where((rr == base + N_AUX) & to_head, -1.0, 0.0))
        rs = _hi_dot(_tri(TM, False), dc) + carry_sc[...]
        carry_sc[...] = rs[0:1, :]
        dfl = jnp.where(lane < FOX_HEADS, rs * jax.nn.sigmoid(-(fl_ref[...] + b_ref[...])), 0.0)
        dfl_ref[...] = dfl
        db = jnp.sum(dfl, axis=0, keepdims=True)

        @pl.when(first)
        def _():
            db_ref[...] = db

        @pl.when(jnp.logical_not(first))
        def _():
            db_ref[...] += db

    rev = lambda i: nb - 1 - i
    nat = pl.BlockSpec((TM, D_FOX), lambda i: (rev(i), 0))
    return _pallas(
        body, out_shape=(jax.ShapeDtypeStruct((L, D_FOX), F32),) * 2
        + (jax.ShapeDtypeStruct((L, PAIR), F32), jax.ShapeDtypeStruct((1, PAIR), F32)),
        grid=(nb,),
        in_specs=[pl.BlockSpec((FOX_HEADS, PAIR, TM), lambda i: (0, 0, rev(i))),
                  pl.BlockSpec((FOX_HEADS, TM, PAIR), lambda i: (0, rev(i), 0)),
                  pl.BlockSpec((TM, PAIR), lambda i: (rev(i), O_FL // PAIR)), _whole((1, PAIR))],
        out_specs=(nat, nat, pl.BlockSpec((TM, PAIR), lambda i: (rev(i), 0)), _whole((1, PAIR))),
        scratch_shapes=[pltpu.VMEM((1, PAIR), F32)], name="fox_post_bwd")(dqt, dkraw, proj, b)


def _s5_expand():
    r = lax.broadcasted_iota(jnp.int32, (S5_STATE, S5_STATE * S5_GROUP_CH), 0)
    c = lax.broadcasted_iota(jnp.int32, (S5_STATE, S5_STATE * S5_GROUP_CH), 1)
    return jnp.where(c // S5_GROUP_CH == r, 1.0, 0.0).astype(F32)


def _s5_disc_math(ar, ai, ldt, br, bi):
    dt = jnp.exp(ldt)
    mag = jnp.exp(ar * dt)
    lr = mag * jnp.cos(ai * dt)
    li = mag * jnp.sin(ai * dt)
    den = ar * ar + ai * ai
    fr = ((lr - 1.0) * ar + li * ai) / den
    fi = (li * ar - (lr - 1.0) * ai) / den
    e = _s5_expand()
    fre = jnp.dot(fr, e, precision=_HI, preferred_element_type=F32)
    fie = jnp.dot(fi, e, precision=_HI, preferred_element_type=F32)
    return lr, li, fre * br - fie * bi, fre * bi + fie * br


def _layer_blocks(arrs):
    return [pl.BlockSpec((None,) + a.shape[1:], lambda l: (l, 0, 0)) for a in arrs]


def _s5_disc(ar, ai, ldt, br, bi):
    def body(ar_ref, ai_ref, ldt_ref, br_ref, bi_ref, lr_ref, li_ref, bbr_ref, bbi_ref):
        lr, li, bbr, bbi = _s5_disc_math(ar_ref[...], ai_ref[...], ldt_ref[...], br_ref[...], bi_ref[...])
        lr_ref[...] = lr
        li_ref[...] = li
        bbr_ref[...] = bbr
        bbi_ref[...] = bbi

    ins = (ar, ai, ldt, br, bi)
    outs = (ar, ai, br, bi)
    return _pallas(body, out_shape=tuple(jax.ShapeDtypeStruct(a.shape, F32) for a in outs), grid=(DEPTH,),
                   in_specs=_layer_blocks(ins), out_specs=tuple(_layer_blocks(outs)), name="s5_disc")(*ins)


def _s5_disc_bwd(ar, ai, ldt, br, bi, dlr, dli, dbbr, dbbi):
    def body(ar_ref, ai_ref, ldt_ref, br_ref, bi_ref, dlr_ref, dli_ref, dbbr_ref, dbbi_ref,
             dar_ref, dai_ref, dldt_ref, dbr_ref, dbi_ref):
        _, vjp = jax.vjp(_s5_disc_math, ar_ref[...], ai_ref[...], ldt_ref[...], br_ref[...], bi_ref[...])
        dar, dai, dldt, dbr, dbi = vjp((dlr_ref[...], dli_ref[...], dbbr_ref[...], dbbi_ref[...]))
        dar_ref[...] = dar
        dai_ref[...] = dai
        dldt_ref[...] = dldt
        dbr_ref[...] = dbr
        dbi_ref[...] = dbi

    ins = (ar, ai, ldt, br, bi, dlr, dli, dbbr, dbbi)
    outs = (ar, ai, ldt, br, bi)
    return _pallas(body, out_shape=tuple(jax.ShapeDtypeStruct(a.shape, F32) for a in outs), grid=(DEPTH,),
                   in_specs=_layer_blocks(ins), out_specs=tuple(_layer_blocks(outs)), name="s5_disc_bwd")(*ins)


SLAB = 2 * S5_CH // 128


def _slab_rows(s, ts):
    return pl.ds(s, ts, stride=SLAB)


def _slab_pair(ref, s, ts):
    return jnp.concatenate([ref[_slab_rows(s, ts), :].astype(_MXU), ref[_slab_rows(s + 1, ts), :].astype(_MXU)], axis=-1)


def _s5_fwd(proj, wb, wc, lam, d, w_glu):
    L = proj.shape[0]
    ts = min(TS, L)

    def body(u_ref, wb_ref, wc_ref, lam_ref, d_ref, wg_ref, xs_ref, ypre_ref, ys_ref, b_sc, c_sc):
        @pl.when(pl.program_id(0) == 0)
        def _():
            c_sc[...] = jnp.zeros((SLAB, 128), F32)

        u = u_ref[...]
        ub = u.astype(_MXU)
        for s in range(0, SLAB, 2):
            b2 = _dot(ub, wb_ref[:, 128 * s:128 * (s + 2)])
            b_sc[_slab_rows(s, ts), :] = b2[:, :128]
            b_sc[_slab_rows(s + 1, ts), :] = b2[:, 128:]
        lr, li = lam_ref[0:8, :], lam_ref[8:16, :]

        def step(t, carry):
            xr, xi = carry
            row = pl.multiple_of(t * SLAB, SLAB)
            nr = lr * xr - li * xi + b_sc[pl.ds(row, 8), :]
            ni = lr * xi + li * xr + b_sc[pl.ds(row + 8, 8), :]
            xs_ref[pl.ds(row, 8), :] = nr
            xs_ref[pl.ds(row + 8, 8), :] = ni
            return nr, ni

        xr, xi = lax.fori_loop(0, ts, step, (c_sc[0:8, :], c_sc[8:16, :]), unroll=8)
        c_sc[0:8, :] = xr
        c_sc[8:16, :] = xi
        y = jnp.zeros((ts, D_S5), F32)
        for s in range(0, SLAB, 2):
            y = y + _dot(_slab_pair(xs_ref, s, ts), wc_ref[128 * s:128 * (s + 2), :])
        ypre_ref[...] = y
        y1 = jax.nn.gelu(y + d_ref[...] * u)
        ys_ref[...] = y1 * jax.nn.sigmoid(_dot(y1, wg_ref[...]))

    row = _rows(ts, D_S5)
    slabs = pl.BlockSpec((ts * SLAB, 128), lambda n: (n, 0))
    return _pallas(
        body, out_shape=(jax.ShapeDtypeStruct((L * SLAB, 128), F32), jax.ShapeDtypeStruct((L, D_S5), F32),
                         jax.ShapeDtypeStruct((L, D_S5), F32)),
        grid=(L // ts,),
        in_specs=[_rows(ts, D_S5, O_SU // D_S5), _whole((D_S5, 2 * S5_CH)), _whole((2 * S5_CH, D_S5)), _whole((SLAB, 128)),
                  _whole((1, D_S5)), _whole((D_S5, D_S5))],
        out_specs=(slabs, row, row),
        scratch_shapes=[pltpu.VMEM((ts * SLAB, 128), F32), pltpu.VMEM((SLAB, 128), F32)], name="s5_fwd")(
            proj, wb, wc, lam, d, w_glu)


def _s5_bwd(proj, ypre, dys, xs, wb, wc, lam, d, w_glu):
    L = proj.shape[0]
    ts = min(TS, L)
    nb = L // ts

    def body(u_ref, y_ref, dys_ref, xs_ref, xp_ref, wb_ref, wc_ref, lam_ref, d_ref, wg_ref,
             du_ref, dwb_ref, dwc_ref, dlam_ref, dd_ref, dwg_ref, dx_sc, g_sc, c_sc):
        n = pl.program_id(0)

        @pl.when(n == 0)
        def _():
            c_sc[...] = jnp.zeros((SLAB, 128), F32)
            dlam_ref[...] = jnp.zeros((SLAB, 128), F32)
            dwb_ref[...] = jnp.zeros((D_S5, 2 * S5_CH), F32)
            dwc_ref[...] = jnp.zeros((2 * S5_CH, D_S5), F32)
            dd_ref[...] = jnp.zeros((1, D_S5), F32)
            dwg_ref[...] = jnp.zeros((D_S5, D_S5), F32)

        u, dv, dout = u_ref[...], d_ref[...], dys_ref[...]
        y1, gelu_vjp = jax.vjp(jax.nn.gelu, y_ref[...] + dv * u)
        sg = jax.nn.sigmoid(_dot(y1, wg_ref[...]))
        dz = dout * y1 * sg * (1.0 - sg)
        dy, = gelu_vjp(dout * sg + _dot(dz, wg_ref[...], _NT))
        dd_ref[...] += jnp.sum(dy * u, axis=0, keepdims=True)
        dwg_ref[...] += _dot(y1, dz, _TN)
        dyb = dy.astype(_MXU)
        for s in range(0, SLAB, 2):
            cols = slice(128 * s, 128 * (s + 2))
            dx2 = _dot(dyb, wc_ref[cols, :], _NT)
            dx_sc[_slab_rows(s, ts), :] = dx2[:, :128]
            dx_sc[_slab_rows(s + 1, ts), :] = dx2[:, 128:]
            dwc_ref[cols, :] += _dot(_slab_pair(xs_ref, s, ts), dyb, _TN)
        lr, li = lam_ref[0:8, :], lam_ref[8:16, :]

        def adjoint(row, pr, pi, carry):
            gr, gi, ar, ai = carry
            nr = dx_sc[pl.ds(row, 8), :] + lr * gr + li * gi
            ni = dx_sc[pl.ds(row + 8, 8), :] - li * gr + lr * gi
            g_sc[pl.ds(row, 8), :] = nr
            g_sc[pl.ds(row + 8, 8), :] = ni
            return nr, ni, ar + nr * pr + ni * pi, ai - nr * pi + ni * pr

        def step(k, carry):
            row = pl.multiple_of((ts - 1 - k) * SLAB, SLAB)
            prev = pl.multiple_of((ts - 2 - k) * SLAB, SLAB)
            return adjoint(row, xs_ref[pl.ds(prev, 8), :], xs_ref[pl.ds(prev + 8, 8), :], carry)

        z = jnp.zeros((8, 128), F32)
        carry = lax.fori_loop(0, ts - 1, step, (c_sc[0:8, :], c_sc[8:16, :], z, z), unroll=8)
        has_prev = jnp.where(n == nb - 1, 0.0, 1.0)
        gr, gi, ar, ai = adjoint(0, xp_ref[0:8, :] * has_prev, xp_ref[8:16, :] * has_prev, carry)
        c_sc[0:8, :] = gr
        c_sc[8:16, :] = gi
        dlam_ref[0:8, :] += ar
        dlam_ref[8:16, :] += ai
        ub = u.astype(_MXU)
        du = dy * dv
        for s in range(0, SLAB, 2):
            cols = slice(128 * s, 128 * (s + 2))
            gs = _slab_pair(g_sc, s, ts)
            du = du + _dot(gs, wb_ref[:, cols], _NT)
            dwb_ref[:, cols] += _dot(ub, gs, _TN)
        du_ref[...] = du

    blk = lambda n: nb - 1 - n
    row = pl.BlockSpec((ts, D_S5), lambda n: (blk(n), 0))
    return _pallas(
        body, out_shape=(jax.ShapeDtypeStruct((L, D_S5), F32), jax.ShapeDtypeStruct((D_S5, 2 * S5_CH), F32),
                         jax.ShapeDtypeStruct((2 * S5_CH, D_S5), F32), jax.ShapeDtypeStruct((SLAB, 128), F32),
                         jax.ShapeDtypeStruct((1, D_S5), F32), jax.ShapeDtypeStruct((D_S5, D_S5), F32)),
        grid=(nb,),
        in_specs=[pl.BlockSpec((ts, D_S5), lambda n: (blk(n), O_SU // D_S5)), row, row,
                  pl.BlockSpec((ts * SLAB, 128), lambda n: (blk(n), 0)),
                  pl.BlockSpec((SLAB, 128), lambda n: (jnp.maximum(blk(n) * ts - 1, 0), 0)),
                  _whole((D_S5, 2 * S5_CH)), _whole((2 * S5_CH, D_S5)), _whole((SLAB, 128)), _whole((1, D_S5)),
                  _whole((D_S5, D_S5))],
        out_specs=(row, _whole((D_S5, 2 * S5_CH)), _whole((2 * S5_CH, D_S5)), _whole((SLAB, 128)), _whole((1, D_S5)),
                   _whole((D_S5, D_S5))),
        scratch_shapes=[pltpu.VMEM((ts * SLAB, 128), F32), pltpu.VMEM((ts * SLAB, 128), F32), pltpu.VMEM((SLAB, 128), F32)],
        name="s5_bwd")(proj, ypre, dys, xs, xs, wb, wc, lam, d, w_glu)


def _rot(z, cos, sin):
    lane = lax.broadcasted_iota(jnp.int32, z.shape, 1)
    zs = z * sin
    half = HEAD_DIM // 2
    return z * cos + jnp.where(lane % HEAD_DIM < half, pltpu.roll(zs, PAIR - half, 1), pltpu.roll(zs, half, 1))


def _head_avg():
    r = lax.broadcasted_iota(jnp.int32, (PAIR, PAIR), 0) // HEAD_DIM
    c = lax.broadcasted_iota(jnp.int32, (PAIR, PAIR), 1) // HEAD_DIM
    return jnp.where(r == c, 1.0 / HEAD_DIM, 0.0).astype(F32)


def _ret_tables(tq):
    lg = jnp.log1p(-(2.0 ** (-5.0 - jnp.arange(RET_HEADS, dtype=F32))))
    scale = 1.0 / math.sqrt(HEAD_DIM)
    pos = jnp.arange(tq)
    n = pos.astype(F32)
    dist = jnp.abs(n[:, None] - n[None, :])
    ok = (pos[None, :] // CHUNK) <= (pos[:, None] // CHUNK)
    w = jnp.where(ok[None], scale * jnp.exp(lg[:, None, None] * dist[None]), 0.0)
    lgl = jnp.repeat(lg, HEAD_DIM)
    dq_tab = scale * jnp.exp(lgl[None, :] * (n[:, None] + 1.0))
    dk_tab = jnp.exp(lgl[None, :] * (tq - 1.0 - n[:, None]))
    blk = jnp.arange(PAIR) // HEAD_DIM
    bd = (blk[:, None] == blk[None, :]).astype(F32)
    gbd = bd[None] * jnp.exp(lgl.reshape(RET_HEADS // 2, PAIR)[:, :, None] * tq)
    return dict(w=w, wt=w.transpose(0, 2, 1), dq=dq_tab, dk=dk_tab, gbd=gbd, bd=bd)


def _ret_specs(tq, nq, rev):
    blk = (lambda i: nq - 1 - i) if rev else (lambda i: i)
    col = lambda o: pl.BlockSpec((tq, PAIR), lambda p, i: (blk(i), o // PAIR + p))
    return dict(
        rq=col(O_RQ), rk=col(O_RK), rv=col(O_RV), nat=col(0),
        w=pl.BlockSpec((2, tq, tq), lambda p, i: (p, 0, 0)), tab=pl.BlockSpec((tq, PAIR), lambda p, i: (0, p)),
        gbd=pl.BlockSpec((None, PAIR, PAIR), lambda p, i: (p, 0, 0)), bd=pl.BlockSpec((PAIR, PAIR), lambda p, i: (0, 0)),
        gn=pl.BlockSpec((1, PAIR), lambda p, i: (0, p)),
        st=pl.BlockSpec((None, None, PAIR, PAIR), lambda p, i: (p, blk(i), 0, 0)))


def _ret_fwd(proj, cos_t, sin_t, tabs, gn):
    L = proj.shape[0]
    tq = tabs["w"].shape[1]
    nq = L // tq

    def body(rq_ref, rk_ref, rv_ref, cos_ref, sin_ref, w_ref, dqt_ref, dkt_ref, gbd_ref, bd_ref, gn_ref,
             o_ref, y_ref, st_ref, s_sc):
        @pl.when(pl.program_id(1) == 0)
        def _():
            s_sc[...] = jnp.zeros((PAIR, PAIR), F32)

        state = s_sc[...]
        st_ref[...] = state
        cos, sin = cos_ref[...], sin_ref[...]
        q2, k2, v2 = _rot(rq_ref[...], cos, sin), _rot(rk_ref[...], cos, sin), rv_ref[...]
        o = _dot(q2 * dqt_ref[...], state)
        for h in range(2):
            own = _own((tq, PAIR), h)
            a = _dot(jnp.where(own, q2, 0.0), k2, _NT) * w_ref[h]
            o = o + _dot(a, jnp.where(own, v2, 0.0))
        s_sc[...] = gbd_ref[...] * state + bd_ref[...] * _dot(k2 * dkt_ref[...], v2, _TN)
        o_ref[...] = o
        avg = _head_avg()
        oc = o - _hi_dot(o, avg)
        y_ref[...] = oc * lax.rsqrt(_hi_dot(oc * oc, avg) + EPS) * gn_ref[...]

    sp = _ret_specs(tq, nq, False)
    nat = jax.ShapeDtypeStruct((L, D_RET), F32)
    return _pallas(
        body, out_shape=(nat, nat, jax.ShapeDtypeStruct((RET_HEADS // 2, nq, PAIR, PAIR), F32)), grid=(RET_HEADS // 2, nq),
        in_specs=[sp["rq"], sp["rk"], sp["rv"], sp["nat"], sp["nat"], sp["w"], sp["tab"], sp["tab"], sp["gbd"], sp["bd"],
                  sp["gn"]],
        out_specs=(sp["nat"], sp["nat"], sp["st"]), scratch_shapes=[pltpu.VMEM((PAIR, PAIR), F32)],
        name="ret_fwd")(proj, proj, proj, cos_t, sin_t, tabs["w"], tabs["dq"], tabs["dk"], tabs["gbd"], tabs["bd"], gn)


def _ret_bwd(proj, cos_t, sin_t, tabs, gn, o_pre, dy, states):
    L = proj.shape[0]
    tq = tabs["w"].shape[1]
    nq = L // tq

    def body(rq_ref, rk_ref, rv_ref, cos_ref, sin_ref, w_ref, wt_ref, dqt_ref, dkt_ref, gbd_ref, bd_ref, gn_ref,
             o_ref, dy_ref, st_ref, drq_ref, drk_ref, drv_ref, dgn_ref, g_sc):
        first = pl.program_id(1) == 0

        @pl.when(first)
        def _():
            g_sc[...] = jnp.zeros((PAIR, PAIR), F32)

        cos, sin = cos_ref[...], sin_ref[...]
        q2, k2, v2 = _rot(rq_ref[...], cos, sin), _rot(rk_ref[...], cos, sin), rv_ref[...]
        avg = _head_avg()
        ov, dyv = o_ref[...], dy_ref[...]
        oc = ov - _hi_dot(ov, avg)
        r = lax.rsqrt(_hi_dot(oc * oc, avg) + EPS)
        oh = oc * r
        dgn = jnp.sum(dyv * oh, axis=0, keepdims=True)
        doh = dyv * gn_ref[...]
        do = r * (doh - _hi_dot(doh, avg) - oh * _hi_dot(doh * oh, avg))
        state, g = st_ref[...], g_sc[...]
        dqt, dkt = dqt_ref[...], dkt_ref[...]
        dq = _dot(do, state, _NT) * dqt
        dk = _dot(v2, g, _NT) * dkt
        dv = _dot(k2 * dkt, g)
        g_sc[...] = gbd_ref[...] * g + bd_ref[...] * _dot(q2 * dqt, do, _TN)
        for h in range(2):
            own = _own((tq, PAIR), h)
            qm, dom = jnp.where(own, q2, 0.0), jnp.where(own, do, 0.0)
            dv = dv + _dot(_dot(k2, qm, _NT) * wt_ref[h], dom)
            dq = dq + _dot(_dot(dom, v2, _NT) * w_ref[h], jnp.where(own, k2, 0.0))
            dk = dk + _dot(_dot(v2, dom, _NT) * wt_ref[h], qm)
        drq_ref[...] = _rot(dq, cos, -sin)
        drk_ref[...] = _rot(dk, cos, -sin)
        drv_ref[...] = dv

        @pl.when(first)
        def _():
            dgn_ref[...] = dgn

        @pl.when(jnp.logical_not(first))
        def _():
            dgn_ref[...] += dgn

    sp = _ret_specs(tq, nq, True)
    nat = jax.ShapeDtypeStruct((L, D_RET), F32)
    return _pallas(
        body, out_shape=(nat, nat, nat, jax.ShapeDtypeStruct((1, D_RET), F32)), grid=(RET_HEADS // 2, nq),
        in_specs=[sp["rq"], sp["rk"], sp["rv"], sp["nat"], sp["nat"], sp["w"], sp["w"], sp["tab"], sp["tab"], sp["gbd"],
                  sp["bd"], sp["gn"], sp["nat"], sp["nat"], sp["st"]],
        out_specs=(sp["nat"], sp["nat"], sp["nat"], sp["gn"]), scratch_shapes=[pltpu.VMEM((PAIR, PAIR), F32)],
        name="ret_bwd")(proj, proj, proj, cos_t, sin_t, tabs["w"], tabs["wt"], tabs["dq"], tabs["dk"], tabs["gbd"],
                        tabs["bd"], gn, o_pre, dy, states)


def _gate_out(yf, ys, yr, proj, x, w):
    L = x.shape[0]

    def body(yf_ref, ys_ref, yr_ref, g_ref, x_ref, w_ref, y_ref, xn_ref):
        cat = jnp.concatenate([yf_ref[...], ys_ref[...], yr_ref[...]], axis=-1)
        y = cat * jax.nn.silu(g_ref[...])
        y_ref[...] = y
        xn_ref[...] = x_ref[...] + _dot(y, w_ref[...])

    full = _rows(TM, D_MODEL)
    return _pallas(body, out_shape=(jax.ShapeDtypeStruct((L, D_MODEL), F32),) * 2, grid=(L // TM,),
                   in_specs=[_rows(TM, D_FOX), _rows(TM, D_S5), _rows(TM, D_RET), _rows(TM, D_MODEL, O_GATE // D_MODEL),
                             full, _whole((D_MODEL, D_MODEL))],
                   out_specs=(full, full), name="gate_out")(yf, ys, yr, proj, x, w)


def _gate_out_bwd(dxn, w, yf, ys, yr, proj):
    L = dxn.shape[0]

    def body(dx_ref, w_ref, yf_ref, ys_ref, yr_ref, g_ref, dyf_ref, dys_ref, dyr_ref, dg_ref):
        dy = _dot(dx_ref[...], w_ref[...], _NT)
        g = g_ref[...]
        sg = jax.nn.sigmoid(g)
        dcat = dy * (g * sg)
        dyf_ref[...] = dcat[:, :D_FOX]
        dys_ref[...] = dcat[:, D_FOX:D_FOX + D_S5]
        dyr_ref[...] = dcat[:, D_FOX + D_S5:]
        cat = jnp.concatenate([yf_ref[...], ys_ref[...], yr_ref[...]], axis=-1)
        dg_ref[...] = dy * cat * (sg * (1.0 + g * (1.0 - sg)))

    full = _rows(TM, D_MODEL)
    f, s, r = _rows(TM, D_FOX), _rows(TM, D_S5), _rows(TM, D_RET)
    return _pallas(body, out_shape=(jax.ShapeDtypeStruct((L, D_FOX), F32), jax.ShapeDtypeStruct((L, D_S5), F32),
                                    jax.ShapeDtypeStruct((L, D_RET), F32), jax.ShapeDtypeStruct((L, D_MODEL), F32)),
                   grid=(L // TM,),
                   in_specs=[full, _whole((D_MODEL, D_MODEL)), f, s, r, _rows(TM, D_MODEL, O_GATE // D_MODEL)],
                   out_specs=(f, s, r, full), name="gate_out_bwd")(dxn, w, yf, ys, yr, proj)


def _final_loss(x, g, tgt):
    L = x.shape[0]

    def body(x_ref, g_ref, t_ref, loss_ref, dx_ref, dg_ref):
        xv, gv = x_ref[...], g_ref[...]
        r = lax.rsqrt(jnp.mean(xv * xv, axis=-1, keepdims=True) + EPS)
        err = xv * r * gv - t_ref[...]
        part = 0.5 * jnp.sum(jnp.mean(err * err, axis=-1, keepdims=True), axis=0, keepdims=True)
        dx, dg = _rms_bwd(xv, gv, err * (1.0 / D_MODEL))
        dx_ref[...] = dx

        @pl.when(pl.program_id(0) == 0)
        def _():
            loss_ref[...] = part
            dg_ref[...] = dg

        @pl.when(pl.program_id(0) != 0)
        def _():
            loss_ref[...] += part
            dg_ref[...] += dg

    full = _rows(TM, D_MODEL)
    return _pallas(body, out_shape=(jax.ShapeDtypeStruct((1, 1), F32), jax.ShapeDtypeStruct((L, D_MODEL), F32),
                                    jax.ShapeDtypeStruct((1, D_MODEL), F32)),
                   grid=(L // TM,), in_specs=[full, _whole((1, D_MODEL)), full],
                   out_specs=(_whole((1, 1)), full, _whole((1, D_MODEL))), name="final_loss")(x, g, tgt)


def _block_diag(blocks):
    n, g, r, c = blocks.shape
    eye = jnp.eye(g, dtype=blocks.dtype)
    return (blocks[:, :, :, None, :] * eye[None, :, None, :, None]).reshape(n, g * r, g * c)


def _diag_blocks(m, g):
    n, r, c = m.shape[0], m.shape[1] // g, m.shape[2] // g
    eye = jnp.eye(g, dtype=m.dtype)
    return jnp.sum(m.reshape(n, g, r, g, c) * eye[None, :, None, :, None], axis=3)


def _rope_tables(L):
    half = HEAD_DIM // 2
    freqs = ROPE_BASE ** (-jnp.arange(half, dtype=F32) / half)
    ang = jnp.arange(L, dtype=F32)[:, None] * freqs[None, :]
    cos, sin = jnp.cos(ang), jnp.sin(ang)
    cos_t = jnp.tile(jnp.concatenate([cos, cos], axis=-1), (1, RET_HEADS))
    sin_t = jnp.tile(jnp.concatenate([sin, -sin], axis=-1), (1, RET_HEADS))
    return cos_t, sin_t


def _s5_disc_args(small):
    g, s, ch = S5_GROUPS, S5_STATE, S5_GROUP_CH
    return (small["s5_a_re"], small["s5_a_im"], small["s5_log_dt"][:, :, None],
            small["s5_b_re"].reshape(DEPTH, g, s * ch), small["s5_b_im"].reshape(DEPTH, g, s * ch))


def _s5_mats(small):
    g, s, ch = S5_GROUPS, S5_STATE, S5_GROUP_CH
    lr, li, bbr, bbi = _s5_disc(*_s5_disc_args(small))
    lam = jnp.concatenate([lr.reshape(DEPTH, 8, 128), li.reshape(DEPTH, 8, 128)], axis=1)
    wb = jnp.concatenate([_block_diag(b.reshape(DEPTH, g, s, ch).transpose(0, 1, 3, 2)) for b in (bbr, bbi)], axis=2)
    wc = jnp.concatenate([_block_diag(c.transpose(0, 1, 3, 2)) for c in (small["s5_c_re"], -small["s5_c_im"])], axis=1)
    return lam, wb.astype(_MXU), wc.astype(_MXU)


def _s5_param_grads(small, dwb, dwc, dlam):
    g, s, ch = S5_GROUPS, S5_STATE, S5_GROUP_CH
    dc = [_diag_blocks(m, g).transpose(0, 1, 3, 2) for m in (dwc[:, :S5_CH], dwc[:, S5_CH:])]
    dbb = [_diag_blocks(m, g).transpose(0, 1, 3, 2).reshape(DEPTH, g, s * ch) for m in (dwb[:, :, :S5_CH], dwb[:, :, S5_CH:])]
    dar, dai, dldt, dbr, dbi = _s5_disc_bwd(*_s5_disc_args(small), dlam[:, :8].reshape(DEPTH, g, s),
                                            dlam[:, 8:].reshape(DEPTH, g, s), dbb[0], dbb[1])
    shp = (DEPTH, g, s, ch)
    return dict(s5_a_re=dar, s5_a_im=dai, s5_log_dt=dldt.reshape(DEPTH, g), s5_b_re=dbr.reshape(shp),
                s5_b_im=dbi.reshape(shp), s5_c_re=dc[0], s5_c_im=-dc[1])


def _layer_fwd(x, p, rope, ride=None):
    L = x.shape[0]
    cos_t, sin_t, ret_tabs = rope
    s = {"x": x}
    proj, h = _norm_inproj(x, p["norm_w"], p["w_in"])
    s["proj"], s["h"] = proj, h
    qa, ka, kat, vt = _fox_prep(proj, _fox_cumsum(proj, p["b_f"]))
    yf, lse, landed = _fox_fwd(qa, ka, vt, ride)
    s.update(qa=qa, ka=ka, kat=kat, lse=lse, yf=yf)
    xs, ypre, ys = _s5_fwd(proj, p["wb"], p["wc"], p["lam"], p["d"], p["w_glu"])
    s.update(xs=xs, ypre=ypre, ys=ys)
    o_pre, yr, states = _ret_fwd(proj, cos_t, sin_t, ret_tabs, p["gn_w"])
    s.update(o_pre=o_pre, yr=yr, states=states)
    y, xn = _gate_out(yf, ys, yr, proj, x, p["w_out"])
    s["y"] = y
    return xn, s, landed


def _layer_bwd(dxn, s, p, rope, ride=None):
    L = dxn.shape[0]
    cos_t, sin_t, ret_tabs = rope
    g = {}
    proj = s["proj"]
    dyf, dys, dyr, dgate = _gate_out_bwd(dxn, p["w_out"], s["yf"], s["ys"], s["yr"], proj)
    g["w_out"] = _mm(s["y"], dxn, ta=True, name="dw_out")
    drq, drk, drv, dgn = _ret_bwd(proj, cos_t, sin_t, ret_tabs, p["gn_w"], s["o_pre"], dyr, s["states"])
    g["ret_gn_w"] = dgn.reshape(D_RET)
    dsu, g["wb"], g["wc"], g["lam"], dd, g["s5_w_glu"] = _s5_bwd(proj, s["ypre"], dys, s["xs"], p["wb"], p["wc"], p["lam"],
                                                                 p["d"], p["w_glu"])
    g["s5_d"] = dd.reshape(D_S5)
    dqt, dkraw, dv, landed = _fox_bwd(s["qa"], s["ka"], s["kat"], proj, dyf, s["yf"], s["lse"], ride)
    dq, dk, dfl, dbf = _fox_post_bwd(dqt, dkraw, proj, p["b_f"])
    g["fox_b_f"] = dbf[0, :FOX_HEADS]
    pieces = [dgate, dq, dk, dv, dsu, drq, drk, drv, dfl]
    dx, dnw = _inproj_bwd_dx(pieces, p["w_in"], s["x"], p["norm_w"], dxn)
    g["norm_w"] = dnw.reshape(D_MODEL)
    g["w_in"] = _dw_in(s["h"], pieces)
    return dx, g, landed


def _layer_params(l, w_in_p, w_glu, w_out, small, s5_mats):
    lam, wb, wc = s5_mats
    return dict(
        norm_w=small["norm_w"][l][None], w_in=w_in_p, b_f=jnp.pad(small["fox_b_f"][l], (0, PAIR - FOX_HEADS))[None],
        lam=lam[l], wb=wb[l], wc=wc[l], d=small["s5_d"][l][None], w_glu=w_glu, gn_w=small["ret_gn_w"][l][None], w_out=w_out)


_SHARDED = ("w_in", "s5_w_glu", "w_out")
_WIRE = jnp.bfloat16


_RUNS = ((2568, 3592, O_GATE), (0, 1536, O_FQ), (1544, 2568, O_SU), (1536, 1544, O_FL))


def _shard_pieces():
    out = []
    for a, b, pad in _RUNS:
        while a < b:
            j = a // W_SHARD
            e = min(b, (j + 1) * W_SHARD)
            out.append((j, a - j * W_SHARD, e - j * W_SHARD, pad))
            pad, a = pad + e - a, e
    return out


def _gathered_weights(g_in, g_glu, g_out):
    cols = [g_in[j, :, a:e] for j, a, e, _ in _shard_pieces()]
    cols.append(jnp.zeros((D_MODEL, D_INP - O_FL - FOX_HEADS), g_in.dtype))
    return jnp.concatenate(cols, axis=1), g_glu.reshape(D_S5, D_S5), g_out.reshape(D_MODEL, D_MODEL)


def _grad_slots(g):
    w_in = g["w_in"].astype(_WIRE)
    slots = []
    for j in range(N_DEV):
        mine = sorted((a, e, pad) for jj, a, e, pad in _shard_pieces() if jj == j)
        slots.append(jnp.concatenate([w_in[:, pad:pad + e - a] for a, e, pad in mine], axis=1))
    return [jnp.stack(slots), g["s5_w_glu"].reshape(N_DEV, D_S5 // N_DEV, D_S5).astype(_WIRE),
            g["w_out"].reshape(N_DEV, D_MODEL // N_DEV, D_MODEL).astype(_WIRE)]


def _step_grads(x, tgt, small, full=None, shards=None):
    L = x.shape[0]
    rope = _rope_tables(L) + (_ret_tables(min(TQ, L)),)
    s5_mats = _s5_mats(small)
    gather = [False] * len(_SHARDED)
    if shards is not None:
        nxt = _gathered_weights(*_exchange([s[0] for s in shards], gather, "gather_layer0"))
    saved, params = [], []
    for l in range(DEPTH):
        weights = nxt if shards is not None else tuple(f[l] for f in full)
        ride = ([s[l + 1] for s in shards], gather) if shards is not None and l + 1 < DEPTH else None
        params.append(_layer_params(l, *weights, small, s5_mats))
        x, s, landed = _layer_fwd(x, params[l], rope, ride)
        if ride is not None:
            nxt = _gathered_weights(*landed)
        saved.append(s)
    loss, dx, dfw = _final_loss(x, small["final_norm_w"][None], tgt)
    grads, partials, waiting = [None] * DEPTH, [None] * DEPTH, None
    scatter = [True] * len(_SHARDED)
    for l in reversed(range(DEPTH)):
        dx, grads[l], landed = _layer_bwd(dx, saved[l], params[l], rope, (waiting, scatter) if waiting is not None else None)
        if waiting is not None:
            partials[l + 1] = landed
        if shards is not None:
            waiting = _grad_slots(grads[l])
    stack = lambda n: jnp.stack([g[n] for g in grads])
    small_g = {n: stack(n) for n in ("norm_w", "fox_b_f", "s5_d", "ret_gn_w")}
    small_g.update(_s5_param_grads(small, stack("wb"), stack("wc"), stack("lam")), final_norm_w=dfw.reshape(D_MODEL))
    if shards is None:
        return loss, dx, grads, small_g
    packed = _pack([small_g[n] for n in _SMALL]).astype(_WIRE)
    landed = _exchange(waiting + [packed], scatter + [False], "exchange_layer0")
    partials[0] = landed[:-1]
    return loss, dx, grads, small_g, partials, landed[-1]


_MESH = pl.DeviceIdType.MESH
_ANY = pl.BlockSpec(memory_space=pl.ANY)


def _me_and_peers():
    x, y, c = lax.axis_index("x"), lax.axis_index("y"), lax.axis_index("c")
    flip = lambda a, bit: (1 - a) if bit else a
    peers = []
    for r in range(1, N_DEV):
        px, py, pc = flip(x, (r >> 2) & 1), flip(y, (r >> 1) & 1), flip(c, r & 1)
        peers.append(((px, py, pc), 4 * px + 2 * py + pc))
    return 4 * x + 2 * y + c, peers


def _exchange_copies(srcs, dsts, sems, scatter):
    send_sems, recv_sems, local_sems = sems
    me, peers = _me_and_peers()
    pick = lambda t, to: srcs[t].at[to] if scatter[t] else srcs[t]
    own = [pltpu.make_async_copy(pick(t, me), dsts[t].at[me], local_sems.at[t]) for t in range(len(srcs))]
    sends, waits = [], []
    for r, (dev, idx) in enumerate(peers):
        for t in range(len(srcs)):
            for land, out in ((me, sends), (idx, waits)):
                out.append(pltpu.make_async_remote_copy(pick(t, idx), dsts[t].at[land], send_sems.at[t, r], recv_sems.at[t, r],
                                                        device_id=dev, device_id_type=_MESH))
    return own, sends, waits


def _exchange_start(srcs, dsts, sems, scatter):
    own, sends, _ = _exchange_copies(srcs, dsts, sems, scatter)
    for cp in own + sends:
        cp.start()


def _exchange_wait(srcs, dsts, sems, scatter):
    own, _, waits = _exchange_copies(srcs, dsts, sems, scatter)
    for cp in waits + own:
        cp.wait()


def _exchange_shapes(arrs, scatter):
    outs = [jax.ShapeDtypeStruct(a.shape if sc else (N_DEV,) + a.shape, a.dtype) for a, sc in zip(arrs, scatter)]
    n = len(arrs)
    sems = [pltpu.SemaphoreType.DMA((n, N_DEV - 1)), pltpu.SemaphoreType.DMA((n, N_DEV - 1)), pltpu.SemaphoreType.DMA((n,))]
    return outs, sems


def _exchange(arrs, scatter, name):
    n = len(arrs)

    def body(*refs):
        _exchange_start(refs[:n], refs[n:2 * n], refs[2 * n:], scatter)
        _exchange_wait(refs[:n], refs[n:2 * n], refs[2 * n:], scatter)

    outs, sems = _exchange_shapes(arrs, scatter)
    return _pallas(body, out_shape=tuple(outs), in_specs=[_ANY] * n, out_specs=tuple([_ANY] * n), scratch_shapes=sems,
                   name=name)(*arrs)


def _riding(body, n_in, n_out, ride, is_first, is_last):
    if ride is None:
        return body, [], [], [], []
    arrs, scatter = ride
    n = len(arrs)
    outs, sems = _exchange_shapes(arrs, scatter)

    def wrapped(*refs):
        ins, srcs = refs[:n_in], refs[n_in:n_in + n]
        own_outs, dsts = refs[n_in + n:n_in + n + n_out], refs[n_in + n + n_out:n_in + 2 * n + n_out]
        scratch, ex_sems = refs[n_in + 2 * n + n_out:-3], refs[-3:]

        @pl.when(is_first())
        def _():
            _exchange_start(srcs, dsts, ex_sems, scatter)

        body(*ins, *own_outs, *scratch)

        @pl.when(is_last())
        def _():
            _exchange_wait(srcs, dsts, ex_sems, scatter)

    return wrapped, list(arrs), [_ANY] * n, outs, sems


def _adamw(parts, w, m, v, name):
    n, rows, cols = parts.shape
    tm = next(t for t in (256, 128, 64, 32, 16) if rows % t == 0)

    def body(p_ref, w_ref, m_ref, v_ref, g_ref, d_ref, nm_ref, nv_ref):
        g = p_ref[0].astype(F32)
        for i in range(1, n):
            g = g + p_ref[i].astype(F32)
        nm = ADAM_B1 * m_ref[...] + (1.0 - ADAM_B1) * g
        nv = ADAM_B2 * v_ref[...] + (1.0 - ADAM_B2) * jnp.square(g)
        m_hat = nm / (1.0 - ADAM_B1 ** ADAM_STEP)
        v_hat = nv / (1.0 - ADAM_B2 ** ADAM_STEP)
        g_ref[...] = g
        d_ref[...] = -ADAM_LR * (m_hat / (jnp.sqrt(v_hat) + ADAM_EPS) + ADAM_WD * w_ref[...])
        nm_ref[...] = nm
        nv_ref[...] = nv

    row = pl.BlockSpec((tm, cols), lambda i: (i, 0))
    return _pallas(body, out_shape=(jax.ShapeDtypeStruct((rows, cols), F32),) * 4, grid=(rows // tm,),
                   in_specs=[pl.BlockSpec((n, tm, cols), lambda i: (0, i, 0)), row, row, row], out_specs=(row,) * 4,
                   name=name)(parts, w, m, v)


_WEIGHTS = ("norm_w", "w_in", "fox_b_f", "s5_a_re", "s5_a_im", "s5_b_re", "s5_b_im", "s5_c_re", "s5_c_im", "s5_d",
            "s5_log_dt", "s5_w_glu", "ret_gn_w", "w_out", "final_norm_w")
_SMALL = tuple(n for n in _WEIGHTS if n not in _SHARDED)
_LANES = 128


def _pack(arrs):
    flat = jnp.concatenate([a.reshape(-1) for a in arrs])
    rows = -(-flat.shape[0] // (_LANES * _LANES)) * _LANES
    return jnp.pad(flat, (0, rows * _LANES - flat.shape[0])).reshape(rows, _LANES)


def _unpack(packed, like):
    flat, out, off = packed.reshape(-1), [], 0
    for a in like:
        out.append(flat[off:off + a.size].reshape(a.shape))
        off += a.size
    return out


def kernel(x, norm_w, w_in, fox_b_f, s5_a_re, s5_a_im, s5_b_re, s5_b_im, s5_c_re, s5_c_im, s5_d, s5_log_dt, s5_w_glu, ret_gn_w, w_out, final_norm_w, loss_target, m_norm_w, m_w_in, m_fox_b_f, m_s5_a_re, m_s5_a_im, m_s5_b_re, m_s5_b_im, m_s5_c_re, m_s5_c_im, m_s5_d, m_s5_log_dt, m_s5_w_glu, m_ret_gn_w, m_w_out, m_final_norm_w, v_norm_w, v_w_in, v_fox_b_f, v_s5_a_re, v_s5_a_im, v_s5_b_re, v_s5_b_im, v_s5_c_re, v_s5_c_im, v_s5_d, v_s5_log_dt, v_s5_w_glu, v_ret_gn_w, v_w_out, v_final_norm_w):
    w = dict(norm_w=norm_w, w_in=w_in, fox_b_f=fox_b_f, s5_a_re=s5_a_re, s5_a_im=s5_a_im, s5_b_re=s5_b_re, s5_b_im=s5_b_im,
             s5_c_re=s5_c_re, s5_c_im=s5_c_im, s5_d=s5_d, s5_log_dt=s5_log_dt, s5_w_glu=s5_w_glu, ret_gn_w=ret_gn_w,
             w_out=w_out, final_norm_w=final_norm_w)
    m = dict(norm_w=m_norm_w, w_in=m_w_in, fox_b_f=m_fox_b_f, s5_a_re=m_s5_a_re, s5_a_im=m_s5_a_im, s5_b_re=m_s5_b_re,
             s5_b_im=m_s5_b_im, s5_c_re=m_s5_c_re, s5_c_im=m_s5_c_im, s5_d=m_s5_d, s5_log_dt=m_s5_log_dt,
             s5_w_glu=m_s5_w_glu, ret_gn_w=m_ret_gn_w, w_out=m_w_out, final_norm_w=m_final_norm_w)
    v = dict(norm_w=v_norm_w, w_in=v_w_in, fox_b_f=v_fox_b_f, s5_a_re=v_s5_a_re, s5_a_im=v_s5_a_im, s5_b_re=v_s5_b_re,
             s5_b_im=v_s5_b_im, s5_c_re=v_s5_c_re, s5_c_im=v_s5_c_im, s5_d=v_s5_d, s5_log_dt=v_s5_log_dt,
             s5_w_glu=v_s5_w_glu, ret_gn_w=v_ret_gn_w, w_out=v_w_out, final_norm_w=v_final_norm_w)

    small = {n: w[n] for n in _SMALL}
    loss, dx, _, _, partials, r_small = _step_grads(x[0], loss_target[0], small, shards=[w[n].astype(_MXU) for n in _SHARDED])

    res = {}
    for t, n in enumerate(_SHARDED):
        cols = w[n].shape[-1]
        r = jnp.stack([partials[l][t] for l in range(DEPTH)], axis=1)
        outs = _adamw(r.reshape(N_DEV, -1, cols), w[n].reshape(-1, cols), m[n].reshape(-1, cols), v[n].reshape(-1, cols),
                      "adamw_" + n)
        res[n] = [o.reshape(w[n].shape) for o in outs]
    small_w = [w[n] for n in _SMALL]
    outs = _adamw(r_small, _pack(small_w), _pack([m[n] for n in _SMALL]), _pack([v[n] for n in _SMALL]), "adamw_small")
    for k, o in enumerate(outs):
        for n, a in zip(_SMALL, _unpack(o, small_w)):
            res.setdefault(n, [None] * 4)[k] = a

    loss = lax.psum(loss[0, 0], ("x", "y", "c"))
    return (loss, dx[None], *[res[n][0] for n in _WEIGHTS], *[res[n][1] for n in _WEIGHTS],
            *[res[n][2] for n in _WEIGHTS], *[res[n][3] for n in _WEIGHTS])
```

```python
import math

import jax
import jax.numpy as jnp
from jax import lax
from jax.experimental import pallas as pl
from jax.experimental.pallas import tpu as pltpu

F32 = jnp.float32
_MXU = jnp.bfloat16
_HI = lax.Precision.HIGHEST

N_DEV = 8
DEPTH = 4
D_MODEL = 1024
HEAD_DIM = 64
D_FOX = 512
FOX_HEADS = 8
D_S5 = 256
S5_GROUPS = 16
S5_GROUP_CH = 16
S5_STATE = 64
S5_CH = S5_GROUPS * S5_STATE
D_RET = 256
RET_HEADS = 4
CHUNK = 64
ROPE_BASE = 10000.0
EPS = 1e-6
D_IN = 3592
D_INP = 3712
W_SHARD = D_IN // N_DEV
O_GATE, O_FQ, O_FK, O_FV, O_SU, O_RQ, O_RK, O_RV, O_FL = 0, 1024, 1536, 2048, 2560, 2816, 3072, 3328, 3584

ADAM_LR, ADAM_B1, ADAM_B2, ADAM_EPS, ADAM_WD, ADAM_STEP = 0.001, 0.9, 0.999, 1e-08, 0.01, 10

TM = 256
TQ = 512
TS = 256
NEG = -1e30
VMEM_BIG = 56 * 1024 * 1024


def _pallas(body, **kw):
    return pl.pallas_call(body, **kw)


def _whole(shape):
    n = len(shape)
    return pl.BlockSpec(shape, lambda *_: (0,) * n)


def _rows(tm, width, col=0):
    return pl.BlockSpec((tm, width), lambda i: (i, col))


def _dot(a, b, dims=(((1,), (0,)), ((), ()))):
    return lax.dot_general(a.astype(_MXU), b.astype(_MXU), dims, preferred_element_type=F32)


_NT = (((1,), (1,)), ((), ()))
_TN = (((0,), (0,)), ((), ()))


def _norm_inproj(x, g, w):
    L = x.shape[0]

    def body(x_ref, g_ref, w_ref, p_ref, h_ref):
        xv = x_ref[...]
        r = lax.rsqrt(jnp.mean(xv * xv, axis=-1, keepdims=True) + EPS)
        h = (xv * r * g_ref[...]).astype(_MXU)
        h_ref[...] = h
        p_ref[...] = _dot(h, w_ref[...])

    return _pallas(body, out_shape=(jax.ShapeDtypeStruct((L, D_INP), F32), jax.ShapeDtypeStruct((L, D_MODEL), _MXU)),
                   grid=(L // TM,), in_specs=[_rows(TM, D_MODEL), _whole((1, D_MODEL)), _whole((D_MODEL, D_INP))],
                   out_specs=(_rows(TM, D_INP), _rows(TM, D_MODEL)), name="norm_inproj",
                   compiler_params=pltpu.CompilerParams(vmem_limit_bytes=VMEM_BIG))(x, g, w)


def _rms_bwd(xv, g, dh):
    r = lax.rsqrt(jnp.mean(xv * xv, axis=-1, keepdims=True) + EPS)
    xh = xv * r
    dg = jnp.sum(dh * xh, axis=0, keepdims=True)
    dxh = dh * g
    dx = r * (dxh - xh * jnp.mean(dxh * xh, axis=-1, keepdims=True))
    return dx, dg


def _inproj_bwd(pieces, w, x, g, dres, h):
    L = x.shape[0]
    n = len(pieces)

    def body(*refs):
        w_ref, x_ref, g_ref, dr_ref, h_ref, dx_ref, dg_ref, dw_ref = refs[n:]
        dproj = jnp.concatenate([r[...].astype(_MXU) for r in refs[:n]], axis=-1)
        dh = _dot(dproj, w_ref[...], _NT)
        dx, dg = _rms_bwd(x_ref[...], g_ref[...], dh)
        dx_ref[...] = dx + dr_ref[...]
        dw = _dot(h_ref[...], dproj, _TN)

        @pl.when(pl.program_id(0) == 0)
        def _():
            dg_ref[...] = dg
            dw_ref[...] = dw

        @pl.when(pl.program_id(0) != 0)
        def _():
            dg_ref[...] += dg
            dw_ref[...] += dw

    resident = pl.BlockSpec((D_MODEL, D_INP), lambda i: (0, 0), pipeline_mode=pl.Buffered(1))
    return _pallas(body, out_shape=(jax.ShapeDtypeStruct((L, D_MODEL), F32), jax.ShapeDtypeStruct((1, D_MODEL), F32),
                                    jax.ShapeDtypeStruct((D_MODEL, D_INP), F32)),
                   grid=(L // TM,),
                   in_specs=[_rows(TM, p.shape[1]) for p in pieces]
                   + [resident, _rows(TM, D_MODEL), _whole((1, D_MODEL)), _rows(TM, D_MODEL), _rows(TM, D_MODEL)],
                   out_specs=(_rows(TM, D_MODEL), _whole((1, D_MODEL)), resident), name="inproj_bwd",
                   compiler_params=pltpu.CompilerParams(vmem_limit_bytes=VMEM_BIG))(*pieces, w, x, g, dres, h)


PAIR = 2 * HEAD_DIM
N_AUX = 3


def _own(shape, h):
    return lax.broadcasted_iota(jnp.int32, shape, len(shape) - 1) // HEAD_DIM == h


def _hi_dot(a, b):
    return jnp.dot(a, b, precision=_HI, preferred_element_type=F32)


def _tri(n, lower):
    r = lax.broadcasted_iota(jnp.int32, (n, n), 0)
    c = lax.broadcasted_iota(jnp.int32, (n, n), 1)
    return jnp.where(r >= c if lower else r <= c, 1.0, 0.0).astype(F32)


def _fox_cumsum(proj, b):
    L = proj.shape[0]

    def body(fl_ref, b_ref, c_ref, carry_sc):
        @pl.when(pl.program_id(0) == 0)
        def _():
            carry_sc[...] = jnp.zeros((1, PAIR), F32)

        lane = lax.broadcasted_iota(jnp.int32, (TM, PAIR), 1)
        lf = jnp.where(lane < FOX_HEADS, jax.nn.log_sigmoid(fl_ref[...] + b_ref[...]), 0.0)
        cs = _hi_dot(_tri(TM, True), lf) + carry_sc[...]
        c_ref[...] = cs
        carry_sc[...] = cs[TM - 1:TM, :]

    return _pallas(body, out_shape=jax.ShapeDtypeStruct((L, PAIR), F32), grid=(L // TM,),
                   in_specs=[_rows(TM, PAIR, O_FL // PAIR), _whole((1, PAIR))], out_specs=_rows(TM, PAIR),
                   scratch_shapes=[pltpu.VMEM((1, PAIR), F32)], name="fox_cumsum")(proj, b)


def _fox_prep(proj, c):
    L = proj.shape[0]

    def body(q_ref, k_ref, v_ref, c_ref, qa_ref, ka_ref, kat_ref, vt_ref):
        lane = lax.broadcasted_iota(jnp.int32, (TM, PAIR), 1)
        cv = c_ref[...]
        for p in range(FOX_HEADS // 2):
            cols = slice(PAIR * p, PAIR * (p + 1))
            q2, k2 = q_ref[:, cols], k_ref[:, cols]
            vt_ref[p] = v_ref[:, cols].T.astype(_MXU)
            for e in range(2):
                h = 2 * p + e
                own = lane // HEAD_DIM == e
                a = lane - (HEAD_DIM if e == 0 else 0)
                pick = (lax.broadcasted_iota(jnp.int32, (PAIR, PAIR), 0) == h).astype(F32)
                rest = _hi_dot(cv, pick)
                aux_q = jnp.where((a >= N_AUX) & (a < 2 * N_AUX), 1.0, 0.0)
                aux_k = jnp.where((a >= 0) & (a < N_AUX), 1.0, 0.0)
                for n in range(N_AUX):
                    part = rest.astype(_MXU).astype(F32)
                    rest = rest - part
                    aux_q = jnp.where(a == n, part, aux_q)
                    aux_k = jnp.where(a == N_AUX + n, -part, aux_k)
                ka = jnp.where(own, k2, aux_k)
                qa_ref[h] = jnp.where(own, q2 * (1.0 / math.sqrt(HEAD_DIM)), aux_q).astype(_MXU)
                ka_ref[h] = ka.astype(_MXU)
                kat_ref[h] = ka.T.astype(_MXU)

    hl = jax.ShapeDtypeStruct((FOX_HEADS, L, PAIR), _MXU)
    nat = lambda o: _rows(TM, D_FOX, o // D_FOX)
    rows = pl.BlockSpec((FOX_HEADS, TM, PAIR), lambda i: (0, i, 0))
    return _pallas(
        body, out_shape=(hl, hl, jax.ShapeDtypeStruct((FOX_HEADS, PAIR, L), _MXU),
                         jax.ShapeDtypeStruct((FOX_HEADS // 2, PAIR, L), _MXU)),
        grid=(L // TM,), in_specs=[nat(O_FQ), nat(O_FK), nat(O_FV), _rows(TM, PAIR)],
        out_specs=(rows, rows, pl.BlockSpec((FOX_HEADS, PAIR, TM), lambda i: (0, 0, i)),
                   pl.BlockSpec((FOX_HEADS // 2, PAIR, TM), lambda i: (0, 0, i))),
        name="fox_prep")(proj, proj, proj, c)


def _key_le_query(tq):
    return lax.broadcasted_iota(jnp.int32, (tq, tq), 0) <= lax.broadcasted_iota(jnp.int32, (tq, tq), 1)


def _grid_ends(n0, n1):
    first = lambda: (pl.program_id(0) == 0) & (pl.program_id(1) == 0)
    last = lambda: (pl.program_id(0) == n0 - 1) & (pl.program_id(1) == n1 - 1)
    return first, last


def _fox_fwd(qa, ka, vt, ride=None):
    H, L, _ = qa.shape
    tq = min(TQ, L)
    nq = L // tq

    def body(qa_ref, ka_ref, vt_ref, o_ref, lse_ref, m_sc, l_sc, acc_sc):
        i = pl.program_id(1)
        m_sc[...] = jnp.full((2, 1, tq), NEG, F32)
        l_sc[...] = jnp.zeros((2, 1, tq), F32)
        acc_sc[...] = jnp.zeros((2, PAIR, tq), F32)

        def block(j, nk, masked):
            keys = pl.ds(pl.multiple_of(j * tq, tq), nk * tq)
            vt_blk = vt_ref[:, keys]
            sts = [_dot(ka_ref[e, keys, :], qa_ref[e], _NT) for e in range(2)]
            pts, alphas = [], []
            for e in range(2):
                st = jnp.where(_key_le_query(tq), sts[e], NEG) if masked else sts[e]
                m_prev = m_sc[e]
                m_new = jnp.maximum(m_prev, jnp.max(st, axis=0, keepdims=True))
                alphas.append(jnp.exp(m_prev - m_new))
                pt = jnp.exp(st - m_new)
                l_sc[e] = alphas[e] * l_sc[e] + jnp.sum(pt, axis=0, keepdims=True)
                m_sc[e] = m_new
                pts.append(pt.astype(_MXU))
            for e in range(2):
                acc_sc[e] = alphas[e] * acc_sc[e] + _dot(vt_blk, pts[e])

        def two_blocks(jj, carry):
            block(2 * jj, 2, False)
            return carry

        lax.fori_loop(0, i // 2, two_blocks, 0)

        @pl.when(i % 2 == 1)
        def _():
            block(i - 1, 1, False)

        block(i, 1, True)
        row = lax.broadcasted_iota(jnp.int32, (PAIR, tq), 0)
        ot = jnp.where(row < HEAD_DIM, acc_sc[0] / l_sc[0], acc_sc[1] / l_sc[1])
        o_ref[...] = ot.T
        for e in range(2):
            lse_ref[e] = m_sc[e] + jnp.log(l_sc[e])

    body, ex_in, ex_specs, ex_out, ex_sems = _riding(body, 3, 2, ride, *_grid_ends(H // 2, nq))
    res = _pallas(
        body, out_shape=(jax.ShapeDtypeStruct((L, D_FOX), F32), jax.ShapeDtypeStruct((H, 1, L), F32), *ex_out),
        grid=(H // 2, nq),
        in_specs=[pl.BlockSpec((2, tq, PAIR), lambda p, i: (p, i, 0)), pl.BlockSpec((2, L, PAIR), lambda p, i: (p, 0, 0)),
                  pl.BlockSpec((None, PAIR, L), lambda p, i: (p, 0, 0)), *ex_specs],
        out_specs=(pl.BlockSpec((tq, PAIR), lambda p, i: (i, p)), pl.BlockSpec((2, 1, tq), lambda p, i: (p, 0, i)),
                   *ex_specs),
        scratch_shapes=[pltpu.VMEM((2, 1, tq), F32), pltpu.VMEM((2, 1, tq), F32), pltpu.VMEM((2, PAIR, tq), F32), *ex_sems],
        name="fox_fwd" if ride is None else "fox_fwd_gather")(qa, ka, vt, *ex_in)
    return res[0], res[1], list(res[2:])


def _fox_bwd(qa, ka, kat, proj, do, o, lse, ride=None):
    H, L, _ = qa.shape
    tq = min(TQ, L)
    nq = L // tq

    def body(qa_ref, ka_ref, kat_ref, v_ref, do_ref, o_ref, lse_ref, dqt_ref, dk_ref, dv_ref, delta_sc, dk_sc, dv_sc):
        j = pl.program_id(1)

        @pl.when(j == 0)
        def _():
            head_rows = (lax.broadcasted_iota(jnp.int32, (8, PAIR), 1) // HEAD_DIM
                         == lax.broadcasted_iota(jnp.int32, (8, PAIR), 0)).astype(F32)
            delta_sc[...] = lax.dot_general(head_rows, do_ref[...] * o_ref[...], _NT, precision=_HI,
                                            preferred_element_type=F32)
            dqt_ref[...] = jnp.zeros((2, PAIR, L), F32)

        dk_sc[...] = jnp.zeros((2, tq, PAIR), F32)
        dv_sc[...] = jnp.zeros((tq, PAIR), F32)
        vb = v_ref[...]

        def block(i, masked):
            qs = pl.ds(pl.multiple_of(i * tq, tq), tq)
            dob = do_ref[qs, :]
            for e in range(2):
                own = _own((tq, PAIR), e)
                qh = qa_ref[e, qs, :]
                pt = jnp.exp(_dot(ka_ref[e], qh, _NT) - lse_ref[e, :, qs])
                if masked:
                    pt = jnp.where(_key_le_query(tq), pt, 0.0)
                dv_sc[...] += _dot(pt, jnp.where(own, dob, 0.0))
                dpt = _dot(jnp.where(own, vb, 0.0), dob, _NT)
                ds = (pt * (dpt - delta_sc[e:e + 1, qs])).astype(_MXU)
                dk_sc[e] += _dot(ds, qh)
                dqt_ref[e, :, qs] += _dot(kat_ref[e], ds)

        def off_diagonal(i, carry):
            block(i, False)
            return carry

        block(j, True)
        lax.fori_loop(j + 1, nq, off_diagonal, 0)
        dk_ref[...] = dk_sc[...]
        dv_ref[...] = dv_sc[...]

    nat = pl.BlockSpec((L, PAIR), lambda p, j: (0, p))
    body, ex_in, ex_specs, ex_out, ex_sems = _riding(body, 7, 3, ride, *_grid_ends(H // 2, nq))
    res = _pallas(
        body, out_shape=(jax.ShapeDtypeStruct((H, PAIR, L), F32), jax.ShapeDtypeStruct((H, L, PAIR), F32),
                         jax.ShapeDtypeStruct((L, D_FOX), F32), *ex_out),
        grid=(H // 2, nq),
        in_specs=[pl.BlockSpec((2, L, PAIR), lambda p, j: (p, 0, 0)), pl.BlockSpec((2, tq, PAIR), lambda p, j: (p, j, 0)),
                  pl.BlockSpec((2, PAIR, tq), lambda p, j: (p, 0, j)),
                  pl.BlockSpec((tq, PAIR), lambda p, j: (j, O_FV // PAIR + p)), nat, nat,
                  pl.BlockSpec((2, 1, L), lambda p, j: (p, 0, 0)), *ex_specs],
        out_specs=(pl.BlockSpec((2, PAIR, L), lambda p, j: (p, 0, 0)), pl.BlockSpec((2, tq, PAIR), lambda p, j: (p, j, 0)),
                   pl.BlockSpec((tq, PAIR), lambda p, j: (j, p)), *ex_specs),
        scratch_shapes=[pltpu.VMEM((8, L), F32), pltpu.VMEM((2, tq, PAIR), F32), pltpu.VMEM((tq, PAIR), F32), *ex_sems],
        name="fox_bwd" if ride is None else "fox_bwd_exchange",
        compiler_params=pltpu.CompilerParams(vmem_limit_bytes=VMEM_BIG))(qa, ka, kat, proj, do, o, lse, *ex_in)
    return res[0], res[1], res[2], list(res[3:])


def _fox_post_bwd(dqt, dkraw, proj, b):
    L = proj.shape[0]
    nb = L // TM

    def body(dqt_ref, dkr_ref, fl_ref, b_ref, dq_ref, dk_ref, dfl_ref, db_ref, carry_sc):
        first = pl.program_id(0) == 0

        @pl.when(first)
        def _():
            carry_sc[...] = jnp.zeros((1, PAIR), F32)

        lane = lax.broadcasted_iota(jnp.int32, (TM, PAIR), 1)
        rr = lax.broadcasted_iota(jnp.int32, (PAIR, PAIR), 0)
        cc = lax.broadcasted_iota(jnp.int32, (PAIR, PAIR), 1)
        dc = jnp.zeros((TM, PAIR), F32)
        for p in range(FOX_HEADS // 2):
            cols = slice(PAIR * p, PAIR * (p + 1))
            dqs = [dqt_ref[2 * p + e].T for e in range(2)]
            dks = [dkr_ref[2 * p + e] for e in range(2)]
            dq_ref[:, cols] = jnp.where(lane < HEAD_DIM, dqs[0], dqs[1]) * (1.0 / math.sqrt(HEAD_DIM))
            dk_ref[:, cols] = jnp.where(lane < HEAD_DIM, dks[0], dks[1])
            sums = jnp.zeros((TM, PAIR), F32)
            place = jnp.zeros((PAIR, PAIR), F32)
            for e in range(2):
                base = HEAD_DIM if e == 0 else 0
                sums = jnp.where(lane == base, dqs[e], jnp.where(lane == base + N_AUX, -dks[e], sums))
                place = jnp.where(((rr == base) | (rr == base + N_AUX)) & (cc == 2 * p + e), 1.0, place)
            dc = dc + _hi_dot(sums, place)
        rs = _hi_dot(_tri(TM, False), dc) + carry_sc[...]
        carry_sc[...] = rs[0:1, :]
        dfl = jnp.where(lane < FOX_HEADS, rs * jax.nn.sigmoid(-(fl_ref[...] + b_ref[...])), 0.0)
        dfl_ref[...] = dfl
        db = jnp.sum(dfl, axis=0, keepdims=True)

        @pl.when(first)
        def _():
            db_ref[...] = db

        @pl.when(jnp.logical_not(first))
        def _():
            db_ref[...] += db

    rev = lambda i: nb - 1 - i
    nat = pl.BlockSpec((TM, D_FOX), lambda i: (rev(i), 0))
    return _pallas(
        body, out_shape=(jax.ShapeDtypeStruct((L, D_FOX), F32),) * 2
        + (jax.ShapeDtypeStruct((L, PAIR), F32), jax.ShapeDtypeStruct((1, PAIR), F32)),
        grid=(nb,),
        in_specs=[pl.BlockSpec((FOX_HEADS, PAIR, TM), lambda i: (0, 0, rev(i))),
                  pl.BlockSpec((FOX_HEADS, TM, PAIR), lambda i: (0, rev(i), 0)),
                  pl.BlockSpec((TM, PAIR), lambda i: (rev(i), O_FL // PAIR)), _whole((1, PAIR))],
        out_specs=(nat, nat, pl.BlockSpec((TM, PAIR), lambda i: (rev(i), 0)), _whole((1, PAIR))),
        scratch_shapes=[pltpu.VMEM((1, PAIR), F32)], name="fox_post_bwd")(dqt, dkraw, proj, b)


def _s5_expand():
    r = lax.broadcasted_iota(jnp.int32, (S5_STATE, S5_STATE * S5_GROUP_CH), 0)
    c = lax.broadcasted_iota(jnp.int32, (S5_STATE, S5_STATE * S5_GROUP_CH), 1)
    return jnp.where(c // S5_GROUP_CH == r, 1.0, 0.0).astype(F32)


def _s5_disc_math(ar, ai, ldt, br, bi):
    dt = jnp.exp(ldt)
    mag = jnp.exp(ar * dt)
    lr = mag * jnp.cos(ai * dt)
    li = mag * jnp.sin(ai * dt)
    den = ar * ar + ai * ai
    fr = ((lr - 1.0) * ar + li * ai) / den
    fi = (li * ar - (lr - 1.0) * ai) / den
    e = _s5_expand()
    fre = jnp.dot(fr, e, precision=_HI, preferred_element_type=F32)
    fie = jnp.dot(fi, e, precision=_HI, preferred_element_type=F32)
    return lr, li, fre * br - fie * bi, fre * bi + fie * br


def _layer_blocks(arrs):
    return [pl.BlockSpec((None,) + a.shape[1:], lambda l: (l, 0, 0)) for a in arrs]


def _s5_disc(ar, ai, ldt, br, bi):
    def body(ar_ref, ai_ref, ldt_ref, br_ref, bi_ref, lr_ref, li_ref, bbr_ref, bbi_ref):
        lr, li, bbr, bbi = _s5_disc_math(ar_ref[...], ai_ref[...], ldt_ref[...], br_ref[...], bi_ref[...])
        lr_ref[...] = lr
        li_ref[...] = li
        bbr_ref[...] = bbr
        bbi_ref[...] = bbi

    ins = (ar, ai, ldt, br, bi)
    outs = (ar, ai, br, bi)
    return _pallas(body, out_shape=tuple(jax.ShapeDtypeStruct(a.shape, F32) for a in outs), grid=(DEPTH,),
                   in_specs=_layer_blocks(ins), out_specs=tuple(_layer_blocks(outs)), name="s5_disc")(*ins)


def _s5_disc_bwd(ar, ai, ldt, br, bi, dlr, dli, dbbr, dbbi):
    def body(ar_ref, ai_ref, ldt_ref, br_ref, bi_ref, dlr_ref, dli_ref, dbbr_ref, dbbi_ref,
             dar_ref, dai_ref, dldt_ref, dbr_ref, dbi_ref):
        _, vjp = jax.vjp(_s5_disc_math, ar_ref[...], ai_ref[...], ldt_ref[...], br_ref[...], bi_ref[...])
        dar, dai, dldt, dbr, dbi = vjp((dlr_ref[...], dli_ref[...], dbbr_ref[...], dbbi_ref[...]))
        dar_ref[...] = dar
        dai_ref[...] = dai
        dldt_ref[...] = dldt
        dbr_ref[...] = dbr
        dbi_ref[...] = dbi

    ins = (ar, ai, ldt, br, bi, dlr, dli, dbbr, dbbi)
    outs = (ar, ai, ldt, br, bi)
    return _pallas(body, out_shape=tuple(jax.ShapeDtypeStruct(a.shape, F32) for a in outs), grid=(DEPTH,),
                   in_specs=_layer_blocks(ins), out_specs=tuple(_layer_blocks(outs)), name="s5_disc_bwd")(*ins)


SLAB = 2 * S5_CH // 128


def _slab_rows(s, ts):
    return pl.ds(s, ts, stride=SLAB)


def _slab_pair(ref, s, ts):
    return jnp.concatenate([ref[_slab_rows(s, ts), :].astype(_MXU), ref[_slab_rows(s + 1, ts), :].astype(_MXU)], axis=-1)


def _s5_fwd(proj, wb, wc, lam, d, w_glu):
    L = proj.shape[0]
    ts = min(TS, L)

    def body(u_ref, wb_ref, wc_ref, lam_ref, d_ref, wg_ref, xs_ref, ypre_ref, ys_ref, b_sc, c_sc):
        @pl.when(pl.program_id(0) == 0)
        def _():
            c_sc[...] = jnp.zeros((SLAB, 128), F32)

        u = u_ref[...]
        ub = u.astype(_MXU)
        for s in range(0, SLAB, 2):
            b2 = _dot(ub, wb_ref[:, 128 * s:128 * (s + 2)])
            b_sc[_slab_rows(s, ts), :] = b2[:, :128]
            b_sc[_slab_rows(s + 1, ts), :] = b2[:, 128:]
        lr, li = lam_ref[0:8, :], lam_ref[8:16, :]

        def step(t, carry):
            xr, xi = carry
            row = pl.multiple_of(t * SLAB, SLAB)
            nr = lr * xr - li * xi + b_sc[pl.ds(row, 8), :]
            ni = lr * xi + li * xr + b_sc[pl.ds(row + 8, 8), :]
            xs_ref[pl.ds(row, 8), :] = nr
            xs_ref[pl.ds(row + 8, 8), :] = ni
            return nr, ni

        xr, xi = lax.fori_loop(0, ts, step, (c_sc[0:8, :], c_sc[8:16, :]), unroll=8)
        c_sc[0:8, :] = xr
        c_sc[8:16, :] = xi
        y = jnp.zeros((ts, D_S5), F32)
        for s in range(0, SLAB, 2):
            y = y + _dot(_slab_pair(xs_ref, s, ts), wc_ref[128 * s:128 * (s + 2), :])
        ypre_ref[...] = y
        y1 = jax.nn.gelu(y + d_ref[...] * u)
        ys_ref[...] = y1 * jax.nn.sigmoid(_dot(y1, wg_ref[...]))

    row = _rows(ts, D_S5)
    slabs = pl.BlockSpec((ts * SLAB, 128), lambda n: (n, 0))
    return _pallas(
        body, out_shape=(jax.ShapeDtypeStruct((L * SLAB, 128), F32), jax.ShapeDtypeStruct((L, D_S5), F32),
                         jax.ShapeDtypeStruct((L, D_S5), F32)),
        grid=(L // ts,),
        in_specs=[_rows(ts, D_S5, O_SU // D_S5), _whole((D_S5, 2 * S5_CH)), _whole((2 * S5_CH, D_S5)), _whole((SLAB, 128)),
                  _whole((1, D_S5)), _whole((D_S5, D_S5))],
        out_specs=(slabs, row, row),
        scratch_shapes=[pltpu.VMEM((ts * SLAB, 128), F32), pltpu.VMEM((SLAB, 128), F32)], name="s5_fwd")(
            proj, wb, wc, lam, d, w_glu)


def _s5_bwd(proj, ypre, dys, xs, wb, wc, lam, d, w_glu):
    L = proj.shape[0]
    ts = min(TS, L)
    nb = L // ts

    def body(u_ref, y_ref, dys_ref, xs_ref, xp_ref, wb_ref, wc_ref, lam_ref, d_ref, wg_ref,
             du_ref, dwb_ref, dwc_ref, dlam_ref, dd_ref, dwg_ref, dx_sc, g_sc, c_sc):
        n = pl.program_id(0)

        @pl.when(n == 0)
        def _():
            c_sc[...] = jnp.zeros((SLAB, 128), F32)
            dlam_ref[...] = jnp.zeros((SLAB, 128), F32)
            dwb_ref[...] = jnp.zeros((D_S5, 2 * S5_CH), F32)
            dwc_ref[...] = jnp.zeros((2 * S5_CH, D_S5), F32)
            dd_ref[...] = jnp.zeros((1, D_S5), F32)
            dwg_ref[...] = jnp.zeros((D_S5, D_S5), F32)

        u, dv, dout = u_ref[...], d_ref[...], dys_ref[...]
        y1, gelu_vjp = jax.vjp(jax.nn.gelu, y_ref[...] + dv * u)
        sg = jax.nn.sigmoid(_dot(y1, wg_ref[...]))
        dz = dout * y1 * sg * (1.0 - sg)
        dy, = gelu_vjp(dout * sg + _dot(dz, wg_ref[...], _NT))
        dd_ref[...] += jnp.sum(dy * u, axis=0, keepdims=True)
        dwg_ref[...] += _dot(y1, dz, _TN)
        dyb = dy.astype(_MXU)
        for s in range(0, SLAB, 2):
            cols = slice(128 * s, 128 * (s + 2))
            dx2 = _dot(dyb, wc_ref[cols, :], _NT)
            dx_sc[_slab_rows(s, ts), :] = dx2[:, :128]
            dx_sc[_slab_rows(s + 1, ts), :] = dx2[:, 128:]
            dwc_ref[cols, :] += _dot(_slab_pair(xs_ref, s, ts), dyb, _TN)
        lr, li = lam_ref[0:8, :], lam_ref[8:16, :]

        def adjoint(row, pr, pi, carry):
            gr, gi, ar, ai = carry
            nr = dx_sc[pl.ds(row, 8), :] + lr * gr + li * gi
            ni = dx_sc[pl.ds(row + 8, 8), :] - li * gr + lr * gi
            g_sc[pl.ds(row, 8), :] = nr
            g_sc[pl.ds(row + 8, 8), :] = ni
            return nr, ni, ar + nr * pr + ni * pi, ai - nr * pi + ni * pr

        def step(k, carry):
            row = pl.multiple_of((ts - 1 - k) * SLAB, SLAB)
            prev = pl.multiple_of((ts - 2 - k) * SLAB, SLAB)
            return adjoint(row, xs_ref[pl.ds(prev, 8), :], xs_ref[pl.ds(prev + 8, 8), :], carry)

        z = jnp.zeros((8, 128), F32)
        carry = lax.fori_loop(0, ts - 1, step, (c_sc[0:8, :], c_sc[8:16, :], z, z), unroll=8)
        has_prev = jnp.where(n == nb - 1, 0.0, 1.0)
        gr, gi, ar, ai = adjoint(0, xp_ref[0:8, :] * has_prev, xp_ref[8:16, :] * has_prev, carry)
        c_sc[0:8, :] = gr
        c_sc[8:16, :] = gi
        dlam_ref[0:8, :] += ar
        dlam_ref[8:16, :] += ai
        ub = u.astype(_MXU)
        du = dy * dv
        for s in range(0, SLAB, 2):
            cols = slice(128 * s, 128 * (s + 2))
            gs = _slab_pair(g_sc, s, ts)
            du = du + _dot(gs, wb_ref[:, cols], _NT)
            dwb_ref[:, cols] += _dot(ub, gs, _TN)
        du_ref[...] = du

    blk = lambda n: nb - 1 - n
    row = pl.BlockSpec((ts, D_S5), lambda n: (blk(n), 0))
    return _pallas(
        body, out_shape=(jax.ShapeDtypeStruct((L, D_S5), F32), jax.ShapeDtypeStruct((D_S5, 2 * S5_CH), F32),
                         jax.ShapeDtypeStruct((2 * S5_CH, D_S5), F32), jax.ShapeDtypeStruct((SLAB, 128), F32),
                         jax.ShapeDtypeStruct((1, D_S5), F32), jax.ShapeDtypeStruct((D_S5, D_S5), F32)),
        grid=(nb,),
        in_specs=[pl.BlockSpec((ts, D_S5), lambda n: (blk(n), O_SU // D_S5)), row, row,
                  pl.BlockSpec((ts * SLAB, 128), lambda n: (blk(n), 0)),
                  pl.BlockSpec((SLAB, 128), lambda n: (jnp.maximum(blk(n) * ts - 1, 0), 0)),
                  _whole((D_S5, 2 * S5_CH)), _whole((2 * S5_CH, D_S5)), _whole((SLAB, 128)), _whole((1, D_S5)),
                  _whole((D_S5, D_S5))],
        out_specs=(row, _whole((D_S5, 2 * S5_CH)), _whole((2 * S5_CH, D_S5)), _whole((SLAB, 128)), _whole((1, D_S5)),
                   _whole((D_S5, D_S5))),
        scratch_shapes=[pltpu.VMEM((ts * SLAB, 128), F32), pltpu.VMEM((ts * SLAB, 128), F32), pltpu.VMEM((SLAB, 128), F32)],
        name="s5_bwd")(proj, ypre, dys, xs, xs, wb, wc, lam, d, w_glu)


def _rot(z, cos, sin):
    lane = lax.broadcasted_iota(jnp.int32, z.shape, 1)
    zs = z * sin
    half = HEAD_DIM // 2
    return z * cos + jnp.where(lane % HEAD_DIM < half, pltpu.roll(zs, PAIR - half, 1), pltpu.roll(zs, half, 1))


def _head_avg():
    r = lax.broadcasted_iota(jnp.int32, (PAIR, PAIR), 0) // HEAD_DIM
    c = lax.broadcasted_iota(jnp.int32, (PAIR, PAIR), 1) // HEAD_DIM
    return jnp.where(r == c, 1.0 / HEAD_DIM, 0.0).astype(F32)


def _ret_tables(tq):
    lg = jnp.log1p(-(2.0 ** (-5.0 - jnp.arange(RET_HEADS, dtype=F32))))
    scale = 1.0 / math.sqrt(HEAD_DIM)
    pos = jnp.arange(tq)
    n = pos.astype(F32)
    dist = jnp.abs(n[:, None] - n[None, :])
    ok = (pos[None, :] // CHUNK) <= (pos[:, None] // CHUNK)
    w = jnp.where(ok[None], scale * jnp.exp(lg[:, None, None] * dist[None]), 0.0)
    lgl = jnp.repeat(lg, HEAD_DIM)
    dq_tab = scale * jnp.exp(lgl[None, :] * (n[:, None] + 1.0))
    dk_tab = jnp.exp(lgl[None, :] * (tq - 1.0 - n[:, None]))
    blk = jnp.arange(PAIR) // HEAD_DIM
    bd = (blk[:, None] == blk[None, :]).astype(F32)
    gbd = bd[None] * jnp.exp(lgl.reshape(RET_HEADS // 2, PAIR)[:, :, None] * tq)
    return dict(w=w, wt=w.transpose(0, 2, 1), dq=dq_tab, dk=dk_tab, gbd=gbd, bd=bd)


def _ret_specs(tq, nq, rev):
    blk = (lambda i: nq - 1 - i) if rev else (lambda i: i)
    col = lambda o: pl.BlockSpec((tq, PAIR), lambda p, i: (blk(i), o // PAIR + p))
    return dict(
        rq=col(O_RQ), rk=col(O_RK), rv=col(O_RV), nat=col(0),
        w=pl.BlockSpec((2, tq, tq), lambda p, i: (p, 0, 0)), tab=pl.BlockSpec((tq, PAIR), lambda p, i: (0, p)),
        gbd=pl.BlockSpec((None, PAIR, PAIR), lambda p, i: (p, 0, 0)), bd=pl.BlockSpec((PAIR, PAIR), lambda p, i: (0, 0)),
        gn=pl.BlockSpec((1, PAIR), lambda p, i: (0, p)),
        st=pl.BlockSpec((None, None, PAIR, PAIR), lambda p, i: (p, blk(i), 0, 0)))


def _ret_fwd(proj, cos_t, sin_t, tabs, gn):
    L = proj.shape[0]
    tq = tabs["w"].shape[1]
    nq = L // tq

    def body(rq_ref, rk_ref, rv_ref, cos_ref, sin_ref, w_ref, dqt_ref, dkt_ref, gbd_ref, bd_ref, gn_ref,
             o_ref, y_ref, st_ref, s_sc):
        @pl.when(pl.program_id(1) == 0)
        def _():
            s_sc[...] = jnp.zeros((PAIR, PAIR), F32)

        state = s_sc[...]
        st_ref[...] = state
        cos, sin = cos_ref[...], sin_ref[...]
        q2, k2, v2 = _rot(rq_ref[...], cos, sin), _rot(rk_ref[...], cos, sin), rv_ref[...]
        o = _dot(q2 * dqt_ref[...], state)
        for h in range(2):
            own = _own((tq, PAIR), h)
            a = _dot(jnp.where(own, q2, 0.0), k2, _NT) * w_ref[h]
            o = o + _dot(a, jnp.where(own, v2, 0.0))
        s_sc[...] = gbd_ref[...] * state + bd_ref[...] * _dot(k2 * dkt_ref[...], v2, _TN)
        o_ref[...] = o
        avg = _head_avg()
        oc = o - _hi_dot(o, avg)
        y_ref[...] = oc * lax.rsqrt(_hi_dot(oc * oc, avg) + EPS) * gn_ref[...]

    sp = _ret_specs(tq, nq, False)
    nat = jax.ShapeDtypeStruct((L, D_RET), F32)
    return _pallas(
        body, out_shape=(nat, nat, jax.ShapeDtypeStruct((RET_HEADS // 2, nq, PAIR, PAIR), F32)), grid=(RET_HEADS // 2, nq),
        in_specs=[sp["rq"], sp["rk"], sp["rv"], sp["nat"], sp["nat"], sp["w"], sp["tab"], sp["tab"], sp["gbd"], sp["bd"],
                  sp["gn"]],
        out_specs=(sp["nat"], sp["nat"], sp["st"]), scratch_shapes=[pltpu.VMEM((PAIR, PAIR), F32)],
        name="ret_fwd")(proj, proj, proj, cos_t, sin_t, tabs["w"], tabs["dq"], tabs["dk"], tabs["gbd"], tabs["bd"], gn)


def _ret_bwd(proj, cos_t, sin_t, tabs, gn, o_pre, dy, states):
    L = proj.shape[0]
    tq = tabs["w"].shape[1]
    nq = L // tq

    def body(rq_ref, rk_ref, rv_ref, cos_ref, sin_ref, w_ref, wt_ref, dqt_ref, dkt_ref, gbd_ref, bd_ref, gn_ref,
             o_ref, dy_ref, st_ref, drq_ref, drk_ref, drv_ref, dgn_ref, g_sc):
        first = pl.program_id(1) == 0

        @pl.when(first)
        def _():
            g_sc[...] = jnp.zeros((PAIR, PAIR), F32)

        cos, sin = cos_ref[...], sin_ref[...]
        q2, k2, v2 = _rot(rq_ref[...], cos, sin), _rot(rk_ref[...], cos, sin), rv_ref[...]
        avg = _head_avg()
        ov, dyv = o_ref[...], dy_ref[...]
        oc = ov - _hi_dot(ov, avg)
        r = lax.rsqrt(_hi_dot(oc * oc, avg) + EPS)
        oh = oc * r
        dgn = jnp.sum(dyv * oh, axis=0, keepdims=True)
        doh = dyv * gn_ref[...]
        do = r * (doh - _hi_dot(doh, avg) - oh * _hi_dot(doh * oh, avg))
        state, g = st_ref[...], g_sc[...]
        dqt, dkt = dqt_ref[...], dkt_ref[...]
        dq = _dot(do, state, _NT) * dqt
        dk = _dot(v2, g, _NT) * dkt
        dv = _dot(k2 * dkt, g)
        g_sc[...] = gbd_ref[...] * g + bd_ref[...] * _dot(q2 * dqt, do, _TN)
        for h in range(2):
            own = _own((tq, PAIR), h)
            qm, dom = jnp.where(own, q2, 0.0), jnp.where(own, do, 0.0)
            dv = dv + _dot(_dot(k2, qm, _NT) * wt_ref[h], dom)
            dq = dq + _dot(_dot(dom, v2, _NT) * w_ref[h], jnp.where(own, k2, 0.0))
            dk = dk + _dot(_dot(v2, dom, _NT) * wt_ref[h], qm)
        drq_ref[...] = _rot(dq, cos, -sin)
        drk_ref[...] = _rot(dk, cos, -sin)
        drv_ref[...] = dv

        @pl.when(first)
        def _():
            dgn_ref[...] = dgn

        @pl.when(jnp.logical_not(first))
        def _():
            dgn_ref[...] += dgn

    sp = _ret_specs(tq, nq, True)
    nat = jax.ShapeDtypeStruct((L, D_RET), F32)
    return _pallas(
        body, out_shape=(nat, nat, nat, jax.ShapeDtypeStruct((1, D_RET), F32)), grid=(RET_HEADS // 2, nq),
        in_specs=[sp["rq"], sp["rk"], sp["rv"], sp["nat"], sp["nat"], sp["w"], sp["w"], sp["tab"], sp["tab"], sp["gbd"],
                  sp["bd"], sp["gn"], sp["nat"], sp["nat"], sp["st"]],
        out_specs=(sp["nat"], sp["nat"], sp["nat"], sp["gn"]), scratch_shapes=[pltpu.VMEM((PAIR, PAIR), F32)],
        name="ret_bwd")(proj, proj, proj, cos_t, sin_t, tabs["w"], tabs["wt"], tabs["dq"], tabs["dk"], tabs["gbd"],
                        tabs["bd"], gn, o_pre, dy, states)


def _gate_out(yf, ys, yr, proj, x, w):
    L = x.shape[0]

    def body(yf_ref, ys_ref, yr_ref, g_ref, x_ref, w_ref, xn_ref):
        cat = jnp.concatenate([yf_ref[...], ys_ref[...], yr_ref[...]], axis=-1)
        xn_ref[...] = x_ref[...] + _dot(cat * jax.nn.silu(g_ref[...]), w_ref[...])

    full = _rows(TM, D_MODEL)
    return _pallas(body, out_shape=jax.ShapeDtypeStruct((L, D_MODEL), F32), grid=(L // TM,),
                   in_specs=[_rows(TM, D_FOX), _rows(TM, D_S5), _rows(TM, D_RET), _rows(TM, D_MODEL, O_GATE // D_MODEL),
                             full, _whole((D_MODEL, D_MODEL))],
                   out_specs=full, name="gate_out")(yf, ys, yr, proj, x, w)


def _gate_out_bwd(dxn, w, yf, ys, yr, proj):
    L = dxn.shape[0]

    def body(dx_ref, w_ref, yf_ref, ys_ref, yr_ref, g_ref, dyf_ref, dys_ref, dyr_ref, dg_ref, dw_ref):
        dxv = dx_ref[...].astype(_MXU)
        dy = _dot(dxv, w_ref[...], _NT)
        g = g_ref[...]
        sg = jax.nn.sigmoid(g)
        silu = g * sg
        dcat = dy * silu
        dyf_ref[...] = dcat[:, :D_FOX]
        dys_ref[...] = dcat[:, D_FOX:D_FOX + D_S5]
        dyr_ref[...] = dcat[:, D_FOX + D_S5:]
        cat = jnp.concatenate([yf_ref[...], ys_ref[...], yr_ref[...]], axis=-1)
        dg_ref[...] = dy * cat * (sg * (1.0 + g * (1.0 - sg)))
        dw = _dot(cat * silu, dxv, _TN)

        @pl.when(pl.program_id(0) == 0)
        def _():
            dw_ref[...] = dw

        @pl.when(pl.program_id(0) != 0)
        def _():
            dw_ref[...] += dw

    full = _rows(TM, D_MODEL)
    f, s, r = _rows(TM, D_FOX), _rows(TM, D_S5), _rows(TM, D_RET)
    sq = _whole((D_MODEL, D_MODEL))
    return _pallas(body, out_shape=(jax.ShapeDtypeStruct((L, D_FOX), F32), jax.ShapeDtypeStruct((L, D_S5), F32),
                                    jax.ShapeDtypeStruct((L, D_RET), F32), jax.ShapeDtypeStruct((L, D_MODEL), F32),
                                    jax.ShapeDtypeStruct((D_MODEL, D_MODEL), F32)),
                   grid=(L // TM,), in_specs=[full, sq, f, s, r, _rows(TM, D_MODEL, O_GATE // D_MODEL)],
                   out_specs=(f, s, r, full, sq), name="gate_out_bwd")(dxn, w, yf, ys, yr, proj)


def _final_loss(x, g, tgt):
    L = x.shape[0]

    def body(x_ref, g_ref, t_ref, loss_ref, dx_ref, dg_ref):
        xv, gv = x_ref[...], g_ref[...]
        r = lax.rsqrt(jnp.mean(xv * xv, axis=-1, keepdims=True) + EPS)
        err = xv * r * gv - t_ref[...]
        part = 0.5 * jnp.sum(jnp.mean(err * err, axis=-1, keepdims=True), axis=0, keepdims=True)
        dx, dg = _rms_bwd(xv, gv, err * (1.0 / D_MODEL))
        dx_ref[...] = dx

        @pl.when(pl.program_id(0) == 0)
        def _():
            loss_ref[...] = part
            dg_ref[...] = dg

        @pl.when(pl.program_id(0) != 0)
        def _():
            loss_ref[...] += part
            dg_ref[...] += dg

    full = _rows(TM, D_MODEL)
    return _pallas(body, out_shape=(jax.ShapeDtypeStruct((1, 1), F32), jax.ShapeDtypeStruct((L, D_MODEL), F32),
                                    jax.ShapeDtypeStruct((1, D_MODEL), F32)),
                   grid=(L // TM,), in_specs=[full, _whole((1, D_MODEL)), full],
                   out_specs=(_whole((1, 1)), full, _whole((1, D_MODEL))), name="final_loss")(x, g, tgt)


def _block_diag(blocks):
    n, g, r, c = blocks.shape
    eye = jnp.eye(g, dtype=blocks.dtype)
    return (blocks[:, :, :, None, :] * eye[None, :, None, :, None]).reshape(n, g * r, g * c)


def _diag_blocks(m, g):
    n, r, c = m.shape[0], m.shape[1] // g, m.shape[2] // g
    eye = jnp.eye(g, dtype=m.dtype)
    return jnp.sum(m.reshape(n, g, r, g, c) * eye[None, :, None, :, None], axis=3)


def _rope_tables(L):
    half = HEAD_DIM // 2
    freqs = ROPE_BASE ** (-jnp.arange(half, dtype=F32) / half)
    ang = jnp.arange(L, dtype=F32)[:, None] * freqs[None, :]
    cos, sin = jnp.cos(ang), jnp.sin(ang)
    cos_t = jnp.tile(jnp.concatenate([cos, cos], axis=-1), (1, RET_HEADS))
    sin_t = jnp.tile(jnp.concatenate([sin, -sin], axis=-1), (1, RET_HEADS))
    return cos_t, sin_t


def _s5_disc_args(small):
    g, s, ch = S5_GROUPS, S5_STATE, S5_GROUP_CH
    return (small["s5_a_re"], small["s5_a_im"], small["s5_log_dt"][:, :, None],
            small["s5_b_re"].reshape(DEPTH, g, s * ch), small["s5_b_im"].reshape(DEPTH, g, s * ch))


def _s5_mats(small):
    g, s, ch = S5_GROUPS, S5_STATE, S5_GROUP_CH
    lr, li, bbr, bbi = _s5_disc(*_s5_disc_args(small))
    lam = jnp.concatenate([lr.reshape(DEPTH, 8, 128), li.reshape(DEPTH, 8, 128)], axis=1)
    wb = jnp.concatenate([_block_diag(b.reshape(DEPTH, g, s, ch).transpose(0, 1, 3, 2)) for b in (bbr, bbi)], axis=2)
    wc = jnp.concatenate([_block_diag(c.transpose(0, 1, 3, 2)) for c in (small["s5_c_re"], -small["s5_c_im"])], axis=1)
    return lam, wb.astype(_MXU), wc.astype(_MXU)


def _s5_param_grads(small, dwb, dwc, dlam):
    g, s, ch = S5_GROUPS, S5_STATE, S5_GROUP_CH
    dc = [_diag_blocks(m, g).transpose(0, 1, 3, 2) for m in (dwc[:, :S5_CH], dwc[:, S5_CH:])]
    dbb = [_diag_blocks(m, g).transpose(0, 1, 3, 2).reshape(DEPTH, g, s * ch) for m in (dwb[:, :, :S5_CH], dwb[:, :, S5_CH:])]
    dar, dai, dldt, dbr, dbi = _s5_disc_bwd(*_s5_disc_args(small), dlam[:, :8].reshape(DEPTH, g, s),
                                            dlam[:, 8:].reshape(DEPTH, g, s), dbb[0], dbb[1])
    shp = (DEPTH, g, s, ch)
    return dict(s5_a_re=dar, s5_a_im=dai, s5_log_dt=dldt.reshape(DEPTH, g), s5_b_re=dbr.reshape(shp),
                s5_b_im=dbi.reshape(shp), s5_c_re=dc[0], s5_c_im=-dc[1])


def _layer_fwd(x, p, rope, ride=None):
    L = x.shape[0]
    cos_t, sin_t, ret_tabs = rope
    s = {"x": x}
    proj, h = _norm_inproj(x, p["norm_w"], p["w_in"])
    s["proj"], s["h"] = proj, h
    qa, ka, kat, vt = _fox_prep(proj, _fox_cumsum(proj, p["b_f"]))
    yf, lse, landed = _fox_fwd(qa, ka, vt, ride)
    s.update(qa=qa, ka=ka, kat=kat, lse=lse, yf=yf)
    xs, ypre, ys = _s5_fwd(proj, p["wb"], p["wc"], p["lam"], p["d"], p["w_glu"])
    s.update(xs=xs, ypre=ypre, ys=ys)
    o_pre, yr, states = _ret_fwd(proj, cos_t, sin_t, ret_tabs, p["gn_w"])
    s.update(o_pre=o_pre, yr=yr, states=states)
    return _gate_out(yf, ys, yr, proj, x, p["w_out"]), s, landed


def _layer_bwd(dxn, s, p, rope, ride=None):
    L = dxn.shape[0]
    cos_t, sin_t, ret_tabs = rope
    g = {}
    proj = s["proj"]
    dyf, dys, dyr, dgate, g["w_out"] = _gate_out_bwd(dxn, p["w_out"], s["yf"], s["ys"], s["yr"], proj)
    drq, drk, drv, dgn = _ret_bwd(proj, cos_t, sin_t, ret_tabs, p["gn_w"], s["o_pre"], dyr, s["states"])
    g["ret_gn_w"] = dgn.reshape(D_RET)
    dsu, g["wb"], g["wc"], g["lam"], dd, g["s5_w_glu"] = _s5_bwd(proj, s["ypre"], dys, s["xs"], p["wb"], p["wc"], p["lam"],
                                                                 p["d"], p["w_glu"])
    g["s5_d"] = dd.reshape(D_S5)
    dqt, dkraw, dv, landed = _fox_bwd(s["qa"], s["ka"], s["kat"], proj, dyf, s["yf"], s["lse"], ride)
    dq, dk, dfl, dbf = _fox_post_bwd(dqt, dkraw, proj, p["b_f"])
    g["fox_b_f"] = dbf[0, :FOX_HEADS]
    pieces = [dgate, dq, dk, dv, dsu, drq, drk, drv, dfl]
    dx, dnw, g["w_in"] = _inproj_bwd(pieces, p["w_in"], s["x"], p["norm_w"], dxn, s["h"])
    g["norm_w"] = dnw.reshape(D_MODEL)
    return dx, g, landed


def _layer_params(l, w_in_p, w_glu, w_out, small, s5_mats):
    lam, wb, wc = s5_mats
    return dict(
        norm_w=small["norm_w"][l][None], w_in=w_in_p, b_f=jnp.pad(small["fox_b_f"][l], (0, PAIR - FOX_HEADS))[None],
        lam=lam[l], wb=wb[l], wc=wc[l], d=small["s5_d"][l][None], w_glu=w_glu, gn_w=small["ret_gn_w"][l][None], w_out=w_out)


_SHARDED = ("w_in", "s5_w_glu", "w_out")
_WIRE = jnp.bfloat16


_RUNS = ((2568, 3592, O_GATE), (0, 1536, O_FQ), (1544, 2568, O_SU), (1536, 1544, O_FL))


def _shard_pieces():
    out = []
    for a, b, pad in _RUNS:
        while a < b:
            j = a // W_SHARD
            e = min(b, (j + 1) * W_SHARD)
            out.append((j, a - j * W_SHARD, e - j * W_SHARD, pad))
            pad, a = pad + e - a, e
    return out


def _gathered_weights(g_in, g_glu, g_out):
    cols = [g_in[j, :, a:e] for j, a, e, _ in _shard_pieces()]
    cols.append(jnp.zeros((D_MODEL, D_INP - O_FL - FOX_HEADS), g_in.dtype))
    return jnp.concatenate(cols, axis=1), g_glu.reshape(D_S5, D_S5), g_out.reshape(D_MODEL, D_MODEL)


def _grad_slots(g):
    w_in = g["w_in"].astype(_WIRE)
    slots = []
    for j in range(N_DEV):
        mine = sorted((a, e, pad) for jj, a, e, pad in _shard_pieces() if jj == j)
        slots.append(jnp.concatenate([w_in[:, pad:pad + e - a] for a, e, pad in mine], axis=1))
    return [jnp.stack(slots), g["s5_w_glu"].reshape(N_DEV, D_S5 // N_DEV, D_S5).astype(_WIRE),
            g["w_out"].reshape(N_DEV, D_MODEL // N_DEV, D_MODEL).astype(_WIRE)]


def _step_grads(x, tgt, small, full=None, shards=None):
    L = x.shape[0]
    rope = _rope_tables(L) + (_ret_tables(min(TQ, L)),)
    s5_mats = _s5_mats(small)
    gather = [False] * len(_SHARDED)
    if shards is not None:
        nxt = _gathered_weights(*_exchange([s[0] for s in shards], gather, "gather_layer0"))
    saved, params = [], []
    for l in range(DEPTH):
        weights = nxt if shards is not None else tuple(f[l] for f in full)
        ride = ([s[l + 1] for s in shards], gather) if shards is not None and l + 1 < DEPTH else None
        params.append(_layer_params(l, *weights, small, s5_mats))
        x, s, landed = _layer_fwd(x, params[l], rope, ride)
        if ride is not None:
            nxt = _gathered_weights(*landed)
        saved.append(s)
    loss, dx, dfw = _final_loss(x, small["final_norm_w"][None], tgt)
    grads, partials, waiting = [None] * DEPTH, [None] * DEPTH, None
    scatter = [True] * len(_SHARDED)
    for l in reversed(range(DEPTH)):
        dx, grads[l], landed = _layer_bwd(dx, saved[l], params[l], rope, (waiting, scatter) if waiting is not None else None)
        if waiting is not None:
            partials[l + 1] = landed
        if shards is not None:
            waiting = _grad_slots(grads[l])
    stack = lambda n: jnp.stack([g[n] for g in grads])
    small_g = {n: stack(n) for n in ("norm_w", "fox_b_f", "s5_d", "ret_gn_w")}
    small_g.update(_s5_param_grads(small, stack("wb"), stack("wc"), stack("lam")), final_norm_w=dfw.reshape(D_MODEL))
    if shards is None:
        return loss, dx, grads, small_g
    packed = _pack([small_g[n] for n in _SMALL]).astype(_WIRE)
    landed = _exchange(waiting + [packed], scatter + [False], "exchange_layer0")
    partials[0] = landed[:-1]
    return loss, dx, grads, small_g, partials, landed[-1]


_MESH = pl.DeviceIdType.MESH
_ANY = pl.BlockSpec(memory_space=pl.ANY)


def _me_and_peers():
    x, y, c = lax.axis_index("x"), lax.axis_index("y"), lax.axis_index("c")
    flip = lambda a, bit: (1 - a) if bit else a
    peers = []
    for r in range(1, N_DEV):
        px, py, pc = flip(x, (r >> 2) & 1), flip(y, (r >> 1) & 1), flip(c, r & 1)
        peers.append(((px, py, pc), 4 * px + 2 * py + pc))
    return 4 * x + 2 * y + c, peers


def _exchange_copies(srcs, dsts, sems, scatter):
    send_sems, recv_sems, local_sems = sems
    me, peers = _me_and_peers()
    pick = lambda t, to: srcs[t].at[to] if scatter[t] else srcs[t]
    own = [pltpu.make_async_copy(pick(t, me), dsts[t].at[me], local_sems.at[t]) for t in range(len(srcs))]
    sends, waits = [], []
    for r, (dev, idx) in enumerate(peers):
        for t in range(len(srcs)):
            for land, out in ((me, sends), (idx, waits)):
                out.append(pltpu.make_async_remote_copy(pick(t, idx), dsts[t].at[land], send_sems.at[t, r], recv_sems.at[t, r],
                                                        device_id=dev, device_id_type=_MESH))
    return own, sends, waits


def _exchange_start(srcs, dsts, sems, scatter):
    own, sends, _ = _exchange_copies(srcs, dsts, sems, scatter)
    for cp in own + sends:
        cp.start()


def _exchange_wait(srcs, dsts, sems, scatter):
    own, _, waits = _exchange_copies(srcs, dsts, sems, scatter)
    for cp in waits + own:
        cp.wait()


def _exchange_shapes(arrs, scatter):
    outs = [jax.ShapeDtypeStruct(a.shape if sc else (N_DEV,) + a.shape, a.dtype) for a, sc in zip(arrs, scatter)]
    n = len(arrs)
    sems = [pltpu.SemaphoreType.DMA((n, N_DEV - 1)), pltpu.SemaphoreType.DMA((n, N_DEV - 1)), pltpu.SemaphoreType.DMA((n,))]
    return outs, sems


def _exchange(arrs, scatter, name):
    n = len(arrs)

    def body(*refs):
        _exchange_start(refs[:n], refs[n:2 * n], refs[2 * n:], scatter)
        _exchange_wait(refs[:n], refs[n:2 * n], refs[2 * n:], scatter)

    outs, sems = _exchange_shapes(arrs, scatter)
    return _pallas(body, out_shape=tuple(outs), in_specs=[_ANY] * n, out_specs=tuple([_ANY] * n), scratch_shapes=sems,
                   name=name)(*arrs)


def _riding(body, n_in, n_out, ride, is_first, is_last):
    if ride is None:
        return body, [], [], [], []
    arrs, scatter = ride
    n = len(arrs)
    outs, sems = _exchange_shapes(arrs, scatter)

    def wrapped(*refs):
        ins, srcs = refs[:n_in], refs[n_in:n_in + n]
        own_outs, dsts = refs[n_in + n:n_in + n + n_out], refs[n_in + n + n_out:n_in + 2 * n + n_out]
        scratch, ex_sems = refs[n_in + 2 * n + n_out:-3], refs[-3:]

        @pl.when(is_first())
        def _():
            _exchange_start(srcs, dsts, ex_sems, scatter)

        body(*ins, *own_outs, *scratch)

        @pl.when(is_last())
        def _():
            _exchange_wait(srcs, dsts, ex_sems, scatter)

    return wrapped, list(arrs), [_ANY] * n, outs, sems


def _adamw(parts, w, m, v, name):
    n, nb, rows, cols = parts.shape
    tm = next(t for t in (256, 128, 64, 32, 16) if rows % t == 0)

    def body(p_ref, w_ref, m_ref, v_ref, g_ref, d_ref, nm_ref, nv_ref):
        g = p_ref[0].astype(F32)
        for i in range(1, n):
            g = g + p_ref[i].astype(F32)
        nm = ADAM_B1 * m_ref[...] + (1.0 - ADAM_B1) * g
        nv = ADAM_B2 * v_ref[...] + (1.0 - ADAM_B2) * jnp.square(g)
        m_hat = nm / (1.0 - ADAM_B1 ** ADAM_STEP)
        v_hat = nv / (1.0 - ADAM_B2 ** ADAM_STEP)
        g_ref[...] = g
        d_ref[...] = -ADAM_LR * (m_hat / (jnp.sqrt(v_hat) + ADAM_EPS) + ADAM_WD * w_ref[...])
        nm_ref[...] = nm
        nv_ref[...] = nv

    row = pl.BlockSpec((None, tm, cols), lambda b, i: (b, i, 0))
    return _pallas(body, out_shape=(jax.ShapeDtypeStruct((nb, rows, cols), F32),) * 4, grid=(nb, rows // tm),
                   in_specs=[pl.BlockSpec((n, None, tm, cols), lambda b, i: (0, b, i, 0)), row, row, row],
                   out_specs=(row,) * 4, name=name)(parts, w, m, v)


_WEIGHTS = ("norm_w", "w_in", "fox_b_f", "s5_a_re", "s5_a_im", "s5_b_re", "s5_b_im", "s5_c_re", "s5_c_im", "s5_d",
            "s5_log_dt", "s5_w_glu", "ret_gn_w", "w_out", "final_norm_w")
_SMALL = tuple(n for n in _WEIGHTS if n not in _SHARDED)
_LANES = 128


def _pack(arrs):
    flat = jnp.concatenate([a.reshape(-1) for a in arrs])
    rows = -(-flat.shape[0] // (_LANES * _LANES)) * _LANES
    return jnp.pad(flat, (0, rows * _LANES - flat.shape[0])).reshape(rows, _LANES)


def _unpack(packed, like):
    flat, out, off = packed.reshape(-1), [], 0
    for a in like:
        out.append(flat[off:off + a.size].reshape(a.shape))
        off += a.size
    return out


def kernel(x, norm_w, w_in, fox_b_f, s5_a_re, s5_a_im, s5_b_re, s5_b_im, s5_c_re, s5_c_im, s5_d, s5_log_dt, s5_w_glu, ret_gn_w, w_out, final_norm_w, loss_target, m_norm_w, m_w_in, m_fox_b_f, m_s5_a_re, m_s5_a_im, m_s5_b_re, m_s5_b_im, m_s5_c_re, m_s5_c_im, m_s5_d, m_s5_log_dt, m_s5_w_glu, m_ret_gn_w, m_w_out, m_final_norm_w, v_norm_w, v_w_in, v_fox_b_f, v_s5_a_re, v_s5_a_im, v_s5_b_re, v_s5_b_im, v_s5_c_re, v_s5_c_im, v_s5_d, v_s5_log_dt, v_s5_w_glu, v_ret_gn_w, v_w_out, v_final_norm_w):
    w = dict(norm_w=norm_w, w_in=w_in, fox_b_f=fox_b_f, s5_a_re=s5_a_re, s5_a_im=s5_a_im, s5_b_re=s5_b_re, s5_b_im=s5_b_im,
             s5_c_re=s5_c_re, s5_c_im=s5_c_im, s5_d=s5_d, s5_log_dt=s5_log_dt, s5_w_glu=s5_w_glu, ret_gn_w=ret_gn_w,
             w_out=w_out, final_norm_w=final_norm_w)
    m = dict(norm_w=m_norm_w, w_in=m_w_in, fox_b_f=m_fox_b_f, s5_a_re=m_s5_a_re, s5_a_im=m_s5_a_im, s5_b_re=m_s5_b_re,
             s5_b_im=m_s5_b_im, s5_c_re=m_s5_c_re, s5_c_im=m_s5_c_im, s5_d=m_s5_d, s5_log_dt=m_s5_log_dt,
             s5_w_glu=m_s5_w_glu, ret_gn_w=m_ret_gn_w, w_out=m_w_out, final_norm_w=m_final_norm_w)
    v = dict(norm_w=v_norm_w, w_in=v_w_in, fox_b_f=v_fox_b_f, s5_a_re=v_s5_a_re, s5_a_im=v_s5_a_im, s5_b_re=v_s5_b_re,
             s5_b_im=v_s5_b_im, s5_c_re=v_s5_c_re, s5_c_im=v_s5_c_im, s5_d=v_s5_d, s5_log_dt=v_s5_log_dt,
             s5_w_glu=v_s5_w_glu, ret_gn_w=v_ret_gn_w, w_out=v_w_out, final_norm_w=v_final_norm_w)

    small = {n: w[n] for n in _SMALL}
    loss, dx, _, _, partials, r_small = _step_grads(x[0], loss_target[0], small, shards=[w[n].astype(_MXU) for n in _SHARDED])

    res = {}
    for t, n in enumerate(_SHARDED):
        res[n] = _adamw(jnp.stack([partials[l][t] for l in range(DEPTH)], axis=1), w[n], m[n], v[n], "adamw_" + n)
    small_w = [w[n] for n in _SMALL]
    outs = _adamw(r_small[:, None], *[_pack([d[n] for n in _SMALL])[None] for d in (w, m, v)], "adamw_small")
    for k, o in enumerate(outs):
        for n, a in zip(_SMALL, _unpack(o, small_w)):
            res.setdefault(n, [None] * 4)[k] = a

    loss = lax.psum(loss[0, 0], ("x", "y", "c"))
    return (loss, dx[None], *[res[n][0] for n in _WEIGHTS], *[res[n][1] for n in _WEIGHTS],
            *[res[n][2] for n in _WEIGHTS], *[res[n][3] for n in _WEIGHTS])
```

```python
import math

import jax
import jax.numpy as jnp
from jax import lax
from jax.experimental import pallas as pl
from jax.experimental.pallas import tpu as pltpu

F32 = jnp.float32
_MXU = jnp.bfloat16
_HI = lax.Precision.HIGHEST

N_DEV = 8
DEPTH = 4
D_MODEL = 1024
HEAD_DIM = 64
D_FOX = 512
FOX_HEADS = 8
D_S5 = 256
S5_GROUPS = 16
S5_GROUP_CH = 16
S5_STATE = 64
S5_CH = S5_GROUPS * S5_STATE
D_RET = 256
RET_HEADS = 4
CHUNK = 64
ROPE_BASE = 10000.0
EPS = 1e-6
D_IN = 3592
D_INP = 3712
W_SHARD = D_IN // N_DEV
O_GATE, O_FQ, O_FK, O_FV, O_SU, O_RQ, O_RK, O_RV, O_FL = 0, 1024, 1536, 2048, 2560, 2816, 3072, 3328, 3584

ADAM_LR, ADAM_B1, ADAM_B2, ADAM_EPS, ADAM_WD, ADAM_STEP = 0.001, 0.9, 0.999, 1e-08, 0.01, 10

TM = 256
TQ = 512
TS = 512
NEG = -1e30
VMEM_BIG = 56 * 1024 * 1024


def _pallas(body, **kw):
    return pl.pallas_call(body, **kw)


def _whole(shape):
    n = len(shape)
    return pl.BlockSpec(shape, lambda *_: (0,) * n)


def _rows(tm, width, col=0):
    return pl.BlockSpec((tm, width), lambda i: (i, col))


def _dot(a, b, dims=(((1,), (0,)), ((), ()))):
    return lax.dot_general(a.astype(_MXU), b.astype(_MXU), dims, preferred_element_type=F32)


_NT = (((1,), (1,)), ((), ()))
_TN = (((0,), (0,)), ((), ()))


def _norm_inproj(x, g, w):
    L = x.shape[0]

    def body(x_ref, g_ref, w_ref, p_ref, h_ref):
        xv = x_ref[...]
        r = lax.rsqrt(jnp.mean(xv * xv, axis=-1, keepdims=True) + EPS)
        h = (xv * r * g_ref[...]).astype(_MXU)
        h_ref[...] = h
        p_ref[...] = _dot(h, w_ref[...])

    return _pallas(body, out_shape=(jax.ShapeDtypeStruct((L, D_INP), F32), jax.ShapeDtypeStruct((L, D_MODEL), _MXU)),
                   grid=(L // TM,), in_specs=[_rows(TM, D_MODEL), _whole((1, D_MODEL)), _whole((D_MODEL, D_INP))],
                   out_specs=(_rows(TM, D_INP), _rows(TM, D_MODEL)), name="norm_inproj",
                   compiler_params=pltpu.CompilerParams(vmem_limit_bytes=VMEM_BIG))(x, g, w)


def _rms_bwd(xv, g, dh):
    r = lax.rsqrt(jnp.mean(xv * xv, axis=-1, keepdims=True) + EPS)
    xh = xv * r
    dg = jnp.sum(dh * xh, axis=0, keepdims=True)
    dxh = dh * g
    dx = r * (dxh - xh * jnp.mean(dxh * xh, axis=-1, keepdims=True))
    return dx, dg


def _inproj_bwd(pieces, w, x, g, dres, h):
    L = x.shape[0]
    n = len(pieces)

    def body(*refs):
        w_ref, x_ref, g_ref, dr_ref, h_ref, dx_ref, dg_ref, dw_ref = refs[n:]
        dproj = jnp.concatenate([r[...].astype(_MXU) for r in refs[:n]], axis=-1)
        dh = _dot(dproj, w_ref[...], _NT)
        dx, dg = _rms_bwd(x_ref[...], g_ref[...], dh)
        dx_ref[...] = dx + dr_ref[...]
        dw = _dot(h_ref[...], dproj, _TN)

        @pl.when(pl.program_id(0) == 0)
        def _():
            dg_ref[...] = dg
            dw_ref[...] = dw

        @pl.when(pl.program_id(0) != 0)
        def _():
            dg_ref[...] += dg
            dw_ref[...] += dw

    resident = pl.BlockSpec((D_MODEL, D_INP), lambda i: (0, 0), pipeline_mode=pl.Buffered(1))
    return _pallas(body, out_shape=(jax.ShapeDtypeStruct((L, D_MODEL), F32), jax.ShapeDtypeStruct((1, D_MODEL), F32),
                                    jax.ShapeDtypeStruct((D_MODEL, D_INP), F32)),
                   grid=(L // TM,),
                   in_specs=[_rows(TM, p.shape[1]) for p in pieces]
                   + [resident, _rows(TM, D_MODEL), _whole((1, D_MODEL)), _rows(TM, D_MODEL), _rows(TM, D_MODEL)],
                   out_specs=(_rows(TM, D_MODEL), _whole((1, D_MODEL)), resident), name="inproj_bwd",
                   compiler_params=pltpu.CompilerParams(vmem_limit_bytes=VMEM_BIG))(*pieces, w, x, g, dres, h)


PAIR = 2 * HEAD_DIM
N_AUX = 3


def _own(shape, h):
    return lax.broadcasted_iota(jnp.int32, shape, len(shape) - 1) // HEAD_DIM == h


def _hi_dot(a, b):
    return jnp.dot(a, b, precision=_HI, preferred_element_type=F32)


def _tri(n, lower):
    r = lax.broadcasted_iota(jnp.int32, (n, n), 0)
    c = lax.broadcasted_iota(jnp.int32, (n, n), 1)
    return jnp.where(r >= c if lower else r <= c, 1.0, 0.0).astype(F32)


def _fox_cumsum(proj, b):
    L = proj.shape[0]

    def body(fl_ref, b_ref, c_ref, carry_sc):
        @pl.when(pl.program_id(0) == 0)
        def _():
            carry_sc[...] = jnp.zeros((1, PAIR), F32)

        lane = lax.broadcasted_iota(jnp.int32, (TM, PAIR), 1)
        lf = jnp.where(lane < FOX_HEADS, jax.nn.log_sigmoid(fl_ref[...] + b_ref[...]), 0.0)
        cs = _hi_dot(_tri(TM, True), lf) + carry_sc[...]
        c_ref[...] = cs
        carry_sc[...] = cs[TM - 1:TM, :]

    return _pallas(body, out_shape=jax.ShapeDtypeStruct((L, PAIR), F32), grid=(L // TM,),
                   in_specs=[_rows(TM, PAIR, O_FL // PAIR), _whole((1, PAIR))], out_specs=_rows(TM, PAIR),
                   scratch_shapes=[pltpu.VMEM((1, PAIR), F32)], name="fox_cumsum")(proj, b)


def _fox_prep(proj, c):
    L = proj.shape[0]

    def body(q_ref, k_ref, v_ref, c_ref, qa_ref, ka_ref, kat_ref, vt_ref):
        lane = lax.broadcasted_iota(jnp.int32, (TM, PAIR), 1)
        cv = c_ref[...]
        for p in range(FOX_HEADS // 2):
            cols = slice(PAIR * p, PAIR * (p + 1))
            q2, k2 = q_ref[:, cols], k_ref[:, cols]
            vt_ref[p] = v_ref[:, cols].T.astype(_MXU)
            for e in range(2):
                h = 2 * p + e
                own = lane // HEAD_DIM == e
                a = lane - (HEAD_DIM if e == 0 else 0)
                pick = (lax.broadcasted_iota(jnp.int32, (PAIR, PAIR), 0) == h).astype(F32)
                rest = _hi_dot(cv, pick)
                aux_q = jnp.where((a >= N_AUX) & (a < 2 * N_AUX), 1.0, 0.0)
                aux_k = jnp.where((a >= 0) & (a < N_AUX), 1.0, 0.0)
                for n in range(N_AUX):
                    part = rest.astype(_MXU).astype(F32)
                    rest = rest - part
                    aux_q = jnp.where(a == n, part, aux_q)
                    aux_k = jnp.where(a == N_AUX + n, -part, aux_k)
                ka = jnp.where(own, k2, aux_k)
                qa_ref[h] = jnp.where(own, q2 * (1.0 / math.sqrt(HEAD_DIM)), aux_q).astype(_MXU)
                ka_ref[h] = ka.astype(_MXU)
                kat_ref[h] = ka.T.astype(_MXU)

    hl = jax.ShapeDtypeStruct((FOX_HEADS, L, PAIR), _MXU)
    nat = lambda o: _rows(TM, D_FOX, o // D_FOX)
    rows = pl.BlockSpec((FOX_HEADS, TM, PAIR), lambda i: (0, i, 0))
    return _pallas(
        body, out_shape=(hl, hl, jax.ShapeDtypeStruct((FOX_HEADS, PAIR, L), _MXU),
                         jax.ShapeDtypeStruct((FOX_HEADS // 2, PAIR, L), _MXU)),
        grid=(L // TM,), in_specs=[nat(O_FQ), nat(O_FK), nat(O_FV), _rows(TM, PAIR)],
        out_specs=(rows, rows, pl.BlockSpec((FOX_HEADS, PAIR, TM), lambda i: (0, 0, i)),
                   pl.BlockSpec((FOX_HEADS // 2, PAIR, TM), lambda i: (0, 0, i))),
        name="fox_prep")(proj, proj, proj, c)


def _key_le_query(tq):
    return lax.broadcasted_iota(jnp.int32, (tq, tq), 0) <= lax.broadcasted_iota(jnp.int32, (tq, tq), 1)


def _grid_ends(n0, n1):
    first = lambda: (pl.program_id(0) == 0) & (pl.program_id(1) == 0)
    last = lambda: (pl.program_id(0) == n0 - 1) & (pl.program_id(1) == n1 - 1)
    return first, last


def _fox_fwd(qa, ka, vt, ride=None):
    H, L, _ = qa.shape
    tq = min(TQ, L)
    nq = L // tq

    def body(qa_ref, ka_ref, vt_ref, o_ref, lse_ref, m_sc, l_sc, acc_sc):
        i = pl.program_id(1)
        m_sc[...] = jnp.full((2, 1, tq), NEG, F32)
        l_sc[...] = jnp.zeros((2, 1, tq), F32)
        acc_sc[...] = jnp.zeros((2, HEAD_DIM, tq), F32)

        def block(j, nk, masked):
            keys = pl.ds(pl.multiple_of(j * tq, tq), nk * tq)
            vt_blk = vt_ref[:, keys]
            sts = [_dot(ka_ref[e, keys, :], qa_ref[e], _NT) for e in range(2)]
            pts, alphas = [], []
            for e in range(2):
                st = jnp.where(_key_le_query(tq), sts[e], NEG) if masked else sts[e]
                m_prev = m_sc[e]
                m_new = jnp.maximum(m_prev, jnp.max(st, axis=0, keepdims=True))
                alphas.append(jnp.exp(m_prev - m_new))
                pt = jnp.exp(st - m_new)
                l_sc[e] = alphas[e] * l_sc[e] + jnp.sum(pt, axis=0, keepdims=True)
                m_sc[e] = m_new
                pts.append(pt.astype(_MXU))
            for e in range(2):
                acc_sc[e] = alphas[e] * acc_sc[e] + _dot(vt_blk[HEAD_DIM * e:HEAD_DIM * (e + 1)], pts[e])

        def two_blocks(jj, carry):
            block(2 * jj, 2, False)
            return carry

        lax.fori_loop(0, i // 2, two_blocks, 0)

        @pl.when(i % 2 == 1)
        def _():
            block(i - 1, 1, False)

        block(i, 1, True)
        o_ref[...] = jnp.concatenate([acc_sc[0] / l_sc[0], acc_sc[1] / l_sc[1]], axis=0).T
        for e in range(2):
            lse_ref[e] = m_sc[e] + jnp.log(l_sc[e])

    body, ex_in, ex_specs, ex_out, ex_sems = _riding(body, 3, 2, ride, *_grid_ends(H // 2, nq))
    res = _pallas(
        body, out_shape=(jax.ShapeDtypeStruct((L, D_FOX), F32), jax.ShapeDtypeStruct((H, 1, L), F32), *ex_out),
        grid=(H // 2, nq),
        in_specs=[pl.BlockSpec((2, tq, PAIR), lambda p, i: (p, i, 0)), pl.BlockSpec((2, L, PAIR), lambda p, i: (p, 0, 0)),
                  pl.BlockSpec((None, PAIR, L), lambda p, i: (p, 0, 0)), *ex_specs],
        out_specs=(pl.BlockSpec((tq, PAIR), lambda p, i: (i, p)), pl.BlockSpec((2, 1, tq), lambda p, i: (p, 0, i)),
                   *ex_specs),
        scratch_shapes=[pltpu.VMEM((2, 1, tq), F32), pltpu.VMEM((2, 1, tq), F32), pltpu.VMEM((2, HEAD_DIM, tq), F32), *ex_sems],
        name="fox_fwd" if ride is None else "fox_fwd_gather")(qa, ka, vt, *ex_in)
    return res[0], res[1], list(res[2:])


def _fox_bwd(qa, ka, kat, proj, do, o, lse, ride=None):
    H, L, _ = qa.shape
    tq = min(TQ, L)
    nq = L // tq

    def body(qa_ref, ka_ref, kat_ref, v_ref, do_ref, o_ref, lse_ref, dqt_ref, dk_ref, dv_ref, delta_sc, dk_sc, dv_sc):
        j = pl.program_id(1)

        @pl.when(j == 0)
        def _():
            head_rows = (lax.broadcasted_iota(jnp.int32, (8, PAIR), 1) // HEAD_DIM
                         == lax.broadcasted_iota(jnp.int32, (8, PAIR), 0)).astype(F32)
            delta_sc[...] = lax.dot_general(head_rows, do_ref[...] * o_ref[...], _NT, precision=_HI,
                                            preferred_element_type=F32)
            dqt_ref[...] = jnp.zeros((2, PAIR, L), F32)

        dk_sc[...] = jnp.zeros((2, tq, PAIR), F32)
        dv_sc[...] = jnp.zeros((tq, PAIR), F32)
        vb = v_ref[...]

        def block(i, masked):
            qs = pl.ds(pl.multiple_of(i * tq, tq), tq)
            dob = do_ref[qs, :]
            for e in range(2):
                own = _own((tq, PAIR), e)
                qh = qa_ref[e, qs, :]
                pt = jnp.exp(_dot(ka_ref[e], qh, _NT) - lse_ref[e, :, qs])
                if masked:
                    pt = jnp.where(_key_le_query(tq), pt, 0.0)
                dv_sc[...] += _dot(pt, jnp.where(own, dob, 0.0))
                dpt = _dot(jnp.where(own, vb, 0.0), dob, _NT)
                ds = (pt * (dpt - delta_sc[e:e + 1, qs])).astype(_MXU)
                dk_sc[e] += _dot(ds, qh)
                dqt_ref[e, :, qs] += _dot(kat_ref[e], ds)

        def off_diagonal(i, carry):
            block(i, False)
            return carry

        block(j, True)
        lax.fori_loop(j + 1, nq, off_diagonal, 0)
        dk_ref[...] = dk_sc[...]
        dv_ref[...] = dv_sc[...]

    nat = pl.BlockSpec((L, PAIR), lambda p, j: (0, p))
    body, ex_in, ex_specs, ex_out, ex_sems = _riding(body, 7, 3, ride, *_grid_ends(H // 2, nq))
    res = _pallas(
        body, out_shape=(jax.ShapeDtypeStruct((H, PAIR, L), F32), jax.ShapeDtypeStruct((H, L, PAIR), F32),
                         jax.ShapeDtypeStruct((L, D_FOX), F32), *ex_out),
        grid=(H // 2, nq),
        in_specs=[pl.BlockSpec((2, L, PAIR), lambda p, j: (p, 0, 0)), pl.BlockSpec((2, tq, PAIR), lambda p, j: (p, j, 0)),
                  pl.BlockSpec((2, PAIR, tq), lambda p, j: (p, 0, j)),
                  pl.BlockSpec((tq, PAIR), lambda p, j: (j, O_FV // PAIR + p)), nat, nat,
                  pl.BlockSpec((2, 1, L), lambda p, j: (p, 0, 0)), *ex_specs],
        out_specs=(pl.BlockSpec((2, PAIR, L), lambda p, j: (p, 0, 0)), pl.BlockSpec((2, tq, PAIR), lambda p, j: (p, j, 0)),
                   pl.BlockSpec((tq, PAIR), lambda p, j: (j, p)), *ex_specs),
        scratch_shapes=[pltpu.VMEM((8, L), F32), pltpu.VMEM((2, tq, PAIR), F32), pltpu.VMEM((tq, PAIR), F32), *ex_sems],
        name="fox_bwd" if ride is None else "fox_bwd_exchange",
        compiler_params=pltpu.CompilerParams(vmem_limit_bytes=VMEM_BIG))(qa, ka, kat, proj, do, o, lse, *ex_in)
    return res[0], res[1], res[2], list(res[3:])


def _fox_post_bwd(dqt, dkraw, proj, b):
    L = proj.shape[0]
    nb = L // TM

    def body(dqt_ref, dkr_ref, fl_ref, b_ref, dq_ref, dk_ref, dfl_ref, db_ref, carry_sc):
        first = pl.program_id(0) == 0

        @pl.when(first)
        def _():
            carry_sc[...] = jnp.zeros((1, PAIR), F32)

        lane = lax.broadcasted_iota(jnp.int32, (TM, PAIR), 1)
        rr = lax.broadcasted_iota(jnp.int32, (PAIR, PAIR), 0)
        cc = lax.broadcasted_iota(jnp.int32, (PAIR, PAIR), 1)
        dc = jnp.zeros((TM, PAIR), F32)
        for p in range(FOX_HEADS // 2):
            cols = slice(PAIR * p, PAIR * (p + 1))
            dqs = [dqt_ref[2 * p + e].T for e in range(2)]
            dks = [dkr_ref[2 * p + e] for e in range(2)]
            dq_ref[:, cols] = jnp.where(lane < HEAD_DIM, dqs[0], dqs[1]) * (1.0 / math.sqrt(HEAD_DIM))
            dk_ref[:, cols] = jnp.where(lane < HEAD_DIM, dks[0], dks[1])
            sums = jnp.zeros((TM, PAIR), F32)
            place = jnp.zeros((PAIR, PAIR), F32)
            for e in range(2):
                base = HEAD_DIM if e == 0 else 0
                sums = jnp.where(lane == base, dqs[e], jnp.where(lane == base + N_AUX, -dks[e], sums))
                place = jnp.where(((rr == base) | (rr == base + N_AUX)) & (cc == 2 * p + e), 1.0, place)
            dc = dc + _hi_dot(sums, place)
        rs = _hi_dot(_tri(TM, False), dc) + carry_sc[...]
        carry_sc[...] = rs[0:1, :]
        dfl = jnp.where(lane < FOX_HEADS, rs * jax.nn.sigmoid(-(fl_ref[...] + b_ref[...])), 0.0)
        dfl_ref[...] = dfl
        db = jnp.sum(dfl, axis=0, keepdims=True)

        @pl.when(first)
        def _():
            db_ref[...] = db

        @pl.when(jnp.logical_not(first))
        def _():
            db_ref[...] += db

    rev = lambda i: nb - 1 - i
    nat = pl.BlockSpec((TM, D_FOX), lambda i: (rev(i), 0))
    return _pallas(
        body, out_shape=(jax.ShapeDtypeStruct((L, D_FOX), F32),) * 2
        + (jax.ShapeDtypeStruct((L, PAIR), F32), jax.ShapeDtypeStruct((1, PAIR), F32)),
        grid=(nb,),
        in_specs=[pl.BlockSpec((FOX_HEADS, PAIR, TM), lambda i: (0, 0, rev(i))),
                  pl.BlockSpec((FOX_HEADS, TM, PAIR), lambda i: (0, rev(i), 0)),
                  pl.BlockSpec((TM, PAIR), lambda i: (rev(i), O_FL // PAIR)), _whole((1, PAIR))],
        out_specs=(nat, nat, pl.BlockSpec((TM, PAIR), lambda i: (rev(i), 0)), _whole((1, PAIR))),
        scratch_shapes=[pltpu.VMEM((1, PAIR), F32)], name="fox_post_bwd")(dqt, dkraw, proj, b)


def _s5_expand():
    r = lax.broadcasted_iota(jnp.int32, (S5_STATE, S5_STATE * S5_GROUP_CH), 0)
    c = lax.broadcasted_iota(jnp.int32, (S5_STATE, S5_STATE * S5_GROUP_CH), 1)
    return jnp.where(c // S5_GROUP_CH == r, 1.0, 0.0).astype(F32)


def _s5_disc_math(ar, ai, ldt, br, bi):
    dt = jnp.exp(ldt)
    mag = jnp.exp(ar * dt)
    lr = mag * jnp.cos(ai * dt)
    li = mag * jnp.sin(ai * dt)
    den = ar * ar + ai * ai
    fr = ((lr - 1.0) * ar + li * ai) / den
    fi = (li * ar - (lr - 1.0) * ai) / den
    e = _s5_expand()
    fre = jnp.dot(fr, e, precision=_HI, preferred_element_type=F32)
    fie = jnp.dot(fi, e, precision=_HI, preferred_element_type=F32)
    return lr, li, fre * br - fie * bi, fre * bi + fie * br


def _layer_blocks(arrs):
    return [pl.BlockSpec((None,) + a.shape[1:], lambda l: (l, 0, 0)) for a in arrs]


def _s5_disc(ar, ai, ldt, br, bi):
    def body(ar_ref, ai_ref, ldt_ref, br_ref, bi_ref, lr_ref, li_ref, bbr_ref, bbi_ref):
        lr, li, bbr, bbi = _s5_disc_math(ar_ref[...], ai_ref[...], ldt_ref[...], br_ref[...], bi_ref[...])
        lr_ref[...] = lr
        li_ref[...] = li
        bbr_ref[...] = bbr
        bbi_ref[...] = bbi

    ins = (ar, ai, ldt, br, bi)
    outs = (ar, ai, br, bi)
    return _pallas(body, out_shape=tuple(jax.ShapeDtypeStruct(a.shape, F32) for a in outs), grid=(DEPTH,),
                   in_specs=_layer_blocks(ins), out_specs=tuple(_layer_blocks(outs)), name="s5_disc")(*ins)


def _s5_disc_bwd(ar, ai, ldt, br, bi, dlr, dli, dbbr, dbbi):
    def body(ar_ref, ai_ref, ldt_ref, br_ref, bi_ref, dlr_ref, dli_ref, dbbr_ref, dbbi_ref,
             dar_ref, dai_ref, dldt_ref, dbr_ref, dbi_ref):
        _, vjp = jax.vjp(_s5_disc_math, ar_ref[...], ai_ref[...], ldt_ref[...], br_ref[...], bi_ref[...])
        dar, dai, dldt, dbr, dbi = vjp((dlr_ref[...], dli_ref[...], dbbr_ref[...], dbbi_ref[...]))
        dar_ref[...] = dar
        dai_ref[...] = dai
        dldt_ref[...] = dldt
        dbr_ref[...] = dbr
        dbi_ref[...] = dbi

    ins = (ar, ai, ldt, br, bi, dlr, dli, dbbr, dbbi)
    outs = (ar, ai, ldt, br, bi)
    return _pallas(body, out_shape=tuple(jax.ShapeDtypeStruct(a.shape, F32) for a in outs), grid=(DEPTH,),
                   in_specs=_layer_blocks(ins), out_specs=tuple(_layer_blocks(outs)), name="s5_disc_bwd")(*ins)


SLAB = 2 * S5_CH // 128


def _slab_rows(s, ts):
    return pl.ds(s, ts, stride=SLAB)


def _slab_pair(ref, s, ts):
    return jnp.concatenate([ref[_slab_rows(s, ts), :].astype(_MXU), ref[_slab_rows(s + 1, ts), :].astype(_MXU)], axis=-1)


def _s5_fwd(proj, wb, wc, lam, d, w_glu):
    L = proj.shape[0]
    ts = min(TS, L)

    def body(u_ref, wb_ref, wc_ref, lam_ref, d_ref, wg_ref, xs_ref, ypre_ref, ys_ref, b_sc, c_sc):
        @pl.when(pl.program_id(0) == 0)
        def _():
            c_sc[...] = jnp.zeros((SLAB, 128), F32)

        u = u_ref[...]
        ub = u.astype(_MXU)
        for s in range(0, SLAB, 2):
            b2 = _dot(ub, wb_ref[:, 128 * s:128 * (s + 2)])
            b_sc[_slab_rows(s, ts), :] = b2[:, :128]
            b_sc[_slab_rows(s + 1, ts), :] = b2[:, 128:]
        lr, li = lam_ref[0:8, :], lam_ref[8:16, :]

        def step(t, carry):
            xr, xi = carry
            row = pl.multiple_of(t * SLAB, SLAB)
            nr = lr * xr - li * xi + b_sc[pl.ds(row, 8), :]
            ni = lr * xi + li * xr + b_sc[pl.ds(row + 8, 8), :]
            xs_ref[pl.ds(row, 8), :] = nr
            xs_ref[pl.ds(row + 8, 8), :] = ni
            return nr, ni

        xr, xi = lax.fori_loop(0, ts, step, (c_sc[0:8, :], c_sc[8:16, :]), unroll=8)
        c_sc[0:8, :] = xr
        c_sc[8:16, :] = xi
        y = jnp.zeros((ts, D_S5), F32)
        for s in range(0, SLAB, 2):
            y = y + _dot(_slab_pair(xs_ref, s, ts), wc_ref[128 * s:128 * (s + 2), :])
        ypre_ref[...] = y
        y1 = jax.nn.gelu(y + d_ref[...] * u)
        ys_ref[...] = y1 * jax.nn.sigmoid(_dot(y1, wg_ref[...]))

    row = _rows(ts, D_S5)
    slabs = pl.BlockSpec((ts * SLAB, 128), lambda n: (n, 0))
    return _pallas(
        body, out_shape=(jax.ShapeDtypeStruct((L * SLAB, 128), F32), jax.ShapeDtypeStruct((L, D_S5), F32),
                         jax.ShapeDtypeStruct((L, D_S5), F32)),
        grid=(L // ts,),
        in_specs=[_rows(ts, D_S5, O_SU // D_S5), _whole((D_S5, 2 * S5_CH)), _whole((2 * S5_CH, D_S5)), _whole((SLAB, 128)),
                  _whole((1, D_S5)), _whole((D_S5, D_S5))],
        out_specs=(slabs, row, row),
        scratch_shapes=[pltpu.VMEM((ts * SLAB, 128), F32), pltpu.VMEM((SLAB, 128), F32)], name="s5_fwd")(
            proj, wb, wc, lam, d, w_glu)


def _s5_bwd(proj, ypre, dys, xs, wb, wc, lam, d, w_glu):
    L = proj.shape[0]
    ts = min(TS, L)
    nb = L // ts

    def body(u_ref, y_ref, dys_ref, xs_ref, xp_ref, wb_ref, wc_ref, lam_ref, d_ref, wg_ref,
             du_ref, dwb_ref, dwc_ref, dlam_ref, dd_ref, dwg_ref, dx_sc, g_sc, c_sc):
        n = pl.program_id(0)

        @pl.when(n == 0)
        def _():
            c_sc[...] = jnp.zeros((SLAB, 128), F32)
            dlam_ref[...] = jnp.zeros((SLAB, 128), F32)
            dwb_ref[...] = jnp.zeros((D_S5, 2 * S5_CH), F32)
            dwc_ref[...] = jnp.zeros((2 * S5_CH, D_S5), F32)
            dd_ref[...] = jnp.zeros((1, D_S5), F32)
            dwg_ref[...] = jnp.zeros((D_S5, D_S5), F32)

        u, dv, dout = u_ref[...], d_ref[...], dys_ref[...]
        y1, gelu_vjp = jax.vjp(jax.nn.gelu, y_ref[...] + dv * u)
        sg = jax.nn.sigmoid(_dot(y1, wg_ref[...]))
        dz = dout * y1 * sg * (1.0 - sg)
        dy, = gelu_vjp(dout * sg + _dot(dz, wg_ref[...], _NT))
        dd_ref[...] += jnp.sum(dy * u, axis=0, keepdims=True)
        dwg_ref[...] += _dot(y1, dz, _TN)
        dyb = dy.astype(_MXU)
        for s in range(0, SLAB, 2):
            cols = slice(128 * s, 128 * (s + 2))
            dx2 = _dot(dyb, wc_ref[cols, :], _NT)
            dx_sc[_slab_rows(s, ts), :] = dx2[:, :128]
            dx_sc[_slab_rows(s + 1, ts), :] = dx2[:, 128:]
            dwc_ref[cols, :] += _dot(_slab_pair(xs_ref, s, ts), dyb, _TN)
        lr, li = lam_ref[0:8, :], lam_ref[8:16, :]

        def adjoint(row, pr, pi, carry):
            gr, gi, ar, ai = carry
            nr = dx_sc[pl.ds(row, 8), :] + lr * gr + li * gi
            ni = dx_sc[pl.ds(row + 8, 8), :] - li * gr + lr * gi
            g_sc[pl.ds(row, 8), :] = nr
            g_sc[pl.ds(row + 8, 8), :] = ni
            return nr, ni, ar + nr * pr + ni * pi, ai - nr * pi + ni * pr

        def step(k, carry):
            row = pl.multiple_of((ts - 1 - k) * SLAB, SLAB)
            prev = pl.multiple_of((ts - 2 - k) * SLAB, SLAB)
            return adjoint(row, xs_ref[pl.ds(prev, 8), :], xs_ref[pl.ds(prev + 8, 8), :], carry)

        z = jnp.zeros((8, 128), F32)
        carry = lax.fori_loop(0, ts - 1, step, (c_sc[0:8, :], c_sc[8:16, :], z, z), unroll=8)
        has_prev = jnp.where(n == nb - 1, 0.0, 1.0)
        gr, gi, ar, ai = adjoint(0, xp_ref[0:8, :] * has_prev, xp_ref[8:16, :] * has_prev, carry)
        c_sc[0:8, :] = gr
        c_sc[8:16, :] = gi
        dlam_ref[0:8, :] += ar
        dlam_ref[8:16, :] += ai
        ub = u.astype(_MXU)
        du = dy * dv
        for s in range(0, SLAB, 2):
            cols = slice(128 * s, 128 * (s + 2))
            gs = _slab_pair(g_sc, s, ts)
            du = du + _dot(gs, wb_ref[:, cols], _NT)
            dwb_ref[:, cols] += _dot(ub, gs, _TN)
        du_ref[...] = du

    blk = lambda n: nb - 1 - n
    row = pl.BlockSpec((ts, D_S5), lambda n: (blk(n), 0))
    return _pallas(
        body, out_shape=(jax.ShapeDtypeStruct((L, D_S5), F32), jax.ShapeDtypeStruct((D_S5, 2 * S5_CH), F32),
                         jax.ShapeDtypeStruct((2 * S5_CH, D_S5), F32), jax.ShapeDtypeStruct((SLAB, 128), F32),
                         jax.ShapeDtypeStruct((1, D_S5), F32), jax.ShapeDtypeStruct((D_S5, D_S5), F32)),
        grid=(nb,),
        in_specs=[pl.BlockSpec((ts, D_S5), lambda n: (blk(n), O_SU // D_S5)), row, row,
                  pl.BlockSpec((ts * SLAB, 128), lambda n: (blk(n), 0)),
                  pl.BlockSpec((SLAB, 128), lambda n: (jnp.maximum(blk(n) * ts - 1, 0), 0)),
                  _whole((D_S5, 2 * S5_CH)), _whole((2 * S5_CH, D_S5)), _whole((SLAB, 128)), _whole((1, D_S5)),
                  _whole((D_S5, D_S5))],
        out_specs=(row, _whole((D_S5, 2 * S5_CH)), _whole((2 * S5_CH, D_S5)), _whole((SLAB, 128)), _whole((1, D_S5)),
                   _whole((D_S5, D_S5))),
        scratch_shapes=[pltpu.VMEM((ts * SLAB, 128), F32), pltpu.VMEM((ts * SLAB, 128), F32), pltpu.VMEM((SLAB, 128), F32)],
        name="s5_bwd")(proj, ypre, dys, xs, xs, wb, wc, lam, d, w_glu)


def _rot(z, cos, sin):
    lane = lax.broadcasted_iota(jnp.int32, z.shape, 1)
    zs = z * sin
    half = HEAD_DIM // 2
    return z * cos + jnp.where(lane % HEAD_DIM < half, pltpu.roll(zs, PAIR - half, 1), pltpu.roll(zs, half, 1))


def _head_avg():
    r = lax.broadcasted_iota(jnp.int32, (PAIR, PAIR), 0) // HEAD_DIM
    c = lax.broadcasted_iota(jnp.int32, (PAIR, PAIR), 1) // HEAD_DIM
    return jnp.where(r == c, 1.0 / HEAD_DIM, 0.0).astype(F32)


def _ret_tables(tq):
    lg = jnp.log1p(-(2.0 ** (-5.0 - jnp.arange(RET_HEADS, dtype=F32))))
    scale = 1.0 / math.sqrt(HEAD_DIM)
    pos = jnp.arange(tq)
    n = pos.astype(F32)
    dist = jnp.abs(n[:, None] - n[None, :])
    ok = (pos[None, :] // CHUNK) <= (pos[:, None] // CHUNK)
    w = jnp.where(ok[None], scale * jnp.exp(lg[:, None, None] * dist[None]), 0.0)
    lgl = jnp.repeat(lg, HEAD_DIM)
    dq_tab = scale * jnp.exp(lgl[None, :] * (n[:, None] + 1.0))
    dk_tab = jnp.exp(lgl[None, :] * (tq - 1.0 - n[:, None]))
    blk = jnp.arange(PAIR) // HEAD_DIM
    bd = (blk[:, None] == blk[None, :]).astype(F32)
    gbd = bd[None] * jnp.exp(lgl.reshape(RET_HEADS // 2, PAIR)[:, :, None] * tq)
    return dict(w=w, wt=w.transpose(0, 2, 1), dq=dq_tab, dk=dk_tab, gbd=gbd, bd=bd)


def _ret_specs(tq, nq, rev):
    blk = (lambda i: nq - 1 - i) if rev else (lambda i: i)
    col = lambda o: pl.BlockSpec((tq, PAIR), lambda p, i: (blk(i), o // PAIR + p))
    return dict(
        rq=col(O_RQ), rk=col(O_RK), rv=col(O_RV), nat=col(0),
        w=pl.BlockSpec((2, tq, tq), lambda p, i: (p, 0, 0)), tab=pl.BlockSpec((tq, PAIR), lambda p, i: (0, p)),
        gbd=pl.BlockSpec((None, PAIR, PAIR), lambda p, i: (p, 0, 0)), bd=pl.BlockSpec((PAIR, PAIR), lambda p, i: (0, 0)),
        gn=pl.BlockSpec((1, PAIR), lambda p, i: (0, p)),
        st=pl.BlockSpec((None, None, PAIR, PAIR), lambda p, i: (p, blk(i), 0, 0)))


def _ret_fwd(proj, cos_t, sin_t, tabs, gn):
    L = proj.shape[0]
    tq = tabs["w"].shape[1]
    nq = L // tq

    def body(rq_ref, rk_ref, rv_ref, cos_ref, sin_ref, w_ref, dqt_ref, dkt_ref, gbd_ref, bd_ref, gn_ref,
             o_ref, y_ref, st_ref, s_sc):
        @pl.when(pl.program_id(1) == 0)
        def _():
            s_sc[...] = jnp.zeros((PAIR, PAIR), F32)

        state = s_sc[...]
        st_ref[...] = state
        cos, sin = cos_ref[...], sin_ref[...]
        q2, k2, v2 = _rot(rq_ref[...], cos, sin), _rot(rk_ref[...], cos, sin), rv_ref[...]
        o = _dot(q2 * dqt_ref[...], state)
        for h in range(2):
            own = _own((tq, PAIR), h)
            a = _dot(jnp.where(own, q2, 0.0), k2, _NT) * w_ref[h]
            o = o + _dot(a, jnp.where(own, v2, 0.0))
        s_sc[...] = gbd_ref[...] * state + bd_ref[...] * _dot(k2 * dkt_ref[...], v2, _TN)
        o_ref[...] = o
        avg = _head_avg()
        oc = o - _hi_dot(o, avg)
        y_ref[...] = oc * lax.rsqrt(_hi_dot(oc * oc, avg) + EPS) * gn_ref[...]

    sp = _ret_specs(tq, nq, False)
    nat = jax.ShapeDtypeStruct((L, D_RET), F32)
    return _pallas(
        body, out_shape=(nat, nat, jax.ShapeDtypeStruct((RET_HEADS // 2, nq, PAIR, PAIR), F32)), grid=(RET_HEADS // 2, nq),
        in_specs=[sp["rq"], sp["rk"], sp["rv"], sp["nat"], sp["nat"], sp["w"], sp["tab"], sp["tab"], sp["gbd"], sp["bd"],
                  sp["gn"]],
        out_specs=(sp["nat"], sp["nat"], sp["st"]), scratch_shapes=[pltpu.VMEM((PAIR, PAIR), F32)],
        name="ret_fwd")(proj, proj, proj, cos_t, sin_t, tabs["w"], tabs["dq"], tabs["dk"], tabs["gbd"], tabs["bd"], gn)


def _ret_bwd(proj, cos_t, sin_t, tabs, gn, o_pre, dy, states):
    L = proj.shape[0]
    tq = tabs["w"].shape[1]
    nq = L // tq

    def body(rq_ref, rk_ref, rv_ref, cos_ref, sin_ref, w_ref, wt_ref, dqt_ref, dkt_ref, gbd_ref, bd_ref, gn_ref,
             o_ref, dy_ref, st_ref, drq_ref, drk_ref, drv_ref, dgn_ref, g_sc):
        first = pl.program_id(1) == 0

        @pl.when(first)
        def _():
            g_sc[...] = jnp.zeros((PAIR, PAIR), F32)

        cos, sin = cos_ref[...], sin_ref[...]
        q2, k2, v2 = _rot(rq_ref[...], cos, sin), _rot(rk_ref[...], cos, sin), rv_ref[...]
        avg = _head_avg()
        ov, dyv = o_ref[...], dy_ref[...]
        oc = ov - _hi_dot(ov, avg)
        r = lax.rsqrt(_hi_dot(oc * oc, avg) + EPS)
        oh = oc * r
        dgn = jnp.sum(dyv * oh, axis=0, keepdims=True)
        doh = dyv * gn_ref[...]
        do = r * (doh - _hi_dot(doh, avg) - oh * _hi_dot(doh * oh, avg))
        state, g = st_ref[...], g_sc[...]
        dqt, dkt = dqt_ref[...], dkt_ref[...]
        dq = _dot(do, state, _NT) * dqt
        dk = _dot(v2, g, _NT) * dkt
        dv = _dot(k2 * dkt, g)
        g_sc[...] = gbd_ref[...] * g + bd_ref[...] * _dot(q2 * dqt, do, _TN)
        for h in range(2):
            own = _own((tq, PAIR), h)
            qm, dom = jnp.where(own, q2, 0.0), jnp.where(own, do, 0.0)
            dv = dv + _dot(_dot(k2, qm, _NT) * wt_ref[h], dom)
            dq = dq + _dot(_dot(dom, v2, _NT) * w_ref[h], jnp.where(own, k2, 0.0))
            dk = dk + _dot(_dot(v2, dom, _NT) * wt_ref[h], qm)
        drq_ref[...] = _rot(dq, cos, -sin)
        drk_ref[...] = _rot(dk, cos, -sin)
        drv_ref[...] = dv

        @pl.when(first)
        def _():
            dgn_ref[...] = dgn

        @pl.when(jnp.logical_not(first))
        def _():
            dgn_ref[...] += dgn

    sp = _ret_specs(tq, nq, True)
    nat = jax.ShapeDtypeStruct((L, D_RET), F32)
    return _pallas(
        body, out_shape=(nat, nat, nat, jax.ShapeDtypeStruct((1, D_RET), F32)), grid=(RET_HEADS // 2, nq),
        in_specs=[sp["rq"], sp["rk"], sp["rv"], sp["nat"], sp["nat"], sp["w"], sp["w"], sp["tab"], sp["tab"], sp["gbd"],
                  sp["bd"], sp["gn"], sp["nat"], sp["nat"], sp["st"]],
        out_specs=(sp["nat"], sp["nat"], sp["nat"], sp["gn"]), scratch_shapes=[pltpu.VMEM((PAIR, PAIR), F32)],
        name="ret_bwd")(proj, proj, proj, cos_t, sin_t, tabs["w"], tabs["wt"], tabs["dq"], tabs["dk"], tabs["gbd"],
                        tabs["bd"], gn, o_pre, dy, states)


def _gate_out(yf, ys, yr, proj, x, w):
    L = x.shape[0]

    def body(yf_ref, ys_ref, yr_ref, g_ref, x_ref, w_ref, xn_ref):
        cat = jnp.concatenate([yf_ref[...], ys_ref[...], yr_ref[...]], axis=-1)
        xn_ref[...] = x_ref[...] + _dot(cat * jax.nn.silu(g_ref[...]), w_ref[...])

    full = _rows(TM, D_MODEL)
    return _pallas(body, out_shape=jax.ShapeDtypeStruct((L, D_MODEL), F32), grid=(L // TM,),
                   in_specs=[_rows(TM, D_FOX), _rows(TM, D_S5), _rows(TM, D_RET), _rows(TM, D_MODEL, O_GATE // D_MODEL),
                             full, _whole((D_MODEL, D_MODEL))],
                   out_specs=full, name="gate_out")(yf, ys, yr, proj, x, w)


def _gate_out_bwd(dxn, w, yf, ys, yr, proj):
    L = dxn.shape[0]

    def body(dx_ref, w_ref, yf_ref, ys_ref, yr_ref, g_ref, dyf_ref, dys_ref, dyr_ref, dg_ref, dw_ref):
        dxv = dx_ref[...].astype(_MXU)
        dy = _dot(dxv, w_ref[...], _NT)
        g = g_ref[...]
        sg = jax.nn.sigmoid(g)
        silu = g * sg
        dcat = dy * silu
        dyf_ref[...] = dcat[:, :D_FOX]
        dys_ref[...] = dcat[:, D_FOX:D_FOX + D_S5]
        dyr_ref[...] = dcat[:, D_FOX + D_S5:]
        cat = jnp.concatenate([yf_ref[...], ys_ref[...], yr_ref[...]], axis=-1)
        dg_ref[...] = dy * cat * (sg * (1.0 + g * (1.0 - sg)))
        dw = _dot(cat * silu, dxv, _TN)

        @pl.when(pl.program_id(0) == 0)
        def _():
            dw_ref[...] = dw

        @pl.when(pl.program_id(0) != 0)
        def _():
            dw_ref[...] += dw

    full = _rows(TM, D_MODEL)
    f, s, r = _rows(TM, D_FOX), _rows(TM, D_S5), _rows(TM, D_RET)
    sq = _whole((D_MODEL, D_MODEL))
    return _pallas(body, out_shape=(jax.ShapeDtypeStruct((L, D_FOX), F32), jax.ShapeDtypeStruct((L, D_S5), F32),
                                    jax.ShapeDtypeStruct((L, D_RET), F32), jax.ShapeDtypeStruct((L, D_MODEL), F32),
                                    jax.ShapeDtypeStruct((D_MODEL, D_MODEL), F32)),
                   grid=(L // TM,), in_specs=[full, sq, f, s, r, _rows(TM, D_MODEL, O_GATE // D_MODEL)],
                   out_specs=(f, s, r, full, sq), name="gate_out_bwd")(dxn, w, yf, ys, yr, proj)


def _final_loss(x, g, tgt):
    L = x.shape[0]

    def body(x_ref, g_ref, t_ref, loss_ref, dx_ref, dg_ref):
        xv, gv = x_ref[...], g_ref[...]
        r = lax.rsqrt(jnp.mean(xv * xv, axis=-1, keepdims=True) + EPS)
        err = xv * r * gv - t_ref[...]
        part = 0.5 * jnp.sum(jnp.mean(err * err, axis=-1, keepdims=True), axis=0, keepdims=True)
        dx, dg = _rms_bwd(xv, gv, err * (1.0 / D_MODEL))
        dx_ref[...] = dx

        @pl.when(pl.program_id(0) == 0)
        def _():
            loss_ref[...] = part
            dg_ref[...] = dg

        @pl.when(pl.program_id(0) != 0)
        def _():
            loss_ref[...] += part
            dg_ref[...] += dg

    full = _rows(TM, D_MODEL)
    return _pallas(body, out_shape=(jax.ShapeDtypeStruct((1, 1), F32), jax.ShapeDtypeStruct((L, D_MODEL), F32),
                                    jax.ShapeDtypeStruct((1, D_MODEL), F32)),
                   grid=(L // TM,), in_specs=[full, _whole((1, D_MODEL)), full],
                   out_specs=(_whole((1, 1)), full, _whole((1, D_MODEL))), name="final_loss")(x, g, tgt)


def _block_diag(blocks):
    n, g, r, c = blocks.shape
    eye = jnp.eye(g, dtype=blocks.dtype)
    return (blocks[:, :, :, None, :] * eye[None, :, None, :, None]).reshape(n, g * r, g * c)


def _diag_blocks(m, g):
    n, r, c = m.shape[0], m.shape[1] // g, m.shape[2] // g
    eye = jnp.eye(g, dtype=m.dtype)
    return jnp.sum(m.reshape(n, g, r, g, c) * eye[None, :, None, :, None], axis=3)


def _rope_tables(L):
    half = HEAD_DIM // 2
    freqs = ROPE_BASE ** (-jnp.arange(half, dtype=F32) / half)
    ang = jnp.arange(L, dtype=F32)[:, None] * freqs[None, :]
    cos, sin = jnp.cos(ang), jnp.sin(ang)
    cos_t = jnp.tile(jnp.concatenate([cos, cos], axis=-1), (1, RET_HEADS))
    sin_t = jnp.tile(jnp.concatenate([sin, -sin], axis=-1), (1, RET_HEADS))
    return cos_t, sin_t


def _s5_disc_args(small):
    g, s, ch = S5_GROUPS, S5_STATE, S5_GROUP_CH
    return (small["s5_a_re"], small["s5_a_im"], small["s5_log_dt"][:, :, None],
            small["s5_b_re"].reshape(DEPTH, g, s * ch), small["s5_b_im"].reshape(DEPTH, g, s * ch))


def _s5_mats(small):
    g, s, ch = S5_GROUPS, S5_STATE, S5_GROUP_CH
    lr, li, bbr, bbi = _s5_disc(*_s5_disc_args(small))
    lam = jnp.concatenate([lr.reshape(DEPTH, 8, 128), li.reshape(DEPTH, 8, 128)], axis=1)
    wb = jnp.concatenate([_block_diag(b.reshape(DEPTH, g, s, ch).transpose(0, 1, 3, 2)) for b in (bbr, bbi)], axis=2)
    wc = jnp.concatenate([_block_diag(c.transpose(0, 1, 3, 2)) for c in (small["s5_c_re"], -small["s5_c_im"])], axis=1)
    return lam, wb.astype(_MXU), wc.astype(_MXU)


def _s5_param_grads(small, dwb, dwc, dlam):
    g, s, ch = S5_GROUPS, S5_STATE, S5_GROUP_CH
    dc = [_diag_blocks(m, g).transpose(0, 1, 3, 2) for m in (dwc[:, :S5_CH], dwc[:, S5_CH:])]
    dbb = [_diag_blocks(m, g).transpose(0, 1, 3, 2).reshape(DEPTH, g, s * ch) for m in (dwb[:, :, :S5_CH], dwb[:, :, S5_CH:])]
    dar, dai, dldt, dbr, dbi = _s5_disc_bwd(*_s5_disc_args(small), dlam[:, :8].reshape(DEPTH, g, s),
                                            dlam[:, 8:].reshape(DEPTH, g, s), dbb[0], dbb[1])
    shp = (DEPTH, g, s, ch)
    return dict(s5_a_re=dar, s5_a_im=dai, s5_log_dt=dldt.reshape(DEPTH, g), s5_b_re=dbr.reshape(shp),
                s5_b_im=dbi.reshape(shp), s5_c_re=dc[0], s5_c_im=-dc[1])


def _layer_fwd(x, p, rope, ride=None, late=False):
    L = x.shape[0]
    cos_t, sin_t, ret_tabs = rope
    s = {"x": x}
    proj, h = _norm_inproj(x, p["norm_w"], p["w_in"])
    s["proj"], s["h"] = proj, h
    qa, ka, kat, vt = _fox_prep(proj, _fox_cumsum(proj, p["b_f"]))
    yf, lse, landed = _fox_fwd(qa, ka, vt, ride)
    s.update(qa=qa, ka=ka, kat=kat, lse=lse, yf=yf)
    if late:
        p["w_glu"], p["w_out"] = _gathered_rows(landed[-2]), _gathered_rows(landed[-1])
        landed = landed[:-2]
    xs, ypre, ys = _s5_fwd(proj, p["wb"], p["wc"], p["lam"], p["d"], p["w_glu"])
    s.update(xs=xs, ypre=ypre, ys=ys)
    o_pre, yr, states = _ret_fwd(proj, cos_t, sin_t, ret_tabs, p["gn_w"])
    s.update(o_pre=o_pre, yr=yr, states=states)
    return _gate_out(yf, ys, yr, proj, x, p["w_out"]), s, landed


def _layer_bwd(dxn, s, p, rope, ride=None, early=False):
    L = dxn.shape[0]
    cos_t, sin_t, ret_tabs = rope
    g = {}
    proj = s["proj"]
    dyf, dys, dyr, dgate, g["w_out"] = _gate_out_bwd(dxn, p["w_out"], s["yf"], s["ys"], s["yr"], proj)
    drq, drk, drv, dgn = _ret_bwd(proj, cos_t, sin_t, ret_tabs, p["gn_w"], s["o_pre"], dyr, s["states"])
    g["ret_gn_w"] = dgn.reshape(D_RET)
    dsu, g["wb"], g["wc"], g["lam"], dd, g["s5_w_glu"] = _s5_bwd(proj, s["ypre"], dys, s["xs"], p["wb"], p["wc"], p["lam"],
                                                                 p["d"], p["w_glu"])
    g["s5_d"] = dd.reshape(D_S5)
    if early:
        ride = (ride[0] + _row_slots(g), ride[1] + [True, True])
    dqt, dkraw, dv, landed = _fox_bwd(s["qa"], s["ka"], s["kat"], proj, dyf, s["yf"], s["lse"], ride)
    dq, dk, dfl, dbf = _fox_post_bwd(dqt, dkraw, proj, p["b_f"])
    g["fox_b_f"] = dbf[0, :FOX_HEADS]
    pieces = [dgate, dq, dk, dv, dsu, drq, drk, drv, dfl]
    dx, dnw, g["w_in"] = _inproj_bwd(pieces, p["w_in"], s["x"], p["norm_w"], dxn, s["h"])
    g["norm_w"] = dnw.reshape(D_MODEL)
    return dx, g, landed


def _layer_params(l, w_in_p, w_glu, w_out, small, s5_mats):
    lam, wb, wc = s5_mats
    return dict(
        norm_w=small["norm_w"][l][None], w_in=w_in_p, b_f=jnp.pad(small["fox_b_f"][l], (0, PAIR - FOX_HEADS))[None],
        lam=lam[l], wb=wb[l], wc=wc[l], d=small["s5_d"][l][None], w_glu=w_glu, gn_w=small["ret_gn_w"][l][None], w_out=w_out)


_SHARDED = ("w_in", "s5_w_glu", "w_out")
_WIRE = jnp.bfloat16


_RUNS = ((2568, 3592, O_GATE), (0, 1536, O_FQ), (1544, 2568, O_SU), (1536, 1544, O_FL))


def _shard_pieces():
    out = []
    for a, b, pad in _RUNS:
        while a < b:
            j = a // W_SHARD
            e = min(b, (j + 1) * W_SHARD)
            out.append((j, a - j * W_SHARD, e - j * W_SHARD, pad))
            pad, a = pad + e - a, e
    return out


def _gathered_w_in(g_in):
    cols = [g_in[j, :, a:e] for j, a, e, _ in _shard_pieces()]
    cols.append(jnp.zeros((D_MODEL, D_INP - O_FL - FOX_HEADS), g_in.dtype))
    return jnp.concatenate(cols, axis=1)


def _gathered_rows(g):
    return g.reshape(-1, g.shape[-1])


def _w_in_slots(g):
    w_in = g["w_in"].astype(_WIRE)
    slots = []
    for j in range(N_DEV):
        mine = sorted((a, e, pad) for jj, a, e, pad in _shard_pieces() if jj == j)
        slots.append(jnp.concatenate([w_in[:, pad:pad + e - a] for a, e, pad in mine], axis=1))
    return jnp.stack(slots)


def _row_slots(g):
    return [g["s5_w_glu"].reshape(N_DEV, D_S5 // N_DEV, D_S5).astype(_WIRE),
            g["w_out"].reshape(N_DEV, D_MODEL // N_DEV, D_MODEL).astype(_WIRE)]


def _step_grads(x, tgt, small, full=None, shards=None):
    L = x.shape[0]
    rope = _rope_tables(L) + (_ret_tables(min(TQ, L)),)
    s5_mats = _s5_mats(small)
    if shards is not None:
        nxt = (_gathered_w_in(_exchange([shards[0][0]], [False], "gather_layer0")[0]), None, None)
    saved, params = [], []
    for l in range(DEPTH):
        weights = nxt if shards is not None else tuple(f[l] for f in full)
        ride = None
        if shards is not None:
            arrs = [s[l + 1] for s in shards] if l + 1 < DEPTH else []
            arrs += [shards[1][0], shards[2][0]] if l == 0 else []
            ride = (arrs, [False] * len(arrs)) if arrs else None
        params.append(_layer_params(l, *weights, small, s5_mats))
        x, s, landed = _layer_fwd(x, params[l], rope, ride, late=shards is not None and l == 0)
        if landed:
            nxt = (_gathered_w_in(landed[0]), _gathered_rows(landed[1]), _gathered_rows(landed[2]))
        saved.append(s)
    loss, dx, dfw = _final_loss(x, small["final_norm_w"][None], tgt)
    grads, partials, waiting = [None] * DEPTH, [None] * DEPTH, None
    for l in reversed(range(DEPTH)):
        ride = (waiting, [True] * len(waiting)) if waiting is not None else None
        dx, grads[l], landed = _layer_bwd(dx, saved[l], params[l], rope, ride, early=ride is not None and l == 0)
        if waiting is not None:
            partials[l + 1] = landed[:3]
        if shards is not None:
            waiting = [_w_in_slots(grads[l])] + _row_slots(grads[l])
    stack = lambda n: jnp.stack([g[n] for g in grads])
    small_g = {n: stack(n) for n in ("norm_w", "fox_b_f", "s5_d", "ret_gn_w")}
    small_g.update(_s5_param_grads(small, stack("wb"), stack("wc"), stack("lam")), final_norm_w=dfw.reshape(D_MODEL))
    if shards is None:
        return loss, dx, grads, small_g
    packed = _pack([small_g[n] for n in _SMALL]).astype(_WIRE)
    last = _exchange([waiting[0], packed], [True, False], "exchange_layer0")
    partials[0] = [last[0]] + landed[3:]
    return loss, dx, grads, small_g, partials, last[1]


_MESH = pl.DeviceIdType.MESH
_ANY = pl.BlockSpec(memory_space=pl.ANY)


def _me_and_peers():
    x, y, c = lax.axis_index("x"), lax.axis_index("y"), lax.axis_index("c")
    flip = lambda a, bit: (1 - a) if bit else a
    peers = []
    for r in range(1, N_DEV):
        px, py, pc = flip(x, (r >> 2) & 1), flip(y, (r >> 1) & 1), flip(c, r & 1)
        peers.append(((px, py, pc), 4 * px + 2 * py + pc))
    return 4 * x + 2 * y + c, peers


def _exchange_copies(srcs, dsts, sems, scatter):
    send_sems, recv_sems, local_sems = sems
    me, peers = _me_and_peers()
    pick = lambda t, to: srcs[t].at[to] if scatter[t] else srcs[t]
    own = [pltpu.make_async_copy(pick(t, me), dsts[t].at[me], local_sems.at[t]) for t in range(len(srcs))]
    sends, waits = [], []
    for r, (dev, idx) in enumerate(peers):
        for t in range(len(srcs)):
            for land, out in ((me, sends), (idx, waits)):
                out.append(pltpu.make_async_remote_copy(pick(t, idx), dsts[t].at[land], send_sems.at[t, r], recv_sems.at[t, r],
                                                        device_id=dev, device_id_type=_MESH))
    return own, sends, waits


def _exchange_start(srcs, dsts, sems, scatter):
    own, sends, _ = _exchange_copies(srcs, dsts, sems, scatter)
    for cp in own + sends:
        cp.start()


def _exchange_wait(srcs, dsts, sems, scatter):
    own, _, waits = _exchange_copies(srcs, dsts, sems, scatter)
    for cp in waits + own:
        cp.wait()


def _exchange_shapes(arrs, scatter):
    outs = [jax.ShapeDtypeStruct(a.shape if sc else (N_DEV,) + a.shape, a.dtype) for a, sc in zip(arrs, scatter)]
    n = len(arrs)
    sems = [pltpu.SemaphoreType.DMA((n, N_DEV - 1)), pltpu.SemaphoreType.DMA((n, N_DEV - 1)), pltpu.SemaphoreType.DMA((n,))]
    return outs, sems


def _exchange(arrs, scatter, name):
    n = len(arrs)

    def body(*refs):
        _exchange_start(refs[:n], refs[n:2 * n], refs[2 * n:], scatter)
        _exchange_wait(refs[:n], refs[n:2 * n], refs[2 * n:], scatter)

    outs, sems = _exchange_shapes(arrs, scatter)
    return _pallas(body, out_shape=tuple(outs), in_specs=[_ANY] * n, out_specs=tuple([_ANY] * n), scratch_shapes=sems,
                   name=name)(*arrs)


def _riding(body, n_in, n_out, ride, is_first, is_last):
    if ride is None:
        return body, [], [], [], []
    arrs, scatter = ride
    n = len(arrs)
    outs, sems = _exchange_shapes(arrs, scatter)

    def wrapped(*refs):
        ins, srcs = refs[:n_in], refs[n_in:n_in + n]
        own_outs, dsts = refs[n_in + n:n_in + n + n_out], refs[n_in + n + n_out:n_in + 2 * n + n_out]
        scratch, ex_sems = refs[n_in + 2 * n + n_out:-3], refs[-3:]

        @pl.when(is_first())
        def _():
            _exchange_start(srcs, dsts, ex_sems, scatter)

        body(*ins, *own_outs, *scratch)

        @pl.when(is_last())
        def _():
            _exchange_wait(srcs, dsts, ex_sems, scatter)

    return wrapped, list(arrs), [_ANY] * n, outs, sems


def _adamw(parts, w, m, v, name):
    n, nb, rows, cols = parts.shape
    tm = next(t for t in (256, 128, 64, 32, 16) if rows % t == 0)

    def body(p_ref, w_ref, m_ref, v_ref, g_ref, d_ref, nm_ref, nv_ref):
        g = p_ref[0].astype(F32)
        for i in range(1, n):
            g = g + p_ref[i].astype(F32)
        nm = ADAM_B1 * m_ref[...] + (1.0 - ADAM_B1) * g
        nv = ADAM_B2 * v_ref[...] + (1.0 - ADAM_B2) * jnp.square(g)
        m_hat = nm / (1.0 - ADAM_B1 ** ADAM_STEP)
        v_hat = nv / (1.0 - ADAM_B2 ** ADAM_STEP)
        g_ref[...] = g
        d_ref[...] = -ADAM_LR * (m_hat / (jnp.sqrt(v_hat) + ADAM_EPS) + ADAM_WD * w_ref[...])
        nm_ref[...] = nm
        nv_ref[...] = nv

    row = pl.BlockSpec((None, tm, cols), lambda b, i: (b, i, 0))
    return _pallas(body, out_shape=(jax.ShapeDtypeStruct((nb, rows, cols), F32),) * 4, grid=(nb, rows // tm),
                   in_specs=[pl.BlockSpec((n, None, tm, cols), lambda b, i: (0, b, i, 0)), row, row, row],
                   out_specs=(row,) * 4, name=name)(parts, w, m, v)


_WEIGHTS = ("norm_w", "w_in", "fox_b_f", "s5_a_re", "s5_a_im", "s5_b_re", "s5_b_im", "s5_c_re", "s5_c_im", "s5_d",
            "s5_log_dt", "s5_w_glu", "ret_gn_w", "w_out", "final_norm_w")
_SMALL = tuple(n for n in _WEIGHTS if n not in _SHARDED)
_LANES = 128


def _pack(arrs):
    flat = jnp.concatenate([a.reshape(-1) for a in arrs])
    rows = -(-flat.shape[0] // (_LANES * _LANES)) * _LANES
    return jnp.pad(flat, (0, rows * _LANES - flat.shape[0])).reshape(rows, _LANES)


def _unpack(packed, like):
    flat, out, off = packed.reshape(-1), [], 0
    for a in like:
        out.append(flat[off:off + a.size].reshape(a.shape))
        off += a.size
    return out


def kernel(x, norm_w, w_in, fox_b_f, s5_a_re, s5_a_im, s5_b_re, s5_b_im, s5_c_re, s5_c_im, s5_d, s5_log_dt, s5_w_glu, ret_gn_w, w_out, final_norm_w, loss_target, m_norm_w, m_w_in, m_fox_b_f, m_s5_a_re, m_s5_a_im, m_s5_b_re, m_s5_b_im, m_s5_c_re, m_s5_c_im, m_s5_d, m_s5_log_dt, m_s5_w_glu, m_ret_gn_w, m_w_out, m_final_norm_w, v_norm_w, v_w_in, v_fox_b_f, v_s5_a_re, v_s5_a_im, v_s5_b_re, v_s5_b_im, v_s5_c_re, v_s5_c_im, v_s5_d, v_s5_log_dt, v_s5_w_glu, v_ret_gn_w, v_w_out, v_final_norm_w):
    w = dict(norm_w=norm_w, w_in=w_in, fox_b_f=fox_b_f, s5_a_re=s5_a_re, s5_a_im=s5_a_im, s5_b_re=s5_b_re, s5_b_im=s5_b_im,
             s5_c_re=s5_c_re, s5_c_im=s5_c_im, s5_d=s5_d, s5_log_dt=s5_log_dt, s5_w_glu=s5_w_glu, ret_gn_w=ret_gn_w,
             w_out=w_out, final_norm_w=final_norm_w)
    m = dict(norm_w=m_norm_w, w_in=m_w_in, fox_b_f=m_fox_b_f, s5_a_re=m_s5_a_re, s5_a_im=m_s5_a_im, s5_b_re=m_s5_b_re,
             s5_b_im=m_s5_b_im, s5_c_re=m_s5_c_re, s5_c_im=m_s5_c_im, s5_d=m_s5_d, s5_log_dt=m_s5_log_dt,
             s5_w_glu=m_s5_w_glu, ret_gn_w=m_ret_gn_w, w_out=m_w_out, final_norm_w=m_final_norm_w)
    v = dict(norm_w=v_norm_w, w_in=v_w_in, fox_b_f=v_fox_b_f, s5_a_re=v_s5_a_re, s5_a_im=v_s5_a_im, s5_b_re=v_s5_b_re,
             s5_b_im=v_s5_b_im, s5_c_re=v_s5_c_re, s5_c_im=v_s5_c_im, s5_d=v_s5_d, s5_log_dt=v_s5_log_dt,
             s5_w_glu=v_s5_w_glu, ret_gn_w=v_ret_gn_w, w_out=v_w_out, final_norm_w=v_final_norm_w)

    small = {n: w[n] for n in _SMALL}
    loss, dx, _, _, partials, r_small = _step_grads(x[0], loss_target[0], small, shards=[w[n].astype(_MXU) for n in _SHARDED])

    res = {}
    for t, n in enumerate(_SHARDED):
        res[n] = _adamw(jnp.stack([partials[l][t] for l in range(DEPTH)], axis=1), w[n], m[n], v[n], "adamw_" + n)
    small_w = [w[n] for n in _SMALL]
    outs = _adamw(r_small[:, None], *[_pack([d[n] for n in _SMALL])[None] for d in (w, m, v)], "adamw_small")
    for k, o in enumerate(outs):
        for n, a in zip(_SMALL, _unpack(o, small_w)):
            res.setdefault(n, [None] * 4)[k] = a

    loss = lax.psum(loss[0, 0], ("x", "y", "c"))
    return (loss, dx[None], *[res[n][0] for n in _WEIGHTS], *[res[n][1] for n in _WEIGHTS],
            *[res[n][2] for n in _WEIGHTS], *[res[n][3] for n in _WEIGHTS])
```

```python
import math

import jax
import jax.numpy as jnp
from jax import lax
from jax.experimental import pallas as pl
from jax.experimental.pallas import tpu as pltpu

F32 = jnp.float32
_MXU = jnp.bfloat16
_HI = lax.Precision.HIGHEST

N_DEV = 8
DEPTH = 4
D_MODEL = 1024
HEAD_DIM = 64
D_FOX = 512
FOX_HEADS = 8
D_S5 = 256
S5_GROUPS = 16
S5_GROUP_CH = 16
S5_STATE = 64
S5_CH = S5_GROUPS * S5_STATE
D_RET = 256
RET_HEADS = 4
CHUNK = 64
ROPE_BASE = 10000.0
EPS = 1e-6
D_IN = 3592
D_INP = 3712
W_SHARD = D_IN // N_DEV
O_GATE, O_FQ, O_FK, O_FV, O_SU, O_RQ, O_RK, O_RV, O_FL = 0, 1024, 1536, 2048, 2560, 2816, 3072, 3328, 3584

ADAM_LR, ADAM_B1, ADAM_B2, ADAM_EPS, ADAM_WD, ADAM_STEP = 0.001, 0.9, 0.999, 1e-08, 0.01, 10

TM = 256
TMB = 512
TQ = 512
TS = 512
NEG = -1e30
VMEM_BIG = 56 * 1024 * 1024


def _pallas(body, **kw):
    return pl.pallas_call(body, **kw)


def _whole(shape):
    n = len(shape)
    return pl.BlockSpec(shape, lambda *_: (0,) * n)


def _rows(tm, width, col=0):
    return pl.BlockSpec((tm, width), lambda i: (i, col))


def _of_layer(param):
    a, l = param
    return a, pl.BlockSpec((None,) + a.shape[1:], lambda *_: (l,) + (0,) * (a.ndim - 1))


def _dot(a, b, dims=(((1,), (0,)), ((), ()))):
    return lax.dot_general(a.astype(_MXU), b.astype(_MXU), dims, preferred_element_type=F32)


_NT = (((1,), (1,)), ((), ()))
_TN = (((0,), (0,)), ((), ()))


def _norm_inproj(x, g, w):
    L = x.shape[0]

    def body(x_ref, g_ref, w_ref, p_ref, h_ref):
        xv = x_ref[...]
        r = lax.rsqrt(jnp.mean(xv * xv, axis=-1, keepdims=True) + EPS)
        h = (xv * r * g_ref[...]).astype(_MXU)
        h_ref[...] = h
        p_ref[...] = _dot(h, w_ref[...])

    tm = min(TMB, L)
    g, g_spec = _of_layer(g)
    return _pallas(body, out_shape=(jax.ShapeDtypeStruct((L, D_INP), F32), jax.ShapeDtypeStruct((L, D_MODEL), _MXU)),
                   grid=(L // tm,),
                   in_specs=[_rows(tm, D_MODEL), g_spec,
                             pl.BlockSpec((D_MODEL, D_INP), lambda i: (0, 0), pipeline_mode=pl.Buffered(1))],
                   out_specs=(_rows(tm, D_INP), _rows(tm, D_MODEL)), name="norm_inproj",
                   compiler_params=pltpu.CompilerParams(vmem_limit_bytes=VMEM_BIG))(x, g, w)


def _rms_bwd(xv, g, dh):
    r = lax.rsqrt(jnp.mean(xv * xv, axis=-1, keepdims=True) + EPS)
    xh = xv * r
    dg = jnp.sum(dh * xh, axis=0, keepdims=True)
    dxh = dh * g
    dx = r * (dxh - xh * jnp.mean(dxh * xh, axis=-1, keepdims=True))
    return dx, dg


def _inproj_bwd(pieces, w, x, g, dres, h):
    L = x.shape[0]
    n = len(pieces)

    def body(*refs):
        w_ref, x_ref, g_ref, dr_ref, h_ref, dx_ref, dg_ref, dw_ref = refs[n:]
        dproj = jnp.concatenate([r[...].astype(_MXU) for r in refs[:n]], axis=-1)
        dh = _dot(dproj, w_ref[...], _NT)
        dx, dg = _rms_bwd(x_ref[...], g_ref[...], dh)
        dx_ref[...] = dx + dr_ref[...]
        dw = _dot(h_ref[...], dproj, _TN)

        @pl.when(pl.program_id(0) == 0)
        def _():
            dg_ref[...] = dg
            dw_ref[...] = dw

        @pl.when(pl.program_id(0) != 0)
        def _():
            dg_ref[...] += dg
            dw_ref[...] += dw

    resident = pl.BlockSpec((D_MODEL, D_INP), lambda i: (0, 0), pipeline_mode=pl.Buffered(1))
    g, g_spec = _of_layer(g)
    return _pallas(body, out_shape=(jax.ShapeDtypeStruct((L, D_MODEL), F32), jax.ShapeDtypeStruct((1, D_MODEL), F32),
                                    jax.ShapeDtypeStruct((D_MODEL, D_INP), F32)),
                   grid=(L // TM,),
                   in_specs=[_rows(TM, p.shape[1]) for p in pieces]
                   + [resident, _rows(TM, D_MODEL), g_spec, _rows(TM, D_MODEL), _rows(TM, D_MODEL)],
                   out_specs=(_rows(TM, D_MODEL), _whole((1, D_MODEL)), resident), name="inproj_bwd",
                   compiler_params=pltpu.CompilerParams(vmem_limit_bytes=VMEM_BIG))(*pieces, w, x, g, dres, h)


PAIR = 2 * HEAD_DIM
N_AUX = 3


def _own(shape, h):
    return lax.broadcasted_iota(jnp.int32, shape, len(shape) - 1) // HEAD_DIM == h


def _hi_dot(a, b):
    return jnp.dot(a, b, precision=_HI, preferred_element_type=F32)


def _tri(n, lower):
    r = lax.broadcasted_iota(jnp.int32, (n, n), 0)
    c = lax.broadcasted_iota(jnp.int32, (n, n), 1)
    return jnp.where(r >= c if lower else r <= c, 1.0, 0.0).astype(F32)


def _fox_cumsum(proj, b):
    L = proj.shape[0]

    def body(fl_ref, b_ref, c_ref, carry_sc):
        @pl.when(pl.program_id(0) == 0)
        def _():
            carry_sc[...] = jnp.zeros((1, PAIR), F32)

        lane = lax.broadcasted_iota(jnp.int32, (TM, PAIR), 1)
        lf = jnp.where(lane < FOX_HEADS, jax.nn.log_sigmoid(fl_ref[...] + b_ref[...]), 0.0)
        cs = _hi_dot(_tri(TM, True), lf) + carry_sc[...]
        c_ref[...] = cs
        carry_sc[...] = cs[TM - 1:TM, :]

    b, b_spec = _of_layer(b)
    return _pallas(body, out_shape=jax.ShapeDtypeStruct((L, PAIR), F32), grid=(L // TM,),
                   in_specs=[_rows(TM, PAIR, O_FL // PAIR), b_spec], out_specs=_rows(TM, PAIR),
                   scratch_shapes=[pltpu.VMEM((1, PAIR), F32)], name="fox_cumsum")(proj, b)


def _fox_prep(proj, c):
    L = proj.shape[0]

    def body(q_ref, k_ref, v_ref, c_ref, qa_ref, ka_ref, kat_ref, vt_ref):
        lane = lax.broadcasted_iota(jnp.int32, (TM, PAIR), 1)
        cv = c_ref[...]
        for p in range(FOX_HEADS // 2):
            cols = slice(PAIR * p, PAIR * (p + 1))
            q2, k2 = q_ref[:, cols], k_ref[:, cols]
            vt_ref[p] = v_ref[:, cols].T.astype(_MXU)
            for e in range(2):
                h = 2 * p + e
                own = lane // HEAD_DIM == e
                a = lane - (HEAD_DIM if e == 0 else 0)
                pick = (lax.broadcasted_iota(jnp.int32, (PAIR, PAIR), 0) == h).astype(F32)
                rest = _hi_dot(cv, pick)
                aux_q = jnp.where((a >= N_AUX) & (a < 2 * N_AUX), 1.0, 0.0)
                aux_k = jnp.where((a >= 0) & (a < N_AUX), 1.0, 0.0)
                for n in range(N_AUX):
                    part = rest.astype(_MXU).astype(F32)
                    rest = rest - part
                    aux_q = jnp.where(a == n, part, aux_q)
                    aux_k = jnp.where(a == N_AUX + n, -part, aux_k)
                ka = jnp.where(own, k2, aux_k)
                qa_ref[h] = jnp.where(own, q2 * (1.0 / math.sqrt(HEAD_DIM)), aux_q).astype(_MXU)
                ka_ref[h] = ka.astype(_MXU)
                kat_ref[h] = ka.T.astype(_MXU)

    hl = jax.ShapeDtypeStruct((FOX_HEADS, L, PAIR), _MXU)
    nat = lambda o: _rows(TM, D_FOX, o // D_FOX)
    rows = pl.BlockSpec((FOX_HEADS, TM, PAIR), lambda i: (0, i, 0))
    return _pallas(
        body, out_shape=(hl, hl, jax.ShapeDtypeStruct((FOX_HEADS, PAIR, L), _MXU),
                         jax.ShapeDtypeStruct((FOX_HEADS // 2, PAIR, L), _MXU)),
        grid=(L // TM,), in_specs=[nat(O_FQ), nat(O_FK), nat(O_FV), _rows(TM, PAIR)],
        out_specs=(rows, rows, pl.BlockSpec((FOX_HEADS, PAIR, TM), lambda i: (0, 0, i)),
                   pl.BlockSpec((FOX_HEADS // 2, PAIR, TM), lambda i: (0, 0, i))),
        name="fox_prep")(proj, proj, proj, c)


def _key_le_query(tq):
    return lax.broadcasted_iota(jnp.int32, (tq, tq), 0) <= lax.broadcasted_iota(jnp.int32, (tq, tq), 1)


def _grid_ends(n0, n1):
    first = lambda: (pl.program_id(0) == 0) & (pl.program_id(1) == 0)
    last = lambda: (pl.program_id(0) == n0 - 1) & (pl.program_id(1) == n1 - 1)
    return first, last


def _fox_fwd(qa, ka, vt, ride=None):
    H, L, _ = qa.shape
    tq = min(TQ, L)
    nq = L // tq

    def body(qa_ref, ka_ref, vt_ref, o_ref, lse_ref, m_sc, l_sc, acc_sc):
        i = pl.program_id(1)
        m_sc[...] = jnp.full((2, 1, tq), NEG, F32)
        l_sc[...] = jnp.zeros((2, 1, tq), F32)
        acc_sc[...] = jnp.zeros((2, HEAD_DIM, tq), F32)

        def block(j, nk, masked):
            keys = pl.ds(pl.multiple_of(j * tq, tq), nk * tq)
            vt_blk = vt_ref[:, keys]
            sts = [_dot(ka_ref[e, keys, :], qa_ref[e], _NT) for e in range(2)]
            pts, alphas = [], []
            for e in range(2):
                st = jnp.where(_key_le_query(tq), sts[e], NEG) if masked else sts[e]
                m_prev = m_sc[e]
                m_new = jnp.maximum(m_prev, jnp.max(st, axis=0, keepdims=True))
                alphas.append(jnp.exp(m_prev - m_new))
                pt = jnp.exp(st - m_new)
                l_sc[e] = alphas[e] * l_sc[e] + jnp.sum(pt, axis=0, keepdims=True)
                m_sc[e] = m_new
                pts.append(pt.astype(_MXU))
            for e in range(2):
                acc_sc[e] = alphas[e] * acc_sc[e] + _dot(vt_blk[HEAD_DIM * e:HEAD_DIM * (e + 1)], pts[e])

        def two_blocks(jj, carry):
            block(2 * jj, 2, False)
            return carry

        lax.fori_loop(0, i // 2, two_blocks, 0)

        @pl.when(i % 2 == 1)
        def _():
            block(i - 1, 1, False)

        block(i, 1, True)
        o_ref[...] = jnp.concatenate([acc_sc[0] / l_sc[0], acc_sc[1] / l_sc[1]], axis=0).T
        for e in range(2):
            lse_ref[e] = m_sc[e] + jnp.log(l_sc[e])

    body, ex_in, ex_specs, ex_out, ex_sems = _riding(body, 3, 2, ride, *_grid_ends(H // 2, nq))
    res = _pallas(
        body, out_shape=(jax.ShapeDtypeStruct((L, D_FOX), F32), jax.ShapeDtypeStruct((H, 1, L), F32), *ex_out),
        grid=(H // 2, nq),
        in_specs=[pl.BlockSpec((2, tq, PAIR), lambda p, i: (p, i, 0)), pl.BlockSpec((2, L, PAIR), lambda p, i: (p, 0, 0)),
                  pl.BlockSpec((None, PAIR, L), lambda p, i: (p, 0, 0)), *ex_specs],
        out_specs=(pl.BlockSpec((tq, PAIR), lambda p, i: (i, p)), pl.BlockSpec((2, 1, tq), lambda p, i: (p, 0, i)),
                   *ex_specs),
        scratch_shapes=[pltpu.VMEM((2, 1, tq), F32), pltpu.VMEM((2, 1, tq), F32), pltpu.VMEM((2, HEAD_DIM, tq), F32), *ex_sems],
        name="fox_fwd" if ride is None else "fox_fwd_gather")(qa, ka, vt, *ex_in)
    return res[0], res[1], list(res[2:])


def _fox_bwd(qa, ka, kat, proj, do, o, lse, ride=None):
    H, L, _ = qa.shape
    tq = min(TQ, L)
    nq = L // tq

    def body(qa_ref, ka_ref, kat_ref, v_ref, do_ref, o_ref, lse_ref, dqt_ref, dk_ref, dv_ref, delta_sc, dk_sc, dv_sc):
        j = pl.program_id(1)

        @pl.when(j == 0)
        def _():
            head_rows = (lax.broadcasted_iota(jnp.int32, (8, PAIR), 1) // HEAD_DIM
                         == lax.broadcasted_iota(jnp.int32, (8, PAIR), 0)).astype(F32)
            delta_sc[...] = lax.dot_general(head_rows, do_ref[...] * o_ref[...], _NT, precision=_HI,
                                            preferred_element_type=F32)
            dqt_ref[...] = jnp.zeros((2, PAIR, L), F32)

        dk_sc[...] = jnp.zeros((2, tq, PAIR), F32)
        dv_sc[...] = jnp.zeros((tq, PAIR), F32)
        vb = v_ref[...]

        def block(i, masked):
            qs = pl.ds(pl.multiple_of(i * tq, tq), tq)
            dob = do_ref[qs, :]
            for e in range(2):
                own = _own((tq, PAIR), e)
                qh = qa_ref[e, qs, :]
                pt = jnp.exp(_dot(ka_ref[e], qh, _NT) - lse_ref[e, :, qs])
                if masked:
                    pt = jnp.where(_key_le_query(tq), pt, 0.0)
                dv_sc[...] += _dot(pt, jnp.where(own, dob, 0.0))
                dpt = _dot(jnp.where(own, vb, 0.0), dob, _NT)
                ds = (pt * (dpt - delta_sc[e:e + 1, qs])).astype(_MXU)
                dk_sc[e] += _dot(ds, qh)
                dqt_ref[e, :, qs] += _dot(kat_ref[e], ds)

        def off_diagonal(i, carry):
            block(i, False)
            return carry

        block(j, True)
        lax.fori_loop(j + 1, nq, off_diagonal, 0)
        dk_ref[...] = dk_sc[...]
        dv_ref[...] = dv_sc[...]

    nat = pl.BlockSpec((L, PAIR), lambda p, j: (0, p))
    body, ex_in, ex_specs, ex_out, ex_sems = _riding(body, 7, 3, ride, *_grid_ends(H // 2, nq))
    res = _pallas(
        body, out_shape=(jax.ShapeDtypeStruct((H, PAIR, L), F32), jax.ShapeDtypeStruct((H, L, PAIR), F32),
                         jax.ShapeDtypeStruct((L, D_FOX), F32), *ex_out),
        grid=(H // 2, nq),
        in_specs=[pl.BlockSpec((2, L, PAIR), lambda p, j: (p, 0, 0)), pl.BlockSpec((2, tq, PAIR), lambda p, j: (p, j, 0)),
                  pl.BlockSpec((2, PAIR, tq), lambda p, j: (p, 0, j)),
                  pl.BlockSpec((tq, PAIR), lambda p, j: (j, O_FV // PAIR + p)), nat, nat,
                  pl.BlockSpec((2, 1, L), lambda p, j: (p, 0, 0)), *ex_specs],
        out_specs=(pl.BlockSpec((2, PAIR, L), lambda p, j: (p, 0, 0)), pl.BlockSpec((2, tq, PAIR), lambda p, j: (p, j, 0)),
                   pl.BlockSpec((tq, PAIR), lambda p, j: (j, p)), *ex_specs),
        scratch_shapes=[pltpu.VMEM((8, L), F32), pltpu.VMEM((2, tq, PAIR), F32), pltpu.VMEM((tq, PAIR), F32), *ex_sems],
        name="fox_bwd" if ride is None else "fox_bwd_exchange",
        compiler_params=pltpu.CompilerParams(vmem_limit_bytes=VMEM_BIG))(qa, ka, kat, proj, do, o, lse, *ex_in)
    return res[0], res[1], res[2], list(res[3:])


def _fox_post_bwd(dqt, dkraw, proj, b):
    L = proj.shape[0]
    nb = L // TM

    def body(dqt_ref, dkr_ref, fl_ref, b_ref, dq_ref, dk_ref, dfl_ref, db_ref, carry_sc):
        first = pl.program_id(0) == 0

        @pl.when(first)
        def _():
            carry_sc[...] = jnp.zeros((1, PAIR), F32)

        lane = lax.broadcasted_iota(jnp.int32, (TM, PAIR), 1)
        rr = lax.broadcasted_iota(jnp.int32, (PAIR, PAIR), 0)
        cc = lax.broadcasted_iota(jnp.int32, (PAIR, PAIR), 1)
        dc = jnp.zeros((TM, PAIR), F32)
        for p in range(FOX_HEADS // 2):
            cols = slice(PAIR * p, PAIR * (p + 1))
            dqs = [dqt_ref[2 * p + e].T for e in range(2)]
            dks = [dkr_ref[2 * p + e] for e in range(2)]
            dq_ref[:, cols] = jnp.where(lane < HEAD_DIM, dqs[0], dqs[1]) * (1.0 / math.sqrt(HEAD_DIM))
            dk_ref[:, cols] = jnp.where(lane < HEAD_DIM, dks[0], dks[1])
            sums = jnp.zeros((TM, PAIR), F32)
            place = jnp.zeros((PAIR, PAIR), F32)
            for e in range(2):
                base = HEAD_DIM if e == 0 else 0
                sums = jnp.where(lane == base, dqs[e], jnp.where(lane == base + N_AUX, -dks[e], sums))
                place = jnp.where(((rr == base) | (rr == base + N_AUX)) & (cc == 2 * p + e), 1.0, place)
            dc = dc + _hi_dot(sums, place)
        rs = _hi_dot(_tri(TM, False), dc) + carry_sc[...]
        carry_sc[...] = rs[0:1, :]
        dfl = jnp.where(lane < FOX_HEADS, rs * jax.nn.sigmoid(-(fl_ref[...] + b_ref[...])), 0.0)
        dfl_ref[...] = dfl
        db = jnp.sum(dfl, axis=0, keepdims=True)

        @pl.when(first)
        def _():
            db_ref[...] = db

        @pl.when(jnp.logical_not(first))
        def _():
            db_ref[...] += db

    rev = lambda i: nb - 1 - i
    b, b_spec = _of_layer(b)
    nat = pl.BlockSpec((TM, D_FOX), lambda i: (rev(i), 0))
    return _pallas(
        body, out_shape=(jax.ShapeDtypeStruct((L, D_FOX), F32),) * 2
        + (jax.ShapeDtypeStruct((L, PAIR), F32), jax.ShapeDtypeStruct((1, PAIR), F32)),
        grid=(nb,),
        in_specs=[pl.BlockSpec((FOX_HEADS, PAIR, TM), lambda i: (0, 0, rev(i))),
                  pl.BlockSpec((FOX_HEADS, TM, PAIR), lambda i: (0, rev(i), 0)),
                  pl.BlockSpec((TM, PAIR), lambda i: (rev(i), O_FL // PAIR)), b_spec],
        out_specs=(nat, nat, pl.BlockSpec((TM, PAIR), lambda i: (rev(i), 0)), _whole((1, PAIR))),
        scratch_shapes=[pltpu.VMEM((1, PAIR), F32)], name="fox_post_bwd")(dqt, dkraw, proj, b)


def _s5_expand():
    r = lax.broadcasted_iota(jnp.int32, (S5_STATE, S5_STATE * S5_GROUP_CH), 0)
    c = lax.broadcasted_iota(jnp.int32, (S5_STATE, S5_STATE * S5_GROUP_CH), 1)
    return jnp.where(c // S5_GROUP_CH == r, 1.0, 0.0).astype(F32)


def _s5_disc_math(ar, ai, ldt, br, bi):
    dt = jnp.exp(ldt)
    mag = jnp.exp(ar * dt)
    lr = mag * jnp.cos(ai * dt)
    li = mag * jnp.sin(ai * dt)
    den = ar * ar + ai * ai
    fr = ((lr - 1.0) * ar + li * ai) / den
    fi = (li * ar - (lr - 1.0) * ai) / den
    e = _s5_expand()
    fre = jnp.dot(fr, e, precision=_HI, preferred_element_type=F32)
    fie = jnp.dot(fi, e, precision=_HI, preferred_element_type=F32)
    return lr, li, fre * br - fie * bi, fre * bi + fie * br


def _layer_blocks(arrs):
    return [pl.BlockSpec((None,) + a.shape[1:], lambda l: (l, 0, 0)) for a in arrs]


def _s5_disc(ar, ai, ldt, br, bi):
    def body(ar_ref, ai_ref, ldt_ref, br_ref, bi_ref, lr_ref, li_ref, bbr_ref, bbi_ref):
        lr, li, bbr, bbi = _s5_disc_math(ar_ref[...], ai_ref[...], ldt_ref[...], br_ref[...], bi_ref[...])
        lr_ref[...] = lr
        li_ref[...] = li
        bbr_ref[...] = bbr
        bbi_ref[...] = bbi

    ins = (ar, ai, ldt, br, bi)
    outs = (ar, ai, br, bi)
    return _pallas(body, out_shape=tuple(jax.ShapeDtypeStruct(a.shape, F32) for a in outs), grid=(DEPTH,),
                   in_specs=_layer_blocks(ins), out_specs=tuple(_layer_blocks(outs)), name="s5_disc")(*ins)


def _s5_disc_bwd(ar, ai, ldt, br, bi, dlr, dli, dbbr, dbbi):
    def body(ar_ref, ai_ref, ldt_ref, br_ref, bi_ref, dlr_ref, dli_ref, dbbr_ref, dbbi_ref,
             dar_ref, dai_ref, dldt_ref, dbr_ref, dbi_ref):
        _, vjp = jax.vjp(_s5_disc_math, ar_ref[...], ai_ref[...], ldt_ref[...], br_ref[...], bi_ref[...])
        dar, dai, dldt, dbr, dbi = vjp((dlr_ref[...], dli_ref[...], dbbr_ref[...], dbbi_ref[...]))
        dar_ref[...] = dar
        dai_ref[...] = dai
        dldt_ref[...] = dldt
        dbr_ref[...] = dbr
        dbi_ref[...] = dbi

    ins = (ar, ai, ldt, br, bi, dlr, dli, dbbr, dbbi)
    outs = (ar, ai, ldt, br, bi)
    return _pallas(body, out_shape=tuple(jax.ShapeDtypeStruct(a.shape, F32) for a in outs), grid=(DEPTH,),
                   in_specs=_layer_blocks(ins), out_specs=tuple(_layer_blocks(outs)), name="s5_disc_bwd")(*ins)


SLAB = 2 * S5_CH // 128


def _slab_rows(s, ts):
    return pl.ds(s, ts, stride=SLAB)


def _slab_pair(ref, s, ts):
    return jnp.concatenate([ref[_slab_rows(s, ts), :].astype(_MXU), ref[_slab_rows(s + 1, ts), :].astype(_MXU)], axis=-1)


def _s5_fwd(proj, wb, wc, lam, d, w_glu):
    L = proj.shape[0]
    ts = min(TS, L)

    def body(u_ref, wb_ref, wc_ref, lam_ref, d_ref, wg_ref, xs_ref, ypre_ref, ys_ref, b_sc, c_sc):
        @pl.when(pl.program_id(0) == 0)
        def _():
            c_sc[...] = jnp.zeros((SLAB, 128), F32)

        u = u_ref[...]
        ub = u.astype(_MXU)
        for s in range(0, SLAB, 2):
            b2 = _dot(ub, wb_ref[:, 128 * s:128 * (s + 2)])
            b_sc[_slab_rows(s, ts), :] = b2[:, :128]
            b_sc[_slab_rows(s + 1, ts), :] = b2[:, 128:]
        lr, li = lam_ref[0:8, :], lam_ref[8:16, :]

        def step(t, carry):
            xr, xi = carry
            row = pl.multiple_of(t * SLAB, SLAB)
            nr = lr * xr - li * xi + b_sc[pl.ds(row, 8), :]
            ni = lr * xi + li * xr + b_sc[pl.ds(row + 8, 8), :]
            xs_ref[pl.ds(row, 8), :] = nr
            xs_ref[pl.ds(row + 8, 8), :] = ni
            return nr, ni

        xr, xi = lax.fori_loop(0, ts, step, (c_sc[0:8, :], c_sc[8:16, :]), unroll=8)
        c_sc[0:8, :] = xr
        c_sc[8:16, :] = xi
        y = jnp.zeros((ts, D_S5), F32)
        for s in range(0, SLAB, 2):
            y = y + _dot(_slab_pair(xs_ref, s, ts), wc_ref[128 * s:128 * (s + 2), :])
        ypre_ref[...] = y
        y1 = jax.nn.gelu(y + d_ref[...] * u)
        ys_ref[...] = y1 * jax.nn.sigmoid(_dot(y1, wg_ref[...]))

    row = _rows(ts, D_S5)
    slabs = pl.BlockSpec((ts * SLAB, 128), lambda n: (n, 0))
    (wb, wb_spec), (wc, wc_spec), (lam, lam_spec), (d, d_spec) = (_of_layer(a) for a in (wb, wc, lam, d))
    return _pallas(
        body, out_shape=(jax.ShapeDtypeStruct((L * SLAB, 128), F32), jax.ShapeDtypeStruct((L, D_S5), F32),
                         jax.ShapeDtypeStruct((L, D_S5), F32)),
        grid=(L // ts,),
        in_specs=[_rows(ts, D_S5, O_SU // D_S5), wb_spec, wc_spec, lam_spec, d_spec, _whole((D_S5, D_S5))],
        out_specs=(slabs, row, row),
        scratch_shapes=[pltpu.VMEM((ts * SLAB, 128), F32), pltpu.VMEM((SLAB, 128), F32)], name="s5_fwd")(
            proj, wb, wc, lam, d, w_glu)


def _s5_bwd(proj, ypre, dys, xs, wb, wc, lam, d, w_glu):
    L = proj.shape[0]
    ts = min(TS, L)
    nb = L // ts

    def body(u_ref, y_ref, dys_ref, xs_ref, xp_ref, wb_ref, wc_ref, lam_ref, d_ref, wg_ref,
             du_ref, dwb_ref, dwc_ref, dlam_ref, dd_ref, dwg_ref, dx_sc, g_sc, c_sc):
        n = pl.program_id(0)

        @pl.when(n == 0)
        def _():
            c_sc[...] = jnp.zeros((SLAB, 128), F32)
            dlam_ref[...] = jnp.zeros((SLAB, 128), F32)
            dwb_ref[...] = jnp.zeros((D_S5, 2 * S5_CH), F32)
            dwc_ref[...] = jnp.zeros((2 * S5_CH, D_S5), F32)
            dd_ref[...] = jnp.zeros((1, D_S5), F32)
            dwg_ref[...] = jnp.zeros((D_S5, D_S5), F32)

        u, dv, dout = u_ref[...], d_ref[...], dys_ref[...]
        y1, gelu_vjp = jax.vjp(jax.nn.gelu, y_ref[...] + dv * u)
        sg = jax.nn.sigmoid(_dot(y1, wg_ref[...]))
        dz = dout * y1 * sg * (1.0 - sg)
        dy, = gelu_vjp(dout * sg + _dot(dz, wg_ref[...], _NT))
        dd_ref[...] += jnp.sum(dy * u, axis=0, keepdims=True)
        dwg_ref[...] += _dot(y1, dz, _TN)
        dyb = dy.astype(_MXU)
        for s in range(0, SLAB, 2):
            cols = slice(128 * s, 128 * (s + 2))
            dx2 = _dot(dyb, wc_ref[cols, :], _NT)
            dx_sc[_slab_rows(s, ts), :] = dx2[:, :128]
            dx_sc[_slab_rows(s + 1, ts), :] = dx2[:, 128:]
            dwc_ref[cols, :] += _dot(_slab_pair(xs_ref, s, ts), dyb, _TN)
        lr, li = lam_ref[0:8, :], lam_ref[8:16, :]

        def adjoint(row, pr, pi, carry):
            gr, gi, ar, ai = carry
            nr = dx_sc[pl.ds(row, 8), :] + lr * gr + li * gi
            ni = dx_sc[pl.ds(row + 8, 8), :] - li * gr + lr * gi
            g_sc[pl.ds(row, 8), :] = nr
            g_sc[pl.ds(row + 8, 8), :] = ni
            return nr, ni, ar + nr * pr + ni * pi, ai - nr * pi + ni * pr

        def step(k, carry):
            row = pl.multiple_of((ts - 1 - k) * SLAB, SLAB)
            prev = pl.multiple_of((ts - 2 - k) * SLAB, SLAB)
            return adjoint(row, xs_ref[pl.ds(prev, 8), :], xs_ref[pl.ds(prev + 8, 8), :], carry)

        z = jnp.zeros((8, 128), F32)
        carry = lax.fori_loop(0, ts - 1, step, (c_sc[0:8, :], c_sc[8:16, :], z, z), unroll=8)
        has_prev = jnp.where(n == nb - 1, 0.0, 1.0)
        gr, gi, ar, ai = adjoint(0, xp_ref[0:8, :] * has_prev, xp_ref[8:16, :] * has_prev, carry)
        c_sc[0:8, :] = gr
        c_sc[8:16, :] = gi
        dlam_ref[0:8, :] += ar
        dlam_ref[8:16, :] += ai
        ub = u.astype(_MXU)
        du = dy * dv
        for s in range(0, SLAB, 2):
            cols = slice(128 * s, 128 * (s + 2))
            gs = _slab_pair(g_sc, s, ts)
            du = du + _dot(gs, wb_ref[:, cols], _NT)
            dwb_ref[:, cols] += _dot(ub, gs, _TN)
        du_ref[...] = du

    blk = lambda n: nb - 1 - n
    row = pl.BlockSpec((ts, D_S5), lambda n: (blk(n), 0))
    (wb, wb_spec), (wc, wc_spec), (lam, lam_spec), (d, d_spec) = (_of_layer(a) for a in (wb, wc, lam, d))
    return _pallas(
        body, out_shape=(jax.ShapeDtypeStruct((L, D_S5), F32), jax.ShapeDtypeStruct((D_S5, 2 * S5_CH), F32),
                         jax.ShapeDtypeStruct((2 * S5_CH, D_S5), F32), jax.ShapeDtypeStruct((SLAB, 128), F32),
                         jax.ShapeDtypeStruct((1, D_S5), F32), jax.ShapeDtypeStruct((D_S5, D_S5), F32)),
        grid=(nb,),
        in_specs=[pl.BlockSpec((ts, D_S5), lambda n: (blk(n), O_SU // D_S5)), row, row,
                  pl.BlockSpec((ts * SLAB, 128), lambda n: (blk(n), 0)),
                  pl.BlockSpec((SLAB, 128), lambda n: (jnp.maximum(blk(n) * ts - 1, 0), 0)),
                  wb_spec, wc_spec, lam_spec, d_spec, _whole((D_S5, D_S5))],
        out_specs=(row, _whole((D_S5, 2 * S5_CH)), _whole((2 * S5_CH, D_S5)), _whole((SLAB, 128)), _whole((1, D_S5)),
                   _whole((D_S5, D_S5))),
        scratch_shapes=[pltpu.VMEM((ts * SLAB, 128), F32), pltpu.VMEM((ts * SLAB, 128), F32), pltpu.VMEM((SLAB, 128), F32)],
        name="s5_bwd")(proj, ypre, dys, xs, xs, wb, wc, lam, d, w_glu)


def _rot(z, cos, sin):
    lane = lax.broadcasted_iota(jnp.int32, z.shape, 1)
    zs = z * sin
    half = HEAD_DIM // 2
    return z * cos + jnp.where(lane % HEAD_DIM < half, pltpu.roll(zs, PAIR - half, 1), pltpu.roll(zs, half, 1))


def _head_avg():
    r = lax.broadcasted_iota(jnp.int32, (PAIR, PAIR), 0) // HEAD_DIM
    c = lax.broadcasted_iota(jnp.int32, (PAIR, PAIR), 1) // HEAD_DIM
    return jnp.where(r == c, 1.0 / HEAD_DIM, 0.0).astype(F32)


def _ret_tables(tq):
    lg = jnp.log1p(-(2.0 ** (-5.0 - jnp.arange(RET_HEADS, dtype=F32))))
    scale = 1.0 / math.sqrt(HEAD_DIM)
    pos = jnp.arange(tq)
    n = pos.astype(F32)
    dist = jnp.abs(n[:, None] - n[None, :])
    ok = (pos[None, :] // CHUNK) <= (pos[:, None] // CHUNK)
    w = jnp.where(ok[None], scale * jnp.exp(lg[:, None, None] * dist[None]), 0.0)
    lgl = jnp.repeat(lg, HEAD_DIM)
    dq_tab = scale * jnp.exp(lgl[None, :] * (n[:, None] + 1.0))
    dk_tab = jnp.exp(lgl[None, :] * (tq - 1.0 - n[:, None]))
    blk = jnp.arange(PAIR) // HEAD_DIM
    bd = (blk[:, None] == blk[None, :]).astype(F32)
    gbd = bd[None] * jnp.exp(lgl.reshape(RET_HEADS // 2, PAIR)[:, :, None] * tq)
    return dict(w=w, wt=w.transpose(0, 2, 1), dq=dq_tab, dk=dk_tab, gbd=gbd, bd=bd)


def _ret_specs(tq, nq, rev, layer):
    blk = (lambda i: nq - 1 - i) if rev else (lambda i: i)
    col = lambda o: pl.BlockSpec((tq, PAIR), lambda p, i: (blk(i), o // PAIR + p))
    return dict(
        rq=col(O_RQ), rk=col(O_RK), rv=col(O_RV), nat=col(0),
        w=pl.BlockSpec((2, tq, tq), lambda p, i: (p, 0, 0)), tab=pl.BlockSpec((tq, PAIR), lambda p, i: (0, p)),
        gbd=pl.BlockSpec((None, PAIR, PAIR), lambda p, i: (p, 0, 0)), bd=pl.BlockSpec((PAIR, PAIR), lambda p, i: (0, 0)),
        gn=pl.BlockSpec((None, 1, PAIR), lambda p, i: (layer, 0, p)), dgn=pl.BlockSpec((1, PAIR), lambda p, i: (0, p)),
        st=pl.BlockSpec((None, None, PAIR, PAIR), lambda p, i: (p, blk(i), 0, 0)))


def _ret_fwd(proj, cos_t, sin_t, tabs, gn):
    L = proj.shape[0]
    tq = tabs["w"].shape[1]
    nq = L // tq

    def body(rq_ref, rk_ref, rv_ref, cos_ref, sin_ref, w_ref, dqt_ref, dkt_ref, gbd_ref, bd_ref, gn_ref,
             o_ref, y_ref, st_ref, s_sc):
        @pl.when(pl.program_id(1) == 0)
        def _():
            s_sc[...] = jnp.zeros((PAIR, PAIR), F32)

        state = s_sc[...]
        st_ref[...] = state
        cos, sin = cos_ref[...], sin_ref[...]
        q2, k2, v2 = _rot(rq_ref[...], cos, sin), _rot(rk_ref[...], cos, sin), rv_ref[...]
        o = _dot(q2 * dqt_ref[...], state)
        for h in range(2):
            own = _own((tq, PAIR), h)
            a = _dot(jnp.where(own, q2, 0.0), k2, _NT) * w_ref[h]
            o = o + _dot(a, jnp.where(own, v2, 0.0))
        s_sc[...] = gbd_ref[...] * state + bd_ref[...] * _dot(k2 * dkt_ref[...], v2, _TN)
        o_ref[...] = o
        avg = _head_avg()
        oc = o - _hi_dot(o, avg)
        y_ref[...] = oc * lax.rsqrt(_hi_dot(oc * oc, avg) + EPS) * gn_ref[...]

    gn, layer = gn
    sp = _ret_specs(tq, nq, False, layer)
    nat = jax.ShapeDtypeStruct((L, D_RET), F32)
    return _pallas(
        body, out_shape=(nat, nat, jax.ShapeDtypeStruct((RET_HEADS // 2, nq, PAIR, PAIR), F32)), grid=(RET_HEADS // 2, nq),
        in_specs=[sp["rq"], sp["rk"], sp["rv"], sp["nat"], sp["nat"], sp["w"], sp["tab"], sp["tab"], sp["gbd"], sp["bd"],
                  sp["gn"]],
        out_specs=(sp["nat"], sp["nat"], sp["st"]), scratch_shapes=[pltpu.VMEM((PAIR, PAIR), F32)],
        name="ret_fwd")(proj, proj, proj, cos_t, sin_t, tabs["w"], tabs["dq"], tabs["dk"], tabs["gbd"], tabs["bd"], gn)


def _ret_bwd(proj, cos_t, sin_t, tabs, gn, o_pre, dy, states):
    L = proj.shape[0]
    tq = tabs["w"].shape[1]
    nq = L // tq

    def body(rq_ref, rk_ref, rv_ref, cos_ref, sin_ref, w_ref, wt_ref, dqt_ref, dkt_ref, gbd_ref, bd_ref, gn_ref,
             o_ref, dy_ref, st_ref, drq_ref, drk_ref, drv_ref, dgn_ref, g_sc):
        first = pl.program_id(1) == 0

        @pl.when(first)
        def _():
            g_sc[...] = jnp.zeros((PAIR, PAIR), F32)

        cos, sin = cos_ref[...], sin_ref[...]
        q2, k2, v2 = _rot(rq_ref[...], cos, sin), _rot(rk_ref[...], cos, sin), rv_ref[...]
        avg = _head_avg()
        ov, dyv = o_ref[...], dy_ref[...]
        oc = ov - _hi_dot(ov, avg)
        r = lax.rsqrt(_hi_dot(oc * oc, avg) + EPS)
        oh = oc * r
        dgn = jnp.sum(dyv * oh, axis=0, keepdims=True)
        doh = dyv * gn_ref[...]
        do = r * (doh - _hi_dot(doh, avg) - oh * _hi_dot(doh * oh, avg))
        state, g = st_ref[...], g_sc[...]
        dqt, dkt = dqt_ref[...], dkt_ref[...]
        dq = _dot(do, state, _NT) * dqt
        dk = _dot(v2, g, _NT) * dkt
        dv = _dot(k2 * dkt, g)
        g_sc[...] = gbd_ref[...] * g + bd_ref[...] * _dot(q2 * dqt, do, _TN)
        for h in range(2):
            own = _own((tq, PAIR), h)
            qm, dom = jnp.where(own, q2, 0.0), jnp.where(own, do, 0.0)
            dv = dv + _dot(_dot(k2, qm, _NT) * wt_ref[h], dom)
            dq = dq + _dot(_dot(dom, v2, _NT) * w_ref[h], jnp.where(own, k2, 0.0))
            dk = dk + _dot(_dot(v2, dom, _NT) * wt_ref[h], qm)
        drq_ref[...] = _rot(dq, cos, -sin)
        drk_ref[...] = _rot(dk, cos, -sin)
        drv_ref[...] = dv

        @pl.when(first)
        def _():
            dgn_ref[...] = dgn

        @pl.when(jnp.logical_not(first))
        def _():
            dgn_ref[...] += dgn

    gn, layer = gn
    sp = _ret_specs(tq, nq, True, layer)
    nat = jax.ShapeDtypeStruct((L, D_RET), F32)
    return _pallas(
        body, out_shape=(nat, nat, nat, jax.ShapeDtypeStruct((1, D_RET), F32)), grid=(RET_HEADS // 2, nq),
        in_specs=[sp["rq"], sp["rk"], sp["rv"], sp["nat"], sp["nat"], sp["w"], sp["w"], sp["tab"], sp["tab"], sp["gbd"],
                  sp["bd"], sp["gn"], sp["nat"], sp["nat"], sp["st"]],
        out_specs=(sp["nat"], sp["nat"], sp["nat"], sp["dgn"]), scratch_shapes=[pltpu.VMEM((PAIR, PAIR), F32)],
        name="ret_bwd")(proj, proj, proj, cos_t, sin_t, tabs["w"], tabs["wt"], tabs["dq"], tabs["dk"], tabs["gbd"],
                        tabs["bd"], gn, o_pre, dy, states)


def _gate_out(yf, ys, yr, proj, x, w):
    L = x.shape[0]

    def body(yf_ref, ys_ref, yr_ref, g_ref, x_ref, w_ref, xn_ref):
        cat = jnp.concatenate([yf_ref[...], ys_ref[...], yr_ref[...]], axis=-1)
        xn_ref[...] = x_ref[...] + _dot(cat * jax.nn.silu(g_ref[...]), w_ref[...])

    tm = min(TMB, L)
    full = _rows(tm, D_MODEL)
    return _pallas(body, out_shape=jax.ShapeDtypeStruct((L, D_MODEL), F32), grid=(L // tm,),
                   in_specs=[_rows(tm, D_FOX), _rows(tm, D_S5), _rows(tm, D_RET), _rows(tm, D_MODEL, O_GATE // D_MODEL),
                             full, _whole((D_MODEL, D_MODEL))],
                   out_specs=full, name="gate_out")(yf, ys, yr, proj, x, w)


def _gate_out_bwd(dxn, w, yf, ys, yr, proj):
    L = dxn.shape[0]

    def body(dx_ref, w_ref, yf_ref, ys_ref, yr_ref, g_ref, dyf_ref, dys_ref, dyr_ref, dg_ref, dw_ref):
        dxv = dx_ref[...].astype(_MXU)
        dy = _dot(dxv, w_ref[...], _NT)
        g = g_ref[...]
        sg = jax.nn.sigmoid(g)
        silu = g * sg
        dcat = dy * silu
        dyf_ref[...] = dcat[:, :D_FOX]
        dys_ref[...] = dcat[:, D_FOX:D_FOX + D_S5]
        dyr_ref[...] = dcat[:, D_FOX + D_S5:]
        cat = jnp.concatenate([yf_ref[...], ys_ref[...], yr_ref[...]], axis=-1)
        dg_ref[...] = dy * cat * (sg * (1.0 + g * (1.0 - sg)))
        dw = _dot(cat * silu, dxv, _TN)

        @pl.when(pl.program_id(0) == 0)
        def _():
            dw_ref[...] = dw

        @pl.when(pl.program_id(0) != 0)
        def _():
            dw_ref[...] += dw

    tm = min(TMB, L)
    full = _rows(tm, D_MODEL)
    f, s, r = _rows(tm, D_FOX), _rows(tm, D_S5), _rows(tm, D_RET)
    sq = _whole((D_MODEL, D_MODEL))
    return _pallas(body, out_shape=(jax.ShapeDtypeStruct((L, D_FOX), F32), jax.ShapeDtypeStruct((L, D_S5), F32),
                                    jax.ShapeDtypeStruct((L, D_RET), F32), jax.ShapeDtypeStruct((L, D_MODEL), F32),
                                    jax.ShapeDtypeStruct((D_MODEL, D_MODEL), F32)),
                   grid=(L // tm,), in_specs=[full, sq, f, s, r, _rows(tm, D_MODEL, O_GATE // D_MODEL)],
                   out_specs=(f, s, r, full, sq), name="gate_out_bwd",
                   compiler_params=pltpu.CompilerParams(vmem_limit_bytes=VMEM_BIG))(dxn, w, yf, ys, yr, proj)


def _final_loss(x, g, tgt):
    L = x.shape[0]

    def body(x_ref, g_ref, t_ref, loss_ref, dx_ref, dg_ref):
        xv, gv = x_ref[...], g_ref[...]
        r = lax.rsqrt(jnp.mean(xv * xv, axis=-1, keepdims=True) + EPS)
        err = xv * r * gv - t_ref[...]
        part = 0.5 * jnp.sum(jnp.mean(err * err, axis=-1, keepdims=True), axis=0, keepdims=True)
        dx, dg = _rms_bwd(xv, gv, err * (1.0 / D_MODEL))
        dx_ref[...] = dx

        @pl.when(pl.program_id(0) == 0)
        def _():
            loss_ref[...] = part
            dg_ref[...] = dg

        @pl.when(pl.program_id(0) != 0)
        def _():
            loss_ref[...] += part
            dg_ref[...] += dg

    full = _rows(TM, D_MODEL)
    return _pallas(body, out_shape=(jax.ShapeDtypeStruct((1, 1), F32), jax.ShapeDtypeStruct((L, D_MODEL), F32),
                                    jax.ShapeDtypeStruct((1, D_MODEL), F32)),
                   grid=(L // TM,), in_specs=[full, _whole((1, D_MODEL)), full],
                   out_specs=(_whole((1, 1)), full, _whole((1, D_MODEL))), name="final_loss")(x, g, tgt)


def _block_diag(blocks):
    n, g, r, c = blocks.shape
    eye = jnp.eye(g, dtype=blocks.dtype)
    return (blocks[:, :, :, None, :] * eye[None, :, None, :, None]).reshape(n, g * r, g * c)


def _diag_blocks(m, g):
    n, r, c = m.shape[0], m.shape[1] // g, m.shape[2] // g
    eye = jnp.eye(g, dtype=m.dtype)
    return jnp.sum(m.reshape(n, g, r, g, c) * eye[None, :, None, :, None], axis=3)


def _rope_tables(L):
    half = HEAD_DIM // 2
    freqs = ROPE_BASE ** (-jnp.arange(half, dtype=F32) / half)
    ang = jnp.arange(L, dtype=F32)[:, None] * freqs[None, :]
    cos, sin = jnp.cos(ang), jnp.sin(ang)
    cos_t = jnp.tile(jnp.concatenate([cos, cos], axis=-1), (1, RET_HEADS))
    sin_t = jnp.tile(jnp.concatenate([sin, -sin], axis=-1), (1, RET_HEADS))
    return cos_t, sin_t


def _s5_disc_args(small):
    g, s, ch = S5_GROUPS, S5_STATE, S5_GROUP_CH
    return (small["s5_a_re"], small["s5_a_im"], small["s5_log_dt"][:, :, None],
            small["s5_b_re"].reshape(DEPTH, g, s * ch), small["s5_b_im"].reshape(DEPTH, g, s * ch))


def _s5_mats(small):
    g, s, ch = S5_GROUPS, S5_STATE, S5_GROUP_CH
    lr, li, bbr, bbi = _s5_disc(*_s5_disc_args(small))
    lam = jnp.concatenate([lr.reshape(DEPTH, 8, 128), li.reshape(DEPTH, 8, 128)], axis=1)
    wb = jnp.concatenate([_block_diag(b.reshape(DEPTH, g, s, ch).transpose(0, 1, 3, 2)) for b in (bbr, bbi)], axis=2)
    wc = jnp.concatenate([_block_diag(c.transpose(0, 1, 3, 2)) for c in (small["s5_c_re"], -small["s5_c_im"])], axis=1)
    return lam, wb.astype(_MXU), wc.astype(_MXU)


def _s5_param_grads(small, dwb, dwc, dlam):
    g, s, ch = S5_GROUPS, S5_STATE, S5_GROUP_CH
    dc = [_diag_blocks(m, g).transpose(0, 1, 3, 2) for m in (dwc[:, :S5_CH], dwc[:, S5_CH:])]
    dbb = [_diag_blocks(m, g).transpose(0, 1, 3, 2).reshape(DEPTH, g, s * ch) for m in (dwb[:, :, :S5_CH], dwb[:, :, S5_CH:])]
    dar, dai, dldt, dbr, dbi = _s5_disc_bwd(*_s5_disc_args(small), dlam[:, :8].reshape(DEPTH, g, s),
                                            dlam[:, 8:].reshape(DEPTH, g, s), dbb[0], dbb[1])
    shp = (DEPTH, g, s, ch)
    return dict(s5_a_re=dar, s5_a_im=dai, s5_log_dt=dldt.reshape(DEPTH, g), s5_b_re=dbr.reshape(shp),
                s5_b_im=dbi.reshape(shp), s5_c_re=dc[0], s5_c_im=-dc[1])


def _layer_fwd(x, p, rope, ride=None, late=False):
    L = x.shape[0]
    cos_t, sin_t, ret_tabs = rope
    s = {"x": x}
    proj, h = _norm_inproj(x, p["norm_w"], p["w_in"])
    s["proj"], s["h"] = proj, h
    qa, ka, kat, vt = _fox_prep(proj, _fox_cumsum(proj, p["b_f"]))
    yf, lse, landed = _fox_fwd(qa, ka, vt, ride)
    s.update(qa=qa, ka=ka, kat=kat, lse=lse, yf=yf)
    if late:
        p["w_glu"], p["w_out"] = _gathered_rows(landed[-2]), _gathered_rows(landed[-1])
        landed = landed[:-2]
    xs, ypre, ys = _s5_fwd(proj, p["wb"], p["wc"], p["lam"], p["d"], p["w_glu"])
    s.update(xs=xs, ypre=ypre, ys=ys)
    o_pre, yr, states = _ret_fwd(proj, cos_t, sin_t, ret_tabs, p["gn_w"])
    s.update(o_pre=o_pre, yr=yr, states=states)
    return _gate_out(yf, ys, yr, proj, x, p["w_out"]), s, landed


def _layer_bwd(dxn, s, p, rope, ride=None, early=False):
    L = dxn.shape[0]
    cos_t, sin_t, ret_tabs = rope
    g = {}
    proj = s["proj"]
    dyf, dys, dyr, dgate, g["w_out"] = _gate_out_bwd(dxn, p["w_out"], s["yf"], s["ys"], s["yr"], proj)
    drq, drk, drv, dgn = _ret_bwd(proj, cos_t, sin_t, ret_tabs, p["gn_w"], s["o_pre"], dyr, s["states"])
    g["ret_gn_w"] = dgn.reshape(D_RET)
    dsu, g["wb"], g["wc"], g["lam"], dd, g["s5_w_glu"] = _s5_bwd(proj, s["ypre"], dys, s["xs"], p["wb"], p["wc"], p["lam"],
                                                                 p["d"], p["w_glu"])
    g["s5_d"] = dd.reshape(D_S5)
    if early:
        ride = (ride[0] + _row_slots(g), ride[1] + [True, True])
    dqt, dkraw, dv, landed = _fox_bwd(s["qa"], s["ka"], s["kat"], proj, dyf, s["yf"], s["lse"], ride)
    dq, dk, dfl, dbf = _fox_post_bwd(dqt, dkraw, proj, p["b_f"])
    g["fox_b_f"] = dbf[0, :FOX_HEADS]
    pieces = [dgate, dq, dk, dv, dsu, drq, drk, drv, dfl]
    dx, dnw, g["w_in"] = _inproj_bwd(pieces, p["w_in"], s["x"], p["norm_w"], dxn, s["h"])
    g["norm_w"] = dnw.reshape(D_MODEL)
    return dx, g, landed


def _stacked_params(small):
    lam, wb, wc = _s5_mats(small)
    row = lambda a: a[:, None, :]
    return dict(norm_w=row(small["norm_w"]), b_f=row(jnp.pad(small["fox_b_f"], ((0, 0), (0, PAIR - FOX_HEADS)))),
                lam=lam, wb=wb, wc=wc, d=row(small["s5_d"]), gn_w=row(small["ret_gn_w"]))


def _layer_params(l, w_in_p, w_glu, w_out, stacked):
    return dict({k: (a, l) for k, a in stacked.items()}, w_in=w_in_p, w_glu=w_glu, w_out=w_out)


_SHARDED = ("w_in", "s5_w_glu", "w_out")
_WIRE = jnp.bfloat16


_RUNS = ((2568, 3592, O_GATE), (0, 1536, O_FQ), (1544, 2568, O_SU), (1536, 1544, O_FL))


def _shard_pieces():
    out = []
    for a, b, pad in _RUNS:
        while a < b:
            j = a // W_SHARD
            e = min(b, (j + 1) * W_SHARD)
            out.append((j, a - j * W_SHARD, e - j * W_SHARD, pad))
            pad, a = pad + e - a, e
    return out


def _gathered_w_in(g_in):
    cols = [g_in[j, :, a:e] for j, a, e, _ in _shard_pieces()]
    cols.append(jnp.zeros((D_MODEL, D_INP - O_FL - FOX_HEADS), g_in.dtype))
    return jnp.concatenate(cols, axis=1)


def _gathered_rows(g):
    return g.reshape(-1, g.shape[-1])


def _w_in_slots(g):
    w_in = g["w_in"].astype(_WIRE)
    slots = []
    for j in range(N_DEV):
        mine = sorted((a, e, pad) for jj, a, e, pad in _shard_pieces() if jj == j)
        slots.append(jnp.concatenate([w_in[:, pad:pad + e - a] for a, e, pad in mine], axis=1))
    return jnp.stack(slots)


def _row_slots(g):
    return [g["s5_w_glu"].reshape(N_DEV, D_S5 // N_DEV, D_S5).astype(_WIRE),
            g["w_out"].reshape(N_DEV, D_MODEL // N_DEV, D_MODEL).astype(_WIRE)]


def _step_grads(x, tgt, small, full=None, shards=None):
    L = x.shape[0]
    rope = _rope_tables(L) + (_ret_tables(min(TQ, L)),)
    stacked = _stacked_params(small)
    if shards is not None:
        nxt = (_gathered_w_in(_exchange([shards[0][0]], [False], "gather_layer0")[0]), None, None)
    saved, params = [], []
    for l in range(DEPTH):
        weights = nxt if shards is not None else tuple(f[l] for f in full)
        ride = None
        if shards is not None:
            arrs = [s[l + 1] for s in shards] if l + 1 < DEPTH else []
            arrs += [shards[1][0], shards[2][0]] if l == 0 else []
            ride = (arrs, [False] * len(arrs)) if arrs else None
        params.append(_layer_params(l, *weights, stacked))
        x, s, landed = _layer_fwd(x, params[l], rope, ride, late=shards is not None and l == 0)
        if landed:
            nxt = (_gathered_w_in(landed[0]), _gathered_rows(landed[1]), _gathered_rows(landed[2]))
        saved.append(s)
    loss, dx, dfw = _final_loss(x, small["final_norm_w"][None], tgt)
    grads, partials, waiting = [None] * DEPTH, [None] * DEPTH, None
    for l in reversed(range(DEPTH)):
        ride = (waiting, [True] * len(waiting)) if waiting is not None else None
        dx, grads[l], landed = _layer_bwd(dx, saved[l], params[l], rope, ride, early=ride is not None and l == 0)
        if waiting is not None:
            partials[l + 1] = landed[:3]
        if shards is not None:
            waiting = [_w_in_slots(grads[l])] + _row_slots(grads[l])
    stack = lambda n: jnp.stack([g[n] for g in grads])
    small_g = {n: stack(n) for n in ("norm_w", "fox_b_f", "s5_d", "ret_gn_w")}
    small_g.update(_s5_param_grads(small, stack("wb"), stack("wc"), stack("lam")), final_norm_w=dfw.reshape(D_MODEL))
    if shards is None:
        return loss, dx, grads, small_g
    packed = _pack([small_g[n] for n in _SMALL]).astype(_WIRE)
    last = _exchange([waiting[0], packed], [True, False], "exchange_layer0")
    partials[0] = [last[0]] + landed[3:]
    return loss, dx, grads, small_g, partials, last[1]


_MESH = pl.DeviceIdType.MESH
_ANY = pl.BlockSpec(memory_space=pl.ANY)


def _me_and_peers():
    x, y, c = lax.axis_index("x"), lax.axis_index("y"), lax.axis_index("c")
    flip = lambda a, bit: (1 - a) if bit else a
    peers = []
    for r in range(1, N_DEV):
        px, py, pc = flip(x, (r >> 2) & 1), flip(y, (r >> 1) & 1), flip(c, r & 1)
        peers.append(((px, py, pc), 4 * px + 2 * py + pc))
    return 4 * x + 2 * y + c, peers


def _exchange_copies(srcs, dsts, sems, scatter):
    send_sems, recv_sems, local_sems = sems
    me, peers = _me_and_peers()
    pick = lambda t, to: srcs[t].at[to] if scatter[t] else srcs[t]
    own = [pltpu.make_async_copy(pick(t, me), dsts[t].at[me], local_sems.at[t]) for t in range(len(srcs))]
    sends, waits = [], []
    for r, (dev, idx) in enumerate(peers):
        for t in range(len(srcs)):
            for land, out in ((me, sends), (idx, waits)):
                out.append(pltpu.make_async_remote_copy(pick(t, idx), dsts[t].at[land], send_sems.at[t, r], recv_sems.at[t, r],
                                                        device_id=dev, device_id_type=_MESH))
    return own, sends, waits


def _exchange_start(srcs, dsts, sems, scatter):
    own, sends, _ = _exchange_copies(srcs, dsts, sems, scatter)
    for cp in own + sends:
        cp.start()


def _exchange_wait(srcs, dsts, sems, scatter):
    own, _, waits = _exchange_copies(srcs, dsts, sems, scatter)
    for cp in waits + own:
        cp.wait()


def _exchange_shapes(arrs, scatter):
    outs = [jax.ShapeDtypeStruct(a.shape if sc else (N_DEV,) + a.shape, a.dtype) for a, sc in zip(arrs, scatter)]
    n = len(arrs)
    sems = [pltpu.SemaphoreType.DMA((n, N_DEV - 1)), pltpu.SemaphoreType.DMA((n, N_DEV - 1)), pltpu.SemaphoreType.DMA((n,))]
    return outs, sems


def _exchange(arrs, scatter, name):
    n = len(arrs)

    def body(*refs):
        _exchange_start(refs[:n], refs[n:2 * n], refs[2 * n:], scatter)
        _exchange_wait(refs[:n], refs[n:2 * n], refs[2 * n:], scatter)

    outs, sems = _exchange_shapes(arrs, scatter)
    return _pallas(body, out_shape=tuple(outs), in_specs=[_ANY] * n, out_specs=tuple([_ANY] * n), scratch_shapes=sems,
                   name=name)(*arrs)


def _riding(body, n_in, n_out, ride, is_first, is_last):
    if ride is None:
        return body, [], [], [], []
    arrs, scatter = ride
    n = len(arrs)
    outs, sems = _exchange_shapes(arrs, scatter)

    def wrapped(*refs):
        ins, srcs = refs[:n_in], refs[n_in:n_in + n]
        own_outs, dsts = refs[n_in + n:n_in + n + n_out], refs[n_in + n + n_out:n_in + 2 * n + n_out]
        scratch, ex_sems = refs[n_in + 2 * n + n_out:-3], refs[-3:]

        @pl.when(is_first())
        def _():
            _exchange_start(srcs, dsts, ex_sems, scatter)

        body(*ins, *own_outs, *scratch)

        @pl.when(is_last())
        def _():
            _exchange_wait(srcs, dsts, ex_sems, scatter)

    return wrapped, list(arrs), [_ANY] * n, outs, sems


def _adamw(parts, w, m, v, name):
    n, nb, rows, cols = parts.shape
    tm = next(t for t in (256, 128, 64, 32, 16) if rows % t == 0)

    def body(p_ref, w_ref, m_ref, v_ref, g_ref, d_ref, nm_ref, nv_ref):
        g = p_ref[0].astype(F32)
        for i in range(1, n):
            g = g + p_ref[i].astype(F32)
        nm = ADAM_B1 * m_ref[...] + (1.0 - ADAM_B1) * g
        nv = ADAM_B2 * v_ref[...] + (1.0 - ADAM_B2) * jnp.square(g)
        m_hat = nm / (1.0 - ADAM_B1 ** ADAM_STEP)
        v_hat = nv / (1.0 - ADAM_B2 ** ADAM_STEP)
        g_ref[...] = g
        d_ref[...] = -ADAM_LR * (m_hat / (jnp.sqrt(v_hat) + ADAM_EPS) + ADAM_WD * w_ref[...])
        nm_ref[...] = nm
        nv_ref[...] = nv

    row = pl.BlockSpec((None, tm, cols), lambda b, i: (b, i, 0))
    return _pallas(body, out_shape=(jax.ShapeDtypeStruct((nb, rows, cols), F32),) * 4, grid=(nb, rows // tm),
                   in_specs=[pl.BlockSpec((n, None, tm, cols), lambda b, i: (0, b, i, 0)), row, row, row],
                   out_specs=(row,) * 4, name=name)(parts, w, m, v)


_WEIGHTS = ("norm_w", "w_in", "fox_b_f", "s5_a_re", "s5_a_im", "s5_b_re", "s5_b_im", "s5_c_re", "s5_c_im", "s5_d",
            "s5_log_dt", "s5_w_glu", "ret_gn_w", "w_out", "final_norm_w")
_SMALL = tuple(n for n in _WEIGHTS if n not in _SHARDED)
_LANES = 128


def _pack(arrs):
    flat = jnp.concatenate([a.reshape(-1) for a in arrs])
    rows = -(-flat.shape[0] // (_LANES * _LANES)) * _LANES
    return jnp.pad(flat, (0, rows * _LANES - flat.shape[0])).reshape(rows, _LANES)


def _unpack(packed, like):
    flat, out, off = packed.reshape(-1), [], 0
    for a in like:
        out.append(flat[off:off + a.size].reshape(a.shape))
        off += a.size
    return out


def kernel(x, norm_w, w_in, fox_b_f, s5_a_re, s5_a_im, s5_b_re, s5_b_im, s5_c_re, s5_c_im, s5_d, s5_log_dt, s5_w_glu, ret_gn_w, w_out, final_norm_w, loss_target, m_norm_w, m_w_in, m_fox_b_f, m_s5_a_re, m_s5_a_im, m_s5_b_re, m_s5_b_im, m_s5_c_re, m_s5_c_im, m_s5_d, m_s5_log_dt, m_s5_w_glu, m_ret_gn_w, m_w_out, m_final_norm_w, v_norm_w, v_w_in, v_fox_b_f, v_s5_a_re, v_s5_a_im, v_s5_b_re, v_s5_b_im, v_s5_c_re, v_s5_c_im, v_s5_d, v_s5_log_dt, v_s5_w_glu, v_ret_gn_w, v_w_out, v_final_norm_w):
    w = dict(norm_w=norm_w, w_in=w_in, fox_b_f=fox_b_f, s5_a_re=s5_a_re, s5_a_im=s5_a_im, s5_b_re=s5_b_re, s5_b_im=s5_b_im,
             s5_c_re=s5_c_re, s5_c_im=s5_c_im, s5_d=s5_d, s5_log_dt=s5_log_dt, s5_w_glu=s5_w_glu, ret_gn_w=ret_gn_w,
             w_out=w_out, final_norm_w=final_norm_w)
    m = dict(norm_w=m_norm_w, w_in=m_w_in, fox_b_f=m_fox_b_f, s5_a_re=m_s5_a_re, s5_a_im=m_s5_a_im, s5_b_re=m_s5_b_re,
             s5_b_im=m_s5_b_im, s5_c_re=m_s5_c_re, s5_c_im=m_s5_c_im, s5_d=m_s5_d, s5_log_dt=m_s5_log_dt,
             s5_w_glu=m_s5_w_glu, ret_gn_w=m_ret_gn_w, w_out=m_w_out, final_norm_w=m_final_norm_w)
    v = dict(norm_w=v_norm_w, w_in=v_w_in, fox_b_f=v_fox_b_f, s5_a_re=v_s5_a_re, s5_a_im=v_s5_a_im, s5_b_re=v_s5_b_re,
             s5_b_im=v_s5_b_im, s5_c_re=v_s5_c_re, s5_c_im=v_s5_c_im, s5_d=v_s5_d, s5_log_dt=v_s5_log_dt,
             s5_w_glu=v_s5_w_glu, ret_gn_w=v_ret_gn_w, w_out=v_w_out, final_norm_w=v_final_norm_w)

    small = {n: w[n] for n in _SMALL}
    loss, dx, _, _, partials, r_small = _step_grads(x[0], loss_target[0], small, shards=[w[n].astype(_MXU) for n in _SHARDED])

    res = {}
    for t, n in enumerate(_SHARDED):
        res[n] = _adamw(jnp.stack([partials[l][t] for l in range(DEPTH)], axis=1), w[n], m[n], v[n], "adamw_" + n)
    small_w = [w[n] for n in _SMALL]
    outs = _adamw(r_small[:, None], *[_pack([d[n] for n in _SMALL])[None] for d in (w, m, v)], "adamw_small")
    for k, o in enumerate(outs):
        for n, a in zip(_SMALL, _unpack(o, small_w)):
            res.setdefault(n, [None] * 4)[k] = a

    loss = lax.psum(loss[0, 0], ("x", "y", "c"))
    return (loss, dx[None], *[res[n][0] for n in _WEIGHTS], *[res[n][1] for n in _WEIGHTS],
            *[res[n][2] for n in _WEIGHTS], *[res[n][3] for n in _WEIGHTS])
```

```python
import math

import jax
import jax.numpy as jnp
from jax import lax
from jax.experimental import pallas as pl
from jax.experimental.pallas import tpu as pltpu

F32 = jnp.float32
_MXU = jnp.bfloat16
_HI = lax.Precision.HIGHEST

N_DEV = 8
DEPTH = 4
D_MODEL = 1024
HEAD_DIM = 64
D_FOX = 512
FOX_HEADS = 8
D_S5 = 256
S5_GROUPS = 16
S5_GROUP_CH = 16
S5_STATE = 64
S5_CH = S5_GROUPS * S5_STATE
D_RET = 256
RET_HEADS = 4
CHUNK = 64
ROPE_BASE = 10000.0
EPS = 1e-6
D_IN = 3592
D_INP = 3712
W_SHARD = D_IN // N_DEV
O_GATE, O_FQ, O_FK, O_FV, O_SU, O_RQ, O_RK, O_RV, O_FL = 0, 1024, 1536, 2048, 2560, 2816, 3072, 3328, 3584

ADAM_LR, ADAM_B1, ADAM_B2, ADAM_EPS, ADAM_WD, ADAM_STEP = 0.001, 0.9, 0.999, 1e-08, 0.01, 10

TM = 256
TMB = 512
TQ = 512
TS = 512
NEG = -1e30
VMEM_BIG = 56 * 1024 * 1024


def _pallas(body, **kw):
    return pl.pallas_call(body, **kw)


def _whole(shape):
    n = len(shape)
    return pl.BlockSpec(shape, lambda *_: (0,) * n)


def _rows(tm, width, col=0):
    return pl.BlockSpec((tm, width), lambda i: (i, col))


def _of_layer(param):
    a, l = param
    return a, pl.BlockSpec((None,) + a.shape[1:], lambda *_: (l,) + (0,) * (a.ndim - 1))


def _dot(a, b, dims=(((1,), (0,)), ((), ()))):
    return lax.dot_general(a.astype(_MXU), b.astype(_MXU), dims, preferred_element_type=F32)


_NT = (((1,), (1,)), ((), ()))
_TN = (((0,), (0,)), ((), ()))


def _norm_inproj(x, g, w):
    L = x.shape[0]

    def body(x_ref, g_ref, w_ref, p_ref, h_ref):
        xv = x_ref[...]
        r = lax.rsqrt(jnp.mean(xv * xv, axis=-1, keepdims=True) + EPS)
        h = (xv * r * g_ref[...]).astype(_MXU)
        h_ref[...] = h
        p_ref[...] = _dot(h, w_ref[...])

    tm = min(TMB, L)
    g, g_spec = _of_layer(g)
    return _pallas(body, out_shape=(jax.ShapeDtypeStruct((L, D_INP), F32), jax.ShapeDtypeStruct((L, D_MODEL), _MXU)),
                   grid=(L // tm,),
                   in_specs=[_rows(tm, D_MODEL), g_spec,
                             pl.BlockSpec((D_MODEL, D_INP), lambda i: (0, 0), pipeline_mode=pl.Buffered(1))],
                   out_specs=(_rows(tm, D_INP), _rows(tm, D_MODEL)), name="norm_inproj",
                   compiler_params=pltpu.CompilerParams(vmem_limit_bytes=VMEM_BIG))(x, g, w)


def _rms_bwd(xv, g, dh):
    r = lax.rsqrt(jnp.mean(xv * xv, axis=-1, keepdims=True) + EPS)
    xh = xv * r
    dg = jnp.sum(dh * xh, axis=0, keepdims=True)
    dxh = dh * g
    dx = r * (dxh - xh * jnp.mean(dxh * xh, axis=-1, keepdims=True))
    return dx, dg


def _inproj_bwd(pieces, w, x, g, dres, h):
    L = x.shape[0]
    n = len(pieces)

    def body(*refs):
        w_ref, x_ref, g_ref, dr_ref, h_ref, dx_ref, dg_ref, dw_ref = refs[n:]
        dproj = jnp.concatenate([r[...].astype(_MXU) for r in refs[:n]], axis=-1)
        dh = _dot(dproj, w_ref[...], _NT)
        dx, dg = _rms_bwd(x_ref[...], g_ref[...], dh)
        dx_ref[...] = dx + dr_ref[...]
        dw = _dot(h_ref[...], dproj, _TN)

        @pl.when(pl.program_id(0) == 0)
        def _():
            dg_ref[...] = dg
            dw_ref[...] = dw

        @pl.when(pl.program_id(0) != 0)
        def _():
            dg_ref[...] += dg
            dw_ref[...] += dw

    resident = pl.BlockSpec((D_MODEL, D_INP), lambda i: (0, 0), pipeline_mode=pl.Buffered(1))
    g, g_spec = _of_layer(g)
    return _pallas(body, out_shape=(jax.ShapeDtypeStruct((L, D_MODEL), F32), jax.ShapeDtypeStruct((1, D_MODEL), F32),
                                    jax.ShapeDtypeStruct((D_MODEL, D_INP), F32)),
                   grid=(L // TM,),
                   in_specs=[_rows(TM, p.shape[1]) for p in pieces]
                   + [resident, _rows(TM, D_MODEL), g_spec, _rows(TM, D_MODEL), _rows(TM, D_MODEL)],
                   out_specs=(_rows(TM, D_MODEL), _whole((1, D_MODEL)), resident), name="inproj_bwd",
                   compiler_params=pltpu.CompilerParams(vmem_limit_bytes=VMEM_BIG))(*pieces, w, x, g, dres, h)


PAIR = 2 * HEAD_DIM
N_AUX = 3


def _own(shape, h):
    return lax.broadcasted_iota(jnp.int32, shape, len(shape) - 1) // HEAD_DIM == h


def _hi_dot(a, b):
    return jnp.dot(a, b, precision=_HI, preferred_element_type=F32)


def _tri(n, lower):
    r = lax.broadcasted_iota(jnp.int32, (n, n), 0)
    c = lax.broadcasted_iota(jnp.int32, (n, n), 1)
    return jnp.where(r >= c if lower else r <= c, 1.0, 0.0).astype(F32)


def _fox_cumsum(proj, b):
    L = proj.shape[0]

    def body(fl_ref, b_ref, c_ref, carry_sc):
        @pl.when(pl.program_id(0) == 0)
        def _():
            carry_sc[...] = jnp.zeros((1, PAIR), F32)

        lane = lax.broadcasted_iota(jnp.int32, (TM, PAIR), 1)
        lf = jnp.where(lane < FOX_HEADS, jax.nn.log_sigmoid(fl_ref[...] + b_ref[...]), 0.0)
        cs = _hi_dot(_tri(TM, True), lf) + carry_sc[...]
        c_ref[...] = cs
        carry_sc[...] = cs[TM - 1:TM, :]

    b, b_spec = _of_layer(b)
    return _pallas(body, out_shape=jax.ShapeDtypeStruct((L, PAIR), F32), grid=(L // TM,),
                   in_specs=[_rows(TM, PAIR, O_FL // PAIR), b_spec], out_specs=_rows(TM, PAIR),
                   scratch_shapes=[pltpu.VMEM((1, PAIR), F32)], name="fox_cumsum")(proj, b)


def _fox_prep(proj, c):
    L = proj.shape[0]

    def body(q_ref, k_ref, v_ref, c_ref, qa_ref, ka_ref, kat_ref, vt_ref):
        lane = lax.broadcasted_iota(jnp.int32, (TM, PAIR), 1)
        cv = c_ref[...]
        for p in range(FOX_HEADS // 2):
            cols = slice(PAIR * p, PAIR * (p + 1))
            q2, k2 = q_ref[:, cols], k_ref[:, cols]
            vt_ref[p] = v_ref[:, cols].T.astype(_MXU)
            for e in range(2):
                h = 2 * p + e
                own = lane // HEAD_DIM == e
                a = lane - (HEAD_DIM if e == 0 else 0)
                pick = (lax.broadcasted_iota(jnp.int32, (PAIR, PAIR), 0) == h).astype(F32)
                rest = _hi_dot(cv, pick)
                aux_q = jnp.where((a >= N_AUX) & (a < 2 * N_AUX), 1.0, 0.0)
                aux_k = jnp.where((a >= 0) & (a < N_AUX), 1.0, 0.0)
                for n in range(N_AUX):
                    part = rest.astype(_MXU).astype(F32)
                    rest = rest - part
                    aux_q = jnp.where(a == n, part, aux_q)
                    aux_k = jnp.where(a == N_AUX + n, -part, aux_k)
                ka = jnp.where(own, k2, aux_k)
                qa_ref[h] = jnp.where(own, q2 * (1.0 / math.sqrt(HEAD_DIM)), aux_q).astype(_MXU)
                ka_ref[h] = ka.astype(_MXU)
                kat_ref[h] = ka.T.astype(_MXU)

    hl = jax.ShapeDtypeStruct((FOX_HEADS, L, PAIR), _MXU)
    nat = lambda o: _rows(TM, D_FOX, o // D_FOX)
    rows = pl.BlockSpec((FOX_HEADS, TM, PAIR), lambda i: (0, i, 0))
    return _pallas(
        body, out_shape=(hl, hl, jax.ShapeDtypeStruct((FOX_HEADS, PAIR, L), _MXU),
                         jax.ShapeDtypeStruct((FOX_HEADS // 2, PAIR, L), _MXU)),
        grid=(L // TM,), in_specs=[nat(O_FQ), nat(O_FK), nat(O_FV), _rows(TM, PAIR)],
        out_specs=(rows, rows, pl.BlockSpec((FOX_HEADS, PAIR, TM), lambda i: (0, 0, i)),
                   pl.BlockSpec((FOX_HEADS // 2, PAIR, TM), lambda i: (0, 0, i))),
        name="fox_prep")(proj, proj, proj, c)


def _key_le_query(tq):
    return lax.broadcasted_iota(jnp.int32, (tq, tq), 0) <= lax.broadcasted_iota(jnp.int32, (tq, tq), 1)


def _grid_ends(n0, n1):
    first = lambda: (pl.program_id(0) == 0) & (pl.program_id(1) == 0)
    last = lambda: (pl.program_id(0) == n0 - 1) & (pl.program_id(1) == n1 - 1)
    return first, last


def _fox_fwd(qa, ka, vt, ride=None):
    H, L, _ = qa.shape
    tq = min(TQ, L)
    nq = L // tq

    def body(qa_ref, ka_ref, vt_ref, o_ref, lse_ref, m_sc, l_sc, acc_sc):
        i = pl.program_id(1)
        m_sc[...] = jnp.full((2, 1, tq), NEG, F32)
        l_sc[...] = jnp.zeros((2, 1, tq), F32)
        acc_sc[...] = jnp.zeros((2, HEAD_DIM, tq), F32)

        def block(j, nk, masked):
            keys = pl.ds(pl.multiple_of(j * tq, tq), nk * tq)
            vt_blk = vt_ref[:, keys]
            sts = [_dot(ka_ref[e, keys, :], qa_ref[e], _NT) for e in range(2)]
            pts, alphas = [], []
            for e in range(2):
                st = jnp.where(_key_le_query(tq), sts[e], NEG) if masked else sts[e]
                m_prev = m_sc[e]
                m_new = jnp.maximum(m_prev, jnp.max(st, axis=0, keepdims=True))
                alphas.append(jnp.exp(m_prev - m_new))
                pt = jnp.exp(st - m_new)
                l_sc[e] = alphas[e] * l_sc[e] + jnp.sum(pt, axis=0, keepdims=True)
                m_sc[e] = m_new
                pts.append(pt.astype(_MXU))
            for e in range(2):
                acc_sc[e] = alphas[e] * acc_sc[e] + _dot(vt_blk[HEAD_DIM * e:HEAD_DIM * (e + 1)], pts[e])

        def two_blocks(jj, carry):
            block(2 * jj, 2, False)
            return carry

        lax.fori_loop(0, i // 2, two_blocks, 0)

        @pl.when(i % 2 == 1)
        def _():
            block(i - 1, 1, False)

        block(i, 1, True)
        o_ref[...] = jnp.concatenate([acc_sc[0] / l_sc[0], acc_sc[1] / l_sc[1]], axis=0).T
        for e in range(2):
            lse_ref[e] = m_sc[e] + jnp.log(l_sc[e])

    body, ex_in, ex_specs, ex_out, ex_sems = _riding(body, 3, 2, ride, *_grid_ends(H // 2, nq))
    res = _pallas(
        body, out_shape=(jax.ShapeDtypeStruct((L, D_FOX), F32), jax.ShapeDtypeStruct((H, 1, L), F32), *ex_out),
        grid=(H // 2, nq),
        in_specs=[pl.BlockSpec((2, tq, PAIR), lambda p, i: (p, i, 0)), pl.BlockSpec((2, L, PAIR), lambda p, i: (p, 0, 0)),
                  pl.BlockSpec((None, PAIR, L), lambda p, i: (p, 0, 0)), *ex_specs],
        out_specs=(pl.BlockSpec((tq, PAIR), lambda p, i: (i, p)), pl.BlockSpec((2, 1, tq), lambda p, i: (p, 0, i)),
                   *ex_specs),
        scratch_shapes=[pltpu.VMEM((2, 1, tq), F32), pltpu.VMEM((2, 1, tq), F32), pltpu.VMEM((2, HEAD_DIM, tq), F32), *ex_sems],
        name="fox_fwd" if ride is None else "fox_fwd_gather")(qa, ka, vt, *ex_in)
    return res[0], res[1], list(res[2:])


def _fox_bwd(qa, ka, kat, proj, do, o, lse, ride=None):
    H, L, _ = qa.shape
    tq = min(TQ, L)
    nq = L // tq

    def body(qa_ref, ka_ref, kat_ref, v_ref, do_ref, o_ref, lse_ref, dqt_ref, dk_ref, dv_ref, delta_sc, dk_sc, dv_sc):
        j = pl.program_id(1)

        @pl.when(j == 0)
        def _():
            head_rows = (lax.broadcasted_iota(jnp.int32, (8, PAIR), 1) // HEAD_DIM
                         == lax.broadcasted_iota(jnp.int32, (8, PAIR), 0)).astype(F32)
            delta_sc[...] = lax.dot_general(head_rows, do_ref[...] * o_ref[...], _NT, precision=_HI,
                                            preferred_element_type=F32)
            dqt_ref[...] = jnp.zeros((2, PAIR, L), F32)

        dk_sc[...] = jnp.zeros((2, tq, PAIR), F32)
        dv_sc[...] = jnp.zeros((tq, PAIR), F32)
        vb = v_ref[...]

        def block(i, masked):
            qs = pl.ds(pl.multiple_of(i * tq, tq), tq)
            dob = do_ref[qs, :]
            for e in range(2):
                own = _own((tq, PAIR), e)
                qh = qa_ref[e, qs, :]
                pt = jnp.exp(_dot(ka_ref[e], qh, _NT) - lse_ref[e, :, qs])
                if masked:
                    pt = jnp.where(_key_le_query(tq), pt, 0.0)
                dv_sc[...] += _dot(pt, jnp.where(own, dob, 0.0))
                dpt = _dot(jnp.where(own, vb, 0.0), dob, _NT)
                ds = (pt * (dpt - delta_sc[e:e + 1, qs])).astype(_MXU)
                dk_sc[e] += _dot(ds, qh)
                dqt_ref[e, :, qs] += _dot(kat_ref[e], ds)

        def off_diagonal(i, carry):
            block(i, False)
            return carry

        block(j, True)
        lax.fori_loop(j + 1, nq, off_diagonal, 0)
        dk_ref[...] = dk_sc[...]
        dv_ref[...] = dv_sc[...]

    nat = pl.BlockSpec((L, PAIR), lambda p, j: (0, p))
    body, ex_in, ex_specs, ex_out, ex_sems = _riding(body, 7, 3, ride, *_grid_ends(H // 2, nq))
    res = _pallas(
        body, out_shape=(jax.ShapeDtypeStruct((H, PAIR, L), F32), jax.ShapeDtypeStruct((H, L, PAIR), F32),
                         jax.ShapeDtypeStruct((L, D_FOX), F32), *ex_out),
        grid=(H // 2, nq),
        in_specs=[pl.BlockSpec((2, L, PAIR), lambda p, j: (p, 0, 0)), pl.BlockSpec((2, tq, PAIR), lambda p, j: (p, j, 0)),
                  pl.BlockSpec((2, PAIR, tq), lambda p, j: (p, 0, j)),
                  pl.BlockSpec((tq, PAIR), lambda p, j: (j, O_FV // PAIR + p)), nat, nat,
                  pl.BlockSpec((2, 1, L), lambda p, j: (p, 0, 0)), *ex_specs],
        out_specs=(pl.BlockSpec((2, PAIR, L), lambda p, j: (p, 0, 0)), pl.BlockSpec((2, tq, PAIR), lambda p, j: (p, j, 0)),
                   pl.BlockSpec((tq, PAIR), lambda p, j: (j, p)), *ex_specs),
        scratch_shapes=[pltpu.VMEM((8, L), F32), pltpu.VMEM((2, tq, PAIR), F32), pltpu.VMEM((tq, PAIR), F32), *ex_sems],
        name="fox_bwd" if ride is None else "fox_bwd_exchange",
        compiler_params=pltpu.CompilerParams(vmem_limit_bytes=VMEM_BIG))(qa, ka, kat, proj, do, o, lse, *ex_in)
    return res[0], res[1], res[2], list(res[3:])


def _fox_post_bwd(dqt, dkraw, proj, b):
    L = proj.shape[0]
    nb = L // TM

    def body(dqt_ref, dkr_ref, fl_ref, b_ref, dq_ref, dk_ref, dfl_ref, db_ref, carry_sc):
        first = pl.program_id(0) == 0

        @pl.when(first)
        def _():
            carry_sc[...] = jnp.zeros((1, PAIR), F32)

        lane = lax.broadcasted_iota(jnp.int32, (TM, PAIR), 1)
        rr = lax.broadcasted_iota(jnp.int32, (PAIR, PAIR), 0)
        cc = lax.broadcasted_iota(jnp.int32, (PAIR, PAIR), 1)
        dc = jnp.zeros((TM, PAIR), F32)
        for p in range(FOX_HEADS // 2):
            cols = slice(PAIR * p, PAIR * (p + 1))
            dqs = [dqt_ref[2 * p + e].T for e in range(2)]
            dks = [dkr_ref[2 * p + e] for e in range(2)]
            dq_ref[:, cols] = jnp.where(lane < HEAD_DIM, dqs[0], dqs[1]) * (1.0 / math.sqrt(HEAD_DIM))
            dk_ref[:, cols] = jnp.where(lane < HEAD_DIM, dks[0], dks[1])
            sums = jnp.zeros((TM, PAIR), F32)
            place = jnp.zeros((PAIR, PAIR), F32)
            for e in range(2):
                base = HEAD_DIM if e == 0 else 0
                sums = jnp.where(lane == base, dqs[e], jnp.where(lane == base + N_AUX, -dks[e], sums))
                place = jnp.where(((rr == base) | (rr == base + N_AUX)) & (cc == 2 * p + e), 1.0, place)
            dc = dc + _hi_dot(sums, place)
        rs = _hi_dot(_tri(TM, False), dc) + carry_sc[...]
        carry_sc[...] = rs[0:1, :]
        dfl = jnp.where(lane < FOX_HEADS, rs * jax.nn.sigmoid(-(fl_ref[...] + b_ref[...])), 0.0)
        dfl_ref[...] = dfl
        db = jnp.sum(dfl, axis=0, keepdims=True)

        @pl.when(first)
        def _():
            db_ref[...] = db

        @pl.when(jnp.logical_not(first))
        def _():
            db_ref[...] += db

    rev = lambda i: nb - 1 - i
    b, b_spec = _of_layer(b)
    nat = pl.BlockSpec((TM, D_FOX), lambda i: (rev(i), 0))
    return _pallas(
        body, out_shape=(jax.ShapeDtypeStruct((L, D_FOX), F32),) * 2
        + (jax.ShapeDtypeStruct((L, PAIR), F32), jax.ShapeDtypeStruct((1, PAIR), F32)),
        grid=(nb,),
        in_specs=[pl.BlockSpec((FOX_HEADS, PAIR, TM), lambda i: (0, 0, rev(i))),
                  pl.BlockSpec((FOX_HEADS, TM, PAIR), lambda i: (0, rev(i), 0)),
                  pl.BlockSpec((TM, PAIR), lambda i: (rev(i), O_FL // PAIR)), b_spec],
        out_specs=(nat, nat, pl.BlockSpec((TM, PAIR), lambda i: (rev(i), 0)), _whole((1, PAIR))),
        scratch_shapes=[pltpu.VMEM((1, PAIR), F32)], name="fox_post_bwd")(dqt, dkraw, proj, b)


def _s5_expand():
    r = lax.broadcasted_iota(jnp.int32, (S5_STATE, S5_STATE * S5_GROUP_CH), 0)
    c = lax.broadcasted_iota(jnp.int32, (S5_STATE, S5_STATE * S5_GROUP_CH), 1)
    return jnp.where(c // S5_GROUP_CH == r, 1.0, 0.0).astype(F32)


def _s5_disc_math(ar, ai, ldt, br, bi):
    dt = jnp.exp(ldt)
    mag = jnp.exp(ar * dt)
    lr = mag * jnp.cos(ai * dt)
    li = mag * jnp.sin(ai * dt)
    den = ar * ar + ai * ai
    fr = ((lr - 1.0) * ar + li * ai) / den
    fi = (li * ar - (lr - 1.0) * ai) / den
    e = _s5_expand()
    fre = jnp.dot(fr, e, precision=_HI, preferred_element_type=F32)
    fie = jnp.dot(fi, e, precision=_HI, preferred_element_type=F32)
    return lr, li, fre * br - fie * bi, fre * bi + fie * br


def _layer_blocks(arrs):
    return [pl.BlockSpec((None,) + a.shape[1:], lambda l: (l, 0, 0)) for a in arrs]


def _s5_disc(ar, ai, ldt, br, bi):
    def body(ar_ref, ai_ref, ldt_ref, br_ref, bi_ref, lr_ref, li_ref, bbr_ref, bbi_ref):
        lr, li, bbr, bbi = _s5_disc_math(ar_ref[...], ai_ref[...], ldt_ref[...], br_ref[...], bi_ref[...])
        lr_ref[...] = lr
        li_ref[...] = li
        bbr_ref[...] = bbr
        bbi_ref[...] = bbi

    ins = (ar, ai, ldt, br, bi)
    outs = (ar, ai, br, bi)
    return _pallas(body, out_shape=tuple(jax.ShapeDtypeStruct(a.shape, F32) for a in outs), grid=(DEPTH,),
                   in_specs=_layer_blocks(ins), out_specs=tuple(_layer_blocks(outs)), name="s5_disc")(*ins)


def _s5_disc_bwd(ar, ai, ldt, br, bi, dlr, dli, dbbr, dbbi):
    def body(ar_ref, ai_ref, ldt_ref, br_ref, bi_ref, dlr_ref, dli_ref, dbbr_ref, dbbi_ref,
             dar_ref, dai_ref, dldt_ref, dbr_ref, dbi_ref):
        _, vjp = jax.vjp(_s5_disc_math, ar_ref[...], ai_ref[...], ldt_ref[...], br_ref[...], bi_ref[...])
        dar, dai, dldt, dbr, dbi = vjp((dlr_ref[...], dli_ref[...], dbbr_ref[...], dbbi_ref[...]))
        dar_ref[...] = dar
        dai_ref[...] = dai
        dldt_ref[...] = dldt
        dbr_ref[...] = dbr
        dbi_ref[...] = dbi

    ins = (ar, ai, ldt, br, bi, dlr, dli, dbbr, dbbi)
    outs = (ar, ai, ldt, br, bi)
    return _pallas(body, out_shape=tuple(jax.ShapeDtypeStruct(a.shape, F32) for a in outs), grid=(DEPTH,),
                   in_specs=_layer_blocks(ins), out_specs=tuple(_layer_blocks(outs)), name="s5_disc_bwd")(*ins)


SLAB = 2 * S5_CH // 128


def _slab_rows(s, ts):
    return pl.ds(s, ts, stride=SLAB)


def _slab_pair(ref, s, ts):
    return jnp.concatenate([ref[_slab_rows(s, ts), :].astype(_MXU), ref[_slab_rows(s + 1, ts), :].astype(_MXU)], axis=-1)


def _s5_fwd(proj, wb, wc, lam, d, w_glu):
    L = proj.shape[0]
    ts = min(TS, L)

    def body(u_ref, wb_ref, wc_ref, lam_ref, d_ref, wg_ref, xs_ref, ypre_ref, ys_ref, b_sc, c_sc):
        @pl.when(pl.program_id(0) == 0)
        def _():
            c_sc[...] = jnp.zeros((SLAB, 128), F32)

        u = u_ref[...]
        ub = u.astype(_MXU)
        for s in range(0, SLAB, 2):
            b2 = _dot(ub, wb_ref[:, 128 * s:128 * (s + 2)])
            b_sc[_slab_rows(s, ts), :] = b2[:, :128]
            b_sc[_slab_rows(s + 1, ts), :] = b2[:, 128:]
        lr, li = lam_ref[0:8, :], lam_ref[8:16, :]

        def step(t, carry):
            xr, xi = carry
            row = pl.multiple_of(t * SLAB, SLAB)
            nr = lr * xr - li * xi + b_sc[pl.ds(row, 8), :]
            ni = lr * xi + li * xr + b_sc[pl.ds(row + 8, 8), :]
            xs_ref[pl.ds(row, 8), :] = nr
            xs_ref[pl.ds(row + 8, 8), :] = ni
            return nr, ni

        xr, xi = lax.fori_loop(0, ts, step, (c_sc[0:8, :], c_sc[8:16, :]), unroll=8)
        c_sc[0:8, :] = xr
        c_sc[8:16, :] = xi
        y = jnp.zeros((ts, D_S5), F32)
        for s in range(0, SLAB, 2):
            y = y + _dot(_slab_pair(xs_ref, s, ts), wc_ref[128 * s:128 * (s + 2), :])
        ypre_ref[...] = y
        y1 = jax.nn.gelu(y + d_ref[...] * u)
        ys_ref[...] = y1 * jax.nn.sigmoid(_dot(y1, wg_ref[...]))

    row = _rows(ts, D_S5)
    slabs = pl.BlockSpec((ts * SLAB, 128), lambda n: (n, 0))
    (wb, wb_spec), (wc, wc_spec), (lam, lam_spec), (d, d_spec) = (_of_layer(a) for a in (wb, wc, lam, d))
    return _pallas(
        body, out_shape=(jax.ShapeDtypeStruct((L * SLAB, 128), F32), jax.ShapeDtypeStruct((L, D_S5), F32),
                         jax.ShapeDtypeStruct((L, D_S5), F32)),
        grid=(L // ts,),
        in_specs=[_rows(ts, D_S5, O_SU // D_S5), wb_spec, wc_spec, lam_spec, d_spec, _whole((D_S5, D_S5))],
        out_specs=(slabs, row, row),
        scratch_shapes=[pltpu.VMEM((ts * SLAB, 128), F32), pltpu.VMEM((SLAB, 128), F32)], name="s5_fwd")(
            proj, wb, wc, lam, d, w_glu)


def _s5_bwd(proj, ypre, dys, xs, wb, wc, lam, d, w_glu):
    L = proj.shape[0]
    ts = min(TS, L)
    nb = L // ts

    def body(u_ref, y_ref, dys_ref, xs_ref, xp_ref, wb_ref, wc_ref, lam_ref, d_ref, wg_ref,
             du_ref, dwb_ref, dwc_ref, dlam_ref, dd_ref, dwg_ref, dx_sc, g_sc, c_sc):
        n = pl.program_id(0)

        @pl.when(n == 0)
        def _():
            c_sc[...] = jnp.zeros((SLAB, 128), F32)
            dlam_ref[...] = jnp.zeros((SLAB, 128), F32)
            dwb_ref[...] = jnp.zeros((D_S5, 2 * S5_CH), F32)
            dwc_ref[...] = jnp.zeros((2 * S5_CH, D_S5), F32)
            dd_ref[...] = jnp.zeros((1, D_S5), F32)
            dwg_ref[...] = jnp.zeros((D_S5, D_S5), F32)

        u, dv, dout = u_ref[...], d_ref[...], dys_ref[...]
        y1, gelu_vjp = jax.vjp(jax.nn.gelu, y_ref[...] + dv * u)
        sg = jax.nn.sigmoid(_dot(y1, wg_ref[...]))
        dz = dout * y1 * sg * (1.0 - sg)
        dy, = gelu_vjp(dout * sg + _dot(dz, wg_ref[...], _NT))
        dd_ref[...] += jnp.sum(dy * u, axis=0, keepdims=True)
        dwg_ref[...] += _dot(y1, dz, _TN)
        dyb = dy.astype(_MXU)
        for s in range(0, SLAB, 2):
            cols = slice(128 * s, 128 * (s + 2))
            dx2 = _dot(dyb, wc_ref[cols, :], _NT)
            dx_sc[_slab_rows(s, ts), :] = dx2[:, :128]
            dx_sc[_slab_rows(s + 1, ts), :] = dx2[:, 128:]
            dwc_ref[cols, :] += _dot(_slab_pair(xs_ref, s, ts), dyb, _TN)
        lr, li = lam_ref[0:8, :], lam_ref[8:16, :]

        def adjoint(row, pr, pi, carry):
            gr, gi, ar, ai = carry
            nr = dx_sc[pl.ds(row, 8), :] + lr * gr + li * gi
            ni = dx_sc[pl.ds(row + 8, 8), :] - li * gr + lr * gi
            g_sc[pl.ds(row, 8), :] = nr
            g_sc[pl.ds(row + 8, 8), :] = ni
            return nr, ni, ar + nr * pr + ni * pi, ai - nr * pi + ni * pr

        def step(k, carry):
            row = pl.multiple_of((ts - 1 - k) * SLAB, SLAB)
            prev = pl.multiple_of((ts - 2 - k) * SLAB, SLAB)
            return adjoint(row, xs_ref[pl.ds(prev, 8), :], xs_ref[pl.ds(prev + 8, 8), :], carry)

        z = jnp.zeros((8, 128), F32)
        carry = lax.fori_loop(0, ts - 1, step, (c_sc[0:8, :], c_sc[8:16, :], z, z), unroll=8)
        has_prev = jnp.where(n == nb - 1, 0.0, 1.0)
        gr, gi, ar, ai = adjoint(0, xp_ref[0:8, :] * has_prev, xp_ref[8:16, :] * has_prev, carry)
        c_sc[0:8, :] = gr
        c_sc[8:16, :] = gi
        dlam_ref[0:8, :] += ar
        dlam_ref[8:16, :] += ai
        ub = u.astype(_MXU)
        du = dy * dv
        for s in range(0, SLAB, 2):
            cols = slice(128 * s, 128 * (s + 2))
            gs = _slab_pair(g_sc, s, ts)
            du = du + _dot(gs, wb_ref[:, cols], _NT)
            dwb_ref[:, cols] += _dot(ub, gs, _TN)
        du_ref[...] = du

    blk = lambda n: nb - 1 - n
    row = pl.BlockSpec((ts, D_S5), lambda n: (blk(n), 0))
    (wb, wb_spec), (wc, wc_spec), (lam, lam_spec), (d, d_spec) = (_of_layer(a) for a in (wb, wc, lam, d))
    return _pallas(
        body, out_shape=(jax.ShapeDtypeStruct((L, D_S5), F32), jax.ShapeDtypeStruct((D_S5, 2 * S5_CH), F32),
                         jax.ShapeDtypeStruct((2 * S5_CH, D_S5), F32), jax.ShapeDtypeStruct((SLAB, 128), F32),
                         jax.ShapeDtypeStruct((1, D_S5), F32), jax.ShapeDtypeStruct((D_S5, D_S5), F32)),
        grid=(nb,),
        in_specs=[pl.BlockSpec((ts, D_S5), lambda n: (blk(n), O_SU // D_S5)), row, row,
                  pl.BlockSpec((ts * SLAB, 128), lambda n: (blk(n), 0)),
                  pl.BlockSpec((SLAB, 128), lambda n: (jnp.maximum(blk(n) * ts - 1, 0), 0)),
                  wb_spec, wc_spec, lam_spec, d_spec, _whole((D_S5, D_S5))],
        out_specs=(row, _whole((D_S5, 2 * S5_CH)), _whole((2 * S5_CH, D_S5)), _whole((SLAB, 128)), _whole((1, D_S5)),
                   _whole((D_S5, D_S5))),
        scratch_shapes=[pltpu.VMEM((ts * SLAB, 128), F32), pltpu.VMEM((ts * SLAB, 128), F32), pltpu.VMEM((SLAB, 128), F32)],
        name="s5_bwd")(proj, ypre, dys, xs, xs, wb, wc, lam, d, w_glu)


def _rot(z, cos, sin):
    lane = lax.broadcasted_iota(jnp.int32, z.shape, 1)
    zs = z * sin
    half = HEAD_DIM // 2
    return z * cos + jnp.where(lane % HEAD_DIM < half, pltpu.roll(zs, PAIR - half, 1), pltpu.roll(zs, half, 1))


def _head_avg():
    r = lax.broadcasted_iota(jnp.int32, (PAIR, PAIR), 0) // HEAD_DIM
    c = lax.broadcasted_iota(jnp.int32, (PAIR, PAIR), 1) // HEAD_DIM
    return jnp.where(r == c, 1.0 / HEAD_DIM, 0.0).astype(F32)


def _ret_tables(tq):
    lg = jnp.log1p(-(2.0 ** (-5.0 - jnp.arange(RET_HEADS, dtype=F32))))
    scale = 1.0 / math.sqrt(HEAD_DIM)
    pos = jnp.arange(tq)
    n = pos.astype(F32)
    dist = jnp.abs(n[:, None] - n[None, :])
    ok = (pos[None, :] // CHUNK) <= (pos[:, None] // CHUNK)
    w = jnp.where(ok[None], scale * jnp.exp(lg[:, None, None] * dist[None]), 0.0)
    lgl = jnp.repeat(lg, HEAD_DIM)
    dq_tab = scale * jnp.exp(lgl[None, :] * (n[:, None] + 1.0))
    dk_tab = jnp.exp(lgl[None, :] * (tq - 1.0 - n[:, None]))
    blk = jnp.arange(PAIR) // HEAD_DIM
    bd = (blk[:, None] == blk[None, :]).astype(F32)
    gbd = bd[None] * jnp.exp(lgl.reshape(RET_HEADS // 2, PAIR)[:, :, None] * tq)
    return dict(w=w, wt=w.transpose(0, 2, 1), dq=dq_tab, dk=dk_tab, gbd=gbd, bd=bd)


def _ret_specs(tq, nq, rev, layer):
    blk = (lambda i: nq - 1 - i) if rev else (lambda i: i)
    col = lambda o: pl.BlockSpec((tq, PAIR), lambda p, i: (blk(i), o // PAIR + p))
    return dict(
        rq=col(O_RQ), rk=col(O_RK), rv=col(O_RV), nat=col(0),
        w=pl.BlockSpec((2, tq, tq), lambda p, i: (p, 0, 0)), tab=pl.BlockSpec((tq, PAIR), lambda p, i: (0, p)),
        gbd=pl.BlockSpec((None, PAIR, PAIR), lambda p, i: (p, 0, 0)), bd=pl.BlockSpec((PAIR, PAIR), lambda p, i: (0, 0)),
        gn=pl.BlockSpec((None, 1, PAIR), lambda p, i: (layer, 0, p)), dgn=pl.BlockSpec((1, PAIR), lambda p, i: (0, p)),
        st=pl.BlockSpec((None, None, PAIR, PAIR), lambda p, i: (p, blk(i), 0, 0)))


def _ret_fwd(proj, cos_t, sin_t, tabs, gn):
    L = proj.shape[0]
    tq = tabs["w"].shape[1]
    nq = L // tq

    def body(rq_ref, rk_ref, rv_ref, cos_ref, sin_ref, w_ref, dqt_ref, dkt_ref, gbd_ref, bd_ref, gn_ref,
             o_ref, y_ref, st_ref, s_sc):
        @pl.when(pl.program_id(1) == 0)
        def _():
            s_sc[...] = jnp.zeros((PAIR, PAIR), F32)

        state = s_sc[...]
        st_ref[...] = state
        cos, sin = cos_ref[...], sin_ref[...]
        q2, k2, v2 = _rot(rq_ref[...], cos, sin), _rot(rk_ref[...], cos, sin), rv_ref[...]
        o = _dot(q2 * dqt_ref[...], state)
        for h in range(2):
            own = _own((tq, PAIR), h)
            a = _dot(jnp.where(own, q2, 0.0), k2, _NT) * w_ref[h]
            o = o + _dot(a, jnp.where(own, v2, 0.0))
        s_sc[...] = gbd_ref[...] * state + bd_ref[...] * _dot(k2 * dkt_ref[...], v2, _TN)
        o_ref[...] = o
        avg = _head_avg()
        oc = o - _hi_dot(o, avg)
        y_ref[...] = oc * lax.rsqrt(_hi_dot(oc * oc, avg) + EPS) * gn_ref[...]

    gn, layer = gn
    sp = _ret_specs(tq, nq, False, layer)
    nat = jax.ShapeDtypeStruct((L, D_RET), F32)
    return _pallas(
        body, out_shape=(nat, nat, jax.ShapeDtypeStruct((RET_HEADS // 2, nq, PAIR, PAIR), F32)), grid=(RET_HEADS // 2, nq),
        in_specs=[sp["rq"], sp["rk"], sp["rv"], sp["nat"], sp["nat"], sp["w"], sp["tab"], sp["tab"], sp["gbd"], sp["bd"],
                  sp["gn"]],
        out_specs=(sp["nat"], sp["nat"], sp["st"]), scratch_shapes=[pltpu.VMEM((PAIR, PAIR), F32)],
        name="ret_fwd")(proj, proj, proj, cos_t, sin_t, tabs["w"], tabs["dq"], tabs["dk"], tabs["gbd"], tabs["bd"], gn)


def _ret_bwd(proj, cos_t, sin_t, tabs, gn, o_pre, dy, states):
    L = proj.shape[0]
    tq = tabs["w"].shape[1]
    nq = L // tq

    def body(rq_ref, rk_ref, rv_ref, cos_ref, sin_ref, w_ref, wt_ref, dqt_ref, dkt_ref, gbd_ref, bd_ref, gn_ref,
             o_ref, dy_ref, st_ref, drq_ref, drk_ref, drv_ref, dgn_ref, g_sc):
        first = pl.program_id(1) == 0

        @pl.when(first)
        def _():
            g_sc[...] = jnp.zeros((PAIR, PAIR), F32)

        cos, sin = cos_ref[...], sin_ref[...]
        q2, k2, v2 = _rot(rq_ref[...], cos, sin), _rot(rk_ref[...], cos, sin), rv_ref[...]
        avg = _head_avg()
        ov, dyv = o_ref[...], dy_ref[...]
        oc = ov - _hi_dot(ov, avg)
        r = lax.rsqrt(_hi_dot(oc * oc, avg) + EPS)
        oh = oc * r
        dgn = jnp.sum(dyv * oh, axis=0, keepdims=True)
        doh = dyv * gn_ref[...]
        do = r * (doh - _hi_dot(doh, avg) - oh * _hi_dot(doh * oh, avg))
        state, g = st_ref[...], g_sc[...]
        dqt, dkt = dqt_ref[...], dkt_ref[...]
        dq = _dot(do, state, _NT) * dqt
        dk = _dot(v2, g, _NT) * dkt
        dv = _dot(k2 * dkt, g)
        g_sc[...] = gbd_ref[...] * g + bd_ref[...] * _dot(q2 * dqt, do, _TN)
        for h in range(2):
            own = _own((tq, PAIR), h)
            qm, dom = jnp.where(own, q2, 0.0), jnp.where(own, do, 0.0)
            dv = dv + _dot(_dot(k2, qm, _NT) * wt_ref[h], dom)
            dq = dq + _dot(_dot(dom, v2, _NT) * w_ref[h], jnp.where(own, k2, 0.0))
            dk = dk + _dot(_dot(v2, dom, _NT) * wt_ref[h], qm)
        drq_ref[...] = _rot(dq, cos, -sin)
        drk_ref[...] = _rot(dk, cos, -sin)
        drv_ref[...] = dv

        @pl.when(first)
        def _():
            dgn_ref[...] = dgn

        @pl.when(jnp.logical_not(first))
        def _():
            dgn_ref[...] += dgn

    gn, layer = gn
    sp = _ret_specs(tq, nq, True, layer)
    nat = jax.ShapeDtypeStruct((L, D_RET), F32)
    return _pallas(
        body, out_shape=(nat, nat, nat, jax.ShapeDtypeStruct((1, D_RET), F32)), grid=(RET_HEADS // 2, nq),
        in_specs=[sp["rq"], sp["rk"], sp["rv"], sp["nat"], sp["nat"], sp["w"], sp["w"], sp["tab"], sp["tab"], sp["gbd"],
                  sp["bd"], sp["gn"], sp["nat"], sp["nat"], sp["st"]],
        out_specs=(sp["nat"], sp["nat"], sp["nat"], sp["dgn"]), scratch_shapes=[pltpu.VMEM((PAIR, PAIR), F32)],
        name="ret_bwd")(proj, proj, proj, cos_t, sin_t, tabs["w"], tabs["wt"], tabs["dq"], tabs["dk"], tabs["gbd"],
                        tabs["bd"], gn, o_pre, dy, states)


def _gate_out(yf, ys, yr, proj, x, w):
    L = x.shape[0]

    def body(yf_ref, ys_ref, yr_ref, g_ref, x_ref, w_ref, xn_ref):
        cat = jnp.concatenate([yf_ref[...], ys_ref[...], yr_ref[...]], axis=-1)
        xn_ref[...] = x_ref[...] + _dot(cat * jax.nn.silu(g_ref[...]), w_ref[...])

    tm = min(TMB, L)
    full = _rows(tm, D_MODEL)
    return _pallas(body, out_shape=jax.ShapeDtypeStruct((L, D_MODEL), F32), grid=(L // tm,),
                   in_specs=[_rows(tm, D_FOX), _rows(tm, D_S5), _rows(tm, D_RET), _rows(tm, D_MODEL, O_GATE // D_MODEL),
                             full, _whole((D_MODEL, D_MODEL))],
                   out_specs=full, name="gate_out")(yf, ys, yr, proj, x, w)


def _gate_out_bwd(dxn, w, yf, ys, yr, proj):
    L = dxn.shape[0]

    def body(dx_ref, w_ref, yf_ref, ys_ref, yr_ref, g_ref, dyf_ref, dys_ref, dyr_ref, dg_ref, dw_ref):
        dxv = dx_ref[...].astype(_MXU)
        dy = _dot(dxv, w_ref[...], _NT)
        g = g_ref[...]
        sg = jax.nn.sigmoid(g)
        silu = g * sg
        dcat = dy * silu
        dyf_ref[...] = dcat[:, :D_FOX]
        dys_ref[...] = dcat[:, D_FOX:D_FOX + D_S5]
        dyr_ref[...] = dcat[:, D_FOX + D_S5:]
        cat = jnp.concatenate([yf_ref[...], ys_ref[...], yr_ref[...]], axis=-1)
        dg_ref[...] = dy * cat * (sg * (1.0 + g * (1.0 - sg)))
        dw = _dot(cat * silu, dxv, _TN)

        @pl.when(pl.program_id(0) == 0)
        def _():
            dw_ref[...] = dw

        @pl.when(pl.program_id(0) != 0)
        def _():
            dw_ref[...] += dw

    tm = min(TMB, L)
    full = _rows(tm, D_MODEL)
    f, s, r = _rows(tm, D_FOX), _rows(tm, D_S5), _rows(tm, D_RET)
    sq = _whole((D_MODEL, D_MODEL))
    return _pallas(body, out_shape=(jax.ShapeDtypeStruct((L, D_FOX), F32), jax.ShapeDtypeStruct((L, D_S5), F32),
                                    jax.ShapeDtypeStruct((L, D_RET), F32), jax.ShapeDtypeStruct((L, D_MODEL), F32),
                                    jax.ShapeDtypeStruct((D_MODEL, D_MODEL), F32)),
                   grid=(L // tm,), in_specs=[full, sq, f, s, r, _rows(tm, D_MODEL, O_GATE // D_MODEL)],
                   out_specs=(f, s, r, full, sq), name="gate_out_bwd",
                   compiler_params=pltpu.CompilerParams(vmem_limit_bytes=VMEM_BIG))(dxn, w, yf, ys, yr, proj)


def _final_loss(x, g, tgt):
    L = x.shape[0]

    def body(x_ref, g_ref, t_ref, loss_ref, dx_ref, dg_ref):
        xv, gv = x_ref[...], g_ref[...]
        r = lax.rsqrt(jnp.mean(xv * xv, axis=-1, keepdims=True) + EPS)
        err = xv * r * gv - t_ref[...]
        part = 0.5 * jnp.sum(jnp.mean(err * err, axis=-1, keepdims=True), axis=0, keepdims=True)
        dx, dg = _rms_bwd(xv, gv, err * (1.0 / D_MODEL))
        dx_ref[...] = dx

        @pl.when(pl.program_id(0) == 0)
        def _():
            loss_ref[...] = part
            dg_ref[...] = dg

        @pl.when(pl.program_id(0) != 0)
        def _():
            loss_ref[...] += part
            dg_ref[...] += dg

    full = _rows(TM, D_MODEL)
    return _pallas(body, out_shape=(jax.ShapeDtypeStruct((1, 1), F32), jax.ShapeDtypeStruct((L, D_MODEL), F32),
                                    jax.ShapeDtypeStruct((1, D_MODEL), F32)),
                   grid=(L // TM,), in_specs=[full, _whole((1, D_MODEL)), full],
                   out_specs=(_whole((1, 1)), full, _whole((1, D_MODEL))), name="final_loss")(x, g, tgt)


def _block_diag(blocks):
    n, g, r, c = blocks.shape
    eye = jnp.eye(g, dtype=blocks.dtype)
    return (blocks[:, :, :, None, :] * eye[None, :, None, :, None]).reshape(n, g * r, g * c)


def _diag_blocks(m, g):
    n, r, c = m.shape[0], m.shape[1] // g, m.shape[2] // g
    eye = jnp.eye(g, dtype=m.dtype)
    return jnp.sum(m.reshape(n, g, r, g, c) * eye[None, :, None, :, None], axis=3)


def _rope_tables(L):
    half = HEAD_DIM // 2
    freqs = ROPE_BASE ** (-jnp.arange(half, dtype=F32) / half)
    ang = jnp.arange(L, dtype=F32)[:, None] * freqs[None, :]
    cos, sin = jnp.cos(ang), jnp.sin(ang)
    cos_t = jnp.tile(jnp.concatenate([cos, cos], axis=-1), (1, RET_HEADS))
    sin_t = jnp.tile(jnp.concatenate([sin, -sin], axis=-1), (1, RET_HEADS))
    return cos_t, sin_t


def _s5_disc_args(small):
    g, s, ch = S5_GROUPS, S5_STATE, S5_GROUP_CH
    return (small["s5_a_re"], small["s5_a_im"], small["s5_log_dt"][:, :, None],
            small["s5_b_re"].reshape(DEPTH, g, s * ch), small["s5_b_im"].reshape(DEPTH, g, s * ch))


def _s5_mats(small):
    g, s, ch = S5_GROUPS, S5_STATE, S5_GROUP_CH
    lr, li, bbr, bbi = _s5_disc(*_s5_disc_args(small))
    lam = jnp.concatenate([lr.reshape(DEPTH, 8, 128), li.reshape(DEPTH, 8, 128)], axis=1)
    wb = jnp.concatenate([_block_diag(b.reshape(DEPTH, g, s, ch).transpose(0, 1, 3, 2)) for b in (bbr, bbi)], axis=2)
    wc = jnp.concatenate([_block_diag(c.transpose(0, 1, 3, 2)) for c in (small["s5_c_re"], -small["s5_c_im"])], axis=1)
    return lam, wb.astype(_MXU), wc.astype(_MXU)


def _s5_param_grads(small, dwb, dwc, dlam):
    g, s, ch = S5_GROUPS, S5_STATE, S5_GROUP_CH
    dc = [_diag_blocks(m, g).transpose(0, 1, 3, 2) for m in (dwc[:, :S5_CH], dwc[:, S5_CH:])]
    dbb = [_diag_blocks(m, g).transpose(0, 1, 3, 2).reshape(DEPTH, g, s * ch) for m in (dwb[:, :, :S5_CH], dwb[:, :, S5_CH:])]
    dar, dai, dldt, dbr, dbi = _s5_disc_bwd(*_s5_disc_args(small), dlam[:, :8].reshape(DEPTH, g, s),
                                            dlam[:, 8:].reshape(DEPTH, g, s), dbb[0], dbb[1])
    shp = (DEPTH, g, s, ch)
    return dict(s5_a_re=dar, s5_a_im=dai, s5_log_dt=dldt.reshape(DEPTH, g), s5_b_re=dbr.reshape(shp),
                s5_b_im=dbi.reshape(shp), s5_c_re=dc[0], s5_c_im=-dc[1])


def _layer_fwd(x, p, rope, ride=None, late=False):
    L = x.shape[0]
    cos_t, sin_t, ret_tabs = rope
    s = {"x": x}
    proj, h = _norm_inproj(x, p["norm_w"], p["w_in"])
    s["proj"], s["h"] = proj, h
    qa, ka, kat, vt = _fox_prep(proj, _fox_cumsum(proj, p["b_f"]))
    yf, lse, landed = _fox_fwd(qa, ka, vt, ride)
    s.update(qa=qa, ka=ka, kat=kat, lse=lse, yf=yf)
    if late:
        p["w_glu"], p["w_out"] = _gathered_rows(landed[-2]), _gathered_rows(landed[-1])
        landed = landed[:-2]
    xs, ypre, ys = _s5_fwd(proj, p["wb"], p["wc"], p["lam"], p["d"], p["w_glu"])
    s.update(xs=xs, ypre=ypre, ys=ys)
    o_pre, yr, states = _ret_fwd(proj, cos_t, sin_t, ret_tabs, p["gn_w"])
    s.update(o_pre=o_pre, yr=yr, states=states)
    return _gate_out(yf, ys, yr, proj, x, p["w_out"]), s, landed


def _layer_bwd(dxn, s, p, rope, ride=None, early=False):
    L = dxn.shape[0]
    cos_t, sin_t, ret_tabs = rope
    g = {}
    proj = s["proj"]
    dyf, dys, dyr, dgate, g["w_out"] = _gate_out_bwd(dxn, p["w_out"], s["yf"], s["ys"], s["yr"], proj)
    drq, drk, drv, dgn = _ret_bwd(proj, cos_t, sin_t, ret_tabs, p["gn_w"], s["o_pre"], dyr, s["states"])
    g["ret_gn_w"] = dgn.reshape(D_RET)
    dsu, g["wb"], g["wc"], g["lam"], dd, g["s5_w_glu"] = _s5_bwd(proj, s["ypre"], dys, s["xs"], p["wb"], p["wc"], p["lam"],
                                                                 p["d"], p["w_glu"])
    g["s5_d"] = dd.reshape(D_S5)
    if early:
        ride = (ride[0] + _row_slots(g), ride[1] + [True, True])
    dqt, dkraw, dv, landed = _fox_bwd(s["qa"], s["ka"], s["kat"], proj, dyf, s["yf"], s["lse"], ride)
    dq, dk, dfl, dbf = _fox_post_bwd(dqt, dkraw, proj, p["b_f"])
    g["fox_b_f"] = dbf[0, :FOX_HEADS]
    pieces = [dgate, dq, dk, dv, dsu, drq, drk, drv, dfl]
    dx, dnw, g["w_in"] = _inproj_bwd(pieces, p["w_in"], s["x"], p["norm_w"], dxn, s["h"])
    g["norm_w"] = dnw.reshape(D_MODEL)
    return dx, g, landed


def _stacked_params(small):
    lam, wb, wc = _s5_mats(small)
    row = lambda a: a[:, None, :]
    return dict(norm_w=row(small["norm_w"]), b_f=row(jnp.pad(small["fox_b_f"], ((0, 0), (0, PAIR - FOX_HEADS)))),
                lam=lam, wb=wb, wc=wc, d=row(small["s5_d"]), gn_w=row(small["ret_gn_w"]))


def _layer_params(l, w_in_p, w_glu, w_out, stacked):
    return dict({k: (a, l) for k, a in stacked.items()}, w_in=w_in_p, w_glu=w_glu, w_out=w_out)


_SHARDED = ("w_in", "s5_w_glu", "w_out")
_WIRE = jnp.bfloat16


_RUNS = ((2568, 3592, O_GATE), (0, 1536, O_FQ), (1544, 2568, O_SU), (1536, 1544, O_FL))


def _shard_pieces():
    out = []
    for a, b, pad in _RUNS:
        while a < b:
            j = a // W_SHARD
            e = min(b, (j + 1) * W_SHARD)
            out.append((j, a - j * W_SHARD, e - j * W_SHARD, pad))
            pad, a = pad + e - a, e
    return out


def _gathered_w_in(g_in):
    cols = [g_in[j, :, a:e] for j, a, e, _ in _shard_pieces()]
    cols.append(jnp.zeros((D_MODEL, D_INP - O_FL - FOX_HEADS), g_in.dtype))
    return jnp.concatenate(cols, axis=1)


def _gathered_rows(g):
    return g.reshape(-1, g.shape[-1])


def _w_in_slots(g):
    w_in = g["w_in"].astype(_WIRE)
    slots = []
    for j in range(N_DEV):
        mine = sorted((a, e, pad) for jj, a, e, pad in _shard_pieces() if jj == j)
        slots.append(jnp.concatenate([w_in[:, pad:pad + e - a] for a, e, pad in mine], axis=1))
    return jnp.stack(slots)


def _row_slots(g):
    return [g["s5_w_glu"].reshape(N_DEV, D_S5 // N_DEV, D_S5).astype(_WIRE),
            g["w_out"].reshape(N_DEV, D_MODEL // N_DEV, D_MODEL).astype(_WIRE)]


def _step_grads(x, tgt, small, full=None, shards=None):
    L = x.shape[0]
    rope = _rope_tables(L) + (_ret_tables(min(TQ, L)),)
    stacked = _stacked_params(small)
    if shards is not None:
        nxt = (_gathered_w_in(_exchange([shards[0][0]], [False], "gather_layer0")[0]), None, None)
    saved, params = [], []
    for l in range(DEPTH):
        weights = nxt if shards is not None else tuple(f[l] for f in full)
        ride = None
        if shards is not None:
            arrs = [s[l + 1] for s in shards] if l + 1 < DEPTH else []
            arrs += [shards[1][0], shards[2][0]] if l == 0 else []
            ride = (arrs, [False] * len(arrs)) if arrs else None
        params.append(_layer_params(l, *weights, stacked))
        x, s, landed = _layer_fwd(x, params[l], rope, ride, late=shards is not None and l == 0)
        if landed:
            nxt = (_gathered_w_in(landed[0]), _gathered_rows(landed[1]), _gathered_rows(landed[2]))
        saved.append(s)
    loss, dx, dfw = _final_loss(x, small["final_norm_w"][None], tgt)
    grads, partials, waiting = [None] * DEPTH, [None] * DEPTH, None
    for l in reversed(range(DEPTH)):
        ride = (waiting, [True] * len(waiting)) if waiting is not None else None
        dx, grads[l], landed = _layer_bwd(dx, saved[l], params[l], rope, ride, early=ride is not None and l == 0)
        if waiting is not None:
            partials[l + 1] = landed[:3]
        if shards is not None:
            waiting = [_w_in_slots(grads[l])] + _row_slots(grads[l])
    stack = lambda n: jnp.stack([g[n] for g in grads])
    small_g = {n: stack(n) for n in ("norm_w", "fox_b_f", "s5_d", "ret_gn_w")}
    small_g.update(_s5_param_grads(small, stack("wb"), stack("wc"), stack("lam")), final_norm_w=dfw)
    if shards is None:
        return loss, dx, grads, small_g
    last = _exchange([waiting[0]] + [small_g[n].astype(_WIRE) for n in _SMALL], [True] + [False] * len(_SMALL),
                     "exchange_layer0")
    partials[0] = [last[0]] + landed[3:]
    return loss, dx, grads, small_g, partials, dict(zip(_SMALL, last[1:]))


_MESH = pl.DeviceIdType.MESH
_ANY = pl.BlockSpec(memory_space=pl.ANY)


def _me_and_peers():
    x, y, c = lax.axis_index("x"), lax.axis_index("y"), lax.axis_index("c")
    flip = lambda a, bit: (1 - a) if bit else a
    peers = []
    for r in range(1, N_DEV):
        px, py, pc = flip(x, (r >> 2) & 1), flip(y, (r >> 1) & 1), flip(c, r & 1)
        peers.append(((px, py, pc), 4 * px + 2 * py + pc))
    return 4 * x + 2 * y + c, peers


def _exchange_copies(srcs, dsts, sems, scatter):
    send_sems, recv_sems, local_sems = sems
    me, peers = _me_and_peers()
    pick = lambda t, to: srcs[t].at[to] if scatter[t] else srcs[t]
    own = [pltpu.make_async_copy(pick(t, me), dsts[t].at[me], local_sems.at[t]) for t in range(len(srcs))]
    sends, waits = [], []
    for r, (dev, idx) in enumerate(peers):
        for t in range(len(srcs)):
            for land, out in ((me, sends), (idx, waits)):
                out.append(pltpu.make_async_remote_copy(pick(t, idx), dsts[t].at[land], send_sems.at[t, r], recv_sems.at[t, r],
                                                        device_id=dev, device_id_type=_MESH))
    return own, sends, waits


def _exchange_start(srcs, dsts, sems, scatter):
    own, sends, _ = _exchange_copies(srcs, dsts, sems, scatter)
    for cp in own + sends:
        cp.start()


def _exchange_wait(srcs, dsts, sems, scatter):
    own, _, waits = _exchange_copies(srcs, dsts, sems, scatter)
    for cp in waits + own:
        cp.wait()


def _exchange_shapes(arrs, scatter):
    outs = [jax.ShapeDtypeStruct(a.shape if sc else (N_DEV,) + a.shape, a.dtype) for a, sc in zip(arrs, scatter)]
    n = len(arrs)
    sems = [pltpu.SemaphoreType.DMA((n, N_DEV - 1)), pltpu.SemaphoreType.DMA((n, N_DEV - 1)), pltpu.SemaphoreType.DMA((n,))]
    return outs, sems


def _exchange(arrs, scatter, name):
    n = len(arrs)

    def body(*refs):
        _exchange_start(refs[:n], refs[n:2 * n], refs[2 * n:], scatter)
        _exchange_wait(refs[:n], refs[n:2 * n], refs[2 * n:], scatter)

    outs, sems = _exchange_shapes(arrs, scatter)
    return _pallas(body, out_shape=tuple(outs), in_specs=[_ANY] * n, out_specs=tuple([_ANY] * n), scratch_shapes=sems,
                   name=name)(*arrs)


def _riding(body, n_in, n_out, ride, is_first, is_last):
    if ride is None:
        return body, [], [], [], []
    arrs, scatter = ride
    n = len(arrs)
    outs, sems = _exchange_shapes(arrs, scatter)

    def wrapped(*refs):
        ins, srcs = refs[:n_in], refs[n_in:n_in + n]
        own_outs, dsts = refs[n_in + n:n_in + n + n_out], refs[n_in + n + n_out:n_in + 2 * n + n_out]
        scratch, ex_sems = refs[n_in + 2 * n + n_out:-3], refs[-3:]

        @pl.when(is_first())
        def _():
            _exchange_start(srcs, dsts, ex_sems, scatter)

        body(*ins, *own_outs, *scratch)

        @pl.when(is_last())
        def _():
            _exchange_wait(srcs, dsts, ex_sems, scatter)

    return wrapped, list(arrs), [_ANY] * n, outs, sems


def _adamw_body(p_ref, w_ref, m_ref, v_ref, g_ref, d_ref, nm_ref, nv_ref):
    g = p_ref[0].astype(F32)
    for i in range(1, N_DEV):
        g = g + p_ref[i].astype(F32)
    nm = ADAM_B1 * m_ref[...] + (1.0 - ADAM_B1) * g
    nv = ADAM_B2 * v_ref[...] + (1.0 - ADAM_B2) * jnp.square(g)
    m_hat = nm / (1.0 - ADAM_B1 ** ADAM_STEP)
    v_hat = nv / (1.0 - ADAM_B2 ** ADAM_STEP)
    g_ref[...] = g
    d_ref[...] = -ADAM_LR * (m_hat / (jnp.sqrt(v_hat) + ADAM_EPS) + ADAM_WD * w_ref[...])
    nm_ref[...] = nm
    nv_ref[...] = nv


def _adamw(parts, w, m, v, name):
    n, nb, rows, cols = parts.shape
    tm = next(t for t in (256, 128, 64, 32, 16) if rows % t == 0)

    def body(*refs):
        _adamw_body(*refs)

    row = pl.BlockSpec((None, tm, cols), lambda b, i: (b, i, 0))
    return _pallas(body, out_shape=(jax.ShapeDtypeStruct((nb, rows, cols), F32),) * 4, grid=(nb, rows // tm),
                   in_specs=[pl.BlockSpec((n, None, tm, cols), lambda b, i: (0, b, i, 0)), row, row, row],
                   out_specs=(row,) * 4, name=name)(parts, w, m, v)


def _adamw_whole(parts, w, m, v, name):
    def body(*refs):
        _adamw_body(*refs)

    if w.ndim == 2:
        grid = (1,)
        slab = pl.BlockSpec(w.shape, lambda b: (0, 0))
        part = pl.BlockSpec(parts.shape, lambda b: (0, 0, 0))
    else:
        grid, rest = (w.shape[0],), w.shape[1:]
        zeros = (0,) * len(rest)
        slab = pl.BlockSpec((None,) + rest, lambda b: (b,) + zeros)
        part = pl.BlockSpec((N_DEV, None) + rest, lambda b: (0, b) + zeros)
    return _pallas(body, out_shape=(jax.ShapeDtypeStruct(w.shape, F32),) * 4, grid=grid,
                   in_specs=[part, slab, slab, slab], out_specs=(slab,) * 4, name=name)(parts, w, m, v)


_WEIGHTS = ("norm_w", "w_in", "fox_b_f", "s5_a_re", "s5_a_im", "s5_b_re", "s5_b_im", "s5_c_re", "s5_c_im", "s5_d",
            "s5_log_dt", "s5_w_glu", "ret_gn_w", "w_out", "final_norm_w")
_SMALL = tuple(n for n in _WEIGHTS if n not in _SHARDED)


def kernel(x, norm_w, w_in, fox_b_f, s5_a_re, s5_a_im, s5_b_re, s5_b_im, s5_c_re, s5_c_im, s5_d, s5_log_dt, s5_w_glu, ret_gn_w, w_out, final_norm_w, loss_target, m_norm_w, m_w_in, m_fox_b_f, m_s5_a_re, m_s5_a_im, m_s5_b_re, m_s5_b_im, m_s5_c_re, m_s5_c_im, m_s5_d, m_s5_log_dt, m_s5_w_glu, m_ret_gn_w, m_w_out, m_final_norm_w, v_norm_w, v_w_in, v_fox_b_f, v_s5_a_re, v_s5_a_im, v_s5_b_re, v_s5_b_im, v_s5_c_re, v_s5_c_im, v_s5_d, v_s5_log_dt, v_s5_w_glu, v_ret_gn_w, v_w_out, v_final_norm_w):
    w = dict(norm_w=norm_w, w_in=w_in, fox_b_f=fox_b_f, s5_a_re=s5_a_re, s5_a_im=s5_a_im, s5_b_re=s5_b_re, s5_b_im=s5_b_im,
             s5_c_re=s5_c_re, s5_c_im=s5_c_im, s5_d=s5_d, s5_log_dt=s5_log_dt, s5_w_glu=s5_w_glu, ret_gn_w=ret_gn_w,
             w_out=w_out, final_norm_w=final_norm_w)
    m = dict(norm_w=m_norm_w, w_in=m_w_in, fox_b_f=m_fox_b_f, s5_a_re=m_s5_a_re, s5_a_im=m_s5_a_im, s5_b_re=m_s5_b_re,
             s5_b_im=m_s5_b_im, s5_c_re=m_s5_c_re, s5_c_im=m_s5_c_im, s5_d=m_s5_d, s5_log_dt=m_s5_log_dt,
             s5_w_glu=m_s5_w_glu, ret_gn_w=m_ret_gn_w, w_out=m_w_out, final_norm_w=m_final_norm_w)
    v = dict(norm_w=v_norm_w, w_in=v_w_in, fox_b_f=v_fox_b_f, s5_a_re=v_s5_a_re, s5_a_im=v_s5_a_im, s5_b_re=v_s5_b_re,
             s5_b_im=v_s5_b_im, s5_c_re=v_s5_c_re, s5_c_im=v_s5_c_im, s5_d=v_s5_d, s5_log_dt=v_s5_log_dt,
             s5_w_glu=v_s5_w_glu, ret_gn_w=v_ret_gn_w, w_out=v_w_out, final_norm_w=v_final_norm_w)

    small = {n: w[n] for n in _SMALL}
    loss, dx, _, _, partials, r_small = _step_grads(x[0], loss_target[0], small, shards=[w[n].astype(_MXU) for n in _SHARDED])

    res = {}
    for t, n in enumerate(_SHARDED):
        res[n] = _adamw(jnp.stack([partials[l][t] for l in range(DEPTH)], axis=1), w[n], m[n], v[n], "adamw_" + n)
    for n in _SMALL:
        if w[n].ndim == 1:
            outs = _adamw_whole(r_small[n], w[n][None], m[n][None], v[n][None], "adamw_" + n)
            res[n] = [o[0] for o in outs]
        else:
            res[n] = _adamw_whole(r_small[n], w[n], m[n], v[n], "adamw_" + n)

    loss = lax.psum(loss[0, 0], ("x", "y", "c"))
    return (loss, dx[None], *[res[n][0] for n in _WEIGHTS], *[res[n][1] for n in _WEIGHTS],
            *[res[n][2] for n in _WEIGHTS], *[res[n][3] for n in _WEIGHTS])
```

```python
import math

import jax
import jax.numpy as jnp
from jax import lax
from jax.experimental import pallas as pl
from jax.experimental.pallas import tpu as pltpu

F32 = jnp.float32
_MXU = jnp.bfloat16
_HI = lax.Precision.HIGHEST

N_DEV = 8
DEPTH = 4
D_MODEL = 1024
HEAD_DIM = 64
D_FOX = 512
FOX_HEADS = 8
D_S5 = 256
S5_GROUPS = 16
S5_GROUP_CH = 16
S5_STATE = 64
S5_CH = S5_GROUPS * S5_STATE
D_RET = 256
RET_HEADS = 4
CHUNK = 64
ROPE_BASE = 10000.0
EPS = 1e-6
D_IN = 3592
D_INP = 3712
W_SHARD = D_IN // N_DEV
O_GATE, O_FQ, O_FK, O_FV, O_SU, O_RQ, O_RK, O_RV, O_FL = 0, 1024, 1536, 2048, 2560, 2816, 3072, 3328, 3584

ADAM_LR, ADAM_B1, ADAM_B2, ADAM_EPS, ADAM_WD, ADAM_STEP = 0.001, 0.9, 0.999, 1e-08, 0.01, 10

TM = 256
TMB = 512
TQ = 512
TS = 512
NEG = -1e30
VMEM_BIG = 56 * 1024 * 1024


def _pallas(body, **kw):
    return pl.pallas_call(body, **kw)


def _whole(shape):
    n = len(shape)
    return pl.BlockSpec(shape, lambda *_: (0,) * n)


def _rows(tm, width, col=0):
    return pl.BlockSpec((tm, width), lambda i: (i, col))


def _of_layer(param):
    a, l = param
    return a, pl.BlockSpec((None,) + a.shape[1:], lambda *_: (l,) + (0,) * (a.ndim - 1))


def _dot(a, b, dims=(((1,), (0,)), ((), ()))):
    return lax.dot_general(a.astype(_MXU), b.astype(_MXU), dims, preferred_element_type=F32)


_NT = (((1,), (1,)), ((), ()))
_TN = (((0,), (0,)), ((), ()))


def _norm_inproj(x, g, w):
    L = x.shape[0]

    def body(x_ref, g_ref, w_ref, p_ref, h_ref):
        xv = x_ref[...]
        r = lax.rsqrt(jnp.mean(xv * xv, axis=-1, keepdims=True) + EPS)
        h = (xv * r * g_ref[...]).astype(_MXU)
        h_ref[...] = h
        p_ref[...] = _dot(h, w_ref[...])

    tm = min(TMB, L)
    g, g_spec = _of_layer(g)
    return _pallas(body, out_shape=(jax.ShapeDtypeStruct((L, D_INP), F32), jax.ShapeDtypeStruct((L, D_MODEL), _MXU)),
                   grid=(L // tm,),
                   in_specs=[_rows(tm, D_MODEL), g_spec,
                             pl.BlockSpec((D_MODEL, D_INP), lambda i: (0, 0), pipeline_mode=pl.Buffered(1))],
                   out_specs=(_rows(tm, D_INP), _rows(tm, D_MODEL)), name="norm_inproj",
                   compiler_params=pltpu.CompilerParams(vmem_limit_bytes=VMEM_BIG))(x, g, w)


def _rms_bwd(xv, g, dh):
    r = lax.rsqrt(jnp.mean(xv * xv, axis=-1, keepdims=True) + EPS)
    xh = xv * r
    dg = jnp.sum(dh * xh, axis=0, keepdims=True)
    dxh = dh * g
    dx = r * (dxh - xh * jnp.mean(dxh * xh, axis=-1, keepdims=True))
    return dx, dg


def _inproj_bwd(pieces, w, x, g, dres, h):
    L = x.shape[0]
    n = len(pieces)

    def body(*refs):
        w_ref, x_ref, g_ref, dr_ref, h_ref, dx_ref, dg_ref, dw_ref = refs[n:]
        dproj = jnp.concatenate([r[...].astype(_MXU) for r in refs[:n]], axis=-1)
        dh = _dot(dproj, w_ref[...], _NT)
        dx, dg = _rms_bwd(x_ref[...], g_ref[...], dh)
        dx_ref[...] = dx + dr_ref[...]
        dw = _dot(h_ref[...], dproj, _TN)

        @pl.when(pl.program_id(0) == 0)
        def _():
            dg_ref[...] = dg
            dw_ref[...] = dw

        @pl.when(pl.program_id(0) != 0)
        def _():
            dg_ref[...] += dg
            dw_ref[...] += dw

    resident = pl.BlockSpec((D_MODEL, D_INP), lambda i: (0, 0), pipeline_mode=pl.Buffered(1))
    g, g_spec = _of_layer(g)
    return _pallas(body, out_shape=(jax.ShapeDtypeStruct((L, D_MODEL), F32), jax.ShapeDtypeStruct((1, D_MODEL), F32),
                                    jax.ShapeDtypeStruct((D_MODEL, D_INP), F32)),
                   grid=(L // TM,),
                   in_specs=[_rows(TM, p.shape[1]) for p in pieces]
                   + [resident, _rows(TM, D_MODEL), g_spec, _rows(TM, D_MODEL), _rows(TM, D_MODEL)],
                   out_specs=(_rows(TM, D_MODEL), _whole((1, D_MODEL)), resident), name="inproj_bwd",
                   compiler_params=pltpu.CompilerParams(vmem_limit_bytes=VMEM_BIG))(*pieces, w, x, g, dres, h)


PAIR = 2 * HEAD_DIM
N_AUX = 3


def _own(shape, h):
    return lax.broadcasted_iota(jnp.int32, shape, len(shape) - 1) // HEAD_DIM == h


def _hi_dot(a, b):
    return jnp.dot(a, b, precision=_HI, preferred_element_type=F32)


def _tri(n, lower):
    r = lax.broadcasted_iota(jnp.int32, (n, n), 0)
    c = lax.broadcasted_iota(jnp.int32, (n, n), 1)
    return jnp.where(r >= c if lower else r <= c, 1.0, 0.0).astype(F32)


def _fox_cumsum(proj, b):
    L = proj.shape[0]

    def body(fl_ref, b_ref, c_ref, carry_sc):
        @pl.when(pl.program_id(0) == 0)
        def _():
            carry_sc[...] = jnp.zeros((1, PAIR), F32)

        lane = lax.broadcasted_iota(jnp.int32, (TM, PAIR), 1)
        lf = jnp.where(lane < FOX_HEADS, jax.nn.log_sigmoid(fl_ref[...] + b_ref[...]), 0.0)
        cs = _hi_dot(_tri(TM, True), lf) + carry_sc[...]
        c_ref[...] = cs
        carry_sc[...] = cs[TM - 1:TM, :]

    b, b_spec = _of_layer(b)
    return _pallas(body, out_shape=jax.ShapeDtypeStruct((L, PAIR), F32), grid=(L // TM,),
                   in_specs=[_rows(TM, PAIR, O_FL // PAIR), b_spec], out_specs=_rows(TM, PAIR),
                   scratch_shapes=[pltpu.VMEM((1, PAIR), F32)], name="fox_cumsum")(proj, b)


def _fox_prep(proj, c):
    L = proj.shape[0]

    def body(q_ref, k_ref, v_ref, c_ref, qa_ref, ka_ref, kat_ref, vt_ref):
        lane = lax.broadcasted_iota(jnp.int32, (TM, PAIR), 1)
        cv = c_ref[...]
        for p in range(FOX_HEADS // 2):
            cols = slice(PAIR * p, PAIR * (p + 1))
            q2, k2 = q_ref[:, cols], k_ref[:, cols]
            vt_ref[p] = v_ref[:, cols].T.astype(_MXU)
            for e in range(2):
                h = 2 * p + e
                own = lane // HEAD_DIM == e
                a = lane - (HEAD_DIM if e == 0 else 0)
                pick = (lax.broadcasted_iota(jnp.int32, (PAIR, PAIR), 0) == h).astype(F32)
                rest = _hi_dot(cv, pick)
                aux_q = jnp.where((a >= N_AUX) & (a < 2 * N_AUX), 1.0, 0.0)
                aux_k = jnp.where((a >= 0) & (a < N_AUX), 1.0, 0.0)
                for n in range(N_AUX):
                    part = rest.astype(_MXU).astype(F32)
                    rest = rest - part
                    aux_q = jnp.where(a == n, part, aux_q)
                    aux_k = jnp.where(a == N_AUX + n, -part, aux_k)
                ka = jnp.where(own, k2, aux_k)
                qa_ref[h] = jnp.where(own, q2 * (1.0 / math.sqrt(HEAD_DIM)), aux_q).astype(_MXU)
                ka_ref[h] = ka.astype(_MXU)
                kat_ref[h] = ka.T.astype(_MXU)

    hl = jax.ShapeDtypeStruct((FOX_HEADS, L, PAIR), _MXU)
    nat = lambda o: _rows(TM, D_FOX, o // D_FOX)
    rows = pl.BlockSpec((FOX_HEADS, TM, PAIR), lambda i: (0, i, 0))
    return _pallas(
        body, out_shape=(hl, hl, jax.ShapeDtypeStruct((FOX_HEADS, PAIR, L), _MXU),
                         jax.ShapeDtypeStruct((FOX_HEADS // 2, PAIR, L), _MXU)),
        grid=(L // TM,), in_specs=[nat(O_FQ), nat(O_FK), nat(O_FV), _rows(TM, PAIR)],
        out_specs=(rows, rows, pl.BlockSpec((FOX_HEADS, PAIR, TM), lambda i: (0, 0, i)),
                   pl.BlockSpec((FOX_HEADS // 2, PAIR, TM), lambda i: (0, 0, i))),
        name="fox_prep")(proj, proj, proj, c)


def _key_le_query(tq):
    return lax.broadcasted_iota(jnp.int32, (tq, tq), 0) <= lax.broadcasted_iota(jnp.int32, (tq, tq), 1)


def _grid_ends(n0, n1):
    first = lambda: (pl.program_id(0) == 0) & (pl.program_id(1) == 0)
    last = lambda: (pl.program_id(0) == n0 - 1) & (pl.program_id(1) == n1 - 1)
    return first, last


def _fox_fwd(qa, ka, vt, ride=None):
    H, L, _ = qa.shape
    tq = min(TQ, L)
    nq = L // tq

    def body(qa_ref, ka_ref, vt_ref, o_ref, lse_ref, m_sc, l_sc, acc_sc):
        i = pl.program_id(1)
        m_sc[...] = jnp.full((2, 1, tq), NEG, F32)
        l_sc[...] = jnp.zeros((2, 1, tq), F32)
        acc_sc[...] = jnp.zeros((2, HEAD_DIM, tq), F32)

        def block(j, nk, masked):
            keys = pl.ds(pl.multiple_of(j * tq, tq), nk * tq)
            vt_blk = vt_ref[:, keys]
            sts = [_dot(ka_ref[e, keys, :], qa_ref[e], _NT) for e in range(2)]
            pts, alphas = [], []
            for e in range(2):
                st = jnp.where(_key_le_query(tq), sts[e], NEG) if masked else sts[e]
                m_prev = m_sc[e]
                m_new = jnp.maximum(m_prev, jnp.max(st, axis=0, keepdims=True))
                alphas.append(jnp.exp(m_prev - m_new))
                pt = jnp.exp(st - m_new)
                l_sc[e] = alphas[e] * l_sc[e] + jnp.sum(pt, axis=0, keepdims=True)
                m_sc[e] = m_new
                pts.append(pt.astype(_MXU))
            for e in range(2):
                acc_sc[e] = alphas[e] * acc_sc[e] + _dot(vt_blk[HEAD_DIM * e:HEAD_DIM * (e + 1)], pts[e])

        def two_blocks(jj, carry):
            block(2 * jj, 2, False)
            return carry

        lax.fori_loop(0, i // 2, two_blocks, 0)

        @pl.when(i % 2 == 1)
        def _():
            block(i - 1, 1, False)

        block(i, 1, True)
        o_ref[...] = jnp.concatenate([acc_sc[0] / l_sc[0], acc_sc[1] / l_sc[1]], axis=0).T
        for e in range(2):
            lse_ref[e] = m_sc[e] + jnp.log(l_sc[e])

    body, ex_in, ex_specs, ex_out, ex_sems = _riding(body, 3, 2, ride, *_grid_ends(H // 2, nq))
    res = _pallas(
        body, out_shape=(jax.ShapeDtypeStruct((L, D_FOX), F32), jax.ShapeDtypeStruct((H, 1, L), F32), *ex_out),
        grid=(H // 2, nq),
        in_specs=[pl.BlockSpec((2, tq, PAIR), lambda p, i: (p, i, 0)), pl.BlockSpec((2, L, PAIR), lambda p, i: (p, 0, 0)),
                  pl.BlockSpec((None, PAIR, L), lambda p, i: (p, 0, 0)), *ex_specs],
        out_specs=(pl.BlockSpec((tq, PAIR), lambda p, i: (i, p)), pl.BlockSpec((2, 1, tq), lambda p, i: (p, 0, i)),
                   *ex_specs),
        scratch_shapes=[pltpu.VMEM((2, 1, tq), F32), pltpu.VMEM((2, 1, tq), F32), pltpu.VMEM((2, HEAD_DIM, tq), F32), *ex_sems],
        name="fox_fwd" if ride is None else "fox_fwd_gather")(qa, ka, vt, *ex_in)
    return res[0], res[1], list(res[2:])


def _fox_bwd(qa, ka, kat, proj, do, o, lse, ride=None):
    H, L, _ = qa.shape
    tq = min(TQ, L)
    nq = L // tq

    def body(qa_ref, ka_ref, kat_ref, v_ref, do_ref, o_ref, lse_ref, dqt_ref, dk_ref, dv_ref, delta_sc, dk_sc, dv_sc):
        j = pl.program_id(1)

        @pl.when(j == 0)
        def _():
            head_rows = (lax.broadcasted_iota(jnp.int32, (8, PAIR), 1) // HEAD_DIM
                         == lax.broadcasted_iota(jnp.int32, (8, PAIR), 0)).astype(F32)
            delta_sc[...] = lax.dot_general(head_rows, do_ref[...] * o_ref[...], _NT, precision=_HI,
                                            preferred_element_type=F32)
            dqt_ref[...] = jnp.zeros((2, PAIR, L), F32)

        dk_sc[...] = jnp.zeros((2, tq, PAIR), F32)
        dv_sc[...] = jnp.zeros((tq, PAIR), F32)
        vb = v_ref[...]

        def block(i, masked):
            qs = pl.ds(pl.multiple_of(i * tq, tq), tq)
            dob = do_ref[qs, :]
            for e in range(2):
                own = _own((tq, PAIR), e)
                qh = qa_ref[e, qs, :]
                pt = jnp.exp(_dot(ka_ref[e], qh, _NT) - lse_ref[e, :, qs])
                if masked:
                    pt = jnp.where(_key_le_query(tq), pt, 0.0)
                dv_sc[...] += _dot(pt, jnp.where(own, dob, 0.0))
                dpt = _dot(jnp.where(own, vb, 0.0), dob, _NT)
                ds = (pt * (dpt - delta_sc[e:e + 1, qs])).astype(_MXU)
                dk_sc[e] += _dot(ds, qh)
                dqt_ref[e, :, qs] += _dot(kat_ref[e], ds)

        def off_diagonal(i, carry):
            block(i, False)
            return carry

        block(j, True)
        lax.fori_loop(j + 1, nq, off_diagonal, 0)
        dk_ref[...] = dk_sc[...]
        dv_ref[...] = dv_sc[...]

    nat = pl.BlockSpec((L, PAIR), lambda p, j: (0, p))
    body, ex_in, ex_specs, ex_out, ex_sems = _riding(body, 7, 3, ride, *_grid_ends(H // 2, nq))
    res = _pallas(
        body, out_shape=(jax.ShapeDtypeStruct((H, PAIR, L), F32), jax.ShapeDtypeStruct((H, L, PAIR), F32),
                         jax.ShapeDtypeStruct((L, D_FOX), F32), *ex_out),
        grid=(H // 2, nq),
        in_specs=[pl.BlockSpec((2, L, PAIR), lambda p, j: (p, 0, 0)), pl.BlockSpec((2, tq, PAIR), lambda p, j: (p, j, 0)),
                  pl.BlockSpec((2, PAIR, tq), lambda p, j: (p, 0, j)),
                  pl.BlockSpec((tq, PAIR), lambda p, j: (j, O_FV // PAIR + p)), nat, nat,
                  pl.BlockSpec((2, 1, L), lambda p, j: (p, 0, 0)), *ex_specs],
        out_specs=(pl.BlockSpec((2, PAIR, L), lambda p, j: (p, 0, 0)), pl.BlockSpec((2, tq, PAIR), lambda p, j: (p, j, 0)),
                   pl.BlockSpec((tq, PAIR), lambda p, j: (j, p)), *ex_specs),
        scratch_shapes=[pltpu.VMEM((8, L), F32), pltpu.VMEM((2, tq, PAIR), F32), pltpu.VMEM((tq, PAIR), F32), *ex_sems],
        name="fox_bwd" if ride is None else "fox_bwd_exchange",
        compiler_params=pltpu.CompilerParams(vmem_limit_bytes=VMEM_BIG))(qa, ka, kat, proj, do, o, lse, *ex_in)
    return res[0], res[1], res[2], list(res[3:])


def _fox_post_bwd(dqt, dkraw, proj, b):
    L = proj.shape[0]
    nb = L // TM

    def body(dqt_ref, dkr_ref, fl_ref, b_ref, dq_ref, dk_ref, dfl_ref, db_ref, carry_sc):
        first = pl.program_id(0) == 0

        @pl.when(first)
        def _():
            carry_sc[...] = jnp.zeros((1, PAIR), F32)

        lane = lax.broadcasted_iota(jnp.int32, (TM, PAIR), 1)
        rr = lax.broadcasted_iota(jnp.int32, (PAIR, PAIR), 0)
        cc = lax.broadcasted_iota(jnp.int32, (PAIR, PAIR), 1)
        dc = jnp.zeros((TM, PAIR), F32)
        for p in range(FOX_HEADS // 2):
            cols = slice(PAIR * p, PAIR * (p + 1))
            dqs = [dqt_ref[2 * p + e].T for e in range(2)]
            dks = [dkr_ref[2 * p + e] for e in range(2)]
            dq_ref[:, cols] = jnp.where(lane < HEAD_DIM, dqs[0], dqs[1]) * (1.0 / math.sqrt(HEAD_DIM))
            dk_ref[:, cols] = jnp.where(lane < HEAD_DIM, dks[0], dks[1])
            sums = jnp.zeros((TM, PAIR), F32)
            place = jnp.zeros((PAIR, PAIR), F32)
            for e in range(2):
                base = HEAD_DIM if e == 0 else 0
                sums = jnp.where(lane == base, dqs[e], jnp.where(lane == base + N_AUX, -dks[e], sums))
                place = jnp.where(((rr == base) | (rr == base + N_AUX)) & (cc == 2 * p + e), 1.0, place)
            dc = dc + _hi_dot(sums, place)
        rs = _hi_dot(_tri(TM, False), dc) + carry_sc[...]
        carry_sc[...] = rs[0:1, :]
        dfl = jnp.where(lane < FOX_HEADS, rs * jax.nn.sigmoid(-(fl_ref[...] + b_ref[...])), 0.0)
        dfl_ref[...] = dfl
        db = jnp.sum(dfl, axis=0, keepdims=True)

        @pl.when(first)
        def _():
            db_ref[...] = db

        @pl.when(jnp.logical_not(first))
        def _():
            db_ref[...] += db

    rev = lambda i: nb - 1 - i
    b, b_spec = _of_layer(b)
    nat = pl.BlockSpec((TM, D_FOX), lambda i: (rev(i), 0))
    return _pallas(
        body, out_shape=(jax.ShapeDtypeStruct((L, D_FOX), F32),) * 2
        + (jax.ShapeDtypeStruct((L, PAIR), F32), jax.ShapeDtypeStruct((1, PAIR), F32)),
        grid=(nb,),
        in_specs=[pl.BlockSpec((FOX_HEADS, PAIR, TM), lambda i: (0, 0, rev(i))),
                  pl.BlockSpec((FOX_HEADS, TM, PAIR), lambda i: (0, rev(i), 0)),
                  pl.BlockSpec((TM, PAIR), lambda i: (rev(i), O_FL // PAIR)), b_spec],
        out_specs=(nat, nat, pl.BlockSpec((TM, PAIR), lambda i: (rev(i), 0)), _whole((1, PAIR))),
        scratch_shapes=[pltpu.VMEM((1, PAIR), F32)], name="fox_post_bwd")(dqt, dkraw, proj, b)


def _s5_expand():
    r = lax.broadcasted_iota(jnp.int32, (S5_STATE, S5_STATE * S5_GROUP_CH), 0)
    c = lax.broadcasted_iota(jnp.int32, (S5_STATE, S5_STATE * S5_GROUP_CH), 1)
    return jnp.where(c // S5_GROUP_CH == r, 1.0, 0.0).astype(F32)


def _s5_disc_math(ar, ai, ldt, br, bi):
    dt = jnp.exp(ldt)
    mag = jnp.exp(ar * dt)
    lr = mag * jnp.cos(ai * dt)
    li = mag * jnp.sin(ai * dt)
    den = ar * ar + ai * ai
    fr = ((lr - 1.0) * ar + li * ai) / den
    fi = (li * ar - (lr - 1.0) * ai) / den
    e = _s5_expand()
    fre = jnp.dot(fr, e, precision=_HI, preferred_element_type=F32)
    fie = jnp.dot(fi, e, precision=_HI, preferred_element_type=F32)
    return lr, li, fre * br - fie * bi, fre * bi + fie * br


def _layer_blocks(arrs):
    return [pl.BlockSpec((None,) + a.shape[1:], lambda l: (l, 0, 0)) for a in arrs]


def _s5_disc(ar, ai, ldt, br, bi):
    def body(ar_ref, ai_ref, ldt_ref, br_ref, bi_ref, lr_ref, li_ref, bbr_ref, bbi_ref):
        lr, li, bbr, bbi = _s5_disc_math(ar_ref[...], ai_ref[...], ldt_ref[...], br_ref[...], bi_ref[...])
        lr_ref[...] = lr
        li_ref[...] = li
        bbr_ref[...] = bbr
        bbi_ref[...] = bbi

    ins = (ar, ai, ldt, br, bi)
    outs = (ar, ai, br, bi)
    return _pallas(body, out_shape=tuple(jax.ShapeDtypeStruct(a.shape, F32) for a in outs), grid=(DEPTH,),
                   in_specs=_layer_blocks(ins), out_specs=tuple(_layer_blocks(outs)), name="s5_disc")(*ins)


def _s5_disc_bwd(ar, ai, ldt, br, bi, dlr, dli, dbbr, dbbi):
    def body(ar_ref, ai_ref, ldt_ref, br_ref, bi_ref, dlr_ref, dli_ref, dbbr_ref, dbbi_ref,
             dar_ref, dai_ref, dldt_ref, dbr_ref, dbi_ref):
        _, vjp = jax.vjp(_s5_disc_math, ar_ref[...], ai_ref[...], ldt_ref[...], br_ref[...], bi_ref[...])
        dar, dai, dldt, dbr, dbi = vjp((dlr_ref[...], dli_ref[...], dbbr_ref[...], dbbi_ref[...]))
        dar_ref[...] = dar
        dai_ref[...] = dai
        dldt_ref[...] = dldt
        dbr_ref[...] = dbr
        dbi_ref[...] = dbi

    ins = (ar, ai, ldt, br, bi, dlr, dli, dbbr, dbbi)
    outs = (ar, ai, ldt, br, bi)
    return _pallas(body, out_shape=tuple(jax.ShapeDtypeStruct(a.shape, F32) for a in outs), grid=(DEPTH,),
                   in_specs=_layer_blocks(ins), out_specs=tuple(_layer_blocks(outs)), name="s5_disc_bwd")(*ins)


SLAB = 2 * S5_CH // 128


def _slab_rows(s, ts):
    return pl.ds(s, ts, stride=SLAB)


def _slab_pair(ref, s, ts):
    return jnp.concatenate([ref[_slab_rows(s, ts), :].astype(_MXU), ref[_slab_rows(s + 1, ts), :].astype(_MXU)], axis=-1)


def _s5_fwd(proj, wb, wc, lam, d, w_glu):
    L = proj.shape[0]
    ts = min(TS, L)

    def body(u_ref, wb_ref, wc_ref, lam_ref, d_ref, wg_ref, xs_ref, ypre_ref, ys_ref, b_sc, c_sc):
        @pl.when(pl.program_id(0) == 0)
        def _():
            c_sc[...] = jnp.zeros((SLAB, 128), F32)

        u = u_ref[...]
        ub = u.astype(_MXU)
        for s in range(0, SLAB, 2):
            b2 = _dot(ub, wb_ref[:, 128 * s:128 * (s + 2)])
            b_sc[_slab_rows(s, ts), :] = b2[:, :128]
            b_sc[_slab_rows(s + 1, ts), :] = b2[:, 128:]
        lr, li = lam_ref[0:8, :], lam_ref[8:16, :]

        def step(t, carry):
            xr, xi = carry
            row = pl.multiple_of(t * SLAB, SLAB)
            nr = lr * xr - li * xi + b_sc[pl.ds(row, 8), :]
            ni = lr * xi + li * xr + b_sc[pl.ds(row + 8, 8), :]
            xs_ref[pl.ds(row, 8), :] = nr
            xs_ref[pl.ds(row + 8, 8), :] = ni
            return nr, ni

        xr, xi = lax.fori_loop(0, ts, step, (c_sc[0:8, :], c_sc[8:16, :]), unroll=8)
        c_sc[0:8, :] = xr
        c_sc[8:16, :] = xi
        y = jnp.zeros((ts, D_S5), F32)
        for s in range(0, SLAB, 2):
            y = y + _dot(_slab_pair(xs_ref, s, ts), wc_ref[128 * s:128 * (s + 2), :])
        ypre_ref[...] = y
        y1 = jax.nn.gelu(y + d_ref[...] * u)
        ys_ref[...] = y1 * jax.nn.sigmoid(_dot(y1, wg_ref[...]))

    row = _rows(ts, D_S5)
    slabs = pl.BlockSpec((ts * SLAB, 128), lambda n: (n, 0))
    (wb, wb_spec), (wc, wc_spec), (lam, lam_spec), (d, d_spec) = (_of_layer(a) for a in (wb, wc, lam, d))
    return _pallas(
        body, out_shape=(jax.ShapeDtypeStruct((L * SLAB, 128), F32), jax.ShapeDtypeStruct((L, D_S5), F32),
                         jax.ShapeDtypeStruct((L, D_S5), F32)),
        grid=(L // ts,),
        in_specs=[_rows(ts, D_S5, O_SU // D_S5), wb_spec, wc_spec, lam_spec, d_spec, _whole((D_S5, D_S5))],
        out_specs=(slabs, row, row),
        scratch_shapes=[pltpu.VMEM((ts * SLAB, 128), F32), pltpu.VMEM((SLAB, 128), F32)], name="s5_fwd")(
            proj, wb, wc, lam, d, w_glu)


def _s5_bwd(proj, ypre, dys, xs, wb, wc, lam, d, w_glu):
    L = proj.shape[0]
    ts = min(TS, L)
    nb = L // ts

    def body(u_ref, y_ref, dys_ref, xs_ref, xp_ref, wb_ref, wc_ref, lam_ref, d_ref, wg_ref,
             du_ref, dwb_ref, dwc_ref, dlam_ref, dd_ref, dwg_ref, dx_sc, g_sc, c_sc):
        n = pl.program_id(0)

        @pl.when(n == 0)
        def _():
            c_sc[...] = jnp.zeros((SLAB, 128), F32)
            dlam_ref[...] = jnp.zeros((SLAB, 128), F32)
            dwb_ref[...] = jnp.zeros((D_S5, 2 * S5_CH), F32)
            dwc_ref[...] = jnp.zeros((2 * S5_CH, D_S5), F32)
            dd_ref[...] = jnp.zeros((1, D_S5), F32)
            dwg_ref[...] = jnp.zeros((D_S5, D_S5), F32)

        u, dv, dout = u_ref[...], d_ref[...], dys_ref[...]
        y1, gelu_vjp = jax.vjp(jax.nn.gelu, y_ref[...] + dv * u)
        sg = jax.nn.sigmoid(_dot(y1, wg_ref[...]))
        dz = dout * y1 * sg * (1.0 - sg)
        dy, = gelu_vjp(dout * sg + _dot(dz, wg_ref[...], _NT))
        dd_ref[...] += jnp.sum(dy * u, axis=0, keepdims=True)
        dwg_ref[...] += _dot(y1, dz, _TN)
        dyb = dy.astype(_MXU)
        for s in range(0, SLAB, 2):
            cols = slice(128 * s, 128 * (s + 2))
            dx2 = _dot(dyb, wc_ref[cols, :], _NT)
            dx_sc[_slab_rows(s, ts), :] = dx2[:, :128]
            dx_sc[_slab_rows(s + 1, ts), :] = dx2[:, 128:]
            dwc_ref[cols, :] += _dot(_slab_pair(xs_ref, s, ts), dyb, _TN)
        lr, li = lam_ref[0:8, :], lam_ref[8:16, :]

        def adjoint(row, pr, pi, carry):
            gr, gi, ar, ai = carry
            nr = dx_sc[pl.ds(row, 8), :] + lr * gr + li * gi
            ni = dx_sc[pl.ds(row + 8, 8), :] - li * gr + lr * gi
            g_sc[pl.ds(row, 8), :] = nr
            g_sc[pl.ds(row + 8, 8), :] = ni
            return nr, ni, ar + nr * pr + ni * pi, ai - nr * pi + ni * pr

        def step(k, carry):
            row = pl.multiple_of((ts - 1 - k) * SLAB, SLAB)
            prev = pl.multiple_of((ts - 2 - k) * SLAB, SLAB)
            return adjoint(row, xs_ref[pl.ds(prev, 8), :], xs_ref[pl.ds(prev + 8, 8), :], carry)

        z = jnp.zeros((8, 128), F32)
        carry = lax.fori_loop(0, ts - 1, step, (c_sc[0:8, :], c_sc[8:16, :], z, z), unroll=8)
        has_prev = jnp.where(n == nb - 1, 0.0, 1.0)
        gr, gi, ar, ai = adjoint(0, xp_ref[0:8, :] * has_prev, xp_ref[8:16, :] * has_prev, carry)
        c_sc[0:8, :] = gr
        c_sc[8:16, :] = gi
        dlam_ref[0:8, :] += ar
        dlam_ref[8:16, :] += ai
        ub = u.astype(_MXU)
        du = dy * dv
        for s in range(0, SLAB, 2):
            cols = slice(128 * s, 128 * (s + 2))
            gs = _slab_pair(g_sc, s, ts)
            du = du + _dot(gs, wb_ref[:, cols], _NT)
            dwb_ref[:, cols] += _dot(ub, gs, _TN)
        du_ref[...] = du

    blk = lambda n: nb - 1 - n
    row = pl.BlockSpec((ts, D_S5), lambda n: (blk(n), 0))
    (wb, wb_spec), (wc, wc_spec), (lam, lam_spec), (d, d_spec) = (_of_layer(a) for a in (wb, wc, lam, d))
    return _pallas(
        body, out_shape=(jax.ShapeDtypeStruct((L, D_S5), F32), jax.ShapeDtypeStruct((D_S5, 2 * S5_CH), F32),
                         jax.ShapeDtypeStruct((2 * S5_CH, D_S5), F32), jax.ShapeDtypeStruct((SLAB, 128), F32),
                         jax.ShapeDtypeStruct((1, D_S5), F32), jax.ShapeDtypeStruct((D_S5, D_S5), F32)),
        grid=(nb,),
        in_specs=[pl.BlockSpec((ts, D_S5), lambda n: (blk(n), O_SU // D_S5)), row, row,
                  pl.BlockSpec((ts * SLAB, 128), lambda n: (blk(n), 0)),
                  pl.BlockSpec((SLAB, 128), lambda n: (jnp.maximum(blk(n) * ts - 1, 0), 0)),
                  wb_spec, wc_spec, lam_spec, d_spec, _whole((D_S5, D_S5))],
        out_specs=(row, _whole((D_S5, 2 * S5_CH)), _whole((2 * S5_CH, D_S5)), _whole((SLAB, 128)), _whole((1, D_S5)),
                   _whole((D_S5, D_S5))),
        scratch_shapes=[pltpu.VMEM((ts * SLAB, 128), F32), pltpu.VMEM((ts * SLAB, 128), F32), pltpu.VMEM((SLAB, 128), F32)],
        name="s5_bwd")(proj, ypre, dys, xs, xs, wb, wc, lam, d, w_glu)


def _rot(z, cos, sin):
    lane = lax.broadcasted_iota(jnp.int32, z.shape, 1)
    zs = z * sin
    half = HEAD_DIM // 2
    return z * cos + jnp.where(lane % HEAD_DIM < half, pltpu.roll(zs, PAIR - half, 1), pltpu.roll(zs, half, 1))


def _head_avg():
    r = lax.broadcasted_iota(jnp.int32, (PAIR, PAIR), 0) // HEAD_DIM
    c = lax.broadcasted_iota(jnp.int32, (PAIR, PAIR), 1) // HEAD_DIM
    return jnp.where(r == c, 1.0 / HEAD_DIM, 0.0).astype(F32)


def _ret_tables(tq):
    lg = jnp.log1p(-(2.0 ** (-5.0 - jnp.arange(RET_HEADS, dtype=F32))))
    scale = 1.0 / math.sqrt(HEAD_DIM)
    pos = jnp.arange(tq)
    n = pos.astype(F32)
    dist = jnp.abs(n[:, None] - n[None, :])
    ok = (pos[None, :] // CHUNK) <= (pos[:, None] // CHUNK)
    w = jnp.where(ok[None], scale * jnp.exp(lg[:, None, None] * dist[None]), 0.0)
    lgl = jnp.repeat(lg, HEAD_DIM)
    dq_tab = scale * jnp.exp(lgl[None, :] * (n[:, None] + 1.0))
    dk_tab = jnp.exp(lgl[None, :] * (tq - 1.0 - n[:, None]))
    blk = jnp.arange(PAIR) // HEAD_DIM
    bd = (blk[:, None] == blk[None, :]).astype(F32)
    gbd = bd[None] * jnp.exp(lgl.reshape(RET_HEADS // 2, PAIR)[:, :, None] * tq)
    return dict(w=w, wt=w.transpose(0, 2, 1), dq=dq_tab, dk=dk_tab, gbd=gbd, bd=bd)


def _ret_specs(tq, nq, rev, layer):
    blk = (lambda i: nq - 1 - i) if rev else (lambda i: i)
    col = lambda o: pl.BlockSpec((tq, PAIR), lambda p, i: (blk(i), o // PAIR + p))
    return dict(
        rq=col(O_RQ), rk=col(O_RK), rv=col(O_RV), nat=col(0),
        w=pl.BlockSpec((2, tq, tq), lambda p, i: (p, 0, 0)), tab=pl.BlockSpec((tq, PAIR), lambda p, i: (0, p)),
        gbd=pl.BlockSpec((None, PAIR, PAIR), lambda p, i: (p, 0, 0)), bd=pl.BlockSpec((PAIR, PAIR), lambda p, i: (0, 0)),
        gn=pl.BlockSpec((None, 1, PAIR), lambda p, i: (layer, 0, p)), dgn=pl.BlockSpec((1, PAIR), lambda p, i: (0, p)),
        st=pl.BlockSpec((None, None, PAIR, PAIR), lambda p, i: (p, blk(i), 0, 0)))


def _ret_fwd(proj, cos_t, sin_t, tabs, gn):
    L = proj.shape[0]
    tq = tabs["w"].shape[1]
    nq = L // tq

    def body(rq_ref, rk_ref, rv_ref, cos_ref, sin_ref, w_ref, dqt_ref, dkt_ref, gbd_ref, bd_ref, gn_ref,
             o_ref, y_ref, st_ref, s_sc):
        @pl.when(pl.program_id(1) == 0)
        def _():
            s_sc[...] = jnp.zeros((PAIR, PAIR), F32)

        state = s_sc[...]
        st_ref[...] = state
        cos, sin = cos_ref[...], sin_ref[...]
        q2, k2, v2 = _rot(rq_ref[...], cos, sin), _rot(rk_ref[...], cos, sin), rv_ref[...]
        o = _dot(q2 * dqt_ref[...], state)
        for h in range(2):
            own = _own((tq, PAIR), h)
            a = _dot(jnp.where(own, q2, 0.0), k2, _NT) * w_ref[h]
            o = o + _dot(a, jnp.where(own, v2, 0.0))
        s_sc[...] = gbd_ref[...] * state + bd_ref[...] * _dot(k2 * dkt_ref[...], v2, _TN)
        o_ref[...] = o
        avg = _head_avg()
        oc = o - _hi_dot(o, avg)
        y_ref[...] = oc * lax.rsqrt(_hi_dot(oc * oc, avg) + EPS) * gn_ref[...]

    gn, layer = gn
    sp = _ret_specs(tq, nq, False, layer)
    nat = jax.ShapeDtypeStruct((L, D_RET), F32)
    return _pallas(
        body, out_shape=(nat, nat, jax.ShapeDtypeStruct((RET_HEADS // 2, nq, PAIR, PAIR), F32)), grid=(RET_HEADS // 2, nq),
        in_specs=[sp["rq"], sp["rk"], sp["rv"], sp["nat"], sp["nat"], sp["w"], sp["tab"], sp["tab"], sp["gbd"], sp["bd"],
                  sp["gn"]],
        out_specs=(sp["nat"], sp["nat"], sp["st"]), scratch_shapes=[pltpu.VMEM((PAIR, PAIR), F32)],
        name="ret_fwd")(proj, proj, proj, cos_t, sin_t, tabs["w"], tabs["dq"], tabs["dk"], tabs["gbd"], tabs["bd"], gn)


def _ret_bwd(proj, cos_t, sin_t, tabs, gn, o_pre, dy, states):
    L = proj.shape[0]
    tq = tabs["w"].shape[1]
    nq = L // tq

    def body(rq_ref, rk_ref, rv_ref, cos_ref, sin_ref, w_ref, wt_ref, dqt_ref, dkt_ref, gbd_ref, bd_ref, gn_ref,
             o_ref, dy_ref, st_ref, drq_ref, drk_ref, drv_ref, dgn_ref, g_sc):
        first = pl.program_id(1) == 0

        @pl.when(first)
        def _():
            g_sc[...] = jnp.zeros((PAIR, PAIR), F32)

        cos, sin = cos_ref[...], sin_ref[...]
        q2, k2, v2 = _rot(rq_ref[...], cos, sin), _rot(rk_ref[...], cos, sin), rv_ref[...]
        avg = _head_avg()
        ov, dyv = o_ref[...], dy_ref[...]
        oc = ov - _hi_dot(ov, avg)
        r = lax.rsqrt(_hi_dot(oc * oc, avg) + EPS)
        oh = oc * r
        dgn = jnp.sum(dyv * oh, axis=0, keepdims=True)
        doh = dyv * gn_ref[...]
        do = r * (doh - _hi_dot(doh, avg) - oh * _hi_dot(doh * oh, avg))
        state, g = st_ref[...], g_sc[...]
        dqt, dkt = dqt_ref[...], dkt_ref[...]
        dq = _dot(do, state, _NT) * dqt
        dk = _dot(v2, g, _NT) * dkt
        dv = _dot(k2 * dkt, g)
        g_sc[...] = gbd_ref[...] * g + bd_ref[...] * _dot(q2 * dqt, do, _TN)
        for h in range(2):
            own = _own((tq, PAIR), h)
            qm, dom = jnp.where(own, q2, 0.0), jnp.where(own, do, 0.0)
            dv = dv + _dot(_dot(k2, qm, _NT) * wt_ref[h], dom)
            dq = dq + _dot(_dot(dom, v2, _NT) * w_ref[h], jnp.where(own, k2, 0.0))
            dk = dk + _dot(_dot(v2, dom, _NT) * wt_ref[h], qm)
        drq_ref[...] = _rot(dq, cos, -sin)
        drk_ref[...] = _rot(dk, cos, -sin)
        drv_ref[...] = dv

        @pl.when(first)
        def _():
            dgn_ref[...] = dgn

        @pl.when(jnp.logical_not(first))
        def _():
            dgn_ref[...] += dgn

    gn, layer = gn
    sp = _ret_specs(tq, nq, True, layer)
    nat = jax.ShapeDtypeStruct((L, D_RET), F32)
    return _pallas(
        body, out_shape=(nat, nat, nat, jax.ShapeDtypeStruct((1, D_RET), F32)), grid=(RET_HEADS // 2, nq),
        in_specs=[sp["rq"], sp["rk"], sp["rv"], sp["nat"], sp["nat"], sp["w"], sp["w"], sp["tab"], sp["tab"], sp["gbd"],
                  sp["bd"], sp["gn"], sp["nat"], sp["nat"], sp["st"]],
        out_specs=(sp["nat"], sp["nat"], sp["nat"], sp["dgn"]), scratch_shapes=[pltpu.VMEM((PAIR, PAIR), F32)],
        name="ret_bwd")(proj, proj, proj, cos_t, sin_t, tabs["w"], tabs["wt"], tabs["dq"], tabs["dk"], tabs["gbd"],
                        tabs["bd"], gn, o_pre, dy, states)


def _gate_out(yf, ys, yr, proj, x, w):
    L = x.shape[0]

    def body(yf_ref, ys_ref, yr_ref, g_ref, x_ref, w_ref, xn_ref):
        cat = jnp.concatenate([yf_ref[...], ys_ref[...], yr_ref[...]], axis=-1)
        xn_ref[...] = x_ref[...] + _dot(cat * jax.nn.silu(g_ref[...]), w_ref[...])

    tm = min(TMB, L)
    full = _rows(tm, D_MODEL)
    return _pallas(body, out_shape=jax.ShapeDtypeStruct((L, D_MODEL), F32), grid=(L // tm,),
                   in_specs=[_rows(tm, D_FOX), _rows(tm, D_S5), _rows(tm, D_RET), _rows(tm, D_MODEL, O_GATE // D_MODEL),
                             full, _whole((D_MODEL, D_MODEL))],
                   out_specs=full, name="gate_out")(yf, ys, yr, proj, x, w)


def _gate_out_bwd(dxn, w, yf, ys, yr, proj):
    L = dxn.shape[0]

    def body(dx_ref, w_ref, yf_ref, ys_ref, yr_ref, g_ref, dyf_ref, dys_ref, dyr_ref, dg_ref, dw_ref):
        dxv = dx_ref[...].astype(_MXU)
        dy = _dot(dxv, w_ref[...], _NT)
        g = g_ref[...]
        sg = jax.nn.sigmoid(g)
        silu = g * sg
        dcat = dy * silu
        dyf_ref[...] = dcat[:, :D_FOX]
        dys_ref[...] = dcat[:, D_FOX:D_FOX + D_S5]
        dyr_ref[...] = dcat[:, D_FOX + D_S5:]
        cat = jnp.concatenate([yf_ref[...], ys_ref[...], yr_ref[...]], axis=-1)
        dg_ref[...] = dy * cat * (sg * (1.0 + g * (1.0 - sg)))
        dw = _dot(cat * silu, dxv, _TN)

        @pl.when(pl.program_id(0) == 0)
        def _():
            dw_ref[...] = dw

        @pl.when(pl.program_id(0) != 0)
        def _():
            dw_ref[...] += dw

    tm = min(TMB, L)
    full = _rows(tm, D_MODEL)
    f, s, r = _rows(tm, D_FOX), _rows(tm, D_S5), _rows(tm, D_RET)
    sq = _whole((D_MODEL, D_MODEL))
    return _pallas(body, out_shape=(jax.ShapeDtypeStruct((L, D_FOX), F32), jax.ShapeDtypeStruct((L, D_S5), F32),
                                    jax.ShapeDtypeStruct((L, D_RET), F32), jax.ShapeDtypeStruct((L, D_MODEL), F32),
                                    jax.ShapeDtypeStruct((D_MODEL, D_MODEL), F32)),
                   grid=(L // tm,), in_specs=[full, sq, f, s, r, _rows(tm, D_MODEL, O_GATE // D_MODEL)],
                   out_specs=(f, s, r, full, sq), name="gate_out_bwd",
                   compiler_params=pltpu.CompilerParams(vmem_limit_bytes=VMEM_BIG))(dxn, w, yf, ys, yr, proj)


def _final_loss(x, g, tgt):
    L = x.shape[0]

    def body(x_ref, g_ref, t_ref, loss_ref, dx_ref, dg_ref):
        xv, gv = x_ref[...], g_ref[...]
        r = lax.rsqrt(jnp.mean(xv * xv, axis=-1, keepdims=True) + EPS)
        err = xv * r * gv - t_ref[...]
        part = 0.5 * jnp.sum(jnp.mean(err * err, axis=-1, keepdims=True), axis=0, keepdims=True)
        dx, dg = _rms_bwd(xv, gv, err * (1.0 / D_MODEL))
        dx_ref[...] = dx

        @pl.when(pl.program_id(0) == 0)
        def _():
            loss_ref[...] = part
            dg_ref[...] = dg

        @pl.when(pl.program_id(0) != 0)
        def _():
            loss_ref[...] += part
            dg_ref[...] += dg

    full = _rows(TM, D_MODEL)
    return _pallas(body, out_shape=(jax.ShapeDtypeStruct((1, 1), F32), jax.ShapeDtypeStruct((L, D_MODEL), F32),
                                    jax.ShapeDtypeStruct((1, D_MODEL), F32)),
                   grid=(L // TM,), in_specs=[full, _whole((1, D_MODEL)), full],
                   out_specs=(_whole((1, 1)), full, _whole((1, D_MODEL))), name="final_loss")(x, g, tgt)


def _block_diag(blocks):
    n, g, r, c = blocks.shape
    eye = jnp.eye(g, dtype=blocks.dtype)
    return (blocks[:, :, :, None, :] * eye[None, :, None, :, None]).reshape(n, g * r, g * c)


def _diag_blocks(m, g):
    n, r, c = m.shape[0], m.shape[1] // g, m.shape[2] // g
    eye = jnp.eye(g, dtype=m.dtype)
    return jnp.sum(m.reshape(n, g, r, g, c) * eye[None, :, None, :, None], axis=3)


def _rope_tables(L):
    half = HEAD_DIM // 2
    freqs = ROPE_BASE ** (-jnp.arange(half, dtype=F32) / half)
    ang = jnp.arange(L, dtype=F32)[:, None] * freqs[None, :]
    cos, sin = jnp.cos(ang), jnp.sin(ang)
    cos_t = jnp.tile(jnp.concatenate([cos, cos], axis=-1), (1, RET_HEADS))
    sin_t = jnp.tile(jnp.concatenate([sin, -sin], axis=-1), (1, RET_HEADS))
    return cos_t, sin_t


def _s5_disc_args(small):
    g, s, ch = S5_GROUPS, S5_STATE, S5_GROUP_CH
    return (small["s5_a_re"], small["s5_a_im"], small["s5_log_dt"][:, :, None],
            small["s5_b_re"].reshape(DEPTH, g, s * ch), small["s5_b_im"].reshape(DEPTH, g, s * ch))


def _s5_mats(small):
    g, s, ch = S5_GROUPS, S5_STATE, S5_GROUP_CH
    lr, li, bbr, bbi = _s5_disc(*_s5_disc_args(small))
    lam = jnp.concatenate([lr.reshape(DEPTH, 8, 128), li.reshape(DEPTH, 8, 128)], axis=1)
    wb = jnp.concatenate([_block_diag(b.reshape(DEPTH, g, s, ch).transpose(0, 1, 3, 2)) for b in (bbr, bbi)], axis=2)
    wc = jnp.concatenate([_block_diag(c.transpose(0, 1, 3, 2)) for c in (small["s5_c_re"], -small["s5_c_im"])], axis=1)
    return lam, wb.astype(_MXU), wc.astype(_MXU)


def _s5_param_grads(small, dwb, dwc, dlam):
    g, s, ch = S5_GROUPS, S5_STATE, S5_GROUP_CH
    dc = [_diag_blocks(m, g).transpose(0, 1, 3, 2).reshape(DEPTH, g, ch * s) for m in (dwc[:, :S5_CH], dwc[:, S5_CH:])]
    dbb = [_diag_blocks(m, g).transpose(0, 1, 3, 2).reshape(DEPTH, g, s * ch) for m in (dwb[:, :, :S5_CH], dwb[:, :, S5_CH:])]
    dar, dai, dldt, dbr, dbi = _s5_disc_bwd(*_s5_disc_args(small), dlam[:, :8].reshape(DEPTH, g, s),
                                            dlam[:, 8:].reshape(DEPTH, g, s), dbb[0], dbb[1])
    return dict(s5_a_re=dar, s5_a_im=dai, s5_log_dt=dldt.reshape(DEPTH, g), s5_b_re=dbr, s5_b_im=dbi, s5_c_re=dc[0],
                s5_c_im=-dc[1])


_DENSE = ("s5_b_re", "s5_b_im", "s5_c_re", "s5_c_im")


def _layer_fwd(x, p, rope, ride=None, late=False):
    L = x.shape[0]
    cos_t, sin_t, ret_tabs = rope
    s = {"x": x}
    proj, h = _norm_inproj(x, p["norm_w"], p["w_in"])
    s["proj"], s["h"] = proj, h
    qa, ka, kat, vt = _fox_prep(proj, _fox_cumsum(proj, p["b_f"]))
    yf, lse, landed = _fox_fwd(qa, ka, vt, ride)
    s.update(qa=qa, ka=ka, kat=kat, lse=lse, yf=yf)
    if late:
        p["w_glu"], p["w_out"] = _gathered_rows(landed[-2]), _gathered_rows(landed[-1])
        landed = landed[:-2]
    xs, ypre, ys = _s5_fwd(proj, p["wb"], p["wc"], p["lam"], p["d"], p["w_glu"])
    s.update(xs=xs, ypre=ypre, ys=ys)
    o_pre, yr, states = _ret_fwd(proj, cos_t, sin_t, ret_tabs, p["gn_w"])
    s.update(o_pre=o_pre, yr=yr, states=states)
    return _gate_out(yf, ys, yr, proj, x, p["w_out"]), s, landed


def _layer_bwd(dxn, s, p, rope, ride=None, early=False):
    L = dxn.shape[0]
    cos_t, sin_t, ret_tabs = rope
    g = {}
    proj = s["proj"]
    dyf, dys, dyr, dgate, g["w_out"] = _gate_out_bwd(dxn, p["w_out"], s["yf"], s["ys"], s["yr"], proj)
    drq, drk, drv, dgn = _ret_bwd(proj, cos_t, sin_t, ret_tabs, p["gn_w"], s["o_pre"], dyr, s["states"])
    g["ret_gn_w"] = dgn.reshape(D_RET)
    dsu, g["wb"], g["wc"], g["lam"], dd, g["s5_w_glu"] = _s5_bwd(proj, s["ypre"], dys, s["xs"], p["wb"], p["wc"], p["lam"],
                                                                 p["d"], p["w_glu"])
    g["s5_d"] = dd.reshape(D_S5)
    if early:
        ride = (ride[0] + _row_slots(g), ride[1] + [True, True])
    dqt, dkraw, dv, landed = _fox_bwd(s["qa"], s["ka"], s["kat"], proj, dyf, s["yf"], s["lse"], ride)
    dq, dk, dfl, dbf = _fox_post_bwd(dqt, dkraw, proj, p["b_f"])
    g["fox_b_f"] = dbf[0, :FOX_HEADS]
    pieces = [dgate, dq, dk, dv, dsu, drq, drk, drv, dfl]
    dx, dnw, g["w_in"] = _inproj_bwd(pieces, p["w_in"], s["x"], p["norm_w"], dxn, s["h"])
    g["norm_w"] = dnw.reshape(D_MODEL)
    return dx, g, landed


def _stacked_params(small):
    lam, wb, wc = _s5_mats(small)
    row = lambda a: a[:, None, :]
    return dict(norm_w=row(small["norm_w"]), b_f=row(jnp.pad(small["fox_b_f"], ((0, 0), (0, PAIR - FOX_HEADS)))),
                lam=lam, wb=wb, wc=wc, d=row(small["s5_d"]), gn_w=row(small["ret_gn_w"]))


def _layer_params(l, w_in_p, w_glu, w_out, stacked):
    return dict({k: (a, l) for k, a in stacked.items()}, w_in=w_in_p, w_glu=w_glu, w_out=w_out)


_SHARDED = ("w_in", "s5_w_glu", "w_out")
_WIRE = jnp.bfloat16


_RUNS = ((2568, 3592, O_GATE), (0, 1536, O_FQ), (1544, 2568, O_SU), (1536, 1544, O_FL))


def _shard_pieces():
    out = []
    for a, b, pad in _RUNS:
        while a < b:
            j = a // W_SHARD
            e = min(b, (j + 1) * W_SHARD)
            out.append((j, a - j * W_SHARD, e - j * W_SHARD, pad))
            pad, a = pad + e - a, e
    return out


def _gathered_w_in(g_in):
    cols = [g_in[j, :, a:e] for j, a, e, _ in _shard_pieces()]
    cols.append(jnp.zeros((D_MODEL, D_INP - O_FL - FOX_HEADS), g_in.dtype))
    return jnp.concatenate(cols, axis=1)


def _gathered_rows(g):
    return g.reshape(-1, g.shape[-1])


def _w_in_slots(g):
    w_in = g["w_in"].astype(_WIRE)
    slots = []
    for j in range(N_DEV):
        mine = sorted((a, e, pad) for jj, a, e, pad in _shard_pieces() if jj == j)
        slots.append(jnp.concatenate([w_in[:, pad:pad + e - a] for a, e, pad in mine], axis=1))
    return jnp.stack(slots)


def _row_slots(g):
    return [g["s5_w_glu"].reshape(N_DEV, D_S5 // N_DEV, D_S5).astype(_WIRE),
            g["w_out"].reshape(N_DEV, D_MODEL // N_DEV, D_MODEL).astype(_WIRE)]


def _step_grads(x, tgt, small, full=None, shards=None):
    L = x.shape[0]
    rope = _rope_tables(L) + (_ret_tables(min(TQ, L)),)
    stacked = _stacked_params(small)
    if shards is not None:
        nxt = (_gathered_w_in(_exchange([shards[0][0]], [False], "gather_layer0")[0]), None, None)
    saved, params = [], []
    for l in range(DEPTH):
        weights = nxt if shards is not None else tuple(f[l] for f in full)
        ride = None
        if shards is not None:
            arrs = [s[l + 1] for s in shards] if l + 1 < DEPTH else []
            arrs += [shards[1][0], shards[2][0]] if l == 0 else []
            ride = (arrs, [False] * len(arrs)) if arrs else None
        params.append(_layer_params(l, *weights, stacked))
        x, s, landed = _layer_fwd(x, params[l], rope, ride, late=shards is not None and l == 0)
        if landed:
            nxt = (_gathered_w_in(landed[0]), _gathered_rows(landed[1]), _gathered_rows(landed[2]))
        saved.append(s)
    loss, dx, dfw = _final_loss(x, small["final_norm_w"][None], tgt)
    grads, partials, waiting = [None] * DEPTH, [None] * DEPTH, None
    for l in reversed(range(DEPTH)):
        ride = (waiting, [True] * len(waiting)) if waiting is not None else None
        dx, grads[l], landed = _layer_bwd(dx, saved[l], params[l], rope, ride, early=ride is not None and l == 0)
        if waiting is not None:
            partials[l + 1] = landed[:3]
        if shards is not None:
            waiting = [_w_in_slots(grads[l])] + _row_slots(grads[l])
    stack = lambda n: jnp.stack([g[n] for g in grads])
    small_g = {n: stack(n) for n in ("norm_w", "fox_b_f", "s5_d", "ret_gn_w")}
    small_g.update(_s5_param_grads(small, stack("wb"), stack("wc"), stack("lam")), final_norm_w=dfw)
    if shards is None:
        return loss, dx, grads, small_g
    last = _exchange([waiting[0]] + [small_g[n].astype(_WIRE) for n in _SMALL], [True] + [False] * len(_SMALL),
                     "exchange_layer0")
    partials[0] = [last[0]] + landed[3:]
    return loss, dx, grads, small_g, partials, dict(zip(_SMALL, last[1:]))


_MESH = pl.DeviceIdType.MESH
_ANY = pl.BlockSpec(memory_space=pl.ANY)


def _me_and_peers():
    x, y, c = lax.axis_index("x"), lax.axis_index("y"), lax.axis_index("c")
    flip = lambda a, bit: (1 - a) if bit else a
    peers = []
    for r in range(1, N_DEV):
        px, py, pc = flip(x, (r >> 2) & 1), flip(y, (r >> 1) & 1), flip(c, r & 1)
        peers.append(((px, py, pc), 4 * px + 2 * py + pc))
    return 4 * x + 2 * y + c, peers


def _exchange_copies(srcs, dsts, sems, scatter):
    send_sems, recv_sems, local_sems = sems
    me, peers = _me_and_peers()
    pick = lambda t, to: srcs[t].at[to] if scatter[t] else srcs[t]
    own = [pltpu.make_async_copy(pick(t, me), dsts[t].at[me], local_sems.at[t]) for t in range(len(srcs))]
    sends, waits = [], []
    for r, (dev, idx) in enumerate(peers):
        for t in range(len(srcs)):
            for land, out in ((me, sends), (idx, waits)):
                out.append(pltpu.make_async_remote_copy(pick(t, idx), dsts[t].at[land], send_sems.at[t, r], recv_sems.at[t, r],
                                                        device_id=dev, device_id_type=_MESH))
    return own, sends, waits


def _exchange_start(srcs, dsts, sems, scatter):
    own, sends, _ = _exchange_copies(srcs, dsts, sems, scatter)
    for cp in own + sends:
        cp.start()


def _exchange_wait(srcs, dsts, sems, scatter):
    own, _, waits = _exchange_copies(srcs, dsts, sems, scatter)
    for cp in waits + own:
        cp.wait()


def _exchange_shapes(arrs, scatter):
    outs = [jax.ShapeDtypeStruct(a.shape if sc else (N_DEV,) + a.shape, a.dtype) for a, sc in zip(arrs, scatter)]
    n = len(arrs)
    sems = [pltpu.SemaphoreType.DMA((n, N_DEV - 1)), pltpu.SemaphoreType.DMA((n, N_DEV - 1)), pltpu.SemaphoreType.DMA((n,))]
    return outs, sems


def _exchange(arrs, scatter, name):
    n = len(arrs)

    def body(*refs):
        _exchange_start(refs[:n], refs[n:2 * n], refs[2 * n:], scatter)
        _exchange_wait(refs[:n], refs[n:2 * n], refs[2 * n:], scatter)

    outs, sems = _exchange_shapes(arrs, scatter)
    return _pallas(body, out_shape=tuple(outs), in_specs=[_ANY] * n, out_specs=tuple([_ANY] * n), scratch_shapes=sems,
                   name=name)(*arrs)


def _riding(body, n_in, n_out, ride, is_first, is_last):
    if ride is None:
        return body, [], [], [], []
    arrs, scatter = ride
    n = len(arrs)
    outs, sems = _exchange_shapes(arrs, scatter)

    def wrapped(*refs):
        ins, srcs = refs[:n_in], refs[n_in:n_in + n]
        own_outs, dsts = refs[n_in + n:n_in + n + n_out], refs[n_in + n + n_out:n_in + 2 * n + n_out]
        scratch, ex_sems = refs[n_in + 2 * n + n_out:-3], refs[-3:]

        @pl.when(is_first())
        def _():
            _exchange_start(srcs, dsts, ex_sems, scatter)

        body(*ins, *own_outs, *scratch)

        @pl.when(is_last())
        def _():
            _exchange_wait(srcs, dsts, ex_sems, scatter)

    return wrapped, list(arrs), [_ANY] * n, outs, sems


def _adamw_body(p_ref, w_ref, m_ref, v_ref, g_ref, d_ref, nm_ref, nv_ref):
    g = p_ref[0].astype(F32)
    for i in range(1, N_DEV):
        g = g + p_ref[i].astype(F32)
    nm = ADAM_B1 * m_ref[...] + (1.0 - ADAM_B1) * g
    nv = ADAM_B2 * v_ref[...] + (1.0 - ADAM_B2) * jnp.square(g)
    m_hat = nm / (1.0 - ADAM_B1 ** ADAM_STEP)
    v_hat = nv / (1.0 - ADAM_B2 ** ADAM_STEP)
    g_ref[...] = g
    d_ref[...] = -ADAM_LR * (m_hat / (jnp.sqrt(v_hat) + ADAM_EPS) + ADAM_WD * w_ref[...])
    nm_ref[...] = nm
    nv_ref[...] = nv


def _adamw(parts, w, m, v, name):
    n, nb, rows, cols = parts.shape
    tm = next(t for t in (256, 128, 64, 32, 16) if rows % t == 0)

    def body(*refs):
        _adamw_body(*refs)

    row = pl.BlockSpec((None, tm, cols), lambda b, i: (b, i, 0))
    return _pallas(body, out_shape=(jax.ShapeDtypeStruct((nb, rows, cols), F32),) * 4, grid=(nb, rows // tm),
                   in_specs=[pl.BlockSpec((n, None, tm, cols), lambda b, i: (0, b, i, 0)), row, row, row],
                   out_specs=(row,) * 4, name=name)(parts, w, m, v)


def _adamw_whole(parts, w, m, v, name):
    def body(*refs):
        _adamw_body(*refs)

    if w.ndim == 2:
        grid = (1,)
        slab = pl.BlockSpec(w.shape, lambda b: (0, 0))
        part = pl.BlockSpec(parts.shape, lambda b: (0, 0, 0))
    else:
        grid, rest = (w.shape[0],), w.shape[1:]
        zeros = (0,) * len(rest)
        slab = pl.BlockSpec((None,) + rest, lambda b: (b,) + zeros)
        part = pl.BlockSpec((N_DEV, None) + rest, lambda b: (0, b) + zeros)
    return _pallas(body, out_shape=(jax.ShapeDtypeStruct(w.shape, F32),) * 4, grid=grid,
                   in_specs=[part, slab, slab, slab], out_specs=(slab,) * 4, name=name)(parts, w, m, v)


_WEIGHTS = ("norm_w", "w_in", "fox_b_f", "s5_a_re", "s5_a_im", "s5_b_re", "s5_b_im", "s5_c_re", "s5_c_im", "s5_d",
            "s5_log_dt", "s5_w_glu", "ret_gn_w", "w_out", "final_norm_w")
_SMALL = tuple(n for n in _WEIGHTS if n not in _SHARDED)


def kernel(x, norm_w, w_in, fox_b_f, s5_a_re, s5_a_im, s5_b_re, s5_b_im, s5_c_re, s5_c_im, s5_d, s5_log_dt, s5_w_glu, ret_gn_w, w_out, final_norm_w, loss_target, m_norm_w, m_w_in, m_fox_b_f, m_s5_a_re, m_s5_a_im, m_s5_b_re, m_s5_b_im, m_s5_c_re, m_s5_c_im, m_s5_d, m_s5_log_dt, m_s5_w_glu, m_ret_gn_w, m_w_out, m_final_norm_w, v_norm_w, v_w_in, v_fox_b_f, v_s5_a_re, v_s5_a_im, v_s5_b_re, v_s5_b_im, v_s5_c_re, v_s5_c_im, v_s5_d, v_s5_log_dt, v_s5_w_glu, v_ret_gn_w, v_w_out, v_final_norm_w):
    w = dict(norm_w=norm_w, w_in=w_in, fox_b_f=fox_b_f, s5_a_re=s5_a_re, s5_a_im=s5_a_im, s5_b_re=s5_b_re, s5_b_im=s5_b_im,
             s5_c_re=s5_c_re, s5_c_im=s5_c_im, s5_d=s5_d, s5_log_dt=s5_log_dt, s5_w_glu=s5_w_glu, ret_gn_w=ret_gn_w,
             w_out=w_out, final_norm_w=final_norm_w)
    m = dict(norm_w=m_norm_w, w_in=m_w_in, fox_b_f=m_fox_b_f, s5_a_re=m_s5_a_re, s5_a_im=m_s5_a_im, s5_b_re=m_s5_b_re,
             s5_b_im=m_s5_b_im, s5_c_re=m_s5_c_re, s5_c_im=m_s5_c_im, s5_d=m_s5_d, s5_log_dt=m_s5_log_dt,
             s5_w_glu=m_s5_w_glu, ret_gn_w=m_ret_gn_w, w_out=m_w_out, final_norm_w=m_final_norm_w)
    v = dict(norm_w=v_norm_w, w_in=v_w_in, fox_b_f=v_fox_b_f, s5_a_re=v_s5_a_re, s5_a_im=v_s5_a_im, s5_b_re=v_s5_b_re,
             s5_b_im=v_s5_b_im, s5_c_re=v_s5_c_re, s5_c_im=v_s5_c_im, s5_d=v_s5_d, s5_log_dt=v_s5_log_dt,
             s5_w_glu=v_s5_w_glu, ret_gn_w=v_ret_gn_w, w_out=v_w_out, final_norm_w=v_final_norm_w)

    small = {n: w[n] for n in _SMALL}
    loss, dx, _, _, partials, r_small = _step_grads(x[0], loss_target[0], small, shards=[w[n].astype(_MXU) for n in _SHARDED])

    res = {}
    for t, n in enumerate(_SHARDED):
        res[n] = _adamw(jnp.stack([partials[l][t] for l in range(DEPTH)], axis=1), w[n], m[n], v[n], "adamw_" + n)
    for n in _SMALL:
        shape = w[n].shape
        view = (1,) + shape if len(shape) == 1 else shape[:2] + (-1,) if n in _DENSE else shape
        outs = _adamw_whole(r_small[n], *[d[n].reshape(view) for d in (w, m, v)], "adamw_" + n)
        res[n] = [o.reshape(shape) for o in outs]

    loss = lax.psum(loss[0, 0], ("x", "y", "c"))
    return (loss, dx[None], *[res[n][0] for n in _WEIGHTS], *[res[n][1] for n in _WEIGHTS],
            *[res[n][2] for n in _WEIGHTS], *[res[n][3] for n in _WEIGHTS])
```

```python
import math

import jax
import jax.numpy as jnp
from jax import lax
from jax.experimental import pallas as pl
from jax.experimental.pallas import tpu as pltpu

F32 = jnp.float32
_MXU = jnp.bfloat16
_HI = lax.Precision.HIGHEST

N_DEV = 8
DEPTH = 4
D_MODEL = 1024
HEAD_DIM = 64
D_FOX = 512
FOX_HEADS = 8
D_S5 = 256
S5_GROUPS = 16
S5_GROUP_CH = 16
S5_STATE = 64
S5_CH = S5_GROUPS * S5_STATE
D_RET = 256
RET_HEADS = 4
CHUNK = 64
ROPE_BASE = 10000.0
EPS = 1e-6
D_IN = 3592
D_INP = 3712
W_SHARD = D_IN // N_DEV
O_GATE, O_FQ, O_FK, O_FV, O_SU, O_RQ, O_RK, O_RV, O_FL = 0, 1024, 1536, 2048, 2560, 2816, 3072, 3328, 3584

ADAM_LR, ADAM_B1, ADAM_B2, ADAM_EPS, ADAM_WD, ADAM_STEP = 0.001, 0.9, 0.999, 1e-08, 0.01, 10

TM = 256
TMB = 512
TQ = 512
TS = 512
NEG = -1e30
VMEM_BIG = 56 * 1024 * 1024


def _pallas(body, **kw):
    return pl.pallas_call(body, **kw)


def _whole(shape):
    n = len(shape)
    return pl.BlockSpec(shape, lambda *_: (0,) * n)


def _rows(tm, width, col=0):
    return pl.BlockSpec((tm, width), lambda i: (i, col))


def _of_layer(param):
    a, l = param
    return a, pl.BlockSpec((None,) + a.shape[1:], lambda *_: (l,) + (0,) * (a.ndim - 1))


def _dot(a, b, dims=(((1,), (0,)), ((), ()))):
    return lax.dot_general(a.astype(_MXU), b.astype(_MXU), dims, preferred_element_type=F32)


_NT = (((1,), (1,)), ((), ()))
_TN = (((0,), (0,)), ((), ()))


def _norm_inproj(x, g, w):
    L = x.shape[0]

    def body(x_ref, g_ref, w_ref, p_ref, h_ref):
        xv = x_ref[...]
        r = lax.rsqrt(jnp.mean(xv * xv, axis=-1, keepdims=True) + EPS)
        h = (xv * r * g_ref[...]).astype(_MXU)
        h_ref[...] = h
        p_ref[...] = _dot(h, w_ref[...])

    tm = min(TMB, L)
    g, g_spec = _of_layer(g)
    return _pallas(body, out_shape=(jax.ShapeDtypeStruct((L, D_INP), F32), jax.ShapeDtypeStruct((L, D_MODEL), _MXU)),
                   grid=(L // tm,),
                   in_specs=[_rows(tm, D_MODEL), g_spec,
                             pl.BlockSpec((D_MODEL, D_INP), lambda i: (0, 0), pipeline_mode=pl.Buffered(1))],
                   out_specs=(_rows(tm, D_INP), _rows(tm, D_MODEL)), name="norm_inproj",
                   compiler_params=pltpu.CompilerParams(vmem_limit_bytes=VMEM_BIG))(x, g, w)


def _rms_bwd(xv, g, dh):
    r = lax.rsqrt(jnp.mean(xv * xv, axis=-1, keepdims=True) + EPS)
    xh = xv * r
    dg = jnp.sum(dh * xh, axis=0, keepdims=True)
    dxh = dh * g
    dx = r * (dxh - xh * jnp.mean(dxh * xh, axis=-1, keepdims=True))
    return dx, dg


def _inproj_bwd(pieces, w, x, g, dres, h):
    L = x.shape[0]
    n = len(pieces)
    nb = L // TM

    def body(*refs):
        w_ref, x_ref, g_ref, dr_ref, h_ref, dx_ref, dg_ref, dw_ref, acc_sc = refs[n:]
        step = pl.program_id(0)

        @pl.when(step == 0)
        def _():
            acc_sc[...] = jnp.zeros((D_MODEL, D_INP), F32)
            dg_ref[...] = jnp.zeros((1, D_MODEL), F32)

        hv = h_ref[...]
        dh = jnp.zeros((TM, D_MODEL), F32)
        off = 0
        for r in refs[:n]:
            cols = slice(off, off + r.shape[1])
            off += r.shape[1]
            piece = r[...].astype(_MXU)
            dh = dh + _dot(piece, w_ref[:, cols], _NT)
            acc_sc[:, cols] += _dot(hv, piece, _TN)
        dx, dg = _rms_bwd(x_ref[...], g_ref[...], dh)
        dx_ref[...] = dx + dr_ref[...]
        dg_ref[...] += dg

        @pl.when(step == nb - 1)
        def _():
            dw_ref[...] = acc_sc[...].astype(_WIRE)

    resident = pl.BlockSpec((D_MODEL, D_INP), lambda i: (0, 0), pipeline_mode=pl.Buffered(1))
    g, g_spec = _of_layer(g)
    return _pallas(body, out_shape=(jax.ShapeDtypeStruct((L, D_MODEL), F32), jax.ShapeDtypeStruct((1, D_MODEL), F32),
                                    jax.ShapeDtypeStruct((D_MODEL, D_INP), _WIRE)),
                   grid=(nb,),
                   in_specs=[_rows(TM, p.shape[1]) for p in pieces]
                   + [resident, _rows(TM, D_MODEL), g_spec, _rows(TM, D_MODEL), _rows(TM, D_MODEL)],
                   out_specs=(_rows(TM, D_MODEL), _whole((1, D_MODEL)), resident),
                   scratch_shapes=[pltpu.VMEM((D_MODEL, D_INP), F32)], name="inproj_bwd",
                   compiler_params=pltpu.CompilerParams(vmem_limit_bytes=VMEM_BIG))(*pieces, w, x, g, dres, h)


PAIR = 2 * HEAD_DIM
N_AUX = 3


def _own(shape, h):
    return lax.broadcasted_iota(jnp.int32, shape, len(shape) - 1) // HEAD_DIM == h


def _hi_dot(a, b):
    return jnp.dot(a, b, precision=_HI, preferred_element_type=F32)


def _tri(n, lower):
    r = lax.broadcasted_iota(jnp.int32, (n, n), 0)
    c = lax.broadcasted_iota(jnp.int32, (n, n), 1)
    return jnp.where(r >= c if lower else r <= c, 1.0, 0.0).astype(F32)


def _fox_cumsum(proj, b):
    L = proj.shape[0]

    def body(fl_ref, b_ref, c_ref, carry_sc):
        @pl.when(pl.program_id(0) == 0)
        def _():
            carry_sc[...] = jnp.zeros((1, PAIR), F32)

        lane = lax.broadcasted_iota(jnp.int32, (TM, PAIR), 1)
        lf = jnp.where(lane < FOX_HEADS, jax.nn.log_sigmoid(fl_ref[...] + b_ref[...]), 0.0)
        cs = _hi_dot(_tri(TM, True), lf) + carry_sc[...]
        c_ref[...] = cs
        carry_sc[...] = cs[TM - 1:TM, :]

    b, b_spec = _of_layer(b)
    return _pallas(body, out_shape=jax.ShapeDtypeStruct((L, PAIR), F32), grid=(L // TM,),
                   in_specs=[_rows(TM, PAIR, O_FL // PAIR), b_spec], out_specs=_rows(TM, PAIR),
                   scratch_shapes=[pltpu.VMEM((1, PAIR), F32)], name="fox_cumsum")(proj, b)


def _fox_prep(proj, c):
    L = proj.shape[0]

    def body(q_ref, k_ref, v_ref, c_ref, qa_ref, ka_ref, kat_ref, vt_ref):
        lane = lax.broadcasted_iota(jnp.int32, (TM, PAIR), 1)
        cv = c_ref[...]
        for p in range(FOX_HEADS // 2):
            cols = slice(PAIR * p, PAIR * (p + 1))
            q2, k2 = q_ref[:, cols], k_ref[:, cols]
            vt_ref[p] = v_ref[:, cols].T.astype(_MXU)
            for e in range(2):
                h = 2 * p + e
                own = lane // HEAD_DIM == e
                a = lane - (HEAD_DIM if e == 0 else 0)
                rest = jnp.broadcast_to(cv[:, h:h + 1], (TM, PAIR))
                aux_q = jnp.where((a >= N_AUX) & (a < 2 * N_AUX), 1.0, 0.0)
                aux_k = jnp.where((a >= 0) & (a < N_AUX), 1.0, 0.0)
                for n in range(N_AUX):
                    part = rest.astype(_MXU).astype(F32)
                    rest = rest - part
                    aux_q = jnp.where(a == n, part, aux_q)
                    aux_k = jnp.where(a == N_AUX + n, -part, aux_k)
                ka = jnp.where(own, k2, aux_k)
                qa_ref[h] = jnp.where(own, q2 * (1.0 / math.sqrt(HEAD_DIM)), aux_q).astype(_MXU)
                ka_ref[h] = ka.astype(_MXU)
                kat_ref[h] = ka.T.astype(_MXU)

    hl = jax.ShapeDtypeStruct((FOX_HEADS, L, PAIR), _MXU)
    nat = lambda o: _rows(TM, D_FOX, o // D_FOX)
    rows = pl.BlockSpec((FOX_HEADS, TM, PAIR), lambda i: (0, i, 0))
    return _pallas(
        body, out_shape=(hl, hl, jax.ShapeDtypeStruct((FOX_HEADS, PAIR, L), _MXU),
                         jax.ShapeDtypeStruct((FOX_HEADS // 2, PAIR, L), _MXU)),
        grid=(L // TM,), in_specs=[nat(O_FQ), nat(O_FK), nat(O_FV), _rows(TM, PAIR)],
        out_specs=(rows, rows, pl.BlockSpec((FOX_HEADS, PAIR, TM), lambda i: (0, 0, i)),
                   pl.BlockSpec((FOX_HEADS // 2, PAIR, TM), lambda i: (0, 0, i))),
        name="fox_prep")(proj, proj, proj, c)


def _key_le_query(tq):
    return lax.broadcasted_iota(jnp.int32, (tq, tq), 0) <= lax.broadcasted_iota(jnp.int32, (tq, tq), 1)


def _grid_ends(n0, n1):
    first = lambda: (pl.program_id(0) == 0) & (pl.program_id(1) == 0)
    last = lambda: (pl.program_id(0) == n0 - 1) & (pl.program_id(1) == n1 - 1)
    return first, last


def _fox_fwd(qa, ka, vt, ride=None):
    H, L, _ = qa.shape
    tq = min(TQ, L)
    nq = L // tq

    def body(qa_ref, ka_ref, vt_ref, o_ref, lse_ref, m_sc, l_sc, acc_sc):
        i = pl.program_id(1)
        m_sc[...] = jnp.full((2, 1, tq), NEG, F32)
        l_sc[...] = jnp.zeros((2, 1, tq), F32)
        acc_sc[...] = jnp.zeros((2, HEAD_DIM, tq), F32)

        def block(j, nk, masked):
            keys = pl.ds(pl.multiple_of(j * tq, tq), nk * tq)
            vt_blk = vt_ref[:, keys]
            sts = [_dot(ka_ref[e, keys, :], qa_ref[e], _NT) for e in range(2)]
            pts, alphas = [], []
            for e in range(2):
                st = jnp.where(_key_le_query(tq), sts[e], NEG) if masked else sts[e]
                m_prev = m_sc[e]
                m_new = jnp.maximum(m_prev, jnp.max(st, axis=0, keepdims=True))
                alphas.append(jnp.exp(m_prev - m_new))
                pt = jnp.exp(st - m_new)
                l_sc[e] = alphas[e] * l_sc[e] + jnp.sum(pt, axis=0, keepdims=True)
                m_sc[e] = m_new
                pts.append(pt.astype(_MXU))
            for e in range(2):
                acc_sc[e] = alphas[e] * acc_sc[e] + _dot(vt_blk[HEAD_DIM * e:HEAD_DIM * (e + 1)], pts[e])

        def two_blocks(jj, carry):
            block(2 * jj, 2, False)
            return carry

        lax.fori_loop(0, i // 2, two_blocks, 0)

        @pl.when(i % 2 == 1)
        def _():
            block(i - 1, 1, False)

        block(i, 1, True)
        o_ref[...] = jnp.concatenate([acc_sc[0] / l_sc[0], acc_sc[1] / l_sc[1]], axis=0).T
        for e in range(2):
            lse_ref[e] = m_sc[e] + jnp.log(l_sc[e])

    body, ex_in, ex_specs, ex_out, ex_sems = _riding(body, 3, 2, ride, *_grid_ends(H // 2, nq))
    res = _pallas(
        body, out_shape=(jax.ShapeDtypeStruct((L, D_FOX), F32), jax.ShapeDtypeStruct((H, 1, L), F32), *ex_out),
        grid=(H // 2, nq),
        in_specs=[pl.BlockSpec((2, tq, PAIR), lambda p, i: (p, i, 0)), pl.BlockSpec((2, L, PAIR), lambda p, i: (p, 0, 0)),
                  pl.BlockSpec((None, PAIR, L), lambda p, i: (p, 0, 0)), *ex_specs],
        out_specs=(pl.BlockSpec((tq, PAIR), lambda p, i: (i, p)), pl.BlockSpec((2, 1, tq), lambda p, i: (p, 0, i)),
                   *ex_specs),
        scratch_shapes=[pltpu.VMEM((2, 1, tq), F32), pltpu.VMEM((2, 1, tq), F32), pltpu.VMEM((2, HEAD_DIM, tq), F32), *ex_sems],
        name="fox_fwd" if ride is None else "fox_fwd_gather")(qa, ka, vt, *ex_in)
    return res[0], res[1], list(res[2:])


def _fox_bwd(qa, ka, kat, proj, do, o, lse, ride=None):
    H, L, _ = qa.shape
    tq = min(TQ, L)
    nq = L // tq

    def body(qa_ref, ka_ref, kat_ref, v_ref, do_ref, o_ref, lse_ref, dqt_ref, dk_ref, dv_ref, delta_sc, dk_sc, dv_sc):
        j = pl.program_id(1)

        @pl.when(j == 0)
        def _():
            head_rows = (lax.broadcasted_iota(jnp.int32, (8, PAIR), 1) // HEAD_DIM
                         == lax.broadcasted_iota(jnp.int32, (8, PAIR), 0)).astype(F32)
            delta_sc[...] = lax.dot_general(head_rows, do_ref[...] * o_ref[...], _NT, precision=_HI,
                                            preferred_element_type=F32)
            dqt_ref[...] = jnp.zeros((2, PAIR, L), F32)

        dk_sc[...] = jnp.zeros((2, tq, PAIR), F32)
        dv_sc[...] = jnp.zeros((tq, PAIR), F32)
        vb = v_ref[...]

        def block(i, masked):
            qs = pl.ds(pl.multiple_of(i * tq, tq), tq)
            dob = do_ref[qs, :]
            for e in range(2):
                own = _own((tq, PAIR), e)
                qh = qa_ref[e, qs, :]
                pt = jnp.exp(_dot(ka_ref[e], qh, _NT) - lse_ref[e, :, qs])
                if masked:
                    pt = jnp.where(_key_le_query(tq), pt, 0.0)
                dv_sc[...] += _dot(pt, jnp.where(own, dob, 0.0))
                dpt = _dot(jnp.where(own, vb, 0.0), dob, _NT)
                ds = (pt * (dpt - delta_sc[e:e + 1, qs])).astype(_MXU)
                dk_sc[e] += _dot(ds, qh)
                dqt_ref[e, :, qs] += _dot(kat_ref[e], ds)

        def off_diagonal(i, carry):
            block(i, False)
            return carry

        block(j, True)
        lax.fori_loop(j + 1, nq, off_diagonal, 0)
        dk_ref[...] = dk_sc[...]
        dv_ref[...] = dv_sc[...]

    nat = pl.BlockSpec((L, PAIR), lambda p, j: (0, p))
    body, ex_in, ex_specs, ex_out, ex_sems = _riding(body, 7, 3, ride, *_grid_ends(H // 2, nq))
    res = _pallas(
        body, out_shape=(jax.ShapeDtypeStruct((H, PAIR, L), F32), jax.ShapeDtypeStruct((H, L, PAIR), F32),
                         jax.ShapeDtypeStruct((L, D_FOX), F32), *ex_out),
        grid=(H // 2, nq),
        in_specs=[pl.BlockSpec((2, L, PAIR), lambda p, j: (p, 0, 0)), pl.BlockSpec((2, tq, PAIR), lambda p, j: (p, j, 0)),
                  pl.BlockSpec((2, PAIR, tq), lambda p, j: (p, 0, j)),
                  pl.BlockSpec((tq, PAIR), lambda p, j: (j, O_FV // PAIR + p)), nat, nat,
                  pl.BlockSpec((2, 1, L), lambda p, j: (p, 0, 0)), *ex_specs],
        out_specs=(pl.BlockSpec((2, PAIR, L), lambda p, j: (p, 0, 0)), pl.BlockSpec((2, tq, PAIR), lambda p, j: (p, j, 0)),
                   pl.BlockSpec((tq, PAIR), lambda p, j: (j, p)), *ex_specs),
        scratch_shapes=[pltpu.VMEM((8, L), F32), pltpu.VMEM((2, tq, PAIR), F32), pltpu.VMEM((tq, PAIR), F32), *ex_sems],
        name="fox_bwd" if ride is None else "fox_bwd_exchange",
        compiler_params=pltpu.CompilerParams(vmem_limit_bytes=VMEM_BIG))(qa, ka, kat, proj, do, o, lse, *ex_in)
    return res[0], res[1], res[2], list(res[3:])


def _fox_post_bwd(dqt, dkraw, proj, b):
    L = proj.shape[0]
    nb = L // TM

    def body(dqt_ref, dkr_ref, fl_ref, b_ref, dq_ref, dk_ref, dfl_ref, db_ref, carry_sc):
        first = pl.program_id(0) == 0

        @pl.when(first)
        def _():
            carry_sc[...] = jnp.zeros((1, PAIR), F32)

        lane = lax.broadcasted_iota(jnp.int32, (TM, PAIR), 1)
        rr = lax.broadcasted_iota(jnp.int32, (PAIR, PAIR), 0)
        cc = lax.broadcasted_iota(jnp.int32, (PAIR, PAIR), 1)
        dc = jnp.zeros((TM, PAIR), F32)
        for p in range(FOX_HEADS // 2):
            cols = slice(PAIR * p, PAIR * (p + 1))
            dqs = [dqt_ref[2 * p + e].T for e in range(2)]
            dks = [dkr_ref[2 * p + e] for e in range(2)]
            dq_ref[:, cols] = jnp.where(lane < HEAD_DIM, dqs[0], dqs[1]) * (1.0 / math.sqrt(HEAD_DIM))
            dk_ref[:, cols] = jnp.where(lane < HEAD_DIM, dks[0], dks[1])
            sums = jnp.zeros((TM, PAIR), F32)
            place = jnp.zeros((PAIR, PAIR), F32)
            for e in range(2):
                base = HEAD_DIM if e == 0 else 0
                sums = jnp.where(lane == base, dqs[e], jnp.where(lane == base + N_AUX, -dks[e], sums))
                place = jnp.where(((rr == base) | (rr == base + N_AUX)) & (cc == 2 * p + e), 1.0, place)
            dc = dc + _hi_dot(sums, place)
        rs = _hi_dot(_tri(TM, False), dc) + carry_sc[...]
        carry_sc[...] = rs[0:1, :]
        dfl = jnp.where(lane < FOX_HEADS, rs * jax.nn.sigmoid(-(fl_ref[...] + b_ref[...])), 0.0)
        dfl_ref[...] = dfl
        db = jnp.sum(dfl, axis=0, keepdims=True)

        @pl.when(first)
        def _():
            db_ref[...] = db

        @pl.when(jnp.logical_not(first))
        def _():
            db_ref[...] += db

    rev = lambda i: nb - 1 - i
    b, b_spec = _of_layer(b)
    nat = pl.BlockSpec((TM, D_FOX), lambda i: (rev(i), 0))
    return _pallas(
        body, out_shape=(jax.ShapeDtypeStruct((L, D_FOX), F32),) * 2
        + (jax.ShapeDtypeStruct((L, PAIR), F32), jax.ShapeDtypeStruct((1, PAIR), F32)),
        grid=(nb,),
        in_specs=[pl.BlockSpec((FOX_HEADS, PAIR, TM), lambda i: (0, 0, rev(i))),
                  pl.BlockSpec((FOX_HEADS, TM, PAIR), lambda i: (0, rev(i), 0)),
                  pl.BlockSpec((TM, PAIR), lambda i: (rev(i), O_FL // PAIR)), b_spec],
        out_specs=(nat, nat, pl.BlockSpec((TM, PAIR), lambda i: (rev(i), 0)), _whole((1, PAIR))),
        scratch_shapes=[pltpu.VMEM((1, PAIR), F32)], name="fox_post_bwd")(dqt, dkraw, proj, b)


def _s5_expand():
    r = lax.broadcasted_iota(jnp.int32, (S5_STATE, S5_STATE * S5_GROUP_CH), 0)
    c = lax.broadcasted_iota(jnp.int32, (S5_STATE, S5_STATE * S5_GROUP_CH), 1)
    return jnp.where(c // S5_GROUP_CH == r, 1.0, 0.0).astype(F32)


def _s5_disc_math(ar, ai, ldt, br, bi):
    dt = jnp.exp(ldt)
    mag = jnp.exp(ar * dt)
    lr = mag * jnp.cos(ai * dt)
    li = mag * jnp.sin(ai * dt)
    den = ar * ar + ai * ai
    fr = ((lr - 1.0) * ar + li * ai) / den
    fi = (li * ar - (lr - 1.0) * ai) / den
    e = _s5_expand()
    fre = jnp.dot(fr, e, precision=_HI, preferred_element_type=F32)
    fie = jnp.dot(fi, e, precision=_HI, preferred_element_type=F32)
    return lr, li, fre * br - fie * bi, fre * bi + fie * br


def _layer_blocks(arrs):
    return [pl.BlockSpec((None,) + a.shape[1:], lambda l: (l, 0, 0)) for a in arrs]


def _s5_disc(ar, ai, ldt, br, bi):
    def body(ar_ref, ai_ref, ldt_ref, br_ref, bi_ref, lr_ref, li_ref, bbr_ref, bbi_ref):
        lr, li, bbr, bbi = _s5_disc_math(ar_ref[...], ai_ref[...], ldt_ref[...], br_ref[...], bi_ref[...])
        lr_ref[...] = lr
        li_ref[...] = li
        bbr_ref[...] = bbr
        bbi_ref[...] = bbi

    ins = (ar, ai, ldt, br, bi)
    outs = (ar, ai, br, bi)
    return _pallas(body, out_shape=tuple(jax.ShapeDtypeStruct(a.shape, F32) for a in outs), grid=(DEPTH,),
                   in_specs=_layer_blocks(ins), out_specs=tuple(_layer_blocks(outs)), name="s5_disc")(*ins)


def _s5_disc_bwd(ar, ai, ldt, br, bi, dlr, dli, dbbr, dbbi):
    def body(ar_ref, ai_ref, ldt_ref, br_ref, bi_ref, dlr_ref, dli_ref, dbbr_ref, dbbi_ref,
             dar_ref, dai_ref, dldt_ref, dbr_ref, dbi_ref):
        _, vjp = jax.vjp(_s5_disc_math, ar_ref[...], ai_ref[...], ldt_ref[...], br_ref[...], bi_ref[...])
        dar, dai, dldt, dbr, dbi = vjp((dlr_ref[...], dli_ref[...], dbbr_ref[...], dbbi_ref[...]))
        dar_ref[...] = dar
        dai_ref[...] = dai
        dldt_ref[...] = dldt
        dbr_ref[...] = dbr
        dbi_ref[...] = dbi

    ins = (ar, ai, ldt, br, bi, dlr, dli, dbbr, dbbi)
    outs = (ar, ai, ldt, br, bi)
    return _pallas(body, out_shape=tuple(jax.ShapeDtypeStruct(a.shape, F32) for a in outs), grid=(DEPTH,),
                   in_specs=_layer_blocks(ins), out_specs=tuple(_layer_blocks(outs)), name="s5_disc_bwd")(*ins)


SLAB = 2 * S5_CH // 128


def _slab_rows(s, ts):
    return pl.ds(s, ts, stride=SLAB)


def _slab_pair(ref, s, ts):
    return jnp.concatenate([ref[_slab_rows(s, ts), :].astype(_MXU), ref[_slab_rows(s + 1, ts), :].astype(_MXU)], axis=-1)


def _s5_fwd(proj, wb, wc, lam, d, w_glu):
    L = proj.shape[0]
    ts = min(TS, L)

    def body(u_ref, wb_ref, wc_ref, lam_ref, d_ref, wg_ref, xs_ref, ypre_ref, ys_ref, b_sc, c_sc):
        @pl.when(pl.program_id(0) == 0)
        def _():
            c_sc[...] = jnp.zeros((SLAB, 128), F32)

        u = u_ref[...]
        ub = u.astype(_MXU)
        for s in range(0, SLAB, 2):
            b2 = _dot(ub, wb_ref[:, 128 * s:128 * (s + 2)])
            b_sc[_slab_rows(s, ts), :] = b2[:, :128]
            b_sc[_slab_rows(s + 1, ts), :] = b2[:, 128:]
        lr, li = lam_ref[0:8, :], lam_ref[8:16, :]

        def step(t, carry):
            xr, xi = carry
            row = pl.multiple_of(t * SLAB, SLAB)
            nr = lr * xr - li * xi + b_sc[pl.ds(row, 8), :]
            ni = lr * xi + li * xr + b_sc[pl.ds(row + 8, 8), :]
            xs_ref[pl.ds(row, 8), :] = nr
            xs_ref[pl.ds(row + 8, 8), :] = ni
            return nr, ni

        xr, xi = lax.fori_loop(0, ts, step, (c_sc[0:8, :], c_sc[8:16, :]), unroll=8)
        c_sc[0:8, :] = xr
        c_sc[8:16, :] = xi
        y = jnp.zeros((ts, D_S5), F32)
        for s in range(0, SLAB, 2):
            y = y + _dot(_slab_pair(xs_ref, s, ts), wc_ref[128 * s:128 * (s + 2), :])
        ypre_ref[...] = y
        y1 = jax.nn.gelu(y + d_ref[...] * u)
        ys_ref[...] = y1 * jax.nn.sigmoid(_dot(y1, wg_ref[...]))

    row = _rows(ts, D_S5)
    slabs = pl.BlockSpec((ts * SLAB, 128), lambda n: (n, 0))
    (wb, wb_spec), (wc, wc_spec), (lam, lam_spec), (d, d_spec) = (_of_layer(a) for a in (wb, wc, lam, d))
    return _pallas(
        body, out_shape=(jax.ShapeDtypeStruct((L * SLAB, 128), F32), jax.ShapeDtypeStruct((L, D_S5), F32),
                         jax.ShapeDtypeStruct((L, D_S5), F32)),
        grid=(L // ts,),
        in_specs=[_rows(ts, D_S5, O_SU // D_S5), wb_spec, wc_spec, lam_spec, d_spec, _whole((D_S5, D_S5))],
        out_specs=(slabs, row, row),
        scratch_shapes=[pltpu.VMEM((ts * SLAB, 128), F32), pltpu.VMEM((SLAB, 128), F32)], name="s5_fwd")(
            proj, wb, wc, lam, d, w_glu)


def _s5_bwd(proj, ypre, dys, xs, wb, wc, lam, d, w_glu):
    L = proj.shape[0]
    ts = min(TS, L)
    nb = L // ts

    def body(u_ref, y_ref, dys_ref, xs_ref, xp_ref, wb_ref, wc_ref, lam_ref, d_ref, wg_ref,
             du_ref, dwb_ref, dwc_ref, dlam_ref, dd_ref, dwg_ref, dx_sc, g_sc, c_sc):
        n = pl.program_id(0)

        @pl.when(n == 0)
        def _():
            c_sc[...] = jnp.zeros((SLAB, 128), F32)
            dlam_ref[...] = jnp.zeros((SLAB, 128), F32)
            dwb_ref[...] = jnp.zeros((D_S5, 2 * S5_CH), F32)
            dwc_ref[...] = jnp.zeros((2 * S5_CH, D_S5), F32)
            dd_ref[...] = jnp.zeros((1, D_S5), F32)
            dwg_ref[...] = jnp.zeros((D_S5, D_S5), F32)

        u, dv, dout = u_ref[...], d_ref[...], dys_ref[...]
        y1, gelu_vjp = jax.vjp(jax.nn.gelu, y_ref[...] + dv * u)
        sg = jax.nn.sigmoid(_dot(y1, wg_ref[...]))
        dz = dout * y1 * sg * (1.0 - sg)
        dy, = gelu_vjp(dout * sg + _dot(dz, wg_ref[...], _NT))
        dd_ref[...] += jnp.sum(dy * u, axis=0, keepdims=True)
        dwg_ref[...] += _dot(y1, dz, _TN)
        dyb = dy.astype(_MXU)
        for s in range(0, SLAB, 2):
            cols = slice(128 * s, 128 * (s + 2))
            dx2 = _dot(dyb, wc_ref[cols, :], _NT)
            dx_sc[_slab_rows(s, ts), :] = dx2[:, :128]
            dx_sc[_slab_rows(s + 1, ts), :] = dx2[:, 128:]
            dwc_ref[cols, :] += _dot(_slab_pair(xs_ref, s, ts), dyb, _TN)
        lr, li = lam_ref[0:8, :], lam_ref[8:16, :]

        def adjoint(row, pr, pi, carry):
            gr, gi, ar, ai = carry
            nr = dx_sc[pl.ds(row, 8), :] + lr * gr + li * gi
            ni = dx_sc[pl.ds(row + 8, 8), :] - li * gr + lr * gi
            g_sc[pl.ds(row, 8), :] = nr
            g_sc[pl.ds(row + 8, 8), :] = ni
            return nr, ni, ar + nr * pr + ni * pi, ai - nr * pi + ni * pr

        def step(k, carry):
            row = pl.multiple_of((ts - 1 - k) * SLAB, SLAB)
            prev = pl.multiple_of((ts - 2 - k) * SLAB, SLAB)
            return adjoint(row, xs_ref[pl.ds(prev, 8), :], xs_ref[pl.ds(prev + 8, 8), :], carry)

        z = jnp.zeros((8, 128), F32)
        carry = lax.fori_loop(0, ts - 1, step, (c_sc[0:8, :], c_sc[8:16, :], z, z), unroll=8)
        has_prev = jnp.where(n == nb - 1, 0.0, 1.0)
        gr, gi, ar, ai = adjoint(0, xp_ref[0:8, :] * has_prev, xp_ref[8:16, :] * has_prev, carry)
        c_sc[0:8, :] = gr
        c_sc[8:16, :] = gi
        dlam_ref[0:8, :] += ar
        dlam_ref[8:16, :] += ai
        ub = u.astype(_MXU)
        du = dy * dv
        for s in range(0, SLAB, 2):
            cols = slice(128 * s, 128 * (s + 2))
            gs = _slab_pair(g_sc, s, ts)
            du = du + _dot(gs, wb_ref[:, cols], _NT)
            dwb_ref[:, cols] += _dot(ub, gs, _TN)
        du_ref[...] = du

    blk = lambda n: nb - 1 - n
    row = pl.BlockSpec((ts, D_S5), lambda n: (blk(n), 0))
    (wb, wb_spec), (wc, wc_spec), (lam, lam_spec), (d, d_spec) = (_of_layer(a) for a in (wb, wc, lam, d))
    return _pallas(
        body, out_shape=(jax.ShapeDtypeStruct((L, D_S5), F32), jax.ShapeDtypeStruct((D_S5, 2 * S5_CH), F32),
                         jax.ShapeDtypeStruct((2 * S5_CH, D_S5), F32), jax.ShapeDtypeStruct((SLAB, 128), F32),
                         jax.ShapeDtypeStruct((1, D_S5), F32), jax.ShapeDtypeStruct((D_S5, D_S5), F32)),
        grid=(nb,),
        in_specs=[pl.BlockSpec((ts, D_S5), lambda n: (blk(n), O_SU // D_S5)), row, row,
                  pl.BlockSpec((ts * SLAB, 128), lambda n: (blk(n), 0)),
                  pl.BlockSpec((SLAB, 128), lambda n: (jnp.maximum(blk(n) * ts - 1, 0), 0)),
                  wb_spec, wc_spec, lam_spec, d_spec, _whole((D_S5, D_S5))],
        out_specs=(row, _whole((D_S5, 2 * S5_CH)), _whole((2 * S5_CH, D_S5)), _whole((SLAB, 128)), _whole((1, D_S5)),
                   _whole((D_S5, D_S5))),
        scratch_shapes=[pltpu.VMEM((ts * SLAB, 128), F32), pltpu.VMEM((ts * SLAB, 128), F32), pltpu.VMEM((SLAB, 128), F32)],
        name="s5_bwd")(proj, ypre, dys, xs, xs, wb, wc, lam, d, w_glu)


def _rot(z, cos, sin):
    lane = lax.broadcasted_iota(jnp.int32, z.shape, 1)
    zs = z * sin
    half = HEAD_DIM // 2
    return z * cos + jnp.where(lane % HEAD_DIM < half, pltpu.roll(zs, PAIR - half, 1), pltpu.roll(zs, half, 1))


def _head_avg():
    r = lax.broadcasted_iota(jnp.int32, (PAIR, PAIR), 0) // HEAD_DIM
    c = lax.broadcasted_iota(jnp.int32, (PAIR, PAIR), 1) // HEAD_DIM
    return jnp.where(r == c, 1.0 / HEAD_DIM, 0.0).astype(F32)


def _ret_tables(tq):
    lg = jnp.log1p(-(2.0 ** (-5.0 - jnp.arange(RET_HEADS, dtype=F32))))
    scale = 1.0 / math.sqrt(HEAD_DIM)
    pos = jnp.arange(tq)
    n = pos.astype(F32)
    dist = jnp.abs(n[:, None] - n[None, :])
    ok = (pos[None, :] // CHUNK) <= (pos[:, None] // CHUNK)
    w = jnp.where(ok[None], scale * jnp.exp(lg[:, None, None] * dist[None]), 0.0)
    lgl = jnp.repeat(lg, HEAD_DIM)
    dq_tab = scale * jnp.exp(lgl[None, :] * (n[:, None] + 1.0))
    dk_tab = jnp.exp(lgl[None, :] * (tq - 1.0 - n[:, None]))
    blk = jnp.arange(PAIR) // HEAD_DIM
    bd = (blk[:, None] == blk[None, :]).astype(F32)
    gbd = bd[None] * jnp.exp(lgl.reshape(RET_HEADS // 2, PAIR)[:, :, None] * tq)
    return dict(w=w, wt=w.transpose(0, 2, 1), dq=dq_tab, dk=dk_tab, gbd=gbd, bd=bd)


def _ret_specs(tq, nq, rev, layer):
    blk = (lambda i: nq - 1 - i) if rev else (lambda i: i)
    col = lambda o: pl.BlockSpec((tq, PAIR), lambda p, i: (blk(i), o // PAIR + p))
    return dict(
        rq=col(O_RQ), rk=col(O_RK), rv=col(O_RV), nat=col(0),
        w=pl.BlockSpec((2, tq, tq), lambda p, i: (p, 0, 0)), tab=pl.BlockSpec((tq, PAIR), lambda p, i: (0, p)),
        gbd=pl.BlockSpec((None, PAIR, PAIR), lambda p, i: (p, 0, 0)), bd=pl.BlockSpec((PAIR, PAIR), lambda p, i: (0, 0)),
        gn=pl.BlockSpec((None, 1, PAIR), lambda p, i: (layer, 0, p)), dgn=pl.BlockSpec((1, PAIR), lambda p, i: (0, p)),
        st=pl.BlockSpec((None, None, PAIR, PAIR), lambda p, i: (p, blk(i), 0, 0)))


def _ret_fwd(proj, cos_t, sin_t, tabs, gn):
    L = proj.shape[0]
    tq = tabs["w"].shape[1]
    nq = L // tq

    def body(rq_ref, rk_ref, rv_ref, cos_ref, sin_ref, w_ref, dqt_ref, dkt_ref, gbd_ref, bd_ref, gn_ref,
             o_ref, y_ref, st_ref, s_sc):
        @pl.when(pl.program_id(1) == 0)
        def _():
            s_sc[...] = jnp.zeros((PAIR, PAIR), F32)

        state = s_sc[...]
        st_ref[...] = state
        cos, sin = cos_ref[...], sin_ref[...]
        q2, k2, v2 = _rot(rq_ref[...], cos, sin), _rot(rk_ref[...], cos, sin), rv_ref[...]
        o = _dot(q2 * dqt_ref[...], state)
        for h in range(2):
            own = _own((tq, PAIR), h)
            a = _dot(jnp.where(own, q2, 0.0), k2, _NT) * w_ref[h]
            o = o + _dot(a, jnp.where(own, v2, 0.0))
        s_sc[...] = gbd_ref[...] * state + bd_ref[...] * _dot(k2 * dkt_ref[...], v2, _TN)
        o_ref[...] = o
        avg = _head_avg()
        oc = o - _hi_dot(o, avg)
        y_ref[...] = oc * lax.rsqrt(_hi_dot(oc * oc, avg) + EPS) * gn_ref[...]

    gn, layer = gn
    sp = _ret_specs(tq, nq, False, layer)
    nat = jax.ShapeDtypeStruct((L, D_RET), F32)
    return _pallas(
        body, out_shape=(nat, nat, jax.ShapeDtypeStruct((RET_HEADS // 2, nq, PAIR, PAIR), F32)), grid=(RET_HEADS // 2, nq),
        in_specs=[sp["rq"], sp["rk"], sp["rv"], sp["nat"], sp["nat"], sp["w"], sp["tab"], sp["tab"], sp["gbd"], sp["bd"],
                  sp["gn"]],
        out_specs=(sp["nat"], sp["nat"], sp["st"]), scratch_shapes=[pltpu.VMEM((PAIR, PAIR), F32)],
        name="ret_fwd")(proj, proj, proj, cos_t, sin_t, tabs["w"], tabs["dq"], tabs["dk"], tabs["gbd"], tabs["bd"], gn)


def _ret_bwd(proj, cos_t, sin_t, tabs, gn, o_pre, dy, states):
    L = proj.shape[0]
    tq = tabs["w"].shape[1]
    nq = L // tq

    def body(rq_ref, rk_ref, rv_ref, cos_ref, sin_ref, w_ref, wt_ref, dqt_ref, dkt_ref, gbd_ref, bd_ref, gn_ref,
             o_ref, dy_ref, st_ref, drq_ref, drk_ref, drv_ref, dgn_ref, g_sc):
        first = pl.program_id(1) == 0

        @pl.when(first)
        def _():
            g_sc[...] = jnp.zeros((PAIR, PAIR), F32)

        cos, sin = cos_ref[...], sin_ref[...]
        q2, k2, v2 = _rot(rq_ref[...], cos, sin), _rot(rk_ref[...], cos, sin), rv_ref[...]
        avg = _head_avg()
        ov, dyv = o_ref[...], dy_ref[...]
        oc = ov - _hi_dot(ov, avg)
        r = lax.rsqrt(_hi_dot(oc * oc, avg) + EPS)
        oh = oc * r
        dgn = jnp.sum(dyv * oh, axis=0, keepdims=True)
        doh = dyv * gn_ref[...]
        do = r * (doh - _hi_dot(doh, avg) - oh * _hi_dot(doh * oh, avg))
        state, g = st_ref[...], g_sc[...]
        dqt, dkt = dqt_ref[...], dkt_ref[...]
        dq = _dot(do, state, _NT) * dqt
        dk = _dot(v2, g, _NT) * dkt
        dv = _dot(k2 * dkt, g)
        g_sc[...] = gbd_ref[...] * g + bd_ref[...] * _dot(q2 * dqt, do, _TN)
        for h in range(2):
            own = _own((tq, PAIR), h)
            qm, dom = jnp.where(own, q2, 0.0), jnp.where(own, do, 0.0)
            dv = dv + _dot(_dot(k2, qm, _NT) * wt_ref[h], dom)
            dq = dq + _dot(_dot(dom, v2, _NT) * w_ref[h], jnp.where(own, k2, 0.0))
            dk = dk + _dot(_dot(v2, dom, _NT) * wt_ref[h], qm)
        drq_ref[...] = _rot(dq, cos, -sin)
        drk_ref[...] = _rot(dk, cos, -sin)
        drv_ref[...] = dv

        @pl.when(first)
        def _():
            dgn_ref[...] = dgn

        @pl.when(jnp.logical_not(first))
        def _():
            dgn_ref[...] += dgn

    gn, layer = gn
    sp = _ret_specs(tq, nq, True, layer)
    nat = jax.ShapeDtypeStruct((L, D_RET), F32)
    return _pallas(
        body, out_shape=(nat, nat, nat, jax.ShapeDtypeStruct((1, D_RET), F32)), grid=(RET_HEADS // 2, nq),
        in_specs=[sp["rq"], sp["rk"], sp["rv"], sp["nat"], sp["nat"], sp["w"], sp["w"], sp["tab"], sp["tab"], sp["gbd"],
                  sp["bd"], sp["gn"], sp["nat"], sp["nat"], sp["st"]],
        out_specs=(sp["nat"], sp["nat"], sp["nat"], sp["dgn"]), scratch_shapes=[pltpu.VMEM((PAIR, PAIR), F32)],
        name="ret_bwd")(proj, proj, proj, cos_t, sin_t, tabs["w"], tabs["wt"], tabs["dq"], tabs["dk"], tabs["gbd"],
                        tabs["bd"], gn, o_pre, dy, states)


def _gate_out(yf, ys, yr, proj, x, w):
    L = x.shape[0]

    def body(yf_ref, ys_ref, yr_ref, g_ref, x_ref, w_ref, xn_ref):
        cat = jnp.concatenate([yf_ref[...], ys_ref[...], yr_ref[...]], axis=-1)
        xn_ref[...] = x_ref[...] + _dot(cat * jax.nn.silu(g_ref[...]), w_ref[...])

    tm = min(TMB, L)
    full = _rows(tm, D_MODEL)
    return _pallas(body, out_shape=jax.ShapeDtypeStruct((L, D_MODEL), F32), grid=(L // tm,),
                   in_specs=[_rows(tm, D_FOX), _rows(tm, D_S5), _rows(tm, D_RET), _rows(tm, D_MODEL, O_GATE // D_MODEL),
                             full, _whole((D_MODEL, D_MODEL))],
                   out_specs=full, name="gate_out")(yf, ys, yr, proj, x, w)


def _gate_out_bwd(dxn, w, yf, ys, yr, proj):
    L = dxn.shape[0]

    def body(dx_ref, w_ref, yf_ref, ys_ref, yr_ref, g_ref, dyf_ref, dys_ref, dyr_ref, dg_ref, dw_ref):
        dxv = dx_ref[...].astype(_MXU)
        dy = _dot(dxv, w_ref[...], _NT)
        g = g_ref[...]
        sg = jax.nn.sigmoid(g)
        silu = g * sg
        dcat = dy * silu
        dyf_ref[...] = dcat[:, :D_FOX]
        dys_ref[...] = dcat[:, D_FOX:D_FOX + D_S5]
        dyr_ref[...] = dcat[:, D_FOX + D_S5:]
        cat = jnp.concatenate([yf_ref[...], ys_ref[...], yr_ref[...]], axis=-1)
        dg_ref[...] = dy * cat * (sg * (1.0 + g * (1.0 - sg)))
        dw = _dot(cat * silu, dxv, _TN)

        @pl.when(pl.program_id(0) == 0)
        def _():
            dw_ref[...] = dw

        @pl.when(pl.program_id(0) != 0)
        def _():
            dw_ref[...] += dw

    tm = min(TMB, L)
    full = _rows(tm, D_MODEL)
    f, s, r = _rows(tm, D_FOX), _rows(tm, D_S5), _rows(tm, D_RET)
    sq = _whole((D_MODEL, D_MODEL))
    return _pallas(body, out_shape=(jax.ShapeDtypeStruct((L, D_FOX), F32), jax.ShapeDtypeStruct((L, D_S5), F32),
                                    jax.ShapeDtypeStruct((L, D_RET), F32), jax.ShapeDtypeStruct((L, D_MODEL), F32),
                                    jax.ShapeDtypeStruct((D_MODEL, D_MODEL), F32)),
                   grid=(L // tm,), in_specs=[full, sq, f, s, r, _rows(tm, D_MODEL, O_GATE // D_MODEL)],
                   out_specs=(f, s, r, full, sq), name="gate_out_bwd",
                   compiler_params=pltpu.CompilerParams(vmem_limit_bytes=VMEM_BIG))(dxn, w, yf, ys, yr, proj)


def _final_loss(x, g, tgt):
    L = x.shape[0]

    def body(x_ref, g_ref, t_ref, loss_ref, dx_ref, dg_ref):
        xv, gv = x_ref[...], g_ref[...]
        r = lax.rsqrt(jnp.mean(xv * xv, axis=-1, keepdims=True) + EPS)
        err = xv * r * gv - t_ref[...]
        part = 0.5 * jnp.sum(jnp.mean(err * err, axis=-1, keepdims=True), axis=0, keepdims=True)
        dx, dg = _rms_bwd(xv, gv, err * (1.0 / D_MODEL))
        dx_ref[...] = dx

        @pl.when(pl.program_id(0) == 0)
        def _():
            loss_ref[...] = part
            dg_ref[...] = dg

        @pl.when(pl.program_id(0) != 0)
        def _():
            loss_ref[...] += part
            dg_ref[...] += dg

    full = _rows(TM, D_MODEL)
    return _pallas(body, out_shape=(jax.ShapeDtypeStruct((1, 1), F32), jax.ShapeDtypeStruct((L, D_MODEL), F32),
                                    jax.ShapeDtypeStruct((1, D_MODEL), F32)),
                   grid=(L // TM,), in_specs=[full, _whole((1, D_MODEL)), full],
                   out_specs=(_whole((1, 1)), full, _whole((1, D_MODEL))), name="final_loss")(x, g, tgt)


def _block_diag(blocks):
    n, g, r, c = blocks.shape
    eye = jnp.eye(g, dtype=blocks.dtype)
    return (blocks[:, :, :, None, :] * eye[None, :, None, :, None]).reshape(n, g * r, g * c)


def _diag_blocks(m, g):
    n, r, c = m.shape[0], m.shape[1] // g, m.shape[2] // g
    eye = jnp.eye(g, dtype=m.dtype)
    return jnp.sum(m.reshape(n, g, r, g, c) * eye[None, :, None, :, None], axis=3)


def _rope_tables(L):
    half = HEAD_DIM // 2
    freqs = ROPE_BASE ** (-jnp.arange(half, dtype=F32) / half)
    ang = jnp.arange(L, dtype=F32)[:, None] * freqs[None, :]
    cos, sin = jnp.cos(ang), jnp.sin(ang)
    cos_t = jnp.tile(jnp.concatenate([cos, cos], axis=-1), (1, RET_HEADS))
    sin_t = jnp.tile(jnp.concatenate([sin, -sin], axis=-1), (1, RET_HEADS))
    return cos_t, sin_t


def _s5_disc_args(small):
    g, s, ch = S5_GROUPS, S5_STATE, S5_GROUP_CH
    return (small["s5_a_re"], small["s5_a_im"], small["s5_log_dt"][:, :, None],
            small["s5_b_re"].reshape(DEPTH, g, s * ch), small["s5_b_im"].reshape(DEPTH, g, s * ch))


def _s5_mats(small):
    g, s, ch = S5_GROUPS, S5_STATE, S5_GROUP_CH
    lr, li, bbr, bbi = _s5_disc(*_s5_disc_args(small))
    lam = jnp.concatenate([lr.reshape(DEPTH, 8, 128), li.reshape(DEPTH, 8, 128)], axis=1)
    wb = jnp.concatenate([_block_diag(b.reshape(DEPTH, g, s, ch).transpose(0, 1, 3, 2)) for b in (bbr, bbi)], axis=2)
    wc = jnp.concatenate([_block_diag(c.transpose(0, 1, 3, 2)) for c in (small["s5_c_re"], -small["s5_c_im"])], axis=1)
    return lam, wb.astype(_MXU), wc.astype(_MXU)


def _s5_param_grads(small, dwb, dwc, dlam):
    g, s, ch = S5_GROUPS, S5_STATE, S5_GROUP_CH
    dc = [_diag_blocks(m, g).transpose(0, 1, 3, 2).reshape(DEPTH, g, ch * s) for m in (dwc[:, :S5_CH], dwc[:, S5_CH:])]
    dbb = [_diag_blocks(m, g).transpose(0, 1, 3, 2).reshape(DEPTH, g, s * ch) for m in (dwb[:, :, :S5_CH], dwb[:, :, S5_CH:])]
    dar, dai, dldt, dbr, dbi = _s5_disc_bwd(*_s5_disc_args(small), dlam[:, :8].reshape(DEPTH, g, s),
                                            dlam[:, 8:].reshape(DEPTH, g, s), dbb[0], dbb[1])
    return dict(s5_a_re=dar, s5_a_im=dai, s5_log_dt=dldt.reshape(DEPTH, g), s5_b_re=dbr, s5_b_im=dbi, s5_c_re=dc[0],
                s5_c_im=-dc[1])


_DENSE = ("s5_b_re", "s5_b_im", "s5_c_re", "s5_c_im")


def _layer_fwd(x, p, rope, ride=None, late=False):
    L = x.shape[0]
    cos_t, sin_t, ret_tabs = rope
    s = {"x": x}
    proj, h = _norm_inproj(x, p["norm_w"], p["w_in"])
    s["proj"], s["h"] = proj, h
    qa, ka, kat, vt = _fox_prep(proj, _fox_cumsum(proj, p["b_f"]))
    yf, lse, landed = _fox_fwd(qa, ka, vt, ride)
    s.update(qa=qa, ka=ka, kat=kat, lse=lse, yf=yf)
    if late:
        p["w_glu"], p["w_out"] = _gathered_rows(landed[-2]), _gathered_rows(landed[-1])
        landed = landed[:-2]
    xs, ypre, ys = _s5_fwd(proj, p["wb"], p["wc"], p["lam"], p["d"], p["w_glu"])
    s.update(xs=xs, ypre=ypre, ys=ys)
    o_pre, yr, states = _ret_fwd(proj, cos_t, sin_t, ret_tabs, p["gn_w"])
    s.update(o_pre=o_pre, yr=yr, states=states)
    return _gate_out(yf, ys, yr, proj, x, p["w_out"]), s, landed


def _layer_bwd(dxn, s, p, rope, ride=None, early=False):
    L = dxn.shape[0]
    cos_t, sin_t, ret_tabs = rope
    g = {}
    proj = s["proj"]
    dyf, dys, dyr, dgate, g["w_out"] = _gate_out_bwd(dxn, p["w_out"], s["yf"], s["ys"], s["yr"], proj)
    drq, drk, drv, dgn = _ret_bwd(proj, cos_t, sin_t, ret_tabs, p["gn_w"], s["o_pre"], dyr, s["states"])
    g["ret_gn_w"] = dgn.reshape(D_RET)
    dsu, g["wb"], g["wc"], g["lam"], dd, g["s5_w_glu"] = _s5_bwd(proj, s["ypre"], dys, s["xs"], p["wb"], p["wc"], p["lam"],
                                                                 p["d"], p["w_glu"])
    g["s5_d"] = dd.reshape(D_S5)
    if early:
        ride = (ride[0] + _row_slots(g), ride[1] + [True, True])
    dqt, dkraw, dv, landed = _fox_bwd(s["qa"], s["ka"], s["kat"], proj, dyf, s["yf"], s["lse"], ride)
    dq, dk, dfl, dbf = _fox_post_bwd(dqt, dkraw, proj, p["b_f"])
    g["fox_b_f"] = dbf[0, :FOX_HEADS]
    pieces = [dgate, dq, dk, dv, dsu, drq, drk, drv, dfl]
    dx, dnw, g["w_in"] = _inproj_bwd(pieces, p["w_in"], s["x"], p["norm_w"], dxn, s["h"])
    g["norm_w"] = dnw.reshape(D_MODEL)
    return dx, g, landed


def _stacked_params(small):
    lam, wb, wc = _s5_mats(small)
    row = lambda a: a[:, None, :]
    return dict(norm_w=row(small["norm_w"]), b_f=row(jnp.pad(small["fox_b_f"], ((0, 0), (0, PAIR - FOX_HEADS)))),
                lam=lam, wb=wb, wc=wc, d=row(small["s5_d"]), gn_w=row(small["ret_gn_w"]))


def _layer_params(l, w_in_p, w_glu, w_out, stacked):
    return dict({k: (a, l) for k, a in stacked.items()}, w_in=w_in_p, w_glu=w_glu, w_out=w_out)


_SHARDED = ("w_in", "s5_w_glu", "w_out")
_WIRE = jnp.bfloat16


_RUNS = ((2568, 3592, O_GATE), (0, 1536, O_FQ), (1544, 2568, O_SU), (1536, 1544, O_FL))


def _shard_pieces():
    out = []
    for a, b, pad in _RUNS:
        while a < b:
            j = a // W_SHARD
            e = min(b, (j + 1) * W_SHARD)
            out.append((j, a - j * W_SHARD, e - j * W_SHARD, pad))
            pad, a = pad + e - a, e
    return out


def _gathered_w_in(g_in):
    cols = [g_in[j, :, a:e] for j, a, e, _ in _shard_pieces()]
    cols.append(jnp.zeros((D_MODEL, D_INP - O_FL - FOX_HEADS), g_in.dtype))
    return jnp.concatenate(cols, axis=1)


def _gathered_rows(g):
    return g.reshape(-1, g.shape[-1])


def _w_in_slots(g):
    w_in = g["w_in"].astype(_WIRE)
    slots = []
    for j in range(N_DEV):
        mine = sorted((a, e, pad) for jj, a, e, pad in _shard_pieces() if jj == j)
        slots.append(jnp.concatenate([w_in[:, pad:pad + e - a] for a, e, pad in mine], axis=1))
    return jnp.stack(slots)


def _row_slots(g):
    return [g["s5_w_glu"].reshape(N_DEV, D_S5 // N_DEV, D_S5).astype(_WIRE),
            g["w_out"].reshape(N_DEV, D_MODEL // N_DEV, D_MODEL).astype(_WIRE)]


def _step_grads(x, tgt, small, full=None, shards=None):
    L = x.shape[0]
    rope = _rope_tables(L) + (_ret_tables(min(TQ, L)),)
    stacked = _stacked_params(small)
    if shards is not None:
        nxt = (_gathered_w_in(_exchange([shards[0][0]], [False], "gather_layer0")[0]), None, None)
    saved, params = [], []
    for l in range(DEPTH):
        weights = nxt if shards is not None else tuple(f[l] for f in full)
        ride = None
        if shards is not None:
            arrs = [s[l + 1] for s in shards] if l + 1 < DEPTH else []
            arrs += [shards[1][0], shards[2][0]] if l == 0 else []
            ride = (arrs, [False] * len(arrs)) if arrs else None
        params.append(_layer_params(l, *weights, stacked))
        x, s, landed = _layer_fwd(x, params[l], rope, ride, late=shards is not None and l == 0)
        if landed:
            nxt = (_gathered_w_in(landed[0]), _gathered_rows(landed[1]), _gathered_rows(landed[2]))
        saved.append(s)
    loss, dx, dfw = _final_loss(x, small["final_norm_w"][None], tgt)
    grads, partials, waiting = [None] * DEPTH, [None] * DEPTH, None
    for l in reversed(range(DEPTH)):
        ride = (waiting, [True] * len(waiting)) if waiting is not None else None
        dx, grads[l], landed = _layer_bwd(dx, saved[l], params[l], rope, ride, early=ride is not None and l == 0)
        if waiting is not None:
            partials[l + 1] = landed[:3]
        if shards is not None:
            waiting = [_w_in_slots(grads[l])] + _row_slots(grads[l])
    stack = lambda n: jnp.stack([g[n] for g in grads])
    small_g = {n: stack(n) for n in ("norm_w", "fox_b_f", "s5_d", "ret_gn_w")}
    small_g.update(_s5_param_grads(small, stack("wb"), stack("wc"), stack("lam")), final_norm_w=dfw)
    if shards is None:
        return loss, dx, grads, small_g
    last = _exchange([waiting[0]] + [small_g[n].astype(_WIRE) for n in _SMALL], [True] + [False] * len(_SMALL),
                     "exchange_layer0")
    partials[0] = [last[0]] + landed[3:]
    return loss, dx, grads, small_g, partials, dict(zip(_SMALL, last[1:]))


_MESH = pl.DeviceIdType.MESH
_ANY = pl.BlockSpec(memory_space=pl.ANY)


def _me_and_peers():
    x, y, c = lax.axis_index("x"), lax.axis_index("y"), lax.axis_index("c")
    flip = lambda a, bit: (1 - a) if bit else a
    peers = []
    for r in range(1, N_DEV):
        px, py, pc = flip(x, (r >> 2) & 1), flip(y, (r >> 1) & 1), flip(c, r & 1)
        peers.append(((px, py, pc), 4 * px + 2 * py + pc))
    return 4 * x + 2 * y + c, peers


def _exchange_copies(srcs, dsts, sems, scatter):
    send_sems, recv_sems, local_sems = sems
    me, peers = _me_and_peers()
    pick = lambda t, to: srcs[t].at[to] if scatter[t] else srcs[t]
    own = [pltpu.make_async_copy(pick(t, me), dsts[t].at[me], local_sems.at[t]) for t in range(len(srcs))]
    sends, waits = [], []
    for r, (dev, idx) in enumerate(peers):
        for t in range(len(srcs)):
            for land, out in ((me, sends), (idx, waits)):
                out.append(pltpu.make_async_remote_copy(pick(t, idx), dsts[t].at[land], send_sems.at[t, r], recv_sems.at[t, r],
                                                        device_id=dev, device_id_type=_MESH))
    return own, sends, waits


def _exchange_start(srcs, dsts, sems, scatter):
    own, sends, _ = _exchange_copies(srcs, dsts, sems, scatter)
    for cp in own + sends:
        cp.start()


def _exchange_wait(srcs, dsts, sems, scatter):
    own, _, waits = _exchange_copies(srcs, dsts, sems, scatter)
    for cp in waits + own:
        cp.wait()


def _exchange_shapes(arrs, scatter):
    outs = [jax.ShapeDtypeStruct(a.shape if sc else (N_DEV,) + a.shape, a.dtype) for a, sc in zip(arrs, scatter)]
    n = len(arrs)
    sems = [pltpu.SemaphoreType.DMA((n, N_DEV - 1)), pltpu.SemaphoreType.DMA((n, N_DEV - 1)), pltpu.SemaphoreType.DMA((n,))]
    return outs, sems


def _exchange(arrs, scatter, name):
    n = len(arrs)

    def body(*refs):
        _exchange_start(refs[:n], refs[n:2 * n], refs[2 * n:], scatter)
        _exchange_wait(refs[:n], refs[n:2 * n], refs[2 * n:], scatter)

    outs, sems = _exchange_shapes(arrs, scatter)
    return _pallas(body, out_shape=tuple(outs), in_specs=[_ANY] * n, out_specs=tuple([_ANY] * n), scratch_shapes=sems,
                   name=name)(*arrs)


def _riding(body, n_in, n_out, ride, is_first, is_last):
    if ride is None:
        return body, [], [], [], []
    arrs, scatter = ride
    n = len(arrs)
    outs, sems = _exchange_shapes(arrs, scatter)

    def wrapped(*refs):
        ins, srcs = refs[:n_in], refs[n_in:n_in + n]
        own_outs, dsts = refs[n_in + n:n_in + n + n_out], refs[n_in + n + n_out:n_in + 2 * n + n_out]
        scratch, ex_sems = refs[n_in + 2 * n + n_out:-3], refs[-3:]

        @pl.when(is_first())
        def _():
            _exchange_start(srcs, dsts, ex_sems, scatter)

        body(*ins, *own_outs, *scratch)

        @pl.when(is_last())
        def _():
            _exchange_wait(srcs, dsts, ex_sems, scatter)

    return wrapped, list(arrs), [_ANY] * n, outs, sems


def _adamw_body(p_ref, w_ref, m_ref, v_ref, g_ref, d_ref, nm_ref, nv_ref):
    g = p_ref[0].astype(F32)
    for i in range(1, N_DEV):
        g = g + p_ref[i].astype(F32)
    nm = ADAM_B1 * m_ref[...] + (1.0 - ADAM_B1) * g
    nv = ADAM_B2 * v_ref[...] + (1.0 - ADAM_B2) * jnp.square(g)
    m_hat = nm / (1.0 - ADAM_B1 ** ADAM_STEP)
    v_hat = nv / (1.0 - ADAM_B2 ** ADAM_STEP)
    g_ref[...] = g
    d_ref[...] = -ADAM_LR * (m_hat / (jnp.sqrt(v_hat) + ADAM_EPS) + ADAM_WD * w_ref[...])
    nm_ref[...] = nm
    nv_ref[...] = nv


def _adamw(parts, w, m, v, name):
    n, nb, rows, cols = parts.shape
    tm = next(t for t in (256, 128, 64, 32, 16) if rows % t == 0)

    def body(*refs):
        _adamw_body(*refs)

    row = pl.BlockSpec((None, tm, cols), lambda b, i: (b, i, 0))
    return _pallas(body, out_shape=(jax.ShapeDtypeStruct((nb, rows, cols), F32),) * 4, grid=(nb, rows // tm),
                   in_specs=[pl.BlockSpec((n, None, tm, cols), lambda b, i: (0, b, i, 0)), row, row, row],
                   out_specs=(row,) * 4, name=name)(parts, w, m, v)


def _adamw_whole(parts, w, m, v, name):
    def body(*refs):
        _adamw_body(*refs)

    if w.ndim == 2:
        grid = (1,)
        slab = pl.BlockSpec(w.shape, lambda b: (0, 0))
        part = pl.BlockSpec(parts.shape, lambda b: (0, 0, 0))
    else:
        grid, rest = (w.shape[0],), w.shape[1:]
        zeros = (0,) * len(rest)
        slab = pl.BlockSpec((None,) + rest, lambda b: (b,) + zeros)
        part = pl.BlockSpec((N_DEV, None) + rest, lambda b: (0, b) + zeros)
    return _pallas(body, out_shape=(jax.ShapeDtypeStruct(w.shape, F32),) * 4, grid=grid,
                   in_specs=[part, slab, slab, slab], out_specs=(slab,) * 4, name=name)(parts, w, m, v)


_WEIGHTS = ("norm_w", "w_in", "fox_b_f", "s5_a_re", "s5_a_im", "s5_b_re", "s5_b_im", "s5_c_re", "s5_c_im", "s5_d",
            "s5_log_dt", "s5_w_glu", "ret_gn_w", "w_out", "final_norm_w")
_SMALL = tuple(n for n in _WEIGHTS if n not in _SHARDED)


def kernel(x, norm_w, w_in, fox_b_f, s5_a_re, s5_a_im, s5_b_re, s5_b_im, s5_c_re, s5_c_im, s5_d, s5_log_dt, s5_w_glu, ret_gn_w, w_out, final_norm_w, loss_target, m_norm_w, m_w_in, m_fox_b_f, m_s5_a_re, m_s5_a_im, m_s5_b_re, m_s5_b_im, m_s5_c_re, m_s5_c_im, m_s5_d, m_s5_log_dt, m_s5_w_glu, m_ret_gn_w, m_w_out, m_final_norm_w, v_norm_w, v_w_in, v_fox_b_f, v_s5_a_re, v_s5_a_im, v_s5_b_re, v_s5_b_im, v_s5_c_re, v_s5_c_im, v_s5_d, v_s5_log_dt, v_s5_w_glu, v_ret_gn_w, v_w_out, v_final_norm_w):
    w = dict(norm_w=norm_w, w_in=w_in, fox_b_f=fox_b_f, s5_a_re=s5_a_re, s5_a_im=s5_a_im, s5_b_re=s5_b_re, s5_b_im=s5_b_im,
             s5_c_re=s5_c_re, s5_c_im=s5_c_im, s5_d=s5_d, s5_log_dt=s5_log_dt, s5_w_glu=s5_w_glu, ret_gn_w=ret_gn_w,
             w_out=w_out, final_norm_w=final_norm_w)
    m = dict(norm_w=m_norm_w, w_in=m_w_in, fox_b_f=m_fox_b_f, s5_a_re=m_s5_a_re, s5_a_im=m_s5_a_im, s5_b_re=m_s5_b_re,
             s5_b_im=m_s5_b_im, s5_c_re=m_s5_c_re, s5_c_im=m_s5_c_im, s5_d=m_s5_d, s5_log_dt=m_s5_log_dt,
             s5_w_glu=m_s5_w_glu, ret_gn_w=m_ret_gn_w, w_out=m_w_out, final_norm_w=m_final_norm_w)
    v = dict(norm_w=v_norm_w, w_in=v_w_in, fox_b_f=v_fox_b_f, s5_a_re=v_s5_a_re, s5_a_im=v_s5_a_im, s5_b_re=v_s5_b_re,
             s5_b_im=v_s5_b_im, s5_c_re=v_s5_c_re, s5_c_im=v_s5_c_im, s5_d=v_s5_d, s5_log_dt=v_s5_log_dt,
             s5_w_glu=v_s5_w_glu, ret_gn_w=v_ret_gn_w, w_out=v_w_out, final_norm_w=v_final_norm_w)

    small = {n: w[n] for n in _SMALL}
    loss, dx, _, _, partials, r_small = _step_grads(x[0], loss_target[0], small, shards=[w[n].astype(_MXU) for n in _SHARDED])

    res = {}
    for t, n in enumerate(_SHARDED):
        res[n] = _adamw(jnp.stack([partials[l][t] for l in range(DEPTH)], axis=1), w[n], m[n], v[n], "adamw_" + n)
    for n in _SMALL:
        shape = w[n].shape
        view = (1,) + shape if len(shape) == 1 else shape[:2] + (-1,) if n in _DENSE else shape
        outs = _adamw_whole(r_small[n], *[d[n].reshape(view) for d in (w, m, v)], "adamw_" + n)
        res[n] = [o.reshape(shape) for o in outs]

    loss = lax.psum(loss[0, 0], ("x", "y", "c"))
    return (loss, dx[None], *[res[n][0] for n in _WEIGHTS], *[res[n][1] for n in _WEIGHTS],
            *[res[n][2] for n in _WEIGHTS], *[res[n][3] for n in _WEIGHTS])
```

```python
import math

import jax
import jax.numpy as jnp
from jax import lax
from jax.experimental import pallas as pl
from jax.experimental.pallas import tpu as pltpu

F32 = jnp.float32
_MXU = jnp.bfloat16
_HI = lax.Precision.HIGHEST

N_DEV = 8
DEPTH = 4
D_MODEL = 1024
HEAD_DIM = 64
D_FOX = 512
FOX_HEADS = 8
D_S5 = 256
S5_GROUPS = 16
S5_GROUP_CH = 16
S5_STATE = 64
S5_CH = S5_GROUPS * S5_STATE
D_RET = 256
RET_HEADS = 4
CHUNK = 64
ROPE_BASE = 10000.0
EPS = 1e-6
D_IN = 3592
D_INP = 3712
W_SHARD = D_IN // N_DEV
O_GATE, O_FQ, O_FK, O_FV, O_SU, O_RQ, O_RK, O_RV, O_FL = 0, 1024, 1536, 2048, 2560, 2816, 3072, 3328, 3584

ADAM_LR, ADAM_B1, ADAM_B2, ADAM_EPS, ADAM_WD, ADAM_STEP = 0.001, 0.9, 0.999, 1e-08, 0.01, 10

TM = 256
TMB = 512
TQ = 512
TS = 512
NEG = -1e30
VMEM_BIG = 56 * 1024 * 1024


def _pallas(body, **kw):
    return pl.pallas_call(body, **kw)


def _whole(shape):
    n = len(shape)
    return pl.BlockSpec(shape, lambda *_: (0,) * n)


def _rows(tm, width, col=0):
    return pl.BlockSpec((tm, width), lambda i: (i, col))


def _of_layer(param):
    a, l = param
    return a, pl.BlockSpec((None,) + a.shape[1:], lambda *_: (l,) + (0,) * (a.ndim - 1))


def _dot(a, b, dims=(((1,), (0,)), ((), ()))):
    return lax.dot_general(a.astype(_MXU), b.astype(_MXU), dims, preferred_element_type=F32)


_NT = (((1,), (1,)), ((), ()))
_TN = (((0,), (0,)), ((), ()))


def _norm_inproj(x, g, w):
    L = x.shape[0]

    def body(x_ref, g_ref, w_ref, p_ref, h_ref):
        xv = x_ref[...]
        r = lax.rsqrt(jnp.mean(xv * xv, axis=-1, keepdims=True) + EPS)
        h = (xv * r * g_ref[...]).astype(_MXU)
        h_ref[...] = h
        p_ref[...] = _dot(h, w_ref[...])

    tm = min(TMB, L)
    g, g_spec = _of_layer(g)
    return _pallas(body, out_shape=(jax.ShapeDtypeStruct((L, D_INP), F32), jax.ShapeDtypeStruct((L, D_MODEL), _MXU)),
                   grid=(L // tm,),
                   in_specs=[_rows(tm, D_MODEL), g_spec,
                             pl.BlockSpec((D_MODEL, D_INP), lambda i: (0, 0), pipeline_mode=pl.Buffered(1))],
                   out_specs=(_rows(tm, D_INP), _rows(tm, D_MODEL)), name="norm_inproj",
                   compiler_params=pltpu.CompilerParams(vmem_limit_bytes=VMEM_BIG))(x, g, w)


def _rms_bwd(xv, g, dh):
    r = lax.rsqrt(jnp.mean(xv * xv, axis=-1, keepdims=True) + EPS)
    xh = xv * r
    dg = jnp.sum(dh * xh, axis=0, keepdims=True)
    dxh = dh * g
    dx = r * (dxh - xh * jnp.mean(dxh * xh, axis=-1, keepdims=True))
    return dx, dg


def _inproj_bwd(pieces, w, x, g, dres, h):
    L = x.shape[0]
    n = len(pieces)
    nb = L // TM

    def body(*refs):
        w_ref, x_ref, g_ref, dr_ref, h_ref, dx_ref, dg_ref, dw_ref, acc_sc = refs[n:]
        step = pl.program_id(0)

        @pl.when(step == 0)
        def _():
            acc_sc[...] = jnp.zeros((D_MODEL, D_INP), F32)
            dg_ref[...] = jnp.zeros((1, D_MODEL), F32)

        hv = h_ref[...]
        dh = jnp.zeros((TM, D_MODEL), F32)
        off = 0
        for r in refs[:n]:
            cols = slice(off, off + r.shape[1])
            off += r.shape[1]
            piece = r[...].astype(_MXU)
            dh = dh + _dot(piece, w_ref[:, cols], _NT)
            acc_sc[:, cols] += _dot(hv, piece, _TN)
        dx, dg = _rms_bwd(x_ref[...], g_ref[...], dh)
        dx_ref[...] = dx + dr_ref[...]
        dg_ref[...] += dg

        @pl.when(step == nb - 1)
        def _():
            dw_ref[...] = acc_sc[...].astype(_WIRE)

    resident = pl.BlockSpec((D_MODEL, D_INP), lambda i: (0, 0), pipeline_mode=pl.Buffered(1))
    g, g_spec = _of_layer(g)
    return _pallas(body, out_shape=(jax.ShapeDtypeStruct((L, D_MODEL), F32), jax.ShapeDtypeStruct((1, D_MODEL), F32),
                                    jax.ShapeDtypeStruct((D_MODEL, D_INP), _WIRE)),
                   grid=(nb,),
                   in_specs=[_rows(TM, p.shape[1]) for p in pieces]
                   + [resident, _rows(TM, D_MODEL), g_spec, _rows(TM, D_MODEL), _rows(TM, D_MODEL)],
                   out_specs=(_rows(TM, D_MODEL), _whole((1, D_MODEL)), resident),
                   scratch_shapes=[pltpu.VMEM((D_MODEL, D_INP), F32)], name="inproj_bwd",
                   compiler_params=pltpu.CompilerParams(vmem_limit_bytes=VMEM_BIG))(*pieces, w, x, g, dres, h)


PAIR = 2 * HEAD_DIM
N_AUX = 3


def _own(shape, h):
    return lax.broadcasted_iota(jnp.int32, shape, len(shape) - 1) // HEAD_DIM == h


def _hi_dot(a, b):
    return jnp.dot(a, b, precision=_HI, preferred_element_type=F32)


def _tri(n, lower):
    r = lax.broadcasted_iota(jnp.int32, (n, n), 0)
    c = lax.broadcasted_iota(jnp.int32, (n, n), 1)
    return jnp.where(r >= c if lower else r <= c, 1.0, 0.0).astype(F32)


def _fox_cumsum(proj, b):
    L = proj.shape[0]

    def body(fl_ref, b_ref, c_ref, carry_sc):
        @pl.when(pl.program_id(0) == 0)
        def _():
            carry_sc[...] = jnp.zeros((1, PAIR), F32)

        lane = lax.broadcasted_iota(jnp.int32, (TM, PAIR), 1)
        lf = jnp.where(lane < FOX_HEADS, jax.nn.log_sigmoid(fl_ref[...] + b_ref[...]), 0.0)
        cs = _hi_dot(_tri(TM, True), lf) + carry_sc[...]
        c_ref[...] = cs
        carry_sc[...] = cs[TM - 1:TM, :]

    b, b_spec = _of_layer(b)
    return _pallas(body, out_shape=jax.ShapeDtypeStruct((L, PAIR), F32), grid=(L // TM,),
                   in_specs=[_rows(TM, PAIR, O_FL // PAIR), b_spec], out_specs=_rows(TM, PAIR),
                   scratch_shapes=[pltpu.VMEM((1, PAIR), F32)], name="fox_cumsum")(proj, b)


def _fox_prep(proj, c):
    L = proj.shape[0]

    def body(q_ref, k_ref, v_ref, c_ref, qa_ref, ka_ref, kat_ref, vt_ref):
        lane = lax.broadcasted_iota(jnp.int32, (TM, PAIR), 1)
        cv = c_ref[...]
        for p in range(FOX_HEADS // 2):
            cols = slice(PAIR * p, PAIR * (p + 1))
            q2, k2 = q_ref[:, cols], k_ref[:, cols]
            vt_ref[p] = v_ref[:, cols].T.astype(_MXU)
            for e in range(2):
                h = 2 * p + e
                own = lane // HEAD_DIM == e
                a = lane - (HEAD_DIM if e == 0 else 0)
                rest = jnp.broadcast_to(cv[:, h:h + 1], (TM, PAIR))
                aux_q = jnp.where((a >= N_AUX) & (a < 2 * N_AUX), 1.0, 0.0)
                aux_k = jnp.where((a >= 0) & (a < N_AUX), 1.0, 0.0)
                for n in range(N_AUX):
                    part = rest.astype(_MXU).astype(F32)
                    rest = rest - part
                    aux_q = jnp.where(a == n, part, aux_q)
                    aux_k = jnp.where(a == N_AUX + n, -part, aux_k)
                ka = jnp.where(own, k2, aux_k)
                qa_ref[h] = jnp.where(own, q2 * (1.0 / math.sqrt(HEAD_DIM)), aux_q).astype(_MXU)
                ka_ref[h] = ka.astype(_MXU)
                kat_ref[h] = ka.T.astype(_MXU)

    hl = jax.ShapeDtypeStruct((FOX_HEADS, L, PAIR), _MXU)
    nat = lambda o: _rows(TM, D_FOX, o // D_FOX)
    rows = pl.BlockSpec((FOX_HEADS, TM, PAIR), lambda i: (0, i, 0))
    return _pallas(
        body, out_shape=(hl, hl, jax.ShapeDtypeStruct((FOX_HEADS, PAIR, L), _MXU),
                         jax.ShapeDtypeStruct((FOX_HEADS // 2, PAIR, L), _MXU)),
        grid=(L // TM,), in_specs=[nat(O_FQ), nat(O_FK), nat(O_FV), _rows(TM, PAIR)],
        out_specs=(rows, rows, pl.BlockSpec((FOX_HEADS, PAIR, TM), lambda i: (0, 0, i)),
                   pl.BlockSpec((FOX_HEADS // 2, PAIR, TM), lambda i: (0, 0, i))),
        name="fox_prep")(proj, proj, proj, c)


def _key_le_query(tq):
    return lax.broadcasted_iota(jnp.int32, (tq, tq), 0) <= lax.broadcasted_iota(jnp.int32, (tq, tq), 1)


def _grid_ends(n0, n1):
    first = lambda: (pl.program_id(0) == 0) & (pl.program_id(1) == 0)
    last = lambda: (pl.program_id(0) == n0 - 1) & (pl.program_id(1) == n1 - 1)
    return first, last


def _fox_fwd(qa, ka, vt, ride=None):
    H, L, _ = qa.shape
    tq = min(TQ, L)
    nq = L // tq

    def body(qa_ref, ka_ref, vt_ref, o_ref, lse_ref, m_sc, l_sc, acc_sc):
        i = pl.program_id(1)
        m_sc[...] = jnp.full((2, 1, tq), NEG, F32)
        l_sc[...] = jnp.zeros((2, 1, tq), F32)
        acc_sc[...] = jnp.zeros((2, HEAD_DIM, tq), F32)

        def block(j, nk, masked):
            keys = pl.ds(pl.multiple_of(j * tq, tq), nk * tq)
            vt_blk = vt_ref[:, keys]
            sts = [_dot(ka_ref[e, keys, :], qa_ref[e], _NT) for e in range(2)]
            pts, alphas = [], []
            for e in range(2):
                st = jnp.where(_key_le_query(tq), sts[e], NEG) if masked else sts[e]
                m_prev = m_sc[e]
                m_new = jnp.maximum(m_prev, jnp.max(st, axis=0, keepdims=True))
                alphas.append(jnp.exp(m_prev - m_new))
                pt = jnp.exp(st - m_new)
                l_sc[e] = alphas[e] * l_sc[e] + jnp.sum(pt, axis=0, keepdims=True)
                m_sc[e] = m_new
                pts.append(pt.astype(_MXU))
            for e in range(2):
                acc_sc[e] = alphas[e] * acc_sc[e] + _dot(vt_blk[HEAD_DIM * e:HEAD_DIM * (e + 1)], pts[e])

        def two_blocks(jj, carry):
            block(2 * jj, 2, False)
            return carry

        lax.fori_loop(0, i // 2, two_blocks, 0)

        @pl.when(i % 2 == 1)
        def _():
            block(i - 1, 1, False)

        block(i, 1, True)
        o_ref[...] = jnp.concatenate([acc_sc[0] / l_sc[0], acc_sc[1] / l_sc[1]], axis=0).T
        for e in range(2):
            lse_ref[e] = m_sc[e] + jnp.log(l_sc[e])

    body, ex_in, ex_specs, ex_out, ex_sems = _riding(body, 3, 2, ride, *_grid_ends(H // 2, nq))
    res = _pallas(
        body, out_shape=(jax.ShapeDtypeStruct((L, D_FOX), F32), jax.ShapeDtypeStruct((H, 1, L), F32), *ex_out),
        grid=(H // 2, nq),
        in_specs=[pl.BlockSpec((2, tq, PAIR), lambda p, i: (p, i, 0)), pl.BlockSpec((2, L, PAIR), lambda p, i: (p, 0, 0)),
                  pl.BlockSpec((None, PAIR, L), lambda p, i: (p, 0, 0)), *ex_specs],
        out_specs=(pl.BlockSpec((tq, PAIR), lambda p, i: (i, p)), pl.BlockSpec((2, 1, tq), lambda p, i: (p, 0, i)),
                   *ex_specs),
        scratch_shapes=[pltpu.VMEM((2, 1, tq), F32), pltpu.VMEM((2, 1, tq), F32), pltpu.VMEM((2, HEAD_DIM, tq), F32), *ex_sems],
        name="fox_fwd" if ride is None else "fox_fwd_gather")(qa, ka, vt, *ex_in)
    return res[0], res[1], list(res[2:])


def _fox_bwd(qa, ka, kat, proj, do, o, lse, ride=None):
    H, L, _ = qa.shape
    tq = min(TQ, L)
    nq = L // tq

    def body(qa_ref, ka_ref, kat_ref, v_ref, do_ref, o_ref, lse_ref, dqt_ref, dk_ref, dv_ref, delta_sc, dk_sc, dv_sc):
        j = pl.program_id(1)

        @pl.when(j == 0)
        def _():
            head_rows = (lax.broadcasted_iota(jnp.int32, (8, PAIR), 1) // HEAD_DIM
                         == lax.broadcasted_iota(jnp.int32, (8, PAIR), 0)).astype(F32)
            delta_sc[...] = lax.dot_general(head_rows, do_ref[...] * o_ref[...], _NT, precision=_HI,
                                            preferred_element_type=F32)
            dqt_ref[...] = jnp.zeros((2, PAIR, L), F32)

        dk_sc[...] = jnp.zeros((2, tq, PAIR), F32)
        dv_sc[...] = jnp.zeros((tq, PAIR), F32)
        vb = v_ref[...]

        def block(i, masked):
            qs = pl.ds(pl.multiple_of(i * tq, tq), tq)
            dob = do_ref[qs, :]
            for e in range(2):
                own = _own((tq, PAIR), e)
                qh = qa_ref[e, qs, :]
                pt = jnp.exp(_dot(ka_ref[e], qh, _NT) - lse_ref[e, :, qs])
                if masked:
                    pt = jnp.where(_key_le_query(tq), pt, 0.0)
                dv_sc[...] += _dot(pt, jnp.where(own, dob, 0.0))
                dpt = _dot(jnp.where(own, vb, 0.0), dob, _NT)
                ds = (pt * (dpt - delta_sc[e:e + 1, qs])).astype(_MXU)
                dk_sc[e] += _dot(ds, qh)
                dqt_ref[e, :, qs] += _dot(kat_ref[e], ds)

        def off_diagonal(i, carry):
            block(i, False)
            return carry

        block(j, True)
        lax.fori_loop(j + 1, nq, off_diagonal, 0)
        dk_ref[...] = dk_sc[...]
        dv_ref[...] = dv_sc[...]

    nat = pl.BlockSpec((L, PAIR), lambda p, j: (0, p))
    body, ex_in, ex_specs, ex_out, ex_sems = _riding(body, 7, 3, ride, *_grid_ends(H // 2, nq))
    res = _pallas(
        body, out_shape=(jax.ShapeDtypeStruct((H, PAIR, L), F32), jax.ShapeDtypeStruct((H, L, PAIR), F32),
                         jax.ShapeDtypeStruct((L, D_FOX), F32), *ex_out),
        grid=(H // 2, nq),
        in_specs=[pl.BlockSpec((2, L, PAIR), lambda p, j: (p, 0, 0)), pl.BlockSpec((2, tq, PAIR), lambda p, j: (p, j, 0)),
                  pl.BlockSpec((2, PAIR, tq), lambda p, j: (p, 0, j)),
                  pl.BlockSpec((tq, PAIR), lambda p, j: (j, O_FV // PAIR + p)), nat, nat,
                  pl.BlockSpec((2, 1, L), lambda p, j: (p, 0, 0)), *ex_specs],
        out_specs=(pl.BlockSpec((2, PAIR, L), lambda p, j: (p, 0, 0)), pl.BlockSpec((2, tq, PAIR), lambda p, j: (p, j, 0)),
                   pl.BlockSpec((tq, PAIR), lambda p, j: (j, p)), *ex_specs),
        scratch_shapes=[pltpu.VMEM((8, L), F32), pltpu.VMEM((2, tq, PAIR), F32), pltpu.VMEM((tq, PAIR), F32), *ex_sems],
        name="fox_bwd" if ride is None else "fox_bwd_exchange",
        compiler_params=pltpu.CompilerParams(vmem_limit_bytes=VMEM_BIG))(qa, ka, kat, proj, do, o, lse, *ex_in)
    return res[0], res[1], res[2], list(res[3:])


def _fox_post_bwd(dqt, dkraw, proj, b):
    L = proj.shape[0]
    nb = L // TM

    def body(dqt_ref, dkr_ref, fl_ref, b_ref, dq_ref, dk_ref, dfl_ref, db_ref, carry_sc):
        first = pl.program_id(0) == 0

        @pl.when(first)
        def _():
            carry_sc[...] = jnp.zeros((1, PAIR), F32)

        lane = lax.broadcasted_iota(jnp.int32, (TM, PAIR), 1)
        rr = lax.broadcasted_iota(jnp.int32, (PAIR, PAIR), 0)
        cc = lax.broadcasted_iota(jnp.int32, (PAIR, PAIR), 1)
        dc = jnp.zeros((TM, PAIR), F32)
        for p in range(FOX_HEADS // 2):
            cols = slice(PAIR * p, PAIR * (p + 1))
            dqs = [dqt_ref[2 * p + e].T for e in range(2)]
            dks = [dkr_ref[2 * p + e] for e in range(2)]
            dq_ref[:, cols] = jnp.where(lane < HEAD_DIM, dqs[0], dqs[1]) * (1.0 / math.sqrt(HEAD_DIM))
            dk_ref[:, cols] = jnp.where(lane < HEAD_DIM, dks[0], dks[1])
            sums = jnp.zeros((TM, PAIR), F32)
            place = jnp.zeros((PAIR, PAIR), F32)
            for e in range(2):
                base = HEAD_DIM if e == 0 else 0
                sums = jnp.where(lane == base, dqs[e], jnp.where(lane == base + N_AUX, -dks[e], sums))
                place = jnp.where(((rr == base) | (rr == base + N_AUX)) & (cc == 2 * p + e), 1.0, place)
            dc = dc + _hi_dot(sums, place)
        rs = _hi_dot(_tri(TM, False), dc) + carry_sc[...]
        carry_sc[...] = rs[0:1, :]
        dfl = jnp.where(lane < FOX_HEADS, rs * jax.nn.sigmoid(-(fl_ref[...] + b_ref[...])), 0.0)
        dfl_ref[...] = dfl
        db = jnp.sum(dfl, axis=0, keepdims=True)

        @pl.when(first)
        def _():
            db_ref[...] = db

        @pl.when(jnp.logical_not(first))
        def _():
            db_ref[...] += db

    rev = lambda i: nb - 1 - i
    b, b_spec = _of_layer(b)
    nat = pl.BlockSpec((TM, D_FOX), lambda i: (rev(i), 0))
    return _pallas(
        body, out_shape=(jax.ShapeDtypeStruct((L, D_FOX), F32),) * 2
        + (jax.ShapeDtypeStruct((L, PAIR), F32), jax.ShapeDtypeStruct((1, PAIR), F32)),
        grid=(nb,),
        in_specs=[pl.BlockSpec((FOX_HEADS, PAIR, TM), lambda i: (0, 0, rev(i))),
                  pl.BlockSpec((FOX_HEADS, TM, PAIR), lambda i: (0, rev(i), 0)),
                  pl.BlockSpec((TM, PAIR), lambda i: (rev(i), O_FL // PAIR)), b_spec],
        out_specs=(nat, nat, pl.BlockSpec((TM, PAIR), lambda i: (rev(i), 0)), _whole((1, PAIR))),
        scratch_shapes=[pltpu.VMEM((1, PAIR), F32)], name="fox_post_bwd")(dqt, dkraw, proj, b)


def _s5_expand():
    r = lax.broadcasted_iota(jnp.int32, (S5_STATE, S5_STATE * S5_GROUP_CH), 0)
    c = lax.broadcasted_iota(jnp.int32, (S5_STATE, S5_STATE * S5_GROUP_CH), 1)
    return jnp.where(c // S5_GROUP_CH == r, 1.0, 0.0).astype(F32)


def _s5_disc_math(ar, ai, ldt, br, bi):
    dt = jnp.exp(ldt)
    mag = jnp.exp(ar * dt)
    lr = mag * jnp.cos(ai * dt)
    li = mag * jnp.sin(ai * dt)
    den = ar * ar + ai * ai
    fr = ((lr - 1.0) * ar + li * ai) / den
    fi = (li * ar - (lr - 1.0) * ai) / den
    e = _s5_expand()
    fre = jnp.dot(fr, e, precision=_HI, preferred_element_type=F32)
    fie = jnp.dot(fi, e, precision=_HI, preferred_element_type=F32)
    return lr, li, fre * br - fie * bi, fre * bi + fie * br


def _layer_blocks(arrs):
    return [pl.BlockSpec((None,) + a.shape[1:], lambda l: (l, 0, 0)) for a in arrs]


def _s5_disc(ar, ai, ldt, br, bi):
    def body(ar_ref, ai_ref, ldt_ref, br_ref, bi_ref, lr_ref, li_ref, bbr_ref, bbi_ref):
        lr, li, bbr, bbi = _s5_disc_math(ar_ref[...], ai_ref[...], ldt_ref[...], br_ref[...], bi_ref[...])
        lr_ref[...] = lr
        li_ref[...] = li
        bbr_ref[...] = bbr
        bbi_ref[...] = bbi

    ins = (ar, ai, ldt, br, bi)
    outs = (ar, ai, br, bi)
    return _pallas(body, out_shape=tuple(jax.ShapeDtypeStruct(a.shape, F32) for a in outs), grid=(DEPTH,),
                   in_specs=_layer_blocks(ins), out_specs=tuple(_layer_blocks(outs)), name="s5_disc")(*ins)


def _s5_disc_bwd(ar, ai, ldt, br, bi, dlr, dli, dbbr, dbbi):
    def body(ar_ref, ai_ref, ldt_ref, br_ref, bi_ref, dlr_ref, dli_ref, dbbr_ref, dbbi_ref,
             dar_ref, dai_ref, dldt_ref, dbr_ref, dbi_ref):
        _, vjp = jax.vjp(_s5_disc_math, ar_ref[...], ai_ref[...], ldt_ref[...], br_ref[...], bi_ref[...])
        dar, dai, dldt, dbr, dbi = vjp((dlr_ref[...], dli_ref[...], dbbr_ref[...], dbbi_ref[...]))
        dar_ref[...] = dar
        dai_ref[...] = dai
        dldt_ref[...] = dldt
        dbr_ref[...] = dbr
        dbi_ref[...] = dbi

    ins = (ar, ai, ldt, br, bi, dlr, dli, dbbr, dbbi)
    outs = (ar, ai, ldt, br, bi)
    return _pallas(body, out_shape=tuple(jax.ShapeDtypeStruct(a.shape, F32) for a in outs), grid=(DEPTH,),
                   in_specs=_layer_blocks(ins), out_specs=tuple(_layer_blocks(outs)), name="s5_disc_bwd")(*ins)


SLAB = 2 * S5_CH // 128


def _slab_rows(s, ts):
    return pl.ds(s, ts, stride=SLAB)


def _slab_pair(ref, s, ts):
    return jnp.concatenate([ref[_slab_rows(s, ts), :].astype(_MXU), ref[_slab_rows(s + 1, ts), :].astype(_MXU)], axis=-1)


def _s5_fwd(proj, wb, wc, lam, d, w_glu):
    L = proj.shape[0]
    ts = min(TS, L)

    def body(u_ref, wb_ref, wc_ref, lam_ref, d_ref, wg_ref, xs_ref, ypre_ref, ys_ref, b_sc, c_sc):
        @pl.when(pl.program_id(0) == 0)
        def _():
            c_sc[...] = jnp.zeros((SLAB, 128), F32)

        u = u_ref[...]
        ub = u.astype(_MXU)
        for s in range(0, SLAB, 2):
            b2 = _dot(ub, wb_ref[:, 128 * s:128 * (s + 2)])
            b_sc[_slab_rows(s, ts), :] = b2[:, :128]
            b_sc[_slab_rows(s + 1, ts), :] = b2[:, 128:]
        lr, li = lam_ref[0:8, :], lam_ref[8:16, :]

        def step(t, carry):
            xr, xi = carry
            row = pl.multiple_of(t * SLAB, SLAB)
            nr = lr * xr - li * xi + b_sc[pl.ds(row, 8), :]
            ni = lr * xi + li * xr + b_sc[pl.ds(row + 8, 8), :]
            xs_ref[pl.ds(row, 8), :] = nr
            xs_ref[pl.ds(row + 8, 8), :] = ni
            return nr, ni

        xr, xi = lax.fori_loop(0, ts, step, (c_sc[0:8, :], c_sc[8:16, :]), unroll=8)
        c_sc[0:8, :] = xr
        c_sc[8:16, :] = xi
        y = jnp.zeros((ts, D_S5), F32)
        for s in range(0, SLAB, 2):
            y = y + _dot(_slab_pair(xs_ref, s, ts), wc_ref[128 * s:128 * (s + 2), :])
        ypre_ref[...] = y
        y1 = jax.nn.gelu(y + d_ref[...] * u)
        ys_ref[...] = y1 * jax.nn.sigmoid(_dot(y1, wg_ref[...]))

    row = _rows(ts, D_S5)
    slabs = pl.BlockSpec((ts * SLAB, 128), lambda n: (n, 0))
    (wb, wb_spec), (wc, wc_spec), (lam, lam_spec), (d, d_spec) = (_of_layer(a) for a in (wb, wc, lam, d))
    return _pallas(
        body, out_shape=(jax.ShapeDtypeStruct((L * SLAB, 128), F32), jax.ShapeDtypeStruct((L, D_S5), F32),
                         jax.ShapeDtypeStruct((L, D_S5), F32)),
        grid=(L // ts,),
        in_specs=[_rows(ts, D_S5, O_SU // D_S5), wb_spec, wc_spec, lam_spec, d_spec, _whole((D_S5, D_S5))],
        out_specs=(slabs, row, row),
        scratch_shapes=[pltpu.VMEM((ts * SLAB, 128), F32), pltpu.VMEM((SLAB, 128), F32)], name="s5_fwd")(
            proj, wb, wc, lam, d, w_glu)


def _s5_bwd(proj, ypre, dys, xs, wb, wc, lam, d, w_glu):
    L = proj.shape[0]
    ts = min(TS, L)
    nb = L // ts

    def body(u_ref, y_ref, dys_ref, xs_ref, xp_ref, wb_ref, wc_ref, lam_ref, d_ref, wg_ref,
             du_ref, dwb_ref, dwc_ref, dlam_ref, dd_ref, dwg_ref, dx_sc, g_sc, c_sc):
        n = pl.program_id(0)

        @pl.when(n == 0)
        def _():
            c_sc[...] = jnp.zeros((SLAB, 128), F32)
            dlam_ref[...] = jnp.zeros((SLAB, 128), F32)
            dwb_ref[...] = jnp.zeros((D_S5, 2 * S5_CH), F32)
            dwc_ref[...] = jnp.zeros((2 * S5_CH, D_S5), F32)
            dd_ref[...] = jnp.zeros((1, D_S5), F32)
            dwg_ref[...] = jnp.zeros((D_S5, D_S5), F32)

        u, dv, dout = u_ref[...], d_ref[...], dys_ref[...]
        y1, gelu_vjp = jax.vjp(jax.nn.gelu, y_ref[...] + dv * u)
        sg = jax.nn.sigmoid(_dot(y1, wg_ref[...]))
        dz = dout * y1 * sg * (1.0 - sg)
        dy, = gelu_vjp(dout * sg + _dot(dz, wg_ref[...], _NT))
        dd_ref[...] += jnp.sum(dy * u, axis=0, keepdims=True)
        dwg_ref[...] += _dot(y1, dz, _TN)
        dyb = dy.astype(_MXU)
        for s in range(0, SLAB, 2):
            cols = slice(128 * s, 128 * (s + 2))
            dx2 = _dot(dyb, wc_ref[cols, :], _NT)
            dx_sc[_slab_rows(s, ts), :] = dx2[:, :128]
            dx_sc[_slab_rows(s + 1, ts), :] = dx2[:, 128:]
            dwc_ref[cols, :] += _dot(_slab_pair(xs_ref, s, ts), dyb, _TN)
        lr, li = lam_ref[0:8, :], lam_ref[8:16, :]

        def adjoint(row, pr, pi, carry):
            gr, gi, ar, ai = carry
            nr = dx_sc[pl.ds(row, 8), :] + lr * gr + li * gi
            ni = dx_sc[pl.ds(row + 8, 8), :] - li * gr + lr * gi
            g_sc[pl.ds(row, 8), :] = nr
            g_sc[pl.ds(row + 8, 8), :] = ni
            return nr, ni, ar + nr * pr + ni * pi, ai - nr * pi + ni * pr

        def step(k, carry):
            row = pl.multiple_of((ts - 1 - k) * SLAB, SLAB)
            prev = pl.multiple_of((ts - 2 - k) * SLAB, SLAB)
            return adjoint(row, xs_ref[pl.ds(prev, 8), :], xs_ref[pl.ds(prev + 8, 8), :], carry)

        z = jnp.zeros((8, 128), F32)
        carry = lax.fori_loop(0, ts - 1, step, (c_sc[0:8, :], c_sc[8:16, :], z, z), unroll=8)
        has_prev = jnp.where(n == nb - 1, 0.0, 1.0)
        gr, gi, ar, ai = adjoint(0, xp_ref[0:8, :] * has_prev, xp_ref[8:16, :] * has_prev, carry)
        c_sc[0:8, :] = gr
        c_sc[8:16, :] = gi
        dlam_ref[0:8, :] += ar
        dlam_ref[8:16, :] += ai
        ub = u.astype(_MXU)
        du = dy * dv
        for s in range(0, SLAB, 2):
            cols = slice(128 * s, 128 * (s + 2))
            gs = _slab_pair(g_sc, s, ts)
            du = du + _dot(gs, wb_ref[:, cols], _NT)
            dwb_ref[:, cols] += _dot(ub, gs, _TN)
        du_ref[...] = du

    blk = lambda n: nb - 1 - n
    row = pl.BlockSpec((ts, D_S5), lambda n: (blk(n), 0))
    (wb, wb_spec), (wc, wc_spec), (lam, lam_spec), (d, d_spec) = (_of_layer(a) for a in (wb, wc, lam, d))
    return _pallas(
        body, out_shape=(jax.ShapeDtypeStruct((L, D_S5), F32), jax.ShapeDtypeStruct((D_S5, 2 * S5_CH), F32),
                         jax.ShapeDtypeStruct((2 * S5_CH, D_S5), F32), jax.ShapeDtypeStruct((SLAB, 128), F32),
                         jax.ShapeDtypeStruct((1, D_S5), F32), jax.ShapeDtypeStruct((D_S5, D_S5), F32)),
        grid=(nb,),
        in_specs=[pl.BlockSpec((ts, D_S5), lambda n: (blk(n), O_SU // D_S5)), row, row,
                  pl.BlockSpec((ts * SLAB, 128), lambda n: (blk(n), 0)),
                  pl.BlockSpec((SLAB, 128), lambda n: (jnp.maximum(blk(n) * ts - 1, 0), 0)),
                  wb_spec, wc_spec, lam_spec, d_spec, _whole((D_S5, D_S5))],
        out_specs=(row, _whole((D_S5, 2 * S5_CH)), _whole((2 * S5_CH, D_S5)), _whole((SLAB, 128)), _whole((1, D_S5)),
                   _whole((D_S5, D_S5))),
        scratch_shapes=[pltpu.VMEM((ts * SLAB, 128), F32), pltpu.VMEM((ts * SLAB, 128), F32), pltpu.VMEM((SLAB, 128), F32)],
        name="s5_bwd")(proj, ypre, dys, xs, xs, wb, wc, lam, d, w_glu)


def _rot(z, cos, sin):
    lane = lax.broadcasted_iota(jnp.int32, z.shape, 1)
    zs = z * sin
    half = HEAD_DIM // 2
    return z * cos + jnp.where(lane % HEAD_DIM < half, pltpu.roll(zs, PAIR - half, 1), pltpu.roll(zs, half, 1))


def _head_avg():
    r = lax.broadcasted_iota(jnp.int32, (PAIR, PAIR), 0) // HEAD_DIM
    c = lax.broadcasted_iota(jnp.int32, (PAIR, PAIR), 1) // HEAD_DIM
    return jnp.where(r == c, 1.0 / HEAD_DIM, 0.0).astype(F32)


def _ret_tables(tq):
    lg = jnp.log1p(-(2.0 ** (-5.0 - jnp.arange(RET_HEADS, dtype=F32))))
    scale = 1.0 / math.sqrt(HEAD_DIM)
    pos = jnp.arange(tq)
    n = pos.astype(F32)
    dist = jnp.abs(n[:, None] - n[None, :])
    ok = (pos[None, :] // CHUNK) <= (pos[:, None] // CHUNK)
    w = jnp.where(ok[None], scale * jnp.exp(lg[:, None, None] * dist[None]), 0.0)
    lgl = jnp.repeat(lg, HEAD_DIM)
    dq_tab = scale * jnp.exp(lgl[None, :] * (n[:, None] + 1.0))
    dk_tab = jnp.exp(lgl[None, :] * (tq - 1.0 - n[:, None]))
    blk = jnp.arange(PAIR) // HEAD_DIM
    bd = (blk[:, None] == blk[None, :]).astype(F32)
    gbd = bd[None] * jnp.exp(lgl.reshape(RET_HEADS // 2, PAIR)[:, :, None] * tq)
    return dict(w=w, wt=w.transpose(0, 2, 1), dq=dq_tab, dk=dk_tab, gbd=gbd, bd=bd)


def _ret_specs(tq, nq, rev, layer):
    blk = (lambda i: nq - 1 - i) if rev else (lambda i: i)
    col = lambda o: pl.BlockSpec((tq, PAIR), lambda p, i: (blk(i), o // PAIR + p))
    return dict(
        rq=col(O_RQ), rk=col(O_RK), rv=col(O_RV), nat=col(0),
        w=pl.BlockSpec((2, tq, tq), lambda p, i: (p, 0, 0)), tab=pl.BlockSpec((tq, PAIR), lambda p, i: (0, p)),
        gbd=pl.BlockSpec((None, PAIR, PAIR), lambda p, i: (p, 0, 0)), bd=pl.BlockSpec((PAIR, PAIR), lambda p, i: (0, 0)),
        gn=pl.BlockSpec((None, 1, PAIR), lambda p, i: (layer, 0, p)), dgn=pl.BlockSpec((1, PAIR), lambda p, i: (0, p)),
        st=pl.BlockSpec((None, None, PAIR, PAIR), lambda p, i: (p, blk(i), 0, 0)))


def _ret_fwd(proj, cos_t, sin_t, tabs, gn):
    L = proj.shape[0]
    tq = tabs["w"].shape[1]
    nq = L // tq

    def body(rq_ref, rk_ref, rv_ref, cos_ref, sin_ref, w_ref, dqt_ref, dkt_ref, gbd_ref, bd_ref, gn_ref,
             o_ref, y_ref, st_ref, s_sc):
        @pl.when(pl.program_id(1) == 0)
        def _():
            s_sc[...] = jnp.zeros((PAIR, PAIR), F32)

        state = s_sc[...]
        st_ref[...] = state
        cos, sin = cos_ref[...], sin_ref[...]
        q2, k2, v2 = _rot(rq_ref[...], cos, sin), _rot(rk_ref[...], cos, sin), rv_ref[...]
        owns = [_own((tq, PAIR), h) for h in range(2)]
        scores = [_dot(jnp.where(owns[h], q2, 0.0), k2, _NT) for h in range(2)]
        o = _dot(q2 * dqt_ref[...], state)
        for h in range(2):
            o = o + _dot(scores[h] * w_ref[h], jnp.where(owns[h], v2, 0.0))
        s_sc[...] = gbd_ref[...] * state + bd_ref[...] * _dot(k2 * dkt_ref[...], v2, _TN)
        o_ref[...] = o
        avg = _head_avg()
        oc = o - _hi_dot(o, avg)
        y_ref[...] = oc * lax.rsqrt(_hi_dot(oc * oc, avg) + EPS) * gn_ref[...]

    gn, layer = gn
    sp = _ret_specs(tq, nq, False, layer)
    nat = jax.ShapeDtypeStruct((L, D_RET), F32)
    return _pallas(
        body, out_shape=(nat, nat, jax.ShapeDtypeStruct((RET_HEADS // 2, nq, PAIR, PAIR), F32)), grid=(RET_HEADS // 2, nq),
        in_specs=[sp["rq"], sp["rk"], sp["rv"], sp["nat"], sp["nat"], sp["w"], sp["tab"], sp["tab"], sp["gbd"], sp["bd"],
                  sp["gn"]],
        out_specs=(sp["nat"], sp["nat"], sp["st"]), scratch_shapes=[pltpu.VMEM((PAIR, PAIR), F32)],
        name="ret_fwd")(proj, proj, proj, cos_t, sin_t, tabs["w"], tabs["dq"], tabs["dk"], tabs["gbd"], tabs["bd"], gn)


def _ret_bwd(proj, cos_t, sin_t, tabs, gn, o_pre, dy, states):
    L = proj.shape[0]
    tq = tabs["w"].shape[1]
    nq = L // tq

    def body(rq_ref, rk_ref, rv_ref, cos_ref, sin_ref, w_ref, wt_ref, dqt_ref, dkt_ref, gbd_ref, bd_ref, gn_ref,
             o_ref, dy_ref, st_ref, drq_ref, drk_ref, drv_ref, dgn_ref, g_sc):
        first = pl.program_id(1) == 0

        @pl.when(first)
        def _():
            g_sc[...] = jnp.zeros((PAIR, PAIR), F32)

        cos, sin = cos_ref[...], sin_ref[...]
        q2, k2, v2 = _rot(rq_ref[...], cos, sin), _rot(rk_ref[...], cos, sin), rv_ref[...]
        avg = _head_avg()
        ov, dyv = o_ref[...], dy_ref[...]
        oc = ov - _hi_dot(ov, avg)
        r = lax.rsqrt(_hi_dot(oc * oc, avg) + EPS)
        oh = oc * r
        dgn = jnp.sum(dyv * oh, axis=0, keepdims=True)
        doh = dyv * gn_ref[...]
        do = r * (doh - _hi_dot(doh, avg) - oh * _hi_dot(doh * oh, avg))
        state, g = st_ref[...], g_sc[...]
        dqt, dkt = dqt_ref[...], dkt_ref[...]
        dq = _dot(do, state, _NT) * dqt
        dk = _dot(v2, g, _NT) * dkt
        dv = _dot(k2 * dkt, g)
        g_sc[...] = gbd_ref[...] * g + bd_ref[...] * _dot(q2 * dqt, do, _TN)
        owns = [_own((tq, PAIR), h) for h in range(2)]
        qms = [jnp.where(owns[h], q2, 0.0) for h in range(2)]
        doms = [jnp.where(owns[h], do, 0.0) for h in range(2)]
        ats = [_dot(k2, qms[h], _NT) for h in range(2)]
        das = [_dot(doms[h], v2, _NT) for h in range(2)]
        for h in range(2):
            dv = dv + _dot(ats[h] * wt_ref[h], doms[h])
            daw = (das[h] * w_ref[h]).astype(_MXU)
            dq = dq + _dot(daw, jnp.where(owns[h], k2, 0.0))
            dk = dk + _dot(daw.T, qms[h])
        drq_ref[...] = _rot(dq, cos, -sin)
        drk_ref[...] = _rot(dk, cos, -sin)
        drv_ref[...] = dv

        @pl.when(first)
        def _():
            dgn_ref[...] = dgn

        @pl.when(jnp.logical_not(first))
        def _():
            dgn_ref[...] += dgn

    gn, layer = gn
    sp = _ret_specs(tq, nq, True, layer)
    nat = jax.ShapeDtypeStruct((L, D_RET), F32)
    return _pallas(
        body, out_shape=(nat, nat, nat, jax.ShapeDtypeStruct((1, D_RET), F32)), grid=(RET_HEADS // 2, nq),
        in_specs=[sp["rq"], sp["rk"], sp["rv"], sp["nat"], sp["nat"], sp["w"], sp["w"], sp["tab"], sp["tab"], sp["gbd"],
                  sp["bd"], sp["gn"], sp["nat"], sp["nat"], sp["st"]],
        out_specs=(sp["nat"], sp["nat"], sp["nat"], sp["dgn"]), scratch_shapes=[pltpu.VMEM((PAIR, PAIR), F32)],
        name="ret_bwd")(proj, proj, proj, cos_t, sin_t, tabs["w"], tabs["wt"], tabs["dq"], tabs["dk"], tabs["gbd"],
                        tabs["bd"], gn, o_pre, dy, states)


def _gate_out(yf, ys, yr, proj, x, w):
    L = x.shape[0]

    def body(yf_ref, ys_ref, yr_ref, g_ref, x_ref, w_ref, xn_ref):
        cat = jnp.concatenate([yf_ref[...], ys_ref[...], yr_ref[...]], axis=-1)
        xn_ref[...] = x_ref[...] + _dot(cat * jax.nn.silu(g_ref[...]), w_ref[...])

    tm = min(TMB, L)
    full = _rows(tm, D_MODEL)
    return _pallas(body, out_shape=jax.ShapeDtypeStruct((L, D_MODEL), F32), grid=(L // tm,),
                   in_specs=[_rows(tm, D_FOX), _rows(tm, D_S5), _rows(tm, D_RET), _rows(tm, D_MODEL, O_GATE // D_MODEL),
                             full, _whole((D_MODEL, D_MODEL))],
                   out_specs=full, name="gate_out")(yf, ys, yr, proj, x, w)


def _gate_out_bwd(dxn, w, yf, ys, yr, proj):
    L = dxn.shape[0]

    def body(dx_ref, w_ref, yf_ref, ys_ref, yr_ref, g_ref, dyf_ref, dys_ref, dyr_ref, dg_ref, dw_ref):
        dxv = dx_ref[...].astype(_MXU)
        dy = _dot(dxv, w_ref[...], _NT)
        g = g_ref[...]
        sg = jax.nn.sigmoid(g)
        silu = g * sg
        dcat = dy * silu
        dyf_ref[...] = dcat[:, :D_FOX]
        dys_ref[...] = dcat[:, D_FOX:D_FOX + D_S5]
        dyr_ref[...] = dcat[:, D_FOX + D_S5:]
        cat = jnp.concatenate([yf_ref[...], ys_ref[...], yr_ref[...]], axis=-1)
        dg_ref[...] = dy * cat * (sg * (1.0 + g * (1.0 - sg)))
        dw = _dot(cat * silu, dxv, _TN)

        @pl.when(pl.program_id(0) == 0)
        def _():
            dw_ref[...] = dw

        @pl.when(pl.program_id(0) != 0)
        def _():
            dw_ref[...] += dw

    tm = min(TMB, L)
    full = _rows(tm, D_MODEL)
    f, s, r = _rows(tm, D_FOX), _rows(tm, D_S5), _rows(tm, D_RET)
    sq = _whole((D_MODEL, D_MODEL))
    return _pallas(body, out_shape=(jax.ShapeDtypeStruct((L, D_FOX), F32), jax.ShapeDtypeStruct((L, D_S5), F32),
                                    jax.ShapeDtypeStruct((L, D_RET), F32), jax.ShapeDtypeStruct((L, D_MODEL), F32),
                                    jax.ShapeDtypeStruct((D_MODEL, D_MODEL), F32)),
                   grid=(L // tm,), in_specs=[full, sq, f, s, r, _rows(tm, D_MODEL, O_GATE // D_MODEL)],
                   out_specs=(f, s, r, full, sq), name="gate_out_bwd",
                   compiler_params=pltpu.CompilerParams(vmem_limit_bytes=VMEM_BIG))(dxn, w, yf, ys, yr, proj)


def _final_loss(x, g, tgt):
    L = x.shape[0]

    def body(x_ref, g_ref, t_ref, loss_ref, dx_ref, dg_ref):
        xv, gv = x_ref[...], g_ref[...]
        r = lax.rsqrt(jnp.mean(xv * xv, axis=-1, keepdims=True) + EPS)
        err = xv * r * gv - t_ref[...]
        part = 0.5 * jnp.sum(jnp.mean(err * err, axis=-1, keepdims=True), axis=0, keepdims=True)
        dx, dg = _rms_bwd(xv, gv, err * (1.0 / D_MODEL))
        dx_ref[...] = dx

        @pl.when(pl.program_id(0) == 0)
        def _():
            loss_ref[...] = part
            dg_ref[...] = dg

        @pl.when(pl.program_id(0) != 0)
        def _():
            loss_ref[...] += part
            dg_ref[...] += dg

    full = _rows(TM, D_MODEL)
    return _pallas(body, out_shape=(jax.ShapeDtypeStruct((1, 1), F32), jax.ShapeDtypeStruct((L, D_MODEL), F32),
                                    jax.ShapeDtypeStruct((1, D_MODEL), F32)),
                   grid=(L // TM,), in_specs=[full, _whole((1, D_MODEL)), full],
                   out_specs=(_whole((1, 1)), full, _whole((1, D_MODEL))), name="final_loss")(x, g, tgt)


def _block_diag(blocks):
    n, g, r, c = blocks.shape
    eye = jnp.eye(g, dtype=blocks.dtype)
    return (blocks[:, :, :, None, :] * eye[None, :, None, :, None]).reshape(n, g * r, g * c)


def _diag_blocks(m, g):
    n, r, c = m.shape[0], m.shape[1] // g, m.shape[2] // g
    eye = jnp.eye(g, dtype=m.dtype)
    return jnp.sum(m.reshape(n, g, r, g, c) * eye[None, :, None, :, None], axis=3)


def _rope_tables(L):
    half = HEAD_DIM // 2
    freqs = ROPE_BASE ** (-jnp.arange(half, dtype=F32) / half)
    ang = jnp.arange(L, dtype=F32)[:, None] * freqs[None, :]
    cos, sin = jnp.cos(ang), jnp.sin(ang)
    cos_t = jnp.tile(jnp.concatenate([cos, cos], axis=-1), (1, RET_HEADS))
    sin_t = jnp.tile(jnp.concatenate([sin, -sin], axis=-1), (1, RET_HEADS))
    return cos_t, sin_t


def _s5_disc_args(small):
    g, s, ch = S5_GROUPS, S5_STATE, S5_GROUP_CH
    return (small["s5_a_re"], small["s5_a_im"], small["s5_log_dt"][:, :, None],
            small["s5_b_re"].reshape(DEPTH, g, s * ch), small["s5_b_im"].reshape(DEPTH, g, s * ch))


def _s5_mats(small):
    g, s, ch = S5_GROUPS, S5_STATE, S5_GROUP_CH
    lr, li, bbr, bbi = _s5_disc(*_s5_disc_args(small))
    lam = jnp.concatenate([lr.reshape(DEPTH, 8, 128), li.reshape(DEPTH, 8, 128)], axis=1)
    wb = jnp.concatenate([_block_diag(b.reshape(DEPTH, g, s, ch).transpose(0, 1, 3, 2)) for b in (bbr, bbi)], axis=2)
    wc = jnp.concatenate([_block_diag(c.transpose(0, 1, 3, 2)) for c in (small["s5_c_re"], -small["s5_c_im"])], axis=1)
    return lam, wb.astype(_MXU), wc.astype(_MXU)


def _s5_param_grads(small, dwb, dwc, dlam):
    g, s, ch = S5_GROUPS, S5_STATE, S5_GROUP_CH
    dc = [_diag_blocks(m, g).transpose(0, 1, 3, 2).reshape(DEPTH, g, ch * s) for m in (dwc[:, :S5_CH], dwc[:, S5_CH:])]
    dbb = [_diag_blocks(m, g).transpose(0, 1, 3, 2).reshape(DEPTH, g, s * ch) for m in (dwb[:, :, :S5_CH], dwb[:, :, S5_CH:])]
    dar, dai, dldt, dbr, dbi = _s5_disc_bwd(*_s5_disc_args(small), dlam[:, :8].reshape(DEPTH, g, s),
                                            dlam[:, 8:].reshape(DEPTH, g, s), dbb[0], dbb[1])
    return dict(s5_a_re=dar, s5_a_im=dai, s5_log_dt=dldt.reshape(DEPTH, g), s5_b_re=dbr, s5_b_im=dbi, s5_c_re=dc[0],
                s5_c_im=-dc[1])


_DENSE = ("s5_b_re", "s5_b_im", "s5_c_re", "s5_c_im")


def _layer_fwd(x, p, rope, ride=None, late=False):
    L = x.shape[0]
    cos_t, sin_t, ret_tabs = rope
    s = {"x": x}
    proj, h = _norm_inproj(x, p["norm_w"], p["w_in"])
    s["proj"], s["h"] = proj, h
    qa, ka, kat, vt = _fox_prep(proj, _fox_cumsum(proj, p["b_f"]))
    yf, lse, landed = _fox_fwd(qa, ka, vt, ride)
    s.update(qa=qa, ka=ka, kat=kat, lse=lse, yf=yf)
    if late:
        p["w_glu"], p["w_out"] = _gathered_rows(landed[-2]), _gathered_rows(landed[-1])
        landed = landed[:-2]
    xs, ypre, ys = _s5_fwd(proj, p["wb"], p["wc"], p["lam"], p["d"], p["w_glu"])
    s.update(xs=xs, ypre=ypre, ys=ys)
    o_pre, yr, states = _ret_fwd(proj, cos_t, sin_t, ret_tabs, p["gn_w"])
    s.update(o_pre=o_pre, yr=yr, states=states)
    return _gate_out(yf, ys, yr, proj, x, p["w_out"]), s, landed


def _layer_bwd(dxn, s, p, rope, ride=None, early=False):
    L = dxn.shape[0]
    cos_t, sin_t, ret_tabs = rope
    g = {}
    proj = s["proj"]
    dyf, dys, dyr, dgate, g["w_out"] = _gate_out_bwd(dxn, p["w_out"], s["yf"], s["ys"], s["yr"], proj)
    drq, drk, drv, dgn = _ret_bwd(proj, cos_t, sin_t, ret_tabs, p["gn_w"], s["o_pre"], dyr, s["states"])
    g["ret_gn_w"] = dgn.reshape(D_RET)
    dsu, g["wb"], g["wc"], g["lam"], dd, g["s5_w_glu"] = _s5_bwd(proj, s["ypre"], dys, s["xs"], p["wb"], p["wc"], p["lam"],
                                                                 p["d"], p["w_glu"])
    g["s5_d"] = dd.reshape(D_S5)
    if early:
        ride = (ride[0] + _row_slots(g), ride[1] + [True, True])
    dqt, dkraw, dv, landed = _fox_bwd(s["qa"], s["ka"], s["kat"], proj, dyf, s["yf"], s["lse"], ride)
    dq, dk, dfl, dbf = _fox_post_bwd(dqt, dkraw, proj, p["b_f"])
    g["fox_b_f"] = dbf[0, :FOX_HEADS]
    pieces = [dgate, dq, dk, dv, dsu, drq, drk, drv, dfl]
    dx, dnw, g["w_in"] = _inproj_bwd(pieces, p["w_in"], s["x"], p["norm_w"], dxn, s["h"])
    g["norm_w"] = dnw.reshape(D_MODEL)
    return dx, g, landed


def _stacked_params(small):
    lam, wb, wc = _s5_mats(small)
    row = lambda a: a[:, None, :]
    return dict(norm_w=row(small["norm_w"]), b_f=row(jnp.pad(small["fox_b_f"], ((0, 0), (0, PAIR - FOX_HEADS)))),
                lam=lam, wb=wb, wc=wc, d=row(small["s5_d"]), gn_w=row(small["ret_gn_w"]))


def _layer_params(l, w_in_p, w_glu, w_out, stacked):
    return dict({k: (a, l) for k, a in stacked.items()}, w_in=w_in_p, w_glu=w_glu, w_out=w_out)


_SHARDED = ("w_in", "s5_w_glu", "w_out")
_WIRE = jnp.bfloat16


_RUNS = ((2568, 3592, O_GATE), (0, 1536, O_FQ), (1544, 2568, O_SU), (1536, 1544, O_FL))


def _shard_pieces():
    out = []
    for a, b, pad in _RUNS:
        while a < b:
            j = a // W_SHARD
            e = min(b, (j + 1) * W_SHARD)
            out.append((j, a - j * W_SHARD, e - j * W_SHARD, pad))
            pad, a = pad + e - a, e
    return out


def _gathered_w_in(g_in):
    cols = [g_in[j, :, a:e] for j, a, e, _ in _shard_pieces()]
    cols.append(jnp.zeros((D_MODEL, D_INP - O_FL - FOX_HEADS), g_in.dtype))
    return jnp.concatenate(cols, axis=1)


def _gathered_rows(g):
    return g.reshape(-1, g.shape[-1])


def _w_in_slots(g):
    w_in = g["w_in"].astype(_WIRE)
    slots = []
    for j in range(N_DEV):
        mine = sorted((a, e, pad) for jj, a, e, pad in _shard_pieces() if jj == j)
        slots.append(jnp.concatenate([w_in[:, pad:pad + e - a] for a, e, pad in mine], axis=1))
    return jnp.stack(slots)


def _row_slots(g):
    return [g["s5_w_glu"].reshape(N_DEV, D_S5 // N_DEV, D_S5).astype(_WIRE),
            g["w_out"].reshape(N_DEV, D_MODEL // N_DEV, D_MODEL).astype(_WIRE)]


def _step_grads(x, tgt, small, full=None, shards=None):
    L = x.shape[0]
    rope = _rope_tables(L) + (_ret_tables(min(TQ, L)),)
    stacked = _stacked_params(small)
    if shards is not None:
        nxt = (_gathered_w_in(_exchange([shards[0][0]], [False], "gather_layer0")[0]), None, None)
    saved, params = [], []
    for l in range(DEPTH):
        weights = nxt if shards is not None else tuple(f[l] for f in full)
        ride = None
        if shards is not None:
            arrs = [s[l + 1] for s in shards] if l + 1 < DEPTH else []
            arrs += [shards[1][0], shards[2][0]] if l == 0 else []
            ride = (arrs, [False] * len(arrs)) if arrs else None
        params.append(_layer_params(l, *weights, stacked))
        x, s, landed = _layer_fwd(x, params[l], rope, ride, late=shards is not None and l == 0)
        if landed:
            nxt = (_gathered_w_in(landed[0]), _gathered_rows(landed[1]), _gathered_rows(landed[2]))
        saved.append(s)
    loss, dx, dfw = _final_loss(x, small["final_norm_w"][None], tgt)
    grads, partials, waiting = [None] * DEPTH, [None] * DEPTH, None
    for l in reversed(range(DEPTH)):
        ride = (waiting, [True] * len(waiting)) if waiting is not None else None
        dx, grads[l], landed = _layer_bwd(dx, saved[l], params[l], rope, ride, early=ride is not None and l == 0)
        if waiting is not None:
            partials[l + 1] = landed[:3]
        if shards is not None:
            waiting = [_w_in_slots(grads[l])] + _row_slots(grads[l])
    stack = lambda n: jnp.stack([g[n] for g in grads])
    small_g = {n: stack(n) for n in ("norm_w", "fox_b_f", "s5_d", "ret_gn_w")}
    small_g.update(_s5_param_grads(small, stack("wb"), stack("wc"), stack("lam")), final_norm_w=dfw)
    if shards is None:
        return loss, dx, grads, small_g
    last = _exchange([waiting[0]] + [small_g[n].astype(_WIRE) for n in _SMALL], [True] + [False] * len(_SMALL),
                     "exchange_layer0")
    partials[0] = [last[0]] + landed[3:]
    return loss, dx, grads, small_g, partials, dict(zip(_SMALL, last[1:]))


_MESH = pl.DeviceIdType.MESH
_ANY = pl.BlockSpec(memory_space=pl.ANY)


def _me_and_peers():
    x, y, c = lax.axis_index("x"), lax.axis_index("y"), lax.axis_index("c")
    flip = lambda a, bit: (1 - a) if bit else a
    peers = []
    for r in range(1, N_DEV):
        px, py, pc = flip(x, (r >> 2) & 1), flip(y, (r >> 1) & 1), flip(c, r & 1)
        peers.append(((px, py, pc), 4 * px + 2 * py + pc))
    return 4 * x + 2 * y + c, peers


def _exchange_copies(srcs, dsts, sems, scatter):
    send_sems, recv_sems, local_sems = sems
    me, peers = _me_and_peers()
    pick = lambda t, to: srcs[t].at[to] if scatter[t] else srcs[t]
    own = [pltpu.make_async_copy(pick(t, me), dsts[t].at[me], local_sems.at[t]) for t in range(len(srcs))]
    sends, waits = [], []
    for r, (dev, idx) in enumerate(peers):
        for t in range(len(srcs)):
            for land, out in ((me, sends), (idx, waits)):
                out.append(pltpu.make_async_remote_copy(pick(t, idx), dsts[t].at[land], send_sems.at[t, r], recv_sems.at[t, r],
                                                        device_id=dev, device_id_type=_MESH))
    return own, sends, waits


def _exchange_start(srcs, dsts, sems, scatter):
    own, sends, _ = _exchange_copies(srcs, dsts, sems, scatter)
    for cp in own + sends:
        cp.start()


def _exchange_wait(srcs, dsts, sems, scatter):
    own, _, waits = _exchange_copies(srcs, dsts, sems, scatter)
    for cp in waits + own:
        cp.wait()


def _exchange_shapes(arrs, scatter):
    outs = [jax.ShapeDtypeStruct(a.shape if sc else (N_DEV,) + a.shape, a.dtype) for a, sc in zip(arrs, scatter)]
    n = len(arrs)
    sems = [pltpu.SemaphoreType.DMA((n, N_DEV - 1)), pltpu.SemaphoreType.DMA((n, N_DEV - 1)), pltpu.SemaphoreType.DMA((n,))]
    return outs, sems


def _exchange(arrs, scatter, name):
    n = len(arrs)

    def body(*refs):
        _exchange_start(refs[:n], refs[n:2 * n], refs[2 * n:], scatter)
        _exchange_wait(refs[:n], refs[n:2 * n], refs[2 * n:], scatter)

    outs, sems = _exchange_shapes(arrs, scatter)
    return _pallas(body, out_shape=tuple(outs), in_specs=[_ANY] * n, out_specs=tuple([_ANY] * n), scratch_shapes=sems,
                   name=name)(*arrs)


def _riding(body, n_in, n_out, ride, is_first, is_last):
    if ride is None:
        return body, [], [], [], []
    arrs, scatter = ride
    n = len(arrs)
    outs, sems = _exchange_shapes(arrs, scatter)

    def wrapped(*refs):
        ins, srcs = refs[:n_in], refs[n_in:n_in + n]
        own_outs, dsts = refs[n_in + n:n_in + n + n_out], refs[n_in + n + n_out:n_in + 2 * n + n_out]
        scratch, ex_sems = refs[n_in + 2 * n + n_out:-3], refs[-3:]

        @pl.when(is_first())
        def _():
            _exchange_start(srcs, dsts, ex_sems, scatter)

        body(*ins, *own_outs, *scratch)

        @pl.when(is_last())
        def _():
            _exchange_wait(srcs, dsts, ex_sems, scatter)

    return wrapped, list(arrs), [_ANY] * n, outs, sems


def _adamw_body(p_ref, w_ref, m_ref, v_ref, g_ref, d_ref, nm_ref, nv_ref):
    g = p_ref[0].astype(F32)
    for i in range(1, N_DEV):
        g = g + p_ref[i].astype(F32)
    nm = ADAM_B1 * m_ref[...] + (1.0 - ADAM_B1) * g
    nv = ADAM_B2 * v_ref[...] + (1.0 - ADAM_B2) * jnp.square(g)
    m_hat = nm / (1.0 - ADAM_B1 ** ADAM_STEP)
    v_hat = nv / (1.0 - ADAM_B2 ** ADAM_STEP)
    g_ref[...] = g
    d_ref[...] = -ADAM_LR * (m_hat / (jnp.sqrt(v_hat) + ADAM_EPS) + ADAM_WD * w_ref[...])
    nm_ref[...] = nm
    nv_ref[...] = nv


def _adamw(parts, w, m, v, name):
    n, nb, rows, cols = parts.shape
    tm = next(t for t in (256, 128, 64, 32, 16) if rows % t == 0)

    def body(*refs):
        _adamw_body(*refs)

    row = pl.BlockSpec((None, tm, cols), lambda b, i: (b, i, 0))
    return _pallas(body, out_shape=(jax.ShapeDtypeStruct((nb, rows, cols), F32),) * 4, grid=(nb, rows // tm),
                   in_specs=[pl.BlockSpec((n, None, tm, cols), lambda b, i: (0, b, i, 0)), row, row, row],
                   out_specs=(row,) * 4, name=name)(parts, w, m, v)


def _adamw_whole(parts, w, m, v, name):
    def body(*refs):
        _adamw_body(*refs)

    if w.ndim == 2:
        grid = (1,)
        slab = pl.BlockSpec(w.shape, lambda b: (0, 0))
        part = pl.BlockSpec(parts.shape, lambda b: (0, 0, 0))
    else:
        grid, rest = (w.shape[0],), w.shape[1:]
        zeros = (0,) * len(rest)
        slab = pl.BlockSpec((None,) + rest, lambda b: (b,) + zeros)
        part = pl.BlockSpec((N_DEV, None) + rest, lambda b: (0, b) + zeros)
    return _pallas(body, out_shape=(jax.ShapeDtypeStruct(w.shape, F32),) * 4, grid=grid,
                   in_specs=[part, slab, slab, slab], out_specs=(slab,) * 4, name=name)(parts, w, m, v)


_WEIGHTS = ("norm_w", "w_in", "fox_b_f", "s5_a_re", "s5_a_im", "s5_b_re", "s5_b_im", "s5_c_re", "s5_c_im", "s5_d",
            "s5_log_dt", "s5_w_glu", "ret_gn_w", "w_out", "final_norm_w")
_SMALL = tuple(n for n in _WEIGHTS if n not in _SHARDED)


def kernel(x, norm_w, w_in, fox_b_f, s5_a_re, s5_a_im, s5_b_re, s5_b_im, s5_c_re, s5_c_im, s5_d, s5_log_dt, s5_w_glu, ret_gn_w, w_out, final_norm_w, loss_target, m_norm_w, m_w_in, m_fox_b_f, m_s5_a_re, m_s5_a_im, m_s5_b_re, m_s5_b_im, m_s5_c_re, m_s5_c_im, m_s5_d, m_s5_log_dt, m_s5_w_glu, m_ret_gn_w, m_w_out, m_final_norm_w, v_norm_w, v_w_in, v_fox_b_f, v_s5_a_re, v_s5_a_im, v_s5_b_re, v_s5_b_im, v_s5_c_re, v_s5_c_im, v_s5_d, v_s5_log_dt, v_s5_w_glu, v_ret_gn_w, v_w_out, v_final_norm_w):
    w = dict(norm_w=norm_w, w_in=w_in, fox_b_f=fox_b_f, s5_a_re=s5_a_re, s5_a_im=s5_a_im, s5_b_re=s5_b_re, s5_b_im=s5_b_im,
             s5_c_re=s5_c_re, s5_c_im=s5_c_im, s5_d=s5_d, s5_log_dt=s5_log_dt, s5_w_glu=s5_w_glu, ret_gn_w=ret_gn_w,
             w_out=w_out, final_norm_w=final_norm_w)
    m = dict(norm_w=m_norm_w, w_in=m_w_in, fox_b_f=m_fox_b_f, s5_a_re=m_s5_a_re, s5_a_im=m_s5_a_im, s5_b_re=m_s5_b_re,
             s5_b_im=m_s5_b_im, s5_c_re=m_s5_c_re, s5_c_im=m_s5_c_im, s5_d=m_s5_d, s5_log_dt=m_s5_log_dt,
             s5_w_glu=m_s5_w_glu, ret_gn_w=m_ret_gn_w, w_out=m_w_out, final_norm_w=m_final_norm_w)
    v = dict(norm_w=v_norm_w, w_in=v_w_in, fox_b_f=v_fox_b_f, s5_a_re=v_s5_a_re, s5_a_im=v_s5_a_im, s5_b_re=v_s5_b_re,
             s5_b_im=v_s5_b_im, s5_c_re=v_s5_c_re, s5_c_im=v_s5_c_im, s5_d=v_s5_d, s5_log_dt=v_s5_log_dt,
             s5_w_glu=v_s5_w_glu, ret_gn_w=v_ret_gn_w, w_out=v_w_out, final_norm_w=v_final_norm_w)

    small = {n: w[n] for n in _SMALL}
    loss, dx, _, _, partials, r_small = _step_grads(x[0], loss_target[0], small, shards=[w[n].astype(_MXU) for n in _SHARDED])

    res = {}
    for t, n in enumerate(_SHARDED):
        res[n] = _adamw(jnp.stack([partials[l][t] for l in range(DEPTH)], axis=1), w[n], m[n], v[n], "adamw_" + n)
    for n in _SMALL:
        shape = w[n].shape
        view = (1,) + shape if len(shape) == 1 else shape[:2] + (-1,) if n in _DENSE else shape
        outs = _adamw_whole(r_small[n], *[d[n].reshape(view) for d in (w, m, v)], "adamw_" + n)
        res[n] = [o.reshape(shape) for o in outs]

    loss = lax.psum(loss[0, 0], ("x", "y", "c"))
    return (loss, dx[None], *[res[n][0] for n in _WEIGHTS], *[res[n][1] for n in _WEIGHTS],
            *[res[n][2] for n in _WEIGHTS], *[res[n][3] for n in _WEIGHTS])
```

```python
import math

import jax
import jax.numpy as jnp
from jax import lax
from jax.experimental import pallas as pl
from jax.experimental.pallas import tpu as pltpu

F32 = jnp.float32
_MXU = jnp.bfloat16
_HI = lax.Precision.HIGHEST

N_DEV = 8
DEPTH = 4
D_MODEL = 1024
HEAD_DIM = 64
D_FOX = 512
FOX_HEADS = 8
D_S5 = 256
S5_GROUPS = 16
S5_GROUP_CH = 16
S5_STATE = 64
S5_CH = S5_GROUPS * S5_STATE
D_RET = 256
RET_HEADS = 4
CHUNK = 64
ROPE_BASE = 10000.0
EPS = 1e-6
D_IN = 3592
D_INP = 3712
W_SHARD = D_IN // N_DEV
O_GATE, O_FQ, O_FK, O_FV, O_SU, O_RQ, O_RK, O_RV, O_FL = 0, 1024, 1536, 2048, 2560, 2816, 3072, 3328, 3584

ADAM_LR, ADAM_B1, ADAM_B2, ADAM_EPS, ADAM_WD, ADAM_STEP = 0.001, 0.9, 0.999, 1e-08, 0.01, 10

TM = 256
TMB = 512
TQ = 512
TS = 512
NEG = -1e30
VMEM_BIG = 56 * 1024 * 1024


def _pallas(body, **kw):
    return pl.pallas_call(body, **kw)


def _whole(shape):
    n = len(shape)
    return pl.BlockSpec(shape, lambda *_: (0,) * n)


def _rows(tm, width, col=0):
    return pl.BlockSpec((tm, width), lambda i: (i, col))


def _of_layer(param):
    a, l = param
    return a, pl.BlockSpec((None,) + a.shape[1:], lambda *_: (l,) + (0,) * (a.ndim - 1))


def _dot(a, b, dims=(((1,), (0,)), ((), ()))):
    return lax.dot_general(a.astype(_MXU), b.astype(_MXU), dims, preferred_element_type=F32)


_NT = (((1,), (1,)), ((), ()))
_TN = (((0,), (0,)), ((), ()))


def _norm_inproj(x, g, w):
    L = x.shape[0]

    def body(x_ref, g_ref, w_ref, p_ref, h_ref):
        xv = x_ref[...]
        r = lax.rsqrt(jnp.mean(xv * xv, axis=-1, keepdims=True) + EPS)
        h = (xv * r * g_ref[...]).astype(_MXU)
        h_ref[...] = h
        p_ref[...] = _dot(h, w_ref[...])

    tm = min(TMB, L)
    g, g_spec = _of_layer(g)
    return _pallas(body, out_shape=(jax.ShapeDtypeStruct((L, D_INP), F32), jax.ShapeDtypeStruct((L, D_MODEL), _MXU)),
                   grid=(L // tm,),
                   in_specs=[_rows(tm, D_MODEL), g_spec,
                             pl.BlockSpec((D_MODEL, D_INP), lambda i: (0, 0), pipeline_mode=pl.Buffered(1))],
                   out_specs=(_rows(tm, D_INP), _rows(tm, D_MODEL)), name="norm_inproj",
                   compiler_params=pltpu.CompilerParams(vmem_limit_bytes=VMEM_BIG))(x, g, w)


def _rms_bwd(xv, g, dh):
    r = lax.rsqrt(jnp.mean(xv * xv, axis=-1, keepdims=True) + EPS)
    xh = xv * r
    dg = jnp.sum(dh * xh, axis=0, keepdims=True)
    dxh = dh * g
    dx = r * (dxh - xh * jnp.mean(dxh * xh, axis=-1, keepdims=True))
    return dx, dg


def _inproj_bwd(pieces, w, x, g, dres, h):
    L = x.shape[0]
    n = len(pieces)
    nb = L // TM

    def body(*refs):
        w_ref, x_ref, g_ref, dr_ref, h_ref, dx_ref, dg_ref, dw_ref, acc_sc = refs[n:]
        step = pl.program_id(0)

        @pl.when(step == 0)
        def _():
            acc_sc[...] = jnp.zeros((D_MODEL, D_INP), F32)
            dg_ref[...] = jnp.zeros((1, D_MODEL), F32)

        hv = h_ref[...]
        dh = jnp.zeros((TM, D_MODEL), F32)
        off = 0
        for r in refs[:n]:
            cols = slice(off, off + r.shape[1])
            off += r.shape[1]
            piece = r[...].astype(_MXU)
            dh = dh + _dot(piece, w_ref[:, cols], _NT)
            acc_sc[:, cols] += _dot(hv, piece, _TN)
        dx, dg = _rms_bwd(x_ref[...], g_ref[...], dh)
        dx_ref[...] = dx + dr_ref[...]
        dg_ref[...] += dg

        @pl.when(step == nb - 1)
        def _():
            dw_ref[...] = acc_sc[...].astype(_WIRE)

    resident = pl.BlockSpec((D_MODEL, D_INP), lambda i: (0, 0), pipeline_mode=pl.Buffered(1))
    g, g_spec = _of_layer(g)
    return _pallas(body, out_shape=(jax.ShapeDtypeStruct((L, D_MODEL), F32), jax.ShapeDtypeStruct((1, D_MODEL), F32),
                                    jax.ShapeDtypeStruct((D_MODEL, D_INP), _WIRE)),
                   grid=(nb,),
                   in_specs=[_rows(TM, p.shape[1]) for p in pieces]
                   + [resident, _rows(TM, D_MODEL), g_spec, _rows(TM, D_MODEL), _rows(TM, D_MODEL)],
                   out_specs=(_rows(TM, D_MODEL), _whole((1, D_MODEL)), resident),
                   scratch_shapes=[pltpu.VMEM((D_MODEL, D_INP), F32)], name="inproj_bwd",
                   compiler_params=pltpu.CompilerParams(vmem_limit_bytes=VMEM_BIG))(*pieces, w, x, g, dres, h)


PAIR = 2 * HEAD_DIM
N_AUX = 3


def _own(shape, h):
    return lax.broadcasted_iota(jnp.int32, shape, len(shape) - 1) // HEAD_DIM == h


def _hi_dot(a, b):
    return jnp.dot(a, b, precision=_HI, preferred_element_type=F32)


def _tri(n, lower):
    r = lax.broadcasted_iota(jnp.int32, (n, n), 0)
    c = lax.broadcasted_iota(jnp.int32, (n, n), 1)
    return jnp.where(r >= c if lower else r <= c, 1.0, 0.0).astype(F32)


def _fox_cumsum(proj, b):
    L = proj.shape[0]

    def body(fl_ref, b_ref, c_ref, carry_sc):
        @pl.when(pl.program_id(0) == 0)
        def _():
            carry_sc[...] = jnp.zeros((1, PAIR), F32)

        lane = lax.broadcasted_iota(jnp.int32, (TM, PAIR), 1)
        lf = jnp.where(lane < FOX_HEADS, jax.nn.log_sigmoid(fl_ref[...] + b_ref[...]), 0.0)
        cs = _hi_dot(_tri(TM, True), lf) + carry_sc[...]
        c_ref[...] = cs
        carry_sc[...] = cs[TM - 1:TM, :]

    b, b_spec = _of_layer(b)
    return _pallas(body, out_shape=jax.ShapeDtypeStruct((L, PAIR), F32), grid=(L // TM,),
                   in_specs=[_rows(TM, PAIR, O_FL // PAIR), b_spec], out_specs=_rows(TM, PAIR),
                   scratch_shapes=[pltpu.VMEM((1, PAIR), F32)], name="fox_cumsum")(proj, b)


def _fox_prep(proj, c):
    L = proj.shape[0]

    def body(q_ref, k_ref, v_ref, c_ref, qa_ref, ka_ref, kat_ref, vt_ref):
        lane = lax.broadcasted_iota(jnp.int32, (TM, PAIR), 1)
        cv = c_ref[...]
        for p in range(FOX_HEADS // 2):
            cols = slice(PAIR * p, PAIR * (p + 1))
            q2, k2 = q_ref[:, cols], k_ref[:, cols]
            vt_ref[p] = v_ref[:, cols].T.astype(_MXU)
            for e in range(2):
                h = 2 * p + e
                own = lane // HEAD_DIM == e
                a = lane - (HEAD_DIM if e == 0 else 0)
                rest = jnp.broadcast_to(cv[:, h:h + 1], (TM, PAIR))
                aux_q = jnp.where((a >= N_AUX) & (a < 2 * N_AUX), 1.0, 0.0)
                aux_k = jnp.where((a >= 0) & (a < N_AUX), 1.0, 0.0)
                for n in range(N_AUX):
                    part = rest.astype(_MXU).astype(F32)
                    rest = rest - part
                    aux_q = jnp.where(a == n, part, aux_q)
                    aux_k = jnp.where(a == N_AUX + n, -part, aux_k)
                ka = jnp.where(own, k2, aux_k)
                qa_ref[h] = jnp.where(own, q2 * (1.0 / math.sqrt(HEAD_DIM)), aux_q).astype(_MXU)
                ka_ref[h] = ka.astype(_MXU)
                kat_ref[h] = ka.T.astype(_MXU)

    hl = jax.ShapeDtypeStruct((FOX_HEADS, L, PAIR), _MXU)
    nat = lambda o: _rows(TM, D_FOX, o // D_FOX)
    rows = pl.BlockSpec((FOX_HEADS, TM, PAIR), lambda i: (0, i, 0))
    return _pallas(
        body, out_shape=(hl, hl, jax.ShapeDtypeStruct((FOX_HEADS, PAIR, L), _MXU),
                         jax.ShapeDtypeStruct((FOX_HEADS // 2, PAIR, L), _MXU)),
        grid=(L // TM,), in_specs=[nat(O_FQ), nat(O_FK), nat(O_FV), _rows(TM, PAIR)],
        out_specs=(rows, rows, pl.BlockSpec((FOX_HEADS, PAIR, TM), lambda i: (0, 0, i)),
                   pl.BlockSpec((FOX_HEADS // 2, PAIR, TM), lambda i: (0, 0, i))),
        name="fox_prep")(proj, proj, proj, c)


def _key_le_query(tq):
    return lax.broadcasted_iota(jnp.int32, (tq, tq), 0) <= lax.broadcasted_iota(jnp.int32, (tq, tq), 1)


def _grid_ends(n0, n1):
    first = lambda: (pl.program_id(0) == 0) & (pl.program_id(1) == 0)
    last = lambda: (pl.program_id(0) == n0 - 1) & (pl.program_id(1) == n1 - 1)
    return first, last


def _fox_fwd(qa, ka, vt, ride=None):
    H, L, _ = qa.shape
    tq = min(TQ, L)
    nq = L // tq

    def body(qa_ref, ka_ref, vt_ref, o_ref, lse_ref, m_sc, l_sc, acc_sc):
        i = pl.program_id(1)
        m_sc[...] = jnp.full((2, 1, tq), NEG, F32)
        l_sc[...] = jnp.zeros((2, 1, tq), F32)
        acc_sc[...] = jnp.zeros((2, HEAD_DIM, tq), F32)

        def block(j, nk, masked):
            keys = pl.ds(pl.multiple_of(j * tq, tq), nk * tq)
            vt_blk = vt_ref[:, keys]
            sts = [_dot(ka_ref[e, keys, :], qa_ref[e], _NT) for e in range(2)]
            pts, alphas = [], []
            for e in range(2):
                st = jnp.where(_key_le_query(tq), sts[e], NEG) if masked else sts[e]
                m_prev = m_sc[e]
                m_new = jnp.maximum(m_prev, jnp.max(st, axis=0, keepdims=True))
                alphas.append(jnp.exp(m_prev - m_new))
                pt = jnp.exp(st - m_new)
                l_sc[e] = alphas[e] * l_sc[e] + jnp.sum(pt, axis=0, keepdims=True)
                m_sc[e] = m_new
                pts.append(pt.astype(_MXU))
            for e in range(2):
                acc_sc[e] = alphas[e] * acc_sc[e] + _dot(vt_blk[HEAD_DIM * e:HEAD_DIM * (e + 1)], pts[e])

        def two_blocks(jj, carry):
            block(2 * jj, 2, False)
            return carry

        lax.fori_loop(0, i // 2, two_blocks, 0)

        @pl.when(i % 2 == 1)
        def _():
            block(i - 1, 1, False)

        block(i, 1, True)
        o_ref[...] = jnp.concatenate([acc_sc[0] / l_sc[0], acc_sc[1] / l_sc[1]], axis=0).T
        for e in range(2):
            lse_ref[e] = m_sc[e] + jnp.log(l_sc[e])

    body, ex_in, ex_specs, ex_out, ex_sems = _riding(body, 3, 2, ride, *_grid_ends(H // 2, nq))
    res = _pallas(
        body, out_shape=(jax.ShapeDtypeStruct((L, D_FOX), F32), jax.ShapeDtypeStruct((H, 1, L), F32), *ex_out),
        grid=(H // 2, nq),
        in_specs=[pl.BlockSpec((2, tq, PAIR), lambda p, i: (p, i, 0)), pl.BlockSpec((2, L, PAIR), lambda p, i: (p, 0, 0)),
                  pl.BlockSpec((None, PAIR, L), lambda p, i: (p, 0, 0)), *ex_specs],
        out_specs=(pl.BlockSpec((tq, PAIR), lambda p, i: (i, p)), pl.BlockSpec((2, 1, tq), lambda p, i: (p, 0, i)),
                   *ex_specs),
        scratch_shapes=[pltpu.VMEM((2, 1, tq), F32), pltpu.VMEM((2, 1, tq), F32), pltpu.VMEM((2, HEAD_DIM, tq), F32), *ex_sems],
        name="fox_fwd" if ride is None else "fox_fwd_gather")(qa, ka, vt, *ex_in)
    return res[0], res[1], list(res[2:])


def _fox_bwd(qa, ka, kat, proj, do, o, lse, ride=None):
    H, L, _ = qa.shape
    tq = min(TQ, L)
    nq = L // tq

    def body(qa_ref, ka_ref, kat_ref, v_ref, do_ref, o_ref, lse_ref, dqt_ref, dk_ref, dv_ref, delta_sc, dk_sc, dv_sc):
        j = pl.program_id(1)

        @pl.when(j == 0)
        def _():
            head_rows = (lax.broadcasted_iota(jnp.int32, (8, PAIR), 1) // HEAD_DIM
                         == lax.broadcasted_iota(jnp.int32, (8, PAIR), 0)).astype(F32)
            delta_sc[...] = lax.dot_general(head_rows, do_ref[...] * o_ref[...], _NT, precision=_HI,
                                            preferred_element_type=F32)
            dqt_ref[...] = jnp.zeros((2, PAIR, L), F32)

        dk_sc[...] = jnp.zeros((2, tq, PAIR), F32)
        dv_sc[...] = jnp.zeros((tq, PAIR), F32)
        vb = v_ref[...]

        def block(i, masked):
            qs = pl.ds(pl.multiple_of(i * tq, tq), tq)
            dob = do_ref[qs, :]
            for e in range(2):
                own = _own((tq, PAIR), e)
                qh = qa_ref[e, qs, :]
                pt = jnp.exp(_dot(ka_ref[e], qh, _NT) - lse_ref[e, :, qs])
                if masked:
                    pt = jnp.where(_key_le_query(tq), pt, 0.0)
                dv_sc[...] += _dot(pt, jnp.where(own, dob, 0.0))
                dpt = _dot(jnp.where(own, vb, 0.0), dob, _NT)
                ds = (pt * (dpt - delta_sc[e:e + 1, qs])).astype(_MXU)
                dk_sc[e] += _dot(ds, qh)
                dqt_ref[e, :, qs] += _dot(kat_ref[e], ds)

        def off_diagonal(i, carry):
            block(i, False)
            return carry

        block(j, True)
        lax.fori_loop(j + 1, nq, off_diagonal, 0)
        dk_ref[...] = dk_sc[...]
        dv_ref[...] = dv_sc[...]

    nat = pl.BlockSpec((L, PAIR), lambda p, j: (0, p))
    body, ex_in, ex_specs, ex_out, ex_sems = _riding(body, 7, 3, ride, *_grid_ends(H // 2, nq))
    res = _pallas(
        body, out_shape=(jax.ShapeDtypeStruct((H, PAIR, L), F32), jax.ShapeDtypeStruct((H, L, PAIR), F32),
                         jax.ShapeDtypeStruct((L, D_FOX), F32), *ex_out),
        grid=(H // 2, nq),
        in_specs=[pl.BlockSpec((2, L, PAIR), lambda p, j: (p, 0, 0)), pl.BlockSpec((2, tq, PAIR), lambda p, j: (p, j, 0)),
                  pl.BlockSpec((2, PAIR, tq), lambda p, j: (p, 0, j)),
                  pl.BlockSpec((tq, PAIR), lambda p, j: (j, O_FV // PAIR + p)), nat, nat,
                  pl.BlockSpec((2, 1, L), lambda p, j: (p, 0, 0)), *ex_specs],
        out_specs=(pl.BlockSpec((2, PAIR, L), lambda p, j: (p, 0, 0)), pl.BlockSpec((2, tq, PAIR), lambda p, j: (p, j, 0)),
                   pl.BlockSpec((tq, PAIR), lambda p, j: (j, p)), *ex_specs),
        scratch_shapes=[pltpu.VMEM((8, L), F32), pltpu.VMEM((2, tq, PAIR), F32), pltpu.VMEM((tq, PAIR), F32), *ex_sems],
        name="fox_bwd" if ride is None else "fox_bwd_exchange",
        compiler_params=pltpu.CompilerParams(vmem_limit_bytes=VMEM_BIG))(qa, ka, kat, proj, do, o, lse, *ex_in)
    return res[0], res[1], res[2], list(res[3:])


def _fox_post_bwd(dqt, dkraw, proj, b):
    L = proj.shape[0]
    nb = L // TM

    def body(dqt_ref, dkr_ref, fl_ref, b_ref, dq_ref, dk_ref, dfl_ref, db_ref, carry_sc):
        first = pl.program_id(0) == 0

        @pl.when(first)
        def _():
            carry_sc[...] = jnp.zeros((1, PAIR), F32)

        lane = lax.broadcasted_iota(jnp.int32, (TM, PAIR), 1)
        rr = lax.broadcasted_iota(jnp.int32, (PAIR, PAIR), 0)
        cc = lax.broadcasted_iota(jnp.int32, (PAIR, PAIR), 1)
        dc = jnp.zeros((TM, PAIR), F32)
        for p in range(FOX_HEADS // 2):
            cols = slice(PAIR * p, PAIR * (p + 1))
            dqs = [dqt_ref[2 * p + e].T for e in range(2)]
            dks = [dkr_ref[2 * p + e] for e in range(2)]
            dq_ref[:, cols] = jnp.where(lane < HEAD_DIM, dqs[0], dqs[1]) * (1.0 / math.sqrt(HEAD_DIM))
            dk_ref[:, cols] = jnp.where(lane < HEAD_DIM, dks[0], dks[1])
            sums = jnp.zeros((TM, PAIR), F32)
            place = jnp.zeros((PAIR, PAIR), F32)
            for e in range(2):
                base = HEAD_DIM if e == 0 else 0
                sums = jnp.where(lane == base, dqs[e], jnp.where(lane == base + N_AUX, -dks[e], sums))
                place = jnp.where(((rr == base) | (rr == base + N_AUX)) & (cc == 2 * p + e), 1.0, place)
            dc = dc + _hi_dot(sums, place)
        rs = _hi_dot(_tri(TM, False), dc) + carry_sc[...]
        carry_sc[...] = rs[0:1, :]
        dfl = jnp.where(lane < FOX_HEADS, rs * jax.nn.sigmoid(-(fl_ref[...] + b_ref[...])), 0.0)
        dfl_ref[...] = dfl
        db = jnp.sum(dfl, axis=0, keepdims=True)

        @pl.when(first)
        def _():
            db_ref[...] = db

        @pl.when(jnp.logical_not(first))
        def _():
            db_ref[...] += db

    rev = lambda i: nb - 1 - i
    b, b_spec = _of_layer(b)
    nat = pl.BlockSpec((TM, D_FOX), lambda i: (rev(i), 0))
    return _pallas(
        body, out_shape=(jax.ShapeDtypeStruct((L, D_FOX), F32),) * 2
        + (jax.ShapeDtypeStruct((L, PAIR), F32), jax.ShapeDtypeStruct((1, PAIR), F32)),
        grid=(nb,),
        in_specs=[pl.BlockSpec((FOX_HEADS, PAIR, TM), lambda i: (0, 0, rev(i))),
                  pl.BlockSpec((FOX_HEADS, TM, PAIR), lambda i: (0, rev(i), 0)),
                  pl.BlockSpec((TM, PAIR), lambda i: (rev(i), O_FL // PAIR)), b_spec],
        out_specs=(nat, nat, pl.BlockSpec((TM, PAIR), lambda i: (rev(i), 0)), _whole((1, PAIR))),
        scratch_shapes=[pltpu.VMEM((1, PAIR), F32)], name="fox_post_bwd")(dqt, dkraw, proj, b)


def _s5_expand():
    r = lax.broadcasted_iota(jnp.int32, (S5_STATE, S5_STATE * S5_GROUP_CH), 0)
    c = lax.broadcasted_iota(jnp.int32, (S5_STATE, S5_STATE * S5_GROUP_CH), 1)
    return jnp.where(c // S5_GROUP_CH == r, 1.0, 0.0).astype(F32)


def _s5_disc_math(ar, ai, ldt, br, bi):
    dt = jnp.exp(ldt)
    mag = jnp.exp(ar * dt)
    lr = mag * jnp.cos(ai * dt)
    li = mag * jnp.sin(ai * dt)
    den = ar * ar + ai * ai
    fr = ((lr - 1.0) * ar + li * ai) / den
    fi = (li * ar - (lr - 1.0) * ai) / den
    e = _s5_expand()
    fre = jnp.dot(fr, e, precision=_HI, preferred_element_type=F32)
    fie = jnp.dot(fi, e, precision=_HI, preferred_element_type=F32)
    return lr, li, fre * br - fie * bi, fre * bi + fie * br


def _layer_blocks(arrs):
    return [pl.BlockSpec((None,) + a.shape[1:], lambda l: (l, 0, 0)) for a in arrs]


def _s5_disc(ar, ai, ldt, br, bi):
    def body(ar_ref, ai_ref, ldt_ref, br_ref, bi_ref, lr_ref, li_ref, bbr_ref, bbi_ref):
        lr, li, bbr, bbi = _s5_disc_math(ar_ref[...], ai_ref[...], ldt_ref[...], br_ref[...], bi_ref[...])
        lr_ref[...] = lr
        li_ref[...] = li
        bbr_ref[...] = bbr
        bbi_ref[...] = bbi

    ins = (ar, ai, ldt, br, bi)
    outs = (ar, ai, br, bi)
    return _pallas(body, out_shape=tuple(jax.ShapeDtypeStruct(a.shape, F32) for a in outs), grid=(DEPTH,),
                   in_specs=_layer_blocks(ins), out_specs=tuple(_layer_blocks(outs)), name="s5_disc")(*ins)


def _s5_disc_bwd(ar, ai, ldt, br, bi, dlr, dli, dbbr, dbbi):
    def body(ar_ref, ai_ref, ldt_ref, br_ref, bi_ref, dlr_ref, dli_ref, dbbr_ref, dbbi_ref,
             dar_ref, dai_ref, dldt_ref, dbr_ref, dbi_ref):
        _, vjp = jax.vjp(_s5_disc_math, ar_ref[...], ai_ref[...], ldt_ref[...], br_ref[...], bi_ref[...])
        dar, dai, dldt, dbr, dbi = vjp((dlr_ref[...], dli_ref[...], dbbr_ref[...], dbbi_ref[...]))
        dar_ref[...] = dar
        dai_ref[...] = dai
        dldt_ref[...] = dldt
        dbr_ref[...] = dbr
        dbi_ref[...] = dbi

    ins = (ar, ai, ldt, br, bi, dlr, dli, dbbr, dbbi)
    outs = (ar, ai, ldt, br, bi)
    return _pallas(body, out_shape=tuple(jax.ShapeDtypeStruct(a.shape, F32) for a in outs), grid=(DEPTH,),
                   in_specs=_layer_blocks(ins), out_specs=tuple(_layer_blocks(outs)), name="s5_disc_bwd")(*ins)


SLAB = 2 * S5_CH // 128
PITCH = 24


def _slab_rows(s, ts):
    return pl.ds(s, ts, stride=PITCH)


def _slab_pair(ref, s, ts):
    return jnp.concatenate([ref[_slab_rows(s, ts), :].astype(_MXU), ref[_slab_rows(s + 1, ts), :].astype(_MXU)], axis=-1)


def _s5_fwd(proj, wb, wc, lam, d, w_glu):
    L = proj.shape[0]
    ts = min(TS, L)

    def body(u_ref, wb_ref, wc_ref, lam_ref, d_ref, wg_ref, xs_ref, ypre_ref, ys_ref, b_sc, c_sc):
        @pl.when(pl.program_id(0) == 0)
        def _():
            c_sc[...] = jnp.zeros((SLAB, 128), F32)

        u = u_ref[...]
        ub = u.astype(_MXU)
        for s in range(0, SLAB, 2):
            b2 = _dot(ub, wb_ref[:, 128 * s:128 * (s + 2)])
            b_sc[_slab_rows(s, ts), :] = b2[:, :128]
            b_sc[_slab_rows(s + 1, ts), :] = b2[:, 128:]
        lr, li = lam_ref[0:8, :], lam_ref[8:16, :]

        def step(t, carry):
            xr, xi = carry
            row = pl.multiple_of(t * PITCH, 8)
            nr = lr * xr - li * xi + b_sc[pl.ds(row, 8), :]
            ni = lr * xi + li * xr + b_sc[pl.ds(row + 8, 8), :]
            xs_ref[pl.ds(row, 8), :] = nr
            xs_ref[pl.ds(row + 8, 8), :] = ni
            return nr, ni

        xr, xi = lax.fori_loop(0, ts, step, (c_sc[0:8, :], c_sc[8:16, :]), unroll=8)
        c_sc[0:8, :] = xr
        c_sc[8:16, :] = xi
        y = jnp.zeros((ts, D_S5), F32)
        for s in range(0, SLAB, 2):
            y = y + _dot(_slab_pair(xs_ref, s, ts), wc_ref[128 * s:128 * (s + 2), :])
        ypre_ref[...] = y
        y1 = jax.nn.gelu(y + d_ref[...] * u)
        ys_ref[...] = y1 * jax.nn.sigmoid(_dot(y1, wg_ref[...]))

    row = _rows(ts, D_S5)
    slabs = pl.BlockSpec((ts * PITCH, 128), lambda n: (n, 0))
    (wb, wb_spec), (wc, wc_spec), (lam, lam_spec), (d, d_spec) = (_of_layer(a) for a in (wb, wc, lam, d))
    return _pallas(
        body, out_shape=(jax.ShapeDtypeStruct((L * PITCH, 128), F32), jax.ShapeDtypeStruct((L, D_S5), F32),
                         jax.ShapeDtypeStruct((L, D_S5), F32)),
        grid=(L // ts,),
        in_specs=[_rows(ts, D_S5, O_SU // D_S5), wb_spec, wc_spec, lam_spec, d_spec, _whole((D_S5, D_S5))],
        out_specs=(slabs, row, row),
        scratch_shapes=[pltpu.VMEM((ts * PITCH, 128), F32), pltpu.VMEM((SLAB, 128), F32)], name="s5_fwd")(
            proj, wb, wc, lam, d, w_glu)


def _s5_bwd(proj, ypre, dys, xs, wb, wc, lam, d, w_glu):
    L = proj.shape[0]
    ts = min(TS, L)
    nb = L // ts

    def body(u_ref, y_ref, dys_ref, xs_ref, xp_ref, wb_ref, wc_ref, lam_ref, d_ref, wg_ref,
             du_ref, dwb_ref, dwc_ref, dlam_ref, dd_ref, dwg_ref, dx_sc, g_sc, c_sc):
        n = pl.program_id(0)

        @pl.when(n == 0)
        def _():
            c_sc[...] = jnp.zeros((SLAB, 128), F32)
            dlam_ref[...] = jnp.zeros((SLAB, 128), F32)
            dwb_ref[...] = jnp.zeros((D_S5, 2 * S5_CH), F32)
            dwc_ref[...] = jnp.zeros((2 * S5_CH, D_S5), F32)
            dd_ref[...] = jnp.zeros((1, D_S5), F32)
            dwg_ref[...] = jnp.zeros((D_S5, D_S5), F32)

        u, dv, dout = u_ref[...], d_ref[...], dys_ref[...]
        y1, gelu_vjp = jax.vjp(jax.nn.gelu, y_ref[...] + dv * u)
        sg = jax.nn.sigmoid(_dot(y1, wg_ref[...]))
        dz = dout * y1 * sg * (1.0 - sg)
        dy, = gelu_vjp(dout * sg + _dot(dz, wg_ref[...], _NT))
        dd_ref[...] += jnp.sum(dy * u, axis=0, keepdims=True)
        dwg_ref[...] += _dot(y1, dz, _TN)
        dyb = dy.astype(_MXU)
        for s in range(0, SLAB, 2):
            cols = slice(128 * s, 128 * (s + 2))
            dx2 = _dot(dyb, wc_ref[cols, :], _NT)
            dx_sc[_slab_rows(s, ts), :] = dx2[:, :128]
            dx_sc[_slab_rows(s + 1, ts), :] = dx2[:, 128:]
            dwc_ref[cols, :] += _dot(_slab_pair(xs_ref, s, ts), dyb, _TN)
        lr, li = lam_ref[0:8, :], lam_ref[8:16, :]

        def adjoint(row, pr, pi, carry):
            gr, gi, ar, ai = carry
            nr = dx_sc[pl.ds(row, 8), :] + lr * gr + li * gi
            ni = dx_sc[pl.ds(row + 8, 8), :] - li * gr + lr * gi
            g_sc[pl.ds(row, 8), :] = nr
            g_sc[pl.ds(row + 8, 8), :] = ni
            return nr, ni, ar + nr * pr + ni * pi, ai - nr * pi + ni * pr

        def step(k, carry):
            row = pl.multiple_of((ts - 1 - k) * PITCH, 8)
            prev = pl.multiple_of((ts - 2 - k) * PITCH, 8)
            return adjoint(row, xs_ref[pl.ds(prev, 8), :], xs_ref[pl.ds(prev + 8, 8), :], carry)

        z = jnp.zeros((8, 128), F32)
        carry = lax.fori_loop(0, ts - 1, step, (c_sc[0:8, :], c_sc[8:16, :], z, z), unroll=8)
        has_prev = jnp.where(n == nb - 1, 0.0, 1.0)
        gr, gi, ar, ai = adjoint(0, xp_ref[0:8, :] * has_prev, xp_ref[8:16, :] * has_prev, carry)
        c_sc[0:8, :] = gr
        c_sc[8:16, :] = gi
        dlam_ref[0:8, :] += ar
        dlam_ref[8:16, :] += ai
        ub = u.astype(_MXU)
        du = dy * dv
        for s in range(0, SLAB, 2):
            cols = slice(128 * s, 128 * (s + 2))
            gs = _slab_pair(g_sc, s, ts)
            du = du + _dot(gs, wb_ref[:, cols], _NT)
            dwb_ref[:, cols] += _dot(ub, gs, _TN)
        du_ref[...] = du

    blk = lambda n: nb - 1 - n
    row = pl.BlockSpec((ts, D_S5), lambda n: (blk(n), 0))
    (wb, wb_spec), (wc, wc_spec), (lam, lam_spec), (d, d_spec) = (_of_layer(a) for a in (wb, wc, lam, d))
    return _pallas(
        body, out_shape=(jax.ShapeDtypeStruct((L, D_S5), F32), jax.ShapeDtypeStruct((D_S5, 2 * S5_CH), F32),
                         jax.ShapeDtypeStruct((2 * S5_CH, D_S5), F32), jax.ShapeDtypeStruct((SLAB, 128), F32),
                         jax.ShapeDtypeStruct((1, D_S5), F32), jax.ShapeDtypeStruct((D_S5, D_S5), F32)),
        grid=(nb,),
        in_specs=[pl.BlockSpec((ts, D_S5), lambda n: (blk(n), O_SU // D_S5)), row, row,
                  pl.BlockSpec((ts * PITCH, 128), lambda n: (blk(n), 0)),
                  pl.BlockSpec((PITCH, 128), lambda n: (jnp.maximum(blk(n) * ts - 1, 0), 0)),
                  wb_spec, wc_spec, lam_spec, d_spec, _whole((D_S5, D_S5))],
        out_specs=(row, _whole((D_S5, 2 * S5_CH)), _whole((2 * S5_CH, D_S5)), _whole((SLAB, 128)), _whole((1, D_S5)),
                   _whole((D_S5, D_S5))),
        scratch_shapes=[pltpu.VMEM((ts * PITCH, 128), F32), pltpu.VMEM((ts * PITCH, 128), F32), pltpu.VMEM((SLAB, 128), F32)],
        name="s5_bwd", compiler_params=pltpu.CompilerParams(vmem_limit_bytes=VMEM_BIG))(
            proj, ypre, dys, xs, xs, wb, wc, lam, d, w_glu)


def _rot(z, cos, sin):
    lane = lax.broadcasted_iota(jnp.int32, z.shape, 1)
    zs = z * sin
    half = HEAD_DIM // 2
    return z * cos + jnp.where(lane % HEAD_DIM < half, pltpu.roll(zs, PAIR - half, 1), pltpu.roll(zs, half, 1))


def _head_avg():
    r = lax.broadcasted_iota(jnp.int32, (PAIR, PAIR), 0) // HEAD_DIM
    c = lax.broadcasted_iota(jnp.int32, (PAIR, PAIR), 1) // HEAD_DIM
    return jnp.where(r == c, 1.0 / HEAD_DIM, 0.0).astype(F32)


def _ret_tables(tq):
    lg = jnp.log1p(-(2.0 ** (-5.0 - jnp.arange(RET_HEADS, dtype=F32))))
    scale = 1.0 / math.sqrt(HEAD_DIM)
    pos = jnp.arange(tq)
    n = pos.astype(F32)
    dist = jnp.abs(n[:, None] - n[None, :])
    ok = (pos[None, :] // CHUNK) <= (pos[:, None] // CHUNK)
    w = jnp.where(ok[None], scale * jnp.exp(lg[:, None, None] * dist[None]), 0.0)
    lgl = jnp.repeat(lg, HEAD_DIM)
    dq_tab = scale * jnp.exp(lgl[None, :] * (n[:, None] + 1.0))
    dk_tab = jnp.exp(lgl[None, :] * (tq - 1.0 - n[:, None]))
    blk = jnp.arange(PAIR) // HEAD_DIM
    bd = (blk[:, None] == blk[None, :]).astype(F32)
    gbd = bd[None] * jnp.exp(lgl.reshape(RET_HEADS // 2, PAIR)[:, :, None] * tq)
    return dict(w=w, wt=w.transpose(0, 2, 1), dq=dq_tab, dk=dk_tab, gbd=gbd, bd=bd)


def _ret_specs(tq, nq, rev, layer):
    blk = (lambda i: nq - 1 - i) if rev else (lambda i: i)
    col = lambda o: pl.BlockSpec((tq, PAIR), lambda p, i: (blk(i), o // PAIR + p))
    return dict(
        rq=col(O_RQ), rk=col(O_RK), rv=col(O_RV), nat=col(0),
        w=pl.BlockSpec((2, tq, tq), lambda p, i: (p, 0, 0)), tab=pl.BlockSpec((tq, PAIR), lambda p, i: (0, p)),
        gbd=pl.BlockSpec((None, PAIR, PAIR), lambda p, i: (p, 0, 0)), bd=pl.BlockSpec((PAIR, PAIR), lambda p, i: (0, 0)),
        gn=pl.BlockSpec((None, 1, PAIR), lambda p, i: (layer, 0, p)), dgn=pl.BlockSpec((1, PAIR), lambda p, i: (0, p)),
        st=pl.BlockSpec((None, None, PAIR, PAIR), lambda p, i: (p, blk(i), 0, 0)))


def _ret_fwd(proj, cos_t, sin_t, tabs, gn):
    L = proj.shape[0]
    tq = tabs["w"].shape[1]
    nq = L // tq

    def body(rq_ref, rk_ref, rv_ref, cos_ref, sin_ref, w_ref, dqt_ref, dkt_ref, gbd_ref, bd_ref, gn_ref,
             o_ref, y_ref, st_ref, s_sc):
        @pl.when(pl.program_id(1) == 0)
        def _():
            s_sc[...] = jnp.zeros((PAIR, PAIR), F32)

        state = s_sc[...]
        st_ref[...] = state
        cos, sin = cos_ref[...], sin_ref[...]
        q2, k2, v2 = _rot(rq_ref[...], cos, sin), _rot(rk_ref[...], cos, sin), rv_ref[...]
        owns = [_own((tq, PAIR), h) for h in range(2)]
        scores = [_dot(jnp.where(owns[h], q2, 0.0), k2, _NT) for h in range(2)]
        o = _dot(q2 * dqt_ref[...], state)
        for h in range(2):
            o = o + _dot(scores[h] * w_ref[h], jnp.where(owns[h], v2, 0.0))
        s_sc[...] = gbd_ref[...] * state + bd_ref[...] * _dot(k2 * dkt_ref[...], v2, _TN)
        o_ref[...] = o
        avg = _head_avg()
        oc = o - _hi_dot(o, avg)
        y_ref[...] = oc * lax.rsqrt(_hi_dot(oc * oc, avg) + EPS) * gn_ref[...]

    gn, layer = gn
    sp = _ret_specs(tq, nq, False, layer)
    nat = jax.ShapeDtypeStruct((L, D_RET), F32)
    return _pallas(
        body, out_shape=(nat, nat, jax.ShapeDtypeStruct((RET_HEADS // 2, nq, PAIR, PAIR), F32)), grid=(RET_HEADS // 2, nq),
        in_specs=[sp["rq"], sp["rk"], sp["rv"], sp["nat"], sp["nat"], sp["w"], sp["tab"], sp["tab"], sp["gbd"], sp["bd"],
                  sp["gn"]],
        out_specs=(sp["nat"], sp["nat"], sp["st"]), scratch_shapes=[pltpu.VMEM((PAIR, PAIR), F32)],
        name="ret_fwd")(proj, proj, proj, cos_t, sin_t, tabs["w"], tabs["dq"], tabs["dk"], tabs["gbd"], tabs["bd"], gn)


def _ret_bwd(proj, cos_t, sin_t, tabs, gn, o_pre, dy, states):
    L = proj.shape[0]
    tq = tabs["w"].shape[1]
    nq = L // tq

    def body(rq_ref, rk_ref, rv_ref, cos_ref, sin_ref, w_ref, wt_ref, dqt_ref, dkt_ref, gbd_ref, bd_ref, gn_ref,
             o_ref, dy_ref, st_ref, drq_ref, drk_ref, drv_ref, dgn_ref, g_sc):
        first = pl.program_id(1) == 0

        @pl.when(first)
        def _():
            g_sc[...] = jnp.zeros((PAIR, PAIR), F32)

        cos, sin = cos_ref[...], sin_ref[...]
        q2, k2, v2 = _rot(rq_ref[...], cos, sin), _rot(rk_ref[...], cos, sin), rv_ref[...]
        avg = _head_avg()
        ov, dyv = o_ref[...], dy_ref[...]
        oc = ov - _hi_dot(ov, avg)
        r = lax.rsqrt(_hi_dot(oc * oc, avg) + EPS)
        oh = oc * r
        dgn = jnp.sum(dyv * oh, axis=0, keepdims=True)
        doh = dyv * gn_ref[...]
        do = r * (doh - _hi_dot(doh, avg) - oh * _hi_dot(doh * oh, avg))
        state, g = st_ref[...], g_sc[...]
        dqt, dkt = dqt_ref[...], dkt_ref[...]
        dq = _dot(do, state, _NT) * dqt
        dk = _dot(v2, g, _NT) * dkt
        dv = _dot(k2 * dkt, g)
        g_sc[...] = gbd_ref[...] * g + bd_ref[...] * _dot(q2 * dqt, do, _TN)
        owns = [_own((tq, PAIR), h) for h in range(2)]
        qms = [jnp.where(owns[h], q2, 0.0) for h in range(2)]
        doms = [jnp.where(owns[h], do, 0.0) for h in range(2)]
        ats = [_dot(k2, qms[h], _NT) for h in range(2)]
        das = [_dot(doms[h], v2, _NT) for h in range(2)]
        for h in range(2):
            dv = dv + _dot(ats[h] * wt_ref[h], doms[h])
            daw = (das[h] * w_ref[h]).astype(_MXU)
            dq = dq + _dot(daw, jnp.where(owns[h], k2, 0.0))
            dk = dk + _dot(daw.T, qms[h])
        drq_ref[...] = _rot(dq, cos, -sin)
        drk_ref[...] = _rot(dk, cos, -sin)
        drv_ref[...] = dv

        @pl.when(first)
        def _():
            dgn_ref[...] = dgn

        @pl.when(jnp.logical_not(first))
        def _():
            dgn_ref[...] += dgn

    gn, layer = gn
    sp = _ret_specs(tq, nq, True, layer)
    nat = jax.ShapeDtypeStruct((L, D_RET), F32)
    return _pallas(
        body, out_shape=(nat, nat, nat, jax.ShapeDtypeStruct((1, D_RET), F32)), grid=(RET_HEADS // 2, nq),
        in_specs=[sp["rq"], sp["rk"], sp["rv"], sp["nat"], sp["nat"], sp["w"], sp["w"], sp["tab"], sp["tab"], sp["gbd"],
                  sp["bd"], sp["gn"], sp["nat"], sp["nat"], sp["st"]],
        out_specs=(sp["nat"], sp["nat"], sp["nat"], sp["dgn"]), scratch_shapes=[pltpu.VMEM((PAIR, PAIR), F32)],
        name="ret_bwd")(proj, proj, proj, cos_t, sin_t, tabs["w"], tabs["wt"], tabs["dq"], tabs["dk"], tabs["gbd"],
                        tabs["bd"], gn, o_pre, dy, states)


def _gate_out(yf, ys, yr, proj, x, w):
    L = x.shape[0]

    def body(yf_ref, ys_ref, yr_ref, g_ref, x_ref, w_ref, xn_ref):
        cat = jnp.concatenate([yf_ref[...], ys_ref[...], yr_ref[...]], axis=-1)
        xn_ref[...] = x_ref[...] + _dot(cat * jax.nn.silu(g_ref[...]), w_ref[...])

    tm = min(TMB, L)
    full = _rows(tm, D_MODEL)
    return _pallas(body, out_shape=jax.ShapeDtypeStruct((L, D_MODEL), F32), grid=(L // tm,),
                   in_specs=[_rows(tm, D_FOX), _rows(tm, D_S5), _rows(tm, D_RET), _rows(tm, D_MODEL, O_GATE // D_MODEL),
                             full, _whole((D_MODEL, D_MODEL))],
                   out_specs=full, name="gate_out")(yf, ys, yr, proj, x, w)


def _gate_out_bwd(dxn, w, yf, ys, yr, proj):
    L = dxn.shape[0]

    def body(dx_ref, w_ref, yf_ref, ys_ref, yr_ref, g_ref, dyf_ref, dys_ref, dyr_ref, dg_ref, dw_ref):
        dxv = dx_ref[...].astype(_MXU)
        dy = _dot(dxv, w_ref[...], _NT)
        g = g_ref[...]
        sg = jax.nn.sigmoid(g)
        silu = g * sg
        dcat = dy * silu
        dyf_ref[...] = dcat[:, :D_FOX]
        dys_ref[...] = dcat[:, D_FOX:D_FOX + D_S5]
        dyr_ref[...] = dcat[:, D_FOX + D_S5:]
        cat = jnp.concatenate([yf_ref[...], ys_ref[...], yr_ref[...]], axis=-1)
        dg_ref[...] = dy * cat * (sg * (1.0 + g * (1.0 - sg)))
        dw = _dot(cat * silu, dxv, _TN)

        @pl.when(pl.program_id(0) == 0)
        def _():
            dw_ref[...] = dw

        @pl.when(pl.program_id(0) != 0)
        def _():
            dw_ref[...] += dw

    tm = min(TMB, L)
    full = _rows(tm, D_MODEL)
    f, s, r = _rows(tm, D_FOX), _rows(tm, D_S5), _rows(tm, D_RET)
    sq = _whole((D_MODEL, D_MODEL))
    return _pallas(body, out_shape=(jax.ShapeDtypeStruct((L, D_FOX), F32), jax.ShapeDtypeStruct((L, D_S5), F32),
                                    jax.ShapeDtypeStruct((L, D_RET), F32), jax.ShapeDtypeStruct((L, D_MODEL), F32),
                                    jax.ShapeDtypeStruct((D_MODEL, D_MODEL), F32)),
                   grid=(L // tm,), in_specs=[full, sq, f, s, r, _rows(tm, D_MODEL, O_GATE // D_MODEL)],
                   out_specs=(f, s, r, full, sq), name="gate_out_bwd",
                   compiler_params=pltpu.CompilerParams(vmem_limit_bytes=VMEM_BIG))(dxn, w, yf, ys, yr, proj)


def _final_loss(x, g, tgt):
    L = x.shape[0]

    def body(x_ref, g_ref, t_ref, loss_ref, dx_ref, dg_ref):
        xv, gv = x_ref[...], g_ref[...]
        r = lax.rsqrt(jnp.mean(xv * xv, axis=-1, keepdims=True) + EPS)
        err = xv * r * gv - t_ref[...]
        part = 0.5 * jnp.sum(jnp.mean(err * err, axis=-1, keepdims=True), axis=0, keepdims=True)
        dx, dg = _rms_bwd(xv, gv, err * (1.0 / D_MODEL))
        dx_ref[...] = dx

        @pl.when(pl.program_id(0) == 0)
        def _():
            loss_ref[...] = part
            dg_ref[...] = dg

        @pl.when(pl.program_id(0) != 0)
        def _():
            loss_ref[...] += part
            dg_ref[...] += dg

    full = _rows(TM, D_MODEL)
    return _pallas(body, out_shape=(jax.ShapeDtypeStruct((1, 1), F32), jax.ShapeDtypeStruct((L, D_MODEL), F32),
                                    jax.ShapeDtypeStruct((1, D_MODEL), F32)),
                   grid=(L // TM,), in_specs=[full, _whole((1, D_MODEL)), full],
                   out_specs=(_whole((1, 1)), full, _whole((1, D_MODEL))), name="final_loss")(x, g, tgt)


def _block_diag(blocks):
    n, g, r, c = blocks.shape
    eye = jnp.eye(g, dtype=blocks.dtype)
    return (blocks[:, :, :, None, :] * eye[None, :, None, :, None]).reshape(n, g * r, g * c)


def _diag_blocks(m, g):
    n, r, c = m.shape[0], m.shape[1] // g, m.shape[2] // g
    eye = jnp.eye(g, dtype=m.dtype)
    return jnp.sum(m.reshape(n, g, r, g, c) * eye[None, :, None, :, None], axis=3)


def _rope_tables(L):
    half = HEAD_DIM // 2
    freqs = ROPE_BASE ** (-jnp.arange(half, dtype=F32) / half)
    ang = jnp.arange(L, dtype=F32)[:, None] * freqs[None, :]
    cos, sin = jnp.cos(ang), jnp.sin(ang)
    cos_t = jnp.tile(jnp.concatenate([cos, cos], axis=-1), (1, RET_HEADS))
    sin_t = jnp.tile(jnp.concatenate([sin, -sin], axis=-1), (1, RET_HEADS))
    return cos_t, sin_t


def _s5_disc_args(small):
    g, s, ch = S5_GROUPS, S5_STATE, S5_GROUP_CH
    return (small["s5_a_re"], small["s5_a_im"], small["s5_log_dt"][:, :, None],
            small["s5_b_re"].reshape(DEPTH, g, s * ch), small["s5_b_im"].reshape(DEPTH, g, s * ch))


def _s5_mats(small):
    g, s, ch = S5_GROUPS, S5_STATE, S5_GROUP_CH
    lr, li, bbr, bbi = _s5_disc(*_s5_disc_args(small))
    lam = jnp.concatenate([lr.reshape(DEPTH, 8, 128), li.reshape(DEPTH, 8, 128)], axis=1)
    wb = jnp.concatenate([_block_diag(b.reshape(DEPTH, g, s, ch).transpose(0, 1, 3, 2)) for b in (bbr, bbi)], axis=2)
    wc = jnp.concatenate([_block_diag(c.transpose(0, 1, 3, 2)) for c in (small["s5_c_re"], -small["s5_c_im"])], axis=1)
    return lam, wb.astype(_MXU), wc.astype(_MXU)


def _s5_param_grads(small, dwb, dwc, dlam):
    g, s, ch = S5_GROUPS, S5_STATE, S5_GROUP_CH
    dc = [_diag_blocks(m, g).transpose(0, 1, 3, 2).reshape(DEPTH, g, ch * s) for m in (dwc[:, :S5_CH], dwc[:, S5_CH:])]
    dbb = [_diag_blocks(m, g).transpose(0, 1, 3, 2).reshape(DEPTH, g, s * ch) for m in (dwb[:, :, :S5_CH], dwb[:, :, S5_CH:])]
    dar, dai, dldt, dbr, dbi = _s5_disc_bwd(*_s5_disc_args(small), dlam[:, :8].reshape(DEPTH, g, s),
                                            dlam[:, 8:].reshape(DEPTH, g, s), dbb[0], dbb[1])
    return dict(s5_a_re=dar, s5_a_im=dai, s5_log_dt=dldt.reshape(DEPTH, g), s5_b_re=dbr, s5_b_im=dbi, s5_c_re=dc[0],
                s5_c_im=-dc[1])


_DENSE = ("s5_b_re", "s5_b_im", "s5_c_re", "s5_c_im")


def _layer_fwd(x, p, rope, ride=None, late=False):
    L = x.shape[0]
    cos_t, sin_t, ret_tabs = rope
    s = {"x": x}
    proj, h = _norm_inproj(x, p["norm_w"], p["w_in"])
    s["proj"], s["h"] = proj, h
    qa, ka, kat, vt = _fox_prep(proj, _fox_cumsum(proj, p["b_f"]))
    yf, lse, landed = _fox_fwd(qa, ka, vt, ride)
    s.update(qa=qa, ka=ka, kat=kat, lse=lse, yf=yf)
    if late:
        p["w_glu"], p["w_out"] = _gathered_rows(landed[-2]), _gathered_rows(landed[-1])
        landed = landed[:-2]
    xs, ypre, ys = _s5_fwd(proj, p["wb"], p["wc"], p["lam"], p["d"], p["w_glu"])
    s.update(xs=xs, ypre=ypre, ys=ys)
    o_pre, yr, states = _ret_fwd(proj, cos_t, sin_t, ret_tabs, p["gn_w"])
    s.update(o_pre=o_pre, yr=yr, states=states)
    return _gate_out(yf, ys, yr, proj, x, p["w_out"]), s, landed


def _layer_bwd(dxn, s, p, rope, ride=None, early=False):
    L = dxn.shape[0]
    cos_t, sin_t, ret_tabs = rope
    g = {}
    proj = s["proj"]
    dyf, dys, dyr, dgate, g["w_out"] = _gate_out_bwd(dxn, p["w_out"], s["yf"], s["ys"], s["yr"], proj)
    drq, drk, drv, dgn = _ret_bwd(proj, cos_t, sin_t, ret_tabs, p["gn_w"], s["o_pre"], dyr, s["states"])
    g["ret_gn_w"] = dgn.reshape(D_RET)
    dsu, g["wb"], g["wc"], g["lam"], dd, g["s5_w_glu"] = _s5_bwd(proj, s["ypre"], dys, s["xs"], p["wb"], p["wc"], p["lam"],
                                                                 p["d"], p["w_glu"])
    g["s5_d"] = dd.reshape(D_S5)
    if early:
        ride = (ride[0] + _row_slots(g), ride[1] + [True, True])
    dqt, dkraw, dv, landed = _fox_bwd(s["qa"], s["ka"], s["kat"], proj, dyf, s["yf"], s["lse"], ride)
    dq, dk, dfl, dbf = _fox_post_bwd(dqt, dkraw, proj, p["b_f"])
    g["fox_b_f"] = dbf[0, :FOX_HEADS]
    pieces = [dgate, dq, dk, dv, dsu, drq, drk, drv, dfl]
    dx, dnw, g["w_in"] = _inproj_bwd(pieces, p["w_in"], s["x"], p["norm_w"], dxn, s["h"])
    g["norm_w"] = dnw.reshape(D_MODEL)
    return dx, g, landed


def _stacked_params(small):
    lam, wb, wc = _s5_mats(small)
    row = lambda a: a[:, None, :]
    return dict(norm_w=row(small["norm_w"]), b_f=row(jnp.pad(small["fox_b_f"], ((0, 0), (0, PAIR - FOX_HEADS)))),
                lam=lam, wb=wb, wc=wc, d=row(small["s5_d"]), gn_w=row(small["ret_gn_w"]))


def _layer_params(l, w_in_p, w_glu, w_out, stacked):
    return dict({k: (a, l) for k, a in stacked.items()}, w_in=w_in_p, w_glu=w_glu, w_out=w_out)


_SHARDED = ("w_in", "s5_w_glu", "w_out")
_WIRE = jnp.bfloat16


_RUNS = ((2568, 3592, O_GATE), (0, 1536, O_FQ), (1544, 2568, O_SU), (1536, 1544, O_FL))


def _shard_pieces():
    out = []
    for a, b, pad in _RUNS:
        while a < b:
            j = a // W_SHARD
            e = min(b, (j + 1) * W_SHARD)
            out.append((j, a - j * W_SHARD, e - j * W_SHARD, pad))
            pad, a = pad + e - a, e
    return out


def _gathered_w_in(g_in):
    cols = [g_in[j, :, a:e] for j, a, e, _ in _shard_pieces()]
    cols.append(jnp.zeros((D_MODEL, D_INP - O_FL - FOX_HEADS), g_in.dtype))
    return jnp.concatenate(cols, axis=1)


def _gathered_rows(g):
    return g.reshape(-1, g.shape[-1])


def _w_in_slots(g):
    w_in = g["w_in"].astype(_WIRE)
    slots = []
    for j in range(N_DEV):
        mine = sorted((a, e, pad) for jj, a, e, pad in _shard_pieces() if jj == j)
        slots.append(jnp.concatenate([w_in[:, pad:pad + e - a] for a, e, pad in mine], axis=1))
    return jnp.stack(slots)


def _row_slots(g):
    return [g["s5_w_glu"].reshape(N_DEV, D_S5 // N_DEV, D_S5).astype(_WIRE),
            g["w_out"].reshape(N_DEV, D_MODEL // N_DEV, D_MODEL).astype(_WIRE)]


def _step_grads(x, tgt, small, full=None, shards=None):
    L = x.shape[0]
    rope = _rope_tables(L) + (_ret_tables(min(TQ, L)),)
    stacked = _stacked_params(small)
    if shards is not None:
        nxt = (_gathered_w_in(_exchange([shards[0][0]], [False], "gather_layer0")[0]), None, None)
    saved, params = [], []
    for l in range(DEPTH):
        weights = nxt if shards is not None else tuple(f[l] for f in full)
        ride = None
        if shards is not None:
            arrs = [s[l + 1] for s in shards] if l + 1 < DEPTH else []
            arrs += [shards[1][0], shards[2][0]] if l == 0 else []
            ride = (arrs, [False] * len(arrs)) if arrs else None
        params.append(_layer_params(l, *weights, stacked))
        x, s, landed = _layer_fwd(x, params[l], rope, ride, late=shards is not None and l == 0)
        if landed:
            nxt = (_gathered_w_in(landed[0]), _gathered_rows(landed[1]), _gathered_rows(landed[2]))
        saved.append(s)
    loss, dx, dfw = _final_loss(x, small["final_norm_w"][None], tgt)
    grads, partials, waiting = [None] * DEPTH, [None] * DEPTH, None
    for l in reversed(range(DEPTH)):
        ride = (waiting, [True] * len(waiting)) if waiting is not None else None
        dx, grads[l], landed = _layer_bwd(dx, saved[l], params[l], rope, ride, early=ride is not None and l == 0)
        if waiting is not None:
            partials[l + 1] = landed[:3]
        if shards is not None:
            waiting = [_w_in_slots(grads[l])] + _row_slots(grads[l])
    stack = lambda n: jnp.stack([g[n] for g in grads])
    small_g = {n: stack(n) for n in ("norm_w", "fox_b_f", "s5_d", "ret_gn_w")}
    small_g.update(_s5_param_grads(small, stack("wb"), stack("wc"), stack("lam")), final_norm_w=dfw)
    if shards is None:
        return loss, dx, grads, small_g
    last = _exchange([waiting[0]] + [small_g[n].astype(_WIRE) for n in _SMALL], [True] + [False] * len(_SMALL),
                     "exchange_layer0")
    partials[0] = [last[0]] + landed[3:]
    return loss, dx, grads, small_g, partials, dict(zip(_SMALL, last[1:]))


_MESH = pl.DeviceIdType.MESH
_ANY = pl.BlockSpec(memory_space=pl.ANY)


def _me_and_peers():
    x, y, c = lax.axis_index("x"), lax.axis_index("y"), lax.axis_index("c")
    flip = lambda a, bit: (1 - a) if bit else a
    peers = []
    for r in range(1, N_DEV):
        px, py, pc = flip(x, (r >> 2) & 1), flip(y, (r >> 1) & 1), flip(c, r & 1)
        peers.append(((px, py, pc), 4 * px + 2 * py + pc))
    return 4 * x + 2 * y + c, peers


def _exchange_copies(srcs, dsts, sems, scatter):
    send_sems, recv_sems, local_sems = sems
    me, peers = _me_and_peers()
    pick = lambda t, to: srcs[t].at[to] if scatter[t] else srcs[t]
    own = [pltpu.make_async_copy(pick(t, me), dsts[t].at[me], local_sems.at[t]) for t in range(len(srcs))]
    sends, waits = [], []
    for r, (dev, idx) in enumerate(peers):
        for t in range(len(srcs)):
            for land, out in ((me, sends), (idx, waits)):
                out.append(pltpu.make_async_remote_copy(pick(t, idx), dsts[t].at[land], send_sems.at[t, r], recv_sems.at[t, r],
                                                        device_id=dev, device_id_type=_MESH))
    return own, sends, waits


def _exchange_start(srcs, dsts, sems, scatter):
    own, sends, _ = _exchange_copies(srcs, dsts, sems, scatter)
    for cp in own + sends:
        cp.start()


def _exchange_wait(srcs, dsts, sems, scatter):
    own, _, waits = _exchange_copies(srcs, dsts, sems, scatter)
    for cp in waits + own:
        cp.wait()


def _exchange_shapes(arrs, scatter):
    outs = [jax.ShapeDtypeStruct(a.shape if sc else (N_DEV,) + a.shape, a.dtype) for a, sc in zip(arrs, scatter)]
    n = len(arrs)
    sems = [pltpu.SemaphoreType.DMA((n, N_DEV - 1)), pltpu.SemaphoreType.DMA((n, N_DEV - 1)), pltpu.SemaphoreType.DMA((n,))]
    return outs, sems


def _exchange(arrs, scatter, name):
    n = len(arrs)

    def body(*refs):
        _exchange_start(refs[:n], refs[n:2 * n], refs[2 * n:], scatter)
        _exchange_wait(refs[:n], refs[n:2 * n], refs[2 * n:], scatter)

    outs, sems = _exchange_shapes(arrs, scatter)
    return _pallas(body, out_shape=tuple(outs), in_specs=[_ANY] * n, out_specs=tuple([_ANY] * n), scratch_shapes=sems,
                   name=name)(*arrs)


def _riding(body, n_in, n_out, ride, is_first, is_last):
    if ride is None:
        return body, [], [], [], []
    arrs, scatter = ride
    n = len(arrs)
    outs, sems = _exchange_shapes(arrs, scatter)

    def wrapped(*refs):
        ins, srcs = refs[:n_in], refs[n_in:n_in + n]
        own_outs, dsts = refs[n_in + n:n_in + n + n_out], refs[n_in + n + n_out:n_in + 2 * n + n_out]
        scratch, ex_sems = refs[n_in + 2 * n + n_out:-3], refs[-3:]

        @pl.when(is_first())
        def _():
            _exchange_start(srcs, dsts, ex_sems, scatter)

        body(*ins, *own_outs, *scratch)

        @pl.when(is_last())
        def _():
            _exchange_wait(srcs, dsts, ex_sems, scatter)

    return wrapped, list(arrs), [_ANY] * n, outs, sems


def _adamw_body(p_ref, w_ref, m_ref, v_ref, g_ref, d_ref, nm_ref, nv_ref):
    g = p_ref[0].astype(F32)
    for i in range(1, N_DEV):
        g = g + p_ref[i].astype(F32)
    nm = ADAM_B1 * m_ref[...] + (1.0 - ADAM_B1) * g
    nv = ADAM_B2 * v_ref[...] + (1.0 - ADAM_B2) * jnp.square(g)
    m_hat = nm / (1.0 - ADAM_B1 ** ADAM_STEP)
    v_hat = nv / (1.0 - ADAM_B2 ** ADAM_STEP)
    g_ref[...] = g
    d_ref[...] = -ADAM_LR * (m_hat / (jnp.sqrt(v_hat) + ADAM_EPS) + ADAM_WD * w_ref[...])
    nm_ref[...] = nm
    nv_ref[...] = nv


def _adamw(parts, w, m, v, name):
    n, nb, rows, cols = parts.shape
    tm = next(t for t in (256, 128, 64, 32, 16) if rows % t == 0)

    def body(*refs):
        _adamw_body(*refs)

    row = pl.BlockSpec((None, tm, cols), lambda b, i: (b, i, 0))
    return _pallas(body, out_shape=(jax.ShapeDtypeStruct((nb, rows, cols), F32),) * 4, grid=(nb, rows // tm),
                   in_specs=[pl.BlockSpec((n, None, tm, cols), lambda b, i: (0, b, i, 0)), row, row, row],
                   out_specs=(row,) * 4, name=name)(parts, w, m, v)


def _adamw_whole(parts, w, m, v, name):
    def body(*refs):
        _adamw_body(*refs)

    if w.ndim == 2:
        grid = (1,)
        slab = pl.BlockSpec(w.shape, lambda b: (0, 0))
        part = pl.BlockSpec(parts.shape, lambda b: (0, 0, 0))
    else:
        grid, rest = (w.shape[0],), w.shape[1:]
        zeros = (0,) * len(rest)
        slab = pl.BlockSpec((None,) + rest, lambda b: (b,) + zeros)
        part = pl.BlockSpec((N_DEV, None) + rest, lambda b: (0, b) + zeros)
    return _pallas(body, out_shape=(jax.ShapeDtypeStruct(w.shape, F32),) * 4, grid=grid,
                   in_specs=[part, slab, slab, slab], out_specs=(slab,) * 4, name=name)(parts, w, m, v)


_WEIGHTS = ("norm_w", "w_in", "fox_b_f", "s5_a_re", "s5_a_im", "s5_b_re", "s5_b_im", "s5_c_re", "s5_c_im", "s5_d",
            "s5_log_dt", "s5_w_glu", "ret_gn_w", "w_out", "final_norm_w")
_SMALL = tuple(n for n in _WEIGHTS if n not in _SHARDED)


def kernel(x, norm_w, w_in, fox_b_f, s5_a_re, s5_a_im, s5_b_re, s5_b_im, s5_c_re, s5_c_im, s5_d, s5_log_dt, s5_w_glu, ret_gn_w, w_out, final_norm_w, loss_target, m_norm_w, m_w_in, m_fox_b_f, m_s5_a_re, m_s5_a_im, m_s5_b_re, m_s5_b_im, m_s5_c_re, m_s5_c_im, m_s5_d, m_s5_log_dt, m_s5_w_glu, m_ret_gn_w, m_w_out, m_final_norm_w, v_norm_w, v_w_in, v_fox_b_f, v_s5_a_re, v_s5_a_im, v_s5_b_re, v_s5_b_im, v_s5_c_re, v_s5_c_im, v_s5_d, v_s5_log_dt, v_s5_w_glu, v_ret_gn_w, v_w_out, v_final_norm_w):
    w = dict(norm_w=norm_w, w_in=w_in, fox_b_f=fox_b_f, s5_a_re=s5_a_re, s5_a_im=s5_a_im, s5_b_re=s5_b_re, s5_b_im=s5_b_im,
             s5_c_re=s5_c_re, s5_c_im=s5_c_im, s5_d=s5_d, s5_log_dt=s5_log_dt, s5_w_glu=s5_w_glu, ret_gn_w=ret_gn_w,
             w_out=w_out, final_norm_w=final_norm_w)
    m = dict(norm_w=m_norm_w, w_in=m_w_in, fox_b_f=m_fox_b_f, s5_a_re=m_s5_a_re, s5_a_im=m_s5_a_im, s5_b_re=m_s5_b_re,
             s5_b_im=m_s5_b_im, s5_c_re=m_s5_c_re, s5_c_im=m_s5_c_im, s5_d=m_s5_d, s5_log_dt=m_s5_log_dt,
             s5_w_glu=m_s5_w_glu, ret_gn_w=m_ret_gn_w, w_out=m_w_out, final_norm_w=m_final_norm_w)
    v = dict(norm_w=v_norm_w, w_in=v_w_in, fox_b_f=v_fox_b_f, s5_a_re=v_s5_a_re, s5_a_im=v_s5_a_im, s5_b_re=v_s5_b_re,
             s5_b_im=v_s5_b_im, s5_c_re=v_s5_c_re, s5_c_im=v_s5_c_im, s5_d=v_s5_d, s5_log_dt=v_s5_log_dt,
             s5_w_glu=v_s5_w_glu, ret_gn_w=v_ret_gn_w, w_out=v_w_out, final_norm_w=v_final_norm_w)

    small = {n: w[n] for n in _SMALL}
    loss, dx, _, _, partials, r_small = _step_grads(x[0], loss_target[0], small, shards=[w[n].astype(_MXU) for n in _SHARDED])

    res = {}
    for t, n in enumerate(_SHARDED):
        res[n] = _adamw(jnp.stack([partials[l][t] for l in range(DEPTH)], axis=1), w[n], m[n], v[n], "adamw_" + n)
    for n in _SMALL:
        shape = w[n].shape
        view = (1,) + shape if len(shape) == 1 else shape[:2] + (-1,) if n in _DENSE else shape
        outs = _adamw_whole(r_small[n], *[d[n].reshape(view) for d in (w, m, v)], "adamw_" + n)
        res[n] = [o.reshape(shape) for o in outs]

    loss = lax.psum(loss[0, 0], ("x", "y", "c"))
    return (loss, dx[None], *[res[n][0] for n in _WEIGHTS], *[res[n][1] for n in _WEIGHTS],
            *[res[n][2] for n in _WEIGHTS], *[res[n][3] for n in _WEIGHTS])
```

```python
import math

import jax
import jax.numpy as jnp
from jax import lax
from jax.experimental import pallas as pl
from jax.experimental.pallas import tpu as pltpu

F32 = jnp.float32
_MXU = jnp.bfloat16
_HI = lax.Precision.HIGHEST

N_DEV = 8
DEPTH = 4
D_MODEL = 1024
HEAD_DIM = 64
D_FOX = 512
FOX_HEADS = 8
D_S5 = 256
S5_GROUPS = 16
S5_GROUP_CH = 16
S5_STATE = 64
S5_CH = S5_GROUPS * S5_STATE
D_RET = 256
RET_HEADS = 4
CHUNK = 64
ROPE_BASE = 10000.0
EPS = 1e-6
D_IN = 3592
D_INP = 3712
W_SHARD = D_IN // N_DEV
O_GATE, O_FQ, O_FK, O_FV, O_SU, O_RQ, O_RK, O_RV, O_FL = 0, 1024, 1536, 2048, 2560, 2816, 3072, 3328, 3584

ADAM_LR, ADAM_B1, ADAM_B2, ADAM_EPS, ADAM_WD, ADAM_STEP = 0.001, 0.9, 0.999, 1e-08, 0.01, 10

TM = 256
TMB = 512
TQ = 512
TS = 512
NEG = -1e30
VMEM_BIG = 56 * 1024 * 1024


def _pallas(body, **kw):
    return pl.pallas_call(body, **kw)


def _whole(shape):
    n = len(shape)
    return pl.BlockSpec(shape, lambda *_: (0,) * n)


def _rows(tm, width, col=0):
    return pl.BlockSpec((tm, width), lambda i: (i, col))


def _of_layer(param):
    a, l = param
    return a, pl.BlockSpec((None,) + a.shape[1:], lambda *_: (l,) + (0,) * (a.ndim - 1))


def _dot(a, b, dims=(((1,), (0,)), ((), ()))):
    return lax.dot_general(a.astype(_MXU), b.astype(_MXU), dims, preferred_element_type=F32)


_NT = (((1,), (1,)), ((), ()))
_TN = (((0,), (0,)), ((), ()))


def _norm_inproj(x, g, w):
    L = x.shape[0]

    def body(x_ref, g_ref, w_ref, p_ref, h_ref):
        xv = x_ref[...]
        r = lax.rsqrt(jnp.mean(xv * xv, axis=-1, keepdims=True) + EPS)
        h = (xv * r * g_ref[...]).astype(_MXU)
        h_ref[...] = h
        p_ref[...] = _dot(h, w_ref[...])

    tm = min(TMB, L)
    g, g_spec = _of_layer(g)
    return _pallas(body, out_shape=(jax.ShapeDtypeStruct((L, D_INP), F32), jax.ShapeDtypeStruct((L, D_MODEL), _MXU)),
                   grid=(L // tm,),
                   in_specs=[_rows(tm, D_MODEL), g_spec,
                             pl.BlockSpec((D_MODEL, D_INP), lambda i: (0, 0), pipeline_mode=pl.Buffered(1))],
                   out_specs=(_rows(tm, D_INP), _rows(tm, D_MODEL)), name="norm_inproj",
                   compiler_params=pltpu.CompilerParams(vmem_limit_bytes=VMEM_BIG))(x, g, w)


def _rms_bwd(xv, g, dh):
    r = lax.rsqrt(jnp.mean(xv * xv, axis=-1, keepdims=True) + EPS)
    xh = xv * r
    dg = jnp.sum(dh * xh, axis=0, keepdims=True)
    dxh = dh * g
    dx = r * (dxh - xh * jnp.mean(dxh * xh, axis=-1, keepdims=True))
    return dx, dg


def _inproj_bwd(pieces, w, x, g, dres, h):
    L = x.shape[0]
    n = len(pieces)
    nb = L // TM

    def body(*refs):
        w_ref, x_ref, g_ref, dr_ref, h_ref, dx_ref, dg_ref, dw_ref, acc_sc = refs[n:]
        step = pl.program_id(0)

        @pl.when(step == 0)
        def _():
            acc_sc[...] = jnp.zeros((D_MODEL, D_INP), F32)
            dg_ref[...] = jnp.zeros((1, D_MODEL), F32)

        hv = h_ref[...]
        dh = jnp.zeros((TM, D_MODEL), F32)
        off = 0
        for r in refs[:n]:
            cols = slice(off, off + r.shape[1])
            off += r.shape[1]
            piece = r[...].astype(_MXU)
            dh = dh + _dot(piece, w_ref[:, cols], _NT)
            acc_sc[:, cols] += _dot(hv, piece, _TN)
        dx, dg = _rms_bwd(x_ref[...], g_ref[...], dh)
        dx_ref[...] = dx + dr_ref[...]
        dg_ref[...] += dg

        @pl.when(step == nb - 1)
        def _():
            dw_ref[...] = acc_sc[...].astype(_WIRE)

    resident = pl.BlockSpec((D_MODEL, D_INP), lambda i: (0, 0), pipeline_mode=pl.Buffered(1))
    g, g_spec = _of_layer(g)
    return _pallas(body, out_shape=(jax.ShapeDtypeStruct((L, D_MODEL), F32), jax.ShapeDtypeStruct((1, D_MODEL), F32),
                                    jax.ShapeDtypeStruct((D_MODEL, D_INP), _WIRE)),
                   grid=(nb,),
                   in_specs=[_rows(TM, p.shape[1]) for p in pieces]
                   + [resident, _rows(TM, D_MODEL), g_spec, _rows(TM, D_MODEL), _rows(TM, D_MODEL)],
                   out_specs=(_rows(TM, D_MODEL), _whole((1, D_MODEL)), resident),
                   scratch_shapes=[pltpu.VMEM((D_MODEL, D_INP), F32)], name="inproj_bwd",
                   compiler_params=pltpu.CompilerParams(vmem_limit_bytes=VMEM_BIG))(*pieces, w, x, g, dres, h)


PAIR = 2 * HEAD_DIM
N_AUX = 3


def _own(shape, h):
    return lax.broadcasted_iota(jnp.int32, shape, len(shape) - 1) // HEAD_DIM == h


def _split(x):
    parts = []
    for _ in range(N_AUX):
        part = x.astype(_MXU)
        parts.append(part)
        x = x - part.astype(F32)
    return parts


def _exact_dot(a, b, exact):
    if exact == "b":
        return sum(_dot(part, b) for part in _split(a))
    return sum(_dot(a, part) for part in _split(b))


def _tri(n, lower):
    r = lax.broadcasted_iota(jnp.int32, (n, n), 0)
    c = lax.broadcasted_iota(jnp.int32, (n, n), 1)
    return jnp.where(r >= c if lower else r <= c, 1.0, 0.0).astype(F32)


def _fox_cumsum(proj, b):
    L = proj.shape[0]

    def body(fl_ref, b_ref, c_ref, carry_sc):
        @pl.when(pl.program_id(0) == 0)
        def _():
            carry_sc[...] = jnp.zeros((1, PAIR), F32)

        lane = lax.broadcasted_iota(jnp.int32, (TM, PAIR), 1)
        lf = jnp.where(lane < FOX_HEADS, jax.nn.log_sigmoid(fl_ref[...] + b_ref[...]), 0.0)
        cs = _exact_dot(_tri(TM, True), lf, "a") + carry_sc[...]
        c_ref[...] = cs
        carry_sc[...] = cs[TM - 1:TM, :]

    b, b_spec = _of_layer(b)
    return _pallas(body, out_shape=jax.ShapeDtypeStruct((L, PAIR), F32), grid=(L // TM,),
                   in_specs=[_rows(TM, PAIR, O_FL // PAIR), b_spec], out_specs=_rows(TM, PAIR),
                   scratch_shapes=[pltpu.VMEM((1, PAIR), F32)], name="fox_cumsum")(proj, b)


def _fox_prep(proj, c):
    L = proj.shape[0]

    def body(q_ref, k_ref, v_ref, c_ref, qa_ref, ka_ref, kat_ref, vt_ref):
        lane = lax.broadcasted_iota(jnp.int32, (TM, PAIR), 1)
        cv = c_ref[...]
        for p in range(FOX_HEADS // 2):
            cols = slice(PAIR * p, PAIR * (p + 1))
            q2, k2 = q_ref[:, cols], k_ref[:, cols]
            vt_ref[p] = v_ref[:, cols].T.astype(_MXU)
            for e in range(2):
                h = 2 * p + e
                own = lane // HEAD_DIM == e
                a = lane - (HEAD_DIM if e == 0 else 0)
                rest = jnp.broadcast_to(cv[:, h:h + 1], (TM, PAIR))
                aux_q = jnp.where((a >= N_AUX) & (a < 2 * N_AUX), 1.0, 0.0)
                aux_k = jnp.where((a >= 0) & (a < N_AUX), 1.0, 0.0)
                for n in range(N_AUX):
                    part = rest.astype(_MXU).astype(F32)
                    rest = rest - part
                    aux_q = jnp.where(a == n, part, aux_q)
                    aux_k = jnp.where(a == N_AUX + n, -part, aux_k)
                ka = jnp.where(own, k2, aux_k)
                qa_ref[h] = jnp.where(own, q2 * (1.0 / math.sqrt(HEAD_DIM)), aux_q).astype(_MXU)
                ka_ref[h] = ka.astype(_MXU)
                kat_ref[h] = ka.T.astype(_MXU)

    hl = jax.ShapeDtypeStruct((FOX_HEADS, L, PAIR), _MXU)
    nat = lambda o: _rows(TM, D_FOX, o // D_FOX)
    rows = pl.BlockSpec((FOX_HEADS, TM, PAIR), lambda i: (0, i, 0))
    return _pallas(
        body, out_shape=(hl, hl, jax.ShapeDtypeStruct((FOX_HEADS, PAIR, L), _MXU),
                         jax.ShapeDtypeStruct((FOX_HEADS // 2, PAIR, L), _MXU)),
        grid=(L // TM,), in_specs=[nat(O_FQ), nat(O_FK), nat(O_FV), _rows(TM, PAIR)],
        out_specs=(rows, rows, pl.BlockSpec((FOX_HEADS, PAIR, TM), lambda i: (0, 0, i)),
                   pl.BlockSpec((FOX_HEADS // 2, PAIR, TM), lambda i: (0, 0, i))),
        name="fox_prep")(proj, proj, proj, c)


def _key_le_query(tq):
    return lax.broadcasted_iota(jnp.int32, (tq, tq), 0) <= lax.broadcasted_iota(jnp.int32, (tq, tq), 1)


def _grid_ends(n0, n1):
    first = lambda: (pl.program_id(0) == 0) & (pl.program_id(1) == 0)
    last = lambda: (pl.program_id(0) == n0 - 1) & (pl.program_id(1) == n1 - 1)
    return first, last


def _fox_fwd(qa, ka, vt, ride=None):
    H, L, _ = qa.shape
    tq = min(TQ, L)
    nq = L // tq

    def body(qa_ref, ka_ref, vt_ref, o_ref, lse_ref, m_sc, l_sc, acc_sc):
        i = pl.program_id(1)
        m_sc[...] = jnp.full((2, 1, tq), NEG, F32)
        l_sc[...] = jnp.zeros((2, 1, tq), F32)
        acc_sc[...] = jnp.zeros((2, HEAD_DIM, tq), F32)

        def block(j, nk, masked):
            keys = pl.ds(pl.multiple_of(j * tq, tq), nk * tq)
            vt_blk = vt_ref[:, keys]
            sts = [_dot(ka_ref[e, keys, :], qa_ref[e], _NT) for e in range(2)]
            pts, alphas = [], []
            for e in range(2):
                st = jnp.where(_key_le_query(tq), sts[e], NEG) if masked else sts[e]
                m_prev = m_sc[e]
                m_new = jnp.maximum(m_prev, jnp.max(st, axis=0, keepdims=True))
                alphas.append(jnp.exp(m_prev - m_new))
                pt = jnp.exp(st - m_new)
                l_sc[e] = alphas[e] * l_sc[e] + jnp.sum(pt, axis=0, keepdims=True)
                m_sc[e] = m_new
                pts.append(pt.astype(_MXU))
            for e in range(2):
                acc_sc[e] = alphas[e] * acc_sc[e] + _dot(vt_blk[HEAD_DIM * e:HEAD_DIM * (e + 1)], pts[e])

        def two_blocks(jj, carry):
            block(2 * jj, 2, False)
            return carry

        lax.fori_loop(0, i // 2, two_blocks, 0)

        @pl.when(i % 2 == 1)
        def _():
            block(i - 1, 1, False)

        block(i, 1, True)
        o_ref[...] = jnp.concatenate([acc_sc[0] / l_sc[0], acc_sc[1] / l_sc[1]], axis=0).T
        for e in range(2):
            lse_ref[e] = m_sc[e] + jnp.log(l_sc[e])

    body, ex_in, ex_specs, ex_out, ex_sems = _riding(body, 3, 2, ride, *_grid_ends(H // 2, nq))
    res = _pallas(
        body, out_shape=(jax.ShapeDtypeStruct((L, D_FOX), F32), jax.ShapeDtypeStruct((H, 1, L), F32), *ex_out),
        grid=(H // 2, nq),
        in_specs=[pl.BlockSpec((2, tq, PAIR), lambda p, i: (p, i, 0)), pl.BlockSpec((2, L, PAIR), lambda p, i: (p, 0, 0)),
                  pl.BlockSpec((None, PAIR, L), lambda p, i: (p, 0, 0)), *ex_specs],
        out_specs=(pl.BlockSpec((tq, PAIR), lambda p, i: (i, p)), pl.BlockSpec((2, 1, tq), lambda p, i: (p, 0, i)),
                   *ex_specs),
        scratch_shapes=[pltpu.VMEM((2, 1, tq), F32), pltpu.VMEM((2, 1, tq), F32), pltpu.VMEM((2, HEAD_DIM, tq), F32), *ex_sems],
        name="fox_fwd" if ride is None else "fox_fwd_gather")(qa, ka, vt, *ex_in)
    return res[0], res[1], list(res[2:])


def _fox_bwd(qa, ka, kat, proj, do, o, lse, ride=None):
    H, L, _ = qa.shape
    tq = min(TQ, L)
    nq = L // tq

    def body(qa_ref, ka_ref, kat_ref, v_ref, do_ref, o_ref, lse_ref, dqt_ref, dk_ref, dv_ref, delta_sc, dk_sc, dv_sc):
        j = pl.program_id(1)

        @pl.when(j == 0)
        def _():
            head_rows = (lax.broadcasted_iota(jnp.int32, (8, PAIR), 1) // HEAD_DIM
                         == lax.broadcasted_iota(jnp.int32, (8, PAIR), 0)).astype(F32)
            delta_sc[...] = lax.dot_general(head_rows, do_ref[...] * o_ref[...], _NT, precision=_HI,
                                            preferred_element_type=F32)
            dqt_ref[...] = jnp.zeros((2, PAIR, L), F32)

        dk_sc[...] = jnp.zeros((2, tq, PAIR), F32)
        dv_sc[...] = jnp.zeros((tq, PAIR), F32)
        vb = v_ref[...]

        def block(i, masked):
            qs = pl.ds(pl.multiple_of(i * tq, tq), tq)
            dob = do_ref[qs, :]
            for e in range(2):
                own = _own((tq, PAIR), e)
                qh = qa_ref[e, qs, :]
                pt = jnp.exp(_dot(ka_ref[e], qh, _NT) - lse_ref[e, :, qs])
                if masked:
                    pt = jnp.where(_key_le_query(tq), pt, 0.0)
                dv_sc[...] += _dot(pt, jnp.where(own, dob, 0.0))
                dpt = _dot(jnp.where(own, vb, 0.0), dob, _NT)
                ds = (pt * (dpt - delta_sc[e:e + 1, qs])).astype(_MXU)
                dk_sc[e] += _dot(ds, qh)
                dqt_ref[e, :, qs] += _dot(kat_ref[e], ds)

        def off_diagonal(i, carry):
            block(i, False)
            return carry

        block(j, True)
        lax.fori_loop(j + 1, nq, off_diagonal, 0)
        dk_ref[...] = dk_sc[...]
        dv_ref[...] = dv_sc[...]

    nat = pl.BlockSpec((L, PAIR), lambda p, j: (0, p))
    body, ex_in, ex_specs, ex_out, ex_sems = _riding(body, 7, 3, ride, *_grid_ends(H // 2, nq))
    res = _pallas(
        body, out_shape=(jax.ShapeDtypeStruct((H, PAIR, L), F32), jax.ShapeDtypeStruct((H, L, PAIR), F32),
                         jax.ShapeDtypeStruct((L, D_FOX), F32), *ex_out),
        grid=(H // 2, nq),
        in_specs=[pl.BlockSpec((2, L, PAIR), lambda p, j: (p, 0, 0)), pl.BlockSpec((2, tq, PAIR), lambda p, j: (p, j, 0)),
                  pl.BlockSpec((2, PAIR, tq), lambda p, j: (p, 0, j)),
                  pl.BlockSpec((tq, PAIR), lambda p, j: (j, O_FV // PAIR + p)), nat, nat,
                  pl.BlockSpec((2, 1, L), lambda p, j: (p, 0, 0)), *ex_specs],
        out_specs=(pl.BlockSpec((2, PAIR, L), lambda p, j: (p, 0, 0)), pl.BlockSpec((2, tq, PAIR), lambda p, j: (p, j, 0)),
                   pl.BlockSpec((tq, PAIR), lambda p, j: (j, p)), *ex_specs),
        scratch_shapes=[pltpu.VMEM((8, L), F32), pltpu.VMEM((2, tq, PAIR), F32), pltpu.VMEM((tq, PAIR), F32), *ex_sems],
        name="fox_bwd" if ride is None else "fox_bwd_exchange",
        compiler_params=pltpu.CompilerParams(vmem_limit_bytes=VMEM_BIG))(qa, ka, kat, proj, do, o, lse, *ex_in)
    return res[0], res[1], res[2], list(res[3:])


def _fox_post_bwd(dqt, dkraw, proj, b):
    L = proj.shape[0]
    nb = L // TM

    def body(dqt_ref, dkr_ref, fl_ref, b_ref, dq_ref, dk_ref, dfl_ref, db_ref, carry_sc):
        first = pl.program_id(0) == 0

        @pl.when(first)
        def _():
            carry_sc[...] = jnp.zeros((1, PAIR), F32)

        lane = lax.broadcasted_iota(jnp.int32, (TM, PAIR), 1)
        rr = lax.broadcasted_iota(jnp.int32, (PAIR, PAIR), 0)
        cc = lax.broadcasted_iota(jnp.int32, (PAIR, PAIR), 1)
        dc = jnp.zeros((TM, PAIR), F32)
        for p in range(FOX_HEADS // 2):
            cols = slice(PAIR * p, PAIR * (p + 1))
            dqs = [dqt_ref[2 * p + e].T for e in range(2)]
            dks = [dkr_ref[2 * p + e] for e in range(2)]
            dq_ref[:, cols] = jnp.where(lane < HEAD_DIM, dqs[0], dqs[1]) * (1.0 / math.sqrt(HEAD_DIM))
            dk_ref[:, cols] = jnp.where(lane < HEAD_DIM, dks[0], dks[1])
            sums = jnp.zeros((TM, PAIR), F32)
            place = jnp.zeros((PAIR, PAIR), F32)
            for e in range(2):
                base = HEAD_DIM if e == 0 else 0
                sums = jnp.where(lane == base, dqs[e], jnp.where(lane == base + N_AUX, -dks[e], sums))
                place = jnp.where(((rr == base) | (rr == base + N_AUX)) & (cc == 2 * p + e), 1.0, place)
            dc = dc + _exact_dot(sums, place, "b")
        rs = _exact_dot(_tri(TM, False), dc, "a") + carry_sc[...]
        carry_sc[...] = rs[0:1, :]
        dfl = jnp.where(lane < FOX_HEADS, rs * jax.nn.sigmoid(-(fl_ref[...] + b_ref[...])), 0.0)
        dfl_ref[...] = dfl
        db = jnp.sum(dfl, axis=0, keepdims=True)

        @pl.when(first)
        def _():
            db_ref[...] = db

        @pl.when(jnp.logical_not(first))
        def _():
            db_ref[...] += db

    rev = lambda i: nb - 1 - i
    b, b_spec = _of_layer(b)
    nat = pl.BlockSpec((TM, D_FOX), lambda i: (rev(i), 0))
    return _pallas(
        body, out_shape=(jax.ShapeDtypeStruct((L, D_FOX), F32),) * 2
        + (jax.ShapeDtypeStruct((L, PAIR), F32), jax.ShapeDtypeStruct((1, PAIR), F32)),
        grid=(nb,),
        in_specs=[pl.BlockSpec((FOX_HEADS, PAIR, TM), lambda i: (0, 0, rev(i))),
                  pl.BlockSpec((FOX_HEADS, TM, PAIR), lambda i: (0, rev(i), 0)),
                  pl.BlockSpec((TM, PAIR), lambda i: (rev(i), O_FL // PAIR)), b_spec],
        out_specs=(nat, nat, pl.BlockSpec((TM, PAIR), lambda i: (rev(i), 0)), _whole((1, PAIR))),
        scratch_shapes=[pltpu.VMEM((1, PAIR), F32)], name="fox_post_bwd")(dqt, dkraw, proj, b)


def _s5_expand():
    r = lax.broadcasted_iota(jnp.int32, (S5_STATE, S5_STATE * S5_GROUP_CH), 0)
    c = lax.broadcasted_iota(jnp.int32, (S5_STATE, S5_STATE * S5_GROUP_CH), 1)
    return jnp.where(c // S5_GROUP_CH == r, 1.0, 0.0).astype(F32)


def _s5_disc_math(ar, ai, ldt, br, bi):
    dt = jnp.exp(ldt)
    mag = jnp.exp(ar * dt)
    lr = mag * jnp.cos(ai * dt)
    li = mag * jnp.sin(ai * dt)
    den = ar * ar + ai * ai
    fr = ((lr - 1.0) * ar + li * ai) / den
    fi = (li * ar - (lr - 1.0) * ai) / den
    e = _s5_expand()
    fre = jnp.dot(fr, e, precision=_HI, preferred_element_type=F32)
    fie = jnp.dot(fi, e, precision=_HI, preferred_element_type=F32)
    return lr, li, fre * br - fie * bi, fre * bi + fie * br


def _layer_blocks(arrs):
    return [pl.BlockSpec((None,) + a.shape[1:], lambda l: (l, 0, 0)) for a in arrs]


def _s5_disc(ar, ai, ldt, br, bi):
    def body(ar_ref, ai_ref, ldt_ref, br_ref, bi_ref, lr_ref, li_ref, bbr_ref, bbi_ref):
        lr, li, bbr, bbi = _s5_disc_math(ar_ref[...], ai_ref[...], ldt_ref[...], br_ref[...], bi_ref[...])
        lr_ref[...] = lr
        li_ref[...] = li
        bbr_ref[...] = bbr
        bbi_ref[...] = bbi

    ins = (ar, ai, ldt, br, bi)
    outs = (ar, ai, br, bi)
    return _pallas(body, out_shape=tuple(jax.ShapeDtypeStruct(a.shape, F32) for a in outs), grid=(DEPTH,),
                   in_specs=_layer_blocks(ins), out_specs=tuple(_layer_blocks(outs)), name="s5_disc")(*ins)


def _s5_disc_bwd(ar, ai, ldt, br, bi, dlr, dli, dbbr, dbbi):
    def body(ar_ref, ai_ref, ldt_ref, br_ref, bi_ref, dlr_ref, dli_ref, dbbr_ref, dbbi_ref,
             dar_ref, dai_ref, dldt_ref, dbr_ref, dbi_ref):
        _, vjp = jax.vjp(_s5_disc_math, ar_ref[...], ai_ref[...], ldt_ref[...], br_ref[...], bi_ref[...])
        dar, dai, dldt, dbr, dbi = vjp((dlr_ref[...], dli_ref[...], dbbr_ref[...], dbbi_ref[...]))
        dar_ref[...] = dar
        dai_ref[...] = dai
        dldt_ref[...] = dldt
        dbr_ref[...] = dbr
        dbi_ref[...] = dbi

    ins = (ar, ai, ldt, br, bi, dlr, dli, dbbr, dbbi)
    outs = (ar, ai, ldt, br, bi)
    return _pallas(body, out_shape=tuple(jax.ShapeDtypeStruct(a.shape, F32) for a in outs), grid=(DEPTH,),
                   in_specs=_layer_blocks(ins), out_specs=tuple(_layer_blocks(outs)), name="s5_disc_bwd")(*ins)


SLAB = 2 * S5_CH // 128
PITCH = 24


def _slab_rows(s, ts):
    return pl.ds(s, ts, stride=PITCH)


def _slab_pair(ref, s, ts):
    return jnp.concatenate([ref[_slab_rows(s, ts), :].astype(_MXU), ref[_slab_rows(s + 1, ts), :].astype(_MXU)], axis=-1)


def _s5_fwd(proj, wb, wc, lam, d, w_glu):
    L = proj.shape[0]
    ts = min(TS, L)

    def body(u_ref, wb_ref, wc_ref, lam_ref, d_ref, wg_ref, xs_ref, ypre_ref, ys_ref, b_sc, c_sc):
        @pl.when(pl.program_id(0) == 0)
        def _():
            c_sc[...] = jnp.zeros((SLAB, 128), F32)

        u = u_ref[...]
        ub = u.astype(_MXU)
        for s in range(0, SLAB, 2):
            b2 = _dot(ub, wb_ref[:, 128 * s:128 * (s + 2)])
            b_sc[_slab_rows(s, ts), :] = b2[:, :128]
            b_sc[_slab_rows(s + 1, ts), :] = b2[:, 128:]
        lr, li = lam_ref[0:8, :], lam_ref[8:16, :]

        def step(t, carry):
            xr, xi = carry
            row = pl.multiple_of(t * PITCH, 8)
            nr = lr * xr - li * xi + b_sc[pl.ds(row, 8), :]
            ni = lr * xi + li * xr + b_sc[pl.ds(row + 8, 8), :]
            xs_ref[pl.ds(row, 8), :] = nr
            xs_ref[pl.ds(row + 8, 8), :] = ni
            return nr, ni

        xr, xi = lax.fori_loop(0, ts, step, (c_sc[0:8, :], c_sc[8:16, :]), unroll=8)
        c_sc[0:8, :] = xr
        c_sc[8:16, :] = xi
        y = jnp.zeros((ts, D_S5), F32)
        for s in range(0, SLAB, 2):
            y = y + _dot(_slab_pair(xs_ref, s, ts), wc_ref[128 * s:128 * (s + 2), :])
        ypre_ref[...] = y
        y1 = jax.nn.gelu(y + d_ref[...] * u)
        ys_ref[...] = y1 * jax.nn.sigmoid(_dot(y1, wg_ref[...]))

    row = _rows(ts, D_S5)
    slabs = pl.BlockSpec((ts * PITCH, 128), lambda n: (n, 0))
    (wb, wb_spec), (wc, wc_spec), (lam, lam_spec), (d, d_spec) = (_of_layer(a) for a in (wb, wc, lam, d))
    return _pallas(
        body, out_shape=(jax.ShapeDtypeStruct((L * PITCH, 128), F32), jax.ShapeDtypeStruct((L, D_S5), F32),
                         jax.ShapeDtypeStruct((L, D_S5), F32)),
        grid=(L // ts,),
        in_specs=[_rows(ts, D_S5, O_SU // D_S5), wb_spec, wc_spec, lam_spec, d_spec, _whole((D_S5, D_S5))],
        out_specs=(slabs, row, row),
        scratch_shapes=[pltpu.VMEM((ts * PITCH, 128), F32), pltpu.VMEM((SLAB, 128), F32)], name="s5_fwd")(
            proj, wb, wc, lam, d, w_glu)


def _s5_bwd(proj, ypre, dys, xs, wb, wc, lam, d, w_glu):
    L = proj.shape[0]
    ts = min(TS, L)
    nb = L // ts

    def body(u_ref, y_ref, dys_ref, xs_ref, xp_ref, wb_ref, wc_ref, lam_ref, d_ref, wg_ref,
             du_ref, dwb_ref, dwc_ref, dlam_ref, dd_ref, dwg_ref, dx_sc, g_sc, c_sc):
        n = pl.program_id(0)

        @pl.when(n == 0)
        def _():
            c_sc[...] = jnp.zeros((SLAB, 128), F32)
            dlam_ref[...] = jnp.zeros((SLAB, 128), F32)
            dwb_ref[...] = jnp.zeros((D_S5, 2 * S5_CH), F32)
            dwc_ref[...] = jnp.zeros((2 * S5_CH, D_S5), F32)
            dd_ref[...] = jnp.zeros((1, D_S5), F32)
            dwg_ref[...] = jnp.zeros((D_S5, D_S5), F32)

        u, dv, dout = u_ref[...], d_ref[...], dys_ref[...]
        y1, gelu_vjp = jax.vjp(jax.nn.gelu, y_ref[...] + dv * u)
        sg = jax.nn.sigmoid(_dot(y1, wg_ref[...]))
        dz = dout * y1 * sg * (1.0 - sg)
        dy, = gelu_vjp(dout * sg + _dot(dz, wg_ref[...], _NT))
        dd_ref[...] += jnp.sum(dy * u, axis=0, keepdims=True)
        dwg_ref[...] += _dot(y1, dz, _TN)
        dyb = dy.astype(_MXU)
        for s in range(0, SLAB, 2):
            cols = slice(128 * s, 128 * (s + 2))
            dx2 = _dot(dyb, wc_ref[cols, :], _NT)
            dx_sc[_slab_rows(s, ts), :] = dx2[:, :128]
            dx_sc[_slab_rows(s + 1, ts), :] = dx2[:, 128:]
            dwc_ref[cols, :] += _dot(_slab_pair(xs_ref, s, ts), dyb, _TN)
        lr, li = lam_ref[0:8, :], lam_ref[8:16, :]

        def adjoint(row, pr, pi, carry):
            gr, gi, ar, ai = carry
            nr = dx_sc[pl.ds(row, 8), :] + lr * gr + li * gi
            ni = dx_sc[pl.ds(row + 8, 8), :] - li * gr + lr * gi
            g_sc[pl.ds(row, 8), :] = nr
            g_sc[pl.ds(row + 8, 8), :] = ni
            return nr, ni, ar + nr * pr + ni * pi, ai - nr * pi + ni * pr

        def step(k, carry):
            row = pl.multiple_of((ts - 1 - k) * PITCH, 8)
            prev = pl.multiple_of((ts - 2 - k) * PITCH, 8)
            return adjoint(row, xs_ref[pl.ds(prev, 8), :], xs_ref[pl.ds(prev + 8, 8), :], carry)

        z = jnp.zeros((8, 128), F32)
        carry = lax.fori_loop(0, ts - 1, step, (c_sc[0:8, :], c_sc[8:16, :], z, z), unroll=8)
        has_prev = jnp.where(n == nb - 1, 0.0, 1.0)
        gr, gi, ar, ai = adjoint(0, xp_ref[0:8, :] * has_prev, xp_ref[8:16, :] * has_prev, carry)
        c_sc[0:8, :] = gr
        c_sc[8:16, :] = gi
        dlam_ref[0:8, :] += ar
        dlam_ref[8:16, :] += ai
        ub = u.astype(_MXU)
        du = dy * dv
        for s in range(0, SLAB, 2):
            cols = slice(128 * s, 128 * (s + 2))
            gs = _slab_pair(g_sc, s, ts)
            du = du + _dot(gs, wb_ref[:, cols], _NT)
            dwb_ref[:, cols] += _dot(ub, gs, _TN)
        du_ref[...] = du

    blk = lambda n: nb - 1 - n
    row = pl.BlockSpec((ts, D_S5), lambda n: (blk(n), 0))
    (wb, wb_spec), (wc, wc_spec), (lam, lam_spec), (d, d_spec) = (_of_layer(a) for a in (wb, wc, lam, d))
    return _pallas(
        body, out_shape=(jax.ShapeDtypeStruct((L, D_S5), F32), jax.ShapeDtypeStruct((D_S5, 2 * S5_CH), F32),
                         jax.ShapeDtypeStruct((2 * S5_CH, D_S5), F32), jax.ShapeDtypeStruct((SLAB, 128), F32),
                         jax.ShapeDtypeStruct((1, D_S5), F32), jax.ShapeDtypeStruct((D_S5, D_S5), F32)),
        grid=(nb,),
        in_specs=[pl.BlockSpec((ts, D_S5), lambda n: (blk(n), O_SU // D_S5)), row, row,
                  pl.BlockSpec((ts * PITCH, 128), lambda n: (blk(n), 0)),
                  pl.BlockSpec((PITCH, 128), lambda n: (jnp.maximum(blk(n) * ts - 1, 0), 0)),
                  wb_spec, wc_spec, lam_spec, d_spec, _whole((D_S5, D_S5))],
        out_specs=(row, _whole((D_S5, 2 * S5_CH)), _whole((2 * S5_CH, D_S5)), _whole((SLAB, 128)), _whole((1, D_S5)),
                   _whole((D_S5, D_S5))),
        scratch_shapes=[pltpu.VMEM((ts * PITCH, 128), F32), pltpu.VMEM((ts * PITCH, 128), F32), pltpu.VMEM((SLAB, 128), F32)],
        name="s5_bwd", compiler_params=pltpu.CompilerParams(vmem_limit_bytes=VMEM_BIG))(
            proj, ypre, dys, xs, xs, wb, wc, lam, d, w_glu)


def _rot(z, cos, sin):
    lane = lax.broadcasted_iota(jnp.int32, z.shape, 1)
    zs = z * sin
    half = HEAD_DIM // 2
    return z * cos + jnp.where(lane % HEAD_DIM < half, pltpu.roll(zs, PAIR - half, 1), pltpu.roll(zs, half, 1))


def _head_avg():
    r = lax.broadcasted_iota(jnp.int32, (PAIR, PAIR), 0) // HEAD_DIM
    c = lax.broadcasted_iota(jnp.int32, (PAIR, PAIR), 1) // HEAD_DIM
    return jnp.where(r == c, 1.0 / HEAD_DIM, 0.0).astype(F32)


def _ret_tables(tq):
    lg = jnp.log1p(-(2.0 ** (-5.0 - jnp.arange(RET_HEADS, dtype=F32))))
    scale = 1.0 / math.sqrt(HEAD_DIM)
    pos = jnp.arange(tq)
    n = pos.astype(F32)
    dist = jnp.abs(n[:, None] - n[None, :])
    ok = (pos[None, :] // CHUNK) <= (pos[:, None] // CHUNK)
    w = jnp.where(ok[None], scale * jnp.exp(lg[:, None, None] * dist[None]), 0.0)
    lgl = jnp.repeat(lg, HEAD_DIM)
    dq_tab = scale * jnp.exp(lgl[None, :] * (n[:, None] + 1.0))
    dk_tab = jnp.exp(lgl[None, :] * (tq - 1.0 - n[:, None]))
    blk = jnp.arange(PAIR) // HEAD_DIM
    bd = (blk[:, None] == blk[None, :]).astype(F32)
    gbd = bd[None] * jnp.exp(lgl.reshape(RET_HEADS // 2, PAIR)[:, :, None] * tq)
    return dict(w=w, wt=w.transpose(0, 2, 1), dq=dq_tab, dk=dk_tab, gbd=gbd, bd=bd)


def _ret_specs(tq, nq, rev, layer):
    blk = (lambda i: nq - 1 - i) if rev else (lambda i: i)
    col = lambda o: pl.BlockSpec((tq, PAIR), lambda p, i: (blk(i), o // PAIR + p))
    return dict(
        rq=col(O_RQ), rk=col(O_RK), rv=col(O_RV), nat=col(0),
        w=pl.BlockSpec((2, tq, tq), lambda p, i: (p, 0, 0)), tab=pl.BlockSpec((tq, PAIR), lambda p, i: (0, p)),
        gbd=pl.BlockSpec((None, PAIR, PAIR), lambda p, i: (p, 0, 0)), bd=pl.BlockSpec((PAIR, PAIR), lambda p, i: (0, 0)),
        gn=pl.BlockSpec((None, 1, PAIR), lambda p, i: (layer, 0, p)), dgn=pl.BlockSpec((1, PAIR), lambda p, i: (0, p)),
        st=pl.BlockSpec((None, None, PAIR, PAIR), lambda p, i: (p, blk(i), 0, 0)))


def _ret_fwd(proj, cos_t, sin_t, tabs, gn):
    L = proj.shape[0]
    tq = tabs["w"].shape[1]
    nq = L // tq

    def body(rq_ref, rk_ref, rv_ref, cos_ref, sin_ref, w_ref, dqt_ref, dkt_ref, gbd_ref, bd_ref, gn_ref,
             o_ref, y_ref, st_ref, s_sc):
        @pl.when(pl.program_id(1) == 0)
        def _():
            s_sc[...] = jnp.zeros((PAIR, PAIR), F32)

        state = s_sc[...]
        st_ref[...] = state
        cos, sin = cos_ref[...], sin_ref[...]
        q2, k2, v2 = _rot(rq_ref[...], cos, sin), _rot(rk_ref[...], cos, sin), rv_ref[...]
        owns = [_own((tq, PAIR), h) for h in range(2)]
        scores = [_dot(jnp.where(owns[h], q2, 0.0), k2, _NT) for h in range(2)]
        o = _dot(q2 * dqt_ref[...], state)
        for h in range(2):
            o = o + _dot(scores[h] * w_ref[h], jnp.where(owns[h], v2, 0.0))
        s_sc[...] = gbd_ref[...] * state + bd_ref[...] * _dot(k2 * dkt_ref[...], v2, _TN)
        o_ref[...] = o
        avg = _head_avg()
        oc = o - _exact_dot(o, avg, "b")
        y_ref[...] = oc * lax.rsqrt(_exact_dot(oc * oc, avg, "b") + EPS) * gn_ref[...]

    gn, layer = gn
    sp = _ret_specs(tq, nq, False, layer)
    nat = jax.ShapeDtypeStruct((L, D_RET), F32)
    return _pallas(
        body, out_shape=(nat, nat, jax.ShapeDtypeStruct((RET_HEADS // 2, nq, PAIR, PAIR), F32)), grid=(RET_HEADS // 2, nq),
        in_specs=[sp["rq"], sp["rk"], sp["rv"], sp["nat"], sp["nat"], sp["w"], sp["tab"], sp["tab"], sp["gbd"], sp["bd"],
                  sp["gn"]],
        out_specs=(sp["nat"], sp["nat"], sp["st"]), scratch_shapes=[pltpu.VMEM((PAIR, PAIR), F32)],
        name="ret_fwd")(proj, proj, proj, cos_t, sin_t, tabs["w"], tabs["dq"], tabs["dk"], tabs["gbd"], tabs["bd"], gn)


def _ret_bwd(proj, cos_t, sin_t, tabs, gn, o_pre, dy, states):
    L = proj.shape[0]
    tq = tabs["w"].shape[1]
    nq = L // tq

    def body(rq_ref, rk_ref, rv_ref, cos_ref, sin_ref, w_ref, wt_ref, dqt_ref, dkt_ref, gbd_ref, bd_ref, gn_ref,
             o_ref, dy_ref, st_ref, drq_ref, drk_ref, drv_ref, dgn_ref, g_sc):
        first = pl.program_id(1) == 0

        @pl.when(first)
        def _():
            g_sc[...] = jnp.zeros((PAIR, PAIR), F32)

        cos, sin = cos_ref[...], sin_ref[...]
        q2, k2, v2 = _rot(rq_ref[...], cos, sin), _rot(rk_ref[...], cos, sin), rv_ref[...]
        avg = _head_avg()
        ov, dyv = o_ref[...], dy_ref[...]
        oc = ov - _exact_dot(ov, avg, "b")
        r = lax.rsqrt(_exact_dot(oc * oc, avg, "b") + EPS)
        oh = oc * r
        dgn = jnp.sum(dyv * oh, axis=0, keepdims=True)
        doh = dyv * gn_ref[...]
        do = r * (doh - _exact_dot(doh, avg, "b") - oh * _exact_dot(doh * oh, avg, "b"))
        state, g = st_ref[...], g_sc[...]
        dqt, dkt = dqt_ref[...], dkt_ref[...]
        dq = _dot(do, state, _NT) * dqt
        dk = _dot(v2, g, _NT) * dkt
        dv = _dot(k2 * dkt, g)
        g_sc[...] = gbd_ref[...] * g + bd_ref[...] * _dot(q2 * dqt, do, _TN)
        owns = [_own((tq, PAIR), h) for h in range(2)]
        qms = [jnp.where(owns[h], q2, 0.0) for h in range(2)]
        doms = [jnp.where(owns[h], do, 0.0) for h in range(2)]
        ats = [_dot(k2, qms[h], _NT) for h in range(2)]
        das = [_dot(doms[h], v2, _NT) for h in range(2)]
        for h in range(2):
            dv = dv + _dot(ats[h] * wt_ref[h], doms[h])
            daw = (das[h] * w_ref[h]).astype(_MXU)
            dq = dq + _dot(daw, jnp.where(owns[h], k2, 0.0))
            dk = dk + _dot(daw.T, qms[h])
        drq_ref[...] = _rot(dq, cos, -sin)
        drk_ref[...] = _rot(dk, cos, -sin)
        drv_ref[...] = dv

        @pl.when(first)
        def _():
            dgn_ref[...] = dgn

        @pl.when(jnp.logical_not(first))
        def _():
            dgn_ref[...] += dgn

    gn, layer = gn
    sp = _ret_specs(tq, nq, True, layer)
    nat = jax.ShapeDtypeStruct((L, D_RET), F32)
    return _pallas(
        body, out_shape=(nat, nat, nat, jax.ShapeDtypeStruct((1, D_RET), F32)), grid=(RET_HEADS // 2, nq),
        in_specs=[sp["rq"], sp["rk"], sp["rv"], sp["nat"], sp["nat"], sp["w"], sp["w"], sp["tab"], sp["tab"], sp["gbd"],
                  sp["bd"], sp["gn"], sp["nat"], sp["nat"], sp["st"]],
        out_specs=(sp["nat"], sp["nat"], sp["nat"], sp["dgn"]), scratch_shapes=[pltpu.VMEM((PAIR, PAIR), F32)],
        name="ret_bwd")(proj, proj, proj, cos_t, sin_t, tabs["w"], tabs["wt"], tabs["dq"], tabs["dk"], tabs["gbd"],
                        tabs["bd"], gn, o_pre, dy, states)


def _gate_out(yf, ys, yr, proj, x, w):
    L = x.shape[0]

    def body(yf_ref, ys_ref, yr_ref, g_ref, x_ref, w_ref, xn_ref):
        cat = jnp.concatenate([yf_ref[...], ys_ref[...], yr_ref[...]], axis=-1)
        xn_ref[...] = x_ref[...] + _dot(cat * jax.nn.silu(g_ref[...]), w_ref[...])

    tm = min(TMB, L)
    full = _rows(tm, D_MODEL)
    return _pallas(body, out_shape=jax.ShapeDtypeStruct((L, D_MODEL), F32), grid=(L // tm,),
                   in_specs=[_rows(tm, D_FOX), _rows(tm, D_S5), _rows(tm, D_RET), _rows(tm, D_MODEL, O_GATE // D_MODEL),
                             full, _whole((D_MODEL, D_MODEL))],
                   out_specs=full, name="gate_out")(yf, ys, yr, proj, x, w)


def _gate_out_bwd(dxn, w, yf, ys, yr, proj):
    L = dxn.shape[0]

    def body(dx_ref, w_ref, yf_ref, ys_ref, yr_ref, g_ref, dyf_ref, dys_ref, dyr_ref, dg_ref, dw_ref):
        dxv = dx_ref[...].astype(_MXU)
        dy = _dot(dxv, w_ref[...], _NT)
        g = g_ref[...]
        sg = jax.nn.sigmoid(g)
        silu = g * sg
        dcat = dy * silu
        dyf_ref[...] = dcat[:, :D_FOX]
        dys_ref[...] = dcat[:, D_FOX:D_FOX + D_S5]
        dyr_ref[...] = dcat[:, D_FOX + D_S5:]
        cat = jnp.concatenate([yf_ref[...], ys_ref[...], yr_ref[...]], axis=-1)
        dg_ref[...] = dy * cat * (sg * (1.0 + g * (1.0 - sg)))
        dw = _dot(cat * silu, dxv, _TN)

        @pl.when(pl.program_id(0) == 0)
        def _():
            dw_ref[...] = dw

        @pl.when(pl.program_id(0) != 0)
        def _():
            dw_ref[...] += dw

    tm = min(TMB, L)
    full = _rows(tm, D_MODEL)
    f, s, r = _rows(tm, D_FOX), _rows(tm, D_S5), _rows(tm, D_RET)
    sq = _whole((D_MODEL, D_MODEL))
    return _pallas(body, out_shape=(jax.ShapeDtypeStruct((L, D_FOX), F32), jax.ShapeDtypeStruct((L, D_S5), F32),
                                    jax.ShapeDtypeStruct((L, D_RET), F32), jax.ShapeDtypeStruct((L, D_MODEL), F32),
                                    jax.ShapeDtypeStruct((D_MODEL, D_MODEL), F32)),
                   grid=(L // tm,), in_specs=[full, sq, f, s, r, _rows(tm, D_MODEL, O_GATE // D_MODEL)],
                   out_specs=(f, s, r, full, sq), name="gate_out_bwd",
                   compiler_params=pltpu.CompilerParams(vmem_limit_bytes=VMEM_BIG))(dxn, w, yf, ys, yr, proj)


def _final_loss(x, g, tgt):
    L = x.shape[0]

    def body(x_ref, g_ref, t_ref, loss_ref, dx_ref, dg_ref):
        xv, gv = x_ref[...], g_ref[...]
        r = lax.rsqrt(jnp.mean(xv * xv, axis=-1, keepdims=True) + EPS)
        err = xv * r * gv - t_ref[...]
        part = 0.5 * jnp.sum(jnp.mean(err * err, axis=-1, keepdims=True), axis=0, keepdims=True)
        dx, dg = _rms_bwd(xv, gv, err * (1.0 / D_MODEL))
        dx_ref[...] = dx

        @pl.when(pl.program_id(0) == 0)
        def _():
            loss_ref[...] = part
            dg_ref[...] = dg

        @pl.when(pl.program_id(0) != 0)
        def _():
            loss_ref[...] += part
            dg_ref[...] += dg

    full = _rows(TM, D_MODEL)
    return _pallas(body, out_shape=(jax.ShapeDtypeStruct((1, 1), F32), jax.ShapeDtypeStruct((L, D_MODEL), F32),
                                    jax.ShapeDtypeStruct((1, D_MODEL), F32)),
                   grid=(L // TM,), in_specs=[full, _whole((1, D_MODEL)), full],
                   out_specs=(_whole((1, 1)), full, _whole((1, D_MODEL))), name="final_loss")(x, g, tgt)


def _block_diag(blocks):
    n, g, r, c = blocks.shape
    eye = jnp.eye(g, dtype=blocks.dtype)
    return (blocks[:, :, :, None, :] * eye[None, :, None, :, None]).reshape(n, g * r, g * c)


def _diag_blocks(m, g):
    n, r, c = m.shape[0], m.shape[1] // g, m.shape[2] // g
    eye = jnp.eye(g, dtype=m.dtype)
    return jnp.sum(m.reshape(n, g, r, g, c) * eye[None, :, None, :, None], axis=3)


def _rope_tables(L):
    half = HEAD_DIM // 2
    freqs = ROPE_BASE ** (-jnp.arange(half, dtype=F32) / half)
    ang = jnp.arange(L, dtype=F32)[:, None] * freqs[None, :]
    cos, sin = jnp.cos(ang), jnp.sin(ang)
    cos_t = jnp.tile(jnp.concatenate([cos, cos], axis=-1), (1, RET_HEADS))
    sin_t = jnp.tile(jnp.concatenate([sin, -sin], axis=-1), (1, RET_HEADS))
    return cos_t, sin_t


def _s5_disc_args(small):
    g, s, ch = S5_GROUPS, S5_STATE, S5_GROUP_CH
    return (small["s5_a_re"], small["s5_a_im"], small["s5_log_dt"][:, :, None],
            small["s5_b_re"].reshape(DEPTH, g, s * ch), small["s5_b_im"].reshape(DEPTH, g, s * ch))


def _s5_mats(small):
    g, s, ch = S5_GROUPS, S5_STATE, S5_GROUP_CH
    lr, li, bbr, bbi = _s5_disc(*_s5_disc_args(small))
    lam = jnp.concatenate([lr.reshape(DEPTH, 8, 128), li.reshape(DEPTH, 8, 128)], axis=1)
    wb = jnp.concatenate([_block_diag(b.reshape(DEPTH, g, s, ch).transpose(0, 1, 3, 2)) for b in (bbr, bbi)], axis=2)
    wc = jnp.concatenate([_block_diag(c.transpose(0, 1, 3, 2)) for c in (small["s5_c_re"], -small["s5_c_im"])], axis=1)
    return lam, wb.astype(_MXU), wc.astype(_MXU)


def _s5_param_grads(small, dwb, dwc, dlam):
    g, s, ch = S5_GROUPS, S5_STATE, S5_GROUP_CH
    dc = [_diag_blocks(m, g).transpose(0, 1, 3, 2).reshape(DEPTH, g, ch * s) for m in (dwc[:, :S5_CH], dwc[:, S5_CH:])]
    dbb = [_diag_blocks(m, g).transpose(0, 1, 3, 2).reshape(DEPTH, g, s * ch) for m in (dwb[:, :, :S5_CH], dwb[:, :, S5_CH:])]
    dar, dai, dldt, dbr, dbi = _s5_disc_bwd(*_s5_disc_args(small), dlam[:, :8].reshape(DEPTH, g, s),
                                            dlam[:, 8:].reshape(DEPTH, g, s), dbb[0], dbb[1])
    return dict(s5_a_re=dar, s5_a_im=dai, s5_log_dt=dldt.reshape(DEPTH, g), s5_b_re=dbr, s5_b_im=dbi, s5_c_re=dc[0],
                s5_c_im=-dc[1])


_DENSE = ("s5_b_re", "s5_b_im", "s5_c_re", "s5_c_im")


def _layer_fwd(x, p, rope, ride=None, late=False):
    L = x.shape[0]
    cos_t, sin_t, ret_tabs = rope
    s = {"x": x}
    proj, h = _norm_inproj(x, p["norm_w"], p["w_in"])
    s["proj"], s["h"] = proj, h
    qa, ka, kat, vt = _fox_prep(proj, _fox_cumsum(proj, p["b_f"]))
    yf, lse, landed = _fox_fwd(qa, ka, vt, ride)
    s.update(qa=qa, ka=ka, kat=kat, lse=lse, yf=yf)
    if late:
        p["w_glu"], p["w_out"] = _gathered_rows(landed[-2]), _gathered_rows(landed[-1])
        landed = landed[:-2]
    xs, ypre, ys = _s5_fwd(proj, p["wb"], p["wc"], p["lam"], p["d"], p["w_glu"])
    s.update(xs=xs, ypre=ypre, ys=ys)
    o_pre, yr, states = _ret_fwd(proj, cos_t, sin_t, ret_tabs, p["gn_w"])
    s.update(o_pre=o_pre, yr=yr, states=states)
    return _gate_out(yf, ys, yr, proj, x, p["w_out"]), s, landed


def _layer_bwd(dxn, s, p, rope, ride=None, early=False):
    L = dxn.shape[0]
    cos_t, sin_t, ret_tabs = rope
    g = {}
    proj = s["proj"]
    dyf, dys, dyr, dgate, g["w_out"] = _gate_out_bwd(dxn, p["w_out"], s["yf"], s["ys"], s["yr"], proj)
    drq, drk, drv, dgn = _ret_bwd(proj, cos_t, sin_t, ret_tabs, p["gn_w"], s["o_pre"], dyr, s["states"])
    g["ret_gn_w"] = dgn.reshape(D_RET)
    dsu, g["wb"], g["wc"], g["lam"], dd, g["s5_w_glu"] = _s5_bwd(proj, s["ypre"], dys, s["xs"], p["wb"], p["wc"], p["lam"],
                                                                 p["d"], p["w_glu"])
    g["s5_d"] = dd.reshape(D_S5)
    if early:
        ride = (ride[0] + _row_slots(g), ride[1] + [True, True])
    dqt, dkraw, dv, landed = _fox_bwd(s["qa"], s["ka"], s["kat"], proj, dyf, s["yf"], s["lse"], ride)
    dq, dk, dfl, dbf = _fox_post_bwd(dqt, dkraw, proj, p["b_f"])
    g["fox_b_f"] = dbf[0, :FOX_HEADS]
    pieces = [dgate, dq, dk, dv, dsu, drq, drk, drv, dfl]
    dx, dnw, g["w_in"] = _inproj_bwd(pieces, p["w_in"], s["x"], p["norm_w"], dxn, s["h"])
    g["norm_w"] = dnw.reshape(D_MODEL)
    return dx, g, landed


def _stacked_params(small):
    lam, wb, wc = _s5_mats(small)
    row = lambda a: a[:, None, :]
    return dict(norm_w=row(small["norm_w"]), b_f=row(jnp.pad(small["fox_b_f"], ((0, 0), (0, PAIR - FOX_HEADS)))),
                lam=lam, wb=wb, wc=wc, d=row(small["s5_d"]), gn_w=row(small["ret_gn_w"]))


def _layer_params(l, w_in_p, w_glu, w_out, stacked):
    return dict({k: (a, l) for k, a in stacked.items()}, w_in=w_in_p, w_glu=w_glu, w_out=w_out)


_SHARDED = ("w_in", "s5_w_glu", "w_out")
_WIRE = jnp.bfloat16


_RUNS = ((2568, 3592, O_GATE), (0, 1536, O_FQ), (1544, 2568, O_SU), (1536, 1544, O_FL))


def _shard_pieces():
    out = []
    for a, b, pad in _RUNS:
        while a < b:
            j = a // W_SHARD
            e = min(b, (j + 1) * W_SHARD)
            out.append((j, a - j * W_SHARD, e - j * W_SHARD, pad))
            pad, a = pad + e - a, e
    return out


def _gathered_w_in(g_in):
    cols = [g_in[j, :, a:e] for j, a, e, _ in _shard_pieces()]
    cols.append(jnp.zeros((D_MODEL, D_INP - O_FL - FOX_HEADS), g_in.dtype))
    return jnp.concatenate(cols, axis=1)


def _gathered_rows(g):
    return g.reshape(-1, g.shape[-1])


def _w_in_slots(g):
    w_in = g["w_in"].astype(_WIRE)
    slots = []
    for j in range(N_DEV):
        mine = sorted((a, e, pad) for jj, a, e, pad in _shard_pieces() if jj == j)
        slots.append(jnp.concatenate([w_in[:, pad:pad + e - a] for a, e, pad in mine], axis=1))
    return jnp.stack(slots)


def _row_slots(g):
    return [g["s5_w_glu"].reshape(N_DEV, D_S5 // N_DEV, D_S5).astype(_WIRE),
            g["w_out"].reshape(N_DEV, D_MODEL // N_DEV, D_MODEL).astype(_WIRE)]


def _step_grads(x, tgt, small, full=None, shards=None):
    L = x.shape[0]
    rope = _rope_tables(L) + (_ret_tables(min(TQ, L)),)
    stacked = _stacked_params(small)
    if shards is not None:
        nxt = (_gathered_w_in(_exchange([shards[0][0]], [False], "gather_layer0")[0]), None, None)
    saved, params = [], []
    for l in range(DEPTH):
        weights = nxt if shards is not None else tuple(f[l] for f in full)
        ride = None
        if shards is not None:
            arrs = [s[l + 1] for s in shards] if l + 1 < DEPTH else []
            arrs += [shards[1][0], shards[2][0]] if l == 0 else []
            ride = (arrs, [False] * len(arrs)) if arrs else None
        params.append(_layer_params(l, *weights, stacked))
        x, s, landed = _layer_fwd(x, params[l], rope, ride, late=shards is not None and l == 0)
        if landed:
            nxt = (_gathered_w_in(landed[0]), _gathered_rows(landed[1]), _gathered_rows(landed[2]))
        saved.append(s)
    loss, dx, dfw = _final_loss(x, small["final_norm_w"][None], tgt)
    grads, partials, waiting = [None] * DEPTH, [None] * DEPTH, None
    for l in reversed(range(DEPTH)):
        ride = (waiting, [True] * len(waiting)) if waiting is not None else None
        dx, grads[l], landed = _layer_bwd(dx, saved[l], params[l], rope, ride, early=ride is not None and l == 0)
        if waiting is not None:
            partials[l + 1] = landed[:3]
        if shards is not None:
            waiting = [_w_in_slots(grads[l])] + _row_slots(grads[l])
    stack = lambda n: jnp.stack([g[n] for g in grads])
    small_g = {n: stack(n) for n in ("norm_w", "fox_b_f", "s5_d", "ret_gn_w")}
    small_g.update(_s5_param_grads(small, stack("wb"), stack("wc"), stack("lam")), final_norm_w=dfw)
    if shards is None:
        return loss, dx, grads, small_g
    last = _exchange([waiting[0]] + [small_g[n].astype(_WIRE) for n in _SMALL], [True] + [False] * len(_SMALL),
                     "exchange_layer0")
    partials[0] = [last[0]] + landed[3:]
    return loss, dx, grads, small_g, partials, dict(zip(_SMALL, last[1:]))


_MESH = pl.DeviceIdType.MESH
_ANY = pl.BlockSpec(memory_space=pl.ANY)


def _me_and_peers():
    x, y, c = lax.axis_index("x"), lax.axis_index("y"), lax.axis_index("c")
    flip = lambda a, bit: (1 - a) if bit else a
    peers = []
    for r in range(1, N_DEV):
        px, py, pc = flip(x, (r >> 2) & 1), flip(y, (r >> 1) & 1), flip(c, r & 1)
        peers.append(((px, py, pc), 4 * px + 2 * py + pc))
    return 4 * x + 2 * y + c, peers


def _exchange_copies(srcs, dsts, sems, scatter):
    send_sems, recv_sems, local_sems = sems
    me, peers = _me_and_peers()
    pick = lambda t, to: srcs[t].at[to] if scatter[t] else srcs[t]
    own = [pltpu.make_async_copy(pick(t, me), dsts[t].at[me], local_sems.at[t]) for t in range(len(srcs))]
    sends, waits = [], []
    for r, (dev, idx) in enumerate(peers):
        for t in range(len(srcs)):
            for land, out in ((me, sends), (idx, waits)):
                out.append(pltpu.make_async_remote_copy(pick(t, idx), dsts[t].at[land], send_sems.at[t, r], recv_sems.at[t, r],
                                                        device_id=dev, device_id_type=_MESH))
    return own, sends, waits


def _exchange_start(srcs, dsts, sems, scatter):
    own, sends, _ = _exchange_copies(srcs, dsts, sems, scatter)
    for cp in own + sends:
        cp.start()


def _exchange_wait(srcs, dsts, sems, scatter):
    own, _, waits = _exchange_copies(srcs, dsts, sems, scatter)
    for cp in waits + own:
        cp.wait()


def _exchange_shapes(arrs, scatter):
    outs = [jax.ShapeDtypeStruct(a.shape if sc else (N_DEV,) + a.shape, a.dtype) for a, sc in zip(arrs, scatter)]
    n = len(arrs)
    sems = [pltpu.SemaphoreType.DMA((n, N_DEV - 1)), pltpu.SemaphoreType.DMA((n, N_DEV - 1)), pltpu.SemaphoreType.DMA((n,))]
    return outs, sems


def _exchange(arrs, scatter, name):
    n = len(arrs)

    def body(*refs):
        _exchange_start(refs[:n], refs[n:2 * n], refs[2 * n:], scatter)
        _exchange_wait(refs[:n], refs[n:2 * n], refs[2 * n:], scatter)

    outs, sems = _exchange_shapes(arrs, scatter)
    return _pallas(body, out_shape=tuple(outs), in_specs=[_ANY] * n, out_specs=tuple([_ANY] * n), scratch_shapes=sems,
                   name=name)(*arrs)


def _riding(body, n_in, n_out, ride, is_first, is_last):
    if ride is None:
        return body, [], [], [], []
    arrs, scatter = ride
    n = len(arrs)
    outs, sems = _exchange_shapes(arrs, scatter)

    def wrapped(*refs):
        ins, srcs = refs[:n_in], refs[n_in:n_in + n]
        own_outs, dsts = refs[n_in + n:n_in + n + n_out], refs[n_in + n + n_out:n_in + 2 * n + n_out]
        scratch, ex_sems = refs[n_in + 2 * n + n_out:-3], refs[-3:]

        @pl.when(is_first())
        def _():
            _exchange_start(srcs, dsts, ex_sems, scatter)

        body(*ins, *own_outs, *scratch)

        @pl.when(is_last())
        def _():
            _exchange_wait(srcs, dsts, ex_sems, scatter)

    return wrapped, list(arrs), [_ANY] * n, outs, sems


def _adamw_body(p_ref, w_ref, m_ref, v_ref, g_ref, d_ref, nm_ref, nv_ref):
    g = p_ref[0].astype(F32)
    for i in range(1, N_DEV):
        g = g + p_ref[i].astype(F32)
    nm = ADAM_B1 * m_ref[...] + (1.0 - ADAM_B1) * g
    nv = ADAM_B2 * v_ref[...] + (1.0 - ADAM_B2) * jnp.square(g)
    m_hat = nm / (1.0 - ADAM_B1 ** ADAM_STEP)
    v_hat = nv / (1.0 - ADAM_B2 ** ADAM_STEP)
    g_ref[...] = g
    d_ref[...] = -ADAM_LR * (m_hat / (jnp.sqrt(v_hat) + ADAM_EPS) + ADAM_WD * w_ref[...])
    nm_ref[...] = nm
    nv_ref[...] = nv


def _adamw(parts, w, m, v, name):
    n, nb, rows, cols = parts.shape
    tm = next(t for t in (256, 128, 64, 32, 16) if rows % t == 0)

    def body(*refs):
        _adamw_body(*refs)

    row = pl.BlockSpec((None, tm, cols), lambda b, i: (b, i, 0))
    return _pallas(body, out_shape=(jax.ShapeDtypeStruct((nb, rows, cols), F32),) * 4, grid=(nb, rows // tm),
                   in_specs=[pl.BlockSpec((n, None, tm, cols), lambda b, i: (0, b, i, 0)), row, row, row],
                   out_specs=(row,) * 4, name=name)(parts, w, m, v)


def _adamw_whole(parts, w, m, v, name):
    def body(*refs):
        _adamw_body(*refs)

    if w.ndim == 2:
        grid = (1,)
        slab = pl.BlockSpec(w.shape, lambda b: (0, 0))
        part = pl.BlockSpec(parts.shape, lambda b: (0, 0, 0))
    else:
        grid, rest = (w.shape[0],), w.shape[1:]
        zeros = (0,) * len(rest)
        slab = pl.BlockSpec((None,) + rest, lambda b: (b,) + zeros)
        part = pl.BlockSpec((N_DEV, None) + rest, lambda b: (0, b) + zeros)
    return _pallas(body, out_shape=(jax.ShapeDtypeStruct(w.shape, F32),) * 4, grid=grid,
                   in_specs=[part, slab, slab, slab], out_specs=(slab,) * 4, name=name)(parts, w, m, v)


_WEIGHTS = ("norm_w", "w_in", "fox_b_f", "s5_a_re", "s5_a_im", "s5_b_re", "s5_b_im", "s5_c_re", "s5_c_im", "s5_d",
            "s5_log_dt", "s5_w_glu", "ret_gn_w", "w_out", "final_norm_w")
_SMALL = tuple(n for n in _WEIGHTS if n not in _SHARDED)


def kernel(x, norm_w, w_in, fox_b_f, s5_a_re, s5_a_im, s5_b_re, s5_b_im, s5_c_re, s5_c_im, s5_d, s5_log_dt, s5_w_glu, ret_gn_w, w_out, final_norm_w, loss_target, m_norm_w, m_w_in, m_fox_b_f, m_s5_a_re, m_s5_a_im, m_s5_b_re, m_s5_b_im, m_s5_c_re, m_s5_c_im, m_s5_d, m_s5_log_dt, m_s5_w_glu, m_ret_gn_w, m_w_out, m_final_norm_w, v_norm_w, v_w_in, v_fox_b_f, v_s5_a_re, v_s5_a_im, v_s5_b_re, v_s5_b_im, v_s5_c_re, v_s5_c_im, v_s5_d, v_s5_log_dt, v_s5_w_glu, v_ret_gn_w, v_w_out, v_final_norm_w):
    w = dict(norm_w=norm_w, w_in=w_in, fox_b_f=fox_b_f, s5_a_re=s5_a_re, s5_a_im=s5_a_im, s5_b_re=s5_b_re, s5_b_im=s5_b_im,
             s5_c_re=s5_c_re, s5_c_im=s5_c_im, s5_d=s5_d, s5_log_dt=s5_log_dt, s5_w_glu=s5_w_glu, ret_gn_w=ret_gn_w,
             w_out=w_out, final_norm_w=final_norm_w)
    m = dict(norm_w=m_norm_w, w_in=m_w_in, fox_b_f=m_fox_b_f, s5_a_re=m_s5_a_re, s5_a_im=m_s5_a_im, s5_b_re=m_s5_b_re,
             s5_b_im=m_s5_b_im, s5_c_re=m_s5_c_re, s5_c_im=m_s5_c_im, s5_d=m_s5_d, s5_log_dt=m_s5_log_dt,
             s5_w_glu=m_s5_w_glu, ret_gn_w=m_ret_gn_w, w_out=m_w_out, final_norm_w=m_final_norm_w)
    v = dict(norm_w=v_norm_w, w_in=v_w_in, fox_b_f=v_fox_b_f, s5_a_re=v_s5_a_re, s5_a_im=v_s5_a_im, s5_b_re=v_s5_b_re,
             s5_b_im=v_s5_b_im, s5_c_re=v_s5_c_re, s5_c_im=v_s5_c_im, s5_d=v_s5_d, s5_log_dt=v_s5_log_dt,
             s5_w_glu=v_s5_w_glu, ret_gn_w=v_ret_gn_w, w_out=v_w_out, final_norm_w=v_final_norm_w)

    small = {n: w[n] for n in _SMALL}
    loss, dx, _, _, partials, r_small = _step_grads(x[0], loss_target[0], small, shards=[w[n].astype(_MXU) for n in _SHARDED])

    res = {}
    for t, n in enumerate(_SHARDED):
        res[n] = _adamw(jnp.stack([partials[l][t] for l in range(DEPTH)], axis=1), w[n], m[n], v[n], "adamw_" + n)
    for n in _SMALL:
        shape = w[n].shape
        view = (1,) + shape if len(shape) == 1 else shape[:2] + (-1,) if n in _DENSE else shape
        outs = _adamw_whole(r_small[n], *[d[n].reshape(view) for d in (w, m, v)], "adamw_" + n)
        res[n] = [o.reshape(shape) for o in outs]

    loss = lax.psum(loss[0, 0], ("x", "y", "c"))
    return (loss, dx[None], *[res[n][0] for n in _WEIGHTS], *[res[n][1] for n in _WEIGHTS],
            *[res[n][2] for n in _WEIGHTS], *[res[n][3] for n in _WEIGHTS])
```

```python
import math

import jax
import jax.numpy as jnp
from jax import lax
from jax.experimental import pallas as pl
from jax.experimental.pallas import tpu as pltpu

F32 = jnp.float32
_MXU = jnp.bfloat16
_HI = lax.Precision.HIGHEST

N_DEV = 8
DEPTH = 4
D_MODEL = 1024
HEAD_DIM = 64
D_FOX = 512
FOX_HEADS = 8
D_S5 = 256
S5_GROUPS = 16
S5_GROUP_CH = 16
S5_STATE = 64
S5_CH = S5_GROUPS * S5_STATE
D_RET = 256
RET_HEADS = 4
CHUNK = 64
ROPE_BASE = 10000.0
EPS = 1e-6
D_IN = 3592
D_INP = 3712
W_SHARD = D_IN // N_DEV
O_GATE, O_FQ, O_FK, O_FV, O_SU, O_RQ, O_RK, O_RV, O_FL = 0, 1024, 1536, 2048, 2560, 2816, 3072, 3328, 3584

ADAM_LR, ADAM_B1, ADAM_B2, ADAM_EPS, ADAM_WD, ADAM_STEP = 0.001, 0.9, 0.999, 1e-08, 0.01, 10

TM = 256
TMB = 512
TQ = 512
TS = 512
NEG = -1e30
VMEM_BIG = 56 * 1024 * 1024


def _pallas(body, **kw):
    return pl.pallas_call(body, **kw)


def _whole(shape):
    n = len(shape)
    return pl.BlockSpec(shape, lambda *_: (0,) * n)


def _rows(tm, width, col=0):
    return pl.BlockSpec((tm, width), lambda i: (i, col))


def _of_layer(param):
    a, l = param
    return a, pl.BlockSpec((None,) + a.shape[1:], lambda *_: (l,) + (0,) * (a.ndim - 1))


def _dot(a, b, dims=(((1,), (0,)), ((), ()))):
    return lax.dot_general(a.astype(_MXU), b.astype(_MXU), dims, preferred_element_type=F32)


_NT = (((1,), (1,)), ((), ()))
_TN = (((0,), (0,)), ((), ()))


def _norm_inproj(x, g, w):
    L = x.shape[0]

    def body(x_ref, g_ref, w_ref, p_ref, h_ref):
        xv = x_ref[...]
        r = lax.rsqrt(jnp.mean(xv * xv, axis=-1, keepdims=True) + EPS)
        h = (xv * r * g_ref[...]).astype(_MXU)
        h_ref[...] = h
        p_ref[...] = _dot(h, w_ref[...])

    tm = min(TMB, L)
    g, g_spec = _of_layer(g)
    return _pallas(body, out_shape=(jax.ShapeDtypeStruct((L, D_INP), F32), jax.ShapeDtypeStruct((L, D_MODEL), _MXU)),
                   grid=(L // tm,),
                   in_specs=[_rows(tm, D_MODEL), g_spec,
                             pl.BlockSpec((D_MODEL, D_INP), lambda i: (0, 0), pipeline_mode=pl.Buffered(1))],
                   out_specs=(_rows(tm, D_INP), _rows(tm, D_MODEL)), name="norm_inproj",
                   compiler_params=pltpu.CompilerParams(vmem_limit_bytes=VMEM_BIG))(x, g, w)


def _rms_bwd(xv, g, dh):
    r = lax.rsqrt(jnp.mean(xv * xv, axis=-1, keepdims=True) + EPS)
    xh = xv * r
    dg = jnp.sum(dh * xh, axis=0, keepdims=True)
    dxh = dh * g
    dx = r * (dxh - xh * jnp.mean(dxh * xh, axis=-1, keepdims=True))
    return dx, dg


def _inproj_bwd(pieces, w, x, g, dres, h):
    L = x.shape[0]
    n = len(pieces)
    nb = L // TM

    def body(*refs):
        w_ref, x_ref, g_ref, dr_ref, h_ref, dx_ref, dg_ref, dw_ref, acc_sc = refs[n:]
        step = pl.program_id(0)

        @pl.when(step == 0)
        def _():
            acc_sc[...] = jnp.zeros((D_MODEL, D_INP), F32)
            dg_ref[...] = jnp.zeros((1, D_MODEL), F32)

        hv = h_ref[...]
        dh = jnp.zeros((TM, D_MODEL), F32)
        off = 0
        for r in refs[:n]:
            cols = slice(off, off + r.shape[1])
            off += r.shape[1]
            piece = r[...].astype(_MXU)
            dh = dh + _dot(piece, w_ref[:, cols], _NT)
            acc_sc[:, cols] += _dot(hv, piece, _TN)
        dx, dg = _rms_bwd(x_ref[...], g_ref[...], dh)
        dx_ref[...] = dx + dr_ref[...]
        dg_ref[...] += dg

        @pl.when(step == nb - 1)
        def _():
            dw_ref[...] = acc_sc[...].astype(_WIRE)

    resident = pl.BlockSpec((D_MODEL, D_INP), lambda i: (0, 0), pipeline_mode=pl.Buffered(1))
    g, g_spec = _of_layer(g)
    return _pallas(body, out_shape=(jax.ShapeDtypeStruct((L, D_MODEL), F32), jax.ShapeDtypeStruct((1, D_MODEL), F32),
                                    jax.ShapeDtypeStruct((D_MODEL, D_INP), _WIRE)),
                   grid=(nb,),
                   in_specs=[_rows(TM, p.shape[1]) for p in pieces]
                   + [resident, _rows(TM, D_MODEL), g_spec, _rows(TM, D_MODEL), _rows(TM, D_MODEL)],
                   out_specs=(_rows(TM, D_MODEL), _whole((1, D_MODEL)), resident),
                   scratch_shapes=[pltpu.VMEM((D_MODEL, D_INP), F32)], name="inproj_bwd",
                   compiler_params=pltpu.CompilerParams(vmem_limit_bytes=VMEM_BIG))(*pieces, w, x, g, dres, h)


PAIR = 2 * HEAD_DIM
N_AUX = 3


def _own(shape, h):
    return lax.broadcasted_iota(jnp.int32, shape, len(shape) - 1) // HEAD_DIM == h


def _split(x):
    parts = []
    for _ in range(N_AUX):
        part = x.astype(_MXU)
        parts.append(part)
        x = x - part.astype(F32)
    return parts


def _exact_dot(a, b, exact):
    if exact == "b":
        return sum(_dot(part, b) for part in _split(a))
    return sum(_dot(a, part) for part in _split(b))


def _tri(n, lower):
    r = lax.broadcasted_iota(jnp.int32, (n, n), 0)
    c = lax.broadcasted_iota(jnp.int32, (n, n), 1)
    return jnp.where(r >= c if lower else r <= c, 1.0, 0.0).astype(F32)


def _fox_cumsum(proj, b):
    L = proj.shape[0]

    def body(fl_ref, b_ref, c_ref, carry_sc):
        @pl.when(pl.program_id(0) == 0)
        def _():
            carry_sc[...] = jnp.zeros((1, PAIR), F32)

        lane = lax.broadcasted_iota(jnp.int32, (TM, PAIR), 1)
        lf = jnp.where(lane < FOX_HEADS, jax.nn.log_sigmoid(fl_ref[...] + b_ref[...]), 0.0)
        cs = _exact_dot(_tri(TM, True), lf, "a") + carry_sc[...]
        c_ref[...] = cs
        carry_sc[...] = cs[TM - 1:TM, :]

    b, b_spec = _of_layer(b)
    return _pallas(body, out_shape=jax.ShapeDtypeStruct((L, PAIR), F32), grid=(L // TM,),
                   in_specs=[_rows(TM, PAIR, O_FL // PAIR), b_spec], out_specs=_rows(TM, PAIR),
                   scratch_shapes=[pltpu.VMEM((1, PAIR), F32)], name="fox_cumsum")(proj, b)


def _fox_prep(proj, c):
    L = proj.shape[0]

    def body(q_ref, k_ref, v_ref, c_ref, qa_ref, ka_ref, kat_ref, vt_ref):
        lane = lax.broadcasted_iota(jnp.int32, (TM, PAIR), 1)
        cv = c_ref[...]
        for p in range(FOX_HEADS // 2):
            cols = slice(PAIR * p, PAIR * (p + 1))
            q2, k2 = q_ref[:, cols], k_ref[:, cols]
            vt_ref[p] = v_ref[:, cols].T.astype(_MXU)
            for e in range(2):
                h = 2 * p + e
                own = lane // HEAD_DIM == e
                a = lane - (HEAD_DIM if e == 0 else 0)
                rest = jnp.broadcast_to(cv[:, h:h + 1], (TM, PAIR))
                aux_q = jnp.where((a >= N_AUX) & (a < 2 * N_AUX), 1.0, 0.0)
                aux_k = jnp.where((a >= 0) & (a < N_AUX), 1.0, 0.0)
                for n in range(N_AUX):
                    part = rest.astype(_MXU).astype(F32)
                    rest = rest - part
                    aux_q = jnp.where(a == n, part, aux_q)
                    aux_k = jnp.where(a == N_AUX + n, -part, aux_k)
                ka = jnp.where(own, k2, aux_k)
                qa_ref[h] = jnp.where(own, q2 * (1.0 / math.sqrt(HEAD_DIM)), aux_q).astype(_MXU)
                ka_ref[h] = ka.astype(_MXU)
                kat_ref[h] = ka.T.astype(_MXU)

    hl = jax.ShapeDtypeStruct((FOX_HEADS, L, PAIR), _MXU)
    nat = lambda o: _rows(TM, D_FOX, o // D_FOX)
    rows = pl.BlockSpec((FOX_HEADS, TM, PAIR), lambda i: (0, i, 0))
    return _pallas(
        body, out_shape=(hl, hl, jax.ShapeDtypeStruct((FOX_HEADS, PAIR, L), _MXU),
                         jax.ShapeDtypeStruct((FOX_HEADS // 2, PAIR, L), _MXU)),
        grid=(L // TM,), in_specs=[nat(O_FQ), nat(O_FK), nat(O_FV), _rows(TM, PAIR)],
        out_specs=(rows, rows, pl.BlockSpec((FOX_HEADS, PAIR, TM), lambda i: (0, 0, i)),
                   pl.BlockSpec((FOX_HEADS // 2, PAIR, TM), lambda i: (0, 0, i))),
        name="fox_prep")(proj, proj, proj, c)


def _key_le_query(tq):
    return lax.broadcasted_iota(jnp.int32, (tq, tq), 0) <= lax.broadcasted_iota(jnp.int32, (tq, tq), 1)


def _grid_ends(n0, n1):
    first = lambda: (pl.program_id(0) == 0) & (pl.program_id(1) == 0)
    last = lambda: (pl.program_id(0) == n0 - 1) & (pl.program_id(1) == n1 - 1)
    return first, last


def _fox_fwd(qa, ka, vt, ride=None):
    H, L, _ = qa.shape
    tq = min(TQ, L)
    nq = L // tq

    def body(qa_ref, ka_ref, vt_ref, o_ref, lse_ref, m_sc, l_sc, acc_sc):
        i = pl.program_id(1)
        m_sc[...] = jnp.full((2, 1, tq), NEG, F32)
        l_sc[...] = jnp.zeros((2, 1, tq), F32)
        acc_sc[...] = jnp.zeros((2, HEAD_DIM, tq), F32)

        def block(j, nk, masked):
            keys = pl.ds(pl.multiple_of(j * tq, tq), nk * tq)
            vt_blk = vt_ref[:, keys]
            sts = [_dot(ka_ref[e, keys, :], qa_ref[e], _NT) for e in range(2)]
            pts, alphas = [], []
            for e in range(2):
                st = jnp.where(_key_le_query(tq), sts[e], NEG) if masked else sts[e]
                m_prev = m_sc[e]
                m_new = jnp.maximum(m_prev, jnp.max(st, axis=0, keepdims=True))
                alphas.append(jnp.exp(m_prev - m_new))
                pt = jnp.exp(st - m_new)
                l_sc[e] = alphas[e] * l_sc[e] + jnp.sum(pt, axis=0, keepdims=True)
                m_sc[e] = m_new
                pts.append(pt.astype(_MXU))
            for e in range(2):
                acc_sc[e] = alphas[e] * acc_sc[e] + _dot(vt_blk[HEAD_DIM * e:HEAD_DIM * (e + 1)], pts[e])

        def two_blocks(jj, carry):
            block(2 * jj, 2, False)
            return carry

        lax.fori_loop(0, i // 2, two_blocks, 0)

        @pl.when(i % 2 == 1)
        def _():
            block(i - 1, 1, False)

        block(i, 1, True)
        o_ref[...] = jnp.concatenate([acc_sc[0] / l_sc[0], acc_sc[1] / l_sc[1]], axis=0).T
        for e in range(2):
            lse_ref[e] = m_sc[e] + jnp.log(l_sc[e])

    body, ex_in, ex_specs, ex_out, ex_sems = _riding(body, 3, 2, ride, *_grid_ends(H // 2, nq))
    res = _pallas(
        body, out_shape=(jax.ShapeDtypeStruct((L, D_FOX), F32), jax.ShapeDtypeStruct((H, 1, L), F32), *ex_out),
        grid=(H // 2, nq),
        in_specs=[pl.BlockSpec((2, tq, PAIR), lambda p, i: (p, i, 0)), pl.BlockSpec((2, L, PAIR), lambda p, i: (p, 0, 0)),
                  pl.BlockSpec((None, PAIR, L), lambda p, i: (p, 0, 0)), *ex_specs],
        out_specs=(pl.BlockSpec((tq, PAIR), lambda p, i: (i, p)), pl.BlockSpec((2, 1, tq), lambda p, i: (p, 0, i)),
                   *ex_specs),
        scratch_shapes=[pltpu.VMEM((2, 1, tq), F32), pltpu.VMEM((2, 1, tq), F32), pltpu.VMEM((2, HEAD_DIM, tq), F32), *ex_sems],
        name="fox_fwd" if ride is None else "fox_fwd_gather")(qa, ka, vt, *ex_in)
    return res[0], res[1], list(res[2:])


def _fox_bwd(qa, ka, kat, proj, do, o, lse, ride=None):
    H, L, _ = qa.shape
    tq = min(TQ, L)
    nq = L // tq

    def body(qa_ref, ka_ref, kat_ref, v_ref, do_ref, o_ref, lse_ref, dqt_ref, dk_ref, dv_ref, delta_sc, dk_sc, dv_sc):
        j = pl.program_id(1)

        @pl.when(j == 0)
        def _():
            head_rows = (lax.broadcasted_iota(jnp.int32, (8, PAIR), 1) // HEAD_DIM
                         == lax.broadcasted_iota(jnp.int32, (8, PAIR), 0)).astype(F32)
            delta_sc[...] = lax.dot_general(head_rows, do_ref[...] * o_ref[...], _NT, precision=_HI,
                                            preferred_element_type=F32)
            dqt_ref[...] = jnp.zeros((2, PAIR, L), F32)

        dk_sc[...] = jnp.zeros((2, tq, PAIR), F32)
        dv_sc[...] = jnp.zeros((tq, PAIR), F32)
        vb = v_ref[...]

        def block(i, masked):
            qs = pl.ds(pl.multiple_of(i * tq, tq), tq)
            dob = do_ref[qs, :]
            for e in range(2):
                own = _own((tq, PAIR), e)
                qh = qa_ref[e, qs, :]
                pt = jnp.exp(_dot(ka_ref[e], qh, _NT) - lse_ref[e, :, qs])
                if masked:
                    pt = jnp.where(_key_le_query(tq), pt, 0.0)
                dv_sc[...] += _dot(pt, jnp.where(own, dob, 0.0))
                dpt = _dot(jnp.where(own, vb, 0.0), dob, _NT)
                ds = (pt * (dpt - delta_sc[e:e + 1, qs])).astype(_MXU)
                dk_sc[e] += _dot(ds, qh)
                dqt_ref[e, :, qs] += _dot(kat_ref[e], ds)

        def off_diagonal(i, carry):
            block(i, False)
            return carry

        block(j, True)
        lax.fori_loop(j + 1, nq, off_diagonal, 0)
        dk_ref[...] = dk_sc[...]
        dv_ref[...] = dv_sc[...]

    nat = pl.BlockSpec((L, PAIR), lambda p, j: (0, p))
    body, ex_in, ex_specs, ex_out, ex_sems = _riding(body, 7, 3, ride, *_grid_ends(H // 2, nq))
    res = _pallas(
        body, out_shape=(jax.ShapeDtypeStruct((H, PAIR, L), F32), jax.ShapeDtypeStruct((H, L, PAIR), F32),
                         jax.ShapeDtypeStruct((L, D_FOX), F32), *ex_out),
        grid=(H // 2, nq),
        in_specs=[pl.BlockSpec((2, L, PAIR), lambda p, j: (p, 0, 0)), pl.BlockSpec((2, tq, PAIR), lambda p, j: (p, j, 0)),
                  pl.BlockSpec((2, PAIR, tq), lambda p, j: (p, 0, j)),
                  pl.BlockSpec((tq, PAIR), lambda p, j: (j, O_FV // PAIR + p)), nat, nat,
                  pl.BlockSpec((2, 1, L), lambda p, j: (p, 0, 0)), *ex_specs],
        out_specs=(pl.BlockSpec((2, PAIR, L), lambda p, j: (p, 0, 0)), pl.BlockSpec((2, tq, PAIR), lambda p, j: (p, j, 0)),
                   pl.BlockSpec((tq, PAIR), lambda p, j: (j, p)), *ex_specs),
        scratch_shapes=[pltpu.VMEM((8, L), F32), pltpu.VMEM((2, tq, PAIR), F32), pltpu.VMEM((tq, PAIR), F32), *ex_sems],
        name="fox_bwd" if ride is None else "fox_bwd_exchange",
        compiler_params=pltpu.CompilerParams(vmem_limit_bytes=VMEM_BIG))(qa, ka, kat, proj, do, o, lse, *ex_in)
    return res[0], res[1], res[2], list(res[3:])


def _fox_post_bwd(dqt, dkraw, proj, b):
    L = proj.shape[0]
    nb = L // TM

    def body(dqt_ref, dkr_ref, fl_ref, b_ref, dq_ref, dk_ref, dfl_ref, db_ref, carry_sc):
        first = pl.program_id(0) == 0

        @pl.when(first)
        def _():
            carry_sc[...] = jnp.zeros((1, PAIR), F32)

        lane = lax.broadcasted_iota(jnp.int32, (TM, PAIR), 1)
        rr = lax.broadcasted_iota(jnp.int32, (PAIR, PAIR), 0)
        cc = lax.broadcasted_iota(jnp.int32, (PAIR, PAIR), 1)
        dc = jnp.zeros((TM, PAIR), F32)
        for p in range(FOX_HEADS // 2):
            cols = slice(PAIR * p, PAIR * (p + 1))
            dqs = [dqt_ref[2 * p + e].T for e in range(2)]
            dks = [dkr_ref[2 * p + e] for e in range(2)]
            dq_ref[:, cols] = jnp.where(lane < HEAD_DIM, dqs[0], dqs[1]) * (1.0 / math.sqrt(HEAD_DIM))
            dk_ref[:, cols] = jnp.where(lane < HEAD_DIM, dks[0], dks[1])
            sums = jnp.zeros((TM, PAIR), F32)
            place = jnp.zeros((PAIR, PAIR), F32)
            for e in range(2):
                base = HEAD_DIM if e == 0 else 0
                sums = jnp.where(lane == base, dqs[e], jnp.where(lane == base + N_AUX, -dks[e], sums))
                place = jnp.where(((rr == base) | (rr == base + N_AUX)) & (cc == 2 * p + e), 1.0, place)
            dc = dc + _exact_dot(sums, place, "b")
        rs = _exact_dot(_tri(TM, False), dc, "a") + carry_sc[...]
        carry_sc[...] = rs[0:1, :]
        dfl = jnp.where(lane < FOX_HEADS, rs * jax.nn.sigmoid(-(fl_ref[...] + b_ref[...])), 0.0)
        dfl_ref[...] = dfl
        db = jnp.sum(dfl, axis=0, keepdims=True)

        @pl.when(first)
        def _():
            db_ref[...] = db

        @pl.when(jnp.logical_not(first))
        def _():
            db_ref[...] += db

    rev = lambda i: nb - 1 - i
    b, b_spec = _of_layer(b)
    nat = pl.BlockSpec((TM, D_FOX), lambda i: (rev(i), 0))
    return _pallas(
        body, out_shape=(jax.ShapeDtypeStruct((L, D_FOX), F32),) * 2
        + (jax.ShapeDtypeStruct((L, PAIR), F32), jax.ShapeDtypeStruct((1, PAIR), F32)),
        grid=(nb,),
        in_specs=[pl.BlockSpec((FOX_HEADS, PAIR, TM), lambda i: (0, 0, rev(i))),
                  pl.BlockSpec((FOX_HEADS, TM, PAIR), lambda i: (0, rev(i), 0)),
                  pl.BlockSpec((TM, PAIR), lambda i: (rev(i), O_FL // PAIR)), b_spec],
        out_specs=(nat, nat, pl.BlockSpec((TM, PAIR), lambda i: (rev(i), 0)), _whole((1, PAIR))),
        scratch_shapes=[pltpu.VMEM((1, PAIR), F32)], name="fox_post_bwd")(dqt, dkraw, proj, b)


def _s5_expand():
    r = lax.broadcasted_iota(jnp.int32, (S5_STATE, S5_STATE * S5_GROUP_CH), 0)
    c = lax.broadcasted_iota(jnp.int32, (S5_STATE, S5_STATE * S5_GROUP_CH), 1)
    return jnp.where(c // S5_GROUP_CH == r, 1.0, 0.0).astype(F32)


def _s5_disc_math(ar, ai, ldt, br, bi):
    dt = jnp.exp(ldt)
    mag = jnp.exp(ar * dt)
    lr = mag * jnp.cos(ai * dt)
    li = mag * jnp.sin(ai * dt)
    den = ar * ar + ai * ai
    fr = ((lr - 1.0) * ar + li * ai) / den
    fi = (li * ar - (lr - 1.0) * ai) / den
    e = _s5_expand()
    fre = jnp.dot(fr, e, precision=_HI, preferred_element_type=F32)
    fie = jnp.dot(fi, e, precision=_HI, preferred_element_type=F32)
    return lr, li, fre * br - fie * bi, fre * bi + fie * br


def _layer_blocks(arrs):
    return [pl.BlockSpec((None,) + a.shape[1:], lambda l: (l, 0, 0)) for a in arrs]


def _s5_disc(ar, ai, ldt, br, bi):
    def body(ar_ref, ai_ref, ldt_ref, br_ref, bi_ref, lr_ref, li_ref, bbr_ref, bbi_ref):
        lr, li, bbr, bbi = _s5_disc_math(ar_ref[...], ai_ref[...], ldt_ref[...], br_ref[...], bi_ref[...])
        lr_ref[...] = lr
        li_ref[...] = li
        bbr_ref[...] = bbr
        bbi_ref[...] = bbi

    ins = (ar, ai, ldt, br, bi)
    outs = (ar, ai, br, bi)
    return _pallas(body, out_shape=tuple(jax.ShapeDtypeStruct(a.shape, F32) for a in outs), grid=(DEPTH,),
                   in_specs=_layer_blocks(ins), out_specs=tuple(_layer_blocks(outs)), name="s5_disc")(*ins)


def _s5_disc_bwd(ar, ai, ldt, br, bi, dlr, dli, dbbr, dbbi):
    def body(ar_ref, ai_ref, ldt_ref, br_ref, bi_ref, dlr_ref, dli_ref, dbbr_ref, dbbi_ref,
             dar_ref, dai_ref, dldt_ref, dbr_ref, dbi_ref):
        _, vjp = jax.vjp(_s5_disc_math, ar_ref[...], ai_ref[...], ldt_ref[...], br_ref[...], bi_ref[...])
        dar, dai, dldt, dbr, dbi = vjp((dlr_ref[...], dli_ref[...], dbbr_ref[...], dbbi_ref[...]))
        dar_ref[...] = dar
        dai_ref[...] = dai
        dldt_ref[...] = dldt
        dbr_ref[...] = dbr
        dbi_ref[...] = dbi

    ins = (ar, ai, ldt, br, bi, dlr, dli, dbbr, dbbi)
    outs = (ar, ai, ldt, br, bi)
    return _pallas(body, out_shape=tuple(jax.ShapeDtypeStruct(a.shape, F32) for a in outs), grid=(DEPTH,),
                   in_specs=_layer_blocks(ins), out_specs=tuple(_layer_blocks(outs)), name="s5_disc_bwd")(*ins)


SLAB = 2 * S5_CH // 128
PITCH = 24


def _slab_rows(s, ts):
    return pl.ds(s, ts, stride=PITCH)


def _slab_pair(ref, s, ts):
    return jnp.concatenate([ref[_slab_rows(s, ts), :].astype(_MXU), ref[_slab_rows(s + 1, ts), :].astype(_MXU)], axis=-1)


def _s5_fwd(proj, wb, wc, lam, d, w_glu):
    L = proj.shape[0]
    ts = min(TS, L)

    def body(u_ref, wb_ref, wc_ref, lam_ref, d_ref, wg_ref, xs_ref, ypre_ref, ys_ref, b_sc, c_sc):
        @pl.when(pl.program_id(0) == 0)
        def _():
            c_sc[...] = jnp.zeros((SLAB, 128), F32)

        u = u_ref[...]
        ub = u.astype(_MXU)
        for s in range(0, SLAB, 2):
            b2 = _dot(ub, wb_ref[:, 128 * s:128 * (s + 2)])
            b_sc[_slab_rows(s, ts), :] = b2[:, :128]
            b_sc[_slab_rows(s + 1, ts), :] = b2[:, 128:]
        lr, li = lam_ref[0:8, :], lam_ref[8:16, :]

        def step(t, carry):
            xr, xi = carry
            row = pl.multiple_of(t * PITCH, 8)
            nr = lr * xr - li * xi + b_sc[pl.ds(row, 8), :]
            ni = lr * xi + li * xr + b_sc[pl.ds(row + 8, 8), :]
            xs_ref[pl.ds(row, 8), :] = nr
            xs_ref[pl.ds(row + 8, 8), :] = ni
            return nr, ni

        xr, xi = lax.fori_loop(0, ts, step, (c_sc[0:8, :], c_sc[8:16, :]), unroll=8)
        c_sc[0:8, :] = xr
        c_sc[8:16, :] = xi
        y = jnp.zeros((ts, D_S5), F32)
        for s in range(0, SLAB, 2):
            y = y + _dot(_slab_pair(xs_ref, s, ts), wc_ref[128 * s:128 * (s + 2), :])
        ypre_ref[...] = y
        y1 = jax.nn.gelu(y + d_ref[...] * u)
        ys_ref[...] = y1 * jax.nn.sigmoid(_dot(y1, wg_ref[...]))

    row = _rows(ts, D_S5)
    slabs = pl.BlockSpec((ts * PITCH, 128), lambda n: (n, 0))
    (wb, wb_spec), (wc, wc_spec), (lam, lam_spec), (d, d_spec) = (_of_layer(a) for a in (wb, wc, lam, d))
    return _pallas(
        body, out_shape=(jax.ShapeDtypeStruct((L * PITCH, 128), F32), jax.ShapeDtypeStruct((L, D_S5), F32),
                         jax.ShapeDtypeStruct((L, D_S5), F32)),
        grid=(L // ts,),
        in_specs=[_rows(ts, D_S5, O_SU // D_S5), wb_spec, wc_spec, lam_spec, d_spec, _whole((D_S5, D_S5))],
        out_specs=(slabs, row, row),
        scratch_shapes=[pltpu.VMEM((ts * PITCH, 128), F32), pltpu.VMEM((SLAB, 128), F32)], name="s5_fwd")(
            proj, wb, wc, lam, d, w_glu)


def _s5_bwd(proj, ypre, dys, xs, wb, wc, lam, d, w_glu):
    L = proj.shape[0]
    ts = min(TS, L)
    nb = L // ts

    def body(u_ref, y_ref, dys_ref, xs_ref, xp_ref, wb_ref, wc_ref, lam_ref, d_ref, wg_ref,
             du_ref, dwb_ref, dwc_ref, dlam_ref, dd_ref, dwg_ref, dx_sc, g_sc, c_sc):
        n = pl.program_id(0)

        @pl.when(n == 0)
        def _():
            c_sc[...] = jnp.zeros((SLAB, 128), F32)
            dlam_ref[...] = jnp.zeros((SLAB, 128), F32)
            dwb_ref[...] = jnp.zeros((D_S5, 2 * S5_CH), F32)
            dwc_ref[...] = jnp.zeros((2 * S5_CH, D_S5), F32)
            dd_ref[...] = jnp.zeros((1, D_S5), F32)
            dwg_ref[...] = jnp.zeros((D_S5, D_S5), F32)

        u, dv, dout = u_ref[...], d_ref[...], dys_ref[...]
        y1, gelu_vjp = jax.vjp(jax.nn.gelu, y_ref[...] + dv * u)
        sg = jax.nn.sigmoid(_dot(y1, wg_ref[...]))
        dz = dout * y1 * sg * (1.0 - sg)
        dy, = gelu_vjp(dout * sg + _dot(dz, wg_ref[...], _NT))
        dd_ref[...] += jnp.sum(dy * u, axis=0, keepdims=True)
        dwg_ref[...] += _dot(y1, dz, _TN)
        dyb = dy.astype(_MXU)
        for s in range(0, SLAB, 2):
            cols = slice(128 * s, 128 * (s + 2))
            dx2 = _dot(dyb, wc_ref[cols, :], _NT)
            dx_sc[_slab_rows(s, ts), :] = dx2[:, :128]
            dx_sc[_slab_rows(s + 1, ts), :] = dx2[:, 128:]
            dwc_ref[cols, :] += _dot(_slab_pair(xs_ref, s, ts), dyb, _TN)
        lr, li = lam_ref[0:8, :], lam_ref[8:16, :]

        def step(k, carry):
            gr, gi = carry
            row = pl.multiple_of((ts - 1 - k) * PITCH, 8)
            nr = dx_sc[pl.ds(row, 8), :] + lr * gr + li * gi
            ni = dx_sc[pl.ds(row + 8, 8), :] - li * gr + lr * gi
            g_sc[pl.ds(row, 8), :] = nr
            g_sc[pl.ds(row + 8, 8), :] = ni
            return nr, ni

        gr, gi = lax.fori_loop(0, ts, step, (c_sc[0:8, :], c_sc[8:16, :]), unroll=8)
        c_sc[0:8, :] = gr
        c_sc[8:16, :] = gi
        n1 = (ts - 1) * PITCH
        g3 = g_sc[pl.ds(PITCH, n1), :].reshape(ts - 1, PITCH, 128)
        x3 = xs_ref[pl.ds(0, n1), :].reshape(ts - 1, PITCH, 128)
        g3r, g3i, x3r, x3i = g3[:, 0:8], g3[:, 8:16], x3[:, 0:8], x3[:, 8:16]
        has_prev = jnp.where(n == nb - 1, 0.0, 1.0)
        g0r, g0i = g_sc[0:8, :], g_sc[8:16, :]
        pr, pi = xp_ref[0:8, :] * has_prev, xp_ref[8:16, :] * has_prev
        dlam_ref[0:8, :] += jnp.sum(g3r * x3r + g3i * x3i, axis=0) + g0r * pr + g0i * pi
        dlam_ref[8:16, :] += jnp.sum(g3i * x3r - g3r * x3i, axis=0) + g0i * pr - g0r * pi
        ub = u.astype(_MXU)
        du = dy * dv
        for s in range(0, SLAB, 2):
            cols = slice(128 * s, 128 * (s + 2))
            gs = _slab_pair(g_sc, s, ts)
            du = du + _dot(gs, wb_ref[:, cols], _NT)
            dwb_ref[:, cols] += _dot(ub, gs, _TN)
        du_ref[...] = du

    blk = lambda n: nb - 1 - n
    row = pl.BlockSpec((ts, D_S5), lambda n: (blk(n), 0))
    (wb, wb_spec), (wc, wc_spec), (lam, lam_spec), (d, d_spec) = (_of_layer(a) for a in (wb, wc, lam, d))
    return _pallas(
        body, out_shape=(jax.ShapeDtypeStruct((L, D_S5), F32), jax.ShapeDtypeStruct((D_S5, 2 * S5_CH), F32),
                         jax.ShapeDtypeStruct((2 * S5_CH, D_S5), F32), jax.ShapeDtypeStruct((SLAB, 128), F32),
                         jax.ShapeDtypeStruct((1, D_S5), F32), jax.ShapeDtypeStruct((D_S5, D_S5), F32)),
        grid=(nb,),
        in_specs=[pl.BlockSpec((ts, D_S5), lambda n: (blk(n), O_SU // D_S5)), row, row,
                  pl.BlockSpec((ts * PITCH, 128), lambda n: (blk(n), 0)),
                  pl.BlockSpec((PITCH, 128), lambda n: (jnp.maximum(blk(n) * ts - 1, 0), 0)),
                  wb_spec, wc_spec, lam_spec, d_spec, _whole((D_S5, D_S5))],
        out_specs=(row, _whole((D_S5, 2 * S5_CH)), _whole((2 * S5_CH, D_S5)), _whole((SLAB, 128)), _whole((1, D_S5)),
                   _whole((D_S5, D_S5))),
        scratch_shapes=[pltpu.VMEM((ts * PITCH, 128), F32), pltpu.VMEM((ts * PITCH, 128), F32), pltpu.VMEM((SLAB, 128), F32)],
        name="s5_bwd", compiler_params=pltpu.CompilerParams(vmem_limit_bytes=VMEM_BIG))(
            proj, ypre, dys, xs, xs, wb, wc, lam, d, w_glu)


def _rot(z, cos, sin):
    lane = lax.broadcasted_iota(jnp.int32, z.shape, 1)
    zs = z * sin
    half = HEAD_DIM // 2
    return z * cos + jnp.where(lane % HEAD_DIM < half, pltpu.roll(zs, PAIR - half, 1), pltpu.roll(zs, half, 1))


def _head_avg():
    r = lax.broadcasted_iota(jnp.int32, (PAIR, PAIR), 0) // HEAD_DIM
    c = lax.broadcasted_iota(jnp.int32, (PAIR, PAIR), 1) // HEAD_DIM
    return jnp.where(r == c, 1.0 / HEAD_DIM, 0.0).astype(F32)


def _ret_tables(tq):
    lg = jnp.log1p(-(2.0 ** (-5.0 - jnp.arange(RET_HEADS, dtype=F32))))
    scale = 1.0 / math.sqrt(HEAD_DIM)
    pos = jnp.arange(tq)
    n = pos.astype(F32)
    dist = jnp.abs(n[:, None] - n[None, :])
    ok = (pos[None, :] // CHUNK) <= (pos[:, None] // CHUNK)
    w = jnp.where(ok[None], scale * jnp.exp(lg[:, None, None] * dist[None]), 0.0)
    lgl = jnp.repeat(lg, HEAD_DIM)
    dq_tab = scale * jnp.exp(lgl[None, :] * (n[:, None] + 1.0))
    dk_tab = jnp.exp(lgl[None, :] * (tq - 1.0 - n[:, None]))
    blk = jnp.arange(PAIR) // HEAD_DIM
    bd = (blk[:, None] == blk[None, :]).astype(F32)
    gbd = bd[None] * jnp.exp(lgl.reshape(RET_HEADS // 2, PAIR)[:, :, None] * tq)
    return dict(w=w, wt=w.transpose(0, 2, 1), dq=dq_tab, dk=dk_tab, gbd=gbd, bd=bd)


def _ret_specs(tq, nq, rev, layer):
    blk = (lambda i: nq - 1 - i) if rev else (lambda i: i)
    col = lambda o: pl.BlockSpec((tq, PAIR), lambda p, i: (blk(i), o // PAIR + p))
    return dict(
        rq=col(O_RQ), rk=col(O_RK), rv=col(O_RV), nat=col(0),
        w=pl.BlockSpec((2, tq, tq), lambda p, i: (p, 0, 0)), tab=pl.BlockSpec((tq, PAIR), lambda p, i: (0, p)),
        gbd=pl.BlockSpec((None, PAIR, PAIR), lambda p, i: (p, 0, 0)), bd=pl.BlockSpec((PAIR, PAIR), lambda p, i: (0, 0)),
        gn=pl.BlockSpec((None, 1, PAIR), lambda p, i: (layer, 0, p)), dgn=pl.BlockSpec((1, PAIR), lambda p, i: (0, p)),
        st=pl.BlockSpec((None, None, PAIR, PAIR), lambda p, i: (p, blk(i), 0, 0)))


def _ret_fwd(proj, cos_t, sin_t, tabs, gn):
    L = proj.shape[0]
    tq = tabs["w"].shape[1]
    nq = L // tq

    def body(rq_ref, rk_ref, rv_ref, cos_ref, sin_ref, w_ref, dqt_ref, dkt_ref, gbd_ref, bd_ref, gn_ref,
             o_ref, y_ref, st_ref, s_sc):
        @pl.when(pl.program_id(1) == 0)
        def _():
            s_sc[...] = jnp.zeros((PAIR, PAIR), F32)

        state = s_sc[...]
        st_ref[...] = state
        cos, sin = cos_ref[...], sin_ref[...]
        q2, k2, v2 = _rot(rq_ref[...], cos, sin), _rot(rk_ref[...], cos, sin), rv_ref[...]
        owns = [_own((tq, PAIR), h) for h in range(2)]
        scores = [_dot(jnp.where(owns[h], q2, 0.0), k2, _NT) for h in range(2)]
        o = _dot(q2 * dqt_ref[...], state)
        for h in range(2):
            o = o + _dot(scores[h] * w_ref[h], jnp.where(owns[h], v2, 0.0))
        s_sc[...] = gbd_ref[...] * state + bd_ref[...] * _dot(k2 * dkt_ref[...], v2, _TN)
        o_ref[...] = o
        avg = _head_avg()
        oc = o - _exact_dot(o, avg, "b")
        y_ref[...] = oc * lax.rsqrt(_exact_dot(oc * oc, avg, "b") + EPS) * gn_ref[...]

    gn, layer = gn
    sp = _ret_specs(tq, nq, False, layer)
    nat = jax.ShapeDtypeStruct((L, D_RET), F32)
    return _pallas(
        body, out_shape=(nat, nat, jax.ShapeDtypeStruct((RET_HEADS // 2, nq, PAIR, PAIR), F32)), grid=(RET_HEADS // 2, nq),
        in_specs=[sp["rq"], sp["rk"], sp["rv"], sp["nat"], sp["nat"], sp["w"], sp["tab"], sp["tab"], sp["gbd"], sp["bd"],
                  sp["gn"]],
        out_specs=(sp["nat"], sp["nat"], sp["st"]), scratch_shapes=[pltpu.VMEM((PAIR, PAIR), F32)],
        name="ret_fwd")(proj, proj, proj, cos_t, sin_t, tabs["w"], tabs["dq"], tabs["dk"], tabs["gbd"], tabs["bd"], gn)


def _ret_bwd(proj, cos_t, sin_t, tabs, gn, o_pre, dy, states):
    L = proj.shape[0]
    tq = tabs["w"].shape[1]
    nq = L // tq

    def body(rq_ref, rk_ref, rv_ref, cos_ref, sin_ref, w_ref, wt_ref, dqt_ref, dkt_ref, gbd_ref, bd_ref, gn_ref,
             o_ref, dy_ref, st_ref, drq_ref, drk_ref, drv_ref, dgn_ref, g_sc):
        first = pl.program_id(1) == 0

        @pl.when(first)
        def _():
            g_sc[...] = jnp.zeros((PAIR, PAIR), F32)

        cos, sin = cos_ref[...], sin_ref[...]
        q2, k2, v2 = _rot(rq_ref[...], cos, sin), _rot(rk_ref[...], cos, sin), rv_ref[...]
        avg = _head_avg()
        ov, dyv = o_ref[...], dy_ref[...]
        oc = ov - _exact_dot(ov, avg, "b")
        r = lax.rsqrt(_exact_dot(oc * oc, avg, "b") + EPS)
        oh = oc * r
        dgn = jnp.sum(dyv * oh, axis=0, keepdims=True)
        doh = dyv * gn_ref[...]
        do = r * (doh - _exact_dot(doh, avg, "b") - oh * _exact_dot(doh * oh, avg, "b"))
        state, g = st_ref[...], g_sc[...]
        dqt, dkt = dqt_ref[...], dkt_ref[...]
        dq = _dot(do, state, _NT) * dqt
        dk = _dot(v2, g, _NT) * dkt
        dv = _dot(k2 * dkt, g)
        g_sc[...] = gbd_ref[...] * g + bd_ref[...] * _dot(q2 * dqt, do, _TN)
        owns = [_own((tq, PAIR), h) for h in range(2)]
        qms = [jnp.where(owns[h], q2, 0.0) for h in range(2)]
        doms = [jnp.where(owns[h], do, 0.0) for h in range(2)]
        ats = [_dot(k2, qms[h], _NT) for h in range(2)]
        das = [_dot(doms[h], v2, _NT) for h in range(2)]
        for h in range(2):
            dv = dv + _dot(ats[h] * wt_ref[h], doms[h])
            daw = (das[h] * w_ref[h]).astype(_MXU)
            dq = dq + _dot(daw, jnp.where(owns[h], k2, 0.0))
            dk = dk + _dot(daw.T, qms[h])
        drq_ref[...] = _rot(dq, cos, -sin)
        drk_ref[...] = _rot(dk, cos, -sin)
        drv_ref[...] = dv

        @pl.when(first)
        def _():
            dgn_ref[...] = dgn

        @pl.when(jnp.logical_not(first))
        def _():
            dgn_ref[...] += dgn

    gn, layer = gn
    sp = _ret_specs(tq, nq, True, layer)
    nat = jax.ShapeDtypeStruct((L, D_RET), F32)
    return _pallas(
        body, out_shape=(nat, nat, nat, jax.ShapeDtypeStruct((1, D_RET), F32)), grid=(RET_HEADS // 2, nq),
        in_specs=[sp["rq"], sp["rk"], sp["rv"], sp["nat"], sp["nat"], sp["w"], sp["w"], sp["tab"], sp["tab"], sp["gbd"],
                  sp["bd"], sp["gn"], sp["nat"], sp["nat"], sp["st"]],
        out_specs=(sp["nat"], sp["nat"], sp["nat"], sp["dgn"]), scratch_shapes=[pltpu.VMEM((PAIR, PAIR), F32)],
        name="ret_bwd")(proj, proj, proj, cos_t, sin_t, tabs["w"], tabs["wt"], tabs["dq"], tabs["dk"], tabs["gbd"],
                        tabs["bd"], gn, o_pre, dy, states)


def _gate_out(yf, ys, yr, proj, x, w):
    L = x.shape[0]

    def body(yf_ref, ys_ref, yr_ref, g_ref, x_ref, w_ref, xn_ref):
        cat = jnp.concatenate([yf_ref[...], ys_ref[...], yr_ref[...]], axis=-1)
        xn_ref[...] = x_ref[...] + _dot(cat * jax.nn.silu(g_ref[...]), w_ref[...])

    tm = min(TMB, L)
    full = _rows(tm, D_MODEL)
    return _pallas(body, out_shape=jax.ShapeDtypeStruct((L, D_MODEL), F32), grid=(L // tm,),
                   in_specs=[_rows(tm, D_FOX), _rows(tm, D_S5), _rows(tm, D_RET), _rows(tm, D_MODEL, O_GATE // D_MODEL),
                             full, _whole((D_MODEL, D_MODEL))],
                   out_specs=full, name="gate_out")(yf, ys, yr, proj, x, w)


def _gate_out_bwd(dxn, w, yf, ys, yr, proj):
    L = dxn.shape[0]

    def body(dx_ref, w_ref, yf_ref, ys_ref, yr_ref, g_ref, dyf_ref, dys_ref, dyr_ref, dg_ref, dw_ref):
        dxv = dx_ref[...].astype(_MXU)
        dy = _dot(dxv, w_ref[...], _NT)
        g = g_ref[...]
        sg = jax.nn.sigmoid(g)
        silu = g * sg
        dcat = dy * silu
        dyf_ref[...] = dcat[:, :D_FOX]
        dys_ref[...] = dcat[:, D_FOX:D_FOX + D_S5]
        dyr_ref[...] = dcat[:, D_FOX + D_S5:]
        cat = jnp.concatenate([yf_ref[...], ys_ref[...], yr_ref[...]], axis=-1)
        dg_ref[...] = dy * cat * (sg * (1.0 + g * (1.0 - sg)))
        dw = _dot(cat * silu, dxv, _TN)

        @pl.when(pl.program_id(0) == 0)
        def _():
            dw_ref[...] = dw

        @pl.when(pl.program_id(0) != 0)
        def _():
            dw_ref[...] += dw

    tm = min(TMB, L)
    full = _rows(tm, D_MODEL)
    f, s, r = _rows(tm, D_FOX), _rows(tm, D_S5), _rows(tm, D_RET)
    sq = _whole((D_MODEL, D_MODEL))
    return _pallas(body, out_shape=(jax.ShapeDtypeStruct((L, D_FOX), F32), jax.ShapeDtypeStruct((L, D_S5), F32),
                                    jax.ShapeDtypeStruct((L, D_RET), F32), jax.ShapeDtypeStruct((L, D_MODEL), F32),
                                    jax.ShapeDtypeStruct((D_MODEL, D_MODEL), F32)),
                   grid=(L // tm,), in_specs=[full, sq, f, s, r, _rows(tm, D_MODEL, O_GATE // D_MODEL)],
                   out_specs=(f, s, r, full, sq), name="gate_out_bwd",
                   compiler_params=pltpu.CompilerParams(vmem_limit_bytes=VMEM_BIG))(dxn, w, yf, ys, yr, proj)


def _final_loss(x, g, tgt):
    L = x.shape[0]

    def body(x_ref, g_ref, t_ref, loss_ref, dx_ref, dg_ref):
        xv, gv = x_ref[...], g_ref[...]
        r = lax.rsqrt(jnp.mean(xv * xv, axis=-1, keepdims=True) + EPS)
        err = xv * r * gv - t_ref[...]
        part = 0.5 * jnp.sum(jnp.mean(err * err, axis=-1, keepdims=True), axis=0, keepdims=True)
        dx, dg = _rms_bwd(xv, gv, err * (1.0 / D_MODEL))
        dx_ref[...] = dx

        @pl.when(pl.program_id(0) == 0)
        def _():
            loss_ref[...] = part
            dg_ref[...] = dg

        @pl.when(pl.program_id(0) != 0)
        def _():
            loss_ref[...] += part
            dg_ref[...] += dg

    full = _rows(TM, D_MODEL)
    return _pallas(body, out_shape=(jax.ShapeDtypeStruct((1, 1), F32), jax.ShapeDtypeStruct((L, D_MODEL), F32),
                                    jax.ShapeDtypeStruct((1, D_MODEL), F32)),
                   grid=(L // TM,), in_specs=[full, _whole((1, D_MODEL)), full],
                   out_specs=(_whole((1, 1)), full, _whole((1, D_MODEL))), name="final_loss")(x, g, tgt)


def _block_diag(blocks):
    n, g, r, c = blocks.shape
    eye = jnp.eye(g, dtype=blocks.dtype)
    return (blocks[:, :, :, None, :] * eye[None, :, None, :, None]).reshape(n, g * r, g * c)


def _diag_blocks(m, g):
    n, r, c = m.shape[0], m.shape[1] // g, m.shape[2] // g
    eye = jnp.eye(g, dtype=m.dtype)
    return jnp.sum(m.reshape(n, g, r, g, c) * eye[None, :, None, :, None], axis=3)


def _rope_tables(L):
    half = HEAD_DIM // 2
    freqs = ROPE_BASE ** (-jnp.arange(half, dtype=F32) / half)
    ang = jnp.arange(L, dtype=F32)[:, None] * freqs[None, :]
    cos, sin = jnp.cos(ang), jnp.sin(ang)
    cos_t = jnp.tile(jnp.concatenate([cos, cos], axis=-1), (1, RET_HEADS))
    sin_t = jnp.tile(jnp.concatenate([sin, -sin], axis=-1), (1, RET_HEADS))
    return cos_t, sin_t


def _s5_disc_args(small):
    g, s, ch = S5_GROUPS, S5_STATE, S5_GROUP_CH
    return (small["s5_a_re"], small["s5_a_im"], small["s5_log_dt"][:, :, None],
            small["s5_b_re"].reshape(DEPTH, g, s * ch), small["s5_b_im"].reshape(DEPTH, g, s * ch))


def _s5_mats(small):
    g, s, ch = S5_GROUPS, S5_STATE, S5_GROUP_CH
    lr, li, bbr, bbi = _s5_disc(*_s5_disc_args(small))
    lam = jnp.concatenate([lr.reshape(DEPTH, 8, 128), li.reshape(DEPTH, 8, 128)], axis=1)
    blocks = lambda a: _block_diag(a.astype(_MXU).transpose(0, 1, 3, 2))
    wb = jnp.concatenate([blocks(b.reshape(DEPTH, g, s, ch)) for b in (bbr, bbi)], axis=2)
    wc = jnp.concatenate([blocks(c) for c in (small["s5_c_re"], -small["s5_c_im"])], axis=1)
    return lam, wb, wc


def _s5_param_grads(small, dwb, dwc, dlam):
    g, s, ch = S5_GROUPS, S5_STATE, S5_GROUP_CH
    dc = [_diag_blocks(m, g).transpose(0, 1, 3, 2).reshape(DEPTH, g, ch * s) for m in (dwc[:, :S5_CH], dwc[:, S5_CH:])]
    dbb = [_diag_blocks(m, g).transpose(0, 1, 3, 2).reshape(DEPTH, g, s * ch) for m in (dwb[:, :, :S5_CH], dwb[:, :, S5_CH:])]
    dar, dai, dldt, dbr, dbi = _s5_disc_bwd(*_s5_disc_args(small), dlam[:, :8].reshape(DEPTH, g, s),
                                            dlam[:, 8:].reshape(DEPTH, g, s), dbb[0], dbb[1])
    return dict(s5_a_re=dar, s5_a_im=dai, s5_log_dt=dldt.reshape(DEPTH, g), s5_b_re=dbr, s5_b_im=dbi, s5_c_re=dc[0],
                s5_c_im=-dc[1])


_DENSE = ("s5_b_re", "s5_b_im", "s5_c_re", "s5_c_im")


def _layer_fwd(x, p, rope, ride=None, late=False):
    L = x.shape[0]
    cos_t, sin_t, ret_tabs = rope
    s = {"x": x}
    proj, h = _norm_inproj(x, p["norm_w"], p["w_in"])
    s["proj"], s["h"] = proj, h
    qa, ka, kat, vt = _fox_prep(proj, _fox_cumsum(proj, p["b_f"]))
    yf, lse, landed = _fox_fwd(qa, ka, vt, ride)
    s.update(qa=qa, ka=ka, kat=kat, lse=lse, yf=yf)
    if late:
        p["w_glu"], p["w_out"] = _gathered_rows(landed[-2]), _gathered_rows(landed[-1])
        landed = landed[:-2]
    xs, ypre, ys = _s5_fwd(proj, p["wb"], p["wc"], p["lam"], p["d"], p["w_glu"])
    s.update(xs=xs, ypre=ypre, ys=ys)
    o_pre, yr, states = _ret_fwd(proj, cos_t, sin_t, ret_tabs, p["gn_w"])
    s.update(o_pre=o_pre, yr=yr, states=states)
    return _gate_out(yf, ys, yr, proj, x, p["w_out"]), s, landed


def _layer_bwd(dxn, s, p, rope, ride=None, early=False):
    L = dxn.shape[0]
    cos_t, sin_t, ret_tabs = rope
    g = {}
    proj = s["proj"]
    dyf, dys, dyr, dgate, g["w_out"] = _gate_out_bwd(dxn, p["w_out"], s["yf"], s["ys"], s["yr"], proj)
    drq, drk, drv, dgn = _ret_bwd(proj, cos_t, sin_t, ret_tabs, p["gn_w"], s["o_pre"], dyr, s["states"])
    g["ret_gn_w"] = dgn.reshape(D_RET)
    dsu, g["wb"], g["wc"], g["lam"], dd, g["s5_w_glu"] = _s5_bwd(proj, s["ypre"], dys, s["xs"], p["wb"], p["wc"], p["lam"],
                                                                 p["d"], p["w_glu"])
    g["s5_d"] = dd.reshape(D_S5)
    if early:
        ride = (ride[0] + _row_slots(g), ride[1] + [True, True])
    dqt, dkraw, dv, landed = _fox_bwd(s["qa"], s["ka"], s["kat"], proj, dyf, s["yf"], s["lse"], ride)
    dq, dk, dfl, dbf = _fox_post_bwd(dqt, dkraw, proj, p["b_f"])
    g["fox_b_f"] = dbf[0, :FOX_HEADS]
    pieces = [dgate, dq, dk, dv, dsu, drq, drk, drv, dfl]
    dx, dnw, g["w_in"] = _inproj_bwd(pieces, p["w_in"], s["x"], p["norm_w"], dxn, s["h"])
    g["norm_w"] = dnw.reshape(D_MODEL)
    return dx, g, landed


def _stacked_params(small):
    lam, wb, wc = _s5_mats(small)
    row = lambda a: a[:, None, :]
    return dict(norm_w=row(small["norm_w"]), b_f=row(jnp.pad(small["fox_b_f"], ((0, 0), (0, PAIR - FOX_HEADS)))),
                lam=lam, wb=wb, wc=wc, d=row(small["s5_d"]), gn_w=row(small["ret_gn_w"]))


def _layer_params(l, w_in_p, w_glu, w_out, stacked):
    return dict({k: (a, l) for k, a in stacked.items()}, w_in=w_in_p, w_glu=w_glu, w_out=w_out)


_SHARDED = ("w_in", "s5_w_glu", "w_out")
_WIRE = jnp.bfloat16


_RUNS = ((2568, 3592, O_GATE), (0, 1536, O_FQ), (1544, 2568, O_SU), (1536, 1544, O_FL))


def _shard_pieces():
    out = []
    for a, b, pad in _RUNS:
        while a < b:
            j = a // W_SHARD
            e = min(b, (j + 1) * W_SHARD)
            out.append((j, a - j * W_SHARD, e - j * W_SHARD, pad))
            pad, a = pad + e - a, e
    return out


def _gathered_w_in(g_in):
    cols = [g_in[j, :, a:e] for j, a, e, _ in _shard_pieces()]
    cols.append(jnp.zeros((D_MODEL, D_INP - O_FL - FOX_HEADS), g_in.dtype))
    return jnp.concatenate(cols, axis=1)


def _gathered_rows(g):
    return g.reshape(-1, g.shape[-1])


def _w_in_slots(g):
    w_in = g["w_in"].astype(_WIRE)
    slots = []
    for j in range(N_DEV):
        mine = sorted((a, e, pad) for jj, a, e, pad in _shard_pieces() if jj == j)
        slots.append(jnp.concatenate([w_in[:, pad:pad + e - a] for a, e, pad in mine], axis=1))
    return jnp.stack(slots)


def _row_slots(g):
    return [g["s5_w_glu"].reshape(N_DEV, D_S5 // N_DEV, D_S5).astype(_WIRE),
            g["w_out"].reshape(N_DEV, D_MODEL // N_DEV, D_MODEL).astype(_WIRE)]


def _step_grads(x, tgt, small, full=None, shards=None):
    L = x.shape[0]
    rope = _rope_tables(L) + (_ret_tables(min(TQ, L)),)
    stacked = _stacked_params(small)
    if shards is not None:
        nxt = (_gathered_w_in(_exchange([shards[0][0]], [False], "gather_layer0")[0]), None, None)
    saved, params = [], []
    for l in range(DEPTH):
        weights = nxt if shards is not None else tuple(f[l] for f in full)
        ride = None
        if shards is not None:
            arrs = [s[l + 1] for s in shards] if l + 1 < DEPTH else []
            arrs += [shards[1][0], shards[2][0]] if l == 0 else []
            ride = (arrs, [False] * len(arrs)) if arrs else None
        params.append(_layer_params(l, *weights, stacked))
        x, s, landed = _layer_fwd(x, params[l], rope, ride, late=shards is not None and l == 0)
        if landed:
            nxt = (_gathered_w_in(landed[0]), _gathered_rows(landed[1]), _gathered_rows(landed[2]))
        saved.append(s)
    loss, dx, dfw = _final_loss(x, small["final_norm_w"][None], tgt)
    grads, partials, waiting = [None] * DEPTH, [None] * DEPTH, None
    for l in reversed(range(DEPTH)):
        ride = (waiting, [True] * len(waiting)) if waiting is not None else None
        dx, grads[l], landed = _layer_bwd(dx, saved[l], params[l], rope, ride, early=ride is not None and l == 0)
        if waiting is not None:
            partials[l + 1] = landed[:3]
        if shards is not None:
            waiting = [_w_in_slots(grads[l])] + _row_slots(grads[l])
    stack = lambda n: jnp.stack([g[n] for g in grads])
    small_g = {n: stack(n) for n in ("norm_w", "fox_b_f", "s5_d", "ret_gn_w")}
    small_g.update(_s5_param_grads(small, stack("wb"), stack("wc"), stack("lam")), final_norm_w=dfw)
    if shards is None:
        return loss, dx, grads, small_g
    last = _exchange([waiting[0]] + [small_g[n].astype(_WIRE) for n in _SMALL], [True] + [False] * len(_SMALL),
                     "exchange_layer0")
    partials[0] = [last[0]] + landed[3:]
    return loss, dx, grads, small_g, partials, dict(zip(_SMALL, last[1:]))


_MESH = pl.DeviceIdType.MESH
_ANY = pl.BlockSpec(memory_space=pl.ANY)


def _me_and_peers():
    x, y, c = lax.axis_index("x"), lax.axis_index("y"), lax.axis_index("c")
    flip = lambda a, bit: (1 - a) if bit else a
    peers = []
    for r in range(1, N_DEV):
        px, py, pc = flip(x, (r >> 2) & 1), flip(y, (r >> 1) & 1), flip(c, r & 1)
        peers.append(((px, py, pc), 4 * px + 2 * py + pc))
    return 4 * x + 2 * y + c, peers


def _exchange_copies(srcs, dsts, sems, scatter):
    send_sems, recv_sems, local_sems = sems
    me, peers = _me_and_peers()
    pick = lambda t, to: srcs[t].at[to] if scatter[t] else srcs[t]
    own = [pltpu.make_async_copy(pick(t, me), dsts[t].at[me], local_sems.at[t]) for t in range(len(srcs))]
    sends, waits = [], []
    for r, (dev, idx) in enumerate(peers):
        for t in range(len(srcs)):
            for land, out in ((me, sends), (idx, waits)):
                out.append(pltpu.make_async_remote_copy(pick(t, idx), dsts[t].at[land], send_sems.at[t, r], recv_sems.at[t, r],
                                                        device_id=dev, device_id_type=_MESH))
    return own, sends, waits


def _exchange_start(srcs, dsts, sems, scatter):
    own, sends, _ = _exchange_copies(srcs, dsts, sems, scatter)
    for cp in own + sends:
        cp.start()


def _exchange_wait(srcs, dsts, sems, scatter):
    own, _, waits = _exchange_copies(srcs, dsts, sems, scatter)
    for cp in waits + own:
        cp.wait()


def _exchange_shapes(arrs, scatter):
    outs = [jax.ShapeDtypeStruct(a.shape if sc else (N_DEV,) + a.shape, a.dtype) for a, sc in zip(arrs, scatter)]
    n = len(arrs)
    sems = [pltpu.SemaphoreType.DMA((n, N_DEV - 1)), pltpu.SemaphoreType.DMA((n, N_DEV - 1)), pltpu.SemaphoreType.DMA((n,))]
    return outs, sems


def _exchange(arrs, scatter, name):
    n = len(arrs)

    def body(*refs):
        _exchange_start(refs[:n], refs[n:2 * n], refs[2 * n:], scatter)
        _exchange_wait(refs[:n], refs[n:2 * n], refs[2 * n:], scatter)

    outs, sems = _exchange_shapes(arrs, scatter)
    return _pallas(body, out_shape=tuple(outs), in_specs=[_ANY] * n, out_specs=tuple([_ANY] * n), scratch_shapes=sems,
                   name=name)(*arrs)


def _riding(body, n_in, n_out, ride, is_first, is_last):
    if ride is None:
        return body, [], [], [], []
    arrs, scatter = ride
    n = len(arrs)
    outs, sems = _exchange_shapes(arrs, scatter)

    def wrapped(*refs):
        ins, srcs = refs[:n_in], refs[n_in:n_in + n]
        own_outs, dsts = refs[n_in + n:n_in + n + n_out], refs[n_in + n + n_out:n_in + 2 * n + n_out]
        scratch, ex_sems = refs[n_in + 2 * n + n_out:-3], refs[-3:]

        @pl.when(is_first())
        def _():
            _exchange_start(srcs, dsts, ex_sems, scatter)

        body(*ins, *own_outs, *scratch)

        @pl.when(is_last())
        def _():
            _exchange_wait(srcs, dsts, ex_sems, scatter)

    return wrapped, list(arrs), [_ANY] * n, outs, sems


def _adamw_body(p_ref, w_ref, m_ref, v_ref, g_ref, d_ref, nm_ref, nv_ref):
    g = p_ref[0].astype(F32)
    for i in range(1, N_DEV):
        g = g + p_ref[i].astype(F32)
    nm = ADAM_B1 * m_ref[...] + (1.0 - ADAM_B1) * g
    nv = ADAM_B2 * v_ref[...] + (1.0 - ADAM_B2) * jnp.square(g)
    m_hat = nm / (1.0 - ADAM_B1 ** ADAM_STEP)
    v_hat = nv / (1.0 - ADAM_B2 ** ADAM_STEP)
    g_ref[...] = g
    d_ref[...] = -ADAM_LR * (m_hat / (jnp.sqrt(v_hat) + ADAM_EPS) + ADAM_WD * w_ref[...])
    nm_ref[...] = nm
    nv_ref[...] = nv


def _adamw(parts, w, m, v, name):
    n, nb, rows, cols = parts.shape
    tm = next(t for t in (256, 128, 64, 32, 16) if rows % t == 0)

    def body(*refs):
        _adamw_body(*refs)

    row = pl.BlockSpec((None, tm, cols), lambda b, i: (b, i, 0))
    return _pallas(body, out_shape=(jax.ShapeDtypeStruct((nb, rows, cols), F32),) * 4, grid=(nb, rows // tm),
                   in_specs=[pl.BlockSpec((n, None, tm, cols), lambda b, i: (0, b, i, 0)), row, row, row],
                   out_specs=(row,) * 4, name=name)(parts, w, m, v)


def _adamw_whole(parts, w, m, v, name):
    def body(*refs):
        _adamw_body(*refs)

    if w.ndim == 2:
        grid = (1,)
        slab = pl.BlockSpec(w.shape, lambda b: (0, 0))
        part = pl.BlockSpec(parts.shape, lambda b: (0, 0, 0))
    else:
        grid, rest = (w.shape[0],), w.shape[1:]
        zeros = (0,) * len(rest)
        slab = pl.BlockSpec((None,) + rest, lambda b: (b,) + zeros)
        part = pl.BlockSpec((N_DEV, None) + rest, lambda b: (0, b) + zeros)
    return _pallas(body, out_shape=(jax.ShapeDtypeStruct(w.shape, F32),) * 4, grid=grid,
                   in_specs=[part, slab, slab, slab], out_specs=(slab,) * 4, name=name)(parts, w, m, v)


_WEIGHTS = ("norm_w", "w_in", "fox_b_f", "s5_a_re", "s5_a_im", "s5_b_re", "s5_b_im", "s5_c_re", "s5_c_im", "s5_d",
            "s5_log_dt", "s5_w_glu", "ret_gn_w", "w_out", "final_norm_w")
_SMALL = tuple(n for n in _WEIGHTS if n not in _SHARDED)


def kernel(x, norm_w, w_in, fox_b_f, s5_a_re, s5_a_im, s5_b_re, s5_b_im, s5_c_re, s5_c_im, s5_d, s5_log_dt, s5_w_glu, ret_gn_w, w_out, final_norm_w, loss_target, m_norm_w, m_w_in, m_fox_b_f, m_s5_a_re, m_s5_a_im, m_s5_b_re, m_s5_b_im, m_s5_c_re, m_s5_c_im, m_s5_d, m_s5_log_dt, m_s5_w_glu, m_ret_gn_w, m_w_out, m_final_norm_w, v_norm_w, v_w_in, v_fox_b_f, v_s5_a_re, v_s5_a_im, v_s5_b_re, v_s5_b_im, v_s5_c_re, v_s5_c_im, v_s5_d, v_s5_log_dt, v_s5_w_glu, v_ret_gn_w, v_w_out, v_final_norm_w):
    w = dict(norm_w=norm_w, w_in=w_in, fox_b_f=fox_b_f, s5_a_re=s5_a_re, s5_a_im=s5_a_im, s5_b_re=s5_b_re, s5_b_im=s5_b_im,
             s5_c_re=s5_c_re, s5_c_im=s5_c_im, s5_d=s5_d, s5_log_dt=s5_log_dt, s5_w_glu=s5_w_glu, ret_gn_w=ret_gn_w,
             w_out=w_out, final_norm_w=final_norm_w)
    m = dict(norm_w=m_norm_w, w_in=m_w_in, fox_b_f=m_fox_b_f, s5_a_re=m_s5_a_re, s5_a_im=m_s5_a_im, s5_b_re=m_s5_b_re,
             s5_b_im=m_s5_b_im, s5_c_re=m_s5_c_re, s5_c_im=m_s5_c_im, s5_d=m_s5_d, s5_log_dt=m_s5_log_dt,
             s5_w_glu=m_s5_w_glu, ret_gn_w=m_ret_gn_w, w_out=m_w_out, final_norm_w=m_final_norm_w)
    v = dict(norm_w=v_norm_w, w_in=v_w_in, fox_b_f=v_fox_b_f, s5_a_re=v_s5_a_re, s5_a_im=v_s5_a_im, s5_b_re=v_s5_b_re,
             s5_b_im=v_s5_b_im, s5_c_re=v_s5_c_re, s5_c_im=v_s5_c_im, s5_d=v_s5_d, s5_log_dt=v_s5_log_dt,
             s5_w_glu=v_s5_w_glu, ret_gn_w=v_ret_gn_w, w_out=v_w_out, final_norm_w=v_final_norm_w)

    small = {n: w[n] for n in _SMALL}
    loss, dx, _, _, partials, r_small = _step_grads(x[0], loss_target[0], small, shards=[w[n].astype(_MXU) for n in _SHARDED])

    res = {}
    for t, n in enumerate(_SHARDED):
        res[n] = _adamw(jnp.stack([partials[l][t] for l in range(DEPTH)], axis=1), w[n], m[n], v[n], "adamw_" + n)
    for n in _SMALL:
        shape = w[n].shape
        view = (1,) + shape if len(shape) == 1 else shape[:2] + (-1,) if n in _DENSE else shape
        outs = _adamw_whole(r_small[n], *[d[n].reshape(view) for d in (w, m, v)], "adamw_" + n)
        res[n] = [o.reshape(shape) for o in outs]

    loss = lax.psum(loss[0, 0], ("x", "y", "c"))
    return (loss, dx[None], *[res[n][0] for n in _WEIGHTS], *[res[n][1] for n in _WEIGHTS],
            *[res[n][2] for n in _WEIGHTS], *[res[n][3] for n in _WEIGHTS])
```

```python
import math

import jax
import jax.numpy as jnp
from jax import lax
from jax.experimental import pallas as pl
from jax.experimental.pallas import tpu as pltpu

F32 = jnp.float32
_MXU = jnp.bfloat16
_HI = lax.Precision.HIGHEST

N_DEV = 8
DEPTH = 4
D_MODEL = 1024
HEAD_DIM = 64
D_FOX = 512
FOX_HEADS = 8
D_S5 = 256
S5_GROUPS = 16
S5_GROUP_CH = 16
S5_STATE = 64
S5_CH = S5_GROUPS * S5_STATE
D_RET = 256
RET_HEADS = 4
CHUNK = 64
ROPE_BASE = 10000.0
EPS = 1e-6
D_IN = 3592
D_INP = 3712
W_SHARD = D_IN // N_DEV
O_GATE, O_FQ, O_FK, O_FV, O_SU, O_RQ, O_RK, O_RV, O_FL = 0, 1024, 1536, 2048, 2560, 2816, 3072, 3328, 3584

ADAM_LR, ADAM_B1, ADAM_B2, ADAM_EPS, ADAM_WD, ADAM_STEP = 0.001, 0.9, 0.999, 1e-08, 0.01, 10

TM = 256
TMB = 512
TQ = 512
TS = 512
NEG = -1e30
VMEM_BIG = 56 * 1024 * 1024


def _pallas(body, **kw):
    return pl.pallas_call(body, **kw)


def _whole(shape):
    n = len(shape)
    return pl.BlockSpec(shape, lambda *_: (0,) * n)


def _rows(tm, width, col=0):
    return pl.BlockSpec((tm, width), lambda i: (i, col))


def _of_layer(param):
    a, l = param
    return a, pl.BlockSpec((None,) + a.shape[1:], lambda *_: (l,) + (0,) * (a.ndim - 1))


def _dot(a, b, dims=(((1,), (0,)), ((), ()))):
    return lax.dot_general(a.astype(_MXU), b.astype(_MXU), dims, preferred_element_type=F32)


_NT = (((1,), (1,)), ((), ()))
_TN = (((0,), (0,)), ((), ()))


def _norm_inproj(x, g, w):
    L = x.shape[0]

    def body(x_ref, g_ref, w_ref, p_ref, h_ref):
        xv = x_ref[...]
        r = lax.rsqrt(jnp.mean(xv * xv, axis=-1, keepdims=True) + EPS)
        h = (xv * r * g_ref[...]).astype(_MXU)
        h_ref[...] = h
        p_ref[...] = _dot(h, w_ref[...])

    tm = min(TMB, L)
    g, g_spec = _of_layer(g)
    return _pallas(body, out_shape=(jax.ShapeDtypeStruct((L, D_INP), F32), jax.ShapeDtypeStruct((L, D_MODEL), _MXU)),
                   grid=(L // tm,),
                   in_specs=[_rows(tm, D_MODEL), g_spec,
                             pl.BlockSpec((D_MODEL, D_INP), lambda i: (0, 0), pipeline_mode=pl.Buffered(1))],
                   out_specs=(_rows(tm, D_INP), _rows(tm, D_MODEL)), name="norm_inproj",
                   compiler_params=pltpu.CompilerParams(vmem_limit_bytes=VMEM_BIG))(x, g, w)


def _rms_bwd(xv, g, dh):
    r = lax.rsqrt(jnp.mean(xv * xv, axis=-1, keepdims=True) + EPS)
    xh = xv * r
    dg = jnp.sum(dh * xh, axis=0, keepdims=True)
    dxh = dh * g
    dx = r * (dxh - xh * jnp.mean(dxh * xh, axis=-1, keepdims=True))
    return dx, dg


def _inproj_bwd(pieces, w, x, g, dres, h):
    L = x.shape[0]
    n = len(pieces)
    nb = L // TM

    def body(*refs):
        w_ref, x_ref, g_ref, dr_ref, h_ref, dx_ref, dg_ref, dw_ref, acc_sc = refs[n:]
        step = pl.program_id(0)

        @pl.when(step == 0)
        def _():
            acc_sc[...] = jnp.zeros((D_MODEL, D_INP), F32)
            dg_ref[...] = jnp.zeros((1, D_MODEL), F32)

        hv = h_ref[...]
        dh = jnp.zeros((TM, D_MODEL), F32)
        off = 0
        for r in refs[:n]:
            cols = slice(off, off + r.shape[1])
            off += r.shape[1]
            piece = r[...].astype(_MXU)
            dh = dh + _dot(piece, w_ref[:, cols], _NT)
            acc_sc[:, cols] += _dot(hv, piece, _TN)
        dx, dg = _rms_bwd(x_ref[...], g_ref[...], dh)
        dx_ref[...] = dx + dr_ref[...]
        dg_ref[...] += dg

        @pl.when(step == nb - 1)
        def _():
            dw_ref[...] = acc_sc[...].astype(_WIRE)

    resident = pl.BlockSpec((D_MODEL, D_INP), lambda i: (0, 0), pipeline_mode=pl.Buffered(1))
    g, g_spec = _of_layer(g)
    return _pallas(body, out_shape=(jax.ShapeDtypeStruct((L, D_MODEL), F32), jax.ShapeDtypeStruct((1, D_MODEL), F32),
                                    jax.ShapeDtypeStruct((D_MODEL, D_INP), _WIRE)),
                   grid=(nb,),
                   in_specs=[_rows(TM, p.shape[1]) for p in pieces]
                   + [resident, _rows(TM, D_MODEL), g_spec, _rows(TM, D_MODEL), _rows(TM, D_MODEL)],
                   out_specs=(_rows(TM, D_MODEL), _whole((1, D_MODEL)), resident),
                   scratch_shapes=[pltpu.VMEM((D_MODEL, D_INP), F32)], name="inproj_bwd",
                   compiler_params=pltpu.CompilerParams(vmem_limit_bytes=VMEM_BIG))(*pieces, w, x, g, dres, h)


PAIR = 2 * HEAD_DIM
N_AUX = 3


def _own(shape, h):
    return lax.broadcasted_iota(jnp.int32, shape, len(shape) - 1) // HEAD_DIM == h


def _split(x):
    parts = []
    for _ in range(N_AUX):
        part = x.astype(_MXU)
        parts.append(part)
        x = x - part.astype(F32)
    return parts


def _exact_dot(a, b, exact):
    if exact == "b":
        return sum(_dot(part, b) for part in _split(a))
    return sum(_dot(a, part) for part in _split(b))


def _tri(n, lower):
    r = lax.broadcasted_iota(jnp.int32, (n, n), 0)
    c = lax.broadcasted_iota(jnp.int32, (n, n), 1)
    return jnp.where(r >= c if lower else r <= c, 1.0, 0.0).astype(F32)


def _fox_cumsum(proj, b):
    L = proj.shape[0]

    def body(fl_ref, b_ref, c_ref, carry_sc):
        @pl.when(pl.program_id(0) == 0)
        def _():
            carry_sc[...] = jnp.zeros((1, PAIR), F32)

        lane = lax.broadcasted_iota(jnp.int32, (TM, PAIR), 1)
        lf = jnp.where(lane < FOX_HEADS, jax.nn.log_sigmoid(fl_ref[...] + b_ref[...]), 0.0)
        cs = _exact_dot(_tri(TM, True), lf, "a") + carry_sc[...]
        c_ref[...] = cs
        carry_sc[...] = cs[TM - 1:TM, :]

    b, b_spec = _of_layer(b)
    return _pallas(body, out_shape=jax.ShapeDtypeStruct((L, PAIR), F32), grid=(L // TM,),
                   in_specs=[_rows(TM, PAIR, O_FL // PAIR), b_spec], out_specs=_rows(TM, PAIR),
                   scratch_shapes=[pltpu.VMEM((1, PAIR), F32)], name="fox_cumsum")(proj, b)


def _fox_prep(proj, c):
    L = proj.shape[0]

    def body(q_ref, k_ref, v_ref, c_ref, qa_ref, ka_ref, kat_ref, vt_ref):
        lane = lax.broadcasted_iota(jnp.int32, (TM, PAIR), 1)
        cv = c_ref[...]
        for p in range(FOX_HEADS // 2):
            cols = slice(PAIR * p, PAIR * (p + 1))
            q2, k2 = q_ref[:, cols], k_ref[:, cols]
            vt_ref[p] = v_ref[:, cols].T.astype(_MXU)
            for e in range(2):
                h = 2 * p + e
                own = lane // HEAD_DIM == e
                a = lane - (HEAD_DIM if e == 0 else 0)
                rest = jnp.broadcast_to(cv[:, h:h + 1], (TM, PAIR))
                aux_q = jnp.where((a >= N_AUX) & (a < 2 * N_AUX), 1.0, 0.0)
                aux_k = jnp.where((a >= 0) & (a < N_AUX), 1.0, 0.0)
                for n in range(N_AUX):
                    part = rest.astype(_MXU).astype(F32)
                    rest = rest - part
                    aux_q = jnp.where(a == n, part, aux_q)
                    aux_k = jnp.where(a == N_AUX + n, -part, aux_k)
                ka = jnp.where(own, k2, aux_k)
                qa_ref[h] = jnp.where(own, q2 * (1.0 / math.sqrt(HEAD_DIM)), aux_q).astype(_MXU)
                ka_ref[h] = ka.astype(_MXU)
                kat_ref[h] = ka.T.astype(_MXU)

    hl = jax.ShapeDtypeStruct((FOX_HEADS, L, PAIR), _MXU)
    nat = lambda o: _rows(TM, D_FOX, o // D_FOX)
    rows = pl.BlockSpec((FOX_HEADS, TM, PAIR), lambda i: (0, i, 0))
    return _pallas(
        body, out_shape=(hl, hl, jax.ShapeDtypeStruct((FOX_HEADS, PAIR, L), _MXU),
                         jax.ShapeDtypeStruct((FOX_HEADS // 2, PAIR, L), _MXU)),
        grid=(L // TM,), in_specs=[nat(O_FQ), nat(O_FK), nat(O_FV), _rows(TM, PAIR)],
        out_specs=(rows, rows, pl.BlockSpec((FOX_HEADS, PAIR, TM), lambda i: (0, 0, i)),
                   pl.BlockSpec((FOX_HEADS // 2, PAIR, TM), lambda i: (0, 0, i))),
        name="fox_prep")(proj, proj, proj, c)


def _key_le_query(tq):
    return lax.broadcasted_iota(jnp.int32, (tq, tq), 0) <= lax.broadcasted_iota(jnp.int32, (tq, tq), 1)


def _grid_ends(n0, n1):
    first = lambda: (pl.program_id(0) == 0) & (pl.program_id(1) == 0)
    last = lambda: (pl.program_id(0) == n0 - 1) & (pl.program_id(1) == n1 - 1)
    return first, last


def _fox_fwd(qa, ka, vt, ride=None):
    H, L, _ = qa.shape
    tq = min(TQ, L)
    nq = L // tq

    def body(qa_ref, ka_ref, vt_ref, o_ref, lse_ref, m_sc, l_sc, acc_sc):
        i = pl.program_id(1)
        m_sc[...] = jnp.full((2, 1, tq), NEG, F32)
        l_sc[...] = jnp.zeros((2, 1, tq), F32)
        acc_sc[...] = jnp.zeros((2, HEAD_DIM, tq), F32)

        def block(j, nk, masked):
            keys = pl.ds(pl.multiple_of(j * tq, tq), nk * tq)
            vt_blk = vt_ref[:, keys]
            sts = [_dot(ka_ref[e, keys, :], qa_ref[e], _NT) for e in range(2)]
            pts, alphas = [], []
            for e in range(2):
                st = jnp.where(_key_le_query(tq), sts[e], NEG) if masked else sts[e]
                m_prev = m_sc[e]
                m_new = jnp.maximum(m_prev, jnp.max(st, axis=0, keepdims=True))
                alphas.append(jnp.exp(m_prev - m_new))
                pt = jnp.exp(st - m_new)
                l_sc[e] = alphas[e] * l_sc[e] + jnp.sum(pt, axis=0, keepdims=True)
                m_sc[e] = m_new
                pts.append(pt.astype(_MXU))
            for e in range(2):
                acc_sc[e] = alphas[e] * acc_sc[e] + _dot(vt_blk[HEAD_DIM * e:HEAD_DIM * (e + 1)], pts[e])

        def two_blocks(jj, carry):
            block(2 * jj, 2, False)
            return carry

        lax.fori_loop(0, i // 2, two_blocks, 0)

        @pl.when(i % 2 == 1)
        def _():
            block(i - 1, 1, False)

        block(i, 1, True)
        o_ref[...] = jnp.concatenate([acc_sc[0] / l_sc[0], acc_sc[1] / l_sc[1]], axis=0).T
        for e in range(2):
            lse_ref[e] = m_sc[e] + jnp.log(l_sc[e])

    body, ex_in, ex_specs, ex_out, ex_sems = _riding(body, 3, 2, ride, *_grid_ends(H // 2, nq))
    res = _pallas(
        body, out_shape=(jax.ShapeDtypeStruct((L, D_FOX), F32), jax.ShapeDtypeStruct((H, 1, L), F32), *ex_out),
        grid=(H // 2, nq),
        in_specs=[pl.BlockSpec((2, tq, PAIR), lambda p, i: (p, i, 0)), pl.BlockSpec((2, L, PAIR), lambda p, i: (p, 0, 0)),
                  pl.BlockSpec((None, PAIR, L), lambda p, i: (p, 0, 0)), *ex_specs],
        out_specs=(pl.BlockSpec((tq, PAIR), lambda p, i: (i, p)), pl.BlockSpec((2, 1, tq), lambda p, i: (p, 0, i)),
                   *ex_specs),
        scratch_shapes=[pltpu.VMEM((2, 1, tq), F32), pltpu.VMEM((2, 1, tq), F32), pltpu.VMEM((2, HEAD_DIM, tq), F32), *ex_sems],
        name="fox_fwd" if ride is None else "fox_fwd_gather")(qa, ka, vt, *ex_in)
    return res[0], res[1], list(res[2:])


def _fox_bwd(qa, ka, kat, proj, do, o, lse, ride=None):
    H, L, _ = qa.shape
    tq = min(TQ, L)
    nq = L // tq

    def body(qa_ref, ka_ref, kat_ref, v_ref, do_ref, o_ref, lse_ref, dqt_ref, dk_ref, dv_ref, delta_sc, dk_sc, dv_sc):
        j = pl.program_id(1)

        @pl.when(j == 0)
        def _():
            head_rows = (lax.broadcasted_iota(jnp.int32, (8, PAIR), 1) // HEAD_DIM
                         == lax.broadcasted_iota(jnp.int32, (8, PAIR), 0)).astype(F32)
            delta_sc[...] = lax.dot_general(head_rows, do_ref[...] * o_ref[...], _NT, precision=_HI,
                                            preferred_element_type=F32)
            dqt_ref[...] = jnp.zeros((2, PAIR, L), F32)

        dk_sc[...] = jnp.zeros((2, tq, PAIR), F32)
        dv_sc[...] = jnp.zeros((tq, PAIR), F32)
        vb = v_ref[...]

        def block(i, masked, nqb=1):
            qs = pl.ds(pl.multiple_of(i * tq, tq), nqb * tq)
            dob = do_ref[qs, :]
            for e in range(2):
                own = _own((nqb * tq, PAIR), e)
                qh = qa_ref[e, qs, :]
                pt = jnp.exp(_dot(ka_ref[e], qh, _NT) - lse_ref[e, :, qs])
                if masked:
                    pt = jnp.where(_key_le_query(tq), pt, 0.0)
                dv_sc[...] += _dot(pt, jnp.where(own, dob, 0.0))
                dpt = _dot(jnp.where(_own((tq, PAIR), e), vb, 0.0), dob, _NT)
                ds = (pt * (dpt - delta_sc[e:e + 1, qs])).astype(_MXU)
                dk_sc[e] += _dot(ds, qh)
                dqt_ref[e, :, qs] += _dot(kat_ref[e], ds)

        def two_blocks(ii, carry):
            block(j + 1 + 2 * ii, False, 2)
            return carry

        block(j, True)
        later = nq - 1 - j
        lax.fori_loop(0, later // 2, two_blocks, 0)

        @pl.when(later % 2 == 1)
        def _():
            block(nq - 1, False)

        dk_ref[...] = dk_sc[...]
        dv_ref[...] = dv_sc[...]

    nat = pl.BlockSpec((L, PAIR), lambda p, j: (0, p))
    body, ex_in, ex_specs, ex_out, ex_sems = _riding(body, 7, 3, ride, *_grid_ends(H // 2, nq))
    res = _pallas(
        body, out_shape=(jax.ShapeDtypeStruct((H, PAIR, L), F32), jax.ShapeDtypeStruct((H, L, PAIR), F32),
                         jax.ShapeDtypeStruct((L, D_FOX), F32), *ex_out),
        grid=(H // 2, nq),
        in_specs=[pl.BlockSpec((2, L, PAIR), lambda p, j: (p, 0, 0)), pl.BlockSpec((2, tq, PAIR), lambda p, j: (p, j, 0)),
                  pl.BlockSpec((2, PAIR, tq), lambda p, j: (p, 0, j)),
                  pl.BlockSpec((tq, PAIR), lambda p, j: (j, O_FV // PAIR + p)), nat, nat,
                  pl.BlockSpec((2, 1, L), lambda p, j: (p, 0, 0)), *ex_specs],
        out_specs=(pl.BlockSpec((2, PAIR, L), lambda p, j: (p, 0, 0)), pl.BlockSpec((2, tq, PAIR), lambda p, j: (p, j, 0)),
                   pl.BlockSpec((tq, PAIR), lambda p, j: (j, p)), *ex_specs),
        scratch_shapes=[pltpu.VMEM((8, L), F32), pltpu.VMEM((2, tq, PAIR), F32), pltpu.VMEM((tq, PAIR), F32), *ex_sems],
        name="fox_bwd" if ride is None else "fox_bwd_exchange",
        compiler_params=pltpu.CompilerParams(vmem_limit_bytes=VMEM_BIG))(qa, ka, kat, proj, do, o, lse, *ex_in)
    return res[0], res[1], res[2], list(res[3:])


def _fox_post_bwd(dqt, dkraw, proj, b):
    L = proj.shape[0]
    nb = L // TM

    def body(dqt_ref, dkr_ref, fl_ref, b_ref, dq_ref, dk_ref, dfl_ref, db_ref, carry_sc):
        first = pl.program_id(0) == 0

        @pl.when(first)
        def _():
            carry_sc[...] = jnp.zeros((1, PAIR), F32)

        lane = lax.broadcasted_iota(jnp.int32, (TM, PAIR), 1)
        rr = lax.broadcasted_iota(jnp.int32, (PAIR, PAIR), 0)
        cc = lax.broadcasted_iota(jnp.int32, (PAIR, PAIR), 1)
        dc = jnp.zeros((TM, PAIR), F32)
        for p in range(FOX_HEADS // 2):
            cols = slice(PAIR * p, PAIR * (p + 1))
            dqs = [dqt_ref[2 * p + e].T for e in range(2)]
            dks = [dkr_ref[2 * p + e] for e in range(2)]
            dq_ref[:, cols] = jnp.where(lane < HEAD_DIM, dqs[0], dqs[1]) * (1.0 / math.sqrt(HEAD_DIM))
            dk_ref[:, cols] = jnp.where(lane < HEAD_DIM, dks[0], dks[1])
            sums = jnp.zeros((TM, PAIR), F32)
            place = jnp.zeros((PAIR, PAIR), F32)
            for e in range(2):
                base = HEAD_DIM if e == 0 else 0
                sums = jnp.where(lane == base, dqs[e], jnp.where(lane == base + N_AUX, -dks[e], sums))
                place = jnp.where(((rr == base) | (rr == base + N_AUX)) & (cc == 2 * p + e), 1.0, place)
            dc = dc + _exact_dot(sums, place, "b")
        rs = _exact_dot(_tri(TM, False), dc, "a") + carry_sc[...]
        carry_sc[...] = rs[0:1, :]
        dfl = jnp.where(lane < FOX_HEADS, rs * jax.nn.sigmoid(-(fl_ref[...] + b_ref[...])), 0.0)
        dfl_ref[...] = dfl
        db = jnp.sum(dfl, axis=0, keepdims=True)

        @pl.when(first)
        def _():
            db_ref[...] = db

        @pl.when(jnp.logical_not(first))
        def _():
            db_ref[...] += db

    rev = lambda i: nb - 1 - i
    b, b_spec = _of_layer(b)
    nat = pl.BlockSpec((TM, D_FOX), lambda i: (rev(i), 0))
    return _pallas(
        body, out_shape=(jax.ShapeDtypeStruct((L, D_FOX), F32),) * 2
        + (jax.ShapeDtypeStruct((L, PAIR), F32), jax.ShapeDtypeStruct((1, PAIR), F32)),
        grid=(nb,),
        in_specs=[pl.BlockSpec((FOX_HEADS, PAIR, TM), lambda i: (0, 0, rev(i))),
                  pl.BlockSpec((FOX_HEADS, TM, PAIR), lambda i: (0, rev(i), 0)),
                  pl.BlockSpec((TM, PAIR), lambda i: (rev(i), O_FL // PAIR)), b_spec],
        out_specs=(nat, nat, pl.BlockSpec((TM, PAIR), lambda i: (rev(i), 0)), _whole((1, PAIR))),
        scratch_shapes=[pltpu.VMEM((1, PAIR), F32)], name="fox_post_bwd")(dqt, dkraw, proj, b)


def _s5_expand():
    r = lax.broadcasted_iota(jnp.int32, (S5_STATE, S5_STATE * S5_GROUP_CH), 0)
    c = lax.broadcasted_iota(jnp.int32, (S5_STATE, S5_STATE * S5_GROUP_CH), 1)
    return jnp.where(c // S5_GROUP_CH == r, 1.0, 0.0).astype(F32)


def _s5_disc_math(ar, ai, ldt, br, bi):
    dt = jnp.exp(ldt)
    mag = jnp.exp(ar * dt)
    lr = mag * jnp.cos(ai * dt)
    li = mag * jnp.sin(ai * dt)
    den = ar * ar + ai * ai
    fr = ((lr - 1.0) * ar + li * ai) / den
    fi = (li * ar - (lr - 1.0) * ai) / den
    e = _s5_expand()
    fre = jnp.dot(fr, e, precision=_HI, preferred_element_type=F32)
    fie = jnp.dot(fi, e, precision=_HI, preferred_element_type=F32)
    return lr, li, fre * br - fie * bi, fre * bi + fie * br


def _layer_blocks(arrs):
    return [pl.BlockSpec((None,) + a.shape[1:], lambda l: (l, 0, 0)) for a in arrs]


def _s5_disc(ar, ai, ldt, br, bi):
    def body(ar_ref, ai_ref, ldt_ref, br_ref, bi_ref, lr_ref, li_ref, bbr_ref, bbi_ref):
        lr, li, bbr, bbi = _s5_disc_math(ar_ref[...], ai_ref[...], ldt_ref[...], br_ref[...], bi_ref[...])
        lr_ref[...] = lr
        li_ref[...] = li
        bbr_ref[...] = bbr
        bbi_ref[...] = bbi

    ins = (ar, ai, ldt, br, bi)
    outs = (ar, ai, br, bi)
    return _pallas(body, out_shape=tuple(jax.ShapeDtypeStruct(a.shape, F32) for a in outs), grid=(DEPTH,),
                   in_specs=_layer_blocks(ins), out_specs=tuple(_layer_blocks(outs)), name="s5_disc")(*ins)


def _s5_disc_bwd(ar, ai, ldt, br, bi, dlr, dli, dbbr, dbbi):
    def body(ar_ref, ai_ref, ldt_ref, br_ref, bi_ref, dlr_ref, dli_ref, dbbr_ref, dbbi_ref,
             dar_ref, dai_ref, dldt_ref, dbr_ref, dbi_ref):
        _, vjp = jax.vjp(_s5_disc_math, ar_ref[...], ai_ref[...], ldt_ref[...], br_ref[...], bi_ref[...])
        dar, dai, dldt, dbr, dbi = vjp((dlr_ref[...], dli_ref[...], dbbr_ref[...], dbbi_ref[...]))
        dar_ref[...] = dar
        dai_ref[...] = dai
        dldt_ref[...] = dldt
        dbr_ref[...] = dbr
        dbi_ref[...] = dbi

    ins = (ar, ai, ldt, br, bi, dlr, dli, dbbr, dbbi)
    outs = (ar, ai, ldt, br, bi)
    return _pallas(body, out_shape=tuple(jax.ShapeDtypeStruct(a.shape, F32) for a in outs), grid=(DEPTH,),
                   in_specs=_layer_blocks(ins), out_specs=tuple(_layer_blocks(outs)), name="s5_disc_bwd")(*ins)


SLAB = 2 * S5_CH // 128
PITCH = 24


def _slab_rows(s, ts):
    return pl.ds(s, ts, stride=PITCH)


def _slab_pair(ref, s, ts):
    return jnp.concatenate([ref[_slab_rows(s, ts), :].astype(_MXU), ref[_slab_rows(s + 1, ts), :].astype(_MXU)], axis=-1)


def _s5_fwd(proj, wb, wc, lam, d, w_glu):
    L = proj.shape[0]
    ts = min(TS, L)

    def body(u_ref, wb_ref, wc_ref, lam_ref, d_ref, wg_ref, xs_ref, ypre_ref, ys_ref, b_sc, c_sc):
        @pl.when(pl.program_id(0) == 0)
        def _():
            c_sc[...] = jnp.zeros((SLAB, 128), F32)

        u = u_ref[...]
        ub = u.astype(_MXU)
        for s in range(0, SLAB, 2):
            b2 = _dot(ub, wb_ref[:, 128 * s:128 * (s + 2)])
            b_sc[_slab_rows(s, ts), :] = b2[:, :128]
            b_sc[_slab_rows(s + 1, ts), :] = b2[:, 128:]
        lr, li = lam_ref[0:8, :], lam_ref[8:16, :]

        def step(t, carry):
            xr, xi = carry
            row = pl.multiple_of(t * PITCH, 8)
            nr = lr * xr - li * xi + b_sc[pl.ds(row, 8), :]
            ni = lr * xi + li * xr + b_sc[pl.ds(row + 8, 8), :]
            xs_ref[pl.ds(row, 8), :] = nr
            xs_ref[pl.ds(row + 8, 8), :] = ni
            return nr, ni

        xr, xi = lax.fori_loop(0, ts, step, (c_sc[0:8, :], c_sc[8:16, :]), unroll=8)
        c_sc[0:8, :] = xr
        c_sc[8:16, :] = xi
        y = jnp.zeros((ts, D_S5), F32)
        for s in range(0, SLAB, 2):
            y = y + _dot(_slab_pair(xs_ref, s, ts), wc_ref[128 * s:128 * (s + 2), :])
        ypre_ref[...] = y
        y1 = jax.nn.gelu(y + d_ref[...] * u)
        ys_ref[...] = y1 * jax.nn.sigmoid(_dot(y1, wg_ref[...]))

    row = _rows(ts, D_S5)
    slabs = pl.BlockSpec((ts * PITCH, 128), lambda n: (n, 0))
    (wb, wb_spec), (wc, wc_spec), (lam, lam_spec), (d, d_spec) = (_of_layer(a) for a in (wb, wc, lam, d))
    return _pallas(
        body, out_shape=(jax.ShapeDtypeStruct((L * PITCH, 128), F32), jax.ShapeDtypeStruct((L, D_S5), F32),
                         jax.ShapeDtypeStruct((L, D_S5), F32)),
        grid=(L // ts,),
        in_specs=[_rows(ts, D_S5, O_SU // D_S5), wb_spec, wc_spec, lam_spec, d_spec, _whole((D_S5, D_S5))],
        out_specs=(slabs, row, row),
        scratch_shapes=[pltpu.VMEM((ts * PITCH, 128), F32), pltpu.VMEM((SLAB, 128), F32)], name="s5_fwd")(
            proj, wb, wc, lam, d, w_glu)


def _s5_bwd(proj, ypre, dys, xs, wb, wc, lam, d, w_glu):
    L = proj.shape[0]
    ts = min(TS, L)
    nb = L // ts

    def body(u_ref, y_ref, dys_ref, xs_ref, xp_ref, wb_ref, wc_ref, lam_ref, d_ref, wg_ref,
             du_ref, dwb_ref, dwc_ref, dlam_ref, dd_ref, dwg_ref, dx_sc, g_sc, c_sc):
        n = pl.program_id(0)

        @pl.when(n == 0)
        def _():
            c_sc[...] = jnp.zeros((SLAB, 128), F32)
            dlam_ref[...] = jnp.zeros((SLAB, 128), F32)
            dwb_ref[...] = jnp.zeros((D_S5, 2 * S5_CH), F32)
            dwc_ref[...] = jnp.zeros((2 * S5_CH, D_S5), F32)
            dd_ref[...] = jnp.zeros((1, D_S5), F32)
            dwg_ref[...] = jnp.zeros((D_S5, D_S5), F32)

        u, dv, dout = u_ref[...], d_ref[...], dys_ref[...]
        y1, gelu_vjp = jax.vjp(jax.nn.gelu, y_ref[...] + dv * u)
        sg = jax.nn.sigmoid(_dot(y1, wg_ref[...]))
        dz = dout * y1 * sg * (1.0 - sg)
        dy, = gelu_vjp(dout * sg + _dot(dz, wg_ref[...], _NT))
        dd_ref[...] += jnp.sum(dy * u, axis=0, keepdims=True)
        dwg_ref[...] += _dot(y1, dz, _TN)
        dyb = dy.astype(_MXU)
        for s in range(0, SLAB, 2):
            cols = slice(128 * s, 128 * (s + 2))
            dx2 = _dot(dyb, wc_ref[cols, :], _NT)
            dx_sc[_slab_rows(s, ts), :] = dx2[:, :128]
            dx_sc[_slab_rows(s + 1, ts), :] = dx2[:, 128:]
            dwc_ref[cols, :] += _dot(_slab_pair(xs_ref, s, ts), dyb, _TN)
        lr, li = lam_ref[0:8, :], lam_ref[8:16, :]

        def step(k, carry):
            gr, gi = carry
            row = pl.multiple_of((ts - 1 - k) * PITCH, 8)
            nr = dx_sc[pl.ds(row, 8), :] + lr * gr + li * gi
            ni = dx_sc[pl.ds(row + 8, 8), :] - li * gr + lr * gi
            g_sc[pl.ds(row, 8), :] = nr
            g_sc[pl.ds(row + 8, 8), :] = ni
            return nr, ni

        gr, gi = lax.fori_loop(0, ts, step, (c_sc[0:8, :], c_sc[8:16, :]), unroll=8)
        c_sc[0:8, :] = gr
        c_sc[8:16, :] = gi
        n1 = (ts - 1) * PITCH
        g3 = g_sc[pl.ds(PITCH, n1), :].reshape(ts - 1, PITCH, 128)
        x3 = xs_ref[pl.ds(0, n1), :].reshape(ts - 1, PITCH, 128)
        g3r, g3i, x3r, x3i = g3[:, 0:8], g3[:, 8:16], x3[:, 0:8], x3[:, 8:16]
        has_prev = jnp.where(n == nb - 1, 0.0, 1.0)
        g0r, g0i = g_sc[0:8, :], g_sc[8:16, :]
        pr, pi = xp_ref[0:8, :] * has_prev, xp_ref[8:16, :] * has_prev
        dlam_ref[0:8, :] += jnp.sum(g3r * x3r + g3i * x3i, axis=0) + g0r * pr + g0i * pi
        dlam_ref[8:16, :] += jnp.sum(g3i * x3r - g3r * x3i, axis=0) + g0i * pr - g0r * pi
        ub = u.astype(_MXU)
        du = dy * dv
        for s in range(0, SLAB, 2):
            cols = slice(128 * s, 128 * (s + 2))
            gs = _slab_pair(g_sc, s, ts)
            du = du + _dot(gs, wb_ref[:, cols], _NT)
            dwb_ref[:, cols] += _dot(ub, gs, _TN)
        du_ref[...] = du

    blk = lambda n: nb - 1 - n
    row = pl.BlockSpec((ts, D_S5), lambda n: (blk(n), 0))
    (wb, wb_spec), (wc, wc_spec), (lam, lam_spec), (d, d_spec) = (_of_layer(a) for a in (wb, wc, lam, d))
    return _pallas(
        body, out_shape=(jax.ShapeDtypeStruct((L, D_S5), F32), jax.ShapeDtypeStruct((D_S5, 2 * S5_CH), F32),
                         jax.ShapeDtypeStruct((2 * S5_CH, D_S5), F32), jax.ShapeDtypeStruct((SLAB, 128), F32),
                         jax.ShapeDtypeStruct((1, D_S5), F32), jax.ShapeDtypeStruct((D_S5, D_S5), F32)),
        grid=(nb,),
        in_specs=[pl.BlockSpec((ts, D_S5), lambda n: (blk(n), O_SU // D_S5)), row, row,
                  pl.BlockSpec((ts * PITCH, 128), lambda n: (blk(n), 0)),
                  pl.BlockSpec((PITCH, 128), lambda n: (jnp.maximum(blk(n) * ts - 1, 0), 0)),
                  wb_spec, wc_spec, lam_spec, d_spec, _whole((D_S5, D_S5))],
        out_specs=(row, _whole((D_S5, 2 * S5_CH)), _whole((2 * S5_CH, D_S5)), _whole((SLAB, 128)), _whole((1, D_S5)),
                   _whole((D_S5, D_S5))),
        scratch_shapes=[pltpu.VMEM((ts * PITCH, 128), F32), pltpu.VMEM((ts * PITCH, 128), F32), pltpu.VMEM((SLAB, 128), F32)],
        name="s5_bwd", compiler_params=pltpu.CompilerParams(vmem_limit_bytes=VMEM_BIG))(
            proj, ypre, dys, xs, xs, wb, wc, lam, d, w_glu)


def _rot(z, cos, sin):
    lane = lax.broadcasted_iota(jnp.int32, z.shape, 1)
    zs = z * sin
    half = HEAD_DIM // 2
    return z * cos + jnp.where(lane % HEAD_DIM < half, pltpu.roll(zs, PAIR - half, 1), pltpu.roll(zs, half, 1))


def _head_avg():
    r = lax.broadcasted_iota(jnp.int32, (PAIR, PAIR), 0) // HEAD_DIM
    c = lax.broadcasted_iota(jnp.int32, (PAIR, PAIR), 1) // HEAD_DIM
    return jnp.where(r == c, 1.0 / HEAD_DIM, 0.0).astype(F32)


def _ret_tables(tq):
    lg = jnp.log1p(-(2.0 ** (-5.0 - jnp.arange(RET_HEADS, dtype=F32))))
    scale = 1.0 / math.sqrt(HEAD_DIM)
    pos = jnp.arange(tq)
    n = pos.astype(F32)
    dist = jnp.abs(n[:, None] - n[None, :])
    ok = (pos[None, :] // CHUNK) <= (pos[:, None] // CHUNK)
    w = jnp.where(ok[None], scale * jnp.exp(lg[:, None, None] * dist[None]), 0.0)
    lgl = jnp.repeat(lg, HEAD_DIM)
    dq_tab = scale * jnp.exp(lgl[None, :] * (n[:, None] + 1.0))
    dk_tab = jnp.exp(lgl[None, :] * (tq - 1.0 - n[:, None]))
    blk = jnp.arange(PAIR) // HEAD_DIM
    bd = (blk[:, None] == blk[None, :]).astype(F32)
    gbd = bd[None] * jnp.exp(lgl.reshape(RET_HEADS // 2, PAIR)[:, :, None] * tq)
    return dict(w=w, wt=w.transpose(0, 2, 1), dq=dq_tab, dk=dk_tab, gbd=gbd, bd=bd)


def _ret_specs(tq, nq, rev, layer):
    blk = (lambda i: nq - 1 - i) if rev else (lambda i: i)
    col = lambda o: pl.BlockSpec((tq, PAIR), lambda p, i: (blk(i), o // PAIR + p))
    return dict(
        rq=col(O_RQ), rk=col(O_RK), rv=col(O_RV), nat=col(0),
        w=pl.BlockSpec((2, tq, tq), lambda p, i: (p, 0, 0)), tab=pl.BlockSpec((tq, PAIR), lambda p, i: (0, p)),
        gbd=pl.BlockSpec((None, PAIR, PAIR), lambda p, i: (p, 0, 0)), bd=pl.BlockSpec((PAIR, PAIR), lambda p, i: (0, 0)),
        gn=pl.BlockSpec((None, 1, PAIR), lambda p, i: (layer, 0, p)), dgn=pl.BlockSpec((1, PAIR), lambda p, i: (0, p)),
        st=pl.BlockSpec((None, None, PAIR, PAIR), lambda p, i: (p, blk(i), 0, 0)))


def _ret_fwd(proj, cos_t, sin_t, tabs, gn):
    L = proj.shape[0]
    tq = tabs["w"].shape[1]
    nq = L // tq

    def body(rq_ref, rk_ref, rv_ref, cos_ref, sin_ref, w_ref, dqt_ref, dkt_ref, gbd_ref, bd_ref, gn_ref,
             o_ref, y_ref, st_ref, s_sc):
        @pl.when(pl.program_id(1) == 0)
        def _():
            s_sc[...] = jnp.zeros((PAIR, PAIR), F32)

        state = s_sc[...]
        st_ref[...] = state
        cos, sin = cos_ref[...], sin_ref[...]
        q2, k2, v2 = _rot(rq_ref[...], cos, sin), _rot(rk_ref[...], cos, sin), rv_ref[...]
        owns = [_own((tq, PAIR), h) for h in range(2)]
        scores = [_dot(jnp.where(owns[h], q2, 0.0), k2, _NT) for h in range(2)]
        o = _dot(q2 * dqt_ref[...], state)
        for h in range(2):
            o = o + _dot(scores[h] * w_ref[h], jnp.where(owns[h], v2, 0.0))
        s_sc[...] = gbd_ref[...] * state + bd_ref[...] * _dot(k2 * dkt_ref[...], v2, _TN)
        o_ref[...] = o
        avg = _head_avg()
        oc = o - _exact_dot(o, avg, "b")
        y_ref[...] = oc * lax.rsqrt(_exact_dot(oc * oc, avg, "b") + EPS) * gn_ref[...]

    gn, layer = gn
    sp = _ret_specs(tq, nq, False, layer)
    nat = jax.ShapeDtypeStruct((L, D_RET), F32)
    return _pallas(
        body, out_shape=(nat, nat, jax.ShapeDtypeStruct((RET_HEADS // 2, nq, PAIR, PAIR), F32)), grid=(RET_HEADS // 2, nq),
        in_specs=[sp["rq"], sp["rk"], sp["rv"], sp["nat"], sp["nat"], sp["w"], sp["tab"], sp["tab"], sp["gbd"], sp["bd"],
                  sp["gn"]],
        out_specs=(sp["nat"], sp["nat"], sp["st"]), scratch_shapes=[pltpu.VMEM((PAIR, PAIR), F32)],
        name="ret_fwd")(proj, proj, proj, cos_t, sin_t, tabs["w"], tabs["dq"], tabs["dk"], tabs["gbd"], tabs["bd"], gn)


def _ret_bwd(proj, cos_t, sin_t, tabs, gn, o_pre, dy, states):
    L = proj.shape[0]
    tq = tabs["w"].shape[1]
    nq = L // tq

    def body(rq_ref, rk_ref, rv_ref, cos_ref, sin_ref, w_ref, wt_ref, dqt_ref, dkt_ref, gbd_ref, bd_ref, gn_ref,
             o_ref, dy_ref, st_ref, drq_ref, drk_ref, drv_ref, dgn_ref, g_sc):
        first = pl.program_id(1) == 0

        @pl.when(first)
        def _():
            g_sc[...] = jnp.zeros((PAIR, PAIR), F32)

        cos, sin = cos_ref[...], sin_ref[...]
        q2, k2, v2 = _rot(rq_ref[...], cos, sin), _rot(rk_ref[...], cos, sin), rv_ref[...]
        avg = _head_avg()
        ov, dyv = o_ref[...], dy_ref[...]
        oc = ov - _exact_dot(ov, avg, "b")
        r = lax.rsqrt(_exact_dot(oc * oc, avg, "b") + EPS)
        oh = oc * r
        dgn = jnp.sum(dyv * oh, axis=0, keepdims=True)
        doh = dyv * gn_ref[...]
        do = r * (doh - _exact_dot(doh, avg, "b") - oh * _exact_dot(doh * oh, avg, "b"))
        state, g = st_ref[...], g_sc[...]
        dqt, dkt = dqt_ref[...], dkt_ref[...]
        dq = _dot(do, state, _NT) * dqt
        dk = _dot(v2, g, _NT) * dkt
        dv = _dot(k2 * dkt, g)
        g_sc[...] = gbd_ref[...] * g + bd_ref[...] * _dot(q2 * dqt, do, _TN)
        owns = [_own((tq, PAIR), h) for h in range(2)]
        qms = [jnp.where(owns[h], q2, 0.0) for h in range(2)]
        doms = [jnp.where(owns[h], do, 0.0) for h in range(2)]
        ats = [_dot(k2, qms[h], _NT) for h in range(2)]
        das = [_dot(doms[h], v2, _NT) for h in range(2)]
        for h in range(2):
            dv = dv + _dot(ats[h] * wt_ref[h], doms[h])
            daw = (das[h] * w_ref[h]).astype(_MXU)
            dq = dq + _dot(daw, jnp.where(owns[h], k2, 0.0))
            dk = dk + _dot(daw.T, qms[h])
        drq_ref[...] = _rot(dq, cos, -sin)
        drk_ref[...] = _rot(dk, cos, -sin)
        drv_ref[...] = dv

        @pl.when(first)
        def _():
            dgn_ref[...] = dgn

        @pl.when(jnp.logical_not(first))
        def _():
            dgn_ref[...] += dgn

    gn, layer = gn
    sp = _ret_specs(tq, nq, True, layer)
    nat = jax.ShapeDtypeStruct((L, D_RET), F32)
    return _pallas(
        body, out_shape=(nat, nat, nat, jax.ShapeDtypeStruct((1, D_RET), F32)), grid=(RET_HEADS // 2, nq),
        in_specs=[sp["rq"], sp["rk"], sp["rv"], sp["nat"], sp["nat"], sp["w"], sp["w"], sp["tab"], sp["tab"], sp["gbd"],
                  sp["bd"], sp["gn"], sp["nat"], sp["nat"], sp["st"]],
        out_specs=(sp["nat"], sp["nat"], sp["nat"], sp["dgn"]), scratch_shapes=[pltpu.VMEM((PAIR, PAIR), F32)],
        name="ret_bwd")(proj, proj, proj, cos_t, sin_t, tabs["w"], tabs["wt"], tabs["dq"], tabs["dk"], tabs["gbd"],
                        tabs["bd"], gn, o_pre, dy, states)


def _gate_out(yf, ys, yr, proj, x, w):
    L = x.shape[0]

    def body(yf_ref, ys_ref, yr_ref, g_ref, x_ref, w_ref, xn_ref):
        cat = jnp.concatenate([yf_ref[...], ys_ref[...], yr_ref[...]], axis=-1)
        xn_ref[...] = x_ref[...] + _dot(cat * jax.nn.silu(g_ref[...]), w_ref[...])

    tm = min(TMB, L)
    full = _rows(tm, D_MODEL)
    return _pallas(body, out_shape=jax.ShapeDtypeStruct((L, D_MODEL), F32), grid=(L // tm,),
                   in_specs=[_rows(tm, D_FOX), _rows(tm, D_S5), _rows(tm, D_RET), _rows(tm, D_MODEL, O_GATE // D_MODEL),
                             full, _whole((D_MODEL, D_MODEL))],
                   out_specs=full, name="gate_out")(yf, ys, yr, proj, x, w)


def _gate_out_bwd(dxn, w, yf, ys, yr, proj):
    L = dxn.shape[0]

    def body(dx_ref, w_ref, yf_ref, ys_ref, yr_ref, g_ref, dyf_ref, dys_ref, dyr_ref, dg_ref, dw_ref):
        dxv = dx_ref[...].astype(_MXU)
        dy = _dot(dxv, w_ref[...], _NT)
        g = g_ref[...]
        sg = jax.nn.sigmoid(g)
        silu = g * sg
        dcat = dy * silu
        dyf_ref[...] = dcat[:, :D_FOX]
        dys_ref[...] = dcat[:, D_FOX:D_FOX + D_S5]
        dyr_ref[...] = dcat[:, D_FOX + D_S5:]
        cat = jnp.concatenate([yf_ref[...], ys_ref[...], yr_ref[...]], axis=-1)
        dg_ref[...] = dy * cat * (sg * (1.0 + g * (1.0 - sg)))
        dw = _dot(cat * silu, dxv, _TN)

        @pl.when(pl.program_id(0) == 0)
        def _():
            dw_ref[...] = dw

        @pl.when(pl.program_id(0) != 0)
        def _():
            dw_ref[...] += dw

    tm = min(TMB, L)
    full = _rows(tm, D_MODEL)
    f, s, r = _rows(tm, D_FOX), _rows(tm, D_S5), _rows(tm, D_RET)
    sq = _whole((D_MODEL, D_MODEL))
    return _pallas(body, out_shape=(jax.ShapeDtypeStruct((L, D_FOX), F32), jax.ShapeDtypeStruct((L, D_S5), F32),
                                    jax.ShapeDtypeStruct((L, D_RET), F32), jax.ShapeDtypeStruct((L, D_MODEL), F32),
                                    jax.ShapeDtypeStruct((D_MODEL, D_MODEL), F32)),
                   grid=(L // tm,), in_specs=[full, sq, f, s, r, _rows(tm, D_MODEL, O_GATE // D_MODEL)],
                   out_specs=(f, s, r, full, sq), name="gate_out_bwd",
                   compiler_params=pltpu.CompilerParams(vmem_limit_bytes=VMEM_BIG))(dxn, w, yf, ys, yr, proj)


def _final_loss(x, g, tgt):
    L = x.shape[0]

    def body(x_ref, g_ref, t_ref, loss_ref, dx_ref, dg_ref):
        xv, gv = x_ref[...], g_ref[...]
        r = lax.rsqrt(jnp.mean(xv * xv, axis=-1, keepdims=True) + EPS)
        err = xv * r * gv - t_ref[...]
        part = 0.5 * jnp.sum(jnp.mean(err * err, axis=-1, keepdims=True), axis=0, keepdims=True)
        dx, dg = _rms_bwd(xv, gv, err * (1.0 / D_MODEL))
        dx_ref[...] = dx

        @pl.when(pl.program_id(0) == 0)
        def _():
            loss_ref[...] = part
            dg_ref[...] = dg

        @pl.when(pl.program_id(0) != 0)
        def _():
            loss_ref[...] += part
            dg_ref[...] += dg

    full = _rows(TM, D_MODEL)
    return _pallas(body, out_shape=(jax.ShapeDtypeStruct((1, 1), F32), jax.ShapeDtypeStruct((L, D_MODEL), F32),
                                    jax.ShapeDtypeStruct((1, D_MODEL), F32)),
                   grid=(L // TM,), in_specs=[full, _whole((1, D_MODEL)), full],
                   out_specs=(_whole((1, 1)), full, _whole((1, D_MODEL))), name="final_loss")(x, g, tgt)


def _block_diag(blocks):
    n, g, r, c = blocks.shape
    eye = jnp.eye(g, dtype=blocks.dtype)
    return (blocks[:, :, :, None, :] * eye[None, :, None, :, None]).reshape(n, g * r, g * c)


def _diag_blocks(m, g):
    n, r, c = m.shape[0], m.shape[1] // g, m.shape[2] // g
    eye = jnp.eye(g, dtype=m.dtype)
    return jnp.sum(m.reshape(n, g, r, g, c) * eye[None, :, None, :, None], axis=3)


def _rope_tables(L):
    half = HEAD_DIM // 2
    freqs = ROPE_BASE ** (-jnp.arange(half, dtype=F32) / half)
    ang = jnp.arange(L, dtype=F32)[:, None] * freqs[None, :]
    cos, sin = jnp.cos(ang), jnp.sin(ang)
    cos_t = jnp.tile(jnp.concatenate([cos, cos], axis=-1), (1, RET_HEADS))
    sin_t = jnp.tile(jnp.concatenate([sin, -sin], axis=-1), (1, RET_HEADS))
    return cos_t, sin_t


def _s5_disc_args(small):
    g, s, ch = S5_GROUPS, S5_STATE, S5_GROUP_CH
    return (small["s5_a_re"], small["s5_a_im"], small["s5_log_dt"][:, :, None],
            small["s5_b_re"].reshape(DEPTH, g, s * ch), small["s5_b_im"].reshape(DEPTH, g, s * ch))


def _s5_mats(small):
    g, s, ch = S5_GROUPS, S5_STATE, S5_GROUP_CH
    lr, li, bbr, bbi = _s5_disc(*_s5_disc_args(small))
    lam = jnp.concatenate([lr.reshape(DEPTH, 8, 128), li.reshape(DEPTH, 8, 128)], axis=1)
    blocks = lambda a: _block_diag(a.astype(_MXU).transpose(0, 1, 3, 2))
    wb = jnp.concatenate([blocks(b.reshape(DEPTH, g, s, ch)) for b in (bbr, bbi)], axis=2)
    wc = jnp.concatenate([blocks(c) for c in (small["s5_c_re"], -small["s5_c_im"])], axis=1)
    return lam, wb, wc


def _s5_param_grads(small, dwb, dwc, dlam):
    g, s, ch = S5_GROUPS, S5_STATE, S5_GROUP_CH
    dc = [_diag_blocks(m, g).transpose(0, 1, 3, 2).reshape(DEPTH, g, ch * s) for m in (dwc[:, :S5_CH], dwc[:, S5_CH:])]
    dbb = [_diag_blocks(m, g).transpose(0, 1, 3, 2).reshape(DEPTH, g, s * ch) for m in (dwb[:, :, :S5_CH], dwb[:, :, S5_CH:])]
    dar, dai, dldt, dbr, dbi = _s5_disc_bwd(*_s5_disc_args(small), dlam[:, :8].reshape(DEPTH, g, s),
                                            dlam[:, 8:].reshape(DEPTH, g, s), dbb[0], dbb[1])
    return dict(s5_a_re=dar, s5_a_im=dai, s5_log_dt=dldt.reshape(DEPTH, g), s5_b_re=dbr, s5_b_im=dbi, s5_c_re=dc[0],
                s5_c_im=-dc[1])


_DENSE = ("s5_b_re", "s5_b_im", "s5_c_re", "s5_c_im")


def _layer_fwd(x, p, rope, ride=None, late=False):
    L = x.shape[0]
    cos_t, sin_t, ret_tabs = rope
    s = {"x": x}
    proj, h = _norm_inproj(x, p["norm_w"], p["w_in"])
    s["proj"], s["h"] = proj, h
    qa, ka, kat, vt = _fox_prep(proj, _fox_cumsum(proj, p["b_f"]))
    yf, lse, landed = _fox_fwd(qa, ka, vt, ride)
    s.update(qa=qa, ka=ka, kat=kat, lse=lse, yf=yf)
    if late:
        p["w_glu"], p["w_out"] = _gathered_rows(landed[-2]), _gathered_rows(landed[-1])
        landed = landed[:-2]
    xs, ypre, ys = _s5_fwd(proj, p["wb"], p["wc"], p["lam"], p["d"], p["w_glu"])
    s.update(xs=xs, ypre=ypre, ys=ys)
    o_pre, yr, states = _ret_fwd(proj, cos_t, sin_t, ret_tabs, p["gn_w"])
    s.update(o_pre=o_pre, yr=yr, states=states)
    return _gate_out(yf, ys, yr, proj, x, p["w_out"]), s, landed


def _layer_bwd(dxn, s, p, rope, ride=None, early=False):
    L = dxn.shape[0]
    cos_t, sin_t, ret_tabs = rope
    g = {}
    proj = s["proj"]
    dyf, dys, dyr, dgate, g["w_out"] = _gate_out_bwd(dxn, p["w_out"], s["yf"], s["ys"], s["yr"], proj)
    drq, drk, drv, dgn = _ret_bwd(proj, cos_t, sin_t, ret_tabs, p["gn_w"], s["o_pre"], dyr, s["states"])
    g["ret_gn_w"] = dgn.reshape(D_RET)
    dsu, g["wb"], g["wc"], g["lam"], dd, g["s5_w_glu"] = _s5_bwd(proj, s["ypre"], dys, s["xs"], p["wb"], p["wc"], p["lam"],
                                                                 p["d"], p["w_glu"])
    g["s5_d"] = dd.reshape(D_S5)
    if early:
        ride = (ride[0] + _row_slots(g), ride[1] + [True, True])
    dqt, dkraw, dv, landed = _fox_bwd(s["qa"], s["ka"], s["kat"], proj, dyf, s["yf"], s["lse"], ride)
    dq, dk, dfl, dbf = _fox_post_bwd(dqt, dkraw, proj, p["b_f"])
    g["fox_b_f"] = dbf[0, :FOX_HEADS]
    pieces = [dgate, dq, dk, dv, dsu, drq, drk, drv, dfl]
    dx, dnw, g["w_in"] = _inproj_bwd(pieces, p["w_in"], s["x"], p["norm_w"], dxn, s["h"])
    g["norm_w"] = dnw.reshape(D_MODEL)
    return dx, g, landed


def _stacked_params(small):
    lam, wb, wc = _s5_mats(small)
    row = lambda a: a[:, None, :]
    return dict(norm_w=row(small["norm_w"]), b_f=row(jnp.pad(small["fox_b_f"], ((0, 0), (0, PAIR - FOX_HEADS)))),
                lam=lam, wb=wb, wc=wc, d=row(small["s5_d"]), gn_w=row(small["ret_gn_w"]))


def _layer_params(l, w_in_p, w_glu, w_out, stacked):
    return dict({k: (a, l) for k, a in stacked.items()}, w_in=w_in_p, w_glu=w_glu, w_out=w_out)


_SHARDED = ("w_in", "s5_w_glu", "w_out")
_WIRE = jnp.bfloat16


_RUNS = ((2568, 3592, O_GATE), (0, 1536, O_FQ), (1544, 2568, O_SU), (1536, 1544, O_FL))


def _shard_pieces():
    out = []
    for a, b, pad in _RUNS:
        while a < b:
            j = a // W_SHARD
            e = min(b, (j + 1) * W_SHARD)
            out.append((j, a - j * W_SHARD, e - j * W_SHARD, pad))
            pad, a = pad + e - a, e
    return out


def _gathered_w_in(g_in):
    cols = [g_in[j, :, a:e] for j, a, e, _ in _shard_pieces()]
    cols.append(jnp.zeros((D_MODEL, D_INP - O_FL - FOX_HEADS), g_in.dtype))
    return jnp.concatenate(cols, axis=1)


def _gathered_rows(g):
    return g.reshape(-1, g.shape[-1])


def _w_in_slots(g):
    w_in = g["w_in"].astype(_WIRE)
    slots = []
    for j in range(N_DEV):
        mine = sorted((a, e, pad) for jj, a, e, pad in _shard_pieces() if jj == j)
        slots.append(jnp.concatenate([w_in[:, pad:pad + e - a] for a, e, pad in mine], axis=1))
    return jnp.stack(slots)


def _row_slots(g):
    return [g["s5_w_glu"].reshape(N_DEV, D_S5 // N_DEV, D_S5).astype(_WIRE),
            g["w_out"].reshape(N_DEV, D_MODEL // N_DEV, D_MODEL).astype(_WIRE)]


def _step_grads(x, tgt, small, full=None, shards=None):
    L = x.shape[0]
    rope = _rope_tables(L) + (_ret_tables(min(TQ, L)),)
    stacked = _stacked_params(small)
    if shards is not None:
        nxt = (_gathered_w_in(_exchange([shards[0][0]], [False], "gather_layer0")[0]), None, None)
    saved, params = [], []
    for l in range(DEPTH):
        weights = nxt if shards is not None else tuple(f[l] for f in full)
        ride = None
        if shards is not None:
            arrs = [s[l + 1] for s in shards] if l + 1 < DEPTH else []
            arrs += [shards[1][0], shards[2][0]] if l == 0 else []
            ride = (arrs, [False] * len(arrs)) if arrs else None
        params.append(_layer_params(l, *weights, stacked))
        x, s, landed = _layer_fwd(x, params[l], rope, ride, late=shards is not None and l == 0)
        if landed:
            nxt = (_gathered_w_in(landed[0]), _gathered_rows(landed[1]), _gathered_rows(landed[2]))
        saved.append(s)
    loss, dx, dfw = _final_loss(x, small["final_norm_w"][None], tgt)
    grads, partials, waiting = [None] * DEPTH, [None] * DEPTH, None
    for l in reversed(range(DEPTH)):
        ride = (waiting, [True] * len(waiting)) if waiting is not None else None
        dx, grads[l], landed = _layer_bwd(dx, saved[l], params[l], rope, ride, early=ride is not None and l == 0)
        if waiting is not None:
            partials[l + 1] = landed[:3]
        if shards is not None:
            waiting = [_w_in_slots(grads[l])] + _row_slots(grads[l])
    stack = lambda n: jnp.stack([g[n] for g in grads])
    small_g = {n: stack(n) for n in ("norm_w", "fox_b_f", "s5_d", "ret_gn_w")}
    small_g.update(_s5_param_grads(small, stack("wb"), stack("wc"), stack("lam")), final_norm_w=dfw)
    if shards is None:
        return loss, dx, grads, small_g
    last = _exchange([waiting[0]] + [small_g[n].astype(_WIRE) for n in _SMALL], [True] + [False] * len(_SMALL),
                     "exchange_layer0")
    partials[0] = [last[0]] + landed[3:]
    return loss, dx, grads, small_g, partials, dict(zip(_SMALL, last[1:]))


_MESH = pl.DeviceIdType.MESH
_ANY = pl.BlockSpec(memory_space=pl.ANY)


def _me_and_peers():
    x, y, c = lax.axis_index("x"), lax.axis_index("y"), lax.axis_index("c")
    flip = lambda a, bit: (1 - a) if bit else a
    peers = []
    for r in range(1, N_DEV):
        px, py, pc = flip(x, (r >> 2) & 1), flip(y, (r >> 1) & 1), flip(c, r & 1)
        peers.append(((px, py, pc), 4 * px + 2 * py + pc))
    return 4 * x + 2 * y + c, peers


def _exchange_copies(srcs, dsts, sems, scatter):
    send_sems, recv_sems, local_sems = sems
    me, peers = _me_and_peers()
    pick = lambda t, to: srcs[t].at[to] if scatter[t] else srcs[t]
    own = [pltpu.make_async_copy(pick(t, me), dsts[t].at[me], local_sems.at[t]) for t in range(len(srcs))]
    sends, waits = [], []
    for r, (dev, idx) in enumerate(peers):
        for t in range(len(srcs)):
            for land, out in ((me, sends), (idx, waits)):
                out.append(pltpu.make_async_remote_copy(pick(t, idx), dsts[t].at[land], send_sems.at[t, r], recv_sems.at[t, r],
                                                        device_id=dev, device_id_type=_MESH))
    return own, sends, waits


def _exchange_start(srcs, dsts, sems, scatter):
    own, sends, _ = _exchange_copies(srcs, dsts, sems, scatter)
    for cp in own + sends:
        cp.start()


def _exchange_wait(srcs, dsts, sems, scatter):
    own, _, waits = _exchange_copies(srcs, dsts, sems, scatter)
    for cp in waits + own:
        cp.wait()


def _exchange_shapes(arrs, scatter):
    outs = [jax.ShapeDtypeStruct(a.shape if sc else (N_DEV,) + a.shape, a.dtype) for a, sc in zip(arrs, scatter)]
    n = len(arrs)
    sems = [pltpu.SemaphoreType.DMA((n, N_DEV - 1)), pltpu.SemaphoreType.DMA((n, N_DEV - 1)), pltpu.SemaphoreType.DMA((n,))]
    return outs, sems


def _exchange(arrs, scatter, name):
    n = len(arrs)

    def body(*refs):
        _exchange_start(refs[:n], refs[n:2 * n], refs[2 * n:], scatter)
        _exchange_wait(refs[:n], refs[n:2 * n], refs[2 * n:], scatter)

    outs, sems = _exchange_shapes(arrs, scatter)
    return _pallas(body, out_shape=tuple(outs), in_specs=[_ANY] * n, out_specs=tuple([_ANY] * n), scratch_shapes=sems,
                   name=name)(*arrs)


def _riding(body, n_in, n_out, ride, is_first, is_last):
    if ride is None:
        return body, [], [], [], []
    arrs, scatter = ride
    n = len(arrs)
    outs, sems = _exchange_shapes(arrs, scatter)

    def wrapped(*refs):
        ins, srcs = refs[:n_in], refs[n_in:n_in + n]
        own_outs, dsts = refs[n_in + n:n_in + n + n_out], refs[n_in + n + n_out:n_in + 2 * n + n_out]
        scratch, ex_sems = refs[n_in + 2 * n + n_out:-3], refs[-3:]

        @pl.when(is_first())
        def _():
            _exchange_start(srcs, dsts, ex_sems, scatter)

        body(*ins, *own_outs, *scratch)

        @pl.when(is_last())
        def _():
            _exchange_wait(srcs, dsts, ex_sems, scatter)

    return wrapped, list(arrs), [_ANY] * n, outs, sems


def _adamw_body(p_ref, w_ref, m_ref, v_ref, g_ref, d_ref, nm_ref, nv_ref):
    g = p_ref[0].astype(F32)
    for i in range(1, N_DEV):
        g = g + p_ref[i].astype(F32)
    nm = ADAM_B1 * m_ref[...] + (1.0 - ADAM_B1) * g
    nv = ADAM_B2 * v_ref[...] + (1.0 - ADAM_B2) * jnp.square(g)
    m_hat = nm / (1.0 - ADAM_B1 ** ADAM_STEP)
    v_hat = nv / (1.0 - ADAM_B2 ** ADAM_STEP)
    g_ref[...] = g
    d_ref[...] = -ADAM_LR * (m_hat / (jnp.sqrt(v_hat) + ADAM_EPS) + ADAM_WD * w_ref[...])
    nm_ref[...] = nm
    nv_ref[...] = nv


def _adamw(parts, w, m, v, name):
    n, nb, rows, cols = parts.shape
    tm = next(t for t in (256, 128, 64, 32, 16) if rows % t == 0)

    def body(*refs):
        _adamw_body(*refs)

    row = pl.BlockSpec((None, tm, cols), lambda b, i: (b, i, 0))
    return _pallas(body, out_shape=(jax.ShapeDtypeStruct((nb, rows, cols), F32),) * 4, grid=(nb, rows // tm),
                   in_specs=[pl.BlockSpec((n, None, tm, cols), lambda b, i: (0, b, i, 0)), row, row, row],
                   out_specs=(row,) * 4, name=name)(parts, w, m, v)


def _adamw_whole(parts, w, m, v, name):
    def body(*refs):
        _adamw_body(*refs)

    if w.ndim == 2:
        grid = (1,)
        slab = pl.BlockSpec(w.shape, lambda b: (0, 0))
        part = pl.BlockSpec(parts.shape, lambda b: (0, 0, 0))
    else:
        grid, rest = (w.shape[0],), w.shape[1:]
        zeros = (0,) * len(rest)
        slab = pl.BlockSpec((None,) + rest, lambda b: (b,) + zeros)
        part = pl.BlockSpec((N_DEV, None) + rest, lambda b: (0, b) + zeros)
    return _pallas(body, out_shape=(jax.ShapeDtypeStruct(w.shape, F32),) * 4, grid=grid,
                   in_specs=[part, slab, slab, slab], out_specs=(slab,) * 4, name=name)(parts, w, m, v)


_WEIGHTS = ("norm_w", "w_in", "fox_b_f", "s5_a_re", "s5_a_im", "s5_b_re", "s5_b_im", "s5_c_re", "s5_c_im", "s5_d",
            "s5_log_dt", "s5_w_glu", "ret_gn_w", "w_out", "final_norm_w")
_SMALL = tuple(n for n in _WEIGHTS if n not in _SHARDED)


def kernel(x, norm_w, w_in, fox_b_f, s5_a_re, s5_a_im, s5_b_re, s5_b_im, s5_c_re, s5_c_im, s5_d, s5_log_dt, s5_w_glu, ret_gn_w, w_out, final_norm_w, loss_target, m_norm_w, m_w_in, m_fox_b_f, m_s5_a_re, m_s5_a_im, m_s5_b_re, m_s5_b_im, m_s5_c_re, m_s5_c_im, m_s5_d, m_s5_log_dt, m_s5_w_glu, m_ret_gn_w, m_w_out, m_final_norm_w, v_norm_w, v_w_in, v_fox_b_f, v_s5_a_re, v_s5_a_im, v_s5_b_re, v_s5_b_im, v_s5_c_re, v_s5_c_im, v_s5_d, v_s5_log_dt, v_s5_w_glu, v_ret_gn_w, v_w_out, v_final_norm_w):
    w = dict(norm_w=norm_w, w_in=w_in, fox_b_f=fox_b_f, s5_a_re=s5_a_re, s5_a_im=s5_a_im, s5_b_re=s5_b_re, s5_b_im=s5_b_im,
             s5_c_re=s5_c_re, s5_c_im=s5_c_im, s5_d=s5_d, s5_log_dt=s5_log_dt, s5_w_glu=s5_w_glu, ret_gn_w=ret_gn_w,
             w_out=w_out, final_norm_w=final_norm_w)
    m = dict(norm_w=m_norm_w, w_in=m_w_in, fox_b_f=m_fox_b_f, s5_a_re=m_s5_a_re, s5_a_im=m_s5_a_im, s5_b_re=m_s5_b_re,
             s5_b_im=m_s5_b_im, s5_c_re=m_s5_c_re, s5_c_im=m_s5_c_im, s5_d=m_s5_d, s5_log_dt=m_s5_log_dt,
             s5_w_glu=m_s5_w_glu, ret_gn_w=m_ret_gn_w, w_out=m_w_out, final_norm_w=m_final_norm_w)
    v = dict(norm_w=v_norm_w, w_in=v_w_in, fox_b_f=v_fox_b_f, s5_a_re=v_s5_a_re, s5_a_im=v_s5_a_im, s5_b_re=v_s5_b_re,
             s5_b_im=v_s5_b_im, s5_c_re=v_s5_c_re, s5_c_im=v_s5_c_im, s5_d=v_s5_d, s5_log_dt=v_s5_log_dt,
             s5_w_glu=v_s5_w_glu, ret_gn_w=v_ret_gn_w, w_out=v_w_out, final_norm_w=v_final_norm_w)

    small = {n: w[n] for n in _SMALL}
    loss, dx, _, _, partials, r_small = _step_grads(x[0], loss_target[0], small, shards=[w[n].astype(_MXU) for n in _SHARDED])

    res = {}
    for t, n in enumerate(_SHARDED):
        res[n] = _adamw(jnp.stack([partials[l][t] for l in range(DEPTH)], axis=1), w[n], m[n], v[n], "adamw_" + n)
    for n in _SMALL:
        shape = w[n].shape
        view = (1,) + shape if len(shape) == 1 else shape[:2] + (-1,) if n in _DENSE else shape
        outs = _adamw_whole(r_small[n], *[d[n].reshape(view) for d in (w, m, v)], "adamw_" + n)
        res[n] = [o.reshape(shape) for o in outs]

    loss = lax.psum(loss[0, 0], ("x", "y", "c"))
    return (loss, dx[None], *[res[n][0] for n in _WEIGHTS], *[res[n][1] for n in _WEIGHTS],
            *[res[n][2] for n in _WEIGHTS], *[res[n][3] for n in _WEIGHTS])
```

```python
import math

import jax
import jax.numpy as jnp
from jax import lax
from jax.experimental import pallas as pl
from jax.experimental.pallas import tpu as pltpu

F32 = jnp.float32
_MXU = jnp.bfloat16
_HI = lax.Precision.HIGHEST

N_DEV = 8
DEPTH = 4
D_MODEL = 1024
HEAD_DIM = 64
D_FOX = 512
FOX_HEADS = 8
D_S5 = 256
S5_GROUPS = 16
S5_GROUP_CH = 16
S5_STATE = 64
S5_CH = S5_GROUPS * S5_STATE
D_RET = 256
RET_HEADS = 4
CHUNK = 64
ROPE_BASE = 10000.0
EPS = 1e-6
D_IN = 3592
D_INP = 3712
W_SHARD = D_IN // N_DEV
O_GATE, O_FQ, O_FK, O_FV, O_SU, O_RQ, O_RK, O_RV, O_FL = 0, 1024, 1536, 2048, 2560, 2816, 3072, 3328, 3584

ADAM_LR, ADAM_B1, ADAM_B2, ADAM_EPS, ADAM_WD, ADAM_STEP = 0.001, 0.9, 0.999, 1e-08, 0.01, 10

TM = 256
TMB = 512
TQ = 512
TS = 512
NEG = -1e30
VMEM_BIG = 56 * 1024 * 1024


def _pallas(body, **kw):
    return pl.pallas_call(body, **kw)


def _whole(shape):
    n = len(shape)
    return pl.BlockSpec(shape, lambda *_: (0,) * n)


def _rows(tm, width, col=0):
    return pl.BlockSpec((tm, width), lambda i: (i, col))


def _of_layer(param):
    a, l = param
    return a, pl.BlockSpec((None,) + a.shape[1:], lambda *_: (l,) + (0,) * (a.ndim - 1))


def _dot(a, b, dims=(((1,), (0,)), ((), ()))):
    return lax.dot_general(a.astype(_MXU), b.astype(_MXU), dims, preferred_element_type=F32)


_NT = (((1,), (1,)), ((), ()))
_TN = (((0,), (0,)), ((), ()))


def _norm_inproj(x, g, w):
    L = x.shape[0]

    def body(x_ref, g_ref, w_ref, p_ref, h_ref):
        xv = x_ref[...]
        r = lax.rsqrt(jnp.mean(xv * xv, axis=-1, keepdims=True) + EPS)
        h = (xv * r * g_ref[...]).astype(_MXU)
        h_ref[...] = h
        p_ref[...] = _dot(h, w_ref[...])

    tm = min(TMB, L)
    g, g_spec = _of_layer(g)
    return _pallas(body, out_shape=(jax.ShapeDtypeStruct((L, D_INP), F32), jax.ShapeDtypeStruct((L, D_MODEL), _MXU)),
                   grid=(L // tm,),
                   in_specs=[_rows(tm, D_MODEL), g_spec,
                             pl.BlockSpec((D_MODEL, D_INP), lambda i: (0, 0), pipeline_mode=pl.Buffered(1))],
                   out_specs=(_rows(tm, D_INP), _rows(tm, D_MODEL)), name="norm_inproj",
                   compiler_params=pltpu.CompilerParams(vmem_limit_bytes=VMEM_BIG))(x, g, w)


def _rms_bwd(xv, g, dh):
    r = lax.rsqrt(jnp.mean(xv * xv, axis=-1, keepdims=True) + EPS)
    xh = xv * r
    dg = jnp.sum(dh * xh, axis=0, keepdims=True)
    dxh = dh * g
    dx = r * (dxh - xh * jnp.mean(dxh * xh, axis=-1, keepdims=True))
    return dx, dg


def _inproj_bwd(pieces, w, x, g, dres, h):
    L = x.shape[0]
    n = len(pieces)
    nb = L // TM

    def body(*refs):
        w_ref, x_ref, g_ref, dr_ref, h_ref, dx_ref, dg_ref, dw_ref, acc_sc = refs[n:]
        step = pl.program_id(0)

        @pl.when(step == 0)
        def _():
            acc_sc[...] = jnp.zeros((D_MODEL, D_INP), F32)
            dg_ref[...] = jnp.zeros((1, D_MODEL), F32)

        hv = h_ref[...]
        dh = jnp.zeros((TM, D_MODEL), F32)
        off = 0
        for r in refs[:n]:
            cols = slice(off, off + r.shape[1])
            off += r.shape[1]
            piece = r[...].astype(_MXU)
            dh = dh + _dot(piece, w_ref[:, cols], _NT)
            acc_sc[:, cols] += _dot(hv, piece, _TN)
        dx, dg = _rms_bwd(x_ref[...], g_ref[...], dh)
        dx_ref[...] = dx + dr_ref[...]
        dg_ref[...] += dg

        @pl.when(step == nb - 1)
        def _():
            dw_ref[...] = acc_sc[...].astype(_WIRE)

    resident = pl.BlockSpec((D_MODEL, D_INP), lambda i: (0, 0), pipeline_mode=pl.Buffered(1))
    g, g_spec = _of_layer(g)
    return _pallas(body, out_shape=(jax.ShapeDtypeStruct((L, D_MODEL), F32), jax.ShapeDtypeStruct((1, D_MODEL), F32),
                                    jax.ShapeDtypeStruct((D_MODEL, D_INP), _WIRE)),
                   grid=(nb,),
                   in_specs=[_rows(TM, p.shape[1]) for p in pieces]
                   + [resident, _rows(TM, D_MODEL), g_spec, _rows(TM, D_MODEL), _rows(TM, D_MODEL)],
                   out_specs=(_rows(TM, D_MODEL), _whole((1, D_MODEL)), resident),
                   scratch_shapes=[pltpu.VMEM((D_MODEL, D_INP), F32)], name="inproj_bwd",
                   compiler_params=pltpu.CompilerParams(vmem_limit_bytes=VMEM_BIG))(*pieces, w, x, g, dres, h)


PAIR = 2 * HEAD_DIM
N_AUX = 3


def _own(shape, h):
    return lax.broadcasted_iota(jnp.int32, shape, len(shape) - 1) // HEAD_DIM == h


def _split(x):
    parts = []
    for _ in range(N_AUX):
        part = x.astype(_MXU)
        parts.append(part)
        x = x - part.astype(F32)
    return parts


def _exact_dot(a, b, exact):
    if exact == "b":
        return sum(_dot(part, b) for part in _split(a))
    return sum(_dot(a, part) for part in _split(b))


def _tri(n, lower):
    r = lax.broadcasted_iota(jnp.int32, (n, n), 0)
    c = lax.broadcasted_iota(jnp.int32, (n, n), 1)
    return jnp.where(r >= c if lower else r <= c, 1.0, 0.0).astype(F32)


def _fox_cumsum(proj, b):
    L = proj.shape[0]

    def body(fl_ref, b_ref, c_ref, carry_sc):
        @pl.when(pl.program_id(0) == 0)
        def _():
            carry_sc[...] = jnp.zeros((1, PAIR), F32)

        lane = lax.broadcasted_iota(jnp.int32, (TM, PAIR), 1)
        lf = jnp.where(lane < FOX_HEADS, jax.nn.log_sigmoid(fl_ref[...] + b_ref[...]), 0.0)
        cs = _exact_dot(_tri(TM, True), lf, "a") + carry_sc[...]
        c_ref[...] = cs
        carry_sc[...] = cs[TM - 1:TM, :]

    b, b_spec = _of_layer(b)
    return _pallas(body, out_shape=jax.ShapeDtypeStruct((L, PAIR), F32), grid=(L // TM,),
                   in_specs=[_rows(TM, PAIR, O_FL // PAIR), b_spec], out_specs=_rows(TM, PAIR),
                   scratch_shapes=[pltpu.VMEM((1, PAIR), F32)], name="fox_cumsum")(proj, b)


def _fox_prep(proj, c):
    L = proj.shape[0]

    def body(q_ref, k_ref, v_ref, c_ref, qa_ref, ka_ref, kat_ref, vt_ref):
        lane = lax.broadcasted_iota(jnp.int32, (TM, PAIR), 1)
        cv = c_ref[...]
        for p in range(FOX_HEADS // 2):
            cols = slice(PAIR * p, PAIR * (p + 1))
            q2, k2 = q_ref[:, cols], k_ref[:, cols]
            vt_ref[p] = v_ref[:, cols].T.astype(_MXU)
            for e in range(2):
                h = 2 * p + e
                own = lane // HEAD_DIM == e
                a = lane - (HEAD_DIM if e == 0 else 0)
                rest = jnp.broadcast_to(cv[:, h:h + 1], (TM, PAIR))
                aux_q = jnp.where((a >= N_AUX) & (a < 2 * N_AUX), 1.0, 0.0)
                aux_k = jnp.where((a >= 0) & (a < N_AUX), 1.0, 0.0)
                for n in range(N_AUX):
                    part = rest.astype(_MXU).astype(F32)
                    rest = rest - part
                    aux_q = jnp.where(a == n, part, aux_q)
                    aux_k = jnp.where(a == N_AUX + n, -part, aux_k)
                ka = jnp.where(own, k2, aux_k)
                qa_ref[h] = jnp.where(own, q2 * (1.0 / math.sqrt(HEAD_DIM)), aux_q).astype(_MXU)
                ka_ref[h] = ka.astype(_MXU)
                kat_ref[h] = ka.T.astype(_MXU)

    hl = jax.ShapeDtypeStruct((FOX_HEADS, L, PAIR), _MXU)
    nat = lambda o: _rows(TM, D_FOX, o // D_FOX)
    rows = pl.BlockSpec((FOX_HEADS, TM, PAIR), lambda i: (0, i, 0))
    return _pallas(
        body, out_shape=(hl, hl, jax.ShapeDtypeStruct((FOX_HEADS, PAIR, L), _MXU),
                         jax.ShapeDtypeStruct((FOX_HEADS // 2, PAIR, L), _MXU)),
        grid=(L // TM,), in_specs=[nat(O_FQ), nat(O_FK), nat(O_FV), _rows(TM, PAIR)],
        out_specs=(rows, rows, pl.BlockSpec((FOX_HEADS, PAIR, TM), lambda i: (0, 0, i)),
                   pl.BlockSpec((FOX_HEADS // 2, PAIR, TM), lambda i: (0, 0, i))),
        name="fox_prep")(proj, proj, proj, c)


def _key_le_query(tq):
    return lax.broadcasted_iota(jnp.int32, (tq, tq), 0) <= lax.broadcasted_iota(jnp.int32, (tq, tq), 1)


def _grid_ends(n0, n1):
    first = lambda: (pl.program_id(0) == 0) & (pl.program_id(1) == 0)
    last = lambda: (pl.program_id(0) == n0 - 1) & (pl.program_id(1) == n1 - 1)
    return first, last


def _fox_fwd(qa, ka, vt, ride=None):
    H, L, _ = qa.shape
    tq = min(TQ, L)
    nq = L // tq

    def body(qa_ref, ka_ref, vt_ref, o_ref, lse_ref, m_sc, l_sc, acc_sc):
        i = pl.program_id(1)
        m_sc[...] = jnp.full((2, 1, tq), NEG, F32)
        l_sc[...] = jnp.zeros((2, 1, tq), F32)
        acc_sc[...] = jnp.zeros((2, HEAD_DIM, tq), F32)

        def block(j, nk, masked):
            keys = pl.ds(pl.multiple_of(j * tq, tq), nk * tq)
            vt_blk = vt_ref[:, keys]
            sts = [_dot(ka_ref[e, keys, :], qa_ref[e], _NT) for e in range(2)]
            pts, alphas = [], []
            for e in range(2):
                st = jnp.where(_key_le_query(tq), sts[e], NEG) if masked else sts[e]
                m_prev = m_sc[e]
                m_new = jnp.maximum(m_prev, jnp.max(st, axis=0, keepdims=True))
                alphas.append(jnp.exp(m_prev - m_new))
                pt = jnp.exp(st - m_new)
                l_sc[e] = alphas[e] * l_sc[e] + jnp.sum(pt, axis=0, keepdims=True)
                m_sc[e] = m_new
                pts.append(pt.astype(_MXU))
            for e in range(2):
                acc_sc[e] = alphas[e] * acc_sc[e] + _dot(vt_blk[HEAD_DIM * e:HEAD_DIM * (e + 1)], pts[e])

        def two_blocks(jj, carry):
            block(2 * jj, 2, False)
            return carry

        lax.fori_loop(0, i // 2, two_blocks, 0)

        @pl.when(i % 2 == 1)
        def _():
            block(i - 1, 1, False)

        block(i, 1, True)
        o_ref[...] = jnp.concatenate([acc_sc[0] / l_sc[0], acc_sc[1] / l_sc[1]], axis=0).T
        for e in range(2):
            lse_ref[e] = m_sc[e] + jnp.log(l_sc[e])

    body, ex_in, ex_specs, ex_out, ex_sems = _riding(body, 3, 2, ride, *_grid_ends(H // 2, nq))
    res = _pallas(
        body, out_shape=(jax.ShapeDtypeStruct((L, D_FOX), F32), jax.ShapeDtypeStruct((H, 1, L), F32), *ex_out),
        grid=(H // 2, nq),
        in_specs=[pl.BlockSpec((2, tq, PAIR), lambda p, i: (p, i, 0)), pl.BlockSpec((2, L, PAIR), lambda p, i: (p, 0, 0)),
                  pl.BlockSpec((None, PAIR, L), lambda p, i: (p, 0, 0)), *ex_specs],
        out_specs=(pl.BlockSpec((tq, PAIR), lambda p, i: (i, p)), pl.BlockSpec((2, 1, tq), lambda p, i: (p, 0, i)),
                   *ex_specs),
        scratch_shapes=[pltpu.VMEM((2, 1, tq), F32), pltpu.VMEM((2, 1, tq), F32), pltpu.VMEM((2, HEAD_DIM, tq), F32), *ex_sems],
        name="fox_fwd" if ride is None else "fox_fwd_gather")(qa, ka, vt, *ex_in)
    return res[0], res[1], list(res[2:])


def _fox_bwd(qa, ka, kat, proj, do, o, lse, ride=None):
    H, L, _ = qa.shape
    tq = min(TQ, L)
    nq = L // tq

    def body(qa_ref, ka_ref, kat_ref, v_ref, do_ref, o_ref, lse_ref, dqt_ref, dk_ref, dv_ref, delta_sc, dk_sc, dv_sc):
        j = pl.program_id(1)

        @pl.when(j == 0)
        def _():
            head_rows = (lax.broadcasted_iota(jnp.int32, (8, PAIR), 1) // HEAD_DIM
                         == lax.broadcasted_iota(jnp.int32, (8, PAIR), 0)).astype(F32)
            delta_sc[...] = lax.dot_general(head_rows, do_ref[...] * o_ref[...], _NT, precision=_HI,
                                            preferred_element_type=F32)
            dqt_ref[...] = jnp.zeros((2, PAIR, L), F32)

        dk_sc[...] = jnp.zeros((2, tq, PAIR), F32)
        dv_sc[...] = jnp.zeros((tq, PAIR), F32)
        vb = v_ref[...]

        def block(i, masked, nqb=1):
            qs = pl.ds(pl.multiple_of(i * tq, tq), nqb * tq)
            dob = do_ref[qs, :]
            for e in range(2):
                own = _own((nqb * tq, PAIR), e)
                qh = qa_ref[e, qs, :]
                pt = jnp.exp(_dot(ka_ref[e], qh, _NT) - lse_ref[e, :, qs])
                if masked:
                    pt = jnp.where(_key_le_query(tq), pt, 0.0)
                dv_sc[...] += _dot(pt, jnp.where(own, dob, 0.0))
                dpt = _dot(jnp.where(_own((tq, PAIR), e), vb, 0.0), dob, _NT)
                ds = (pt * (dpt - delta_sc[e:e + 1, qs])).astype(_MXU)
                dk_sc[e] += _dot(ds, qh)
                dqt_ref[e, :, qs] += _dot(kat_ref[e], ds)

        def two_blocks(ii, carry):
            block(j + 1 + 2 * ii, False, 2)
            return carry

        block(j, True)
        later = nq - 1 - j
        lax.fori_loop(0, later // 2, two_blocks, 0)

        @pl.when(later % 2 == 1)
        def _():
            block(nq - 1, False)

        dk_ref[...] = dk_sc[...]
        dv_ref[...] = dv_sc[...]

    nat = pl.BlockSpec((L, PAIR), lambda p, j: (0, p))
    body, ex_in, ex_specs, ex_out, ex_sems = _riding(body, 7, 3, ride, *_grid_ends(H // 2, nq))
    res = _pallas(
        body, out_shape=(jax.ShapeDtypeStruct((H, PAIR, L), F32), jax.ShapeDtypeStruct((H, L, PAIR), F32),
                         jax.ShapeDtypeStruct((L, D_FOX), F32), *ex_out),
        grid=(H // 2, nq),
        in_specs=[pl.BlockSpec((2, L, PAIR), lambda p, j: (p, 0, 0)), pl.BlockSpec((2, tq, PAIR), lambda p, j: (p, j, 0)),
                  pl.BlockSpec((2, PAIR, tq), lambda p, j: (p, 0, j)),
                  pl.BlockSpec((tq, PAIR), lambda p, j: (j, O_FV // PAIR + p)), nat, nat,
                  pl.BlockSpec((2, 1, L), lambda p, j: (p, 0, 0)), *ex_specs],
        out_specs=(pl.BlockSpec((2, PAIR, L), lambda p, j: (p, 0, 0)), pl.BlockSpec((2, tq, PAIR), lambda p, j: (p, j, 0)),
                   pl.BlockSpec((tq, PAIR), lambda p, j: (j, p)), *ex_specs),
        scratch_shapes=[pltpu.VMEM((8, L), F32), pltpu.VMEM((2, tq, PAIR), F32), pltpu.VMEM((tq, PAIR), F32), *ex_sems],
        name="fox_bwd" if ride is None else "fox_bwd_exchange",
        compiler_params=pltpu.CompilerParams(vmem_limit_bytes=VMEM_BIG))(qa, ka, kat, proj, do, o, lse, *ex_in)
    return res[0], res[1], res[2], list(res[3:])


def _fox_post_bwd(dqt, dkraw, proj, b):
    L = proj.shape[0]
    nb = L // TM

    def body(dqt_ref, dkr_ref, fl_ref, b_ref, dq_ref, dk_ref, dfl_ref, db_ref, carry_sc):
        first = pl.program_id(0) == 0

        @pl.when(first)
        def _():
            carry_sc[...] = jnp.zeros((1, PAIR), F32)

        lane = lax.broadcasted_iota(jnp.int32, (TM, PAIR), 1)
        rr = lax.broadcasted_iota(jnp.int32, (PAIR, PAIR), 0)
        cc = lax.broadcasted_iota(jnp.int32, (PAIR, PAIR), 1)
        dc = jnp.zeros((TM, PAIR), F32)
        for p in range(FOX_HEADS // 2):
            cols = slice(PAIR * p, PAIR * (p + 1))
            dqs = [dqt_ref[2 * p + e].T for e in range(2)]
            dks = [dkr_ref[2 * p + e] for e in range(2)]
            dq_ref[:, cols] = jnp.where(lane < HEAD_DIM, dqs[0], dqs[1]) * (1.0 / math.sqrt(HEAD_DIM))
            dk_ref[:, cols] = jnp.where(lane < HEAD_DIM, dks[0], dks[1])
            sums = jnp.zeros((TM, PAIR), F32)
            place = jnp.zeros((PAIR, PAIR), F32)
            for e in range(2):
                base = HEAD_DIM if e == 0 else 0
                sums = jnp.where(lane == base, dqs[e], jnp.where(lane == base + N_AUX, -dks[e], sums))
                place = jnp.where(((rr == base) | (rr == base + N_AUX)) & (cc == 2 * p + e), 1.0, place)
            dc = dc + _exact_dot(sums, place, "b")
        rs = _exact_dot(_tri(TM, False), dc, "a") + carry_sc[...]
        carry_sc[...] = rs[0:1, :]
        dfl = jnp.where(lane < FOX_HEADS, rs * jax.nn.sigmoid(-(fl_ref[...] + b_ref[...])), 0.0)
        dfl_ref[...] = dfl
        db = jnp.sum(dfl, axis=0, keepdims=True)

        @pl.when(first)
        def _():
            db_ref[...] = db

        @pl.when(jnp.logical_not(first))
        def _():
            db_ref[...] += db

    rev = lambda i: nb - 1 - i
    b, b_spec = _of_layer(b)
    nat = pl.BlockSpec((TM, D_FOX), lambda i: (rev(i), 0))
    return _pallas(
        body, out_shape=(jax.ShapeDtypeStruct((L, D_FOX), F32),) * 2
        + (jax.ShapeDtypeStruct((L, PAIR), F32), jax.ShapeDtypeStruct((1, PAIR), F32)),
        grid=(nb,),
        in_specs=[pl.BlockSpec((FOX_HEADS, PAIR, TM), lambda i: (0, 0, rev(i))),
                  pl.BlockSpec((FOX_HEADS, TM, PAIR), lambda i: (0, rev(i), 0)),
                  pl.BlockSpec((TM, PAIR), lambda i: (rev(i), O_FL // PAIR)), b_spec],
        out_specs=(nat, nat, pl.BlockSpec((TM, PAIR), lambda i: (rev(i), 0)), _whole((1, PAIR))),
        scratch_shapes=[pltpu.VMEM((1, PAIR), F32)], name="fox_post_bwd")(dqt, dkraw, proj, b)


def _s5_expand():
    r = lax.broadcasted_iota(jnp.int32, (S5_STATE, S5_STATE * S5_GROUP_CH), 0)
    c = lax.broadcasted_iota(jnp.int32, (S5_STATE, S5_STATE * S5_GROUP_CH), 1)
    return jnp.where(c // S5_GROUP_CH == r, 1.0, 0.0).astype(F32)


def _s5_disc_math(ar, ai, ldt, br, bi):
    dt = jnp.exp(ldt)
    mag = jnp.exp(ar * dt)
    lr = mag * jnp.cos(ai * dt)
    li = mag * jnp.sin(ai * dt)
    den = ar * ar + ai * ai
    fr = ((lr - 1.0) * ar + li * ai) / den
    fi = (li * ar - (lr - 1.0) * ai) / den
    e = _s5_expand()
    fre = jnp.dot(fr, e, precision=_HI, preferred_element_type=F32)
    fie = jnp.dot(fi, e, precision=_HI, preferred_element_type=F32)
    return lr, li, fre * br - fie * bi, fre * bi + fie * br


def _layer_blocks(arrs):
    return [pl.BlockSpec((None,) + a.shape[1:], lambda l: (l, 0, 0)) for a in arrs]


def _s5_disc(ar, ai, ldt, br, bi):
    def body(ar_ref, ai_ref, ldt_ref, br_ref, bi_ref, lr_ref, li_ref, bbr_ref, bbi_ref):
        lr, li, bbr, bbi = _s5_disc_math(ar_ref[...], ai_ref[...], ldt_ref[...], br_ref[...], bi_ref[...])
        lr_ref[...] = lr
        li_ref[...] = li
        bbr_ref[...] = bbr
        bbi_ref[...] = bbi

    ins = (ar, ai, ldt, br, bi)
    outs = (ar, ai, br, bi)
    return _pallas(body, out_shape=tuple(jax.ShapeDtypeStruct(a.shape, F32) for a in outs), grid=(DEPTH,),
                   in_specs=_layer_blocks(ins), out_specs=tuple(_layer_blocks(outs)), name="s5_disc")(*ins)


def _s5_disc_bwd(ar, ai, ldt, br, bi, dlr, dli, dbbr, dbbi):
    def body(ar_ref, ai_ref, ldt_ref, br_ref, bi_ref, dlr_ref, dli_ref, dbbr_ref, dbbi_ref,
             dar_ref, dai_ref, dldt_ref, dbr_ref, dbi_ref):
        _, vjp = jax.vjp(_s5_disc_math, ar_ref[...], ai_ref[...], ldt_ref[...], br_ref[...], bi_ref[...])
        dar, dai, dldt, dbr, dbi = vjp((dlr_ref[...], dli_ref[...], dbbr_ref[...], dbbi_ref[...]))
        dar_ref[...] = dar
        dai_ref[...] = dai
        dldt_ref[...] = dldt
        dbr_ref[...] = dbr
        dbi_ref[...] = dbi

    ins = (ar, ai, ldt, br, bi, dlr, dli, dbbr, dbbi)
    outs = (ar, ai, ldt, br, bi)
    return _pallas(body, out_shape=tuple(jax.ShapeDtypeStruct(a.shape, F32) for a in outs), grid=(DEPTH,),
                   in_specs=_layer_blocks(ins), out_specs=tuple(_layer_blocks(outs)), name="s5_disc_bwd")(*ins)


SLAB = 2 * S5_CH // 128
PITCH = 24


def _slab_rows(s, ts):
    return pl.ds(s, ts, stride=PITCH)


def _slab_pair(ref, s, ts):
    return jnp.concatenate([ref[_slab_rows(s, ts), :].astype(_MXU), ref[_slab_rows(s + 1, ts), :].astype(_MXU)], axis=-1)


def _s5_fwd(proj, wb, wc, lam, d, w_glu):
    L = proj.shape[0]
    ts = min(TS, L)

    def body(u_ref, wb_ref, wc_ref, lam_ref, d_ref, wg_ref, xs_ref, ypre_ref, ys_ref, b_sc, c_sc):
        @pl.when(pl.program_id(0) == 0)
        def _():
            c_sc[...] = jnp.zeros((SLAB, 128), F32)

        u = u_ref[...]
        ub = u.astype(_MXU)
        for s in range(0, SLAB, 2):
            b2 = _dot(ub, wb_ref[:, 128 * s:128 * (s + 2)])
            b_sc[_slab_rows(s, ts), :] = b2[:, :128]
            b_sc[_slab_rows(s + 1, ts), :] = b2[:, 128:]
        lr, li = lam_ref[0:8, :], lam_ref[8:16, :]

        def step(t, carry):
            xr, xi = carry
            row = pl.multiple_of(t * PITCH, 8)
            nr = lr * xr - li * xi + b_sc[pl.ds(row, 8), :]
            ni = lr * xi + li * xr + b_sc[pl.ds(row + 8, 8), :]
            xs_ref[pl.ds(row, 8), :] = nr
            xs_ref[pl.ds(row + 8, 8), :] = ni
            return nr, ni

        xr, xi = lax.fori_loop(0, ts, step, (c_sc[0:8, :], c_sc[8:16, :]), unroll=8)
        c_sc[0:8, :] = xr
        c_sc[8:16, :] = xi
        y = jnp.zeros((ts, D_S5), F32)
        for s in range(0, SLAB, 2):
            y = y + _dot(_slab_pair(xs_ref, s, ts), wc_ref[128 * s:128 * (s + 2), :])
        ypre_ref[...] = y
        y1 = jax.nn.gelu(y + d_ref[...] * u)
        ys_ref[...] = y1 * jax.nn.sigmoid(_dot(y1, wg_ref[...]))

    row = _rows(ts, D_S5)
    slabs = pl.BlockSpec((ts * PITCH, 128), lambda n: (n, 0))
    (wb, wb_spec), (wc, wc_spec), (lam, lam_spec), (d, d_spec) = (_of_layer(a) for a in (wb, wc, lam, d))
    return _pallas(
        body, out_shape=(jax.ShapeDtypeStruct((L * PITCH, 128), F32), jax.ShapeDtypeStruct((L, D_S5), F32),
                         jax.ShapeDtypeStruct((L, D_S5), F32)),
        grid=(L // ts,),
        in_specs=[_rows(ts, D_S5, O_SU // D_S5), wb_spec, wc_spec, lam_spec, d_spec, _whole((D_S5, D_S5))],
        out_specs=(slabs, row, row),
        scratch_shapes=[pltpu.VMEM((ts * PITCH, 128), F32), pltpu.VMEM((SLAB, 128), F32)], name="s5_fwd")(
            proj, wb, wc, lam, d, w_glu)


def _s5_bwd(proj, ypre, dys, xs, wb, wc, lam, d, w_glu):
    L = proj.shape[0]
    ts = min(TS, L)
    nb = L // ts

    def body(u_ref, y_ref, dys_ref, xs_ref, xp_ref, wb_ref, wc_ref, lam_ref, d_ref, wg_ref,
             du_ref, dwb_ref, dwc_ref, dlam_ref, dd_ref, dwg_ref, dx_sc, g_sc, c_sc):
        n = pl.program_id(0)

        @pl.when(n == 0)
        def _():
            c_sc[...] = jnp.zeros((SLAB, 128), F32)
            dlam_ref[...] = jnp.zeros((SLAB, 128), F32)
            dwb_ref[...] = jnp.zeros((D_S5, 2 * S5_CH), F32)
            dwc_ref[...] = jnp.zeros((2 * S5_CH, D_S5), F32)
            dd_ref[...] = jnp.zeros((1, D_S5), F32)
            dwg_ref[...] = jnp.zeros((D_S5, D_S5), F32)

        u, dv, dout = u_ref[...], d_ref[...], dys_ref[...]
        y1, gelu_vjp = jax.vjp(jax.nn.gelu, y_ref[...] + dv * u)
        sg = jax.nn.sigmoid(_dot(y1, wg_ref[...]))
        dz = dout * y1 * sg * (1.0 - sg)
        dy, = gelu_vjp(dout * sg + _dot(dz, wg_ref[...], _NT))
        dd_ref[...] += jnp.sum(dy * u, axis=0, keepdims=True)
        dwg_ref[...] += _dot(y1, dz, _TN)
        dyb = dy.astype(_MXU)
        for s in range(0, SLAB, 2):
            cols = slice(128 * s, 128 * (s + 2))
            dx2 = _dot(dyb, wc_ref[cols, :], _NT)
            dx_sc[_slab_rows(s, ts), :] = dx2[:, :128]
            dx_sc[_slab_rows(s + 1, ts), :] = dx2[:, 128:]
            dwc_ref[cols, :] += _dot(_slab_pair(xs_ref, s, ts), dyb, _TN)
        lr, li = lam_ref[0:8, :], lam_ref[8:16, :]

        def step(k, carry):
            gr, gi = carry
            row = pl.multiple_of((ts - 1 - k) * PITCH, 8)
            nr = dx_sc[pl.ds(row, 8), :] + lr * gr + li * gi
            ni = dx_sc[pl.ds(row + 8, 8), :] - li * gr + lr * gi
            g_sc[pl.ds(row, 8), :] = nr
            g_sc[pl.ds(row + 8, 8), :] = ni
            return nr, ni

        gr, gi = lax.fori_loop(0, ts, step, (c_sc[0:8, :], c_sc[8:16, :]), unroll=8)
        c_sc[0:8, :] = gr
        c_sc[8:16, :] = gi
        n1 = (ts - 1) * PITCH
        g3 = g_sc[pl.ds(PITCH, n1), :].reshape(ts - 1, PITCH, 128)
        x3 = xs_ref[pl.ds(0, n1), :].reshape(ts - 1, PITCH, 128)
        g3r, g3i, x3r, x3i = g3[:, 0:8], g3[:, 8:16], x3[:, 0:8], x3[:, 8:16]
        has_prev = jnp.where(n == nb - 1, 0.0, 1.0)
        g0r, g0i = g_sc[0:8, :], g_sc[8:16, :]
        pr, pi = xp_ref[0:8, :] * has_prev, xp_ref[8:16, :] * has_prev
        dlam_ref[0:8, :] += jnp.sum(g3r * x3r + g3i * x3i, axis=0) + g0r * pr + g0i * pi
        dlam_ref[8:16, :] += jnp.sum(g3i * x3r - g3r * x3i, axis=0) + g0i * pr - g0r * pi
        ub = u.astype(_MXU)
        du = dy * dv
        for s in range(0, SLAB, 2):
            cols = slice(128 * s, 128 * (s + 2))
            gs = _slab_pair(g_sc, s, ts)
            du = du + _dot(gs, wb_ref[:, cols], _NT)
            dwb_ref[:, cols] += _dot(ub, gs, _TN)
        du_ref[...] = du

    blk = lambda n: nb - 1 - n
    row = pl.BlockSpec((ts, D_S5), lambda n: (blk(n), 0))
    (wb, wb_spec), (wc, wc_spec), (lam, lam_spec), (d, d_spec) = (_of_layer(a) for a in (wb, wc, lam, d))
    return _pallas(
        body, out_shape=(jax.ShapeDtypeStruct((L, D_S5), F32), jax.ShapeDtypeStruct((D_S5, 2 * S5_CH), F32),
                         jax.ShapeDtypeStruct((2 * S5_CH, D_S5), F32), jax.ShapeDtypeStruct((SLAB, 128), F32),
                         jax.ShapeDtypeStruct((1, D_S5), F32), jax.ShapeDtypeStruct((D_S5, D_S5), F32)),
        grid=(nb,),
        in_specs=[pl.BlockSpec((ts, D_S5), lambda n: (blk(n), O_SU // D_S5)), row, row,
                  pl.BlockSpec((ts * PITCH, 128), lambda n: (blk(n), 0)),
                  pl.BlockSpec((PITCH, 128), lambda n: (jnp.maximum(blk(n) * ts - 1, 0), 0)),
                  wb_spec, wc_spec, lam_spec, d_spec, _whole((D_S5, D_S5))],
        out_specs=(row, _whole((D_S5, 2 * S5_CH)), _whole((2 * S5_CH, D_S5)), _whole((SLAB, 128)), _whole((1, D_S5)),
                   _whole((D_S5, D_S5))),
        scratch_shapes=[pltpu.VMEM((ts * PITCH, 128), F32), pltpu.VMEM((ts * PITCH, 128), F32), pltpu.VMEM((SLAB, 128), F32)],
        name="s5_bwd", compiler_params=pltpu.CompilerParams(vmem_limit_bytes=VMEM_BIG))(
            proj, ypre, dys, xs, xs, wb, wc, lam, d, w_glu)


def _rot(z, cos, sin):
    lane = lax.broadcasted_iota(jnp.int32, z.shape, 1)
    zs = z * sin
    half = HEAD_DIM // 2
    return z * cos + jnp.where(lane % HEAD_DIM < half, pltpu.roll(zs, PAIR - half, 1), pltpu.roll(zs, half, 1))


def _head_avg():
    r = lax.broadcasted_iota(jnp.int32, (PAIR, PAIR), 0) // HEAD_DIM
    c = lax.broadcasted_iota(jnp.int32, (PAIR, PAIR), 1) // HEAD_DIM
    return jnp.where(r == c, 1.0 / HEAD_DIM, 0.0).astype(F32)


def _ret_tables(tq):
    lg = jnp.log1p(-(2.0 ** (-5.0 - jnp.arange(RET_HEADS, dtype=F32))))
    scale = 1.0 / math.sqrt(HEAD_DIM)
    pos = jnp.arange(tq)
    n = pos.astype(F32)
    dist = jnp.abs(n[:, None] - n[None, :])
    ok = (pos[None, :] // CHUNK) <= (pos[:, None] // CHUNK)
    w = jnp.where(ok[None], scale * jnp.exp(lg[:, None, None] * dist[None]), 0.0)
    lgl = jnp.repeat(lg, HEAD_DIM)
    dq_tab = scale * jnp.exp(lgl[None, :] * (n[:, None] + 1.0))
    dk_tab = jnp.exp(lgl[None, :] * (tq - 1.0 - n[:, None]))
    blk = jnp.arange(PAIR) // HEAD_DIM
    bd = (blk[:, None] == blk[None, :]).astype(F32)
    gbd = bd[None] * jnp.exp(lgl.reshape(RET_HEADS // 2, PAIR)[:, :, None] * tq)
    return dict(w=w, wt=w.transpose(0, 2, 1), dq=dq_tab, dk=dk_tab, gbd=gbd, bd=bd)


def _ret_specs(tq, nq, rev, layer):
    blk = (lambda i: nq - 1 - i) if rev else (lambda i: i)
    col = lambda o: pl.BlockSpec((tq, PAIR), lambda p, i: (blk(i), o // PAIR + p))
    return dict(
        rq=col(O_RQ), rk=col(O_RK), rv=col(O_RV), nat=col(0),
        w=pl.BlockSpec((2, tq, tq), lambda p, i: (p, 0, 0)), tab=pl.BlockSpec((tq, PAIR), lambda p, i: (0, p)),
        gbd=pl.BlockSpec((None, PAIR, PAIR), lambda p, i: (p, 0, 0)), bd=pl.BlockSpec((PAIR, PAIR), lambda p, i: (0, 0)),
        gn=pl.BlockSpec((None, 1, PAIR), lambda p, i: (layer, 0, p)), dgn=pl.BlockSpec((1, PAIR), lambda p, i: (0, p)),
        st=pl.BlockSpec((None, None, PAIR, PAIR), lambda p, i: (p, blk(i), 0, 0)))


def _ret_fwd(proj, cos_t, sin_t, tabs, gn):
    L = proj.shape[0]
    tq = tabs["w"].shape[1]
    nq = L // tq

    def body(rq_ref, rk_ref, rv_ref, cos_ref, sin_ref, w_ref, dqt_ref, dkt_ref, gbd_ref, bd_ref, gn_ref,
             o_ref, y_ref, st_ref, s_sc):
        @pl.when(pl.program_id(1) == 0)
        def _():
            s_sc[...] = jnp.zeros((PAIR, PAIR), F32)

        state = s_sc[...]
        st_ref[...] = state
        cos, sin = cos_ref[...], sin_ref[...]
        q2, k2, v2 = _rot(rq_ref[...], cos, sin), _rot(rk_ref[...], cos, sin), rv_ref[...]
        owns = [_own((tq, PAIR), h) for h in range(2)]
        scores = [_dot(jnp.where(owns[h], q2, 0.0), k2, _NT) for h in range(2)]
        o = _dot(q2 * dqt_ref[...], state)
        for h in range(2):
            o = o + _dot(scores[h] * w_ref[h], jnp.where(owns[h], v2, 0.0))
        s_sc[...] = gbd_ref[...] * state + bd_ref[...] * _dot(k2 * dkt_ref[...], v2, _TN)
        o_ref[...] = o
        avg = _head_avg()
        oc = o - _exact_dot(o, avg, "b")
        y_ref[...] = oc * lax.rsqrt(_exact_dot(oc * oc, avg, "b") + EPS) * gn_ref[...]

    gn, layer = gn
    sp = _ret_specs(tq, nq, False, layer)
    nat = jax.ShapeDtypeStruct((L, D_RET), F32)
    return _pallas(
        body, out_shape=(nat, nat, jax.ShapeDtypeStruct((RET_HEADS // 2, nq, PAIR, PAIR), F32)), grid=(RET_HEADS // 2, nq),
        in_specs=[sp["rq"], sp["rk"], sp["rv"], sp["nat"], sp["nat"], sp["w"], sp["tab"], sp["tab"], sp["gbd"], sp["bd"],
                  sp["gn"]],
        out_specs=(sp["nat"], sp["nat"], sp["st"]), scratch_shapes=[pltpu.VMEM((PAIR, PAIR), F32)],
        name="ret_fwd")(proj, proj, proj, cos_t, sin_t, tabs["w"], tabs["dq"], tabs["dk"], tabs["gbd"], tabs["bd"], gn)


def _ret_bwd(proj, cos_t, sin_t, tabs, gn, o_pre, dy, states):
    L = proj.shape[0]
    tq = tabs["w"].shape[1]
    nq = L // tq

    def body(rq_ref, rk_ref, rv_ref, cos_ref, sin_ref, w_ref, wt_ref, dqt_ref, dkt_ref, gbd_ref, bd_ref, gn_ref,
             o_ref, dy_ref, st_ref, drq_ref, drk_ref, drv_ref, dgn_ref, g_sc):
        first = pl.program_id(1) == 0

        @pl.when(first)
        def _():
            g_sc[...] = jnp.zeros((PAIR, PAIR), F32)

        cos, sin = cos_ref[...], sin_ref[...]
        q2, k2, v2 = _rot(rq_ref[...], cos, sin), _rot(rk_ref[...], cos, sin), rv_ref[...]
        avg = _head_avg()
        ov, dyv = o_ref[...], dy_ref[...]
        oc = ov - _exact_dot(ov, avg, "b")
        r = lax.rsqrt(_exact_dot(oc * oc, avg, "b") + EPS)
        oh = oc * r
        dgn = jnp.sum(dyv * oh, axis=0, keepdims=True)
        doh = dyv * gn_ref[...]
        do = r * (doh - _exact_dot(doh, avg, "b") - oh * _exact_dot(doh * oh, avg, "b"))
        state, g = st_ref[...], g_sc[...]
        dqt, dkt = dqt_ref[...], dkt_ref[...]
        dq = _dot(do, state, _NT) * dqt
        dk = _dot(v2, g, _NT) * dkt
        dv = _dot(k2 * dkt, g)
        g_sc[...] = gbd_ref[...] * g + bd_ref[...] * _dot(q2 * dqt, do, _TN)
        owns = [_own((tq, PAIR), h) for h in range(2)]
        qms = [jnp.where(owns[h], q2, 0.0) for h in range(2)]
        doms = [jnp.where(owns[h], do, 0.0) for h in range(2)]
        ats = [_dot(k2, qms[h], _NT) for h in range(2)]
        das = [_dot(doms[h], v2, _NT) for h in range(2)]
        for h in range(2):
            dv = dv + _dot(ats[h] * wt_ref[h], doms[h])
            daw = (das[h] * w_ref[h]).astype(_MXU)
            dq = dq + _dot(daw, jnp.where(owns[h], k2, 0.0))
            dk = dk + _dot(daw.T, qms[h])
        drq_ref[...] = _rot(dq, cos, -sin)
        drk_ref[...] = _rot(dk, cos, -sin)
        drv_ref[...] = dv

        @pl.when(first)
        def _():
            dgn_ref[...] = dgn

        @pl.when(jnp.logical_not(first))
        def _():
            dgn_ref[...] += dgn

    gn, layer = gn
    sp = _ret_specs(tq, nq, True, layer)
    nat = jax.ShapeDtypeStruct((L, D_RET), F32)
    return _pallas(
        body, out_shape=(nat, nat, nat, jax.ShapeDtypeStruct((1, D_RET), F32)), grid=(RET_HEADS // 2, nq),
        in_specs=[sp["rq"], sp["rk"], sp["rv"], sp["nat"], sp["nat"], sp["w"], sp["w"], sp["tab"], sp["tab"], sp["gbd"],
                  sp["bd"], sp["gn"], sp["nat"], sp["nat"], sp["st"]],
        out_specs=(sp["nat"], sp["nat"], sp["nat"], sp["dgn"]), scratch_shapes=[pltpu.VMEM((PAIR, PAIR), F32)],
        name="ret_bwd")(proj, proj, proj, cos_t, sin_t, tabs["w"], tabs["wt"], tabs["dq"], tabs["dk"], tabs["gbd"],
                        tabs["bd"], gn, o_pre, dy, states)


def _gate_out(yf, ys, yr, proj, x, w):
    L = x.shape[0]

    def body(yf_ref, ys_ref, yr_ref, g_ref, x_ref, w_ref, xn_ref):
        cat = jnp.concatenate([yf_ref[...], ys_ref[...], yr_ref[...]], axis=-1)
        xn_ref[...] = x_ref[...] + _dot(cat * jax.nn.silu(g_ref[...]), w_ref[...])

    tm = min(TMB, L)
    full = _rows(tm, D_MODEL)
    return _pallas(body, out_shape=jax.ShapeDtypeStruct((L, D_MODEL), F32), grid=(L // tm,),
                   in_specs=[_rows(tm, D_FOX), _rows(tm, D_S5), _rows(tm, D_RET), _rows(tm, D_MODEL, O_GATE // D_MODEL),
                             full, _whole((D_MODEL, D_MODEL))],
                   out_specs=full, name="gate_out")(yf, ys, yr, proj, x, w)


def _gate_out_bwd(dxn, w, yf, ys, yr, proj):
    L = dxn.shape[0]

    def body(dx_ref, w_ref, yf_ref, ys_ref, yr_ref, g_ref, dyf_ref, dys_ref, dyr_ref, dg_ref, dw_ref):
        @pl.when(pl.program_id(0) == 0)
        def _():
            dw_ref[...] = jnp.zeros((D_MODEL, D_MODEL), F32)

        dxv = dx_ref[...].astype(_MXU)
        off = 0
        for y_ref, dy_ref in ((yf_ref, dyf_ref), (ys_ref, dys_ref), (yr_ref, dyr_ref)):
            cols = slice(off, off + y_ref.shape[1])
            off += y_ref.shape[1]
            dy = _dot(dxv, w_ref[cols, :], _NT)
            g = g_ref[:, cols]
            sg = jax.nn.sigmoid(g)
            silu = g * sg
            cat = y_ref[...]
            dy_ref[...] = dy * silu
            dg_ref[:, cols] = dy * cat * (sg * (1.0 + g * (1.0 - sg)))
            dw_ref[cols, :] += _dot(cat * silu, dxv, _TN)

    tm = min(TMB, L)
    full = _rows(tm, D_MODEL)
    f, s, r = _rows(tm, D_FOX), _rows(tm, D_S5), _rows(tm, D_RET)
    sq = _whole((D_MODEL, D_MODEL))
    return _pallas(body, out_shape=(jax.ShapeDtypeStruct((L, D_FOX), F32), jax.ShapeDtypeStruct((L, D_S5), F32),
                                    jax.ShapeDtypeStruct((L, D_RET), F32), jax.ShapeDtypeStruct((L, D_MODEL), F32),
                                    jax.ShapeDtypeStruct((D_MODEL, D_MODEL), F32)),
                   grid=(L // tm,), in_specs=[full, sq, f, s, r, _rows(tm, D_MODEL, O_GATE // D_MODEL)],
                   out_specs=(f, s, r, full, sq), name="gate_out_bwd",
                   compiler_params=pltpu.CompilerParams(vmem_limit_bytes=VMEM_BIG))(dxn, w, yf, ys, yr, proj)


def _final_loss(x, g, tgt):
    L = x.shape[0]

    def body(x_ref, g_ref, t_ref, loss_ref, dx_ref, dg_ref):
        xv, gv = x_ref[...], g_ref[...]
        r = lax.rsqrt(jnp.mean(xv * xv, axis=-1, keepdims=True) + EPS)
        err = xv * r * gv - t_ref[...]
        part = 0.5 * jnp.sum(jnp.mean(err * err, axis=-1, keepdims=True), axis=0, keepdims=True)
        dx, dg = _rms_bwd(xv, gv, err * (1.0 / D_MODEL))
        dx_ref[...] = dx

        @pl.when(pl.program_id(0) == 0)
        def _():
            loss_ref[...] = part
            dg_ref[...] = dg

        @pl.when(pl.program_id(0) != 0)
        def _():
            loss_ref[...] += part
            dg_ref[...] += dg

    full = _rows(TM, D_MODEL)
    return _pallas(body, out_shape=(jax.ShapeDtypeStruct((1, 1), F32), jax.ShapeDtypeStruct((L, D_MODEL), F32),
                                    jax.ShapeDtypeStruct((1, D_MODEL), F32)),
                   grid=(L // TM,), in_specs=[full, _whole((1, D_MODEL)), full],
                   out_specs=(_whole((1, 1)), full, _whole((1, D_MODEL))), name="final_loss")(x, g, tgt)


def _block_diag(blocks):
    n, g, r, c = blocks.shape
    eye = jnp.eye(g, dtype=blocks.dtype)
    return (blocks[:, :, :, None, :] * eye[None, :, None, :, None]).reshape(n, g * r, g * c)


def _diag_blocks(m, g):
    n, r, c = m.shape[0], m.shape[1] // g, m.shape[2] // g
    eye = jnp.eye(g, dtype=m.dtype)
    return jnp.sum(m.reshape(n, g, r, g, c) * eye[None, :, None, :, None], axis=3)


def _rope_tables(L):
    half = HEAD_DIM // 2
    freqs = ROPE_BASE ** (-jnp.arange(half, dtype=F32) / half)
    ang = jnp.arange(L, dtype=F32)[:, None] * freqs[None, :]
    cos, sin = jnp.cos(ang), jnp.sin(ang)
    cos_t = jnp.tile(jnp.concatenate([cos, cos], axis=-1), (1, RET_HEADS))
    sin_t = jnp.tile(jnp.concatenate([sin, -sin], axis=-1), (1, RET_HEADS))
    return cos_t, sin_t


def _s5_disc_args(small):
    g, s, ch = S5_GROUPS, S5_STATE, S5_GROUP_CH
    return (small["s5_a_re"], small["s5_a_im"], small["s5_log_dt"][:, :, None],
            small["s5_b_re"].reshape(DEPTH, g, s * ch), small["s5_b_im"].reshape(DEPTH, g, s * ch))


def _s5_mats(small):
    g, s, ch = S5_GROUPS, S5_STATE, S5_GROUP_CH
    lr, li, bbr, bbi = _s5_disc(*_s5_disc_args(small))
    lam = jnp.concatenate([lr.reshape(DEPTH, 8, 128), li.reshape(DEPTH, 8, 128)], axis=1)
    blocks = lambda a: _block_diag(a.astype(_MXU).transpose(0, 1, 3, 2))
    wb = jnp.concatenate([blocks(b.reshape(DEPTH, g, s, ch)) for b in (bbr, bbi)], axis=2)
    wc = jnp.concatenate([blocks(c) for c in (small["s5_c_re"], -small["s5_c_im"])], axis=1)
    return lam, wb, wc


def _s5_param_grads(small, dwb, dwc, dlam):
    g, s, ch = S5_GROUPS, S5_STATE, S5_GROUP_CH
    dc = [_diag_blocks(m, g).transpose(0, 1, 3, 2).reshape(DEPTH, g, ch * s) for m in (dwc[:, :S5_CH], dwc[:, S5_CH:])]
    dbb = [_diag_blocks(m, g).transpose(0, 1, 3, 2).reshape(DEPTH, g, s * ch) for m in (dwb[:, :, :S5_CH], dwb[:, :, S5_CH:])]
    dar, dai, dldt, dbr, dbi = _s5_disc_bwd(*_s5_disc_args(small), dlam[:, :8].reshape(DEPTH, g, s),
                                            dlam[:, 8:].reshape(DEPTH, g, s), dbb[0], dbb[1])
    return dict(s5_a_re=dar, s5_a_im=dai, s5_log_dt=dldt.reshape(DEPTH, g), s5_b_re=dbr, s5_b_im=dbi, s5_c_re=dc[0],
                s5_c_im=-dc[1])


_DENSE = ("s5_b_re", "s5_b_im", "s5_c_re", "s5_c_im")


def _layer_fwd(x, p, rope, ride=None, late=False):
    L = x.shape[0]
    cos_t, sin_t, ret_tabs = rope
    s = {"x": x}
    proj, h = _norm_inproj(x, p["norm_w"], p["w_in"])
    s["proj"], s["h"] = proj, h
    qa, ka, kat, vt = _fox_prep(proj, _fox_cumsum(proj, p["b_f"]))
    yf, lse, landed = _fox_fwd(qa, ka, vt, ride)
    s.update(qa=qa, ka=ka, kat=kat, lse=lse, yf=yf)
    if late:
        p["w_glu"], p["w_out"] = _gathered_rows(landed[-2]), _gathered_rows(landed[-1])
        landed = landed[:-2]
    xs, ypre, ys = _s5_fwd(proj, p["wb"], p["wc"], p["lam"], p["d"], p["w_glu"])
    s.update(xs=xs, ypre=ypre, ys=ys)
    o_pre, yr, states = _ret_fwd(proj, cos_t, sin_t, ret_tabs, p["gn_w"])
    s.update(o_pre=o_pre, yr=yr, states=states)
    return _gate_out(yf, ys, yr, proj, x, p["w_out"]), s, landed


def _layer_bwd(dxn, s, p, rope, ride=None, early=False):
    L = dxn.shape[0]
    cos_t, sin_t, ret_tabs = rope
    g = {}
    proj = s["proj"]
    dyf, dys, dyr, dgate, g["w_out"] = _gate_out_bwd(dxn, p["w_out"], s["yf"], s["ys"], s["yr"], proj)
    drq, drk, drv, dgn = _ret_bwd(proj, cos_t, sin_t, ret_tabs, p["gn_w"], s["o_pre"], dyr, s["states"])
    g["ret_gn_w"] = dgn.reshape(D_RET)
    dsu, g["wb"], g["wc"], g["lam"], dd, g["s5_w_glu"] = _s5_bwd(proj, s["ypre"], dys, s["xs"], p["wb"], p["wc"], p["lam"],
                                                                 p["d"], p["w_glu"])
    g["s5_d"] = dd.reshape(D_S5)
    if early:
        ride = (ride[0] + _row_slots(g), ride[1] + [True, True])
    dqt, dkraw, dv, landed = _fox_bwd(s["qa"], s["ka"], s["kat"], proj, dyf, s["yf"], s["lse"], ride)
    dq, dk, dfl, dbf = _fox_post_bwd(dqt, dkraw, proj, p["b_f"])
    g["fox_b_f"] = dbf[0, :FOX_HEADS]
    pieces = [dgate, dq, dk, dv, dsu, drq, drk, drv, dfl]
    dx, dnw, g["w_in"] = _inproj_bwd(pieces, p["w_in"], s["x"], p["norm_w"], dxn, s["h"])
    g["norm_w"] = dnw.reshape(D_MODEL)
    return dx, g, landed


def _stacked_params(small):
    lam, wb, wc = _s5_mats(small)
    row = lambda a: a[:, None, :]
    return dict(norm_w=row(small["norm_w"]), b_f=row(jnp.pad(small["fox_b_f"], ((0, 0), (0, PAIR - FOX_HEADS)))),
                lam=lam, wb=wb, wc=wc, d=row(small["s5_d"]), gn_w=row(small["ret_gn_w"]))


def _layer_params(l, w_in_p, w_glu, w_out, stacked):
    return dict({k: (a, l) for k, a in stacked.items()}, w_in=w_in_p, w_glu=w_glu, w_out=w_out)


_SHARDED = ("w_in", "s5_w_glu", "w_out")
_WIRE = jnp.bfloat16


_RUNS = ((2568, 3592, O_GATE), (0, 1536, O_FQ), (1544, 2568, O_SU), (1536, 1544, O_FL))


def _shard_pieces():
    out = []
    for a, b, pad in _RUNS:
        while a < b:
            j = a // W_SHARD
            e = min(b, (j + 1) * W_SHARD)
            out.append((j, a - j * W_SHARD, e - j * W_SHARD, pad))
            pad, a = pad + e - a, e
    return out


def _gathered_w_in(g_in):
    cols = [g_in[j, :, a:e] for j, a, e, _ in _shard_pieces()]
    cols.append(jnp.zeros((D_MODEL, D_INP - O_FL - FOX_HEADS), g_in.dtype))
    return jnp.concatenate(cols, axis=1)


def _gathered_rows(g):
    return g.reshape(-1, g.shape[-1])


def _w_in_slots(g):
    w_in = g["w_in"].astype(_WIRE)
    slots = []
    for j in range(N_DEV):
        mine = sorted((a, e, pad) for jj, a, e, pad in _shard_pieces() if jj == j)
        slots.append(jnp.concatenate([w_in[:, pad:pad + e - a] for a, e, pad in mine], axis=1))
    return jnp.stack(slots)


def _row_slots(g):
    return [g["s5_w_glu"].reshape(N_DEV, D_S5 // N_DEV, D_S5).astype(_WIRE),
            g["w_out"].reshape(N_DEV, D_MODEL // N_DEV, D_MODEL).astype(_WIRE)]


def _step_grads(x, tgt, small, full=None, shards=None):
    L = x.shape[0]
    rope = _rope_tables(L) + (_ret_tables(min(TQ, L)),)
    stacked = _stacked_params(small)
    if shards is not None:
        nxt = (_gathered_w_in(_exchange([shards[0][0]], [False], "gather_layer0")[0]), None, None)
    saved, params = [], []
    for l in range(DEPTH):
        weights = nxt if shards is not None else tuple(f[l] for f in full)
        ride = None
        if shards is not None:
            arrs = [s[l + 1] for s in shards] if l + 1 < DEPTH else []
            arrs += [shards[1][0], shards[2][0]] if l == 0 else []
            ride = (arrs, [False] * len(arrs)) if arrs else None
        params.append(_layer_params(l, *weights, stacked))
        x, s, landed = _layer_fwd(x, params[l], rope, ride, late=shards is not None and l == 0)
        if landed:
            nxt = (_gathered_w_in(landed[0]), _gathered_rows(landed[1]), _gathered_rows(landed[2]))
        saved.append(s)
    loss, dx, dfw = _final_loss(x, small["final_norm_w"][None], tgt)
    grads, partials, waiting = [None] * DEPTH, [None] * DEPTH, None
    for l in reversed(range(DEPTH)):
        ride = (waiting, [True] * len(waiting)) if waiting is not None else None
        dx, grads[l], landed = _layer_bwd(dx, saved[l], params[l], rope, ride, early=ride is not None and l == 0)
        if waiting is not None:
            partials[l + 1] = landed[:3]
        if shards is not None:
            waiting = [_w_in_slots(grads[l])] + _row_slots(grads[l])
    stack = lambda n: jnp.stack([g[n] for g in grads])
    small_g = {n: stack(n) for n in ("norm_w", "fox_b_f", "s5_d", "ret_gn_w")}
    small_g.update(_s5_param_grads(small, stack("wb"), stack("wc"), stack("lam")), final_norm_w=dfw)
    if shards is None:
        return loss, dx, grads, small_g
    last = _exchange([waiting[0]] + [small_g[n].astype(_WIRE) for n in _SMALL], [True] + [False] * len(_SMALL),
                     "exchange_layer0")
    partials[0] = [last[0]] + landed[3:]
    return loss, dx, grads, small_g, partials, dict(zip(_SMALL, last[1:]))


_MESH = pl.DeviceIdType.MESH
_ANY = pl.BlockSpec(memory_space=pl.ANY)


def _me_and_peers():
    x, y, c = lax.axis_index("x"), lax.axis_index("y"), lax.axis_index("c")
    flip = lambda a, bit: (1 - a) if bit else a
    peers = []
    for r in range(1, N_DEV):
        px, py, pc = flip(x, (r >> 2) & 1), flip(y, (r >> 1) & 1), flip(c, r & 1)
        peers.append(((px, py, pc), 4 * px + 2 * py + pc))
    return 4 * x + 2 * y + c, peers


def _exchange_copies(srcs, dsts, sems, scatter):
    send_sems, recv_sems, local_sems = sems
    me, peers = _me_and_peers()
    pick = lambda t, to: srcs[t].at[to] if scatter[t] else srcs[t]
    own = [pltpu.make_async_copy(pick(t, me), dsts[t].at[me], local_sems.at[t]) for t in range(len(srcs))]
    sends, waits = [], []
    for r, (dev, idx) in enumerate(peers):
        for t in range(len(srcs)):
            for land, out in ((me, sends), (idx, waits)):
                out.append(pltpu.make_async_remote_copy(pick(t, idx), dsts[t].at[land], send_sems.at[t, r], recv_sems.at[t, r],
                                                        device_id=dev, device_id_type=_MESH))
    return own, sends, waits


def _exchange_start(srcs, dsts, sems, scatter):
    own, sends, _ = _exchange_copies(srcs, dsts, sems, scatter)
    for cp in own + sends:
        cp.start()


def _exchange_wait(srcs, dsts, sems, scatter):
    own, _, waits = _exchange_copies(srcs, dsts, sems, scatter)
    for cp in waits + own:
        cp.wait()


def _exchange_shapes(arrs, scatter):
    outs = [jax.ShapeDtypeStruct(a.shape if sc else (N_DEV,) + a.shape, a.dtype) for a, sc in zip(arrs, scatter)]
    n = len(arrs)
    sems = [pltpu.SemaphoreType.DMA((n, N_DEV - 1)), pltpu.SemaphoreType.DMA((n, N_DEV - 1)), pltpu.SemaphoreType.DMA((n,))]
    return outs, sems


def _exchange(arrs, scatter, name):
    n = len(arrs)

    def body(*refs):
        _exchange_start(refs[:n], refs[n:2 * n], refs[2 * n:], scatter)
        _exchange_wait(refs[:n], refs[n:2 * n], refs[2 * n:], scatter)

    outs, sems = _exchange_shapes(arrs, scatter)
    return _pallas(body, out_shape=tuple(outs), in_specs=[_ANY] * n, out_specs=tuple([_ANY] * n), scratch_shapes=sems,
                   name=name)(*arrs)


def _riding(body, n_in, n_out, ride, is_first, is_last):
    if ride is None:
        return body, [], [], [], []
    arrs, scatter = ride
    n = len(arrs)
    outs, sems = _exchange_shapes(arrs, scatter)

    def wrapped(*refs):
        ins, srcs = refs[:n_in], refs[n_in:n_in + n]
        own_outs, dsts = refs[n_in + n:n_in + n + n_out], refs[n_in + n + n_out:n_in + 2 * n + n_out]
        scratch, ex_sems = refs[n_in + 2 * n + n_out:-3], refs[-3:]

        @pl.when(is_first())
        def _():
            _exchange_start(srcs, dsts, ex_sems, scatter)

        body(*ins, *own_outs, *scratch)

        @pl.when(is_last())
        def _():
            _exchange_wait(srcs, dsts, ex_sems, scatter)

    return wrapped, list(arrs), [_ANY] * n, outs, sems


def _adamw_body(p_ref, w_ref, m_ref, v_ref, g_ref, d_ref, nm_ref, nv_ref):
    g = p_ref[0].astype(F32)
    for i in range(1, N_DEV):
        g = g + p_ref[i].astype(F32)
    nm = ADAM_B1 * m_ref[...] + (1.0 - ADAM_B1) * g
    nv = ADAM_B2 * v_ref[...] + (1.0 - ADAM_B2) * jnp.square(g)
    m_hat = nm / (1.0 - ADAM_B1 ** ADAM_STEP)
    v_hat = nv / (1.0 - ADAM_B2 ** ADAM_STEP)
    g_ref[...] = g
    d_ref[...] = -ADAM_LR * (m_hat / (jnp.sqrt(v_hat) + ADAM_EPS) + ADAM_WD * w_ref[...])
    nm_ref[...] = nm
    nv_ref[...] = nv


def _adamw(parts, w, m, v, name):
    n, nb, rows, cols = parts.shape
    tm = next(t for t in (256, 128, 64, 32, 16) if rows % t == 0)

    def body(*refs):
        _adamw_body(*refs)

    row = pl.BlockSpec((None, tm, cols), lambda b, i: (b, i, 0))
    return _pallas(body, out_shape=(jax.ShapeDtypeStruct((nb, rows, cols), F32),) * 4, grid=(nb, rows // tm),
                   in_specs=[pl.BlockSpec((n, None, tm, cols), lambda b, i: (0, b, i, 0)), row, row, row],
                   out_specs=(row,) * 4, name=name)(parts, w, m, v)


def _adamw_whole(parts, w, m, v, name):
    def body(*refs):
        _adamw_body(*refs)

    if w.ndim == 2:
        grid = (1,)
        slab = pl.BlockSpec(w.shape, lambda b: (0, 0))
        part = pl.BlockSpec(parts.shape, lambda b: (0, 0, 0))
    else:
        grid, rest = (w.shape[0],), w.shape[1:]
        zeros = (0,) * len(rest)
        slab = pl.BlockSpec((None,) + rest, lambda b: (b,) + zeros)
        part = pl.BlockSpec((N_DEV, None) + rest, lambda b: (0, b) + zeros)
    return _pallas(body, out_shape=(jax.ShapeDtypeStruct(w.shape, F32),) * 4, grid=grid,
                   in_specs=[part, slab, slab, slab], out_specs=(slab,) * 4, name=name)(parts, w, m, v)


_WEIGHTS = ("norm_w", "w_in", "fox_b_f", "s5_a_re", "s5_a_im", "s5_b_re", "s5_b_im", "s5_c_re", "s5_c_im", "s5_d",
            "s5_log_dt", "s5_w_glu", "ret_gn_w", "w_out", "final_norm_w")
_SMALL = tuple(n for n in _WEIGHTS if n not in _SHARDED)


def kernel(x, norm_w, w_in, fox_b_f, s5_a_re, s5_a_im, s5_b_re, s5_b_im, s5_c_re, s5_c_im, s5_d, s5_log_dt, s5_w_glu, ret_gn_w, w_out, final_norm_w, loss_target, m_norm_w, m_w_in, m_fox_b_f, m_s5_a_re, m_s5_a_im, m_s5_b_re, m_s5_b_im, m_s5_c_re, m_s5_c_im, m_s5_d, m_s5_log_dt, m_s5_w_glu, m_ret_gn_w, m_w_out, m_final_norm_w, v_norm_w, v_w_in, v_fox_b_f, v_s5_a_re, v_s5_a_im, v_s5_b_re, v_s5_b_im, v_s5_c_re, v_s5_c_im, v_s5_d, v_s5_log_dt, v_s5_w_glu, v_ret_gn_w, v_w_out, v_final_norm_w):
    w = dict(norm_w=norm_w, w_in=w_in, fox_b_f=fox_b_f, s5_a_re=s5_a_re, s5_a_im=s5_a_im, s5_b_re=s5_b_re, s5_b_im=s5_b_im,
             s5_c_re=s5_c_re, s5_c_im=s5_c_im, s5_d=s5_d, s5_log_dt=s5_log_dt, s5_w_glu=s5_w_glu, ret_gn_w=ret_gn_w,
             w_out=w_out, final_norm_w=final_norm_w)
    m = dict(norm_w=m_norm_w, w_in=m_w_in, fox_b_f=m_fox_b_f, s5_a_re=m_s5_a_re, s5_a_im=m_s5_a_im, s5_b_re=m_s5_b_re,
             s5_b_im=m_s5_b_im, s5_c_re=m_s5_c_re, s5_c_im=m_s5_c_im, s5_d=m_s5_d, s5_log_dt=m_s5_log_dt,
             s5_w_glu=m_s5_w_glu, ret_gn_w=m_ret_gn_w, w_out=m_w_out, final_norm_w=m_final_norm_w)
    v = dict(norm_w=v_norm_w, w_in=v_w_in, fox_b_f=v_fox_b_f, s5_a_re=v_s5_a_re, s5_a_im=v_s5_a_im, s5_b_re=v_s5_b_re,
             s5_b_im=v_s5_b_im, s5_c_re=v_s5_c_re, s5_c_im=v_s5_c_im, s5_d=v_s5_d, s5_log_dt=v_s5_log_dt,
             s5_w_glu=v_s5_w_glu, ret_gn_w=v_ret_gn_w, w_out=v_w_out, final_norm_w=v_final_norm_w)

    small = {n: w[n] for n in _SMALL}
    loss, dx, _, _, partials, r_small = _step_grads(x[0], loss_target[0], small, shards=[w[n].astype(_MXU) for n in _SHARDED])

    res = {}
    for t, n in enumerate(_SHARDED):
        res[n] = _adamw(jnp.stack([partials[l][t] for l in range(DEPTH)], axis=1), w[n], m[n], v[n], "adamw_" + n)
    for n in _SMALL:
        shape = w[n].shape
        view = (1,) + shape if len(shape) == 1 else shape[:2] + (-1,) if n in _DENSE else shape
        outs = _adamw_whole(r_small[n], *[d[n].reshape(view) for d in (w, m, v)], "adamw_" + n)
        res[n] = [o.reshape(shape) for o in outs]

    loss = lax.psum(loss[0, 0], ("x", "y", "c"))
    return (loss, dx[None], *[res[n][0] for n in _WEIGHTS], *[res[n][1] for n in _WEIGHTS],
            *[res[n][2] for n in _WEIGHTS], *[res[n][3] for n in _WEIGHTS])
```

```python
import math

import jax
import jax.numpy as jnp
from jax import lax
from jax.experimental import pallas as pl
from jax.experimental.pallas import tpu as pltpu

F32 = jnp.float32
_MXU = jnp.bfloat16
_HI = lax.Precision.HIGHEST

N_DEV = 8
DEPTH = 4
D_MODEL = 1024
HEAD_DIM = 64
D_FOX = 512
FOX_HEADS = 8
D_S5 = 256
S5_GROUPS = 16
S5_GROUP_CH = 16
S5_STATE = 64
S5_CH = S5_GROUPS * S5_STATE
D_RET = 256
RET_HEADS = 4
CHUNK = 64
ROPE_BASE = 10000.0
EPS = 1e-6
D_IN = 3592
D_INP = 3712
W_SHARD = D_IN // N_DEV
O_GATE, O_FQ, O_FK, O_FV, O_SU, O_RQ, O_RK, O_RV, O_FL = 0, 1024, 1536, 2048, 2560, 2816, 3072, 3328, 3584

ADAM_LR, ADAM_B1, ADAM_B2, ADAM_EPS, ADAM_WD, ADAM_STEP = 0.001, 0.9, 0.999, 1e-08, 0.01, 10

TM = 256
TMB = 512
TQ = 512
TS = 512
NEG = -1e30
VMEM_BIG = 56 * 1024 * 1024


def _pallas(body, **kw):
    return pl.pallas_call(body, **kw)


def _whole(shape):
    n = len(shape)
    return pl.BlockSpec(shape, lambda *_: (0,) * n)


def _rows(tm, width, col=0):
    return pl.BlockSpec((tm, width), lambda i: (i, col))


def _of_layer(param):
    a, l = param
    return a, pl.BlockSpec((None,) + a.shape[1:], lambda *_: (l,) + (0,) * (a.ndim - 1))


def _dot(a, b, dims=(((1,), (0,)), ((), ()))):
    return lax.dot_general(a.astype(_MXU), b.astype(_MXU), dims, preferred_element_type=F32)


_NT = (((1,), (1,)), ((), ()))
_TN = (((0,), (0,)), ((), ()))


def _norm_inproj(x, g, w):
    L = x.shape[0]

    def body(x_ref, g_ref, w_ref, p_ref, h_ref):
        xv = x_ref[...]
        r = lax.rsqrt(jnp.mean(xv * xv, axis=-1, keepdims=True) + EPS)
        h = (xv * r * g_ref[...]).astype(_MXU)
        h_ref[...] = h
        p_ref[...] = _dot(h, w_ref[...])

    tm = min(TMB, L)
    g, g_spec = _of_layer(g)
    return _pallas(body, out_shape=(jax.ShapeDtypeStruct((L, D_INP), F32), jax.ShapeDtypeStruct((L, D_MODEL), _MXU)),
                   grid=(L // tm,),
                   in_specs=[_rows(tm, D_MODEL), g_spec,
                             pl.BlockSpec((D_MODEL, D_INP), lambda i: (0, 0), pipeline_mode=pl.Buffered(1))],
                   out_specs=(_rows(tm, D_INP), _rows(tm, D_MODEL)), name="norm_inproj",
                   compiler_params=pltpu.CompilerParams(vmem_limit_bytes=VMEM_BIG))(x, g, w)


def _rms_bwd(xv, g, dh):
    r = lax.rsqrt(jnp.mean(xv * xv, axis=-1, keepdims=True) + EPS)
    xh = xv * r
    dg = jnp.sum(dh * xh, axis=0, keepdims=True)
    dxh = dh * g
    dx = r * (dxh - xh * jnp.mean(dxh * xh, axis=-1, keepdims=True))
    return dx, dg


def _inproj_bwd(pieces, w, x, g, dres, h):
    L = x.shape[0]
    n = len(pieces)
    nb = L // TM

    def body(*refs):
        w_ref, x_ref, g_ref, dr_ref, h_ref, dx_ref, dg_ref, dw_ref, acc_sc = refs[n:]
        step = pl.program_id(0)

        @pl.when(step == 0)
        def _():
            acc_sc[...] = jnp.zeros((D_MODEL, D_INP), F32)
            dg_ref[...] = jnp.zeros((1, D_MODEL), F32)

        hv = h_ref[...]
        dh = jnp.zeros((TM, D_MODEL), F32)
        off = 0
        for r in refs[:n]:
            cols = slice(off, off + r.shape[1])
            off += r.shape[1]
            piece = r[...].astype(_MXU)
            dh = dh + _dot(piece, w_ref[:, cols], _NT)
            acc_sc[:, cols] += _dot(hv, piece, _TN)
        dx, dg = _rms_bwd(x_ref[...], g_ref[...], dh)
        dx_ref[...] = dx + dr_ref[...]
        dg_ref[...] += dg

        @pl.when(step == nb - 1)
        def _():
            dw_ref[...] = acc_sc[...].astype(_WIRE)

    resident = pl.BlockSpec((D_MODEL, D_INP), lambda i: (0, 0), pipeline_mode=pl.Buffered(1))
    g, g_spec = _of_layer(g)
    return _pallas(body, out_shape=(jax.ShapeDtypeStruct((L, D_MODEL), F32), jax.ShapeDtypeStruct((1, D_MODEL), F32),
                                    jax.ShapeDtypeStruct((D_MODEL, D_INP), _WIRE)),
                   grid=(nb,),
                   in_specs=[_rows(TM, p.shape[1]) for p in pieces]
                   + [resident, _rows(TM, D_MODEL), g_spec, _rows(TM, D_MODEL), _rows(TM, D_MODEL)],
                   out_specs=(_rows(TM, D_MODEL), _whole((1, D_MODEL)), resident),
                   scratch_shapes=[pltpu.VMEM((D_MODEL, D_INP), F32)], name="inproj_bwd",
                   compiler_params=pltpu.CompilerParams(vmem_limit_bytes=VMEM_BIG))(*pieces, w, x, g, dres, h)


PAIR = 2 * HEAD_DIM
N_AUX = 3


def _own(shape, h):
    return lax.broadcasted_iota(jnp.int32, shape, len(shape) - 1) // HEAD_DIM == h


def _split(x):
    parts = []
    for _ in range(N_AUX):
        part = x.astype(_MXU)
        parts.append(part)
        x = x - part.astype(F32)
    return parts


def _exact_dot(a, b, exact):
    if exact == "b":
        return sum(_dot(part, b) for part in _split(a))
    return sum(_dot(a, part) for part in _split(b))


def _tri(n, lower):
    r = lax.broadcasted_iota(jnp.int32, (n, n), 0)
    c = lax.broadcasted_iota(jnp.int32, (n, n), 1)
    return jnp.where(r >= c if lower else r <= c, 1.0, 0.0).astype(F32)


def _fox_prep(proj, b):
    L = proj.shape[0]

    def body(q_ref, k_ref, v_ref, fl_ref, b_ref, qa_ref, ka_ref, kat_ref, vt_ref, carry_sc):
        @pl.when(pl.program_id(0) == 0)
        def _():
            carry_sc[...] = jnp.zeros((1, PAIR), F32)

        lane = lax.broadcasted_iota(jnp.int32, (TM, PAIR), 1)
        lf = jnp.where(lane < FOX_HEADS, jax.nn.log_sigmoid(fl_ref[...] + b_ref[...]), 0.0)
        cv = _exact_dot(_tri(TM, True), lf, "a") + carry_sc[...]
        carry_sc[...] = cv[TM - 1:TM, :]
        for p in range(FOX_HEADS // 2):
            cols = slice(PAIR * p, PAIR * (p + 1))
            q2, k2 = q_ref[:, cols], k_ref[:, cols]
            vt_ref[p] = v_ref[:, cols].T.astype(_MXU)
            for e in range(2):
                h = 2 * p + e
                own = lane // HEAD_DIM == e
                a = lane - (HEAD_DIM if e == 0 else 0)
                rest = jnp.broadcast_to(cv[:, h:h + 1], (TM, PAIR))
                aux_q = jnp.where((a >= N_AUX) & (a < 2 * N_AUX), 1.0, 0.0)
                aux_k = jnp.where((a >= 0) & (a < N_AUX), 1.0, 0.0)
                for n in range(N_AUX):
                    part = rest.astype(_MXU).astype(F32)
                    rest = rest - part
                    aux_q = jnp.where(a == n, part, aux_q)
                    aux_k = jnp.where(a == N_AUX + n, -part, aux_k)
                ka = jnp.where(own, k2, aux_k)
                qa_ref[h] = jnp.where(own, q2 * (1.0 / math.sqrt(HEAD_DIM)), aux_q).astype(_MXU)
                ka_ref[h] = ka.astype(_MXU)
                kat_ref[h] = ka.T.astype(_MXU)

    hl = jax.ShapeDtypeStruct((FOX_HEADS, L, PAIR), _MXU)
    nat = lambda o: _rows(TM, D_FOX, o // D_FOX)
    rows = pl.BlockSpec((FOX_HEADS, TM, PAIR), lambda i: (0, i, 0))
    b, b_spec = _of_layer(b)
    return _pallas(
        body, out_shape=(hl, hl, jax.ShapeDtypeStruct((FOX_HEADS, PAIR, L), _MXU),
                         jax.ShapeDtypeStruct((FOX_HEADS // 2, PAIR, L), _MXU)),
        grid=(L // TM,), in_specs=[nat(O_FQ), nat(O_FK), nat(O_FV), _rows(TM, PAIR, O_FL // PAIR), b_spec],
        out_specs=(rows, rows, pl.BlockSpec((FOX_HEADS, PAIR, TM), lambda i: (0, 0, i)),
                   pl.BlockSpec((FOX_HEADS // 2, PAIR, TM), lambda i: (0, 0, i))),
        scratch_shapes=[pltpu.VMEM((1, PAIR), F32)], name="fox_prep")(proj, proj, proj, proj, b)


def _key_le_query(tq):
    return lax.broadcasted_iota(jnp.int32, (tq, tq), 0) <= lax.broadcasted_iota(jnp.int32, (tq, tq), 1)


def _grid_ends(n0, n1):
    first = lambda: (pl.program_id(0) == 0) & (pl.program_id(1) == 0)
    last = lambda: (pl.program_id(0) == n0 - 1) & (pl.program_id(1) == n1 - 1)
    return first, last


def _fox_fwd(qa, ka, vt, ride=None):
    H, L, _ = qa.shape
    tq = min(TQ, L)
    nq = L // tq

    def body(qa_ref, ka_ref, vt_ref, o_ref, lse_ref, m_sc, l_sc, acc_sc):
        i = pl.program_id(1)
        m_sc[...] = jnp.full((2, 1, tq), NEG, F32)
        l_sc[...] = jnp.zeros((2, 1, tq), F32)
        acc_sc[...] = jnp.zeros((2, HEAD_DIM, tq), F32)

        def block(j, nk, masked):
            keys = pl.ds(pl.multiple_of(j * tq, tq), nk * tq)
            vt_blk = vt_ref[:, keys]
            sts = [_dot(ka_ref[e, keys, :], qa_ref[e], _NT) for e in range(2)]
            pts, alphas = [], []
            for e in range(2):
                st = jnp.where(_key_le_query(tq), sts[e], NEG) if masked else sts[e]
                m_prev = m_sc[e]
                m_new = jnp.maximum(m_prev, jnp.max(st, axis=0, keepdims=True))
                alphas.append(jnp.exp(m_prev - m_new))
                pt = jnp.exp(st - m_new)
                l_sc[e] = alphas[e] * l_sc[e] + jnp.sum(pt, axis=0, keepdims=True)
                m_sc[e] = m_new
                pts.append(pt.astype(_MXU))
            for e in range(2):
                acc_sc[e] = alphas[e] * acc_sc[e] + _dot(vt_blk[HEAD_DIM * e:HEAD_DIM * (e + 1)], pts[e])

        def two_blocks(jj, carry):
            block(2 * jj, 2, False)
            return carry

        lax.fori_loop(0, i // 2, two_blocks, 0)

        @pl.when(i % 2 == 1)
        def _():
            block(i - 1, 1, False)

        block(i, 1, True)
        o_ref[...] = jnp.concatenate([acc_sc[0] / l_sc[0], acc_sc[1] / l_sc[1]], axis=0).T
        for e in range(2):
            lse_ref[e] = m_sc[e] + jnp.log(l_sc[e])

    body, ex_in, ex_specs, ex_out, ex_sems = _riding(body, 3, 2, ride, *_grid_ends(H // 2, nq))
    res = _pallas(
        body, out_shape=(jax.ShapeDtypeStruct((L, D_FOX), F32), jax.ShapeDtypeStruct((H, 1, L), F32), *ex_out),
        grid=(H // 2, nq),
        in_specs=[pl.BlockSpec((2, tq, PAIR), lambda p, i: (p, i, 0)), pl.BlockSpec((2, L, PAIR), lambda p, i: (p, 0, 0)),
                  pl.BlockSpec((None, PAIR, L), lambda p, i: (p, 0, 0)), *ex_specs],
        out_specs=(pl.BlockSpec((tq, PAIR), lambda p, i: (i, p)), pl.BlockSpec((2, 1, tq), lambda p, i: (p, 0, i)),
                   *ex_specs),
        scratch_shapes=[pltpu.VMEM((2, 1, tq), F32), pltpu.VMEM((2, 1, tq), F32), pltpu.VMEM((2, HEAD_DIM, tq), F32), *ex_sems],
        name="fox_fwd" if ride is None else "fox_fwd_gather")(qa, ka, vt, *ex_in)
    return res[0], res[1], list(res[2:])


def _fox_bwd(qa, ka, kat, proj, do, o, lse, ride=None):
    H, L, _ = qa.shape
    tq = min(TQ, L)
    nq = L // tq

    def body(qa_ref, ka_ref, kat_ref, v_ref, do_ref, o_ref, lse_ref, dqt_ref, dk_ref, dv_ref, delta_sc, dk_sc, dv_sc):
        j = pl.program_id(1)

        @pl.when(j == 0)
        def _():
            head_rows = (lax.broadcasted_iota(jnp.int32, (8, PAIR), 1) // HEAD_DIM
                         == lax.broadcasted_iota(jnp.int32, (8, PAIR), 0)).astype(F32)
            delta_sc[...] = lax.dot_general(head_rows, do_ref[...] * o_ref[...], _NT, precision=_HI,
                                            preferred_element_type=F32)
            dqt_ref[...] = jnp.zeros((2, PAIR, L), F32)

        dk_sc[...] = jnp.zeros((2, tq, PAIR), F32)
        dv_sc[...] = jnp.zeros((tq, PAIR), F32)
        vb = v_ref[...]

        def block(i, masked, nqb=1):
            qs = pl.ds(pl.multiple_of(i * tq, tq), nqb * tq)
            dob = do_ref[qs, :]
            for e in range(2):
                own = _own((nqb * tq, PAIR), e)
                qh = qa_ref[e, qs, :]
                pt = jnp.exp(_dot(ka_ref[e], qh, _NT) - lse_ref[e, :, qs])
                if masked:
                    pt = jnp.where(_key_le_query(tq), pt, 0.0)
                dv_sc[...] += _dot(pt, jnp.where(own, dob, 0.0))
                dpt = _dot(jnp.where(_own((tq, PAIR), e), vb, 0.0), dob, _NT)
                ds = (pt * (dpt - delta_sc[e:e + 1, qs])).astype(_MXU)
                dk_sc[e] += _dot(ds, qh)
                dqt_ref[e, :, qs] += _dot(kat_ref[e], ds)

        def two_blocks(ii, carry):
            block(j + 1 + 2 * ii, False, 2)
            return carry

        block(j, True)
        later = nq - 1 - j
        lax.fori_loop(0, later // 2, two_blocks, 0)

        @pl.when(later % 2 == 1)
        def _():
            block(nq - 1, False)

        dk_ref[...] = dk_sc[...]
        dv_ref[...] = dv_sc[...]

    nat = pl.BlockSpec((L, PAIR), lambda p, j: (0, p))
    body, ex_in, ex_specs, ex_out, ex_sems = _riding(body, 7, 3, ride, *_grid_ends(H // 2, nq))
    res = _pallas(
        body, out_shape=(jax.ShapeDtypeStruct((H, PAIR, L), F32), jax.ShapeDtypeStruct((H, L, PAIR), F32),
                         jax.ShapeDtypeStruct((L, D_FOX), F32), *ex_out),
        grid=(H // 2, nq),
        in_specs=[pl.BlockSpec((2, L, PAIR), lambda p, j: (p, 0, 0)), pl.BlockSpec((2, tq, PAIR), lambda p, j: (p, j, 0)),
                  pl.BlockSpec((2, PAIR, tq), lambda p, j: (p, 0, j)),
                  pl.BlockSpec((tq, PAIR), lambda p, j: (j, O_FV // PAIR + p)), nat, nat,
                  pl.BlockSpec((2, 1, L), lambda p, j: (p, 0, 0)), *ex_specs],
        out_specs=(pl.BlockSpec((2, PAIR, L), lambda p, j: (p, 0, 0)), pl.BlockSpec((2, tq, PAIR), lambda p, j: (p, j, 0)),
                   pl.BlockSpec((tq, PAIR), lambda p, j: (j, p)), *ex_specs),
        scratch_shapes=[pltpu.VMEM((8, L), F32), pltpu.VMEM((2, tq, PAIR), F32), pltpu.VMEM((tq, PAIR), F32), *ex_sems],
        name="fox_bwd" if ride is None else "fox_bwd_exchange",
        compiler_params=pltpu.CompilerParams(vmem_limit_bytes=VMEM_BIG))(qa, ka, kat, proj, do, o, lse, *ex_in)
    return res[0], res[1], res[2], list(res[3:])


def _fox_post_bwd(dqt, dkraw, proj, b):
    L = proj.shape[0]
    nb = L // TM

    def body(dqt_ref, dkr_ref, fl_ref, b_ref, dq_ref, dk_ref, dfl_ref, db_ref, carry_sc):
        first = pl.program_id(0) == 0

        @pl.when(first)
        def _():
            carry_sc[...] = jnp.zeros((1, PAIR), F32)

        lane = lax.broadcasted_iota(jnp.int32, (TM, PAIR), 1)
        rr = lax.broadcasted_iota(jnp.int32, (PAIR, PAIR), 0)
        cc = lax.broadcasted_iota(jnp.int32, (PAIR, PAIR), 1)
        dc = jnp.zeros((TM, PAIR), F32)
        for p in range(FOX_HEADS // 2):
            cols = slice(PAIR * p, PAIR * (p + 1))
            dqs = [dqt_ref[2 * p + e].T for e in range(2)]
            dks = [dkr_ref[2 * p + e] for e in range(2)]
            dq_ref[:, cols] = jnp.where(lane < HEAD_DIM, dqs[0], dqs[1]) * (1.0 / math.sqrt(HEAD_DIM))
            dk_ref[:, cols] = jnp.where(lane < HEAD_DIM, dks[0], dks[1])
            sums = jnp.zeros((TM, PAIR), F32)
            place = jnp.zeros((PAIR, PAIR), F32)
            for e in range(2):
                base = HEAD_DIM if e == 0 else 0
                sums = jnp.where(lane == base, dqs[e], jnp.where(lane == base + N_AUX, -dks[e], sums))
                place = jnp.where(((rr == base) | (rr == base + N_AUX)) & (cc == 2 * p + e), 1.0, place)
            dc = dc + _exact_dot(sums, place, "b")
        rs = _exact_dot(_tri(TM, False), dc, "a") + carry_sc[...]
        carry_sc[...] = rs[0:1, :]
        dfl = jnp.where(lane < FOX_HEADS, rs * jax.nn.sigmoid(-(fl_ref[...] + b_ref[...])), 0.0)
        dfl_ref[...] = dfl
        db = jnp.sum(dfl, axis=0, keepdims=True)

        @pl.when(first)
        def _():
            db_ref[...] = db

        @pl.when(jnp.logical_not(first))
        def _():
            db_ref[...] += db

    rev = lambda i: nb - 1 - i
    b, b_spec = _of_layer(b)
    nat = pl.BlockSpec((TM, D_FOX), lambda i: (rev(i), 0))
    return _pallas(
        body, out_shape=(jax.ShapeDtypeStruct((L, D_FOX), F32),) * 2
        + (jax.ShapeDtypeStruct((L, PAIR), F32), jax.ShapeDtypeStruct((1, PAIR), F32)),
        grid=(nb,),
        in_specs=[pl.BlockSpec((FOX_HEADS, PAIR, TM), lambda i: (0, 0, rev(i))),
                  pl.BlockSpec((FOX_HEADS, TM, PAIR), lambda i: (0, rev(i), 0)),
                  pl.BlockSpec((TM, PAIR), lambda i: (rev(i), O_FL // PAIR)), b_spec],
        out_specs=(nat, nat, pl.BlockSpec((TM, PAIR), lambda i: (rev(i), 0)), _whole((1, PAIR))),
        scratch_shapes=[pltpu.VMEM((1, PAIR), F32)], name="fox_post_bwd")(dqt, dkraw, proj, b)


def _s5_expand():
    r = lax.broadcasted_iota(jnp.int32, (S5_STATE, S5_STATE * S5_GROUP_CH), 0)
    c = lax.broadcasted_iota(jnp.int32, (S5_STATE, S5_STATE * S5_GROUP_CH), 1)
    return jnp.where(c // S5_GROUP_CH == r, 1.0, 0.0).astype(F32)


def _s5_disc_math(ar, ai, ldt, br, bi):
    dt = jnp.exp(ldt)
    mag = jnp.exp(ar * dt)
    lr = mag * jnp.cos(ai * dt)
    li = mag * jnp.sin(ai * dt)
    den = ar * ar + ai * ai
    fr = ((lr - 1.0) * ar + li * ai) / den
    fi = (li * ar - (lr - 1.0) * ai) / den
    e = _s5_expand()
    fre = jnp.dot(fr, e, precision=_HI, preferred_element_type=F32)
    fie = jnp.dot(fi, e, precision=_HI, preferred_element_type=F32)
    return lr, li, fre * br - fie * bi, fre * bi + fie * br


def _layer_blocks(arrs):
    return [pl.BlockSpec((None,) + a.shape[1:], lambda l: (l, 0, 0)) for a in arrs]


def _s5_disc(ar, ai, ldt, br, bi):
    def body(ar_ref, ai_ref, ldt_ref, br_ref, bi_ref, lr_ref, li_ref, bbr_ref, bbi_ref):
        lr, li, bbr, bbi = _s5_disc_math(ar_ref[...], ai_ref[...], ldt_ref[...], br_ref[...], bi_ref[...])
        lr_ref[...] = lr
        li_ref[...] = li
        bbr_ref[...] = bbr
        bbi_ref[...] = bbi

    ins = (ar, ai, ldt, br, bi)
    outs = (ar, ai, br, bi)
    return _pallas(body, out_shape=tuple(jax.ShapeDtypeStruct(a.shape, F32) for a in outs), grid=(DEPTH,),
                   in_specs=_layer_blocks(ins), out_specs=tuple(_layer_blocks(outs)), name="s5_disc")(*ins)


def _s5_disc_bwd(ar, ai, ldt, br, bi, dlr, dli, dbbr, dbbi):
    def body(ar_ref, ai_ref, ldt_ref, br_ref, bi_ref, dlr_ref, dli_ref, dbbr_ref, dbbi_ref,
             dar_ref, dai_ref, dldt_ref, dbr_ref, dbi_ref):
        _, vjp = jax.vjp(_s5_disc_math, ar_ref[...], ai_ref[...], ldt_ref[...], br_ref[...], bi_ref[...])
        dar, dai, dldt, dbr, dbi = vjp((dlr_ref[...], dli_ref[...], dbbr_ref[...], dbbi_ref[...]))
        dar_ref[...] = dar
        dai_ref[...] = dai
        dldt_ref[...] = dldt
        dbr_ref[...] = dbr
        dbi_ref[...] = dbi

    ins = (ar, ai, ldt, br, bi, dlr, dli, dbbr, dbbi)
    outs = (ar, ai, ldt, br, bi)
    return _pallas(body, out_shape=tuple(jax.ShapeDtypeStruct(a.shape, F32) for a in outs), grid=(DEPTH,),
                   in_specs=_layer_blocks(ins), out_specs=tuple(_layer_blocks(outs)), name="s5_disc_bwd")(*ins)


SLAB = 2 * S5_CH // 128
PITCH = 24


def _slab_rows(s, ts):
    return pl.ds(s, ts, stride=PITCH)


def _slab_pair(ref, s, ts):
    return jnp.concatenate([ref[_slab_rows(s, ts), :].astype(_MXU), ref[_slab_rows(s + 1, ts), :].astype(_MXU)], axis=-1)


def _s5_fwd(proj, wb, wc, lam, d, w_glu):
    L = proj.shape[0]
    ts = min(TS, L)

    def body(u_ref, wb_ref, wc_ref, lam_ref, d_ref, wg_ref, xs_ref, ypre_ref, ys_ref, b_sc, c_sc):
        @pl.when(pl.program_id(0) == 0)
        def _():
            c_sc[...] = jnp.zeros((SLAB, 128), F32)

        u = u_ref[...]
        ub = u.astype(_MXU)
        for s in range(0, SLAB, 2):
            b2 = _dot(ub, wb_ref[:, 128 * s:128 * (s + 2)])
            b_sc[_slab_rows(s, ts), :] = b2[:, :128]
            b_sc[_slab_rows(s + 1, ts), :] = b2[:, 128:]
        lr, li = lam_ref[0:8, :], lam_ref[8:16, :]

        def step(t, carry):
            xr, xi = carry
            row = pl.multiple_of(t * PITCH, 8)
            nr = lr * xr - li * xi + b_sc[pl.ds(row, 8), :]
            ni = lr * xi + li * xr + b_sc[pl.ds(row + 8, 8), :]
            xs_ref[pl.ds(row, 8), :] = nr
            xs_ref[pl.ds(row + 8, 8), :] = ni
            return nr, ni

        xr, xi = lax.fori_loop(0, ts, step, (c_sc[0:8, :], c_sc[8:16, :]), unroll=8)
        c_sc[0:8, :] = xr
        c_sc[8:16, :] = xi
        y = jnp.zeros((ts, D_S5), F32)
        for s in range(0, SLAB, 2):
            y = y + _dot(_slab_pair(xs_ref, s, ts), wc_ref[128 * s:128 * (s + 2), :])
        ypre_ref[...] = y
        y1 = jax.nn.gelu(y + d_ref[...] * u)
        ys_ref[...] = y1 * jax.nn.sigmoid(_dot(y1, wg_ref[...]))

    row = _rows(ts, D_S5)
    slabs = pl.BlockSpec((ts * PITCH, 128), lambda n: (n, 0))
    (wb, wb_spec), (wc, wc_spec), (lam, lam_spec), (d, d_spec) = (_of_layer(a) for a in (wb, wc, lam, d))
    return _pallas(
        body, out_shape=(jax.ShapeDtypeStruct((L * PITCH, 128), F32), jax.ShapeDtypeStruct((L, D_S5), F32),
                         jax.ShapeDtypeStruct((L, D_S5), F32)),
        grid=(L // ts,),
        in_specs=[_rows(ts, D_S5, O_SU // D_S5), wb_spec, wc_spec, lam_spec, d_spec, _whole((D_S5, D_S5))],
        out_specs=(slabs, row, row),
        scratch_shapes=[pltpu.VMEM((ts * PITCH, 128), F32), pltpu.VMEM((SLAB, 128), F32)], name="s5_fwd")(
            proj, wb, wc, lam, d, w_glu)


def _s5_bwd(proj, ypre, dys, xs, wb, wc, lam, d, w_glu):
    L = proj.shape[0]
    ts = min(TS, L)
    nb = L // ts

    def body(u_ref, y_ref, dys_ref, xs_ref, xp_ref, wb_ref, wc_ref, lam_ref, d_ref, wg_ref,
             du_ref, dwb_ref, dwc_ref, dlam_ref, dd_ref, dwg_ref, dx_sc, g_sc, c_sc):
        n = pl.program_id(0)

        @pl.when(n == 0)
        def _():
            c_sc[...] = jnp.zeros((SLAB, 128), F32)
            dlam_ref[...] = jnp.zeros((SLAB, 128), F32)
            dwb_ref[...] = jnp.zeros((D_S5, 2 * S5_CH), F32)
            dwc_ref[...] = jnp.zeros((2 * S5_CH, D_S5), F32)
            dd_ref[...] = jnp.zeros((1, D_S5), F32)
            dwg_ref[...] = jnp.zeros((D_S5, D_S5), F32)

        u, dv, dout = u_ref[...], d_ref[...], dys_ref[...]
        y1, gelu_vjp = jax.vjp(jax.nn.gelu, y_ref[...] + dv * u)
        sg = jax.nn.sigmoid(_dot(y1, wg_ref[...]))
        dz = dout * y1 * sg * (1.0 - sg)
        dy, = gelu_vjp(dout * sg + _dot(dz, wg_ref[...], _NT))
        dd_ref[...] += jnp.sum(dy * u, axis=0, keepdims=True)
        dwg_ref[...] += _dot(y1, dz, _TN)
        dyb = dy.astype(_MXU)
        for s in range(0, SLAB, 2):
            cols = slice(128 * s, 128 * (s + 2))
            dx2 = _dot(dyb, wc_ref[cols, :], _NT)
            dx_sc[_slab_rows(s, ts), :] = dx2[:, :128]
            dx_sc[_slab_rows(s + 1, ts), :] = dx2[:, 128:]
            dwc_ref[cols, :] += _dot(_slab_pair(xs_ref, s, ts), dyb, _TN)
        lr, li = lam_ref[0:8, :], lam_ref[8:16, :]

        def step(k, carry):
            gr, gi = carry
            row = pl.multiple_of((ts - 1 - k) * PITCH, 8)
            nr = dx_sc[pl.ds(row, 8), :] + lr * gr + li * gi
            ni = dx_sc[pl.ds(row + 8, 8), :] - li * gr + lr * gi
            g_sc[pl.ds(row, 8), :] = nr
            g_sc[pl.ds(row + 8, 8), :] = ni
            return nr, ni

        gr, gi = lax.fori_loop(0, ts, step, (c_sc[0:8, :], c_sc[8:16, :]), unroll=8)
        c_sc[0:8, :] = gr
        c_sc[8:16, :] = gi
        n1 = (ts - 1) * PITCH
        g3 = g_sc[pl.ds(PITCH, n1), :].reshape(ts - 1, PITCH, 128)
        x3 = xs_ref[pl.ds(0, n1), :].reshape(ts - 1, PITCH, 128)
        g3r, g3i, x3r, x3i = g3[:, 0:8], g3[:, 8:16], x3[:, 0:8], x3[:, 8:16]
        has_prev = jnp.where(n == nb - 1, 0.0, 1.0)
        g0r, g0i = g_sc[0:8, :], g_sc[8:16, :]
        pr, pi = xp_ref[0:8, :] * has_prev, xp_ref[8:16, :] * has_prev
        dlam_ref[0:8, :] += jnp.sum(g3r * x3r + g3i * x3i, axis=0) + g0r * pr + g0i * pi
        dlam_ref[8:16, :] += jnp.sum(g3i * x3r - g3r * x3i, axis=0) + g0i * pr - g0r * pi
        ub = u.astype(_MXU)
        du = dy * dv
        for s in range(0, SLAB, 2):
            cols = slice(128 * s, 128 * (s + 2))
            gs = _slab_pair(g_sc, s, ts)
            du = du + _dot(gs, wb_ref[:, cols], _NT)
            dwb_ref[:, cols] += _dot(ub, gs, _TN)
        du_ref[...] = du

    blk = lambda n: nb - 1 - n
    row = pl.BlockSpec((ts, D_S5), lambda n: (blk(n), 0))
    (wb, wb_spec), (wc, wc_spec), (lam, lam_spec), (d, d_spec) = (_of_layer(a) for a in (wb, wc, lam, d))
    return _pallas(
        body, out_shape=(jax.ShapeDtypeStruct((L, D_S5), F32), jax.ShapeDtypeStruct((D_S5, 2 * S5_CH), F32),
                         jax.ShapeDtypeStruct((2 * S5_CH, D_S5), F32), jax.ShapeDtypeStruct((SLAB, 128), F32),
                         jax.ShapeDtypeStruct((1, D_S5), F32), jax.ShapeDtypeStruct((D_S5, D_S5), F32)),
        grid=(nb,),
        in_specs=[pl.BlockSpec((ts, D_S5), lambda n: (blk(n), O_SU // D_S5)), row, row,
                  pl.BlockSpec((ts * PITCH, 128), lambda n: (blk(n), 0)),
                  pl.BlockSpec((PITCH, 128), lambda n: (jnp.maximum(blk(n) * ts - 1, 0), 0)),
                  wb_spec, wc_spec, lam_spec, d_spec, _whole((D_S5, D_S5))],
        out_specs=(row, _whole((D_S5, 2 * S5_CH)), _whole((2 * S5_CH, D_S5)), _whole((SLAB, 128)), _whole((1, D_S5)),
                   _whole((D_S5, D_S5))),
        scratch_shapes=[pltpu.VMEM((ts * PITCH, 128), F32), pltpu.VMEM((ts * PITCH, 128), F32), pltpu.VMEM((SLAB, 128), F32)],
        name="s5_bwd", compiler_params=pltpu.CompilerParams(vmem_limit_bytes=VMEM_BIG))(
            proj, ypre, dys, xs, xs, wb, wc, lam, d, w_glu)


def _rot(z, cos, sin):
    lane = lax.broadcasted_iota(jnp.int32, z.shape, 1)
    zs = z * sin
    half = HEAD_DIM // 2
    return z * cos + jnp.where(lane % HEAD_DIM < half, pltpu.roll(zs, PAIR - half, 1), pltpu.roll(zs, half, 1))


def _head_avg():
    r = lax.broadcasted_iota(jnp.int32, (PAIR, PAIR), 0) // HEAD_DIM
    c = lax.broadcasted_iota(jnp.int32, (PAIR, PAIR), 1) // HEAD_DIM
    return jnp.where(r == c, 1.0 / HEAD_DIM, 0.0).astype(F32)


def _ret_tables(tq):
    lg = jnp.log1p(-(2.0 ** (-5.0 - jnp.arange(RET_HEADS, dtype=F32))))
    scale = 1.0 / math.sqrt(HEAD_DIM)
    pos = jnp.arange(tq)
    n = pos.astype(F32)
    dist = jnp.abs(n[:, None] - n[None, :])
    ok = (pos[None, :] // CHUNK) <= (pos[:, None] // CHUNK)
    w = jnp.where(ok[None], scale * jnp.exp(lg[:, None, None] * dist[None]), 0.0)
    lgl = jnp.repeat(lg, HEAD_DIM)
    dq_tab = scale * jnp.exp(lgl[None, :] * (n[:, None] + 1.0))
    dk_tab = jnp.exp(lgl[None, :] * (tq - 1.0 - n[:, None]))
    blk = jnp.arange(PAIR) // HEAD_DIM
    bd = (blk[:, None] == blk[None, :]).astype(F32)
    gbd = bd[None] * jnp.exp(lgl.reshape(RET_HEADS // 2, PAIR)[:, :, None] * tq)
    return dict(w=w, wt=w.transpose(0, 2, 1), dq=dq_tab, dk=dk_tab, gbd=gbd, bd=bd)


def _ret_specs(tq, nq, rev, layer):
    blk = (lambda i: nq - 1 - i) if rev else (lambda i: i)
    col = lambda o: pl.BlockSpec((tq, PAIR), lambda p, i: (blk(i), o // PAIR + p))
    return dict(
        rq=col(O_RQ), rk=col(O_RK), rv=col(O_RV), nat=col(0),
        w=pl.BlockSpec((2, tq, tq), lambda p, i: (p, 0, 0)), tab=pl.BlockSpec((tq, PAIR), lambda p, i: (0, p)),
        gbd=pl.BlockSpec((None, PAIR, PAIR), lambda p, i: (p, 0, 0)), bd=pl.BlockSpec((PAIR, PAIR), lambda p, i: (0, 0)),
        gn=pl.BlockSpec((None, 1, PAIR), lambda p, i: (layer, 0, p)), dgn=pl.BlockSpec((1, PAIR), lambda p, i: (0, p)),
        st=pl.BlockSpec((None, None, PAIR, PAIR), lambda p, i: (p, blk(i), 0, 0)))


def _ret_fwd(proj, cos_t, sin_t, tabs, gn):
    L = proj.shape[0]
    tq = tabs["w"].shape[1]
    nq = L // tq

    def body(rq_ref, rk_ref, rv_ref, cos_ref, sin_ref, w_ref, dqt_ref, dkt_ref, gbd_ref, bd_ref, gn_ref,
             o_ref, y_ref, st_ref, s_sc):
        @pl.when(pl.program_id(1) == 0)
        def _():
            s_sc[...] = jnp.zeros((PAIR, PAIR), F32)

        state = s_sc[...]
        st_ref[...] = state
        cos, sin = cos_ref[...], sin_ref[...]
        q2, k2, v2 = _rot(rq_ref[...], cos, sin), _rot(rk_ref[...], cos, sin), rv_ref[...]
        owns = [_own((tq, PAIR), h) for h in range(2)]
        scores = [_dot(jnp.where(owns[h], q2, 0.0), k2, _NT) for h in range(2)]
        o = _dot(q2 * dqt_ref[...], state)
        for h in range(2):
            o = o + _dot(scores[h] * w_ref[h], jnp.where(owns[h], v2, 0.0))
        s_sc[...] = gbd_ref[...] * state + bd_ref[...] * _dot(k2 * dkt_ref[...], v2, _TN)
        o_ref[...] = o
        avg = _head_avg()
        oc = o - _exact_dot(o, avg, "b")
        y_ref[...] = oc * lax.rsqrt(_exact_dot(oc * oc, avg, "b") + EPS) * gn_ref[...]

    gn, layer = gn
    sp = _ret_specs(tq, nq, False, layer)
    nat = jax.ShapeDtypeStruct((L, D_RET), F32)
    return _pallas(
        body, out_shape=(nat, nat, jax.ShapeDtypeStruct((RET_HEADS // 2, nq, PAIR, PAIR), F32)), grid=(RET_HEADS // 2, nq),
        in_specs=[sp["rq"], sp["rk"], sp["rv"], sp["nat"], sp["nat"], sp["w"], sp["tab"], sp["tab"], sp["gbd"], sp["bd"],
                  sp["gn"]],
        out_specs=(sp["nat"], sp["nat"], sp["st"]), scratch_shapes=[pltpu.VMEM((PAIR, PAIR), F32)],
        name="ret_fwd")(proj, proj, proj, cos_t, sin_t, tabs["w"], tabs["dq"], tabs["dk"], tabs["gbd"], tabs["bd"], gn)


def _ret_bwd(proj, cos_t, sin_t, tabs, gn, o_pre, dy, states):
    L = proj.shape[0]
    tq = tabs["w"].shape[1]
    nq = L // tq

    def body(rq_ref, rk_ref, rv_ref, cos_ref, sin_ref, w_ref, wt_ref, dqt_ref, dkt_ref, gbd_ref, bd_ref, gn_ref,
             o_ref, dy_ref, st_ref, drq_ref, drk_ref, drv_ref, dgn_ref, g_sc):
        first = pl.program_id(1) == 0

        @pl.when(first)
        def _():
            g_sc[...] = jnp.zeros((PAIR, PAIR), F32)

        cos, sin = cos_ref[...], sin_ref[...]
        q2, k2, v2 = _rot(rq_ref[...], cos, sin), _rot(rk_ref[...], cos, sin), rv_ref[...]
        avg = _head_avg()
        ov, dyv = o_ref[...], dy_ref[...]
        oc = ov - _exact_dot(ov, avg, "b")
        r = lax.rsqrt(_exact_dot(oc * oc, avg, "b") + EPS)
        oh = oc * r
        dgn = jnp.sum(dyv * oh, axis=0, keepdims=True)
        doh = dyv * gn_ref[...]
        do = r * (doh - _exact_dot(doh, avg, "b") - oh * _exact_dot(doh * oh, avg, "b"))
        state, g = st_ref[...], g_sc[...]
        dqt, dkt = dqt_ref[...], dkt_ref[...]
        dq = _dot(do, state, _NT) * dqt
        dk = _dot(v2, g, _NT) * dkt
        dv = _dot(k2 * dkt, g)
        g_sc[...] = gbd_ref[...] * g + bd_ref[...] * _dot(q2 * dqt, do, _TN)
        owns = [_own((tq, PAIR), h) for h in range(2)]
        qms = [jnp.where(owns[h], q2, 0.0) for h in range(2)]
        doms = [jnp.where(owns[h], do, 0.0) for h in range(2)]
        ats = [_dot(k2, qms[h], _NT) for h in range(2)]
        das = [_dot(doms[h], v2, _NT) for h in range(2)]
        for h in range(2):
            dv = dv + _dot(ats[h] * wt_ref[h], doms[h])
            daw = (das[h] * w_ref[h]).astype(_MXU)
            dq = dq + _dot(daw, jnp.where(owns[h], k2, 0.0))
            dk = dk + _dot(daw.T, qms[h])
        drq_ref[...] = _rot(dq, cos, -sin)
        drk_ref[...] = _rot(dk, cos, -sin)
        drv_ref[...] = dv

        @pl.when(first)
        def _():
            dgn_ref[...] = dgn

        @pl.when(jnp.logical_not(first))
        def _():
            dgn_ref[...] += dgn

    gn, layer = gn
    sp = _ret_specs(tq, nq, True, layer)
    nat = jax.ShapeDtypeStruct((L, D_RET), F32)
    return _pallas(
        body, out_shape=(nat, nat, nat, jax.ShapeDtypeStruct((1, D_RET), F32)), grid=(RET_HEADS // 2, nq),
        in_specs=[sp["rq"], sp["rk"], sp["rv"], sp["nat"], sp["nat"], sp["w"], sp["w"], sp["tab"], sp["tab"], sp["gbd"],
                  sp["bd"], sp["gn"], sp["nat"], sp["nat"], sp["st"]],
        out_specs=(sp["nat"], sp["nat"], sp["nat"], sp["dgn"]), scratch_shapes=[pltpu.VMEM((PAIR, PAIR), F32)],
        name="ret_bwd")(proj, proj, proj, cos_t, sin_t, tabs["w"], tabs["wt"], tabs["dq"], tabs["dk"], tabs["gbd"],
                        tabs["bd"], gn, o_pre, dy, states)


def _gate_out(yf, ys, yr, proj, x, w):
    L = x.shape[0]

    def body(yf_ref, ys_ref, yr_ref, g_ref, x_ref, w_ref, xn_ref):
        cat = jnp.concatenate([yf_ref[...], ys_ref[...], yr_ref[...]], axis=-1)
        xn_ref[...] = x_ref[...] + _dot(cat * jax.nn.silu(g_ref[...]), w_ref[...])

    tm = min(TMB, L)
    full = _rows(tm, D_MODEL)
    return _pallas(body, out_shape=jax.ShapeDtypeStruct((L, D_MODEL), F32), grid=(L // tm,),
                   in_specs=[_rows(tm, D_FOX), _rows(tm, D_S5), _rows(tm, D_RET), _rows(tm, D_MODEL, O_GATE // D_MODEL),
                             full, _whole((D_MODEL, D_MODEL))],
                   out_specs=full, name="gate_out")(yf, ys, yr, proj, x, w)


def _gate_out_bwd(dxn, w, yf, ys, yr, proj):
    L = dxn.shape[0]

    def body(dx_ref, w_ref, yf_ref, ys_ref, yr_ref, g_ref, dyf_ref, dys_ref, dyr_ref, dg_ref, dw_ref):
        @pl.when(pl.program_id(0) == 0)
        def _():
            dw_ref[...] = jnp.zeros((D_MODEL, D_MODEL), F32)

        dxv = dx_ref[...].astype(_MXU)
        off = 0
        for y_ref, dy_ref in ((yf_ref, dyf_ref), (ys_ref, dys_ref), (yr_ref, dyr_ref)):
            cols = slice(off, off + y_ref.shape[1])
            off += y_ref.shape[1]
            dy = _dot(dxv, w_ref[cols, :], _NT)
            g = g_ref[:, cols]
            sg = jax.nn.sigmoid(g)
            silu = g * sg
            cat = y_ref[...]
            dy_ref[...] = dy * silu
            dg_ref[:, cols] = dy * cat * (sg * (1.0 + g * (1.0 - sg)))
            dw_ref[cols, :] += _dot(cat * silu, dxv, _TN)

    tm = min(TMB, L)
    full = _rows(tm, D_MODEL)
    f, s, r = _rows(tm, D_FOX), _rows(tm, D_S5), _rows(tm, D_RET)
    sq = _whole((D_MODEL, D_MODEL))
    return _pallas(body, out_shape=(jax.ShapeDtypeStruct((L, D_FOX), F32), jax.ShapeDtypeStruct((L, D_S5), F32),
                                    jax.ShapeDtypeStruct((L, D_RET), F32), jax.ShapeDtypeStruct((L, D_MODEL), F32),
                                    jax.ShapeDtypeStruct((D_MODEL, D_MODEL), F32)),
                   grid=(L // tm,), in_specs=[full, sq, f, s, r, _rows(tm, D_MODEL, O_GATE // D_MODEL)],
                   out_specs=(f, s, r, full, sq), name="gate_out_bwd",
                   compiler_params=pltpu.CompilerParams(vmem_limit_bytes=VMEM_BIG))(dxn, w, yf, ys, yr, proj)


def _final_loss(x, g, tgt):
    L = x.shape[0]

    def body(x_ref, g_ref, t_ref, loss_ref, dx_ref, dg_ref):
        xv, gv = x_ref[...], g_ref[...]
        r = lax.rsqrt(jnp.mean(xv * xv, axis=-1, keepdims=True) + EPS)
        err = xv * r * gv - t_ref[...]
        part = 0.5 * jnp.sum(jnp.mean(err * err, axis=-1, keepdims=True), axis=0, keepdims=True)
        dx, dg = _rms_bwd(xv, gv, err * (1.0 / D_MODEL))
        dx_ref[...] = dx

        @pl.when(pl.program_id(0) == 0)
        def _():
            loss_ref[...] = part
            dg_ref[...] = dg

        @pl.when(pl.program_id(0) != 0)
        def _():
            loss_ref[...] += part
            dg_ref[...] += dg

    full = _rows(TM, D_MODEL)
    return _pallas(body, out_shape=(jax.ShapeDtypeStruct((1, 1), F32), jax.ShapeDtypeStruct((L, D_MODEL), F32),
                                    jax.ShapeDtypeStruct((1, D_MODEL), F32)),
                   grid=(L // TM,), in_specs=[full, _whole((1, D_MODEL)), full],
                   out_specs=(_whole((1, 1)), full, _whole((1, D_MODEL))), name="final_loss")(x, g, tgt)


def _block_diag(blocks):
    n, g, r, c = blocks.shape
    eye = jnp.eye(g, dtype=blocks.dtype)
    return (blocks[:, :, :, None, :] * eye[None, :, None, :, None]).reshape(n, g * r, g * c)


def _diag_blocks(m, g):
    n, r, c = m.shape[0], m.shape[1] // g, m.shape[2] // g
    eye = jnp.eye(g, dtype=m.dtype)
    return jnp.sum(m.reshape(n, g, r, g, c) * eye[None, :, None, :, None], axis=3)


def _rope_tables(L):
    half = HEAD_DIM // 2
    freqs = ROPE_BASE ** (-jnp.arange(half, dtype=F32) / half)
    ang = jnp.arange(L, dtype=F32)[:, None] * freqs[None, :]
    cos, sin = jnp.cos(ang), jnp.sin(ang)
    cos_t = jnp.tile(jnp.concatenate([cos, cos], axis=-1), (1, RET_HEADS))
    sin_t = jnp.tile(jnp.concatenate([sin, -sin], axis=-1), (1, RET_HEADS))
    return cos_t, sin_t


def _s5_disc_args(small):
    g, s, ch = S5_GROUPS, S5_STATE, S5_GROUP_CH
    return (small["s5_a_re"], small["s5_a_im"], small["s5_log_dt"][:, :, None],
            small["s5_b_re"].reshape(DEPTH, g, s * ch), small["s5_b_im"].reshape(DEPTH, g, s * ch))


def _s5_mats(small):
    g, s, ch = S5_GROUPS, S5_STATE, S5_GROUP_CH
    lr, li, bbr, bbi = _s5_disc(*_s5_disc_args(small))
    lam = jnp.concatenate([lr.reshape(DEPTH, 8, 128), li.reshape(DEPTH, 8, 128)], axis=1)
    blocks = lambda a: _block_diag(a.astype(_MXU).transpose(0, 1, 3, 2))
    wb = jnp.concatenate([blocks(b.reshape(DEPTH, g, s, ch)) for b in (bbr, bbi)], axis=2)
    wc = jnp.concatenate([blocks(c) for c in (small["s5_c_re"], -small["s5_c_im"])], axis=1)
    return lam, wb, wc


def _s5_param_grads(small, dwb, dwc, dlam):
    g, s, ch = S5_GROUPS, S5_STATE, S5_GROUP_CH
    dc = [_diag_blocks(m, g).transpose(0, 1, 3, 2).reshape(DEPTH, g, ch * s) for m in (dwc[:, :S5_CH], dwc[:, S5_CH:])]
    dbb = [_diag_blocks(m, g).transpose(0, 1, 3, 2).reshape(DEPTH, g, s * ch) for m in (dwb[:, :, :S5_CH], dwb[:, :, S5_CH:])]
    dar, dai, dldt, dbr, dbi = _s5_disc_bwd(*_s5_disc_args(small), dlam[:, :8].reshape(DEPTH, g, s),
                                            dlam[:, 8:].reshape(DEPTH, g, s), dbb[0], dbb[1])
    return dict(s5_a_re=dar, s5_a_im=dai, s5_log_dt=dldt.reshape(DEPTH, g), s5_b_re=dbr, s5_b_im=dbi, s5_c_re=dc[0],
                s5_c_im=-dc[1])


_DENSE = ("s5_b_re", "s5_b_im", "s5_c_re", "s5_c_im")


def _layer_fwd(x, p, rope, ride=None, late=False):
    L = x.shape[0]
    cos_t, sin_t, ret_tabs = rope
    s = {"x": x}
    proj, h = _norm_inproj(x, p["norm_w"], p["w_in"])
    s["proj"], s["h"] = proj, h
    qa, ka, kat, vt = _fox_prep(proj, p["b_f"])
    yf, lse, landed = _fox_fwd(qa, ka, vt, ride)
    s.update(qa=qa, ka=ka, kat=kat, lse=lse, yf=yf)
    if late:
        p["w_glu"], p["w_out"] = _gathered_rows(landed[-2]), _gathered_rows(landed[-1])
        landed = landed[:-2]
    xs, ypre, ys = _s5_fwd(proj, p["wb"], p["wc"], p["lam"], p["d"], p["w_glu"])
    s.update(xs=xs, ypre=ypre, ys=ys)
    o_pre, yr, states = _ret_fwd(proj, cos_t, sin_t, ret_tabs, p["gn_w"])
    s.update(o_pre=o_pre, yr=yr, states=states)
    return _gate_out(yf, ys, yr, proj, x, p["w_out"]), s, landed


def _layer_bwd(dxn, s, p, rope, ride=None, early=False):
    L = dxn.shape[0]
    cos_t, sin_t, ret_tabs = rope
    g = {}
    proj = s["proj"]
    dyf, dys, dyr, dgate, g["w_out"] = _gate_out_bwd(dxn, p["w_out"], s["yf"], s["ys"], s["yr"], proj)
    drq, drk, drv, dgn = _ret_bwd(proj, cos_t, sin_t, ret_tabs, p["gn_w"], s["o_pre"], dyr, s["states"])
    g["ret_gn_w"] = dgn.reshape(D_RET)
    dsu, g["wb"], g["wc"], g["lam"], dd, g["s5_w_glu"] = _s5_bwd(proj, s["ypre"], dys, s["xs"], p["wb"], p["wc"], p["lam"],
                                                                 p["d"], p["w_glu"])
    g["s5_d"] = dd.reshape(D_S5)
    if early:
        ride = (ride[0] + _row_slots(g), ride[1] + [True, True])
    dqt, dkraw, dv, landed = _fox_bwd(s["qa"], s["ka"], s["kat"], proj, dyf, s["yf"], s["lse"], ride)
    dq, dk, dfl, dbf = _fox_post_bwd(dqt, dkraw, proj, p["b_f"])
    g["fox_b_f"] = dbf[0, :FOX_HEADS]
    pieces = [dgate, dq, dk, dv, dsu, drq, drk, drv, dfl]
    dx, dnw, g["w_in"] = _inproj_bwd(pieces, p["w_in"], s["x"], p["norm_w"], dxn, s["h"])
    g["norm_w"] = dnw.reshape(D_MODEL)
    return dx, g, landed


def _stacked_params(small):
    lam, wb, wc = _s5_mats(small)
    row = lambda a: a[:, None, :]
    return dict(norm_w=row(small["norm_w"]), b_f=row(jnp.pad(small["fox_b_f"], ((0, 0), (0, PAIR - FOX_HEADS)))),
                lam=lam, wb=wb, wc=wc, d=row(small["s5_d"]), gn_w=row(small["ret_gn_w"]))


def _layer_params(l, w_in_p, w_glu, w_out, stacked):
    return dict({k: (a, l) for k, a in stacked.items()}, w_in=w_in_p, w_glu=w_glu, w_out=w_out)


_SHARDED = ("w_in", "s5_w_glu", "w_out")
_WIRE = jnp.bfloat16


_RUNS = ((2568, 3592, O_GATE), (0, 1536, O_FQ), (1544, 2568, O_SU), (1536, 1544, O_FL))


def _shard_pieces():
    out = []
    for a, b, pad in _RUNS:
        while a < b:
            j = a // W_SHARD
            e = min(b, (j + 1) * W_SHARD)
            out.append((j, a - j * W_SHARD, e - j * W_SHARD, pad))
            pad, a = pad + e - a, e
    return out


def _gathered_w_in(g_in):
    cols = [g_in[j, :, a:e] for j, a, e, _ in _shard_pieces()]
    cols.append(jnp.zeros((D_MODEL, D_INP - O_FL - FOX_HEADS), g_in.dtype))
    return jnp.concatenate(cols, axis=1)


def _gathered_rows(g):
    return g.reshape(-1, g.shape[-1])


def _w_in_slots(g):
    w_in = g["w_in"].astype(_WIRE)
    slots = []
    for j in range(N_DEV):
        mine = sorted((a, e, pad) for jj, a, e, pad in _shard_pieces() if jj == j)
        slots.append(jnp.concatenate([w_in[:, pad:pad + e - a] for a, e, pad in mine], axis=1))
    return jnp.stack(slots)


def _row_slots(g):
    return [g["s5_w_glu"].reshape(N_DEV, D_S5 // N_DEV, D_S5).astype(_WIRE),
            g["w_out"].reshape(N_DEV, D_MODEL // N_DEV, D_MODEL).astype(_WIRE)]


def _step_grads(x, tgt, small, full=None, shards=None):
    L = x.shape[0]
    rope = _rope_tables(L) + (_ret_tables(min(TQ, L)),)
    stacked = _stacked_params(small)
    if shards is not None:
        nxt = (_gathered_w_in(_exchange([shards[0][0]], [False], "gather_layer0")[0]), None, None)
    saved, params = [], []
    for l in range(DEPTH):
        weights = nxt if shards is not None else tuple(f[l] for f in full)
        ride = None
        if shards is not None:
            arrs = [s[l + 1] for s in shards] if l + 1 < DEPTH else []
            arrs += [shards[1][0], shards[2][0]] if l == 0 else []
            ride = (arrs, [False] * len(arrs)) if arrs else None
        params.append(_layer_params(l, *weights, stacked))
        x, s, landed = _layer_fwd(x, params[l], rope, ride, late=shards is not None and l == 0)
        if landed:
            nxt = (_gathered_w_in(landed[0]), _gathered_rows(landed[1]), _gathered_rows(landed[2]))
        saved.append(s)
    loss, dx, dfw = _final_loss(x, small["final_norm_w"][None], tgt)
    grads, partials, waiting = [None] * DEPTH, [None] * DEPTH, None
    for l in reversed(range(DEPTH)):
        ride = (waiting, [True] * len(waiting)) if waiting is not None else None
        dx, grads[l], landed = _layer_bwd(dx, saved[l], params[l], rope, ride, early=ride is not None and l == 0)
        if waiting is not None:
            partials[l + 1] = landed[:3]
        if shards is not None:
            waiting = [_w_in_slots(grads[l])] + _row_slots(grads[l])
    stack = lambda n: jnp.stack([g[n] for g in grads])
    small_g = {n: stack(n) for n in ("norm_w", "fox_b_f", "s5_d", "ret_gn_w")}
    small_g.update(_s5_param_grads(small, stack("wb"), stack("wc"), stack("lam")), final_norm_w=dfw)
    if shards is None:
        return loss, dx, grads, small_g
    last = _exchange([waiting[0]] + [small_g[n].astype(_WIRE) for n in _SMALL], [True] + [False] * len(_SMALL),
                     "exchange_layer0")
    partials[0] = [last[0]] + landed[3:]
    return loss, dx, grads, small_g, partials, dict(zip(_SMALL, last[1:]))


_MESH = pl.DeviceIdType.MESH
_ANY = pl.BlockSpec(memory_space=pl.ANY)


def _me_and_peers():
    x, y, c = lax.axis_index("x"), lax.axis_index("y"), lax.axis_index("c")
    flip = lambda a, bit: (1 - a) if bit else a
    peers = []
    for r in range(1, N_DEV):
        px, py, pc = flip(x, (r >> 2) & 1), flip(y, (r >> 1) & 1), flip(c, r & 1)
        peers.append(((px, py, pc), 4 * px + 2 * py + pc))
    return 4 * x + 2 * y + c, peers


def _exchange_copies(srcs, dsts, sems, scatter):
    send_sems, recv_sems, local_sems = sems
    me, peers = _me_and_peers()
    pick = lambda t, to: srcs[t].at[to] if scatter[t] else srcs[t]
    own = [pltpu.make_async_copy(pick(t, me), dsts[t].at[me], local_sems.at[t]) for t in range(len(srcs))]
    sends, waits = [], []
    for r, (dev, idx) in enumerate(peers):
        for t in range(len(srcs)):
            for land, out in ((me, sends), (idx, waits)):
                out.append(pltpu.make_async_remote_copy(pick(t, idx), dsts[t].at[land], send_sems.at[t, r], recv_sems.at[t, r],
                                                        device_id=dev, device_id_type=_MESH))
    return own, sends, waits


def _exchange_start(srcs, dsts, sems, scatter):
    own, sends, _ = _exchange_copies(srcs, dsts, sems, scatter)
    for cp in own + sends:
        cp.start()


def _exchange_wait(srcs, dsts, sems, scatter):
    own, _, waits = _exchange_copies(srcs, dsts, sems, scatter)
    for cp in waits + own:
        cp.wait()


def _exchange_shapes(arrs, scatter):
    outs = [jax.ShapeDtypeStruct(a.shape if sc else (N_DEV,) + a.shape, a.dtype) for a, sc in zip(arrs, scatter)]
    n = len(arrs)
    sems = [pltpu.SemaphoreType.DMA((n, N_DEV - 1)), pltpu.SemaphoreType.DMA((n, N_DEV - 1)), pltpu.SemaphoreType.DMA((n,))]
    return outs, sems


def _exchange(arrs, scatter, name):
    n = len(arrs)

    def body(*refs):
        _exchange_start(refs[:n], refs[n:2 * n], refs[2 * n:], scatter)
        _exchange_wait(refs[:n], refs[n:2 * n], refs[2 * n:], scatter)

    outs, sems = _exchange_shapes(arrs, scatter)
    return _pallas(body, out_shape=tuple(outs), in_specs=[_ANY] * n, out_specs=tuple([_ANY] * n), scratch_shapes=sems,
                   name=name)(*arrs)


def _riding(body, n_in, n_out, ride, is_first, is_last):
    if ride is None:
        return body, [], [], [], []
    arrs, scatter = ride
    n = len(arrs)
    outs, sems = _exchange_shapes(arrs, scatter)

    def wrapped(*refs):
        ins, srcs = refs[:n_in], refs[n_in:n_in + n]
        own_outs, dsts = refs[n_in + n:n_in + n + n_out], refs[n_in + n + n_out:n_in + 2 * n + n_out]
        scratch, ex_sems = refs[n_in + 2 * n + n_out:-3], refs[-3:]

        @pl.when(is_first())
        def _():
            _exchange_start(srcs, dsts, ex_sems, scatter)

        body(*ins, *own_outs, *scratch)

        @pl.when(is_last())
        def _():
            _exchange_wait(srcs, dsts, ex_sems, scatter)

    return wrapped, list(arrs), [_ANY] * n, outs, sems


def _adamw_body(p_ref, w_ref, m_ref, v_ref, g_ref, d_ref, nm_ref, nv_ref):
    g = p_ref[0].astype(F32)
    for i in range(1, N_DEV):
        g = g + p_ref[i].astype(F32)
    nm = ADAM_B1 * m_ref[...] + (1.0 - ADAM_B1) * g
    nv = ADAM_B2 * v_ref[...] + (1.0 - ADAM_B2) * jnp.square(g)
    m_hat = nm / (1.0 - ADAM_B1 ** ADAM_STEP)
    v_hat = nv / (1.0 - ADAM_B2 ** ADAM_STEP)
    g_ref[...] = g
    d_ref[...] = -ADAM_LR * (m_hat / (jnp.sqrt(v_hat) + ADAM_EPS) + ADAM_WD * w_ref[...])
    nm_ref[...] = nm
    nv_ref[...] = nv


def _adamw(parts, w, m, v, name):
    n, nb, rows, cols = parts.shape
    tm = next(t for t in (256, 128, 64, 32, 16) if rows % t == 0)

    def body(*refs):
        _adamw_body(*refs)

    row = pl.BlockSpec((None, tm, cols), lambda b, i: (b, i, 0))
    return _pallas(body, out_shape=(jax.ShapeDtypeStruct((nb, rows, cols), F32),) * 4, grid=(nb, rows // tm),
                   in_specs=[pl.BlockSpec((n, None, tm, cols), lambda b, i: (0, b, i, 0)), row, row, row],
                   out_specs=(row,) * 4, name=name)(parts, w, m, v)


def _adamw_whole(parts, w, m, v, name):
    def body(*refs):
        _adamw_body(*refs)

    if w.ndim == 2:
        grid = (1,)
        slab = pl.BlockSpec(w.shape, lambda b: (0, 0))
        part = pl.BlockSpec(parts.shape, lambda b: (0, 0, 0))
    else:
        grid, rest = (w.shape[0],), w.shape[1:]
        zeros = (0,) * len(rest)
        slab = pl.BlockSpec((None,) + rest, lambda b: (b,) + zeros)
        part = pl.BlockSpec((N_DEV, None) + rest, lambda b: (0, b) + zeros)
    return _pallas(body, out_shape=(jax.ShapeDtypeStruct(w.shape, F32),) * 4, grid=grid,
                   in_specs=[part, slab, slab, slab], out_specs=(slab,) * 4, name=name)(parts, w, m, v)


_WEIGHTS = ("norm_w", "w_in", "fox_b_f", "s5_a_re", "s5_a_im", "s5_b_re", "s5_b_im", "s5_c_re", "s5_c_im", "s5_d",
            "s5_log_dt", "s5_w_glu", "ret_gn_w", "w_out", "final_norm_w")
_SMALL = tuple(n for n in _WEIGHTS if n not in _SHARDED)


def kernel(x, norm_w, w_in, fox_b_f, s5_a_re, s5_a_im, s5_b_re, s5_b_im, s5_c_re, s5_c_im, s5_d, s5_log_dt, s5_w_glu, ret_gn_w, w_out, final_norm_w, loss_target, m_norm_w, m_w_in, m_fox_b_f, m_s5_a_re, m_s5_a_im, m_s5_b_re, m_s5_b_im, m_s5_c_re, m_s5_c_im, m_s5_d, m_s5_log_dt, m_s5_w_glu, m_ret_gn_w, m_w_out, m_final_norm_w, v_norm_w, v_w_in, v_fox_b_f, v_s5_a_re, v_s5_a_im, v_s5_b_re, v_s5_b_im, v_s5_c_re, v_s5_c_im, v_s5_d, v_s5_log_dt, v_s5_w_glu, v_ret_gn_w, v_w_out, v_final_norm_w):
    w = dict(norm_w=norm_w, w_in=w_in, fox_b_f=fox_b_f, s5_a_re=s5_a_re, s5_a_im=s5_a_im, s5_b_re=s5_b_re, s5_b_im=s5_b_im,
             s5_c_re=s5_c_re, s5_c_im=s5_c_im, s5_d=s5_d, s5_log_dt=s5_log_dt, s5_w_glu=s5_w_glu, ret_gn_w=ret_gn_w,
             w_out=w_out, final_norm_w=final_norm_w)
    m = dict(norm_w=m_norm_w, w_in=m_w_in, fox_b_f=m_fox_b_f, s5_a_re=m_s5_a_re, s5_a_im=m_s5_a_im, s5_b_re=m_s5_b_re,
             s5_b_im=m_s5_b_im, s5_c_re=m_s5_c_re, s5_c_im=m_s5_c_im, s5_d=m_s5_d, s5_log_dt=m_s5_log_dt,
             s5_w_glu=m_s5_w_glu, ret_gn_w=m_ret_gn_w, w_out=m_w_out, final_norm_w=m_final_norm_w)
    v = dict(norm_w=v_norm_w, w_in=v_w_in, fox_b_f=v_fox_b_f, s5_a_re=v_s5_a_re, s5_a_im=v_s5_a_im, s5_b_re=v_s5_b_re,
             s5_b_im=v_s5_b_im, s5_c_re=v_s5_c_re, s5_c_im=v_s5_c_im, s5_d=v_s5_d, s5_log_dt=v_s5_log_dt,
             s5_w_glu=v_s5_w_glu, ret_gn_w=v_ret_gn_w, w_out=v_w_out, final_norm_w=v_final_norm_w)

    small = {n: w[n] for n in _SMALL}
    loss, dx, _, _, partials, r_small = _step_grads(x[0], loss_target[0], small, shards=[w[n].astype(_MXU) for n in _SHARDED])

    res = {}
    for t, n in enumerate(_SHARDED):
        res[n] = _adamw(jnp.stack([partials[l][t] for l in range(DEPTH)], axis=1), w[n], m[n], v[n], "adamw_" + n)
    for n in _SMALL:
        shape = w[n].shape
        view = (1,) + shape if len(shape) == 1 else shape[:2] + (-1,) if n in _DENSE else shape
        outs = _adamw_whole(r_small[n], *[d[n].reshape(view) for d in (w, m, v)], "adamw_" + n)
        res[n] = [o.reshape(shape) for o in outs]

    loss = lax.psum(loss[0, 0], ("x", "y", "c"))
    return (loss, dx[None], *[res[n][0] for n in _WEIGHTS], *[res[n][1] for n in _WEIGHTS],
            *[res[n][2] for n in _WEIGHTS], *[res[n][3] for n in _WEIGHTS])
```
